```python
import jax, jax.numpy as jnp
from jax import lax
import numpy as np

D_MODEL = 2048
BATCH = 8
SEQ = 4096
DEPTH = 1

A_HEADS = 8
A_KEY_DIM = 128
A_VAL_DIM = 128
A_KEY_WIDTH = A_HEADS * A_KEY_DIM
A_WIDTH = A_HEADS * A_VAL_DIM
A_CHUNK = 32
B_GROUPS = 8
B_GROUP_DIM = 128
B_WIDTH = B_GROUPS * B_GROUP_DIM
B_CHUNK = 128
D_FF = 4 * D_MODEL
N_MOD = 6
EPS = 1e-6

IN_SIZES = (A_KEY_WIDTH, A_KEY_WIDTH, A_KEY_WIDTH, A_WIDTH, A_WIDTH, 2 * B_WIDTH, D_MODEL, D_MODEL)
IN_WIDTH = sum(IN_SIZES)
IN_SPLIT_POINTS = tuple(int(s) for s in np.cumsum(IN_SIZES)[:-1])

kernel_name = 'hybrid_hgrn2_sgu_block'


def rms_norm(x, g):
    xf = x.astype(jnp.float32)
    y = xf * lax.rsqrt(jnp.mean(xf * xf, axis=-1, keepdims=True) + EPS)
    return (y * g.astype(jnp.float32)).astype(x.dtype)


def layer_norm(x, g):
    xf = x.astype(jnp.float32)
    mu = jnp.mean(xf, axis=-1, keepdims=True)
    d = xf - mu
    y = d * lax.rsqrt(jnp.mean(d * d, axis=-1, keepdims=True) + EPS)
    return (y * g.astype(jnp.float32)).astype(x.dtype)


def to_heads(t, d):
    b, l, _ = t.shape
    return t.reshape(b, l, A_HEADS, d).transpose(0, 2, 1, 3)


def gated_state_scan(q, k, v, log_f):
    bn, h, l, dk = q.shape
    dv = v.shape[-1]
    n = l // A_CHUNK
    rs = lambda t: t.reshape(bn, h, n, A_CHUNK, t.shape[-1])
    q, k, v, log_f = rs(q), rs(k), rs(v), rs(log_f)
    b = jnp.cumsum(log_f, axis=3)
    b_last = b[:, :, :, -1:, :]
    q_dec = q * jnp.exp(b)
    k_dec = k * jnp.exp(-b)
    k_end = k * jnp.exp(b_last - b)
    mask = jnp.tril(jnp.ones((A_CHUNK, A_CHUNK), dtype=bool))
    att = jnp.einsum('bhnck,bhnsk->bhncs', q_dec, k_dec)
    att = jnp.where(mask, att, 0.0)
    o_intra = jnp.einsum('bhncs,bhnsv->bhncv', att, v)
    decay = jnp.exp(b_last[:, :, :, 0, :])

    def step(state, xs):
        qd, ke, vc, dc = xs
        o = jnp.einsum('bhck,bhkv->bhcv', qd, state)
        state = dc[..., None] * state + jnp.einsum('bhck,bhcv->bhkv', ke, vc)
        return state, o

    xs = (jnp.moveaxis(q_dec, 2, 0), jnp.moveaxis(k_end, 2, 0),
          jnp.moveaxis(v, 2, 0), jnp.moveaxis(decay, 2, 0))
    s0 = jnp.zeros((bn, h, dk, dv), jnp.float32)
    _, o_inter = lax.scan(step, s0, xs)
    o = o_intra + jnp.moveaxis(o_inter, 0, 2)
    return o.reshape(bn, h, l, dv)


def hgrn2_forget(logit, lb):
    lf = logit.astype(jnp.float32)
    log_f = jnp.log(lb + (1.0 - lb) * jax.nn.sigmoid(lf))
    k = (1.0 - lb) * jax.nn.sigmoid(-lf)
    return k, log_f


def hgrn2_bidir(q, f_fw, f_bw, i_v, o_gate, lb_fw, lb_bw, g_norm):
    bn, l, _ = q.shape
    k_fw, lf_fw = hgrn2_forget(f_fw, lb_fw)
    k_bw, lf_bw = hgrn2_forget(f_bw, lb_bw)
    qh = to_heads(q.astype(jnp.float32) * (A_KEY_DIM ** -0.5), A_KEY_DIM)
    vh = to_heads(i_v.astype(jnp.float32), A_VAL_DIM)
    o_fw = gated_state_scan(qh, to_heads(k_fw, A_KEY_DIM), vh, to_heads(lf_fw, A_KEY_DIM))
    fl = lambda t: jnp.flip(t, axis=2)
    o_bw = fl(gated_state_scan(fl(qh), fl(to_heads(k_bw, A_KEY_DIM)), fl(vh),
                               fl(to_heads(lf_bw, A_KEY_DIM))))
    o = rms_norm(o_fw + o_bw, g_norm)
    o = o.transpose(0, 2, 1, 3).reshape(bn, l, A_WIDTH).astype(o_gate.dtype)
    return o * jax.nn.silu(o_gate)


def chunked_sgu(z, g_v, w_s, b_s):
    bn, l, _ = z.shape
    z = jax.nn.gelu(z, approximate=False)
    u, v = jnp.split(z, 2, axis=-1)
    v = layer_norm(v, g_v)
    v = v.reshape(bn, l // B_CHUNK, B_CHUNK, B_GROUPS, B_GROUP_DIM)
    vm = jnp.einsum('gts,bnsgc->bntgc', w_s, v) + b_s.T[None, None, :, :, None]
    return u * vm.reshape(bn, l, B_WIDTH)


def _fwd_setup_inputs(seed: int = 0) -> dict:
    key = jax.random.key(seed)
    ks = jax.random.split(key, 20)
    nrm = lambda k, shape, s: jax.random.normal(k, shape, jnp.float32) * s
    gain = lambda k, shape: 1.0 + nrm(k, shape, 0.02)
    return {
        'x': nrm(ks[0], (BATCH, SEQ, D_MODEL), 1.0),
        'c': nrm(ks[1], (BATCH, D_MODEL), 1.0),
        'w_ada': nrm(ks[2], (DEPTH, D_MODEL, N_MOD * D_MODEL), D_MODEL ** -0.5),
        'b_ada': nrm(ks[3], (DEPTH, N_MOD * D_MODEL), 0.02),
        'g_pre_mix': gain(ks[4], (DEPTH, D_MODEL)),
        'g_post_mix': gain(ks[5], (DEPTH, D_MODEL)),
        'g_pre_ffn': gain(ks[6], (DEPTH, D_MODEL)),
        'g_post_ffn': gain(ks[7], (DEPTH, D_MODEL)),
        'w_in': nrm(ks[8], (DEPTH, D_MODEL, IN_WIDTH), D_MODEL ** -0.5),
        'lb_logits': nrm(ks[9], (2, DEPTH + 1, A_KEY_WIDTH), 0.1),
        'g_hgrn_norm': gain(ks[10], (DEPTH, A_VAL_DIM)),
        'w_a_out': nrm(ks[11], (DEPTH, A_WIDTH, D_MODEL), A_WIDTH ** -0.5),
        'g_sgu_norm': gain(ks[12], (DEPTH, B_WIDTH)),
        'w_spatial': nrm(ks[13], (DEPTH, B_GROUPS, B_CHUNK, B_CHUNK), B_CHUNK ** -0.5),
        'b_spatial': nrm(ks[14], (DEPTH, B_GROUPS, B_CHUNK), 0.02),
        'w_b_out': nrm(ks[15], (DEPTH, B_WIDTH, D_MODEL), B_WIDTH ** -0.5),
        'w_o': nrm(ks[16], (DEPTH, D_MODEL, D_MODEL), D_MODEL ** -0.5),
        'w_ff1': nrm(ks[17], (DEPTH, D_MODEL, D_FF), D_MODEL ** -0.5),
        'w_ff2': nrm(ks[18], (DEPTH, D_FF, D_MODEL), D_FF ** -0.5),
    }


def _fwd_reference(x, c, w_ada, b_ada, g_pre_mix, g_post_mix, g_pre_ffn, g_post_ffn, w_in,
              lb_logits, g_hgrn_norm, w_a_out, g_sgu_norm, w_spatial, b_spatial,
              w_b_out, w_o, w_ff1, w_ff2):
    lb_all = jnp.cumsum(jax.nn.softmax(lb_logits.astype(jnp.float32), axis=1), axis=1)
    h = x
    for l in range(DEPTH):
        mod = jax.nn.silu(c) @ w_ada[l] + b_ada[l]
        sh1, sc1, gt1, sh2, sc2, gt2 = [m[:, None, :] for m in jnp.split(mod, N_MOD, axis=-1)]
        a = rms_norm(h, g_pre_mix[l]) * (1 + sc1) + sh1
        proj = a @ w_in[l]
        q, f_fw, f_bw, i_v, o_gate, z, gate_a, gate_b = jnp.split(proj, IN_SPLIT_POINTS, axis=-1)
        y_a = hgrn2_bidir(q, f_fw, f_bw, i_v, o_gate, lb_all[0, l], lb_all[1, l],
                          g_hgrn_norm[l]) @ w_a_out[l]
        y_b = chunked_sgu(z, g_sgu_norm[l], w_spatial[l], b_spatial[l]) @ w_b_out[l]
        merged = jax.nn.sigmoid(gate_a) * y_a + jax.nn.sigmoid(gate_b) * y_b
        h = h + gt1 * rms_norm(merged @ w_o[l], g_post_mix[l])
        a = rms_norm(h, g_pre_ffn[l]) * (1 + sc2) + sh2
        ff = jnp.square(jax.nn.relu(a @ w_ff1[l])) @ w_ff2[l]
        h = h + gt2 * rms_norm(ff, g_post_ffn[l])
    return h


import jax as _jax
import jax.numpy as _jnp

TWIN_FORMAT = 'train_step'
FWD_PARAMS = ['x', 'c', 'w_ada', 'b_ada', 'g_pre_mix', 'g_post_mix', 'g_pre_ffn', 'g_post_ffn', 'w_in', 'lb_logits', 'g_hgrn_norm', 'w_a_out', 'g_sgu_norm', 'w_spatial', 'b_spatial', 'w_b_out', 'w_o', 'w_ff1', 'w_ff2']
TWIN_WEIGHTS = ['w_ada', 'b_ada', 'g_pre_mix', 'g_post_mix', 'g_pre_ffn', 'g_post_ffn', 'w_in', 'lb_logits', 'g_hgrn_norm', 'w_a_out', 'g_sgu_norm', 'w_spatial', 'b_spatial', 'w_b_out', 'w_o', 'w_ff1', 'w_ff2']
TWIN_DIFF_INPUT = 'x'
TWIN_INPUTS = ['x', 'c', 'w_ada', 'b_ada', 'g_pre_mix', 'g_post_mix', 'g_pre_ffn', 'g_post_ffn', 'w_in', 'lb_logits', 'g_hgrn_norm', 'w_a_out', 'g_sgu_norm', 'w_spatial', 'b_spatial', 'w_b_out', 'w_o', 'w_ff1', 'w_ff2', 'loss_target', 'm_w_ada', 'm_b_ada', 'm_g_pre_mix', 'm_g_post_mix', 'm_g_pre_ffn', 'm_g_post_ffn', 'm_w_in', 'm_lb_logits', 'm_g_hgrn_norm', 'm_w_a_out', 'm_g_sgu_norm', 'm_w_spatial', 'm_b_spatial', 'm_w_b_out', 'm_w_o', 'm_w_ff1', 'm_w_ff2', 'v_w_ada', 'v_b_ada', 'v_g_pre_mix', 'v_g_post_mix', 'v_g_pre_ffn', 'v_g_post_ffn', 'v_w_in', 'v_lb_logits', 'v_g_hgrn_norm', 'v_w_a_out', 'v_g_sgu_norm', 'v_w_spatial', 'v_b_spatial', 'v_w_b_out', 'v_w_o', 'v_w_ff1', 'v_w_ff2']
TWIN_OUTPUTS = ['loss', 'grad_x', 'grad_w_ada', 'grad_b_ada', 'grad_g_pre_mix', 'grad_g_post_mix', 'grad_g_pre_ffn', 'grad_g_post_ffn', 'grad_w_in', 'grad_lb_logits', 'grad_g_hgrn_norm', 'grad_w_a_out', 'grad_g_sgu_norm', 'grad_w_spatial', 'grad_b_spatial', 'grad_w_b_out', 'grad_w_o', 'grad_w_ff1', 'grad_w_ff2', 'delta_w_ada', 'delta_b_ada', 'delta_g_pre_mix', 'delta_g_post_mix', 'delta_g_pre_ffn', 'delta_g_post_ffn', 'delta_w_in', 'delta_lb_logits', 'delta_g_hgrn_norm', 'delta_w_a_out', 'delta_g_sgu_norm', 'delta_w_spatial', 'delta_b_spatial', 'delta_w_b_out', 'delta_w_o', 'delta_w_ff1', 'delta_w_ff2', 'new_m_w_ada', 'new_m_b_ada', 'new_m_g_pre_mix', 'new_m_g_post_mix', 'new_m_g_pre_ffn', 'new_m_g_post_ffn', 'new_m_w_in', 'new_m_lb_logits', 'new_m_g_hgrn_norm', 'new_m_w_a_out', 'new_m_g_sgu_norm', 'new_m_w_spatial', 'new_m_b_spatial', 'new_m_w_b_out', 'new_m_w_o', 'new_m_w_ff1', 'new_m_w_ff2', 'new_v_w_ada', 'new_v_b_ada', 'new_v_g_pre_mix', 'new_v_g_post_mix', 'new_v_g_pre_ffn', 'new_v_g_post_ffn', 'new_v_w_in', 'new_v_lb_logits', 'new_v_g_hgrn_norm', 'new_v_w_a_out', 'new_v_g_sgu_norm', 'new_v_w_spatial', 'new_v_b_spatial', 'new_v_w_b_out', 'new_v_w_o', 'new_v_w_ff1', 'new_v_w_ff2']
TWIN_LEAF_KINDS = {'loss': 'loss', 'grad_x': 'grad_x', 'grad_w_ada': 'grad_w', 'grad_b_ada': 'grad_w', 'grad_g_pre_mix': 'grad_w', 'grad_g_post_mix': 'grad_w', 'grad_g_pre_ffn': 'grad_w', 'grad_g_post_ffn': 'grad_w', 'grad_w_in': 'grad_w', 'grad_lb_logits': 'grad_w', 'grad_g_hgrn_norm': 'grad_w', 'grad_w_a_out': 'grad_w', 'grad_g_sgu_norm': 'grad_w', 'grad_w_spatial': 'grad_w', 'grad_b_spatial': 'grad_w', 'grad_w_b_out': 'grad_w', 'grad_w_o': 'grad_w', 'grad_w_ff1': 'grad_w', 'grad_w_ff2': 'grad_w', 'delta_w_ada': 'delta_w', 'delta_b_ada': 'delta_w', 'delta_g_pre_mix': 'delta_w', 'delta_g_post_mix': 'delta_w', 'delta_g_pre_ffn': 'delta_w', 'delta_g_post_ffn': 'delta_w', 'delta_w_in': 'delta_w', 'delta_lb_logits': 'delta_w', 'delta_g_hgrn_norm': 'delta_w', 'delta_w_a_out': 'delta_w', 'delta_g_sgu_norm': 'delta_w', 'delta_w_spatial': 'delta_w', 'delta_b_spatial': 'delta_w', 'delta_w_b_out': 'delta_w', 'delta_w_o': 'delta_w', 'delta_w_ff1': 'delta_w', 'delta_w_ff2': 'delta_w', 'new_m_w_ada': 'new_m', 'new_m_b_ada': 'new_m', 'new_m_g_pre_mix': 'new_m', 'new_m_g_post_mix': 'new_m', 'new_m_g_pre_ffn': 'new_m', 'new_m_g_post_ffn': 'new_m', 'new_m_w_in': 'new_m', 'new_m_lb_logits': 'new_m', 'new_m_g_hgrn_norm': 'new_m', 'new_m_w_a_out': 'new_m', 'new_m_g_sgu_norm': 'new_m', 'new_m_w_spatial': 'new_m', 'new_m_b_spatial': 'new_m', 'new_m_w_b_out': 'new_m', 'new_m_w_o': 'new_m', 'new_m_w_ff1': 'new_m', 'new_m_w_ff2': 'new_m', 'new_v_w_ada': 'new_v', 'new_v_b_ada': 'new_v', 'new_v_g_pre_mix': 'new_v', 'new_v_g_post_mix': 'new_v', 'new_v_g_pre_ffn': 'new_v', 'new_v_g_post_ffn': 'new_v', 'new_v_w_in': 'new_v', 'new_v_lb_logits': 'new_v', 'new_v_g_hgrn_norm': 'new_v', 'new_v_w_a_out': 'new_v', 'new_v_g_sgu_norm': 'new_v', 'new_v_w_spatial': 'new_v', 'new_v_b_spatial': 'new_v', 'new_v_w_b_out': 'new_v', 'new_v_w_o': 'new_v', 'new_v_w_ff1': 'new_v', 'new_v_w_ff2': 'new_v'}


def _forward(args):
    return _fwd_reference(*[args[k] for k in FWD_PARAMS])


def _output_shape():
    def fwd():
        inp = _fwd_setup_inputs(0)
        return _fwd_reference(*[inp[k] for k in FWD_PARAMS])
    out = _jax.eval_shape(fwd)
    return out.shape, out.dtype

N_MICROBATCH = 1
ADAM_LR = 0.001
ADAM_B1 = 0.9
ADAM_B2 = 0.999
ADAM_EPS = 1e-08
ADAM_WD = 0.01
ADAM_STEP = 10
PER_EXAMPLE_BATCH_AXIS = {'x': 0, 'c': 0, 'loss_target': 0}
SHARED_INPUTS = []
_WEIGHT_DTYPES = {'w_ada': _jnp.float32, 'b_ada': _jnp.float32, 'g_pre_mix': _jnp.float32, 'g_post_mix': _jnp.float32, 'g_pre_ffn': _jnp.float32, 'g_post_ffn': _jnp.float32, 'w_in': _jnp.float32, 'lb_logits': _jnp.float32, 'g_hgrn_norm': _jnp.float32, 'w_a_out': _jnp.float32, 'g_sgu_norm': _jnp.float32, 'w_spatial': _jnp.float32, 'b_spatial': _jnp.float32, 'w_b_out': _jnp.float32, 'w_o': _jnp.float32, 'w_ff1': _jnp.float32, 'w_ff2': _jnp.float32}
MOMENT_SCALE = {'w_ada': 1.272932e+00, 'b_ada': 2.697522e+00, 'g_pre_mix': 1.884415e-01, 'g_post_mix': 6.528989e+00, 'g_pre_ffn': 1.277342e-01, 'g_post_ffn': 6.756584e+00, 'w_in': 1.162541e-01, 'lb_logits': 3.209895e-02, 'g_hgrn_norm': 5.227088e-01, 'w_a_out': 1.469808e-01, 'g_sgu_norm': 1.393714e-01, 'w_spatial': 1.109137e-01, 'b_spatial': 1.192897e-01, 'w_b_out': 9.184867e-02, 'w_o': 1.786865e-01, 'w_ff1': 2.513304e-01, 'w_ff2': 7.867182e-01}


def _to_microbatches(a, axis):
    t = _jnp.moveaxis(a, axis, 0)
    t = t.reshape((N_MICROBATCH, t.shape[0] // N_MICROBATCH) + t.shape[1:])
    return _jnp.moveaxis(t, 1, axis + 1)


def setup_inputs(seed: int = 0) -> dict:
    inp = _fwd_setup_inputs(seed)
    key = _jax.random.fold_in(_jax.random.key(seed), 7919)
    shape, _ = _output_shape()
    out = dict(inp)
    out["loss_target"] = _jax.random.normal(_jax.random.fold_in(key, 0), shape, _jnp.float32)
    for i, name in enumerate(TWIN_WEIGHTS):
        w = inp[name].astype(_jnp.float32)
        if MOMENT_SCALE is None:
            s = _jnp.sqrt(_jnp.mean(_jnp.square(w)) + 1e-30)
        else:
            s = MOMENT_SCALE[name]
        km, kv = _jax.random.split(_jax.random.fold_in(key, i + 1))
        out[name] = w
        out["m_" + name] = s * _jax.random.normal(km, w.shape, _jnp.float32)
        out["v_" + name] = (s * s) * _jax.random.uniform(kv, w.shape, _jnp.float32, 0.5, 1.5)
    if N_MICROBATCH > 1:
        for name, axis in PER_EXAMPLE_BATCH_AXIS.items():
            out[name] = _to_microbatches(out[name], axis)
    return {'x': out['x'], 'c': out['c'], 'w_ada': out['w_ada'], 'b_ada': out['b_ada'], 'g_pre_mix': out['g_pre_mix'], 'g_post_mix': out['g_post_mix'], 'g_pre_ffn': out['g_pre_ffn'], 'g_post_ffn': out['g_post_ffn'], 'w_in': out['w_in'], 'lb_logits': out['lb_logits'], 'g_hgrn_norm': out['g_hgrn_norm'], 'w_a_out': out['w_a_out'], 'g_sgu_norm': out['g_sgu_norm'], 'w_spatial': out['w_spatial'], 'b_spatial': out['b_spatial'], 'w_b_out': out['w_b_out'], 'w_o': out['w_o'], 'w_ff1': out['w_ff1'], 'w_ff2': out['w_ff2'], 'loss_target': out['loss_target'], 'm_w_ada': out['m_w_ada'], 'm_b_ada': out['m_b_ada'], 'm_g_pre_mix': out['m_g_pre_mix'], 'm_g_post_mix': out['m_g_post_mix'], 'm_g_pre_ffn': out['m_g_pre_ffn'], 'm_g_post_ffn': out['m_g_post_ffn'], 'm_w_in': out['m_w_in'], 'm_lb_logits': out['m_lb_logits'], 'm_g_hgrn_norm': out['m_g_hgrn_norm'], 'm_w_a_out': out['m_w_a_out'], 'm_g_sgu_norm': out['m_g_sgu_norm'], 'm_w_spatial': out['m_w_spatial'], 'm_b_spatial': out['m_b_spatial'], 'm_w_b_out': out['m_w_b_out'], 'm_w_o': out['m_w_o'], 'm_w_ff1': out['m_w_ff1'], 'm_w_ff2': out['m_w_ff2'], 'v_w_ada': out['v_w_ada'], 'v_b_ada': out['v_b_ada'], 'v_g_pre_mix': out['v_g_pre_mix'], 'v_g_post_mix': out['v_g_post_mix'], 'v_g_pre_ffn': out['v_g_pre_ffn'], 'v_g_post_ffn': out['v_g_post_ffn'], 'v_w_in': out['v_w_in'], 'v_lb_logits': out['v_lb_logits'], 'v_g_hgrn_norm': out['v_g_hgrn_norm'], 'v_w_a_out': out['v_w_a_out'], 'v_g_sgu_norm': out['v_g_sgu_norm'], 'v_w_spatial': out['v_w_spatial'], 'v_b_spatial': out['v_b_spatial'], 'v_w_b_out': out['v_w_b_out'], 'v_w_o': out['v_w_o'], 'v_w_ff1': out['v_w_ff1'], 'v_w_ff2': out['v_w_ff2']}


def _loss(weights, diff, rest, loss_target):
    with _jax.named_scope("forward"):
        args = {**rest, TWIN_DIFF_INPUT: diff, **{k: w.astype(_WEIGHT_DTYPES[k]) for k, w in weights.items()}}
        y = _forward(args)
    with _jax.named_scope("loss_head"):
        err = _jnp.square(y.astype(_jnp.float32) - loss_target)
        return 0.5 * _jnp.sum(_jnp.mean(err, axis=-1)) if err.ndim else 0.5 * err


def _adamw(w, g, m, v):
    m = ADAM_B1 * m + (1.0 - ADAM_B1) * g
    v = ADAM_B2 * v + (1.0 - ADAM_B2) * _jnp.square(g)
    m_hat = m / (1.0 - ADAM_B1 ** ADAM_STEP)
    v_hat = v / (1.0 - ADAM_B2 ** ADAM_STEP)
    delta = -ADAM_LR * (m_hat / (_jnp.sqrt(v_hat) + ADAM_EPS) + ADAM_WD * w)
    return delta, m, v


def reference(x, c, w_ada, b_ada, g_pre_mix, g_post_mix, g_pre_ffn, g_post_ffn, w_in, lb_logits, g_hgrn_norm, w_a_out, g_sgu_norm, w_spatial, b_spatial, w_b_out, w_o, w_ff1, w_ff2, loss_target, m_w_ada, m_b_ada, m_g_pre_mix, m_g_post_mix, m_g_pre_ffn, m_g_post_ffn, m_w_in, m_lb_logits, m_g_hgrn_norm, m_w_a_out, m_g_sgu_norm, m_w_spatial, m_b_spatial, m_w_b_out, m_w_o, m_w_ff1, m_w_ff2, v_w_ada, v_b_ada, v_g_pre_mix, v_g_post_mix, v_g_pre_ffn, v_g_post_ffn, v_w_in, v_lb_logits, v_g_hgrn_norm, v_w_a_out, v_g_sgu_norm, v_w_spatial, v_b_spatial, v_w_b_out, v_w_o, v_w_ff1, v_w_ff2):
    given = dict(x=x, c=c, w_ada=w_ada, b_ada=b_ada, g_pre_mix=g_pre_mix, g_post_mix=g_post_mix, g_pre_ffn=g_pre_ffn, g_post_ffn=g_post_ffn, w_in=w_in, lb_logits=lb_logits, g_hgrn_norm=g_hgrn_norm, w_a_out=w_a_out, g_sgu_norm=g_sgu_norm, w_spatial=w_spatial, b_spatial=b_spatial, w_b_out=w_b_out, w_o=w_o, w_ff1=w_ff1, w_ff2=w_ff2, loss_target=loss_target, m_w_ada=m_w_ada, m_b_ada=m_b_ada, m_g_pre_mix=m_g_pre_mix, m_g_post_mix=m_g_post_mix, m_g_pre_ffn=m_g_pre_ffn, m_g_post_ffn=m_g_post_ffn, m_w_in=m_w_in, m_lb_logits=m_lb_logits, m_g_hgrn_norm=m_g_hgrn_norm, m_w_a_out=m_w_a_out, m_g_sgu_norm=m_g_sgu_norm, m_w_spatial=m_w_spatial, m_b_spatial=m_b_spatial, m_w_b_out=m_w_b_out, m_w_o=m_w_o, m_w_ff1=m_w_ff1, m_w_ff2=m_w_ff2, v_w_ada=v_w_ada, v_b_ada=v_b_ada, v_g_pre_mix=v_g_pre_mix, v_g_post_mix=v_g_post_mix, v_g_pre_ffn=v_g_pre_ffn, v_g_post_ffn=v_g_post_ffn, v_w_in=v_w_in, v_lb_logits=v_lb_logits, v_g_hgrn_norm=v_g_hgrn_norm, v_w_a_out=v_w_a_out, v_g_sgu_norm=v_g_sgu_norm, v_w_spatial=v_w_spatial, v_b_spatial=v_b_spatial, v_w_b_out=v_w_b_out, v_w_o=v_w_o, v_w_ff1=v_w_ff1, v_w_ff2=v_w_ff2)
    weights = {n: given[n] for n in TWIN_WEIGHTS}
    shared = {n: given[n] for n in SHARED_INPUTS}
    per_example = {n: given[n] for n in ['x', 'c']}
    grad_fn = _jax.value_and_grad(_loss, argnums=(0, 1))

    def one_microbatch(ex, loss_target):
        ex = dict(ex)
        diff = ex.pop(TWIN_DIFF_INPUT)
        return grad_fn(weights, diff, {**shared, **ex}, loss_target)

    if N_MICROBATCH == 1:
        loss, (grad_w, grad_x) = one_microbatch(per_example, given["loss_target"])
    else:
        def body(carry, xs):
            loss_sum, grad_sum = carry
            l_k, (gw_k, gx_k) = one_microbatch(xs[0], xs[1])
            with _jax.named_scope("update"):
                return (loss_sum + l_k, _jax.tree.map(_jnp.add, grad_sum, gw_k)), gx_k

        init = (_jnp.zeros((), _jnp.float32), _jax.tree.map(_jnp.zeros_like, weights))
        (loss, grad_w), grad_x = _jax.lax.scan(body, init, (per_example, given["loss_target"]))
    with _jax.named_scope("update"):
        delta_w, new_m, new_v = {}, {}, {}
        for n in TWIN_WEIGHTS:
            delta_w[n], new_m[n], new_v[n] = _adamw(weights[n], grad_w[n], given["m_" + n], given["v_" + n])
    return (loss, grad_x, *[grad_w[n] for n in TWIN_WEIGHTS], *[delta_w[n] for n in TWIN_WEIGHTS],
            *[new_m[n] for n in TWIN_WEIGHTS], *[new_v[n] for n in TWIN_WEIGHTS])
```

```python
import functools
import math

import jax
import jax.numpy as jnp
from jax import lax
from jax.experimental import pallas as pl
from jax.experimental.pallas import tpu as pltpu

F32 = jnp.float32
BF16 = jnp.bfloat16
MESH = pl.DeviceIdType.MESH
HIGHEST = lax.Precision.HIGHEST

N_DEV = 8
HEAD = 128
A_CHUNK = 32
N_MOD = 6
EPS = 1e-6
LANE = 128
VMEM_LIMIT = 56 * 1024 * 1024

ADAM_LR = 0.001
ADAM_B1 = 0.9
ADAM_B2 = 0.999
ADAM_EPS = 1e-08
ADAM_WD = 0.01
ADAM_STEP = 10

_NN = (((1,), (0,)), ((), ()))
_NT = (((1,), (1,)), ((), ()))
_TN = (((0,), (0,)), ((), ()))


def _dot(a, b, dims=_NN, precision=None):
    return lax.dot_general(a, b, dims, preferred_element_type=F32, precision=precision)


def _bdot(a, b, dims=_NN):
    return _dot(a.astype(BF16), b.astype(BF16), dims)


def _params(n_grid):
    return pltpu.CompilerParams(dimension_semantics=("arbitrary",) * n_grid, vmem_limit_bytes=VMEM_LIMIT)


def _dev_index():
    return lax.axis_index("x") * 4 + lax.axis_index("y") * 2 + lax.axis_index("c")


def _dev_coords(i):
    return (i // 4, (i // 2) % 2, i % 2)


def _sigmoid(x):
    return 1.0 / (1.0 + jnp.exp(-x))


def _erf(x):
    ax = jnp.abs(x)
    t = 1.0 / (1.0 + 0.3275911 * ax)
    poly = ((((1.061405429 * t - 1.453152027) * t + 1.421413741) * t - 0.284496736) * t + 0.254829592) * t
    y = 1.0 - poly * jnp.exp(-ax * ax)
    return jnp.where(x < 0, -y, y)


def _gelu_and_grad(x):
    cdf = 0.5 * (1.0 + _erf(x * (2.0 ** -0.5)))
    pdf = jnp.exp(-0.5 * x * x) * (1.0 / math.sqrt(2.0 * math.pi))
    return x * cdf, cdf + x * pdf


def _rms(x):
    return lax.rsqrt(jnp.mean(x * x, axis=-1, keepdims=True) + EPS)


def _colsum(x):
    return jnp.sum(x, axis=0, keepdims=True)


def _tile(n, want):
    if n <= want:
        return n
    t = (want // LANE) * LANE
    while n % t:
        t -= LANE
    assert t > 0, (n, want)
    return t


def _all_gather_small(name, payload):
    rows = payload.shape[0]

    def body(p_ref, out_ref, send_sems, recv_sems, local_sem):
        me = _dev_index()
        mine = pltpu.make_async_copy(p_ref, out_ref.at[me], local_sem)
        mine.start()
        sends = []
        for r in range(1, N_DEV):
            peer = (me + r) % N_DEV
            cp = pltpu.make_async_remote_copy(
                src_ref=p_ref, dst_ref=out_ref.at[me], send_sem=send_sems.at[r - 1], recv_sem=recv_sems.at[r - 1],
                device_id=_dev_coords(peer), device_id_type=MESH)
            cp.start()
            sends.append(cp)
        for r in range(1, N_DEV):
            src = (me + N_DEV - r) % N_DEV
            pltpu.make_async_remote_copy(
                src_ref=p_ref, dst_ref=out_ref.at[src], send_sem=send_sems.at[r - 1], recv_sem=recv_sems.at[r - 1],
                device_id=_dev_coords(src), device_id_type=MESH).wait_recv()
        for cp in sends:
            cp.wait_send()
        mine.wait()

    return pl.pallas_call(
        body, name=name,
        out_shape=jax.ShapeDtypeStruct((N_DEV, rows, LANE), F32),
        in_specs=[pl.BlockSpec(memory_space=pltpu.VMEM)],
        out_specs=pl.BlockSpec(memory_space=pltpu.VMEM),
        scratch_shapes=[pltpu.SemaphoreType.DMA((N_DEV - 1,)), pltpu.SemaphoreType.DMA((N_DEV - 1,)),
                        pltpu.SemaphoreType.DMA],
        compiler_params=pltpu.CompilerParams(vmem_limit_bytes=VMEM_LIMIT),
    )(payload)


def _region(ref, dev, axis, n):
    start = pl.multiple_of(dev * n, LANE if axis == 1 else 16)
    return ref.at[:, pl.ds(start, n)] if axis == 1 else ref.at[pl.ds(start, n), :]


def _all_gather_weights(shards, axes):
    n_w = len(shards)
    fulls = []
    for s, ax in zip(shards, axes):
        shp = (s.shape[0], s.shape[1] * N_DEV) if ax == 1 else (s.shape[0] * N_DEV, s.shape[1])
        fulls.append(jax.ShapeDtypeStruct(shp, BF16))

    def body(*refs):
        s_refs, f_refs = refs[:n_w], refs[n_w:2 * n_w]
        send_sems, recv_sems, local_sems = refs[2 * n_w:]
        me = _dev_index()
        started = []
        for w in range(n_w):
            n = shards[w].shape[axes[w]]
            mine = pltpu.make_async_copy(s_refs[w], _region(f_refs[w], me, axes[w], n), local_sems.at[w])
            mine.start()
            started.append(mine)
        sends = []
        for w in range(n_w):
            n = shards[w].shape[axes[w]]
            for r in range(1, N_DEV):
                peer = (me + r) % N_DEV
                cp = pltpu.make_async_remote_copy(
                    src_ref=s_refs[w], dst_ref=_region(f_refs[w], me, axes[w], n),
                    send_sem=send_sems.at[w, r - 1], recv_sem=recv_sems.at[w, r - 1],
                    device_id=_dev_coords(peer), device_id_type=MESH)
                cp.start()
                sends.append(cp)
        for w in range(n_w):
            n = shards[w].shape[axes[w]]
            for r in range(1, N_DEV):
                src = (me + N_DEV - r) % N_DEV
                pltpu.make_async_remote_copy(
                    src_ref=s_refs[w], dst_ref=_region(f_refs[w], src, axes[w], n),
                    send_sem=send_sems.at[w, r - 1], recv_sem=recv_sems.at[w, r - 1],
                    device_id=_dev_coords(src), device_id_type=MESH).wait_recv()
        for cp in sends:
            cp.wait_send()
        for cp in started:
            cp.wait()

    any_spec = pl.BlockSpec(memory_space=pl.ANY)
    return pl.pallas_call(
        body, name="all_gather_weights",
        out_shape=fulls,
        in_specs=[any_spec] * n_w, out_specs=[any_spec] * n_w,
        scratch_shapes=[pltpu.SemaphoreType.DMA((n_w, N_DEV - 1)), pltpu.SemaphoreType.DMA((n_w, N_DEV - 1)),
                        pltpu.SemaphoreType.DMA((n_w,))],
    )(*shards)


def _scatter_grads(grads, axes):
    n_w = len(grads)
    lands = []
    for g, ax in zip(grads, axes):
        shp = (g.shape[0], g.shape[1] // N_DEV) if ax == 1 else (g.shape[0] // N_DEV, g.shape[1])
        lands.append(jax.ShapeDtypeStruct((N_DEV - 1,) + shp, BF16))

    def body(*refs):
        g_refs, l_refs = refs[:n_w], refs[n_w:2 * n_w]
        send_sems, recv_sems = refs[2 * n_w:]
        me = _dev_index()
        sends = []
        for w in range(n_w):
            n = lands[w].shape[1 + axes[w]]
            for r in range(1, N_DEV):
                owner = (me + r) % N_DEV
                cp = pltpu.make_async_remote_copy(
                    src_ref=_region(g_refs[w], owner, axes[w], n), dst_ref=l_refs[w].at[r - 1],
                    send_sem=send_sems.at[w, r - 1], recv_sem=recv_sems.at[w, r - 1],
                    device_id=_dev_coords(owner), device_id_type=MESH)
                cp.start()
                sends.append(cp)
        for w in range(n_w):
            n = lands[w].shape[1 + axes[w]]
            for r in range(1, N_DEV):
                src = (me + N_DEV - r) % N_DEV
                pltpu.make_async_remote_copy(
                    src_ref=_region(g_refs[w], me, axes[w], n), dst_ref=l_refs[w].at[r - 1],
                    send_sem=send_sems.at[w, r - 1], recv_sem=recv_sems.at[w, r - 1],
                    device_id=_dev_coords(src), device_id_type=MESH).wait_recv()
        for cp in sends:
            cp.wait_send()

    any_spec = pl.BlockSpec(memory_space=pl.ANY)
    return pl.pallas_call(
        body, name="scatter_grads",
        out_shape=lands,
        in_specs=[any_spec] * n_w, out_specs=[any_spec] * n_w,
        scratch_shapes=[pltpu.SemaphoreType.DMA((n_w, N_DEV - 1)), pltpu.SemaphoreType.DMA((n_w, N_DEV - 1))],
    )(*grads)


def _mm(name, a, b, dims, m, n, k, tm, tn, tk, extras, outs, epilogue):
    nk = k // tk
    ne, no = len(extras), len(outs)
    if dims == _TN:
        a_spec = pl.BlockSpec((tk, tm), lambda i, j, kk: (kk, i))
    else:
        a_spec = pl.BlockSpec((tm, tk), lambda i, j, kk: (i, kk))
    if dims == _NT:
        b_spec = pl.BlockSpec((tn, tk), lambda i, j, kk: (j, kk))
    else:
        b_spec = pl.BlockSpec((tk, tn), lambda i, j, kk: (kk, j))

    def lift(index_map):
        return lambda i, j, kk: index_map(i, j)

    def body(a_ref, b_ref, *rest):
        extra_refs, out_refs = rest[:ne], rest[ne:ne + no]
        i, j, kk = pl.program_id(0), pl.program_id(1), pl.program_id(2)
        part = _dot(a_ref[...], b_ref[...], dims)
        if nk == 1:
            epilogue(part, i, j, extra_refs, out_refs)
        else:
            acc_ref = rest[-1]

            @pl.when(kk == 0)
            def _():
                acc_ref[...] = part

            @pl.when(kk > 0)
            def _():
                acc_ref[...] += part

            @pl.when(kk == nk - 1)
            def _():
                epilogue(acc_ref[...], i, j, extra_refs, out_refs)

    return pl.pallas_call(
        body, name=name,
        grid=(m // tm, n // tn, nk),
        in_specs=[a_spec, b_spec] + [pl.BlockSpec(bs, lift(im)) for _, bs, im in extras],
        out_specs=[pl.BlockSpec(bs, lift(im)) for _, bs, im in outs],
        out_shape=[sd for sd, _, _ in outs],
        scratch_shapes=[pltpu.VMEM((tm, tn), F32)] if nk > 1 else [],
        compiler_params=_params(3),
    )(a, b, *[arr for arr, _, _ in extras])


def _full(shape):
    return shape, (lambda i, j: (0,) * len(shape))


def _grad_w(name, a, dc, tm=512, tn=1024, tk=512):
    t, m = a.shape
    n = dc.shape[1]
    tm, tn, tk = _tile(m, tm), _tile(n, tn), _tile(t, tk)

    def epilogue(acc, i, j, extra_refs, out_refs):
        out_refs[0][...] = acc
        out_refs[1][...] = acc.astype(BF16)

    blk = ((tm, tn), lambda i, j: (i, j))
    return _mm(name, a, dc, _TN, m, n, t, tm, tn, tk, [],
               [(jax.ShapeDtypeStruct((m, n), F32),) + blk, (jax.ShapeDtypeStruct((m, n), BF16),) + blk], epilogue)


def _cast_bf16(name, w):
    r, c = w.shape
    tr = _tile(r, 256)
    return pl.pallas_call(
        lambda w_ref, o_ref: o_ref.__setitem__(Ellipsis, w_ref[...].astype(BF16)), name=name,
        grid=(r // tr,), in_specs=[pl.BlockSpec((tr, c), lambda i: (i, 0))],
        out_specs=pl.BlockSpec((tr, c), lambda i: (i, 0)), out_shape=jax.ShapeDtypeStruct((r, c), BF16),
        compiler_params=_params(1),
    )(w)


def _prep_small(c_row, lb_logits):
    d = c_row.shape[1]
    rows = d // LANE

    def body(c_ref, l_ref, o_ref):
        cv = c_ref[...]
        o_ref[0:rows, :] = cv * _sigmoid(cv)
        lbs = [_sigmoid(l_ref[dr][0:1, :] - l_ref[dr][1:2, :]) for dr in range(2)]
        o_ref[rows:rows + 8, :] = jnp.concatenate(lbs + [jnp.zeros((6, LANE), F32)], axis=0)

    return pl.pallas_call(
        body, name="prep_small", out_shape=jax.ShapeDtypeStruct((rows + 8, LANE), F32),
    )(c_row.reshape(rows, LANE), lb_logits)


def _mod_shard(sc_all, w_ada_shard, b_shard):
    d, n = w_ada_shard.shape
    tn = _tile(n, 512)

    def body(s_ref, w_ref, b_ref, o_ref):
        o_ref[...] = _dot(s_ref[...], w_ref[...], precision=HIGHEST) + b_ref[...]

    return pl.pallas_call(
        body, name="mod_shard", grid=(n // tn,),
        in_specs=[pl.BlockSpec((N_DEV, d), lambda j: (0, 0)), pl.BlockSpec((d, tn), lambda j: (0, j)),
                  pl.BlockSpec((1, tn), lambda j: (0, j))],
        out_specs=pl.BlockSpec((N_DEV, tn), lambda j: (0, j)),
        out_shape=jax.ShapeDtypeStruct((N_DEV, n), F32), compiler_params=_params(1),
    )(sc_all, w_ada_shard, b_shard)


def _norm_mod(x, gain, shift, scale):
    t, d = x.shape
    tm = _tile(t, 512)

    def body(x_ref, g_ref, sh_ref, sc_ref, o_ref):
        xv = x_ref[...]
        o_ref[...] = ((xv * _rms(xv) * g_ref[...]) * (1.0 + sc_ref[...]) + sh_ref[...]).astype(BF16)

    vec = pl.BlockSpec((1, d), lambda i: (0, 0))
    return pl.pallas_call(
        body, name="norm_mod", grid=(t // tm,),
        in_specs=[pl.BlockSpec((tm, d), lambda i: (i, 0)), vec, vec, vec],
        out_specs=pl.BlockSpec((tm, d), lambda i: (i, 0)), out_shape=jax.ShapeDtypeStruct((t, d), BF16),
        compiler_params=_params(1),
    )(x, gain, shift, scale)


def _chunk_masks():
    row = lax.broadcasted_iota(jnp.int32, (HEAD, HEAD), 0)
    col = lax.broadcasted_iota(jnp.int32, (HEAD, HEAD), 1)
    same = (row // A_CHUNK) == (col // A_CHUNK)
    return same & (col <= row), same & (col >= row)


def _hgrn_block(direction, q, f, lb, tril, triu):
    sf = _sigmoid(f)
    big_f = lb + (1.0 - lb) * sf
    k = (1.0 - lb) * (1.0 - sf)
    lf = jnp.log(big_f)
    cf = _dot(tril.astype(F32), lf, precision=HIGHEST)
    cr = _dot(triu.astype(F32), lf, precision=HIGHEST)
    b, rest = (cf, cr - lf) if direction == 0 else (cr, cf - lf)
    return k, sf, big_f, jnp.exp(b), jnp.exp(-b), jnp.exp(rest)


def _hgrn_fwd(proj, lb, g_norm, width):
    t = proj.shape[0]
    heads = width // HEAD
    nb, nc = t // HEAD, t // A_CHUNK
    q_scale = HEAD ** -0.5

    def body(q_ref, ffw_ref, fbw_ref, v_ref, og_ref, lb_ref, g_ref, outa_ref, osum_ref,
             qd_s, ke_s, dc_s, o_s, st_s):
        tril, triu = _chunk_masks()
        f_refs = (ffw_ref, fbw_ref)

        def phase_a(i, carry):
            rows = pl.ds(pl.multiple_of(i * HEAD, HEAD), HEAD)
            qv, vv = q_ref[rows, :] * q_scale, v_ref[rows, :].astype(BF16)
            for d in range(2):
                k, _, _, eb, enb, erest = _hgrn_block(d, qv, f_refs[d][rows, :], lb_ref[d:d + 1, :], tril, triu)
                qd, kd = qv * eb, k * enb
                att = jnp.where(tril if d == 0 else triu, _bdot(qd, kd, _NT), 0.0)
                o_s[d, rows, :] = _bdot(att, vv)
                qd_s[d, rows, :] = qd.astype(BF16)
                ke_s[d, rows, :] = (k * erest).astype(BF16)
                dc_s[d, rows, :] = eb * erest
            return carry

        lax.fori_loop(0, nb, phase_a, 0)
        st_s[...] = jnp.zeros_like(st_s)

        def phase_b(n, carry):
            for d in range(2):
                c = n if d == 0 else nc - 1 - n
                rows = pl.ds(pl.multiple_of(c * A_CHUNK, A_CHUNK), A_CHUNK)
                st = st_s[d]
                o_s[d, rows, :] += _dot(qd_s[d, rows, :], st.astype(BF16), _NT)
                decay = dc_s[d, pl.ds(pl.multiple_of(c * A_CHUNK, A_CHUNK), 1), :]
                st_s[d] = st * decay + _dot(v_ref[rows, :].astype(BF16), ke_s[d, rows, :], _TN)
            return carry

        lax.fori_loop(0, nc, phase_b, 0)

        def phase_c(i, carry):
            rows = pl.ds(pl.multiple_of(i * HEAD, HEAD), HEAD)
            o = o_s[0, rows, :] + o_s[1, rows, :]
            osum_ref[rows, :] = o
            og = og_ref[rows, :]
            outa_ref[rows, :] = (o * _rms(o) * g_ref[...] * (og * _sigmoid(og))).astype(BF16)
            return carry

        lax.fori_loop(0, nb, phase_c, 0)

    def col(p):
        return pl.BlockSpec((t, HEAD), lambda h: (0, p * heads + h))

    return pl.pallas_call(
        body, name="hgrn_fwd", grid=(heads,),
        in_specs=[col(0), col(1), col(2), col(3), col(4),
                  pl.BlockSpec((2, HEAD), lambda h: (0, h)), pl.BlockSpec((1, HEAD), lambda h: (0, 0))],
        out_specs=[pl.BlockSpec((t, HEAD), lambda h: (0, h)), pl.BlockSpec((t, HEAD), lambda h: (0, h))],
        out_shape=[jax.ShapeDtypeStruct((t, width), BF16), jax.ShapeDtypeStruct((t, width), F32)],
        scratch_shapes=[pltpu.VMEM((2, t, HEAD), BF16), pltpu.VMEM((2, t, HEAD), BF16), pltpu.VMEM((2, t, HEAD), F32),
                        pltpu.VMEM((2, t, HEAD), F32), pltpu.VMEM((2, HEAD, HEAD), F32)],
        compiler_params=_params(1),
    )(proj, proj, proj, proj, proj, lb, g_norm)


def _sgu_core(u_pre, v_pre, g_v, ws_ref, bst):
    u, du = _gelu_and_grad(u_pre)
    v, dv = _gelu_and_grad(v_pre)
    mu = jnp.mean(v, axis=-1, keepdims=True)
    dlt = v - mu
    rstd = lax.rsqrt(jnp.mean(dlt * dlt, axis=-1, keepdims=True) + EPS)
    vhat = dlt * rstd
    vn = vhat * g_v
    groups = vn.shape[1] // HEAD
    cols = []
    for g in range(groups):
        vm_g = _bdot(ws_ref[g], vn[:, g * HEAD:(g + 1) * HEAD]) + bst[:, g:g + 1]
        cols.append(vm_g)
    return u, du, dv, vhat, rstd, vn, jnp.concatenate(cols, axis=1)


def _sgu_fwd(proj, g_v, w_s, bst, width, z_block):
    t = proj.shape[0]

    def body(u_ref, v_ref, g_ref, ws_ref, bst_ref, o_ref):
        u, _, _, _, _, _, vm = _sgu_core(u_ref[...], v_ref[...], g_ref[...], ws_ref, bst_ref[...])
        o_ref[...] = (u * vm).astype(BF16)

    groups = width // HEAD
    return pl.pallas_call(
        body, name="sgu_fwd", grid=(t // HEAD,),
        in_specs=[pl.BlockSpec((HEAD, width), lambda i: (i, z_block)), pl.BlockSpec((HEAD, width), lambda i: (i, z_block + 1)),
                  pl.BlockSpec((1, width), lambda i: (0, 0)), pl.BlockSpec((groups, HEAD, HEAD), lambda i: (0, 0, 0)),
                  pl.BlockSpec((HEAD, groups), lambda i: (0, 0))],
        out_specs=pl.BlockSpec((HEAD, width), lambda i: (i, 0)),
        out_shape=jax.ShapeDtypeStruct((t, width), BF16), compiler_params=_params(1),
    )(proj, proj, g_v, w_s, bst)


def _sgu_bwd(proj, dout_b, dproj, g_v, w_s, w_st, bst, width, z_block):
    t = proj.shape[0]
    groups = width // HEAD
    nblk = t // HEAD

    def body(u_ref, v_ref, do_ref, g_ref, ws_ref, wst_ref, bst_ref, dproj_hbm,
             dz_ref, dg_ref, dws_ref, dbst_ref, res_s):
        i, p = pl.program_id(0), pl.program_id(1)

        @pl.when((i == 0) & (p == 0))
        def _():
            dg_ref[...] = jnp.zeros_like(dg_ref)
            dws_ref[...] = jnp.zeros_like(dws_ref)
            dbst_ref[...] = jnp.zeros_like(dbst_ref)

        @pl.when(p == 0)
        def _():
            g_v = g_ref[...]
            u, du, dv, vhat, rstd, vn, vm = _sgu_core(u_ref[...], v_ref[...], g_v, ws_ref, bst_ref[...])
            dout = do_ref[...].astype(F32)
            res_s[0] = (dout * vm * du).astype(BF16)
            dvm = dout * u
            dvn_cols = []
            for g in range(groups):
                sl = slice(g * HEAD, (g + 1) * HEAD)
                dvm_g = dvm[:, sl]
                dbst_ref[:, g:g + 1] += jnp.sum(dvm_g, axis=1, keepdims=True)
                dws_ref[g] += _bdot(dvm_g, vn[:, sl], _NT)
                dvn_cols.append(_bdot(wst_ref[g], dvm_g))
            dvn = jnp.concatenate(dvn_cols, axis=1)
            dg_ref[...] += _colsum(dvn * vhat)
            dvh = dvn * g_v
            dvg = rstd * (dvh - jnp.mean(dvh, axis=-1, keepdims=True)
                          - vhat * jnp.mean(dvh * vhat, axis=-1, keepdims=True))
            res_s[1] = (dvg * dv).astype(BF16)

        dz_ref[...] = res_s[p]

    n_in = dproj.shape[1]
    return pl.pallas_call(
        body, name="sgu_bwd", grid=(nblk, 2),
        in_specs=[pl.BlockSpec((HEAD, width), lambda i, p: (i, z_block)),
                  pl.BlockSpec((HEAD, width), lambda i, p: (i, z_block + 1)),
                  pl.BlockSpec((HEAD, width), lambda i, p: (i, 0)),
                  pl.BlockSpec((1, width), lambda i, p: (0, 0)),
                  pl.BlockSpec((groups, HEAD, HEAD), lambda i, p: (0, 0, 0)),
                  pl.BlockSpec((groups, HEAD, HEAD), lambda i, p: (0, 0, 0)),
                  pl.BlockSpec((HEAD, groups), lambda i, p: (0, 0)),
                  pl.BlockSpec(memory_space=pl.ANY)],
        out_specs=[pl.BlockSpec((HEAD, width), lambda i, p: (i, z_block + p)),
                   pl.BlockSpec((1, width), lambda i, p: (0, 0)),
                   pl.BlockSpec((groups, HEAD, HEAD), lambda i, p: (0, 0, 0)),
                   pl.BlockSpec((HEAD, groups), lambda i, p: (0, 0))],
        out_shape=[jax.ShapeDtypeStruct((t, n_in), BF16), jax.ShapeDtypeStruct((1, width), F32),
                   jax.ShapeDtypeStruct((groups, HEAD, HEAD), F32), jax.ShapeDtypeStruct((HEAD, groups), F32)],
        scratch_shapes=[pltpu.VMEM((2, HEAD, width), BF16)],
        input_output_aliases={7: 0},
        compiler_params=_params(2),
    )(proj, proj, dout_b, g_v, w_s, w_st, bst, dproj)


def _hgrn_bwd(proj, osum, dout_a, dproj, lb, g_norm, width):
    t = proj.shape[0]
    heads = width // HEAD
    nb = t // HEAD
    cpb = HEAD // A_CHUNK
    q_scale = HEAD ** -0.5

    def body(q_ref, ffw_ref, fbw_ref, v_ref, og_ref, osum_ref, douta_ref, lb_ref, g_ref, dproj_hbm,
             out_ref, dgh_ref, dlb_ref, do_s, dq_s, dv_s, res_s, ck_s, st_s, gt_s):
        p = pl.program_id(1)
        f_refs = (ffw_ref, fbw_ref)

        @pl.when(p == 0)
        def _():
            tril, triu = _chunk_masks()
            g_row = g_ref[...]

            def pass_norm(i, dgh):
                rows = pl.ds(pl.multiple_of(i * HEAD, HEAD), HEAD)
                o = osum_ref[rows, :]
                r = _rms(o)
                oh = o * r
                og = og_ref[rows, :]
                sg = _sigmoid(og)
                dout = douta_ref[rows, :].astype(F32)
                don = dout * (og * sg)
                res_s[4, rows, :] = (dout * (oh * g_row) * (sg * (1.0 + og * (1.0 - sg)))).astype(BF16)
                doh = don * g_row
                do_s[rows, :] = r * (doh - oh * jnp.mean(doh * oh, axis=-1, keepdims=True))
                dq_s[rows, :] = jnp.zeros((HEAD, HEAD), F32)
                dv_s[rows, :] = jnp.zeros((HEAD, HEAD), F32)
                return dgh + _colsum(don * oh)

            dgh_ref[...] = lax.fori_loop(0, nb, pass_norm, jnp.zeros((1, HEAD), F32))

            def chunk_order(d):
                return range(cpb) if d == 0 else range(cpb - 1, -1, -1)

            def states_of_block(d, blk, st, k_e, e_b, v_b):
                befores = {}
                for j in chunk_order(d):
                    sl = slice(j * A_CHUNK, (j + 1) * A_CHUNK)
                    befores[j] = st
                    st = st * e_b[j * A_CHUNK:j * A_CHUNK + 1, :] + _bdot(v_b[sl, :], k_e[sl, :], _TN)
                return befores, st

            st_s[...] = jnp.zeros_like(st_s)

            def pass_states(it, carry):
                for d in range(2):
                    blk = it if d == 0 else nb - 1 - it
                    rows = pl.ds(pl.multiple_of(blk * HEAD, HEAD), HEAD)
                    k, _, _, eb, _, erest = _hgrn_block(d, None, f_refs[d][rows, :], lb_ref[d:d + 1, :], tril, triu)
                    ck_s[d, blk] = st_s[d]
                    _, st_s[d] = states_of_block(d, blk, st_s[d], k * erest, eb * erest, v_ref[rows, :])
                return carry

            lax.fori_loop(0, nb, pass_states, 0)
            gt_s[...] = jnp.zeros_like(gt_s)

            def pass_back(it, dlb):
                new = []
                for d in range(2):
                    blk = nb - 1 - it if d == 0 else it
                    rows = pl.ds(pl.multiple_of(blk * HEAD, HEAD), HEAD)
                    mask = tril if d == 0 else triu
                    lb_d = lb_ref[d:d + 1, :]
                    fv = f_refs[d][rows, :]
                    qh = q_ref[rows, :] * q_scale
                    vv = v_ref[rows, :]
                    k, sf, big_f, eb, enb, erest = _hgrn_block(d, qh, fv, lb_d, tril, triu)
                    qd, kd, ke, e_big = qh * eb, k * enb, k * erest, eb * erest
                    do = do_s[rows, :]
                    att = jnp.where(mask, _bdot(qd, kd, _NT), 0.0)
                    datt = jnp.where(mask, _bdot(do, vv, _NT), 0.0)
                    dv = _bdot(att, do, _TN)
                    dqd = _bdot(datt, kd)
                    dkd = _bdot(datt, qd, _TN)
                    befores, after = states_of_block(d, blk, ck_s[d, blk], ke, e_big, vv)
                    order = list(chunk_order(d))
                    afters = {j: (befores[order[n + 1]] if n + 1 < cpb else after) for n, j in enumerate(order)}
                    gt = gt_s[d]
                    dqd_i, dv_i, dke, carry_rows = {}, {}, {}, {}
                    for j in reversed(order):
                        sl = slice(j * A_CHUNK, (j + 1) * A_CHUNK)
                        dqd_i[j] = _bdot(do[sl, :], befores[j])
                        dv_i[j] = _bdot(ke[sl, :], gt, _NT)
                        dke[j] = _bdot(vv[sl, :], gt)
                        carry_rows[j] = jnp.broadcast_to(_colsum(gt * afters[j]), (A_CHUNK, HEAD))
                        gt = gt * e_big[j * A_CHUNK:j * A_CHUNK + 1, :] + _bdot(do[sl, :], qd[sl, :], _TN)
                    gt_s[d] = gt
                    cat = lambda parts: jnp.concatenate([parts[j] for j in range(cpb)], axis=0)
                    dqh = (dqd + cat(dqd_i)) * eb
                    dk = dkd * enb + cat(dke) * erest
                    tt = qh * dqh - k * dk
                    cum = triu if d == 0 else tril
                    dlf = _dot(cum.astype(F32), tt, precision=HIGHEST) + cat(carry_rows)
                    common = dlf / big_f - dk
                    res_s[1 + d, rows, :] = (k * sf * common).astype(BF16)
                    dq_s[rows, :] += dqh
                    dv_s[rows, :] += dv + cat(dv_i)
                    new.append(_colsum((1.0 - sf) * common))
                return dlb + jnp.concatenate(new, axis=0)

            dlb_ref[...] = lax.fori_loop(0, nb, pass_back, jnp.zeros((2, HEAD), F32))

            def pass_out(i, carry):
                rows = pl.ds(pl.multiple_of(i * HEAD, HEAD), HEAD)
                res_s[0, rows, :] = (dq_s[rows, :] * q_scale).astype(BF16)
                res_s[3, rows, :] = dv_s[rows, :].astype(BF16)
                return carry

            lax.fori_loop(0, nb, pass_out, 0)

        out_ref[...] = res_s[p]

    def col(pp):
        return pl.BlockSpec((t, HEAD), lambda h, p: (0, pp * heads + h))

    n_in = dproj.shape[1]
    return pl.pallas_call(
        body, name="hgrn_bwd", grid=(heads, 5),
        in_specs=[col(0), col(1), col(2), col(3), col(4),
                  pl.BlockSpec((t, HEAD), lambda h, p: (0, h)), pl.BlockSpec((t, HEAD), lambda h, p: (0, h)),
                  pl.BlockSpec((2, HEAD), lambda h, p: (0, h)), pl.BlockSpec((1, HEAD), lambda h, p: (0, 0)),
                  pl.BlockSpec(memory_space=pl.ANY)],
        out_specs=[pl.BlockSpec((t, HEAD), lambda h, p: (0, p * heads + h)),
                   pl.BlockSpec((None, 1, HEAD), lambda h, p: (h, 0, 0)),
                   pl.BlockSpec((2, HEAD), lambda h, p: (0, h))],
        out_shape=[jax.ShapeDtypeStruct((t, n_in), BF16), jax.ShapeDtypeStruct((heads, 1, HEAD), F32),
                   jax.ShapeDtypeStruct((2, width), F32)],
        scratch_shapes=[pltpu.VMEM((t, HEAD), F32), pltpu.VMEM((t, HEAD), F32), pltpu.VMEM((t, HEAD), F32),
                        pltpu.VMEM((5, t, HEAD), BF16), pltpu.VMEM((2, nb, HEAD, HEAD), F32),
                        pltpu.VMEM((2, HEAD, HEAD), F32), pltpu.VMEM((2, HEAD, HEAD), F32)],
        input_output_aliases={9: 0},
        compiler_params=_params(2),
    )(proj, proj, proj, proj, proj, osum, dout_a, lb, g_norm, dproj)


def _adamw(w, g, m, v):
    m = ADAM_B1 * m + (1.0 - ADAM_B1) * g
    v = ADAM_B2 * v + (1.0 - ADAM_B2) * (g * g)
    m_hat = m / (1.0 - ADAM_B1 ** ADAM_STEP)
    v_hat = v / (1.0 - ADAM_B2 ** ADAM_STEP)
    delta = -ADAM_LR * (m_hat / (jnp.sqrt(v_hat) + ADAM_EPS) + ADAM_WD * w)
    return delta, m, v


def _adamw_big(name, me, w, m, v, g_full, landing, axis):
    r, c = w.shape
    tr = _tile(r, 128)

    def body(me_ref, w_ref, m_ref, v_ref, g_ref, l_ref, og_ref, od_ref, om_ref, ov_ref):
        g = g_ref[...]
        for s in range(N_DEV - 1):
            g = g + l_ref[s].astype(F32)
        og_ref[...] = g
        od_ref[...], om_ref[...], ov_ref[...] = _adamw(w_ref[...], g, m_ref[...], v_ref[...])

    shard = pl.BlockSpec((tr, c), lambda i, me_ref: (i, 0))
    if axis == 1:
        own = pl.BlockSpec((tr, c), lambda i, me_ref: (i, me_ref[0]))
    else:
        own = pl.BlockSpec((tr, c), lambda i, me_ref: (me_ref[0] * (r // tr) + i, 0))
    grid_spec = pltpu.PrefetchScalarGridSpec(
        num_scalar_prefetch=1, grid=(r // tr,),
        in_specs=[shard, shard, shard, own, pl.BlockSpec((N_DEV - 1, tr, c), lambda i, me_ref: (0, i, 0))],
        out_specs=[shard] * 4)
    return pl.pallas_call(
        body, name=name, grid_spec=grid_spec, out_shape=[jax.ShapeDtypeStruct((r, c), F32)] * 4,
        compiler_params=_params(1),
    )(me, w, m, v, g_full, landing)


def _adamw_ada(sct, dmod_mine, w, m, v):
    d, n = w.shape
    tr = _tile(d, 256)

    def body(s_ref, dm_ref, w_ref, m_ref, v_ref, og_ref, od_ref, om_ref, ov_ref):
        g = _dot(s_ref[...], dm_ref[...], precision=HIGHEST)
        og_ref[...] = g
        od_ref[...], om_ref[...], ov_ref[...] = _adamw(w_ref[...], g, m_ref[...], v_ref[...])

    blk = pl.BlockSpec((tr, n), lambda i: (i, 0))
    return pl.pallas_call(
        body, name="adamw_ada", grid=(d // tr,),
        in_specs=[pl.BlockSpec((tr, N_DEV), lambda i: (i, 0)), pl.BlockSpec((N_DEV, n), lambda i: (0, 0)), blk, blk, blk],
        out_specs=[blk] * 4, out_shape=[jax.ShapeDtypeStruct((d, n), F32)] * 4, compiler_params=_params(1),
    )(sct, dmod_mine, w, m, v)


def _adamw_small(gathered, w, m, v):
    def body(g_ref, w_ref, m_ref, v_ref, og_ref, od_ref, om_ref, ov_ref):
        g = g_ref[0]
        for s in range(1, N_DEV):
            g = g + g_ref[s]
        og_ref[...] = g
        od_ref[...], om_ref[...], ov_ref[...] = _adamw(w_ref[...], g, m_ref[...], v_ref[...])

    return pl.pallas_call(
        body, name="adamw_small", out_shape=[jax.ShapeDtypeStruct(w.shape, F32)] * 4,
        compiler_params=pltpu.CompilerParams(vmem_limit_bytes=VMEM_LIMIT),
    )(gathered, w, m, v)


def _adamw_lb(dlb_mine, lb_logits, m, v):
    def body(d_ref, l_ref, m_ref, v_ref, og_ref, od_ref, om_ref, ov_ref):
        dlb = d_ref[0]
        for s in range(1, N_DEV):
            dlb = dlb + d_ref[s]
        for dr in range(2):
            lb = _sigmoid(l_ref[dr][0:1, :] - l_ref[dr][1:2, :])
            d0 = dlb[dr:dr + 1] * lb * (1.0 - lb)
            g = jnp.concatenate([d0, -d0], axis=0)
            og_ref[dr] = g
            od_ref[dr], om_ref[dr], ov_ref[dr] = _adamw(l_ref[dr], g, m_ref[dr], v_ref[dr])

    return pl.pallas_call(body, name="adamw_lb", out_shape=[jax.ShapeDtypeStruct(lb_logits.shape, F32)] * 4,
                          )(dlb_mine, lb_logits, m, v)


def _rows(a, pad_to=8):
    flat = a.reshape(-1, LANE)
    pad = (-flat.shape[0]) % pad_to
    return jnp.pad(flat, ((0, pad), (0, 0))) if pad else flat


def kernel(x, c, w_ada, b_ada, g_pre_mix, g_post_mix, g_pre_ffn, g_post_ffn, w_in, lb_logits, g_hgrn_norm, w_a_out, g_sgu_norm, w_spatial, b_spatial, w_b_out, w_o, w_ff1, w_ff2, loss_target, m_w_ada, m_b_ada, m_g_pre_mix, m_g_post_mix, m_g_pre_ffn, m_g_post_ffn, m_w_in, m_lb_logits, m_g_hgrn_norm, m_w_a_out, m_g_sgu_norm, m_w_spatial, m_b_spatial, m_w_b_out, m_w_o, m_w_ff1, m_w_ff2, v_w_ada, v_b_ada, v_g_pre_mix, v_g_post_mix, v_g_pre_ffn, v_g_post_ffn, v_w_in, v_lb_logits, v_g_hgrn_norm, v_w_a_out, v_g_sgu_norm, v_w_spatial, v_b_spatial, v_w_b_out, v_w_o, v_w_ff1, v_w_ff2):
    t, d = x.shape[1], x.shape[2]
    n_in = w_in.shape[2] * N_DEV
    width = (n_in - 2 * d) // 7
    heads = width // HEAD
    assert heads == N_DEV and width % LANE == 0
    d_ff = w_ff1.shape[2] * N_DEV
    n_ada = w_ada.shape[2]
    me = _dev_index()
    me_arr = me.reshape(1).astype(jnp.int32)
    x2, tgt = x[0], loss_target[0]

    big = [w_in[0], w_a_out[0], w_b_out[0], w_o[0], w_ff1[0], w_ff2[0]]
    big_axes = [1, 1, 1, 0, 1, 0]
    big_names = ["w_in", "w_a_out", "w_b_out", "w_o", "w_ff1", "w_ff2"]
    shards16 = [_cast_bf16("cast_" + nm, w) for nm, w in zip(big_names, big)]
    wf_in, wf_a, wf_b, wf_o, wf_1, wf_2 = _all_gather_weights(shards16, big_axes)

    c_rows = d // LANE
    small = _all_gather_small("gather_c_lb", _prep_small(c[0:1], lb_logits))
    sc_all = small[:, :c_rows, :].reshape(N_DEV, d)
    lb = jnp.transpose(small[:, c_rows:c_rows + 2, :], (1, 0, 2)).reshape(2, width)
    b_shard = lax.dynamic_slice_in_dim(b_ada, me * n_ada, n_ada, axis=1)
    mod_sh = _mod_shard(sc_all, w_ada[0], b_shard)
    mod_all = _all_gather_small("gather_mod", _rows(mod_sh))
    mod_all = mod_all[:, :N_DEV * n_ada // LANE, :].reshape(N_DEV, N_DEV, n_ada)
    mod6 = lax.dynamic_index_in_dim(mod_all, me, axis=1, keepdims=False).reshape(N_MOD, d)
    sh1, sc1, gt1, sh2, sc2, gt2 = [mod6[i:i + 1] for i in range(N_MOD)]

    a1 = _norm_mod(x2, g_pre_mix, sh1, sc1)
    tm = _tile(t, 512)

    def store_f32(acc, i, j, extra_refs, out_refs):
        out_refs[0][...] = acc

    tn_in = _tile(n_in, 1024)
    proj, = _mm("proj", a1, wf_in, _NN, t, n_in, d, tm, tn_in, d, [],
                [(jax.ShapeDtypeStruct((t, n_in), F32), (tm, tn_in), lambda i, j: (i, j))], store_f32)

    out_a, osum = _hgrn_fwd(proj, lb, g_hgrn_norm, width)
    z_block = 5
    bst = b_spatial[0].T
    out_b = _sgu_fwd(proj, g_sgu_norm, w_spatial[0], bst, width, z_block)

    tn_d = _tile(d, 512)
    blk_d = ((tm, tn_d), lambda i, j: (i, j))
    y_a, = _mm("y_a", out_a, wf_a, _NN, t, d, width, tm, tn_d, width, [],
               [(jax.ShapeDtypeStruct((t, d), F32),) + blk_d], store_f32)
    ga_blk = (5 * width + 2 * width) // tn_d
    gb_blk = ga_blk + d // tn_d

    def merge(acc, i, j, extra_refs, out_refs):
        ga, gb, ya = extra_refs
        out_refs[0][...] = acc
        out_refs[1][...] = (_sigmoid(ga[...]) * ya[...] + _sigmoid(gb[...]) * acc).astype(BF16)

    y_b, merged = _mm("y_b_merge", out_b, wf_b, _NN, t, d, width, tm, tn_d, width,
                      [(proj, (tm, tn_d), lambda i, j: (i, ga_blk + j)), (proj, (tm, tn_d), lambda i, j: (i, gb_blk + j)),
                       (y_a,) + blk_d],
                      [(jax.ShapeDtypeStruct((t, d), F32),) + blk_d, (jax.ShapeDtypeStruct((t, d), BF16),) + blk_d], merge)

    tr = _tile(t, 256)
    row_d = ((tr, d), lambda i, j: (i, 0))
    vec_d = ((1, d), lambda i, j: (0, 0))

    def post_mix(acc, i, j, extra_refs, out_refs):
        x_r, gt1_r, g2_r, g3_r, sc2_r, sh2_r = extra_refs
        h1 = x_r[...] + gt1_r[...] * (acc * _rms(acc) * g2_r[...])
        out_refs[0][...] = acc
        out_refs[1][...] = h1
        out_refs[2][...] = ((h1 * _rms(h1) * g3_r[...]) * (1.0 + sc2_r[...]) + sh2_r[...]).astype(BF16)

    mo, h1, a2 = _mm("w_o_post_mix", merged, wf_o, _NN, t, d, d, tr, d, d,
                     [(x2,) + row_d, (gt1,) + vec_d, (g_post_mix,) + vec_d, (g_pre_ffn,) + vec_d, (sc2,) + vec_d, (sh2,) + vec_d],
                     [(jax.ShapeDtypeStruct((t, d), F32),) + row_d, (jax.ShapeDtypeStruct((t, d), F32),) + row_d,
                      (jax.ShapeDtypeStruct((t, d), BF16),) + row_d], post_mix)

    tn_f = _tile(d_ff, 1024)
    blk_f = ((tm, tn_f), lambda i, j: (i, j))

    def relu_sq(acc, i, j, extra_refs, out_refs):
        r = jnp.maximum(acc, 0.0)
        out_refs[0][...] = acc.astype(BF16)
        out_refs[1][...] = (r * r).astype(BF16)

    hff, act = _mm("ff1", a2, wf_1, _NN, t, d_ff, d, tm, tn_f, d, [],
                   [(jax.ShapeDtypeStruct((t, d_ff), BF16),) + blk_f, (jax.ShapeDtypeStruct((t, d_ff), BF16),) + blk_f], relu_sq)

    sums_d = ((8, d), lambda i, j: (0, 0))

    def loss_head(acc, i, j, extra_refs, out_refs):
        h1_r, tgt_r, gt2_r, g4_r = extra_refs
        dy_r, dff_r, sums_r = out_refs
        r4 = _rms(acc)
        ffn = acc * r4
        n4 = ffn * g4_r[...]
        err = h1_r[...] + gt2_r[...] * n4 - tgt_r[...]
        dy = err * (1.0 / d)
        dy_r[...] = dy
        dn4 = dy * gt2_r[...]
        dffn = dn4 * g4_r[...]
        dff_r[...] = (r4 * (dffn - ffn * jnp.mean(dffn * ffn, axis=-1, keepdims=True))).astype(BF16)

        @pl.when(i == 0)
        def _():
            sums_r[...] = jnp.zeros_like(sums_r)

        sums_r[0:1, :] += _colsum(err * err)
        sums_r[1:2, :] += _colsum(dy * n4)
        sums_r[2:3, :] += _colsum(dn4 * ffn)

    tk_f = _tile(d_ff, 1024)
    dy, dff, sums_f = _mm("ff2_loss", act, wf_2, _NN, t, d, d_ff, tr, d, tk_f,
                          [(h1,) + row_d, (tgt,) + row_d, (gt2,) + vec_d, (g_post_ffn,) + vec_d],
                          [(jax.ShapeDtypeStruct((t, d), F32),) + row_d, (jax.ShapeDtypeStruct((t, d), BF16),) + row_d,
                           (jax.ShapeDtypeStruct((8, d), F32),) + sums_d], loss_head)
    loss = lax.psum((0.5 / d) * jnp.sum(sums_f[0]), ("x", "y", "c"))

    def relu_sq_bwd(acc, i, j, extra_refs, out_refs):
        out_refs[0][...] = (acc * (2.0 * jnp.maximum(extra_refs[0][...].astype(F32), 0.0))).astype(BF16)

    dhff, = _mm("d_hff", dff, wf_2, _NT, t, d_ff, d, tm, tn_f, d, [(hff,) + blk_f],
                [(jax.ShapeDtypeStruct((t, d_ff), BF16),) + blk_f], relu_sq_bwd)
    gw_ff2, gw_ff2_16 = _grad_w("grad_w_ff2", act, dff)
    gw_ff1, gw_ff1_16 = _grad_w("grad_w_ff1", a2, dhff)

    def pre_ffn_bwd(acc, i, j, extra_refs, out_refs):
        h1_r, dy_r, mo_r, sc2_r, g3_r, gt1_r, g2_r = extra_refs
        dh1_r, dmo_r, sums_r = out_refs
        h1v = h1_r[...]
        r3 = _rms(h1v)
        h1n = h1v * r3
        dn3 = acc * (1.0 + sc2_r[...])
        dh1n = dn3 * g3_r[...]
        dh1 = dy_r[...] + r3 * (dh1n - h1n * jnp.mean(dh1n * h1n, axis=-1, keepdims=True))
        dh1_r[...] = dh1
        mov = mo_r[...]
        r2 = _rms(mov)
        mon = mov * r2
        dn2 = dh1 * gt1_r[...]
        dmon = dn2 * g2_r[...]
        dmo_r[...] = (r2 * (dmon - mon * jnp.mean(dmon * mon, axis=-1, keepdims=True))).astype(BF16)

        @pl.when(i == 0)
        def _():
            sums_r[...] = jnp.zeros_like(sums_r)

        sums_r[0:1, :] += _colsum(acc)
        sums_r[1:2, :] += _colsum(acc * (h1n * g3_r[...]))
        sums_r[2:3, :] += _colsum(dn3 * h1n)
        sums_r[3:4, :] += _colsum(dh1 * (mon * g2_r[...]))
        sums_r[4:5, :] += _colsum(dn2 * mon)

    dh1, dmo, sums_m = _mm("d_a2_pre_ffn", dhff, wf_1, _NT, t, d, d_ff, tr, d, tk_f,
                           [(h1,) + row_d, (dy,) + row_d, (mo,) + row_d, (sc2,) + vec_d, (g_pre_ffn,) + vec_d,
                            (gt1,) + vec_d, (g_post_mix,) + vec_d],
                           [(jax.ShapeDtypeStruct((t, d), F32),) + row_d, (jax.ShapeDtypeStruct((t, d), BF16),) + row_d,
                            (jax.ShapeDtypeStruct((8, d), F32),) + sums_d], pre_ffn_bwd)
    gw_o, gw_o_16 = _grad_w("grad_w_o", merged, dmo)

    n_j = d // tn_d

    def merge_bwd_body(dmo_ref, wo_ref, ga_ref, gb_ref, ya_ref, yb_ref, dya_ref, dyb_ref, dproj_ref, acc_s):
        g = pl.program_id(2)

        @pl.when(g == 0)
        def _():
            dm = _dot(dmo_ref[...], wo_ref[...], _NT)
            acc_s[...] = dm
            sa = _sigmoid(ga_ref[...])
            dya_ref[...] = (dm * sa).astype(BF16)
            dproj_ref[...] = (dm * ya_ref[...] * sa * (1.0 - sa)).astype(BF16)

        @pl.when(g == 1)
        def _():
            dm = acc_s[...]
            sb = _sigmoid(gb_ref[...])
            dyb_ref[...] = (dm * sb).astype(BF16)
            dproj_ref[...] = (dm * yb_ref[...] * sb * (1.0 - sb)).astype(BF16)

    tile3 = pl.BlockSpec((tm, tn_d), lambda i, j, g: (i, j))
    dy_a, dy_b, dproj = pl.pallas_call(
        merge_bwd_body, name="d_merged", grid=(t // tm, n_j, 2),
        in_specs=[pl.BlockSpec((tm, d), lambda i, j, g: (i, 0)), pl.BlockSpec((tn_d, d), lambda i, j, g: (j, 0)),
                  pl.BlockSpec((tm, tn_d), lambda i, j, g: (i, ga_blk + j)),
                  pl.BlockSpec((tm, tn_d), lambda i, j, g: (i, gb_blk + j)), tile3, tile3],
        out_specs=[tile3, tile3, pl.BlockSpec((tm, tn_d), lambda i, j, g: (i, ga_blk + g * n_j + j))],
        out_shape=[jax.ShapeDtypeStruct((t, d), BF16), jax.ShapeDtypeStruct((t, d), BF16),
                   jax.ShapeDtypeStruct((t, n_in), BF16)],
        scratch_shapes=[pltpu.VMEM((tm, tn_d), F32)], compiler_params=_params(3),
    )(dmo, wf_o, proj, proj, y_a, y_b)

    def store_bf16(acc, i, j, extra_refs, out_refs):
        out_refs[0][...] = acc.astype(BF16)

    tn_w = _tile(width, 512)
    blk_w = ((tm, tn_w), lambda i, j: (i, j))
    dout_a, = _mm("d_out_a", dy_a, wf_a, _NT, t, width, d, tm, tn_w, d, [],
                  [(jax.ShapeDtypeStruct((t, width), BF16),) + blk_w], store_bf16)
    dout_b, = _mm("d_out_b", dy_b, wf_b, _NT, t, width, d, tm, tn_w, d, [],
                  [(jax.ShapeDtypeStruct((t, width), BF16),) + blk_w], store_bf16)
    gw_a, gw_a_16 = _grad_w("grad_w_a_out", out_a, dy_a)
    gw_b, gw_b_16 = _grad_w("grad_w_b_out", out_b, dy_b)

    w_st = jnp.swapaxes(w_spatial[0], 1, 2)
    dproj, dg_sgu, dw_sp, dbst = _sgu_bwd(proj, dout_b, dproj, g_sgu_norm, w_spatial[0], w_st, bst, width, z_block)
    dproj, dgh_heads, dlb = _hgrn_bwd(proj, osum, dout_a, dproj, lb, g_hgrn_norm, width)
    gw_in, gw_in_16 = _grad_w("grad_w_in", a1, dproj)

    def pre_mix_bwd(acc, i, j, extra_refs, out_refs):
        x_r, dh1_r, sc1_r, g1_r = extra_refs
        dx_r, sums_r = out_refs
        xv = x_r[...]
        r1 = _rms(xv)
        xn = xv * r1
        dn1 = acc * (1.0 + sc1_r[...])
        dxn = dn1 * g1_r[...]
        dx_r[...] = dh1_r[...] + r1 * (dxn - xn * jnp.mean(dxn * xn, axis=-1, keepdims=True))

        @pl.when(i == 0)
        def _():
            sums_r[...] = jnp.zeros_like(sums_r)

        sums_r[0:1, :] += _colsum(acc)
        sums_r[1:2, :] += _colsum(acc * (xn * g1_r[...]))
        sums_r[2:3, :] += _colsum(dn1 * xn)

    tk_in = _tile(n_in, 1024)
    grad_x, sums_x = _mm("d_a1_pre_mix", dproj, wf_in, _NT, t, d, n_in, tr, d, tk_in,
                         [(x2,) + row_d, (dh1,) + row_d, (sc1,) + vec_d, (g_pre_mix,) + vec_d],
                         [(jax.ShapeDtypeStruct((t, d), F32),) + row_d, (jax.ShapeDtypeStruct((8, d), F32),) + sums_d],
                         pre_mix_bwd)

    lands = _scatter_grads([gw_in_16, gw_a_16, gw_b_16, gw_o_16, gw_ff1_16, gw_ff2_16], big_axes)
    full32 = [gw_in, gw_a, gw_b, gw_o, gw_ff1, gw_ff2]
    moms = [m_w_in, m_w_a_out, m_w_b_out, m_w_o, m_w_ff1, m_w_ff2]
    vars_ = [v_w_in, v_w_a_out, v_w_b_out, v_w_o, v_w_ff1, v_w_ff2]
    big_out = {}
    for nm, w, mm_, vv_, gf, ld, ax in zip(big_names, big, moms, vars_, full32, lands, big_axes):
        big_out[nm] = [o[None] for o in _adamw_big("adamw_" + nm, me_arr, w, mm_[0], vv_[0], gf, ld, ax)]

    dmod = jnp.concatenate([sums_x[0:2], sums_m[3:4], sums_m[0:2], sums_f[1:2]], axis=0).reshape(N_DEV, n_ada // LANE, LANE)
    ada_rows = -(-(n_ada // LANE) // 8) * 8
    dmod = jnp.pad(dmod, ((0, 0), (0, ada_rows - n_ada // LANE), (0, 0))).reshape(N_DEV * ada_rows, LANE)
    parts = [dmod, _rows(sums_x[2:3]), _rows(sums_m[4:5]), _rows(sums_m[2:3]), _rows(sums_f[2:3]),
             _rows(jnp.sum(dgh_heads, axis=0)), _rows(dg_sgu), _rows(dw_sp), _rows(dbst.T)]
    n_common = sum(p.shape[0] for p in parts)
    payload = jnp.concatenate(parts + [_rows(dlb)], axis=0)
    gathered = _all_gather_small("gather_small_grads", payload)

    dmod_mine = lax.dynamic_slice_in_dim(gathered[:, :N_DEV * ada_rows, :].reshape(N_DEV, N_DEV, ada_rows * LANE),
                                         me, 1, axis=1)[:, 0, :n_ada]
    ada_out = [o[None] for o in _adamw_ada(sc_all.T, dmod_mine, w_ada[0], m_w_ada[0], v_w_ada[0])]

    def pack(b_, g1_, g2_, g3_, g4_, gh_, gs_, ws_, bs_):
        b3 = b_.reshape(N_DEV, n_ada // LANE, LANE)
        b3 = jnp.pad(b3, ((0, 0), (0, ada_rows - n_ada // LANE), (0, 0))).reshape(N_DEV * ada_rows, LANE)
        return jnp.concatenate([b3, _rows(g1_), _rows(g2_), _rows(g3_), _rows(g4_), _rows(gh_), _rows(gs_),
                                _rows(ws_), _rows(bs_)], axis=0)

    small_w = (b_ada, g_pre_mix, g_post_mix, g_pre_ffn, g_post_ffn, g_hgrn_norm, g_sgu_norm, w_spatial, b_spatial)
    small_m = (m_b_ada, m_g_pre_mix, m_g_post_mix, m_g_pre_ffn, m_g_post_ffn, m_g_hgrn_norm, m_g_sgu_norm, m_w_spatial, m_b_spatial)
    small_v = (v_b_ada, v_g_pre_mix, v_g_post_mix, v_g_pre_ffn, v_g_post_ffn, v_g_hgrn_norm, v_g_sgu_norm, v_w_spatial, v_b_spatial)
    packed = _adamw_small(gathered[:, :n_common, :], pack(*small_w), pack(*small_m), pack(*small_v))

    def unpack(slab):
        outs, at = [], 0
        b3 = slab[:N_DEV * ada_rows].reshape(N_DEV, ada_rows, LANE)[:, :n_ada // LANE, :]
        outs.append(b3.reshape(b_ada.shape))
        at = N_DEV * ada_rows
        for ref in small_w[1:]:
            n_el = ref.size
            n_r = -(-(n_el // LANE) // 8) * 8
            outs.append(slab[at:at + n_el // LANE].reshape(ref.shape))
            at += n_r
        return outs

    small_out = [unpack(s) for s in packed]

    dlb_all = gathered[:, n_common:n_common + 2 * heads, :].reshape(N_DEV, 2, heads, LANE)
    dlb_mine = lax.dynamic_index_in_dim(dlb_all, me, axis=2, keepdims=False)
    lb_out = _adamw_lb(dlb_mine, lb_logits, m_lb_logits, v_lb_logits)

    order = ["w_ada", "b_ada", "g_pre_mix", "g_post_mix", "g_pre_ffn", "g_post_ffn", "w_in", "lb_logits", "g_hgrn_norm",
             "w_a_out", "g_sgu_norm", "w_spatial", "b_spatial", "w_b_out", "w_o", "w_ff1", "w_ff2"]
    small_names = ["b_ada", "g_pre_mix", "g_post_mix", "g_pre_ffn", "g_post_ffn", "g_hgrn_norm", "g_sgu_norm", "w_spatial", "b_spatial"]

    def leaf(kind, nm):
        if nm == "w_ada":
            return ada_out[kind]
        if nm == "lb_logits":
            return lb_out[kind]
        if nm in big_out:
            return big_out[nm][kind]
        return small_out[kind][small_names.index(nm)]

    result = [loss, grad_x[None]]
    for kind in range(4):
        result += [leaf(kind, nm) for nm in order]
    return tuple(result)
```

```python
import functools
import math

import jax
import jax.numpy as jnp
from jax import lax
from jax.experimental import pallas as pl
from jax.experimental.pallas import tpu as pltpu

F32 = jnp.float32
BF16 = jnp.bfloat16
MESH = pl.DeviceIdType.MESH
HIGHEST = lax.Precision.HIGHEST

N_DEV = 8
HEAD = 128
A_CHUNK = 32
N_MOD = 6
EPS = 1e-6
LANE = 128
VMEM_LIMIT = 56 * 1024 * 1024

ADAM_LR = 0.001
ADAM_B1 = 0.9
ADAM_B2 = 0.999
ADAM_EPS = 1e-08
ADAM_WD = 0.01
ADAM_STEP = 10

_NN = (((1,), (0,)), ((), ()))
_NT = (((1,), (1,)), ((), ()))
_TN = (((0,), (0,)), ((), ()))


def _dot(a, b, dims=_NN, precision=None):
    return lax.dot_general(a, b, dims, preferred_element_type=F32, precision=precision)


def _bdot(a, b, dims=_NN):
    return _dot(a.astype(BF16), b.astype(BF16), dims)


def _params(n_grid):
    return pltpu.CompilerParams(dimension_semantics=("arbitrary",) * n_grid, vmem_limit_bytes=VMEM_LIMIT)


def _dev_index():
    return lax.axis_index("x") * 4 + lax.axis_index("y") * 2 + lax.axis_index("c")


def _dev_coords(i):
    return (i // 4, (i // 2) % 2, i % 2)


def _sigmoid(x):
    return 1.0 / (1.0 + jnp.exp(-x))


def _erf(x):
    ax = jnp.abs(x)
    t = 1.0 / (1.0 + 0.3275911 * ax)
    poly = ((((1.061405429 * t - 1.453152027) * t + 1.421413741) * t - 0.284496736) * t + 0.254829592) * t
    y = 1.0 - poly * jnp.exp(-ax * ax)
    return jnp.where(x < 0, -y, y)


def _gelu_and_grad(x):
    cdf = 0.5 * (1.0 + _erf(x * (2.0 ** -0.5)))
    pdf = jnp.exp(-0.5 * x * x) * (1.0 / math.sqrt(2.0 * math.pi))
    return x * cdf, cdf + x * pdf


def _rms(x):
    return lax.rsqrt(jnp.mean(x * x, axis=-1, keepdims=True) + EPS)


def _colsum(x):
    return jnp.sum(x, axis=0, keepdims=True)


def _tile(n, want):
    if n <= want:
        return n
    t = (want // LANE) * LANE
    while n % t:
        t -= LANE
    assert t > 0, (n, want)
    return t


def _all_gather_small(name, payload):
    rows = payload.shape[0]

    def body(p_ref, out_ref, send_sems, recv_sems, local_sem):
        me = _dev_index()
        mine = pltpu.make_async_copy(p_ref, out_ref.at[me], local_sem)
        mine.start()
        sends = []
        for r in range(1, N_DEV):
            peer = (me + r) % N_DEV
            cp = pltpu.make_async_remote_copy(
                src_ref=p_ref, dst_ref=out_ref.at[me], send_sem=send_sems.at[r - 1], recv_sem=recv_sems.at[r - 1],
                device_id=_dev_coords(peer), device_id_type=MESH)
            cp.start()
            sends.append(cp)
        for r in range(1, N_DEV):
            src = (me + N_DEV - r) % N_DEV
            pltpu.make_async_remote_copy(
                src_ref=p_ref, dst_ref=out_ref.at[src], send_sem=send_sems.at[r - 1], recv_sem=recv_sems.at[r - 1],
                device_id=_dev_coords(src), device_id_type=MESH).wait_recv()
        for cp in sends:
            cp.wait_send()
        mine.wait()

    return pl.pallas_call(
        body, name=name,
        out_shape=jax.ShapeDtypeStruct((N_DEV, rows, LANE), F32),
        in_specs=[pl.BlockSpec(memory_space=pltpu.VMEM)],
        out_specs=pl.BlockSpec(memory_space=pltpu.VMEM),
        scratch_shapes=[pltpu.SemaphoreType.DMA((N_DEV - 1,)), pltpu.SemaphoreType.DMA((N_DEV - 1,)),
                        pltpu.SemaphoreType.DMA],
        compiler_params=pltpu.CompilerParams(vmem_limit_bytes=VMEM_LIMIT),
    )(payload)


def _region(ref, dev, axis, n):
    start = pl.multiple_of(dev * n, LANE if axis == 1 else 16)
    return ref.at[:, pl.ds(start, n)] if axis == 1 else ref.at[pl.ds(start, n), :]


class _Exchange:
    def __init__(self, arrays, out_shapes, sems, start, finish):
        self.arrays, self.out_shapes, self.sems, self.start, self.finish = arrays, out_shapes, sems, start, finish


def _gather_plan(shards, axes):
    n_w = len(shards)
    fulls = []
    for s, ax in zip(shards, axes):
        shp = (s.shape[0], s.shape[1] * N_DEV) if ax == 1 else (s.shape[0] * N_DEV, s.shape[1])
        fulls.append(jax.ShapeDtypeStruct(shp, BF16))
    widths = [s.shape[ax] for s, ax in zip(shards, axes)]

    def places():
        x, y, c = lax.axis_index("x"), lax.axis_index("y"), lax.axis_index("c")
        chips = [(1 - x, y), (x, 1 - y), (1 - x, 1 - y)]
        return (x, y, c), (x, y, 1 - c), chips

    def index(p):
        return p[0] * 4 + p[1] * 2 + p[2]

    def copy(w, k, s_refs, f_refs, sems, block, to, from_shard):
        send_sems, recv_sems, _ = sems
        dst = _region(f_refs[w], index(block), axes[w], widths[w])
        return pltpu.make_async_remote_copy(
            src_ref=s_refs[w] if from_shard else dst, dst_ref=dst,
            send_sem=send_sems.at[w, k], recv_sem=recv_sems.at[w, k], device_id=to, device_id_type=MESH)

    def local(w, s_refs, f_refs, sems, me):
        return pltpu.make_async_copy(s_refs[w], _region(f_refs[w], index(me), axes[w], widths[w]), sems[2].at[w])

    def start(s_refs, f_refs, sems):
        me, sib, chips = places()
        for w in range(n_w):
            local(w, s_refs, f_refs, sems, me).start()
            copy(w, 0, s_refs, f_refs, sems, me, sib, True).start()
            for j, chip in enumerate(chips):
                copy(w, 1 + j, s_refs, f_refs, sems, me, (*chip, me[2]), True).start()

    def finish(s_refs, f_refs, sems):
        me, sib, chips = places()
        for w in range(n_w):
            for j, chip in enumerate(chips):
                copy(w, 1 + j, s_refs, f_refs, sems, (*chip, me[2]), me, True).wait_recv()
                copy(w, 4 + j, s_refs, f_refs, sems, (*chip, me[2]), sib, False).start()
        for w in range(n_w):
            copy(w, 0, s_refs, f_refs, sems, sib, me, True).wait_recv()
            for j, chip in enumerate(chips):
                copy(w, 4 + j, s_refs, f_refs, sems, (*chip, sib[2]), me, False).wait_recv()
        for w in range(n_w):
            for k in range(N_DEV - 1):
                copy(w, k, s_refs, f_refs, sems, me, sib, True).wait_send()
            local(w, s_refs, f_refs, sems, me).wait()

    sems = [pltpu.SemaphoreType.DMA((n_w, N_DEV - 1)), pltpu.SemaphoreType.DMA((n_w, N_DEV - 1)),
            pltpu.SemaphoreType.DMA((n_w,))]
    return _Exchange(list(shards), fulls, sems, start, finish)


def _scatter_plan(grads, axes):
    n_w = len(grads)
    lands = []
    for g, ax in zip(grads, axes):
        shp = (g.shape[0], g.shape[1] // N_DEV) if ax == 1 else (g.shape[0] // N_DEV, g.shape[1])
        lands.append(jax.ShapeDtypeStruct((N_DEV - 1,) + shp, BF16))
    widths = [ld.shape[1 + ax] for ld, ax in zip(lands, axes)]

    def copy(w, r, g_refs, l_refs, sems, block, to):
        return pltpu.make_async_remote_copy(
            src_ref=_region(g_refs[w], block, axes[w], widths[w]), dst_ref=l_refs[w].at[r - 1],
            send_sem=sems[0].at[w, r - 1], recv_sem=sems[1].at[w, r - 1],
            device_id=_dev_coords(to), device_id_type=MESH)

    def start(g_refs, l_refs, sems):
        me = _dev_index()
        for w in range(n_w):
            for r in range(1, N_DEV):
                owner = (me + r) % N_DEV
                copy(w, r, g_refs, l_refs, sems, owner, owner).start()

    def finish(g_refs, l_refs, sems):
        me = _dev_index()
        for w in range(n_w):
            for r in range(1, N_DEV):
                copy(w, r, g_refs, l_refs, sems, me, (me + N_DEV - r) % N_DEV).wait_recv()
        for w in range(n_w):
            for r in range(1, N_DEV):
                copy(w, r, g_refs, l_refs, sems, me, (me + r) % N_DEV).wait_send()

    sems = [pltpu.SemaphoreType.DMA((n_w, N_DEV - 1)), pltpu.SemaphoreType.DMA((n_w, N_DEV - 1))]
    return _Exchange(list(grads), lands, sems, start, finish)


def _run_exchange(name, plan):
    n_in, n_out = len(plan.arrays), len(plan.out_shapes)

    def body(*refs):
        ins, outs, sems = refs[:n_in], refs[n_in:n_in + n_out], refs[n_in + n_out:]
        plan.start(ins, outs, sems)
        plan.finish(ins, outs, sems)

    any_spec = pl.BlockSpec(memory_space=pl.ANY)
    return pl.pallas_call(
        body, name=name, out_shape=plan.out_shapes,
        in_specs=[any_spec] * n_in, out_specs=[any_spec] * n_out, scratch_shapes=plan.sems,
    )(*plan.arrays)


def _mm(name, a, b, dims, m, n, k, tm, tn, tk, extras, outs, epilogue, row_chunk=None, exchange=None):
    ni, nj, nk = m // tm, n // tn, k // tk
    ne, no = len(extras), len(outs)
    xin = len(exchange.arrays) if exchange else 0
    xout = len(exchange.out_shapes) if exchange else 0
    if dims == _TN:
        a_spec = pl.BlockSpec((tk, tm), lambda i, j, kk: (kk, i))
    else:
        a_spec = pl.BlockSpec((tm, tk), lambda i, j, kk: (i, kk))
    if dims == _NT:
        b_spec = pl.BlockSpec((tn, tk), lambda i, j, kk: (j, kk))
    else:
        b_spec = pl.BlockSpec((tk, tn), lambda i, j, kk: (kk, j))
    chunks = [slice(None)] if row_chunk is None else [slice(r, r + row_chunk) for r in range(0, tm, row_chunk)]

    def lift(index_map):
        return lambda i, j, kk: index_map(i, j)

    def body(a_ref, b_ref, *rest):
        extra_refs, rest = rest[:ne], rest[ne:]
        xin_refs, rest = rest[:xin], rest[xin:]
        out_refs, rest = rest[:no], rest[no:]
        xout_refs, rest = rest[:xout], rest[xout:]
        i, j, kk = pl.program_id(0), pl.program_id(1), pl.program_id(2)
        if exchange:
            sem_refs = rest[1:] if nk > 1 else rest

            @pl.when((i == 0) & (j == 0) & (kk == 0))
            def _():
                exchange.start(xin_refs, xout_refs, sem_refs)

        part = _dot(a_ref[...], b_ref[...], dims)
        if nk == 1:
            for rows in chunks:
                epilogue(part[rows], i, j, extra_refs, out_refs, rows)
        else:
            acc_ref = rest[0]

            @pl.when(kk == 0)
            def _():
                acc_ref[...] = part

            @pl.when(kk > 0)
            def _():
                acc_ref[...] += part

            @pl.when(kk == nk - 1)
            def _():
                for rows in chunks:
                    epilogue(acc_ref[rows, :], i, j, extra_refs, out_refs, rows)

        if exchange:
            @pl.when((i == ni - 1) & (j == nj - 1) & (kk == nk - 1))
            def _():
                exchange.finish(xin_refs, xout_refs, sem_refs)

    any_spec = pl.BlockSpec(memory_space=pl.ANY)
    results = pl.pallas_call(
        body, name=name,
        grid=(ni, nj, nk),
        in_specs=[a_spec, b_spec] + [pl.BlockSpec(bs, lift(im)) for _, bs, im in extras] + [any_spec] * xin,
        out_specs=[pl.BlockSpec(bs, lift(im)) for _, bs, im in outs] + [any_spec] * xout,
        out_shape=[sd for sd, _, _ in outs] + (list(exchange.out_shapes) if exchange else []),
        scratch_shapes=([pltpu.VMEM((tm, tn), F32)] if nk > 1 else []) + (list(exchange.sems) if exchange else []),
        compiler_params=_params(3),
    )(a, b, *[arr for arr, _, _ in extras], *(exchange.arrays if exchange else []))
    return (results[:no], results[no:]) if exchange else results


def _grad_w(name, a, dc, tm=512, tn=1024):
    t, m = a.shape
    n = dc.shape[1]
    tm, tn = _tile(m, tm), _tile(n, tn)

    def epilogue(acc, i, j, extra_refs, out_refs, rows):
        out_refs[0][...] = acc
        out_refs[1][...] = acc.astype(BF16)

    blk = ((tm, tn), lambda i, j: (i, j))
    return _mm(name, a, dc, _TN, m, n, t, tm, tn, t, [],
               [(jax.ShapeDtypeStruct((m, n), F32),) + blk, (jax.ShapeDtypeStruct((m, n), BF16),) + blk], epilogue)


def _cast_bf16(name, w):
    r, c = w.shape
    tr = _tile(r, 256)
    return pl.pallas_call(
        lambda w_ref, o_ref: o_ref.__setitem__(Ellipsis, w_ref[...].astype(BF16)), name=name,
        grid=(r // tr,), in_specs=[pl.BlockSpec((tr, c), lambda i: (i, 0))],
        out_specs=pl.BlockSpec((tr, c), lambda i: (i, 0)), out_shape=jax.ShapeDtypeStruct((r, c), BF16),
        compiler_params=_params(1),
    )(w)


def _prep_small(c_row, lb_logits):
    d = c_row.shape[1]
    rows = d // LANE

    def body(c_ref, l_ref, o_ref):
        cv = c_ref[...]
        o_ref[0:rows, :] = cv * _sigmoid(cv)
        lbs = [_sigmoid(l_ref[dr][0:1, :] - l_ref[dr][1:2, :]) for dr in range(2)]
        o_ref[rows:rows + 8, :] = jnp.concatenate(lbs + [jnp.zeros((6, LANE), F32)], axis=0)

    return pl.pallas_call(
        body, name="prep_small", out_shape=jax.ShapeDtypeStruct((rows + 8, LANE), F32),
    )(c_row.reshape(rows, LANE), lb_logits)


def _mod_shard(sc_all, w_ada_shard, b_shard):
    d, n = w_ada_shard.shape
    tn = _tile(n, 512)

    def body(s_ref, w_ref, b_ref, o_ref):
        o_ref[...] = _dot(s_ref[...], w_ref[...], precision=HIGHEST) + b_ref[...]

    return pl.pallas_call(
        body, name="mod_shard", grid=(n // tn,),
        in_specs=[pl.BlockSpec((N_DEV, d), lambda j: (0, 0)), pl.BlockSpec((d, tn), lambda j: (0, j)),
                  pl.BlockSpec((1, tn), lambda j: (0, j))],
        out_specs=pl.BlockSpec((N_DEV, tn), lambda j: (0, j)),
        out_shape=jax.ShapeDtypeStruct((N_DEV, n), F32), compiler_params=_params(1),
    )(sc_all, w_ada_shard, b_shard)


def _norm_mod(x, gain, shift, scale):
    t, d = x.shape
    tm = _tile(t, 512)

    def body(x_ref, g_ref, sh_ref, sc_ref, o_ref):
        xv = x_ref[...]
        o_ref[...] = ((xv * _rms(xv) * g_ref[...]) * (1.0 + sc_ref[...]) + sh_ref[...]).astype(BF16)

    vec = pl.BlockSpec((1, d), lambda i: (0, 0))
    return pl.pallas_call(
        body, name="norm_mod", grid=(t // tm,),
        in_specs=[pl.BlockSpec((tm, d), lambda i: (i, 0)), vec, vec, vec],
        out_specs=pl.BlockSpec((tm, d), lambda i: (i, 0)), out_shape=jax.ShapeDtypeStruct((t, d), BF16),
        compiler_params=_params(1),
    )(x, gain, shift, scale)


def _chunk_masks():
    row = lax.broadcasted_iota(jnp.int32, (HEAD, HEAD), 0)
    col = lax.broadcasted_iota(jnp.int32, (HEAD, HEAD), 1)
    same = (row // A_CHUNK) == (col // A_CHUNK)
    return same & (col <= row), same & (col >= row)


def _ones(mask):
    return jnp.where(mask, 1.0, 0.0).astype(BF16)


def _dot_split(ones_bf16, x):
    hi = x.astype(BF16)
    lo = (x - hi.astype(F32)).astype(BF16)
    return _dot(ones_bf16, hi) + _dot(ones_bf16, lo)


def _hgrn_block(direction, f, lb, cum2):
    sf = _sigmoid(f)
    big_f = lb + (1.0 - lb) * sf
    k = (1.0 - lb) * (1.0 - sf)
    lf = jnp.log(big_f)
    both = _dot_split(cum2, lf)
    cf, cr = both[:HEAD], both[HEAD:]
    b, rest = (cf, cr - lf) if direction == 0 else (cr, cf - lf)
    return k, sf, big_f, jnp.exp(b), jnp.exp(-b), jnp.exp(rest)


def _hgrn_fwd(proj, lb, g_norm, width, exchange):
    t = proj.shape[0]
    heads = width // HEAD
    nb, nc = t // HEAD, t // A_CHUNK
    q_scale = HEAD ** -0.5
    xin, xout = len(exchange.arrays), len(exchange.out_shapes)

    def body(q_ref, ffw_ref, fbw_ref, v_ref, og_ref, lb_ref, g_ref, *rest):
        xin_refs, rest = rest[:xin], rest[xin:]
        outa_ref, osum_ref = rest[:2]
        xout_refs, rest = rest[2:2 + xout], rest[2 + xout:]
        qd_s, ke_s, dc_s, o_s, st_s = rest[:5]
        sem_refs = rest[5:]
        h = pl.program_id(0)

        @pl.when(h == 0)
        def _():
            exchange.start(xin_refs, xout_refs, sem_refs)

        tril, triu = _chunk_masks()
        cum2 = jnp.concatenate([_ones(tril), _ones(triu)], axis=0)
        f_refs = (ffw_ref, fbw_ref)

        def phase_a(i, carry):
            rows = pl.ds(pl.multiple_of(i * HEAD, HEAD), HEAD)
            qv, vv = q_ref[rows, :] * q_scale, v_ref[rows, :].astype(BF16)
            for d in range(2):
                k, _, _, eb, enb, erest = _hgrn_block(d, f_refs[d][rows, :], lb_ref[d:d + 1, :], cum2)
                qd, kd = qv * eb, k * enb
                att = jnp.where(tril if d == 0 else triu, _bdot(qd, kd, _NT), 0.0)
                o_s[d, rows, :] = _bdot(att, vv)
                qd_s[d, rows, :] = qd.astype(BF16)
                ke_s[d, rows, :] = (k * erest).astype(BF16)
                dc_s[d, rows, :] = eb * erest
            return carry

        lax.fori_loop(0, nb, phase_a, 0, unroll=2)
        st_s[...] = jnp.zeros_like(st_s)

        def phase_b(n, carry):
            for d in range(2):
                c = n if d == 0 else nc - 1 - n
                rows = pl.ds(pl.multiple_of(c * A_CHUNK, A_CHUNK), A_CHUNK)
                st = st_s[d]
                o_s[d, rows, :] += _dot(qd_s[d, rows, :], st.astype(BF16), _NT)
                decay = dc_s[d, pl.ds(pl.multiple_of(c * A_CHUNK, A_CHUNK), 1), :]
                st_s[d] = st * decay + _dot(v_ref[rows, :].astype(BF16), ke_s[d, rows, :], _TN)
            return carry

        lax.fori_loop(0, nc, phase_b, 0, unroll=4)

        def phase_c(i, carry):
            rows = pl.ds(pl.multiple_of(i * HEAD, HEAD), HEAD)
            o = o_s[0, rows, :] + o_s[1, rows, :]
            osum_ref[rows, :] = o
            og = og_ref[rows, :]
            outa_ref[rows, :] = (o * _rms(o) * g_ref[...] * (og * _sigmoid(og))).astype(BF16)
            return carry

        lax.fori_loop(0, nb, phase_c, 0)

        @pl.when(h == heads - 1)
        def _():
            exchange.finish(xin_refs, xout_refs, sem_refs)

    def col(p):
        return pl.BlockSpec((t, HEAD), lambda h: (0, p * heads + h))

    any_spec = pl.BlockSpec(memory_space=pl.ANY)
    results = pl.pallas_call(
        body, name="hgrn_fwd", grid=(heads,),
        in_specs=[col(0), col(1), col(2), col(3), col(4),
                  pl.BlockSpec((2, HEAD), lambda h: (0, h)), pl.BlockSpec((1, HEAD), lambda h: (0, 0))] + [any_spec] * xin,
        out_specs=[pl.BlockSpec((t, HEAD), lambda h: (0, h)), pl.BlockSpec((t, HEAD), lambda h: (0, h))] + [any_spec] * xout,
        out_shape=[jax.ShapeDtypeStruct((t, width), BF16), jax.ShapeDtypeStruct((t, width), F32)] + list(exchange.out_shapes),
        scratch_shapes=[pltpu.VMEM((2, t, HEAD), BF16), pltpu.VMEM((2, t, HEAD), BF16), pltpu.VMEM((2, t, HEAD), F32),
                        pltpu.VMEM((2, t, HEAD), F32), pltpu.VMEM((2, HEAD, HEAD), F32)] + list(exchange.sems),
        compiler_params=_params(1),
    )(proj, proj, proj, proj, proj, lb, g_norm, *exchange.arrays)
    return results[0], results[1], results[2:]


def _sgu_core(u_pre, v_pre, g_v, ws_ref, bst):
    u, du = _gelu_and_grad(u_pre)
    v, dv = _gelu_and_grad(v_pre)
    mu = jnp.mean(v, axis=-1, keepdims=True)
    dlt = v - mu
    rstd = lax.rsqrt(jnp.mean(dlt * dlt, axis=-1, keepdims=True) + EPS)
    vhat = dlt * rstd
    vn = vhat * g_v
    groups = vn.shape[1] // HEAD
    cols = []
    for g in range(groups):
        vm_g = _bdot(ws_ref[g], vn[:, g * HEAD:(g + 1) * HEAD]) + bst[:, g:g + 1]
        cols.append(vm_g)
    return u, du, dv, vhat, rstd, vn, jnp.concatenate(cols, axis=1)


def _sgu_fwd(proj, g_v, w_s, bst, width, z_block):
    t = proj.shape[0]

    def body(u_ref, v_ref, g_ref, ws_ref, bst_ref, o_ref):
        u, _, _, _, _, _, vm = _sgu_core(u_ref[...], v_ref[...], g_ref[...], ws_ref, bst_ref[...])
        o_ref[...] = (u * vm).astype(BF16)

    groups = width // HEAD
    return pl.pallas_call(
        body, name="sgu_fwd", grid=(t // HEAD,),
        in_specs=[pl.BlockSpec((HEAD, width), lambda i: (i, z_block)), pl.BlockSpec((HEAD, width), lambda i: (i, z_block + 1)),
                  pl.BlockSpec((1, width), lambda i: (0, 0)), pl.BlockSpec((groups, HEAD, HEAD), lambda i: (0, 0, 0)),
                  pl.BlockSpec((HEAD, groups), lambda i: (0, 0))],
        out_specs=pl.BlockSpec((HEAD, width), lambda i: (i, 0)),
        out_shape=jax.ShapeDtypeStruct((t, width), BF16), compiler_params=_params(1),
    )(proj, proj, g_v, w_s, bst)


def _sgu_bwd(proj, dout_b, dproj, g_v, w_s, w_st, bst, width, z_block):
    t = proj.shape[0]
    groups = width // HEAD
    nblk = t // HEAD

    def body(u_ref, v_ref, do_ref, g_ref, ws_ref, wst_ref, bst_ref, dproj_hbm,
             dz_ref, dg_ref, dws_ref, dbst_ref, res_s):
        i, p = pl.program_id(0), pl.program_id(1)

        @pl.when((i == 0) & (p == 0))
        def _():
            dg_ref[...] = jnp.zeros_like(dg_ref)
            dws_ref[...] = jnp.zeros_like(dws_ref)
            dbst_ref[...] = jnp.zeros_like(dbst_ref)

        @pl.when(p == 0)
        def _():
            g_v = g_ref[...]
            u, du, dv, vhat, rstd, vn, vm = _sgu_core(u_ref[...], v_ref[...], g_v, ws_ref, bst_ref[...])
            dout = do_ref[...].astype(F32)
            res_s[0] = (dout * vm * du).astype(BF16)
            dvm = dout * u
            dvn_cols = []
            for g in range(groups):
                sl = slice(g * HEAD, (g + 1) * HEAD)
                dvm_g = dvm[:, sl]
                dbst_ref[:, g:g + 1] += jnp.sum(dvm_g, axis=1, keepdims=True)
                dws_ref[g] += _bdot(dvm_g, vn[:, sl], _NT)
                dvn_cols.append(_bdot(wst_ref[g], dvm_g))
            dvn = jnp.concatenate(dvn_cols, axis=1)
            dg_ref[...] += _colsum(dvn * vhat)
            dvh = dvn * g_v
            dvg = rstd * (dvh - jnp.mean(dvh, axis=-1, keepdims=True)
                          - vhat * jnp.mean(dvh * vhat, axis=-1, keepdims=True))
            res_s[1] = (dvg * dv).astype(BF16)

        dz_ref[...] = res_s[p]

    n_in = dproj.shape[1]
    return pl.pallas_call(
        body, name="sgu_bwd", grid=(nblk, 2),
        in_specs=[pl.BlockSpec((HEAD, width), lambda i, p: (i, z_block)),
                  pl.BlockSpec((HEAD, width), lambda i, p: (i, z_block + 1)),
                  pl.BlockSpec((HEAD, width), lambda i, p: (i, 0)),
                  pl.BlockSpec((1, width), lambda i, p: (0, 0)),
                  pl.BlockSpec((groups, HEAD, HEAD), lambda i, p: (0, 0, 0)),
                  pl.BlockSpec((groups, HEAD, HEAD), lambda i, p: (0, 0, 0)),
                  pl.BlockSpec((HEAD, groups), lambda i, p: (0, 0)),
                  pl.BlockSpec(memory_space=pl.ANY)],
        out_specs=[pl.BlockSpec((HEAD, width), lambda i, p: (i, z_block + p)),
                   pl.BlockSpec((1, width), lambda i, p: (0, 0)),
                   pl.BlockSpec((groups, HEAD, HEAD), lambda i, p: (0, 0, 0)),
                   pl.BlockSpec((HEAD, groups), lambda i, p: (0, 0))],
        out_shape=[jax.ShapeDtypeStruct((t, n_in), BF16), jax.ShapeDtypeStruct((1, width), F32),
                   jax.ShapeDtypeStruct((groups, HEAD, HEAD), F32), jax.ShapeDtypeStruct((HEAD, groups), F32)],
        scratch_shapes=[pltpu.VMEM((2, HEAD, width), BF16)],
        input_output_aliases={7: 0},
        compiler_params=_params(2),
    )(proj, proj, dout_b, g_v, w_s, w_st, bst, dproj)


def _hgrn_bwd(proj, osum, dout_a, dproj, lb, g_norm, width, exchange):
    t = proj.shape[0]
    heads = width // HEAD
    nb = t // HEAD
    cpb = HEAD // A_CHUNK
    q_scale = HEAD ** -0.5
    xin, xout = len(exchange.arrays), len(exchange.out_shapes)

    def body(q_ref, ffw_ref, fbw_ref, v_ref, og_ref, osum_ref, douta_ref, lb_ref, g_ref, dproj_hbm, *rest):
        xin_refs, rest = rest[:xin], rest[xin:]
        out_ref, dgh_ref, dlb_ref = rest[:3]
        xout_refs, rest = rest[3:3 + xout], rest[3 + xout:]
        do_s, dq_s, dv_s, res_s, ck_s, st_s, gt_s = rest[:7]
        sem_refs = rest[7:]
        h, p = pl.program_id(0), pl.program_id(1)
        f_refs = (ffw_ref, fbw_ref)

        @pl.when((h == 0) & (p == 0))
        def _():
            exchange.start(xin_refs, xout_refs, sem_refs)

        @pl.when(p == 0)
        def _():
            tril, triu = _chunk_masks()
            cum2 = jnp.concatenate([_ones(tril), _ones(triu)], axis=0)
            g_row = g_ref[...]

            def pass_norm(i, dgh):
                rows = pl.ds(pl.multiple_of(i * HEAD, HEAD), HEAD)
                o = osum_ref[rows, :]
                r = _rms(o)
                oh = o * r
                og = og_ref[rows, :]
                sg = _sigmoid(og)
                dout = douta_ref[rows, :].astype(F32)
                don = dout * (og * sg)
                res_s[4, rows, :] = (dout * (oh * g_row) * (sg * (1.0 + og * (1.0 - sg)))).astype(BF16)
                doh = don * g_row
                do_s[rows, :] = r * (doh - oh * jnp.mean(doh * oh, axis=-1, keepdims=True))
                dq_s[rows, :] = jnp.zeros((HEAD, HEAD), F32)
                dv_s[rows, :] = jnp.zeros((HEAD, HEAD), F32)
                return dgh + _colsum(don * oh)

            dgh_ref[...] = lax.fori_loop(0, nb, pass_norm, jnp.zeros((1, HEAD), F32))

            def chunk_order(d):
                return range(cpb) if d == 0 else range(cpb - 1, -1, -1)

            def states_of_block(d, blk, st, k_e, e_b, v_b):
                befores = {}
                for j in chunk_order(d):
                    sl = slice(j * A_CHUNK, (j + 1) * A_CHUNK)
                    befores[j] = st
                    st = st * e_b[j * A_CHUNK:j * A_CHUNK + 1, :] + _bdot(v_b[sl, :], k_e[sl, :], _TN)
                return befores, st

            st_s[...] = jnp.zeros_like(st_s)

            def pass_states(it, carry):
                for d in range(2):
                    blk = it if d == 0 else nb - 1 - it
                    rows = pl.ds(pl.multiple_of(blk * HEAD, HEAD), HEAD)
                    k, _, _, eb, _, erest = _hgrn_block(d, f_refs[d][rows, :], lb_ref[d:d + 1, :], cum2)
                    ck_s[d, blk] = st_s[d]
                    _, st_s[d] = states_of_block(d, blk, st_s[d], k * erest, eb * erest, v_ref[rows, :])
                return carry

            lax.fori_loop(0, nb, pass_states, 0, unroll=2)
            gt_s[...] = jnp.zeros_like(gt_s)

            def pass_back(it, dlb):
                new = []
                for d in range(2):
                    blk = nb - 1 - it if d == 0 else it
                    rows = pl.ds(pl.multiple_of(blk * HEAD, HEAD), HEAD)
                    mask = tril if d == 0 else triu
                    lb_d = lb_ref[d:d + 1, :]
                    fv = f_refs[d][rows, :]
                    qh = q_ref[rows, :] * q_scale
                    vv = v_ref[rows, :]
                    k, sf, big_f, eb, enb, erest = _hgrn_block(d, fv, lb_d, cum2)
                    qd, kd, ke, e_big = qh * eb, k * enb, k * erest, eb * erest
                    do = do_s[rows, :]
                    att = jnp.where(mask, _bdot(qd, kd, _NT), 0.0)
                    datt = jnp.where(mask, _bdot(do, vv, _NT), 0.0)
                    dv = _bdot(att, do, _TN)
                    dqd = _bdot(datt, kd)
                    dkd = _bdot(datt, qd, _TN)
                    befores, after = states_of_block(d, blk, ck_s[d, blk], ke, e_big, vv)
                    order = list(chunk_order(d))
                    afters = {j: (befores[order[n + 1]] if n + 1 < cpb else after) for n, j in enumerate(order)}
                    gt = gt_s[d]
                    dqd_i, dv_i, dke, carry_rows = {}, {}, {}, {}
                    for j in reversed(order):
                        sl = slice(j * A_CHUNK, (j + 1) * A_CHUNK)
                        dqd_i[j] = _bdot(do[sl, :], befores[j])
                        dv_i[j] = _bdot(ke[sl, :], gt, _NT)
                        dke[j] = _bdot(vv[sl, :], gt)
                        carry_rows[j] = jnp.broadcast_to(_colsum(gt * afters[j]), (A_CHUNK, HEAD))
                        gt = gt * e_big[j * A_CHUNK:j * A_CHUNK + 1, :] + _bdot(do[sl, :], qd[sl, :], _TN)
                    gt_s[d] = gt
                    cat = lambda parts: jnp.concatenate([parts[j] for j in range(cpb)], axis=0)
                    dqh = (dqd + cat(dqd_i)) * eb
                    dk = dkd * enb + cat(dke) * erest
                    tt = qh * dqh - k * dk
                    cum = triu if d == 0 else tril
                    dlf = _dot_split(_ones(cum), tt) + cat(carry_rows)
                    common = dlf / big_f - dk
                    res_s[1 + d, rows, :] = (k * sf * common).astype(BF16)
                    dq_s[rows, :] += dqh
                    dv_s[rows, :] += dv + cat(dv_i)
                    new.append(_colsum((1.0 - sf) * common))
                return dlb + jnp.concatenate(new, axis=0)

            dlb_ref[...] = lax.fori_loop(0, nb, pass_back, jnp.zeros((2, HEAD), F32), unroll=2)

            def pass_out(i, carry):
                rows = pl.ds(pl.multiple_of(i * HEAD, HEAD), HEAD)
                res_s[0, rows, :] = (dq_s[rows, :] * q_scale).astype(BF16)
                res_s[3, rows, :] = dv_s[rows, :].astype(BF16)
                return carry

            lax.fori_loop(0, nb, pass_out, 0)

        out_ref[...] = res_s[p]

        @pl.when((h == heads - 1) & (p == 4))
        def _():
            exchange.finish(xin_refs, xout_refs, sem_refs)

    def col(pp):
        return pl.BlockSpec((t, HEAD), lambda h, p: (0, pp * heads + h))

    n_in = dproj.shape[1]
    any_spec = pl.BlockSpec(memory_space=pl.ANY)
    results = pl.pallas_call(
        body, name="hgrn_bwd", grid=(heads, 5),
        in_specs=[col(0), col(1), col(2), col(3), col(4),
                  pl.BlockSpec((t, HEAD), lambda h, p: (0, h)), pl.BlockSpec((t, HEAD), lambda h, p: (0, h)),
                  pl.BlockSpec((2, HEAD), lambda h, p: (0, h)), pl.BlockSpec((1, HEAD), lambda h, p: (0, 0)),
                  any_spec] + [any_spec] * xin,
        out_specs=[pl.BlockSpec((t, HEAD), lambda h, p: (0, p * heads + h)),
                   pl.BlockSpec((None, 1, HEAD), lambda h, p: (h, 0, 0)),
                   pl.BlockSpec((2, HEAD), lambda h, p: (0, h))] + [any_spec] * xout,
        out_shape=[jax.ShapeDtypeStruct((t, n_in), BF16), jax.ShapeDtypeStruct((heads, 1, HEAD), F32),
                   jax.ShapeDtypeStruct((2, width), F32)] + list(exchange.out_shapes),
        scratch_shapes=[pltpu.VMEM((t, HEAD), F32), pltpu.VMEM((t, HEAD), F32), pltpu.VMEM((t, HEAD), F32),
                        pltpu.VMEM((5, t, HEAD), BF16), pltpu.VMEM((2, nb, HEAD, HEAD), F32),
                        pltpu.VMEM((2, HEAD, HEAD), F32), pltpu.VMEM((2, HEAD, HEAD), F32)] + list(exchange.sems),
        input_output_aliases={9: 0},
        compiler_params=_params(2),
    )(proj, proj, proj, proj, proj, osum, dout_a, lb, g_norm, dproj, *exchange.arrays)
    return results[0], results[1], results[2], results[3:]


def _adamw(w, g, m, v):
    m = ADAM_B1 * m + (1.0 - ADAM_B1) * g
    v = ADAM_B2 * v + (1.0 - ADAM_B2) * (g * g)
    m_hat = m / (1.0 - ADAM_B1 ** ADAM_STEP)
    v_hat = v / (1.0 - ADAM_B2 ** ADAM_STEP)
    delta = -ADAM_LR * (m_hat / (jnp.sqrt(v_hat) + ADAM_EPS) + ADAM_WD * w)
    return delta, m, v


def _adamw_big(name, me, w, m, v, g_full, landing, axis):
    r, c = w.shape
    tr = _tile(r, 128)

    def body(me_ref, w_ref, m_ref, v_ref, g_ref, l_ref, og_ref, od_ref, om_ref, ov_ref):
        g = g_ref[...]
        for s in range(N_DEV - 1):
            g = g + l_ref[s].astype(F32)
        og_ref[...] = g
        od_ref[...], om_ref[...], ov_ref[...] = _adamw(w_ref[...], g, m_ref[...], v_ref[...])

    shard = pl.BlockSpec((tr, c), lambda i, me_ref: (i, 0))
    if axis == 1:
        own = pl.BlockSpec((tr, c), lambda i, me_ref: (i, me_ref[0]))
    else:
        own = pl.BlockSpec((tr, c), lambda i, me_ref: (me_ref[0] * (r // tr) + i, 0))
    grid_spec = pltpu.PrefetchScalarGridSpec(
        num_scalar_prefetch=1, grid=(r // tr,),
        in_specs=[shard, shard, shard, own, pl.BlockSpec((N_DEV - 1, tr, c), lambda i, me_ref: (0, i, 0))],
        out_specs=[shard] * 4)
    return pl.pallas_call(
        body, name=name, grid_spec=grid_spec, out_shape=[jax.ShapeDtypeStruct((r, c), F32)] * 4,
        compiler_params=_params(1),
    )(me, w, m, v, g_full, landing)


def _adamw_ada(sct, dmod_mine, w, m, v):
    d, n = w.shape
    tr = _tile(d, 256)

    def body(s_ref, dm_ref, w_ref, m_ref, v_ref, og_ref, od_ref, om_ref, ov_ref):
        g = _dot(s_ref[...], dm_ref[...], precision=HIGHEST)
        og_ref[...] = g
        od_ref[...], om_ref[...], ov_ref[...] = _adamw(w_ref[...], g, m_ref[...], v_ref[...])

    blk = pl.BlockSpec((tr, n), lambda i: (i, 0))
    return pl.pallas_call(
        body, name="adamw_ada", grid=(d // tr,),
        in_specs=[pl.BlockSpec((tr, N_DEV), lambda i: (i, 0)), pl.BlockSpec((N_DEV, n), lambda i: (0, 0)), blk, blk, blk],
        out_specs=[blk] * 4, out_shape=[jax.ShapeDtypeStruct((d, n), F32)] * 4, compiler_params=_params(1),
    )(sct, dmod_mine, w, m, v)


def _adamw_small(gathered, w, m, v):
    def body(g_ref, w_ref, m_ref, v_ref, og_ref, od_ref, om_ref, ov_ref):
        g = g_ref[0]
        for s in range(1, N_DEV):
            g = g + g_ref[s]
        og_ref[...] = g
        od_ref[...], om_ref[...], ov_ref[...] = _adamw(w_ref[...], g, m_ref[...], v_ref[...])

    return pl.pallas_call(
        body, name="adamw_small", out_shape=[jax.ShapeDtypeStruct(w.shape, F32)] * 4,
        compiler_params=pltpu.CompilerParams(vmem_limit_bytes=VMEM_LIMIT),
    )(gathered, w, m, v)


def _adamw_lb(dlb_mine, lb_logits, m, v):
    def body(d_ref, l_ref, m_ref, v_ref, og_ref, od_ref, om_ref, ov_ref):
        dlb = d_ref[0]
        for s in range(1, N_DEV):
            dlb = dlb + d_ref[s]
        for dr in range(2):
            lb = _sigmoid(l_ref[dr][0:1, :] - l_ref[dr][1:2, :])
            d0 = dlb[dr:dr + 1] * lb * (1.0 - lb)
            g = jnp.concatenate([d0, -d0], axis=0)
            og_ref[dr] = g
            od_ref[dr], om_ref[dr], ov_ref[dr] = _adamw(l_ref[dr], g, m_ref[dr], v_ref[dr])

    return pl.pallas_call(body, name="adamw_lb", out_shape=[jax.ShapeDtypeStruct(lb_logits.shape, F32)] * 4,
                          )(dlb_mine, lb_logits, m, v)


def _rows(a, pad_to=8):
    flat = a.reshape(-1, LANE)
    pad = (-flat.shape[0]) % pad_to
    return jnp.pad(flat, ((0, pad), (0, 0))) if pad else flat


def kernel(x, c, w_ada, b_ada, g_pre_mix, g_post_mix, g_pre_ffn, g_post_ffn, w_in, lb_logits, g_hgrn_norm, w_a_out, g_sgu_norm, w_spatial, b_spatial, w_b_out, w_o, w_ff1, w_ff2, loss_target, m_w_ada, m_b_ada, m_g_pre_mix, m_g_post_mix, m_g_pre_ffn, m_g_post_ffn, m_w_in, m_lb_logits, m_g_hgrn_norm, m_w_a_out, m_g_sgu_norm, m_w_spatial, m_b_spatial, m_w_b_out, m_w_o, m_w_ff1, m_w_ff2, v_w_ada, v_b_ada, v_g_pre_mix, v_g_post_mix, v_g_pre_ffn, v_g_post_ffn, v_w_in, v_lb_logits, v_g_hgrn_norm, v_w_a_out, v_g_sgu_norm, v_w_spatial, v_b_spatial, v_w_b_out, v_w_o, v_w_ff1, v_w_ff2):
    t, d = x.shape[1], x.shape[2]
    n_in = w_in.shape[2] * N_DEV
    width = (n_in - 2 * d) // 7
    heads = width // HEAD
    assert heads == N_DEV and width % LANE == 0
    d_ff = w_ff1.shape[2] * N_DEV
    n_ada = w_ada.shape[2]
    me = _dev_index()
    me_arr = me.reshape(1).astype(jnp.int32)
    x2, tgt = x[0], loss_target[0]

    big = [w_in[0], w_a_out[0], w_b_out[0], w_o[0], w_ff1[0], w_ff2[0]]
    big_axes = [1, 1, 1, 0, 1, 0]
    big_names = ["w_in", "w_a_out", "w_b_out", "w_o", "w_ff1", "w_ff2"]
    shards16 = [_cast_bf16("cast_" + nm, w) for nm, w in zip(big_names, big)]
    wf_in, = _run_exchange("gather_w_in", _gather_plan(shards16[:1], big_axes[:1]))
    gather_rest = _gather_plan(shards16[1:], big_axes[1:])

    c_rows = d // LANE
    small = _all_gather_small("gather_c_lb", _prep_small(c[0:1], lb_logits))
    sc_all = small[:, :c_rows, :].reshape(N_DEV, d)
    lb = jnp.transpose(small[:, c_rows:c_rows + 2, :], (1, 0, 2)).reshape(2, width)
    b_shard = lax.dynamic_slice_in_dim(b_ada, me * n_ada, n_ada, axis=1)
    mod_sh = _mod_shard(sc_all, w_ada[0], b_shard)
    mod_all = _all_gather_small("gather_mod", _rows(mod_sh))
    mod_all = mod_all[:, :N_DEV * n_ada // LANE, :].reshape(N_DEV, N_DEV, n_ada)
    mod6 = lax.dynamic_index_in_dim(mod_all, me, axis=1, keepdims=False).reshape(N_MOD, d)
    sh1, sc1, gt1, sh2, sc2, gt2 = [mod6[i:i + 1] for i in range(N_MOD)]

    a1 = _norm_mod(x2, g_pre_mix, sh1, sc1)
    tm = _tile(t, 512)

    def store_f32(acc, i, j, extra_refs, out_refs, rows):
        out_refs[0][...] = acc

    tn_in = _tile(n_in, 1024)
    proj, = _mm("proj", a1, wf_in, _NN, t, n_in, d, tm, tn_in, d, [],
                [(jax.ShapeDtypeStruct((t, n_in), F32), (tm, tn_in), lambda i, j: (i, j))], store_f32)

    out_a, osum, (wf_a, wf_b, wf_o, wf_1, wf_2) = _hgrn_fwd(proj, lb, g_hgrn_norm, width, gather_rest)
    z_block = 5
    bst = b_spatial[0].T
    out_b = _sgu_fwd(proj, g_sgu_norm, w_spatial[0], bst, width, z_block)

    tn_d = _tile(d, 512)
    blk_d = ((tm, tn_d), lambda i, j: (i, j))
    y_a, = _mm("y_a", out_a, wf_a, _NN, t, d, width, tm, tn_d, width, [],
               [(jax.ShapeDtypeStruct((t, d), F32),) + blk_d], store_f32)
    ga_blk = (5 * width + 2 * width) // tn_d
    gb_blk = ga_blk + d // tn_d

    def merge(acc, i, j, extra_refs, out_refs, rows):
        ga, gb, ya = extra_refs
        out_refs[0][...] = acc
        out_refs[1][...] = (_sigmoid(ga[...]) * ya[...] + _sigmoid(gb[...]) * acc).astype(BF16)

    y_b, merged = _mm("y_b_merge", out_b, wf_b, _NN, t, d, width, tm, tn_d, width,
                      [(proj, (tm, tn_d), lambda i, j: (i, ga_blk + j)), (proj, (tm, tn_d), lambda i, j: (i, gb_blk + j)),
                       (y_a,) + blk_d],
                      [(jax.ShapeDtypeStruct((t, d), F32),) + blk_d, (jax.ShapeDtypeStruct((t, d), BF16),) + blk_d], merge)

    tr = _tile(t, 256)
    rc = 64 if tr % 64 == 0 else None
    row_d = ((tr, d), lambda i, j: (i, 0))
    vec_d = ((1, d), lambda i, j: (0, 0))

    def post_mix(acc, i, j, extra_refs, out_refs, rows):
        x_r, gt1_r, g2_r, g3_r, sc2_r, sh2_r = extra_refs
        h1 = x_r[rows, :] + gt1_r[...] * (acc * _rms(acc) * g2_r[...])
        out_refs[0][rows, :] = acc
        out_refs[1][rows, :] = h1
        out_refs[2][rows, :] = ((h1 * _rms(h1) * g3_r[...]) * (1.0 + sc2_r[...]) + sh2_r[...]).astype(BF16)

    mo, h1, a2 = _mm("w_o_post_mix", merged, wf_o, _NN, t, d, d, tr, d, d,
                     [(x2,) + row_d, (gt1,) + vec_d, (g_post_mix,) + vec_d, (g_pre_ffn,) + vec_d, (sc2,) + vec_d, (sh2,) + vec_d],
                     [(jax.ShapeDtypeStruct((t, d), F32),) + row_d, (jax.ShapeDtypeStruct((t, d), F32),) + row_d,
                      (jax.ShapeDtypeStruct((t, d), BF16),) + row_d], post_mix, row_chunk=rc)

    tn_f = _tile(d_ff, 1024)
    blk_f = ((tm, tn_f), lambda i, j: (i, j))

    def relu_sq(acc, i, j, extra_refs, out_refs, rows):
        r = jnp.maximum(acc, 0.0)
        out_refs[0][...] = acc.astype(BF16)
        out_refs[1][...] = (r * r).astype(BF16)

    hff, act = _mm("ff1", a2, wf_1, _NN, t, d_ff, d, tm, tn_f, d, [],
                   [(jax.ShapeDtypeStruct((t, d_ff), BF16),) + blk_f, (jax.ShapeDtypeStruct((t, d_ff), BF16),) + blk_f], relu_sq)

    sums_d = ((8, d), lambda i, j: (0, 0))

    def zero_first(sums_r, i, rows):
        if rows.start in (None, 0):
            @pl.when(i == 0)
            def _():
                sums_r[...] = jnp.zeros_like(sums_r)

    def loss_head(acc, i, j, extra_refs, out_refs, rows):
        h1_r, tgt_r, gt2_r, g4_r = extra_refs
        dy_r, dff_r, sums_r = out_refs
        r4 = _rms(acc)
        ffn = acc * r4
        n4 = ffn * g4_r[...]
        err = h1_r[rows, :] + gt2_r[...] * n4 - tgt_r[rows, :]
        dy = err * (1.0 / d)
        dy_r[rows, :] = dy
        dn4 = dy * gt2_r[...]
        dffn = dn4 * g4_r[...]
        dff_r[rows, :] = (r4 * (dffn - ffn * jnp.mean(dffn * ffn, axis=-1, keepdims=True))).astype(BF16)
        zero_first(sums_r, i, rows)

        sums_r[0:1, :] += _colsum(err * err)
        sums_r[1:2, :] += _colsum(dy * n4)
        sums_r[2:3, :] += _colsum(dn4 * ffn)

    tk_f = _tile(d_ff, 2048)
    dy, dff, sums_f = _mm("ff2_loss", act, wf_2, _NN, t, d, d_ff, tr, d, tk_f,
                          [(h1,) + row_d, (tgt,) + row_d, (gt2,) + vec_d, (g_post_ffn,) + vec_d],
                          [(jax.ShapeDtypeStruct((t, d), F32),) + row_d, (jax.ShapeDtypeStruct((t, d), BF16),) + row_d,
                           (jax.ShapeDtypeStruct((8, d), F32),) + sums_d], loss_head, row_chunk=rc)
    loss = lax.psum((0.5 / d) * jnp.sum(sums_f[0]), ("x", "y", "c"))

    def relu_sq_bwd(acc, i, j, extra_refs, out_refs, rows):
        out_refs[0][...] = (acc * (2.0 * jnp.maximum(extra_refs[0][...].astype(F32), 0.0))).astype(BF16)

    dhff, = _mm("d_hff", dff, wf_2, _NT, t, d_ff, d, tm, tn_f, d, [(hff,) + blk_f],
                [(jax.ShapeDtypeStruct((t, d_ff), BF16),) + blk_f], relu_sq_bwd)
    gw_ff2, gw_ff2_16 = _grad_w("grad_w_ff2", act, dff)
    gw_ff1, gw_ff1_16 = _grad_w("grad_w_ff1", a2, dhff)

    def pre_ffn_bwd(acc, i, j, extra_refs, out_refs, rows):
        h1_r, dy_r, mo_r, sc2_r, g3_r, gt1_r, g2_r = extra_refs
        dh1_r, dmo_r, sums_r = out_refs
        h1v = h1_r[rows, :]
        r3 = _rms(h1v)
        h1n = h1v * r3
        dn3 = acc * (1.0 + sc2_r[...])
        dh1n = dn3 * g3_r[...]
        dh1 = dy_r[rows, :] + r3 * (dh1n - h1n * jnp.mean(dh1n * h1n, axis=-1, keepdims=True))
        dh1_r[rows, :] = dh1
        mov = mo_r[rows, :]
        r2 = _rms(mov)
        mon = mov * r2
        dn2 = dh1 * gt1_r[...]
        dmon = dn2 * g2_r[...]
        dmo_r[rows, :] = (r2 * (dmon - mon * jnp.mean(dmon * mon, axis=-1, keepdims=True))).astype(BF16)
        zero_first(sums_r, i, rows)

        sums_r[0:1, :] += _colsum(acc)
        sums_r[1:2, :] += _colsum(acc * (h1n * g3_r[...]))
        sums_r[2:3, :] += _colsum(dn3 * h1n)
        sums_r[3:4, :] += _colsum(dh1 * (mon * g2_r[...]))
        sums_r[4:5, :] += _colsum(dn2 * mon)

    dh1, dmo, sums_m = _mm("d_a2_pre_ffn", dhff, wf_1, _NT, t, d, d_ff, tr, d, tk_f,
                           [(h1,) + row_d, (dy,) + row_d, (mo,) + row_d, (sc2,) + vec_d, (g_pre_ffn,) + vec_d,
                            (gt1,) + vec_d, (g_post_mix,) + vec_d],
                           [(jax.ShapeDtypeStruct((t, d), F32),) + row_d, (jax.ShapeDtypeStruct((t, d), BF16),) + row_d,
                            (jax.ShapeDtypeStruct((8, d), F32),) + sums_d], pre_ffn_bwd, row_chunk=rc)
    gw_o, gw_o_16 = _grad_w("grad_w_o", merged, dmo)

    n_j = d // tn_d

    def merge_bwd_body(dmo_ref, wo_ref, ga_ref, gb_ref, ya_ref, yb_ref, dya_ref, dyb_ref, dproj_ref, acc_s):
        g = pl.program_id(2)

        @pl.when(g == 0)
        def _():
            dm = _dot(dmo_ref[...], wo_ref[...], _NT)
            acc_s[...] = dm
            sa = _sigmoid(ga_ref[...])
            dya_ref[...] = (dm * sa).astype(BF16)
            dproj_ref[...] = (dm * ya_ref[...] * sa * (1.0 - sa)).astype(BF16)

        @pl.when(g == 1)
        def _():
            dm = acc_s[...]
            sb = _sigmoid(gb_ref[...])
            dyb_ref[...] = (dm * sb).astype(BF16)
            dproj_ref[...] = (dm * yb_ref[...] * sb * (1.0 - sb)).astype(BF16)

    tile3 = pl.BlockSpec((tm, tn_d), lambda i, j, g: (i, j))
    dy_a, dy_b, dproj = pl.pallas_call(
        merge_bwd_body, name="d_merged", grid=(t // tm, n_j, 2),
        in_specs=[pl.BlockSpec((tm, d), lambda i, j, g: (i, 0)), pl.BlockSpec((tn_d, d), lambda i, j, g: (j, 0)),
                  pl.BlockSpec((tm, tn_d), lambda i, j, g: (i, ga_blk + j)),
                  pl.BlockSpec((tm, tn_d), lambda i, j, g: (i, gb_blk + j)), tile3, tile3],
        out_specs=[tile3, tile3, pl.BlockSpec((tm, tn_d), lambda i, j, g: (i, ga_blk + g * n_j + j))],
        out_shape=[jax.ShapeDtypeStruct((t, d), BF16), jax.ShapeDtypeStruct((t, d), BF16),
                   jax.ShapeDtypeStruct((t, n_in), BF16)],
        scratch_shapes=[pltpu.VMEM((tm, tn_d), F32)], compiler_params=_params(3),
    )(dmo, wf_o, proj, proj, y_a, y_b)

    def store_bf16(acc, i, j, extra_refs, out_refs, rows):
        out_refs[0][...] = acc.astype(BF16)

    tn_w = _tile(width, 512)
    blk_w = ((tm, tn_w), lambda i, j: (i, j))
    dout_a, = _mm("d_out_a", dy_a, wf_a, _NT, t, width, d, tm, tn_w, d, [],
                  [(jax.ShapeDtypeStruct((t, width), BF16),) + blk_w], store_bf16)
    dout_b, = _mm("d_out_b", dy_b, wf_b, _NT, t, width, d, tm, tn_w, d, [],
                  [(jax.ShapeDtypeStruct((t, width), BF16),) + blk_w], store_bf16)
    gw_a, gw_a_16 = _grad_w("grad_w_a_out", out_a, dy_a)
    gw_b, gw_b_16 = _grad_w("grad_w_b_out", out_b, dy_b)

    w_st = jnp.swapaxes(w_spatial[0], 1, 2)
    dproj, dg_sgu, dw_sp, dbst = _sgu_bwd(proj, dout_b, dproj, g_sgu_norm, w_spatial[0], w_st, bst, width, z_block)
    scatter_rest = _scatter_plan([gw_a_16, gw_b_16, gw_o_16, gw_ff1_16, gw_ff2_16], big_axes[1:])
    dproj, dgh_heads, dlb, lands_rest = _hgrn_bwd(proj, osum, dout_a, dproj, lb, g_hgrn_norm, width, scatter_rest)
    gw_in, gw_in_16 = _grad_w("grad_w_in", a1, dproj)

    def pre_mix_bwd(acc, i, j, extra_refs, out_refs, rows):
        x_r, dh1_r, sc1_r, g1_r = extra_refs
        dx_r, sums_r = out_refs
        xv = x_r[rows, :]
        r1 = _rms(xv)
        xn = xv * r1
        dn1 = acc * (1.0 + sc1_r[...])
        dxn = dn1 * g1_r[...]
        dx_r[rows, :] = dh1_r[rows, :] + r1 * (dxn - xn * jnp.mean(dxn * xn, axis=-1, keepdims=True))
        zero_first(sums_r, i, rows)

        sums_r[0:1, :] += _colsum(acc)
        sums_r[1:2, :] += _colsum(acc * (xn * g1_r[...]))
        sums_r[2:3, :] += _colsum(dn1 * xn)

    tk_in = _tile(n_in, 2816)
    (grad_x, sums_x), (land_in,) = _mm(
        "d_a1_pre_mix", dproj, wf_in, _NT, t, d, n_in, tr, d, tk_in,
        [(x2,) + row_d, (dh1,) + row_d, (sc1,) + vec_d, (g_pre_mix,) + vec_d],
        [(jax.ShapeDtypeStruct((t, d), F32),) + row_d, (jax.ShapeDtypeStruct((8, d), F32),) + sums_d],
        pre_mix_bwd, row_chunk=rc, exchange=_scatter_plan([gw_in_16], big_axes[:1]))

    lands = [land_in] + list(lands_rest)
    full32 = [gw_in, gw_a, gw_b, gw_o, gw_ff1, gw_ff2]
    moms = [m_w_in, m_w_a_out, m_w_b_out, m_w_o, m_w_ff1, m_w_ff2]
    vars_ = [v_w_in, v_w_a_out, v_w_b_out, v_w_o, v_w_ff1, v_w_ff2]
    big_out = {}
    for nm, w, mm_, vv_, gf, ld, ax in zip(big_names, big, moms, vars_, full32, lands, big_axes):
        big_out[nm] = [o[None] for o in _adamw_big("adamw_" + nm, me_arr, w, mm_[0], vv_[0], gf, ld, ax)]

    dmod = jnp.concatenate([sums_x[0:2], sums_m[3:4], sums_m[0:2], sums_f[1:2]], axis=0).reshape(N_DEV, n_ada // LANE, LANE)
    ada_rows = -(-(n_ada // LANE) // 8) * 8
    dmod = jnp.pad(dmod, ((0, 0), (0, ada_rows - n_ada // LANE), (0, 0))).reshape(N_DEV * ada_rows, LANE)
    parts = [dmod, _rows(sums_x[2:3]), _rows(sums_m[4:5]), _rows(sums_m[2:3]), _rows(sums_f[2:3]),
             _rows(jnp.sum(dgh_heads, axis=0)), _rows(dg_sgu), _rows(dw_sp), _rows(dbst.T)]
    n_common = sum(p.shape[0] for p in parts)
    payload = jnp.concatenate(parts + [_rows(dlb)], axis=0)
    gathered = _all_gather_small("gather_small_grads", payload)

    dmod_mine = lax.dynamic_slice_in_dim(gathered[:, :N_DEV * ada_rows, :].reshape(N_DEV, N_DEV, ada_rows * LANE),
                                         me, 1, axis=1)[:, 0, :n_ada]
    ada_out = [o[None] for o in _adamw_ada(sc_all.T, dmod_mine, w_ada[0], m_w_ada[0], v_w_ada[0])]

    def pack(b_, g1_, g2_, g3_, g4_, gh_, gs_, ws_, bs_):
        b3 = b_.reshape(N_DEV, n_ada // LANE, LANE)
        b3 = jnp.pad(b3, ((0, 0), (0, ada_rows - n_ada // LANE), (0, 0))).reshape(N_DEV * ada_rows, LANE)
        return jnp.concatenate([b3, _rows(g1_), _rows(g2_), _rows(g3_), _rows(g4_), _rows(gh_), _rows(gs_),
                                _rows(ws_), _rows(bs_)], axis=0)

    small_w = (b_ada, g_pre_mix, g_post_mix, g_pre_ffn, g_post_ffn, g_hgrn_norm, g_sgu_norm, w_spatial, b_spatial)
    small_m = (m_b_ada, m_g_pre_mix, m_g_post_mix, m_g_pre_ffn, m_g_post_ffn, m_g_hgrn_norm, m_g_sgu_norm, m_w_spatial, m_b_spatial)
    small_v = (v_b_ada, v_g_pre_mix, v_g_post_mix, v_g_pre_ffn, v_g_post_ffn, v_g_hgrn_norm, v_g_sgu_norm, v_w_spatial, v_b_spatial)
    packed = _adamw_small(gathered[:, :n_common, :], pack(*small_w), pack(*small_m), pack(*small_v))

    def unpack(slab):
        outs, at = [], 0
        b3 = slab[:N_DEV * ada_rows].reshape(N_DEV, ada_rows, LANE)[:, :n_ada // LANE, :]
        outs.append(b3.reshape(b_ada.shape))
        at = N_DEV * ada_rows
        for ref in small_w[1:]:
            n_el = ref.size
            n_r = -(-(n_el // LANE) // 8) * 8
            outs.append(slab[at:at + n_el // LANE].reshape(ref.shape))
            at += n_r
        return outs

    small_out = [unpack(s) for s in packed]

    dlb_all = gathered[:, n_common:n_common + 2 * heads, :].reshape(N_DEV, 2, heads, LANE)
    dlb_mine = lax.dynamic_index_in_dim(dlb_all, me, axis=2, keepdims=False)
    lb_out = _adamw_lb(dlb_mine, lb_logits, m_lb_logits, v_lb_logits)

    order = ["w_ada", "b_ada", "g_pre_mix", "g_post_mix", "g_pre_ffn", "g_post_ffn", "w_in", "lb_logits", "g_hgrn_norm",
             "w_a_out", "g_sgu_norm", "w_spatial", "b_spatial", "w_b_out", "w_o", "w_ff1", "w_ff2"]
    small_names = ["b_ada", "g_pre_mix", "g_post_mix", "g_pre_ffn", "g_post_ffn", "g_hgrn_norm", "g_sgu_norm", "w_spatial", "b_spatial"]

    def leaf(kind, nm):
        if nm == "w_ada":
            return ada_out[kind]
        if nm == "lb_logits":
            return lb_out[kind]
        if nm in big_out:
            return big_out[nm][kind]
        return small_out[kind][small_names.index(nm)]

    result = [loss, grad_x[None]]
    for kind in range(4):
        result += [leaf(kind, nm) for nm in order]
    return tuple(result)
```

```python
import functools
import math

import jax
import jax.numpy as jnp
from jax import lax
from jax.experimental import pallas as pl
from jax.experimental.pallas import tpu as pltpu

F32 = jnp.float32
BF16 = jnp.bfloat16
MESH = pl.DeviceIdType.MESH
HIGHEST = lax.Precision.HIGHEST

N_DEV = 8
HEAD = 128
A_CHUNK = 32
N_MOD = 6
EPS = 1e-6
LANE = 128
VMEM_LIMIT = 56 * 1024 * 1024

ADAM_LR = 0.001
ADAM_B1 = 0.9
ADAM_B2 = 0.999
ADAM_EPS = 1e-08
ADAM_WD = 0.01
ADAM_STEP = 10

_NN = (((1,), (0,)), ((), ()))
_NT = (((1,), (1,)), ((), ()))
_TN = (((0,), (0,)), ((), ()))


def _dot(a, b, dims=_NN, precision=None):
    return lax.dot_general(a, b, dims, preferred_element_type=F32, precision=precision)


def _bdot(a, b, dims=_NN):
    return _dot(a.astype(BF16), b.astype(BF16), dims)


def _params(n_grid):
    return pltpu.CompilerParams(dimension_semantics=("arbitrary",) * n_grid, vmem_limit_bytes=VMEM_LIMIT)


def _dev_index():
    return lax.axis_index("x") * 4 + lax.axis_index("y") * 2 + lax.axis_index("c")


def _dev_coords(i):
    return (i // 4, (i // 2) % 2, i % 2)


def _sigmoid(x):
    return 1.0 / (1.0 + jnp.exp(-x))


def _erf(x):
    ax = jnp.abs(x)
    t = 1.0 / (1.0 + 0.3275911 * ax)
    poly = ((((1.061405429 * t - 1.453152027) * t + 1.421413741) * t - 0.284496736) * t + 0.254829592) * t
    y = 1.0 - poly * jnp.exp(-ax * ax)
    return jnp.where(x < 0, -y, y)


def _gelu_and_grad(x):
    cdf = 0.5 * (1.0 + _erf(x * (2.0 ** -0.5)))
    pdf = jnp.exp(-0.5 * x * x) * (1.0 / math.sqrt(2.0 * math.pi))
    return x * cdf, cdf + x * pdf


def _rms(x):
    return lax.rsqrt(jnp.mean(x * x, axis=-1, keepdims=True) + EPS)


def _colsum(x):
    return jnp.sum(x, axis=0, keepdims=True)


def _tile(n, want):
    if n <= want:
        return n
    t = (want // LANE) * LANE
    while n % t:
        t -= LANE
    assert t > 0, (n, want)
    return t


def _all_gather_small(name, payload):
    rows = payload.shape[0]

    def body(p_ref, out_ref, send_sems, recv_sems, local_sem):
        me = _dev_index()
        mine = pltpu.make_async_copy(p_ref, out_ref.at[me], local_sem)
        mine.start()
        sends = []
        for r in range(1, N_DEV):
            peer = (me + r) % N_DEV
            cp = pltpu.make_async_remote_copy(
                src_ref=p_ref, dst_ref=out_ref.at[me], send_sem=send_sems.at[r - 1], recv_sem=recv_sems.at[r - 1],
                device_id=_dev_coords(peer), device_id_type=MESH)
            cp.start()
            sends.append(cp)
        for r in range(1, N_DEV):
            src = (me + N_DEV - r) % N_DEV
            pltpu.make_async_remote_copy(
                src_ref=p_ref, dst_ref=out_ref.at[src], send_sem=send_sems.at[r - 1], recv_sem=recv_sems.at[r - 1],
                device_id=_dev_coords(src), device_id_type=MESH).wait_recv()
        for cp in sends:
            cp.wait_send()
        mine.wait()

    return pl.pallas_call(
        body, name=name,
        out_shape=jax.ShapeDtypeStruct((N_DEV, rows, LANE), F32),
        in_specs=[pl.BlockSpec(memory_space=pltpu.VMEM)],
        out_specs=pl.BlockSpec(memory_space=pltpu.VMEM),
        scratch_shapes=[pltpu.SemaphoreType.DMA((N_DEV - 1,)), pltpu.SemaphoreType.DMA((N_DEV - 1,)),
                        pltpu.SemaphoreType.DMA],
        compiler_params=pltpu.CompilerParams(vmem_limit_bytes=VMEM_LIMIT),
    )(payload)


def _region(ref, dev, axis, n):
    start = pl.multiple_of(dev * n, LANE if axis == 1 else 16)
    return ref.at[:, pl.ds(start, n)] if axis == 1 else ref.at[pl.ds(start, n), :]


class _Exchange:
    def __init__(self, arrays, out_shapes, sems, start, finish):
        self.arrays, self.out_shapes, self.sems, self.start, self.finish = arrays, out_shapes, sems, start, finish


def _gather_plan(shards, axes):
    n_w = len(shards)
    fulls = []
    for s, ax in zip(shards, axes):
        shp = (s.shape[0], s.shape[1] * N_DEV) if ax == 1 else (s.shape[0] * N_DEV, s.shape[1])
        fulls.append(jax.ShapeDtypeStruct(shp, BF16))
    widths = [s.shape[ax] for s, ax in zip(shards, axes)]

    def places():
        x, y, c = lax.axis_index("x"), lax.axis_index("y"), lax.axis_index("c")
        chips = [(1 - x, y), (x, 1 - y), (1 - x, 1 - y)]
        return (x, y, c), (x, y, 1 - c), chips

    def index(p):
        return p[0] * 4 + p[1] * 2 + p[2]

    def copy(w, k, s_refs, f_refs, sems, block, to, from_shard):
        send_sems, recv_sems, _ = sems
        dst = _region(f_refs[w], index(block), axes[w], widths[w])
        return pltpu.make_async_remote_copy(
            src_ref=s_refs[w] if from_shard else dst, dst_ref=dst,
            send_sem=send_sems.at[w, k], recv_sem=recv_sems.at[w, k], device_id=to, device_id_type=MESH)

    def local(w, s_refs, f_refs, sems, me):
        return pltpu.make_async_copy(s_refs[w], _region(f_refs[w], index(me), axes[w], widths[w]), sems[2].at[w])

    def start(s_refs, f_refs, sems):
        me, sib, chips = places()
        for w in range(n_w):
            local(w, s_refs, f_refs, sems, me).start()
            copy(w, 0, s_refs, f_refs, sems, me, sib, True).start()
            for j, chip in enumerate(chips):
                copy(w, 1 + j, s_refs, f_refs, sems, me, (*chip, me[2]), True).start()

    def finish(s_refs, f_refs, sems):
        me, sib, chips = places()
        for w in range(n_w):
            for j, chip in enumerate(chips):
                copy(w, 1 + j, s_refs, f_refs, sems, (*chip, me[2]), me, True).wait_recv()
                copy(w, 4 + j, s_refs, f_refs, sems, (*chip, me[2]), sib, False).start()
        for w in range(n_w):
            copy(w, 0, s_refs, f_refs, sems, sib, me, True).wait_recv()
            for j, chip in enumerate(chips):
                copy(w, 4 + j, s_refs, f_refs, sems, (*chip, sib[2]), me, False).wait_recv()
        for w in range(n_w):
            for k in range(N_DEV - 1):
                copy(w, k, s_refs, f_refs, sems, me, sib, True).wait_send()
            local(w, s_refs, f_refs, sems, me).wait()

    sems = [pltpu.SemaphoreType.DMA((n_w, N_DEV - 1)), pltpu.SemaphoreType.DMA((n_w, N_DEV - 1)),
            pltpu.SemaphoreType.DMA((n_w,))]
    return _Exchange(list(shards), fulls, sems, start, finish)


def _scatter_plan(grads, axes):
    n_w = len(grads)
    lands = []
    for g, ax in zip(grads, axes):
        shp = (g.shape[0], g.shape[1] // N_DEV) if ax == 1 else (g.shape[0] // N_DEV, g.shape[1])
        lands.append(jax.ShapeDtypeStruct((N_DEV - 1,) + shp, BF16))
    widths = [ld.shape[1 + ax] for ld, ax in zip(lands, axes)]

    def copy(w, r, g_refs, l_refs, sems, block, to):
        return pltpu.make_async_remote_copy(
            src_ref=_region(g_refs[w], block, axes[w], widths[w]), dst_ref=l_refs[w].at[r - 1],
            send_sem=sems[0].at[w, r - 1], recv_sem=sems[1].at[w, r - 1],
            device_id=_dev_coords(to), device_id_type=MESH)

    def start(g_refs, l_refs, sems):
        me = _dev_index()
        for w in range(n_w):
            for r in range(1, N_DEV):
                owner = (me + r) % N_DEV
                copy(w, r, g_refs, l_refs, sems, owner, owner).start()

    def finish(g_refs, l_refs, sems):
        me = _dev_index()
        for w in range(n_w):
            for r in range(1, N_DEV):
                copy(w, r, g_refs, l_refs, sems, me, (me + N_DEV - r) % N_DEV).wait_recv()
        for w in range(n_w):
            for r in range(1, N_DEV):
                copy(w, r, g_refs, l_refs, sems, me, (me + r) % N_DEV).wait_send()

    sems = [pltpu.SemaphoreType.DMA((n_w, N_DEV - 1)), pltpu.SemaphoreType.DMA((n_w, N_DEV - 1))]
    return _Exchange(list(grads), lands, sems, start, finish)


def _run_exchange(name, plan):
    n_in, n_out = len(plan.arrays), len(plan.out_shapes)

    def body(*refs):
        ins, outs, sems = refs[:n_in], refs[n_in:n_in + n_out], refs[n_in + n_out:]
        plan.start(ins, outs, sems)
        plan.finish(ins, outs, sems)

    any_spec = pl.BlockSpec(memory_space=pl.ANY)
    return pl.pallas_call(
        body, name=name, out_shape=plan.out_shapes,
        in_specs=[any_spec] * n_in, out_specs=[any_spec] * n_out, scratch_shapes=plan.sems,
    )(*plan.arrays)


def _mm(name, a, b, dims, m, n, k, tm, tn, tk, extras, outs, epilogue, row_chunk=None, exchange=None):
    ni, nj, nk = m // tm, n // tn, k // tk
    ne, no = len(extras), len(outs)
    xin = len(exchange.arrays) if exchange else 0
    xout = len(exchange.out_shapes) if exchange else 0
    if dims == _TN:
        a_spec = pl.BlockSpec((tk, tm), lambda i, j, kk: (kk, i))
    else:
        a_spec = pl.BlockSpec((tm, tk), lambda i, j, kk: (i, kk))
    if dims == _NT:
        b_spec = pl.BlockSpec((tn, tk), lambda i, j, kk: (j, kk))
    else:
        b_spec = pl.BlockSpec((tk, tn), lambda i, j, kk: (kk, j))
    chunks = [slice(None)] if row_chunk is None else [slice(r, r + row_chunk) for r in range(0, tm, row_chunk)]

    def lift(index_map):
        return lambda i, j, kk: index_map(i, j)

    def body(a_ref, b_ref, *rest):
        extra_refs, rest = rest[:ne], rest[ne:]
        xin_refs, rest = rest[:xin], rest[xin:]
        out_refs, rest = rest[:no], rest[no:]
        xout_refs, rest = rest[:xout], rest[xout:]
        i, j, kk = pl.program_id(0), pl.program_id(1), pl.program_id(2)
        if exchange:
            sem_refs = rest[1:] if nk > 1 else rest

            @pl.when((i == 0) & (j == 0) & (kk == 0))
            def _():
                exchange.start(xin_refs, xout_refs, sem_refs)

        part = _dot(a_ref[...], b_ref[...], dims)
        if nk == 1:
            for rows in chunks:
                epilogue(part[rows], i, j, extra_refs, out_refs, rows)
        else:
            acc_ref = rest[0]

            @pl.when(kk == 0)
            def _():
                acc_ref[...] = part

            @pl.when(kk > 0)
            def _():
                acc_ref[...] += part

            @pl.when(kk == nk - 1)
            def _():
                for rows in chunks:
                    epilogue(acc_ref[rows, :], i, j, extra_refs, out_refs, rows)

        if exchange:
            @pl.when((i == ni - 1) & (j == nj - 1) & (kk == nk - 1))
            def _():
                exchange.finish(xin_refs, xout_refs, sem_refs)

    any_spec = pl.BlockSpec(memory_space=pl.ANY)
    results = pl.pallas_call(
        body, name=name,
        grid=(ni, nj, nk),
        in_specs=[a_spec, b_spec] + [pl.BlockSpec(bs, lift(im)) for _, bs, im in extras] + [any_spec] * xin,
        out_specs=[pl.BlockSpec(bs, lift(im)) for _, bs, im in outs] + [any_spec] * xout,
        out_shape=[sd for sd, _, _ in outs] + (list(exchange.out_shapes) if exchange else []),
        scratch_shapes=([pltpu.VMEM((tm, tn), F32)] if nk > 1 else []) + (list(exchange.sems) if exchange else []),
        compiler_params=_params(3),
    )(a, b, *[arr for arr, _, _ in extras], *(exchange.arrays if exchange else []))
    return (results[:no], results[no:]) if exchange else results


def _grad_w(name, a, dc, tm=512, tn=1024):
    t, m = a.shape
    n = dc.shape[1]
    tm, tn = _tile(m, tm), _tile(n, tn)

    def epilogue(acc, i, j, extra_refs, out_refs, rows):
        out_refs[0][...] = acc
        out_refs[1][...] = acc.astype(BF16)

    blk = ((tm, tn), lambda i, j: (i, j))
    return _mm(name, a, dc, _TN, m, n, t, tm, tn, t, [],
               [(jax.ShapeDtypeStruct((m, n), F32),) + blk, (jax.ShapeDtypeStruct((m, n), BF16),) + blk], epilogue)


def _cast_bf16(name, w):
    r, c = w.shape
    tr = _tile(r, 256)
    return pl.pallas_call(
        lambda w_ref, o_ref: o_ref.__setitem__(Ellipsis, w_ref[...].astype(BF16)), name=name,
        grid=(r // tr,), in_specs=[pl.BlockSpec((tr, c), lambda i: (i, 0))],
        out_specs=pl.BlockSpec((tr, c), lambda i: (i, 0)), out_shape=jax.ShapeDtypeStruct((r, c), BF16),
        compiler_params=_params(1),
    )(w)


def _prep_small(c_row, lb_logits):
    d = c_row.shape[1]
    rows = d // LANE

    def body(c_ref, l_ref, o_ref):
        cv = c_ref[...]
        o_ref[0:rows, :] = cv * _sigmoid(cv)
        lbs = [_sigmoid(l_ref[dr][0:1, :] - l_ref[dr][1:2, :]) for dr in range(2)]
        o_ref[rows:rows + 8, :] = jnp.concatenate(lbs + [jnp.zeros((6, LANE), F32)], axis=0)

    return pl.pallas_call(
        body, name="prep_small", out_shape=jax.ShapeDtypeStruct((rows + 8, LANE), F32),
    )(c_row.reshape(rows, LANE), lb_logits)


def _mod_shard(sc_all, w_ada_shard, b_shard):
    d, n = w_ada_shard.shape
    tn = _tile(n, 512)

    def body(s_ref, w_ref, b_ref, o_ref):
        o_ref[...] = _dot(s_ref[...], w_ref[...], precision=HIGHEST) + b_ref[...]

    return pl.pallas_call(
        body, name="mod_shard", grid=(n // tn,),
        in_specs=[pl.BlockSpec((N_DEV, d), lambda j: (0, 0)), pl.BlockSpec((d, tn), lambda j: (0, j)),
                  pl.BlockSpec((1, tn), lambda j: (0, j))],
        out_specs=pl.BlockSpec((N_DEV, tn), lambda j: (0, j)),
        out_shape=jax.ShapeDtypeStruct((N_DEV, n), F32), compiler_params=_params(1),
    )(sc_all, w_ada_shard, b_shard)


def _norm_mod(x, gain, shift, scale):
    t, d = x.shape
    tm = _tile(t, 512)

    def body(x_ref, g_ref, sh_ref, sc_ref, o_ref):
        xv = x_ref[...]
        o_ref[...] = ((xv * _rms(xv) * g_ref[...]) * (1.0 + sc_ref[...]) + sh_ref[...]).astype(BF16)

    vec = pl.BlockSpec((1, d), lambda i: (0, 0))
    return pl.pallas_call(
        body, name="norm_mod", grid=(t // tm,),
        in_specs=[pl.BlockSpec((tm, d), lambda i: (i, 0)), vec, vec, vec],
        out_specs=pl.BlockSpec((tm, d), lambda i: (i, 0)), out_shape=jax.ShapeDtypeStruct((t, d), BF16),
        compiler_params=_params(1),
    )(x, gain, shift, scale)


def _chunk_masks():
    row = lax.broadcasted_iota(jnp.int32, (HEAD, HEAD), 0)
    col = lax.broadcasted_iota(jnp.int32, (HEAD, HEAD), 1)
    same = (row // A_CHUNK) == (col // A_CHUNK)
    return same & (col <= row), same & (col >= row)


def _ones(mask):
    return jnp.where(mask, 1.0, 0.0).astype(BF16)


def _dot_split(ones_bf16, x):
    hi = x.astype(BF16)
    lo = (x - hi.astype(F32)).astype(BF16)
    return _dot(ones_bf16, hi) + _dot(ones_bf16, lo)


def _hgrn_block(direction, f, lb, cum2):
    sf = _sigmoid(f)
    big_f = lb + (1.0 - lb) * sf
    k = (1.0 - lb) * (1.0 - sf)
    lf = jnp.log(big_f)
    both = _dot_split(cum2, lf)
    cf, cr = both[:HEAD], both[HEAD:]
    b, rest = (cf, cr - lf) if direction == 0 else (cr, cf - lf)
    return k, sf, big_f, jnp.exp(b), jnp.exp(-b), jnp.exp(rest)


def _hgrn_fwd(proj, lb, g_norm, width, exchange):
    t = proj.shape[0]
    heads = width // HEAD
    nb, nc = t // HEAD, t // A_CHUNK
    ua = 2 if nb % 2 == 0 else 1
    ub = 8 if nc % 8 == 0 else 4
    q_scale = HEAD ** -0.5
    xin, xout = len(exchange.arrays), len(exchange.out_shapes)

    def body(q_ref, ffw_ref, fbw_ref, v_ref, og_ref, lb_ref, g_ref, *rest):
        xin_refs, rest = rest[:xin], rest[xin:]
        outa_ref, osum_ref = rest[:2]
        xout_refs, rest = rest[2:2 + xout], rest[2 + xout:]
        qd_s, ke_s, dc_s, o_s = rest[:4]
        sem_refs = rest[4:]
        h = pl.program_id(0)

        @pl.when(h == 0)
        def _():
            exchange.start(xin_refs, xout_refs, sem_refs)

        tril, triu = _chunk_masks()
        cum2 = jnp.concatenate([_ones(tril), _ones(triu)], axis=0)
        f_refs = (ffw_ref, fbw_ref)
        lbs = (lb_ref[0:1, :], lb_ref[1:2, :])

        def phase_a(it, carry):
            loaded = []
            for u in range(ua):
                rows = pl.ds(pl.multiple_of((it * ua + u) * HEAD, HEAD), HEAD)
                loaded.append((rows, q_ref[rows, :], v_ref[rows, :], ffw_ref[rows, :], fbw_ref[rows, :]))
            chains = [(d, rows, qv * q_scale, vv.astype(BF16), fv)
                      for rows, qv, vv, f0, f1 in loaded for d, fv in ((0, f0), (1, f1))]
            blocks = [_hgrn_block(d, fv, lbs[d], cum2) for d, _, _, _, fv in chains]
            scaled = [(qv * eb, k * enb, k * erest, eb * erest)
                      for (_, _, qv, _, _), (k, _, _, eb, enb, erest) in zip(chains, blocks)]
            atts = [jnp.where(tril if d == 0 else triu, _bdot(qd, kd, _NT), 0.0)
                    for (d, _, _, _, _), (qd, kd, _, _) in zip(chains, scaled)]
            intras = [_bdot(att, vv) for att, (_, _, _, vv, _) in zip(atts, chains)]
            results = [(d, rows, o_intra, qd.astype(BF16), ke.astype(BF16), decay)
                       for (d, rows, _, _, _), (qd, _, ke, decay), o_intra in zip(chains, scaled, intras)]
            for d, rows, o_intra, qd16, ke16, decay in results:
                o_s[d, rows, :] = o_intra
                qd_s[d, rows, :] = qd16
                ke_s[d, rows, :] = ke16
                dc_s[d, rows, :] = decay
            return carry

        lax.fori_loop(0, nb // ua, phase_a, 0)

        def phase_b(it, states):
            loaded = []
            for u in range(ub):
                n = it * ub + u
                for d in range(2):
                    c = n if d == 0 else nc - 1 - n
                    start = pl.multiple_of(c * A_CHUNK, A_CHUNK)
                    rows = pl.ds(start, A_CHUNK)
                    loaded.append((d, rows, qd_s[d, rows, :], ke_s[d, rows, :], v_ref[rows, :],
                                   dc_s[d, pl.ds(start, 1), :], o_s[d, rows, :]))
            increments = [_dot(vv.astype(BF16), ke16, _TN) for _, _, _, ke16, vv, _, _ in loaded]
            states = list(states)
            befores = []
            for (d, _, _, _, _, decay, _), inc in zip(loaded, increments):
                befores.append(states[d].astype(BF16))
                states[d] = states[d] * decay + inc
            inters = [_dot(qd16, before, _NT) for (_, _, qd16, _, _, _, _), before in zip(loaded, befores)]
            for (d, rows, _, _, _, _, o_intra), o_inter in zip(loaded, inters):
                o_s[d, rows, :] = o_intra + o_inter
            return tuple(states)

        zero_state = jnp.zeros((HEAD, HEAD), F32)
        lax.fori_loop(0, nc // ub, phase_b, (zero_state, zero_state))

        def phase_c(i, carry):
            rows = pl.ds(pl.multiple_of(i * HEAD, HEAD), HEAD)
            o = o_s[0, rows, :] + o_s[1, rows, :]
            osum_ref[rows, :] = o
            og = og_ref[rows, :]
            outa_ref[rows, :] = (o * _rms(o) * g_ref[...] * (og * _sigmoid(og))).astype(BF16)
            return carry

        lax.fori_loop(0, nb, phase_c, 0)

        @pl.when(h == heads - 1)
        def _():
            exchange.finish(xin_refs, xout_refs, sem_refs)

    def col(p):
        return pl.BlockSpec((t, HEAD), lambda h: (0, p * heads + h))

    any_spec = pl.BlockSpec(memory_space=pl.ANY)
    results = pl.pallas_call(
        body, name="hgrn_fwd", grid=(heads,),
        in_specs=[col(0), col(1), col(2), col(3), col(4),
                  pl.BlockSpec((2, HEAD), lambda h: (0, h)), pl.BlockSpec((1, HEAD), lambda h: (0, 0))] + [any_spec] * xin,
        out_specs=[pl.BlockSpec((t, HEAD), lambda h: (0, h)), pl.BlockSpec((t, HEAD), lambda h: (0, h))] + [any_spec] * xout,
        out_shape=[jax.ShapeDtypeStruct((t, width), BF16), jax.ShapeDtypeStruct((t, width), F32)] + list(exchange.out_shapes),
        scratch_shapes=[pltpu.VMEM((2, t, HEAD), BF16), pltpu.VMEM((2, t, HEAD), BF16), pltpu.VMEM((2, t, HEAD), F32),
                        pltpu.VMEM((2, t, HEAD), F32)] + list(exchange.sems),
        compiler_params=_params(1),
    )(proj, proj, proj, proj, proj, lb, g_norm, *exchange.arrays)
    return results[0], results[1], results[2:]


def _sgu_core(u_pre, v_pre, g_v, ws_ref, bst):
    u, du = _gelu_and_grad(u_pre)
    v, dv = _gelu_and_grad(v_pre)
    mu = jnp.mean(v, axis=-1, keepdims=True)
    dlt = v - mu
    rstd = lax.rsqrt(jnp.mean(dlt * dlt, axis=-1, keepdims=True) + EPS)
    vhat = dlt * rstd
    vn = vhat * g_v
    groups = vn.shape[1] // HEAD
    cols = []
    for g in range(groups):
        vm_g = _bdot(ws_ref[g], vn[:, g * HEAD:(g + 1) * HEAD]) + bst[:, g:g + 1]
        cols.append(vm_g)
    return u, du, dv, vhat, rstd, vn, jnp.concatenate(cols, axis=1)


def _sgu_fwd(proj, g_v, w_s, bst, width, z_block):
    t = proj.shape[0]

    def body(u_ref, v_ref, g_ref, ws_ref, bst_ref, o_ref):
        u, _, _, _, _, _, vm = _sgu_core(u_ref[...], v_ref[...], g_ref[...], ws_ref, bst_ref[...])
        o_ref[...] = (u * vm).astype(BF16)

    groups = width // HEAD
    return pl.pallas_call(
        body, name="sgu_fwd", grid=(t // HEAD,),
        in_specs=[pl.BlockSpec((HEAD, width), lambda i: (i, z_block)), pl.BlockSpec((HEAD, width), lambda i: (i, z_block + 1)),
                  pl.BlockSpec((1, width), lambda i: (0, 0)), pl.BlockSpec((groups, HEAD, HEAD), lambda i: (0, 0, 0)),
                  pl.BlockSpec((HEAD, groups), lambda i: (0, 0))],
        out_specs=pl.BlockSpec((HEAD, width), lambda i: (i, 0)),
        out_shape=jax.ShapeDtypeStruct((t, width), BF16), compiler_params=_params(1),
    )(proj, proj, g_v, w_s, bst)


def _sgu_bwd(proj, dout_b, dproj, g_v, w_s, w_st, bst, width, z_block):
    t = proj.shape[0]
    groups = width // HEAD
    nblk = t // HEAD

    def body(u_ref, v_ref, do_ref, g_ref, ws_ref, wst_ref, bst_ref, dproj_hbm,
             dz_ref, dg_ref, dws_ref, dbst_ref, res_s):
        i, p = pl.program_id(0), pl.program_id(1)

        @pl.when((i == 0) & (p == 0))
        def _():
            dg_ref[...] = jnp.zeros_like(dg_ref)
            dws_ref[...] = jnp.zeros_like(dws_ref)
            dbst_ref[...] = jnp.zeros_like(dbst_ref)

        @pl.when(p == 0)
        def _():
            g_v = g_ref[...]
            u, du, dv, vhat, rstd, vn, vm = _sgu_core(u_ref[...], v_ref[...], g_v, ws_ref, bst_ref[...])
            dout = do_ref[...].astype(F32)
            res_s[0] = (dout * vm * du).astype(BF16)
            dvm = dout * u
            dvn_cols = []
            for g in range(groups):
                sl = slice(g * HEAD, (g + 1) * HEAD)
                dvm_g = dvm[:, sl]
                dbst_ref[:, g:g + 1] += jnp.sum(dvm_g, axis=1, keepdims=True)
                dws_ref[g] += _bdot(dvm_g, vn[:, sl], _NT)
                dvn_cols.append(_bdot(wst_ref[g], dvm_g))
            dvn = jnp.concatenate(dvn_cols, axis=1)
            dg_ref[...] += _colsum(dvn * vhat)
            dvh = dvn * g_v
            dvg = rstd * (dvh - jnp.mean(dvh, axis=-1, keepdims=True)
                          - vhat * jnp.mean(dvh * vhat, axis=-1, keepdims=True))
            res_s[1] = (dvg * dv).astype(BF16)

        dz_ref[...] = res_s[p]

    n_in = dproj.shape[1]
    return pl.pallas_call(
        body, name="sgu_bwd", grid=(nblk, 2),
        in_specs=[pl.BlockSpec((HEAD, width), lambda i, p: (i, z_block)),
                  pl.BlockSpec((HEAD, width), lambda i, p: (i, z_block + 1)),
                  pl.BlockSpec((HEAD, width), lambda i, p: (i, 0)),
                  pl.BlockSpec((1, width), lambda i, p: (0, 0)),
                  pl.BlockSpec((groups, HEAD, HEAD), lambda i, p: (0, 0, 0)),
                  pl.BlockSpec((groups, HEAD, HEAD), lambda i, p: (0, 0, 0)),
                  pl.BlockSpec((HEAD, groups), lambda i, p: (0, 0)),
                  pl.BlockSpec(memory_space=pl.ANY)],
        out_specs=[pl.BlockSpec((HEAD, width), lambda i, p: (i, z_block + p)),
                   pl.BlockSpec((1, width), lambda i, p: (0, 0)),
                   pl.BlockSpec((groups, HEAD, HEAD), lambda i, p: (0, 0, 0)),
                   pl.BlockSpec((HEAD, groups), lambda i, p: (0, 0))],
        out_shape=[jax.ShapeDtypeStruct((t, n_in), BF16), jax.ShapeDtypeStruct((1, width), F32),
                   jax.ShapeDtypeStruct((groups, HEAD, HEAD), F32), jax.ShapeDtypeStruct((HEAD, groups), F32)],
        scratch_shapes=[pltpu.VMEM((2, HEAD, width), BF16)],
        input_output_aliases={7: 0},
        compiler_params=_params(2),
    )(proj, proj, dout_b, g_v, w_s, w_st, bst, dproj)


def _hgrn_bwd(proj, osum, dout_a, dproj, lb, g_norm, width, exchange):
    t = proj.shape[0]
    heads = width // HEAD
    nb = t // HEAD
    cpb = HEAD // A_CHUNK
    q_scale = HEAD ** -0.5
    xin, xout = len(exchange.arrays), len(exchange.out_shapes)

    def body(q_ref, ffw_ref, fbw_ref, v_ref, og_ref, osum_ref, douta_ref, lb_ref, g_ref, dproj_hbm, *rest):
        xin_refs, rest = rest[:xin], rest[xin:]
        out_ref, dgh_ref, dlb_ref = rest[:3]
        xout_refs, rest = rest[3:3 + xout], rest[3 + xout:]
        do_s, dq_s, dv_s, res_s, ck_s = rest[:5]
        sem_refs = rest[5:]
        h, p = pl.program_id(0), pl.program_id(1)
        f_refs = (ffw_ref, fbw_ref)

        @pl.when((h == 0) & (p == 0))
        def _():
            exchange.start(xin_refs, xout_refs, sem_refs)

        @pl.when(p == 0)
        def _():
            tril, triu = _chunk_masks()
            cum2 = jnp.concatenate([_ones(tril), _ones(triu)], axis=0)
            g_row = g_ref[...]

            def pass_norm(i, dgh):
                rows = pl.ds(pl.multiple_of(i * HEAD, HEAD), HEAD)
                o = osum_ref[rows, :]
                r = _rms(o)
                oh = o * r
                og = og_ref[rows, :]
                sg = _sigmoid(og)
                dout = douta_ref[rows, :].astype(F32)
                don = dout * (og * sg)
                res_s[4, rows, :] = (dout * (oh * g_row) * (sg * (1.0 + og * (1.0 - sg)))).astype(BF16)
                doh = don * g_row
                do_s[rows, :] = r * (doh - oh * jnp.mean(doh * oh, axis=-1, keepdims=True))
                return dgh + _colsum(don * oh)

            dgh_ref[...] = lax.fori_loop(0, nb, pass_norm, jnp.zeros((1, HEAD), F32))

            lbs = (lb_ref[0:1, :], lb_ref[1:2, :])
            zero_state = jnp.zeros((HEAD, HEAD), F32)

            def chunk_order(d):
                return list(range(cpb)) if d == 0 else list(range(cpb - 1, -1, -1))

            def chunk(x, j):
                return x[j * A_CHUNK:(j + 1) * A_CHUNK, :]

            def decay_row(e_big, j):
                return e_big[j * A_CHUNK:j * A_CHUNK + 1, :]

            def cat(parts):
                return jnp.concatenate([parts[j] for j in range(cpb)], axis=0)

            def block_states(d, start, incs, e_big):
                befores, st = {}, start
                for j in chunk_order(d):
                    befores[j] = st
                    st = st * decay_row(e_big, j) + incs[j]
                return befores, st

            def pass_states(it, states):
                loaded = []
                for d in range(2):
                    blk = it if d == 0 else nb - 1 - it
                    rows = pl.ds(pl.multiple_of(blk * HEAD, HEAD), HEAD)
                    loaded.append((d, blk, f_refs[d][rows, :], v_ref[rows, :]))
                blocks = [_hgrn_block(d, fv, lbs[d], cum2) for d, _, fv, _ in loaded]
                incs = [{j: _bdot(chunk(vv, j), chunk(k * erest, j), _TN) for j in range(cpb)}
                        for (_, _, _, vv), (k, _, _, _, _, erest) in zip(loaded, blocks)]
                after = [block_states(d, states[d], inc, eb * erest)[1]
                         for (d, _, _, _), (_, _, _, eb, _, erest), inc in zip(loaded, blocks, incs)]
                for d, blk, _, _ in loaded:
                    ck_s[d, blk] = states[d]
                return tuple(after)

            lax.fori_loop(0, nb, pass_states, (zero_state, zero_state))

            def pass_back(it, carry):
                gts, dlb = [carry[0], carry[1]], carry[2]
                loaded = []
                for d in range(2):
                    blk = nb - 1 - it if d == 0 else it
                    rows = pl.ds(pl.multiple_of(blk * HEAD, HEAD), HEAD)
                    loaded.append((d, rows, f_refs[d][rows, :], q_ref[rows, :], v_ref[rows, :], do_s[rows, :], ck_s[d, blk]))
                blocks = [_hgrn_block(d, fv, lbs[d], cum2) for d, _, fv, _, _, _, _ in loaded]
                scaled = []
                for (_, _, _, qv, _, _, _), (k, _, _, eb, enb, erest) in zip(loaded, blocks):
                    qh = qv * q_scale
                    scaled.append((qh, qh * eb, k * enb, k * erest, eb * erest))
                masks = [tril if d == 0 else triu for d, *_ in loaded]
                atts = [jnp.where(m, _bdot(qd, kd, _NT), 0.0) for m, (_, qd, kd, _, _) in zip(masks, scaled)]
                datts = [jnp.where(m, _bdot(do, vv, _NT), 0.0) for m, (_, _, _, _, vv, do, _) in zip(masks, loaded)]
                dvs = [_bdot(att, do, _TN) for att, (_, _, _, _, _, do, _) in zip(atts, loaded)]
                dqds = [_bdot(datt, kd) for datt, (_, _, kd, _, _) in zip(datts, scaled)]
                dkds = [_bdot(datt, qd, _TN) for datt, (_, qd, _, _, _) in zip(datts, scaled)]
                s_incs = [{j: _bdot(chunk(vv, j), chunk(ke, j), _TN) for j in range(cpb)}
                          for (_, _, _, _, vv, _, _), (_, _, _, ke, _) in zip(loaded, scaled)]
                g_incs = [{j: _bdot(chunk(do, j), chunk(qd, j), _TN) for j in range(cpb)}
                          for (_, _, _, _, _, do, _), (_, qd, _, _, _) in zip(loaded, scaled)]
                befores, afters, g_at = [], [], []
                for (d, _, _, _, _, _, ck), (_, _, _, _, e_big), s_inc, g_inc in zip(loaded, scaled, s_incs, g_incs):
                    order = chunk_order(d)
                    before, after = block_states(d, ck, s_inc, e_big)
                    befores.append(before)
                    afters.append({j: (before[order[n + 1]] if n + 1 < cpb else after) for n, j in enumerate(order)})
                    at, gt = {}, gts[d]
                    for j in reversed(order):
                        at[j] = gt
                        gt = gt * decay_row(e_big, j) + g_inc[j]
                    gts[d] = gt
                    g_at.append(at)
                dqd_i = [{j: _bdot(chunk(do, j), before[j]) for j in range(cpb)}
                         for (_, _, _, _, _, do, _), before in zip(loaded, befores)]
                dv_i = [{j: _bdot(chunk(ke, j), at[j], _NT) for j in range(cpb)}
                        for (_, _, _, ke, _), at in zip(scaled, g_at)]
                dke = [{j: _bdot(chunk(vv, j), at[j]) for j in range(cpb)}
                       for (_, _, _, _, vv, _, _), at in zip(loaded, g_at)]
                results, new = [], []
                for n, ((d, rows, _, _, _, _, _), (k, sf, big_f, eb, enb, erest), (qh, _, _, _, _)) in enumerate(
                        zip(loaded, blocks, scaled)):
                    dqh = (dqds[n] + cat(dqd_i[n])) * eb
                    dk = dkds[n] * enb + cat(dke[n]) * erest
                    carry_rows = {j: jnp.broadcast_to(_colsum(g_at[n][j] * afters[n][j]), (A_CHUNK, HEAD))
                                  for j in range(cpb)}
                    dlf = _dot_split(_ones(triu if d == 0 else tril), qh * dqh - k * dk) + cat(carry_rows)
                    common = dlf / big_f - dk
                    results.append((d, rows, (k * sf * common).astype(BF16), dqh.astype(BF16),
                                    (dvs[n] + cat(dv_i[n])).astype(BF16)))
                    new.append(_colsum((1.0 - sf) * common))
                for d, rows, df16, dq16, dv16 in results:
                    res_s[1 + d, rows, :] = df16
                    dq_s[d, rows, :] = dq16
                    dv_s[d, rows, :] = dv16
                return gts[0], gts[1], dlb + jnp.concatenate(new, axis=0)

            dlb_ref[...] = lax.fori_loop(0, nb, pass_back, (zero_state, zero_state, jnp.zeros((2, HEAD), F32)))[2]

            def pass_out(i, carry):
                rows = pl.ds(pl.multiple_of(i * HEAD, HEAD), HEAD)
                dq = dq_s[0, rows, :].astype(F32) + dq_s[1, rows, :].astype(F32)
                res_s[0, rows, :] = (dq * q_scale).astype(BF16)
                res_s[3, rows, :] = (dv_s[0, rows, :].astype(F32) + dv_s[1, rows, :].astype(F32)).astype(BF16)
                return carry

            lax.fori_loop(0, nb, pass_out, 0)

        out_ref[...] = res_s[p]

        @pl.when((h == heads - 1) & (p == 4))
        def _():
            exchange.finish(xin_refs, xout_refs, sem_refs)

    def col(pp):
        return pl.BlockSpec((t, HEAD), lambda h, p: (0, pp * heads + h))

    n_in = dproj.shape[1]
    any_spec = pl.BlockSpec(memory_space=pl.ANY)
    results = pl.pallas_call(
        body, name="hgrn_bwd", grid=(heads, 5),
        in_specs=[col(0), col(1), col(2), col(3), col(4),
                  pl.BlockSpec((t, HEAD), lambda h, p: (0, h)), pl.BlockSpec((t, HEAD), lambda h, p: (0, h)),
                  pl.BlockSpec((2, HEAD), lambda h, p: (0, h)), pl.BlockSpec((1, HEAD), lambda h, p: (0, 0)),
                  any_spec] + [any_spec] * xin,
        out_specs=[pl.BlockSpec((t, HEAD), lambda h, p: (0, p * heads + h)),
                   pl.BlockSpec((None, 1, HEAD), lambda h, p: (h, 0, 0)),
                   pl.BlockSpec((2, HEAD), lambda h, p: (0, h))] + [any_spec] * xout,
        out_shape=[jax.ShapeDtypeStruct((t, n_in), BF16), jax.ShapeDtypeStruct((heads, 1, HEAD), F32),
                   jax.ShapeDtypeStruct((2, width), F32)] + list(exchange.out_shapes),
        scratch_shapes=[pltpu.VMEM((t, HEAD), F32), pltpu.VMEM((2, t, HEAD), BF16), pltpu.VMEM((2, t, HEAD), BF16),
                        pltpu.VMEM((5, t, HEAD), BF16), pltpu.VMEM((2, nb, HEAD, HEAD), F32)] + list(exchange.sems),
        input_output_aliases={9: 0},
        compiler_params=_params(2),
    )(proj, proj, proj, proj, proj, osum, dout_a, lb, g_norm, dproj, *exchange.arrays)
    return results[0], results[1], results[2], results[3:]


def _adamw(w, g, m, v):
    m = ADAM_B1 * m + (1.0 - ADAM_B1) * g
    v = ADAM_B2 * v + (1.0 - ADAM_B2) * (g * g)
    m_hat = m / (1.0 - ADAM_B1 ** ADAM_STEP)
    v_hat = v / (1.0 - ADAM_B2 ** ADAM_STEP)
    delta = -ADAM_LR * (m_hat / (jnp.sqrt(v_hat) + ADAM_EPS) + ADAM_WD * w)
    return delta, m, v


def _adamw_big(name, me, w, m, v, g_full, landing, axis):
    r, c = w.shape
    tr = _tile(r, 128)

    def body(me_ref, w_ref, m_ref, v_ref, g_ref, l_ref, og_ref, od_ref, om_ref, ov_ref):
        g = g_ref[...]
        for s in range(N_DEV - 1):
            g = g + l_ref[s].astype(F32)
        og_ref[...] = g
        od_ref[...], om_ref[...], ov_ref[...] = _adamw(w_ref[...], g, m_ref[...], v_ref[...])

    shard = pl.BlockSpec((tr, c), lambda i, me_ref: (i, 0))
    if axis == 1:
        own = pl.BlockSpec((tr, c), lambda i, me_ref: (i, me_ref[0]))
    else:
        own = pl.BlockSpec((tr, c), lambda i, me_ref: (me_ref[0] * (r // tr) + i, 0))
    grid_spec = pltpu.PrefetchScalarGridSpec(
        num_scalar_prefetch=1, grid=(r // tr,),
        in_specs=[shard, shard, shard, own, pl.BlockSpec((N_DEV - 1, tr, c), lambda i, me_ref: (0, i, 0))],
        out_specs=[shard] * 4)
    return pl.pallas_call(
        body, name=name, grid_spec=grid_spec, out_shape=[jax.ShapeDtypeStruct((r, c), F32)] * 4,
        compiler_params=_params(1),
    )(me, w, m, v, g_full, landing)


def _adamw_ada(sct, dmod_mine, w, m, v):
    d, n = w.shape
    tr = _tile(d, 256)

    def body(s_ref, dm_ref, w_ref, m_ref, v_ref, og_ref, od_ref, om_ref, ov_ref):
        g = _dot(s_ref[...], dm_ref[...], precision=HIGHEST)
        og_ref[...] = g
        od_ref[...], om_ref[...], ov_ref[...] = _adamw(w_ref[...], g, m_ref[...], v_ref[...])

    blk = pl.BlockSpec((tr, n), lambda i: (i, 0))
    return pl.pallas_call(
        body, name="adamw_ada", grid=(d // tr,),
        in_specs=[pl.BlockSpec((tr, N_DEV), lambda i: (i, 0)), pl.BlockSpec((N_DEV, n), lambda i: (0, 0)), blk, blk, blk],
        out_specs=[blk] * 4, out_shape=[jax.ShapeDtypeStruct((d, n), F32)] * 4, compiler_params=_params(1),
    )(sct, dmod_mine, w, m, v)


def _adamw_small(gathered, w, m, v):
    def body(g_ref, w_ref, m_ref, v_ref, og_ref, od_ref, om_ref, ov_ref):
        g = g_ref[0]
        for s in range(1, N_DEV):
            g = g + g_ref[s]
        og_ref[...] = g
        od_ref[...], om_ref[...], ov_ref[...] = _adamw(w_ref[...], g, m_ref[...], v_ref[...])

    return pl.pallas_call(
        body, name="adamw_small", out_shape=[jax.ShapeDtypeStruct(w.shape, F32)] * 4,
        compiler_params=pltpu.CompilerParams(vmem_limit_bytes=VMEM_LIMIT),
    )(gathered, w, m, v)


def _adamw_lb(dlb_mine, lb_logits, m, v):
    def body(d_ref, l_ref, m_ref, v_ref, og_ref, od_ref, om_ref, ov_ref):
        dlb = d_ref[0]
        for s in range(1, N_DEV):
            dlb = dlb + d_ref[s]
        for dr in range(2):
            lb = _sigmoid(l_ref[dr][0:1, :] - l_ref[dr][1:2, :])
            d0 = dlb[dr:dr + 1] * lb * (1.0 - lb)
            g = jnp.concatenate([d0, -d0], axis=0)
            og_ref[dr] = g
            od_ref[dr], om_ref[dr], ov_ref[dr] = _adamw(l_ref[dr], g, m_ref[dr], v_ref[dr])

    return pl.pallas_call(body, name="adamw_lb", out_shape=[jax.ShapeDtypeStruct(lb_logits.shape, F32)] * 4,
                          )(dlb_mine, lb_logits, m, v)


def _rows(a, pad_to=8):
    flat = a.reshape(-1, LANE)
    pad = (-flat.shape[0]) % pad_to
    return jnp.pad(flat, ((0, pad), (0, 0))) if pad else flat


def kernel(x, c, w_ada, b_ada, g_pre_mix, g_post_mix, g_pre_ffn, g_post_ffn, w_in, lb_logits, g_hgrn_norm, w_a_out, g_sgu_norm, w_spatial, b_spatial, w_b_out, w_o, w_ff1, w_ff2, loss_target, m_w_ada, m_b_ada, m_g_pre_mix, m_g_post_mix, m_g_pre_ffn, m_g_post_ffn, m_w_in, m_lb_logits, m_g_hgrn_norm, m_w_a_out, m_g_sgu_norm, m_w_spatial, m_b_spatial, m_w_b_out, m_w_o, m_w_ff1, m_w_ff2, v_w_ada, v_b_ada, v_g_pre_mix, v_g_post_mix, v_g_pre_ffn, v_g_post_ffn, v_w_in, v_lb_logits, v_g_hgrn_norm, v_w_a_out, v_g_sgu_norm, v_w_spatial, v_b_spatial, v_w_b_out, v_w_o, v_w_ff1, v_w_ff2):
    t, d = x.shape[1], x.shape[2]
    n_in = w_in.shape[2] * N_DEV
    width = (n_in - 2 * d) // 7
    heads = width // HEAD
    assert heads == N_DEV and width % LANE == 0
    d_ff = w_ff1.shape[2] * N_DEV
    n_ada = w_ada.shape[2]
    me = _dev_index()
    me_arr = me.reshape(1).astype(jnp.int32)
    x2, tgt = x[0], loss_target[0]

    big = [w_in[0], w_a_out[0], w_b_out[0], w_o[0], w_ff1[0], w_ff2[0]]
    big_axes = [1, 1, 1, 0, 1, 0]
    big_names = ["w_in", "w_a_out", "w_b_out", "w_o", "w_ff1", "w_ff2"]
    shards16 = [_cast_bf16("cast_" + nm, w) for nm, w in zip(big_names, big)]
    wf_in, = _run_exchange("gather_w_in", _gather_plan(shards16[:1], big_axes[:1]))
    gather_ff1 = _gather_plan(shards16[4:5], big_axes[4:5])
    gather_mid = _gather_plan(shards16[1:4], big_axes[1:4])
    gather_ff2 = _gather_plan(shards16[5:6], big_axes[5:6])

    c_rows = d // LANE
    small = _all_gather_small("gather_c_lb", _prep_small(c[0:1], lb_logits))
    sc_all = small[:, :c_rows, :].reshape(N_DEV, d)
    lb = jnp.transpose(small[:, c_rows:c_rows + 2, :], (1, 0, 2)).reshape(2, width)
    b_shard = lax.dynamic_slice_in_dim(b_ada, me * n_ada, n_ada, axis=1)
    mod_sh = _mod_shard(sc_all, w_ada[0], b_shard)
    mod_all = _all_gather_small("gather_mod", _rows(mod_sh))
    mod_all = mod_all[:, :N_DEV * n_ada // LANE, :].reshape(N_DEV, N_DEV, n_ada)
    mod6 = lax.dynamic_index_in_dim(mod_all, me, axis=1, keepdims=False).reshape(N_MOD, d)
    sh1, sc1, gt1, sh2, sc2, gt2 = [mod6[i:i + 1] for i in range(N_MOD)]

    a1 = _norm_mod(x2, g_pre_mix, sh1, sc1)
    tm = _tile(t, 512)

    def store_f32(acc, i, j, extra_refs, out_refs, rows):
        out_refs[0][...] = acc

    tn_in = _tile(n_in, 1024)
    (proj,), (wf_1,) = _mm("proj", a1, wf_in, _NN, t, n_in, d, tm, tn_in, d, [],
                           [(jax.ShapeDtypeStruct((t, n_in), F32), (tm, tn_in), lambda i, j: (i, j))], store_f32,
                           exchange=gather_ff1)

    out_a, osum, (wf_a, wf_b, wf_o) = _hgrn_fwd(proj, lb, g_hgrn_norm, width, gather_mid)
    z_block = 5
    bst = b_spatial[0].T
    out_b = _sgu_fwd(proj, g_sgu_norm, w_spatial[0], bst, width, z_block)

    tn_d = _tile(d, 512)
    blk_d = ((tm, tn_d), lambda i, j: (i, j))
    y_a, = _mm("y_a", out_a, wf_a, _NN, t, d, width, tm, tn_d, width, [],
               [(jax.ShapeDtypeStruct((t, d), F32),) + blk_d], store_f32)
    ga_blk = (5 * width + 2 * width) // tn_d
    gb_blk = ga_blk + d // tn_d

    def merge(acc, i, j, extra_refs, out_refs, rows):
        ga, gb, ya = extra_refs
        out_refs[0][...] = acc
        out_refs[1][...] = (_sigmoid(ga[...]) * ya[...] + _sigmoid(gb[...]) * acc).astype(BF16)

    y_b, merged = _mm("y_b_merge", out_b, wf_b, _NN, t, d, width, tm, tn_d, width,
                      [(proj, (tm, tn_d), lambda i, j: (i, ga_blk + j)), (proj, (tm, tn_d), lambda i, j: (i, gb_blk + j)),
                       (y_a,) + blk_d],
                      [(jax.ShapeDtypeStruct((t, d), F32),) + blk_d, (jax.ShapeDtypeStruct((t, d), BF16),) + blk_d], merge)

    tr = _tile(t, 256)
    rc = 64 if tr % 64 == 0 else None
    row_d = ((tr, d), lambda i, j: (i, 0))
    vec_d = ((1, d), lambda i, j: (0, 0))

    def post_mix(acc, i, j, extra_refs, out_refs, rows):
        x_r, gt1_r, g2_r, g3_r, sc2_r, sh2_r = extra_refs
        h1 = x_r[rows, :] + gt1_r[...] * (acc * _rms(acc) * g2_r[...])
        out_refs[0][rows, :] = acc
        out_refs[1][rows, :] = h1
        out_refs[2][rows, :] = ((h1 * _rms(h1) * g3_r[...]) * (1.0 + sc2_r[...]) + sh2_r[...]).astype(BF16)

    mo, h1, a2 = _mm("w_o_post_mix", merged, wf_o, _NN, t, d, d, tr, d, d,
                     [(x2,) + row_d, (gt1,) + vec_d, (g_post_mix,) + vec_d, (g_pre_ffn,) + vec_d, (sc2,) + vec_d, (sh2,) + vec_d],
                     [(jax.ShapeDtypeStruct((t, d), F32),) + row_d, (jax.ShapeDtypeStruct((t, d), F32),) + row_d,
                      (jax.ShapeDtypeStruct((t, d), BF16),) + row_d], post_mix, row_chunk=rc)

    tn_f = _tile(d_ff, 1024)
    blk_f = ((tm, tn_f), lambda i, j: (i, j))

    def relu_sq(acc, i, j, extra_refs, out_refs, rows):
        r = jnp.maximum(acc, 0.0)
        out_refs[0][...] = acc.astype(BF16)
        out_refs[1][...] = (r * r).astype(BF16)

    (hff, act), (wf_2,) = _mm(
        "ff1", a2, wf_1, _NN, t, d_ff, d, tm, tn_f, d, [],
        [(jax.ShapeDtypeStruct((t, d_ff), BF16),) + blk_f, (jax.ShapeDtypeStruct((t, d_ff), BF16),) + blk_f], relu_sq,
        exchange=gather_ff2)

    sums_d = ((8, d), lambda i, j: (0, 0))

    def zero_first(sums_r, i, rows):
        if rows.start in (None, 0):
            @pl.when(i == 0)
            def _():
                sums_r[...] = jnp.zeros_like(sums_r)

    def loss_head(acc, i, j, extra_refs, out_refs, rows):
        h1_r, tgt_r, gt2_r, g4_r = extra_refs
        dy_r, dff_r, sums_r = out_refs
        r4 = _rms(acc)
        ffn = acc * r4
        n4 = ffn * g4_r[...]
        err = h1_r[rows, :] + gt2_r[...] * n4 - tgt_r[rows, :]
        dy = err * (1.0 / d)
        dy_r[rows, :] = dy
        dn4 = dy * gt2_r[...]
        dffn = dn4 * g4_r[...]
        dff_r[rows, :] = (r4 * (dffn - ffn * jnp.mean(dffn * ffn, axis=-1, keepdims=True))).astype(BF16)
        zero_first(sums_r, i, rows)

        sums_r[0:1, :] += _colsum(err * err)
        sums_r[1:2, :] += _colsum(dy * n4)
        sums_r[2:3, :] += _colsum(dn4 * ffn)

    tk_f = _tile(d_ff, 2048)
    dy, dff, sums_f = _mm("ff2_loss", act, wf_2, _NN, t, d, d_ff, tr, d, tk_f,
                          [(h1,) + row_d, (tgt,) + row_d, (gt2,) + vec_d, (g_post_ffn,) + vec_d],
                          [(jax.ShapeDtypeStruct((t, d), F32),) + row_d, (jax.ShapeDtypeStruct((t, d), BF16),) + row_d,
                           (jax.ShapeDtypeStruct((8, d), F32),) + sums_d], loss_head, row_chunk=rc)
    loss = lax.psum((0.5 / d) * jnp.sum(sums_f[0]), ("x", "y", "c"))

    def relu_sq_bwd(acc, i, j, extra_refs, out_refs, rows):
        out_refs[0][...] = (acc * (2.0 * jnp.maximum(extra_refs[0][...].astype(F32), 0.0))).astype(BF16)

    dhff, = _mm("d_hff", dff, wf_2, _NT, t, d_ff, d, tm, tn_f, d, [(hff,) + blk_f],
                [(jax.ShapeDtypeStruct((t, d_ff), BF16),) + blk_f], relu_sq_bwd)
    gw_ff2, gw_ff2_16 = _grad_w("grad_w_ff2", act, dff)
    gw_ff1, gw_ff1_16 = _grad_w("grad_w_ff1", a2, dhff)

    def pre_ffn_bwd(acc, i, j, extra_refs, out_refs, rows):
        h1_r, dy_r, mo_r, sc2_r, g3_r, gt1_r, g2_r = extra_refs
        dh1_r, dmo_r, sums_r = out_refs
        h1v = h1_r[rows, :]
        r3 = _rms(h1v)
        h1n = h1v * r3
        dn3 = acc * (1.0 + sc2_r[...])
        dh1n = dn3 * g3_r[...]
        dh1 = dy_r[rows, :] + r3 * (dh1n - h1n * jnp.mean(dh1n * h1n, axis=-1, keepdims=True))
        dh1_r[rows, :] = dh1
        mov = mo_r[rows, :]
        r2 = _rms(mov)
        mon = mov * r2
        dn2 = dh1 * gt1_r[...]
        dmon = dn2 * g2_r[...]
        dmo_r[rows, :] = (r2 * (dmon - mon * jnp.mean(dmon * mon, axis=-1, keepdims=True))).astype(BF16)
        zero_first(sums_r, i, rows)

        sums_r[0:1, :] += _colsum(acc)
        sums_r[1:2, :] += _colsum(acc * (h1n * g3_r[...]))
        sums_r[2:3, :] += _colsum(dn3 * h1n)
        sums_r[3:4, :] += _colsum(dh1 * (mon * g2_r[...]))
        sums_r[4:5, :] += _colsum(dn2 * mon)

    (dh1, dmo, sums_m), (land_ff2,) = _mm("d_a2_pre_ffn", dhff, wf_1, _NT, t, d, d_ff, tr, d, tk_f,
                           [(h1,) + row_d, (dy,) + row_d, (mo,) + row_d, (sc2,) + vec_d, (g_pre_ffn,) + vec_d,
                            (gt1,) + vec_d, (g_post_mix,) + vec_d],
                           [(jax.ShapeDtypeStruct((t, d), F32),) + row_d, (jax.ShapeDtypeStruct((t, d), BF16),) + row_d,
                            (jax.ShapeDtypeStruct((8, d), F32),) + sums_d], pre_ffn_bwd, row_chunk=rc,
                           exchange=_scatter_plan([gw_ff2_16], big_axes[5:6]))
    gw_o, gw_o_16 = _grad_w("grad_w_o", merged, dmo)

    n_j = d // tn_d

    def merge_bwd_body(dmo_ref, wo_ref, ga_ref, gb_ref, ya_ref, yb_ref, dya_ref, dyb_ref, dproj_ref, acc_s):
        g = pl.program_id(2)

        @pl.when(g == 0)
        def _():
            dm = _dot(dmo_ref[...], wo_ref[...], _NT)
            acc_s[...] = dm
            sa = _sigmoid(ga_ref[...])
            dya_ref[...] = (dm * sa).astype(BF16)
            dproj_ref[...] = (dm * ya_ref[...] * sa * (1.0 - sa)).astype(BF16)

        @pl.when(g == 1)
        def _():
            dm = acc_s[...]
            sb = _sigmoid(gb_ref[...])
            dyb_ref[...] = (dm * sb).astype(BF16)
            dproj_ref[...] = (dm * yb_ref[...] * sb * (1.0 - sb)).astype(BF16)

    tile3 = pl.BlockSpec((tm, tn_d), lambda i, j, g: (i, j))
    dy_a, dy_b, dproj = pl.pallas_call(
        merge_bwd_body, name="d_merged", grid=(t // tm, n_j, 2),
        in_specs=[pl.BlockSpec((tm, d), lambda i, j, g: (i, 0)), pl.BlockSpec((tn_d, d), lambda i, j, g: (j, 0)),
                  pl.BlockSpec((tm, tn_d), lambda i, j, g: (i, ga_blk + j)),
                  pl.BlockSpec((tm, tn_d), lambda i, j, g: (i, gb_blk + j)), tile3, tile3],
        out_specs=[tile3, tile3, pl.BlockSpec((tm, tn_d), lambda i, j, g: (i, ga_blk + g * n_j + j))],
        out_shape=[jax.ShapeDtypeStruct((t, d), BF16), jax.ShapeDtypeStruct((t, d), BF16),
                   jax.ShapeDtypeStruct((t, n_in), BF16)],
        scratch_shapes=[pltpu.VMEM((tm, tn_d), F32)], compiler_params=_params(3),
    )(dmo, wf_o, proj, proj, y_a, y_b)

    def store_bf16(acc, i, j, extra_refs, out_refs, rows):
        out_refs[0][...] = acc.astype(BF16)

    tn_w = _tile(width, 512)
    blk_w = ((tm, tn_w), lambda i, j: (i, j))
    dout_a, = _mm("d_out_a", dy_a, wf_a, _NT, t, width, d, tm, tn_w, d, [],
                  [(jax.ShapeDtypeStruct((t, width), BF16),) + blk_w], store_bf16)
    dout_b, = _mm("d_out_b", dy_b, wf_b, _NT, t, width, d, tm, tn_w, d, [],
                  [(jax.ShapeDtypeStruct((t, width), BF16),) + blk_w], store_bf16)
    gw_a, gw_a_16 = _grad_w("grad_w_a_out", out_a, dy_a)
    gw_b, gw_b_16 = _grad_w("grad_w_b_out", out_b, dy_b)

    w_st = jnp.swapaxes(w_spatial[0], 1, 2)
    dproj, dg_sgu, dw_sp, dbst = _sgu_bwd(proj, dout_b, dproj, g_sgu_norm, w_spatial[0], w_st, bst, width, z_block)
    scatter_rest = _scatter_plan([gw_a_16, gw_b_16, gw_o_16, gw_ff1_16], big_axes[1:5])
    dproj, dgh_heads, dlb, lands_rest = _hgrn_bwd(proj, osum, dout_a, dproj, lb, g_hgrn_norm, width, scatter_rest)
    gw_in, gw_in_16 = _grad_w("grad_w_in", a1, dproj)

    def pre_mix_bwd(acc, i, j, extra_refs, out_refs, rows):
        x_r, dh1_r, sc1_r, g1_r = extra_refs
        dx_r, sums_r = out_refs
        xv = x_r[rows, :]
        r1 = _rms(xv)
        xn = xv * r1
        dn1 = acc * (1.0 + sc1_r[...])
        dxn = dn1 * g1_r[...]
        dx_r[rows, :] = dh1_r[rows, :] + r1 * (dxn - xn * jnp.mean(dxn * xn, axis=-1, keepdims=True))
        zero_first(sums_r, i, rows)

        sums_r[0:1, :] += _colsum(acc)
        sums_r[1:2, :] += _colsum(acc * (xn * g1_r[...]))
        sums_r[2:3, :] += _colsum(dn1 * xn)

    tk_in = _tile(n_in, 2816)
    (grad_x, sums_x), (land_in,) = _mm(
        "d_a1_pre_mix", dproj, wf_in, _NT, t, d, n_in, tr, d, tk_in,
        [(x2,) + row_d, (dh1,) + row_d, (sc1,) + vec_d, (g_pre_mix,) + vec_d],
        [(jax.ShapeDtypeStruct((t, d), F32),) + row_d, (jax.ShapeDtypeStruct((8, d), F32),) + sums_d],
        pre_mix_bwd, row_chunk=rc, exchange=_scatter_plan([gw_in_16], big_axes[:1]))

    lands = [land_in] + list(lands_rest) + [land_ff2]
    full32 = [gw_in, gw_a, gw_b, gw_o, gw_ff1, gw_ff2]
    moms = [m_w_in, m_w_a_out, m_w_b_out, m_w_o, m_w_ff1, m_w_ff2]
    vars_ = [v_w_in, v_w_a_out, v_w_b_out, v_w_o, v_w_ff1, v_w_ff2]
    big_out = {}
    for nm, w, mm_, vv_, gf, ld, ax in zip(big_names, big, moms, vars_, full32, lands, big_axes):
        big_out[nm] = [o[None] for o in _adamw_big("adamw_" + nm, me_arr, w, mm_[0], vv_[0], gf, ld, ax)]

    dmod = jnp.concatenate([sums_x[0:2], sums_m[3:4], sums_m[0:2], sums_f[1:2]], axis=0).reshape(N_DEV, n_ada // LANE, LANE)
    ada_rows = -(-(n_ada // LANE) // 8) * 8
    dmod = jnp.pad(dmod, ((0, 0), (0, ada_rows - n_ada // LANE), (0, 0))).reshape(N_DEV * ada_rows, LANE)
    parts = [dmod, _rows(sums_x[2:3]), _rows(sums_m[4:5]), _rows(sums_m[2:3]), _rows(sums_f[2:3]),
             _rows(jnp.sum(dgh_heads, axis=0)), _rows(dg_sgu), _rows(dw_sp), _rows(dbst.T)]
    n_common = sum(p.shape[0] for p in parts)
    payload = jnp.concatenate(parts + [_rows(dlb)], axis=0)
    gathered = _all_gather_small("gather_small_grads", payload)

    dmod_mine = lax.dynamic_slice_in_dim(gathered[:, :N_DEV * ada_rows, :].reshape(N_DEV, N_DEV, ada_rows * LANE),
                                         me, 1, axis=1)[:, 0, :n_ada]
    ada_out = [o[None] for o in _adamw_ada(sc_all.T, dmod_mine, w_ada[0], m_w_ada[0], v_w_ada[0])]

    def pack(b_, g1_, g2_, g3_, g4_, gh_, gs_, ws_, bs_):
        b3 = b_.reshape(N_DEV, n_ada // LANE, LANE)
        b3 = jnp.pad(b3, ((0, 0), (0, ada_rows - n_ada // LANE), (0, 0))).reshape(N_DEV * ada_rows, LANE)
        return jnp.concatenate([b3, _rows(g1_), _rows(g2_), _rows(g3_), _rows(g4_), _rows(gh_), _rows(gs_),
                                _rows(ws_), _rows(bs_)], axis=0)

    small_w = (b_ada, g_pre_mix, g_post_mix, g_pre_ffn, g_post_ffn, g_hgrn_norm, g_sgu_norm, w_spatial, b_spatial)
    small_m = (m_b_ada, m_g_pre_mix, m_g_post_mix, m_g_pre_ffn, m_g_post_ffn, m_g_hgrn_norm, m_g_sgu_norm, m_w_spatial, m_b_spatial)
    small_v = (v_b_ada, v_g_pre_mix, v_g_post_mix, v_g_pre_ffn, v_g_post_ffn, v_g_hgrn_norm, v_g_sgu_norm, v_w_spatial, v_b_spatial)
    packed = _adamw_small(gathered[:, :n_common, :], pack(*small_w), pack(*small_m), pack(*small_v))

    def unpack(slab):
        outs, at = [], 0
        b3 = slab[:N_DEV * ada_rows].reshape(N_DEV, ada_rows, LANE)[:, :n_ada // LANE, :]
        outs.append(b3.reshape(b_ada.shape))
        at = N_DEV * ada_rows
        for ref in small_w[1:]:
            n_el = ref.size
            n_r = -(-(n_el // LANE) // 8) * 8
            outs.append(slab[at:at + n_el // LANE].reshape(ref.shape))
            at += n_r
        return outs

    small_out = [unpack(s) for s in packed]

    dlb_all = gathered[:, n_common:n_common + 2 * heads, :].reshape(N_DEV, 2, heads, LANE)
    dlb_mine = lax.dynamic_index_in_dim(dlb_all, me, axis=2, keepdims=False)
    lb_out = _adamw_lb(dlb_mine, lb_logits, m_lb_logits, v_lb_logits)

    order = ["w_ada", "b_ada", "g_pre_mix", "g_post_mix", "g_pre_ffn", "g_post_ffn", "w_in", "lb_logits", "g_hgrn_norm",
             "w_a_out", "g_sgu_norm", "w_spatial", "b_spatial", "w_b_out", "w_o", "w_ff1", "w_ff2"]
    small_names = ["b_ada", "g_pre_mix", "g_post_mix", "g_pre_ffn", "g_post_ffn", "g_hgrn_norm", "g_sgu_norm", "w_spatial", "b_spatial"]

    def leaf(kind, nm):
        if nm == "w_ada":
            return ada_out[kind]
        if nm == "lb_logits":
            return lb_out[kind]
        if nm in big_out:
            return big_out[nm][kind]
        return small_out[kind][small_names.index(nm)]

    result = [loss, grad_x[None]]
    for kind in range(4):
        result += [leaf(kind, nm) for nm in order]
    return tuple(result)
```

```python
import functools
import math

import jax
import jax.numpy as jnp
from jax import lax
from jax.experimental import pallas as pl
from jax.experimental.pallas import tpu as pltpu

F32 = jnp.float32
BF16 = jnp.bfloat16
MESH = pl.DeviceIdType.MESH
HIGHEST = lax.Precision.HIGHEST

N_DEV = 8
HEAD = 128
A_CHUNK = 32
N_MOD = 6
EPS = 1e-6
LANE = 128
VMEM_LIMIT = 56 * 1024 * 1024

ADAM_LR = 0.001
ADAM_B1 = 0.9
ADAM_B2 = 0.999
ADAM_EPS = 1e-08
ADAM_WD = 0.01
ADAM_STEP = 10

_NN = (((1,), (0,)), ((), ()))
_NT = (((1,), (1,)), ((), ()))
_TN = (((0,), (0,)), ((), ()))


def _dot(a, b, dims=_NN, precision=None):
    return lax.dot_general(a, b, dims, preferred_element_type=F32, precision=precision)


def _bdot(a, b, dims=_NN):
    return _dot(a.astype(BF16), b.astype(BF16), dims)


def _params(n_grid):
    return pltpu.CompilerParams(dimension_semantics=("arbitrary",) * n_grid, vmem_limit_bytes=VMEM_LIMIT)


def _dev_index():
    return lax.axis_index("x") * 4 + lax.axis_index("y") * 2 + lax.axis_index("c")


def _dev_coords(i):
    return (i // 4, (i // 2) % 2, i % 2)


def _sigmoid(x):
    return 1.0 / (1.0 + jnp.exp(-x))


def _erf(x):
    ax = jnp.abs(x)
    t = 1.0 / (1.0 + 0.3275911 * ax)
    poly = ((((1.061405429 * t - 1.453152027) * t + 1.421413741) * t - 0.284496736) * t + 0.254829592) * t
    y = 1.0 - poly * jnp.exp(-ax * ax)
    return jnp.where(x < 0, -y, y)


def _gelu_and_grad(x):
    cdf = 0.5 * (1.0 + _erf(x * (2.0 ** -0.5)))
    pdf = jnp.exp(-0.5 * x * x) * (1.0 / math.sqrt(2.0 * math.pi))
    return x * cdf, cdf + x * pdf


def _rms(x):
    return lax.rsqrt(jnp.mean(x * x, axis=-1, keepdims=True) + EPS)


def _colsum(x):
    return jnp.sum(x, axis=0, keepdims=True)


def _tile(n, want):
    if n <= want:
        return n
    t = (want // LANE) * LANE
    while n % t:
        t -= LANE
    assert t > 0, (n, want)
    return t


def _all_gather_small(name, payload):
    rows = payload.shape[0]

    def body(p_ref, out_ref, send_sems, recv_sems, local_sem):
        me = _dev_index()
        mine = pltpu.make_async_copy(p_ref, out_ref.at[me], local_sem)
        mine.start()
        sends = []
        for r in range(1, N_DEV):
            peer = (me + r) % N_DEV
            cp = pltpu.make_async_remote_copy(
                src_ref=p_ref, dst_ref=out_ref.at[me], send_sem=send_sems.at[r - 1], recv_sem=recv_sems.at[r - 1],
                device_id=_dev_coords(peer), device_id_type=MESH)
            cp.start()
            sends.append(cp)
        for r in range(1, N_DEV):
            src = (me + N_DEV - r) % N_DEV
            pltpu.make_async_remote_copy(
                src_ref=p_ref, dst_ref=out_ref.at[src], send_sem=send_sems.at[r - 1], recv_sem=recv_sems.at[r - 1],
                device_id=_dev_coords(src), device_id_type=MESH).wait_recv()
        for cp in sends:
            cp.wait_send()
        mine.wait()

    return pl.pallas_call(
        body, name=name,
        out_shape=jax.ShapeDtypeStruct((N_DEV, rows, LANE), F32),
        in_specs=[pl.BlockSpec(memory_space=pltpu.VMEM)],
        out_specs=pl.BlockSpec(memory_space=pltpu.VMEM),
        scratch_shapes=[pltpu.SemaphoreType.DMA((N_DEV - 1,)), pltpu.SemaphoreType.DMA((N_DEV - 1,)),
                        pltpu.SemaphoreType.DMA],
        compiler_params=pltpu.CompilerParams(vmem_limit_bytes=VMEM_LIMIT),
    )(payload)


def _region(ref, dev, axis, n):
    start = pl.multiple_of(dev * n, LANE if axis == 1 else 16)
    return ref.at[:, pl.ds(start, n)] if axis == 1 else ref.at[pl.ds(start, n), :]


class _Exchange:
    def __init__(self, arrays, out_shapes, sems, start, finish):
        self.arrays, self.out_shapes, self.sems, self.start, self.finish = arrays, out_shapes, sems, start, finish


def _gather_plan(shards, axes):
    n_w = len(shards)
    fulls = []
    for s, ax in zip(shards, axes):
        shp = (s.shape[0], s.shape[1] * N_DEV) if ax == 1 else (s.shape[0] * N_DEV, s.shape[1])
        fulls.append(jax.ShapeDtypeStruct(shp, BF16))
    widths = [s.shape[ax] for s, ax in zip(shards, axes)]

    def places():
        x, y, c = lax.axis_index("x"), lax.axis_index("y"), lax.axis_index("c")
        chips = [(1 - x, y), (x, 1 - y), (1 - x, 1 - y)]
        return (x, y, c), (x, y, 1 - c), chips

    def index(p):
        return p[0] * 4 + p[1] * 2 + p[2]

    def copy(w, k, s_refs, f_refs, sems, block, to, from_shard):
        send_sems, recv_sems, _ = sems
        dst = _region(f_refs[w], index(block), axes[w], widths[w])
        return pltpu.make_async_remote_copy(
            src_ref=s_refs[w] if from_shard else dst, dst_ref=dst,
            send_sem=send_sems.at[w, k], recv_sem=recv_sems.at[w, k], device_id=to, device_id_type=MESH)

    def local(w, s_refs, f_refs, sems, me):
        return pltpu.make_async_copy(s_refs[w], _region(f_refs[w], index(me), axes[w], widths[w]), sems[2].at[w])

    def start(s_refs, f_refs, sems):
        me, sib, chips = places()
        for w in range(n_w):
            local(w, s_refs, f_refs, sems, me).start()
            copy(w, 0, s_refs, f_refs, sems, me, sib, True).start()
            for j, chip in enumerate(chips):
                copy(w, 1 + j, s_refs, f_refs, sems, me, (*chip, me[2]), True).start()

    def finish(s_refs, f_refs, sems):
        me, sib, chips = places()
        for w in range(n_w):
            for j, chip in enumerate(chips):
                copy(w, 1 + j, s_refs, f_refs, sems, (*chip, me[2]), me, True).wait_recv()
                copy(w, 4 + j, s_refs, f_refs, sems, (*chip, me[2]), sib, False).start()
        for w in range(n_w):
            copy(w, 0, s_refs, f_refs, sems, sib, me, True).wait_recv()
            for j, chip in enumerate(chips):
                copy(w, 4 + j, s_refs, f_refs, sems, (*chip, sib[2]), me, False).wait_recv()
        for w in range(n_w):
            for k in range(N_DEV - 1):
                copy(w, k, s_refs, f_refs, sems, me, sib, True).wait_send()
            local(w, s_refs, f_refs, sems, me).wait()

    sems = [pltpu.SemaphoreType.DMA((n_w, N_DEV - 1)), pltpu.SemaphoreType.DMA((n_w, N_DEV - 1)),
            pltpu.SemaphoreType.DMA((n_w,))]
    return _Exchange(list(shards), fulls, sems, start, finish)


def _scatter_plan(grads, axes):
    n_w = len(grads)
    lands = []
    for g, ax in zip(grads, axes):
        shp = (g.shape[0], g.shape[1] // N_DEV) if ax == 1 else (g.shape[0] // N_DEV, g.shape[1])
        lands.append(jax.ShapeDtypeStruct((N_DEV - 1,) + shp, BF16))
    widths = [ld.shape[1 + ax] for ld, ax in zip(lands, axes)]

    def copy(w, r, g_refs, l_refs, sems, block, to):
        return pltpu.make_async_remote_copy(
            src_ref=_region(g_refs[w], block, axes[w], widths[w]), dst_ref=l_refs[w].at[r - 1],
            send_sem=sems[0].at[w * (N_DEV - 1) + r - 1], recv_sem=sems[1].at[w * (N_DEV - 1) + r - 1],
            device_id=_dev_coords(to), device_id_type=MESH)

    def start(g_refs, l_refs, sems):
        me = _dev_index()
        for w in range(n_w):
            for r in range(1, N_DEV):
                owner = (me + r) % N_DEV
                copy(w, r, g_refs, l_refs, sems, owner, owner).start()

    def finish(g_refs, l_refs, sems):
        me = _dev_index()
        for w in range(n_w):
            for r in range(1, N_DEV):
                copy(w, r, g_refs, l_refs, sems, me, (me + N_DEV - r) % N_DEV).wait_recv()
        for w in range(n_w):
            for r in range(1, N_DEV):
                copy(w, r, g_refs, l_refs, sems, me, (me + r) % N_DEV).wait_send()

    sems = [pltpu.SemaphoreType.DMA((n_w * (N_DEV - 1),)), pltpu.SemaphoreType.DMA((n_w * (N_DEV - 1),))]
    return _Exchange(list(grads), lands, sems, start, finish)


def _run_exchange(name, plan):
    n_in, n_out = len(plan.arrays), len(plan.out_shapes)

    def body(*refs):
        ins, outs, sems = refs[:n_in], refs[n_in:n_in + n_out], refs[n_in + n_out:]
        plan.start(ins, outs, sems)
        plan.finish(ins, outs, sems)

    any_spec = pl.BlockSpec(memory_space=pl.ANY)
    return pl.pallas_call(
        body, name=name, out_shape=plan.out_shapes,
        in_specs=[any_spec] * n_in, out_specs=[any_spec] * n_out, scratch_shapes=plan.sems,
    )(*plan.arrays)


_NO_EXCHANGE = _Exchange([], [], [], lambda i, o, s: None, lambda i, o, s: None)


def _direct_gather_plan(shards, axes):
    n_w = len(shards)
    fulls = []
    for s, ax in zip(shards, axes):
        shp = (s.shape[0], s.shape[1] * N_DEV) if ax == 1 else (s.shape[0] * N_DEV, s.shape[1])
        fulls.append(jax.ShapeDtypeStruct(shp, BF16))
    widths = [s.shape[ax] for s, ax in zip(shards, axes)]

    def copy(w, r, s_refs, f_refs, sems, block, to):
        return pltpu.make_async_remote_copy(
            src_ref=s_refs[w], dst_ref=_region(f_refs[w], block, axes[w], widths[w]),
            send_sem=sems[0].at[w * (N_DEV - 1) + r - 1], recv_sem=sems[1].at[w * (N_DEV - 1) + r - 1],
            device_id=_dev_coords(to), device_id_type=MESH)

    def start(s_refs, f_refs, sems):
        me = _dev_index()
        for w in range(n_w):
            for r in range(1, N_DEV):
                copy(w, r, s_refs, f_refs, sems, me, (me + r) % N_DEV).start()

    def finish(s_refs, f_refs, sems):
        me = _dev_index()
        for w in range(n_w):
            for r in range(1, N_DEV):
                src = (me + N_DEV - r) % N_DEV
                copy(w, r, s_refs, f_refs, sems, src, src).wait_recv()
        for w in range(n_w):
            for r in range(1, N_DEV):
                copy(w, r, s_refs, f_refs, sems, me, (me + r) % N_DEV).wait_send()

    sems = [pltpu.SemaphoreType.DMA((n_w * (N_DEV - 1),)), pltpu.SemaphoreType.DMA((n_w * (N_DEV - 1),))]
    return _Exchange(list(shards), fulls, sems, start, finish)


_HBM = pl.BlockSpec(memory_space=pltpu.HBM)
_SEM = pl.BlockSpec(memory_space=pltpu.SEMAPHORE)
_EFFECT = pltpu.SideEffectType.DATAFLOW_SIDE_EFFECTING


def _split_start(name, plan):
    n_in, n_out, n_sem = len(plan.arrays), len(plan.out_shapes), len(plan.sems)

    def body(*refs):
        ins, lands = refs[:n_in], refs[n_in:n_in + n_out]
        sems = refs[n_in + n_out:n_in + n_out + n_sem]
        token = refs[-1]
        plan.start(ins, lands, sems)
        token[...] = jnp.zeros_like(token)

    hbm = lambda a: pltpu.HBM(a.shape, a.dtype)
    results = pl.pallas_call(
        body, name=name,
        out_shape=tuple(plan.sems) + tuple(hbm(a) for a in plan.arrays) + tuple(hbm(a) for a in plan.out_shapes)
        + (jax.ShapeDtypeStruct((8, LANE), F32),),
        in_specs=(_HBM,) * (n_in + n_out),
        out_specs=(_SEM,) * n_sem + (_HBM,) * (n_in + n_out) + (pl.BlockSpec(memory_space=pltpu.VMEM),),
        input_output_aliases={i: n_sem + i for i in range(n_in + n_out)},
        compiler_params=pltpu.CompilerParams(has_side_effects=_EFFECT),
    )(*[pltpu.with_memory_space_constraint(a, pltpu.HBM) for a in plan.arrays],
      *[pltpu.with_memory_space_constraint(lax.empty(a.shape, a.dtype), pltpu.HBM) for a in plan.out_shapes])
    return results[:n_sem], results[n_sem:n_sem + n_in + n_out], results[-1]


def _split_wait(name, plan, sems, thru, after):
    n_in, n_out, n_sem = len(plan.arrays), len(plan.out_shapes), len(plan.sems)

    def body(*refs):
        ins, lands = refs[:n_in], refs[n_in:n_in + n_out]
        sem_refs = refs[n_in + n_out:n_in + n_out + n_sem]
        plan.finish(ins, lands, sem_refs)

    hbm = lambda a: pltpu.HBM(a.shape, a.dtype)
    results = pl.pallas_call(
        body, name=name,
        out_shape=tuple(hbm(a) for a in plan.arrays) + tuple(hbm(a) for a in plan.out_shapes),
        in_specs=(_HBM,) * (n_in + n_out) + (_SEM,) * n_sem + (pl.BlockSpec(memory_space=pl.ANY),),
        out_specs=(_HBM,) * (n_in + n_out),
        input_output_aliases={i: i for i in range(n_in + n_out)},
        compiler_params=pltpu.CompilerParams(has_side_effects=_EFFECT),
    )(*thru, *sems, after)
    return results[:n_in], results[n_in:]


def _place_own(name, fulls, shards, axes):
    n_w = len(shards)

    def body(*refs):
        s_refs, f_refs = refs[:n_w], refs[2 * n_w:3 * n_w]
        sems = refs[-1]
        me_i = _dev_index()
        copies = [pltpu.make_async_copy(s_refs[w], _region(f_refs[w], me_i, axes[w], shards[w].shape[axes[w]]), sems.at[w])
                  for w in range(n_w)]
        for cp in copies:
            cp.start()
        for cp in copies:
            cp.wait()

    any_spec = pl.BlockSpec(memory_space=pl.ANY)
    return pl.pallas_call(
        body, name=name, out_shape=[jax.ShapeDtypeStruct(f.shape, f.dtype) for f in fulls],
        in_specs=[any_spec] * (2 * n_w), out_specs=[any_spec] * n_w,
        input_output_aliases={n_w + w: w for w in range(n_w)},
        scratch_shapes=[pltpu.SemaphoreType.DMA((n_w,))],
    )(*shards, *fulls)


def _mm(name, a, b, dims, m, n, k, tm, tn, tk, extras, outs, epilogue, row_chunk=None, exchange=None):
    ni, nj, nk = m // tm, n // tn, k // tk
    ne, no = len(extras), len(outs)
    xin = len(exchange.arrays) if exchange else 0
    xout = len(exchange.out_shapes) if exchange else 0
    if dims == _TN:
        a_spec = pl.BlockSpec((tk, tm), lambda i, j, kk: (kk, i))
    else:
        a_spec = pl.BlockSpec((tm, tk), lambda i, j, kk: (i, kk))
    if dims == _NT:
        b_spec = pl.BlockSpec((tn, tk), lambda i, j, kk: (j, kk))
    else:
        b_spec = pl.BlockSpec((tk, tn), lambda i, j, kk: (kk, j))
    chunks = [slice(None)] if row_chunk is None else [slice(r, r + row_chunk) for r in range(0, tm, row_chunk)]

    def lift(index_map):
        return lambda i, j, kk: index_map(i, j)

    def body(a_ref, b_ref, *rest):
        extra_refs, rest = rest[:ne], rest[ne:]
        xin_refs, rest = rest[:xin], rest[xin:]
        out_refs, rest = rest[:no], rest[no:]
        xout_refs, rest = rest[:xout], rest[xout:]
        i, j, kk = pl.program_id(0), pl.program_id(1), pl.program_id(2)
        if exchange:
            sem_refs = rest[1:] if nk > 1 else rest

            @pl.when((i == 0) & (j == 0) & (kk == 0))
            def _():
                exchange.start(xin_refs, xout_refs, sem_refs)

        part = _dot(a_ref[...], b_ref[...], dims)
        if nk == 1:
            for rows in chunks:
                epilogue(part[rows], i, j, extra_refs, out_refs, rows)
        else:
            acc_ref = rest[0]

            @pl.when(kk == 0)
            def _():
                acc_ref[...] = part

            @pl.when(kk > 0)
            def _():
                acc_ref[...] += part

            @pl.when(kk == nk - 1)
            def _():
                for rows in chunks:
                    epilogue(acc_ref[rows, :], i, j, extra_refs, out_refs, rows)

        if exchange:
            @pl.when((i == ni - 1) & (j == nj - 1) & (kk == nk - 1))
            def _():
                exchange.finish(xin_refs, xout_refs, sem_refs)

    any_spec = pl.BlockSpec(memory_space=pl.ANY)
    results = pl.pallas_call(
        body, name=name,
        grid=(ni, nj, nk),
        in_specs=[a_spec, b_spec] + [pl.BlockSpec(bs, lift(im)) for _, bs, im in extras] + [any_spec] * xin,
        out_specs=[pl.BlockSpec(bs, lift(im)) for _, bs, im in outs] + [any_spec] * xout,
        out_shape=[sd for sd, _, _ in outs] + (list(exchange.out_shapes) if exchange else []),
        scratch_shapes=([pltpu.VMEM((tm, tn), F32)] if nk > 1 else []) + (list(exchange.sems) if exchange else []),
        compiler_params=_params(3),
    )(a, b, *[arr for arr, _, _ in extras], *(exchange.arrays if exchange else []))
    return (results[:no], results[no:]) if exchange else results


def _after(token):
    return [(token, (8, LANE), lambda i, j: (0, 0))]


def _grad_w(name, a, dc, token=None, tm=512, tn=1024):
    t, m = a.shape
    n = dc.shape[1]
    tm, tn = _tile(m, tm), _tile(n, tn)

    def epilogue(acc, i, j, extra_refs, out_refs, rows):
        out_refs[0][...] = acc
        out_refs[1][...] = acc.astype(BF16)

    blk = ((tm, tn), lambda i, j: (i, j))
    return _mm(name, a, dc, _TN, m, n, t, tm, tn, t, _after(token) if token is not None else [],
               [(jax.ShapeDtypeStruct((m, n), F32),) + blk, (jax.ShapeDtypeStruct((m, n), BF16),) + blk], epilogue)


def _cast_bf16(name, w):
    r, c = w.shape
    tr = _tile(r, 256)
    return pl.pallas_call(
        lambda w_ref, o_ref: o_ref.__setitem__(Ellipsis, w_ref[...].astype(BF16)), name=name,
        grid=(r // tr,), in_specs=[pl.BlockSpec((tr, c), lambda i: (i, 0))],
        out_specs=pl.BlockSpec((tr, c), lambda i: (i, 0)), out_shape=jax.ShapeDtypeStruct((r, c), BF16),
        compiler_params=_params(1),
    )(w)


def _prep_small(c_row, lb_logits):
    d = c_row.shape[1]
    rows = d // LANE

    def body(c_ref, l_ref, o_ref):
        cv = c_ref[...]
        o_ref[0:rows, :] = cv * _sigmoid(cv)
        lbs = [_sigmoid(l_ref[dr][0:1, :] - l_ref[dr][1:2, :]) for dr in range(2)]
        o_ref[rows:rows + 8, :] = jnp.concatenate(lbs + [jnp.zeros((6, LANE), F32)], axis=0)

    return pl.pallas_call(
        body, name="prep_small", out_shape=jax.ShapeDtypeStruct((rows + 8, LANE), F32),
    )(c_row.reshape(rows, LANE), lb_logits)


def _mod_shard(sc_all, w_ada_shard, b_shard):
    d, n = w_ada_shard.shape
    tn = _tile(n, 512)

    def body(s_ref, w_ref, b_ref, o_ref):
        o_ref[...] = _dot(s_ref[...], w_ref[...], precision=HIGHEST) + b_ref[...]

    return pl.pallas_call(
        body, name="mod_shard", grid=(n // tn,),
        in_specs=[pl.BlockSpec((N_DEV, d), lambda j: (0, 0)), pl.BlockSpec((d, tn), lambda j: (0, j)),
                  pl.BlockSpec((1, tn), lambda j: (0, j))],
        out_specs=pl.BlockSpec((N_DEV, tn), lambda j: (0, j)),
        out_shape=jax.ShapeDtypeStruct((N_DEV, n), F32), compiler_params=_params(1),
    )(sc_all, w_ada_shard, b_shard)


def _norm_mod(x, gain, shift, scale):
    t, d = x.shape
    tm = _tile(t, 512)

    def body(x_ref, g_ref, sh_ref, sc_ref, o_ref):
        xv = x_ref[...]
        o_ref[...] = ((xv * _rms(xv) * g_ref[...]) * (1.0 + sc_ref[...]) + sh_ref[...]).astype(BF16)

    vec = pl.BlockSpec((1, d), lambda i: (0, 0))
    return pl.pallas_call(
        body, name="norm_mod", grid=(t // tm,),
        in_specs=[pl.BlockSpec((tm, d), lambda i: (i, 0)), vec, vec, vec],
        out_specs=pl.BlockSpec((tm, d), lambda i: (i, 0)), out_shape=jax.ShapeDtypeStruct((t, d), BF16),
        compiler_params=_params(1),
    )(x, gain, shift, scale)


def _chunk_masks():
    row = lax.broadcasted_iota(jnp.int32, (HEAD, HEAD), 0)
    col = lax.broadcasted_iota(jnp.int32, (HEAD, HEAD), 1)
    same = (row // A_CHUNK) == (col // A_CHUNK)
    return same & (col <= row), same & (col >= row)


def _ones(mask):
    return jnp.where(mask, 1.0, 0.0).astype(BF16)


def _dot_split(ones_bf16, x):
    hi = x.astype(BF16)
    lo = (x - hi.astype(F32)).astype(BF16)
    return _dot(ones_bf16, hi) + _dot(ones_bf16, lo)


def _hgrn_block(direction, f, lb, cum2):
    sf = _sigmoid(f)
    big_f = lb + (1.0 - lb) * sf
    k = (1.0 - lb) * (1.0 - sf)
    lf = jnp.log(big_f)
    both = _dot_split(cum2, lf)
    cf, cr = both[:HEAD], both[HEAD:]
    b, rest = (cf, cr - lf) if direction == 0 else (cr, cf - lf)
    return k, sf, big_f, jnp.exp(b), jnp.exp(-b), jnp.exp(rest)


def _hgrn_fwd(proj, lb, g_norm, width, exchange):
    t = proj.shape[0]
    heads = width // HEAD
    nb, nc = t // HEAD, t // A_CHUNK
    ua = 2 if nb % 2 == 0 else 1
    ub = 8 if nc % 8 == 0 else 4
    q_scale = HEAD ** -0.5
    xin, xout = len(exchange.arrays), len(exchange.out_shapes)

    def body(q_ref, ffw_ref, fbw_ref, v_ref, og_ref, lb_ref, g_ref, *rest):
        xin_refs, rest = rest[:xin], rest[xin:]
        outa_ref, osum_ref = rest[:2]
        xout_refs, rest = rest[2:2 + xout], rest[2 + xout:]
        qd_s, ke_s, dc_s, o_s = rest[:4]
        sem_refs = rest[4:]
        h = pl.program_id(0)

        @pl.when(h == 0)
        def _():
            exchange.start(xin_refs, xout_refs, sem_refs)

        tril, triu = _chunk_masks()
        cum2 = jnp.concatenate([_ones(tril), _ones(triu)], axis=0)
        f_refs = (ffw_ref, fbw_ref)
        lbs = (lb_ref[0:1, :], lb_ref[1:2, :])

        def phase_a(it, carry):
            loaded = []
            for u in range(ua):
                rows = pl.ds(pl.multiple_of((it * ua + u) * HEAD, HEAD), HEAD)
                loaded.append((rows, q_ref[rows, :], v_ref[rows, :], ffw_ref[rows, :], fbw_ref[rows, :]))
            chains = [(d, rows, qv * q_scale, vv.astype(BF16), fv)
                      for rows, qv, vv, f0, f1 in loaded for d, fv in ((0, f0), (1, f1))]
            blocks = [_hgrn_block(d, fv, lbs[d], cum2) for d, _, _, _, fv in chains]
            scaled = [(qv * eb, k * enb, k * erest, eb * erest)
                      for (_, _, qv, _, _), (k, _, _, eb, enb, erest) in zip(chains, blocks)]
            atts = [jnp.where(tril if d == 0 else triu, _bdot(qd, kd, _NT), 0.0)
                    for (d, _, _, _, _), (qd, kd, _, _) in zip(chains, scaled)]
            intras = [_bdot(att, vv) for att, (_, _, _, vv, _) in zip(atts, chains)]
            results = [(d, rows, o_intra, qd.astype(BF16), ke.astype(BF16), decay)
                       for (d, rows, _, _, _), (qd, _, ke, decay), o_intra in zip(chains, scaled, intras)]
            for d, rows, o_intra, qd16, ke16, decay in results:
                o_s[d, rows, :] = o_intra
                qd_s[d, rows, :] = qd16
                ke_s[d, rows, :] = ke16
                dc_s[d, rows, :] = decay
            return carry

        lax.fori_loop(0, nb // ua, phase_a, 0)

        def phase_b(it, states):
            loaded = []
            for u in range(ub):
                n = it * ub + u
                for d in range(2):
                    c = n if d == 0 else nc - 1 - n
                    start = pl.multiple_of(c * A_CHUNK, A_CHUNK)
                    rows = pl.ds(start, A_CHUNK)
                    loaded.append((d, rows, qd_s[d, rows, :], ke_s[d, rows, :], v_ref[rows, :],
                                   dc_s[d, pl.ds(start, 1), :], o_s[d, rows, :]))
            increments = [_dot(vv.astype(BF16), ke16, _TN) for _, _, _, ke16, vv, _, _ in loaded]
            states = list(states)
            befores = []
            for (d, _, _, _, _, decay, _), inc in zip(loaded, increments):
                befores.append(states[d].astype(BF16))
                states[d] = states[d] * decay + inc
            inters = [_dot(qd16, before, _NT) for (_, _, qd16, _, _, _, _), before in zip(loaded, befores)]
            for (d, rows, _, _, _, _, o_intra), o_inter in zip(loaded, inters):
                o_s[d, rows, :] = o_intra + o_inter
            return tuple(states)

        zero_state = jnp.zeros((HEAD, HEAD), F32)
        lax.fori_loop(0, nc // ub, phase_b, (zero_state, zero_state))

        def phase_c(i, carry):
            rows = pl.ds(pl.multiple_of(i * HEAD, HEAD), HEAD)
            o = o_s[0, rows, :] + o_s[1, rows, :]
            osum_ref[rows, :] = o
            og = og_ref[rows, :]
            outa_ref[rows, :] = (o * _rms(o) * g_ref[...] * (og * _sigmoid(og))).astype(BF16)
            return carry

        lax.fori_loop(0, nb, phase_c, 0)

        @pl.when(h == heads - 1)
        def _():
            exchange.finish(xin_refs, xout_refs, sem_refs)

    def col(p):
        return pl.BlockSpec((t, HEAD), lambda h: (0, p * heads + h))

    any_spec = pl.BlockSpec(memory_space=pl.ANY)
    results = pl.pallas_call(
        body, name="hgrn_fwd", grid=(heads,),
        in_specs=[col(0), col(1), col(2), col(3), col(4),
                  pl.BlockSpec((2, HEAD), lambda h: (0, h)), pl.BlockSpec((1, HEAD), lambda h: (0, 0))] + [any_spec] * xin,
        out_specs=[pl.BlockSpec((t, HEAD), lambda h: (0, h)), pl.BlockSpec((t, HEAD), lambda h: (0, h))] + [any_spec] * xout,
        out_shape=[jax.ShapeDtypeStruct((t, width), BF16), jax.ShapeDtypeStruct((t, width), F32)] + list(exchange.out_shapes),
        scratch_shapes=[pltpu.VMEM((2, t, HEAD), BF16), pltpu.VMEM((2, t, HEAD), BF16), pltpu.VMEM((2, t, HEAD), F32),
                        pltpu.VMEM((2, t, HEAD), F32)] + list(exchange.sems),
        compiler_params=_params(1),
    )(proj, proj, proj, proj, proj, lb, g_norm, *exchange.arrays)
    return results[0], results[1], results[2:]


def _sgu_core(u_pre, v_pre, g_v, ws_ref, bst):
    u, du = _gelu_and_grad(u_pre)
    v, dv = _gelu_and_grad(v_pre)
    mu = jnp.mean(v, axis=-1, keepdims=True)
    dlt = v - mu
    rstd = lax.rsqrt(jnp.mean(dlt * dlt, axis=-1, keepdims=True) + EPS)
    vhat = dlt * rstd
    vn = vhat * g_v
    groups = vn.shape[1] // HEAD
    cols = []
    for g in range(groups):
        vm_g = _bdot(ws_ref[g], vn[:, g * HEAD:(g + 1) * HEAD]) + bst[:, g:g + 1]
        cols.append(vm_g)
    return u, du, dv, vhat, rstd, vn, jnp.concatenate(cols, axis=1)


def _sgu_fwd(proj, g_v, w_s, bst, width, z_block):
    t = proj.shape[0]

    def body(u_ref, v_ref, g_ref, ws_ref, bst_ref, o_ref):
        u, _, _, _, _, _, vm = _sgu_core(u_ref[...], v_ref[...], g_ref[...], ws_ref, bst_ref[...])
        o_ref[...] = (u * vm).astype(BF16)

    groups = width // HEAD
    return pl.pallas_call(
        body, name="sgu_fwd", grid=(t // HEAD,),
        in_specs=[pl.BlockSpec((HEAD, width), lambda i: (i, z_block)), pl.BlockSpec((HEAD, width), lambda i: (i, z_block + 1)),
                  pl.BlockSpec((1, width), lambda i: (0, 0)), pl.BlockSpec((groups, HEAD, HEAD), lambda i: (0, 0, 0)),
                  pl.BlockSpec((HEAD, groups), lambda i: (0, 0))],
        out_specs=pl.BlockSpec((HEAD, width), lambda i: (i, 0)),
        out_shape=jax.ShapeDtypeStruct((t, width), BF16), compiler_params=_params(1),
    )(proj, proj, g_v, w_s, bst)


def _sgu_bwd(proj, dout_b, dproj, g_v, w_s, w_st, bst, width, z_block):
    t = proj.shape[0]
    groups = width // HEAD
    nblk = t // HEAD

    def body(u_ref, v_ref, do_ref, g_ref, ws_ref, wst_ref, bst_ref, dproj_hbm,
             dz_ref, dg_ref, dws_ref, dbst_ref, res_s):
        i, p = pl.program_id(0), pl.program_id(1)

        @pl.when((i == 0) & (p == 0))
        def _():
            dg_ref[...] = jnp.zeros_like(dg_ref)
            dws_ref[...] = jnp.zeros_like(dws_ref)
            dbst_ref[...] = jnp.zeros_like(dbst_ref)

        @pl.when(p == 0)
        def _():
            g_v = g_ref[...]
            u, du, dv, vhat, rstd, vn, vm = _sgu_core(u_ref[...], v_ref[...], g_v, ws_ref, bst_ref[...])
            dout = do_ref[...].astype(F32)
            res_s[0] = (dout * vm * du).astype(BF16)
            dvm = dout * u
            dvn_cols = []
            for g in range(groups):
                sl = slice(g * HEAD, (g + 1) * HEAD)
                dvm_g = dvm[:, sl]
                dbst_ref[:, g:g + 1] += jnp.sum(dvm_g, axis=1, keepdims=True)
                dws_ref[g] += _bdot(dvm_g, vn[:, sl], _NT)
                dvn_cols.append(_bdot(wst_ref[g], dvm_g))
            dvn = jnp.concatenate(dvn_cols, axis=1)
            dg_ref[...] += _colsum(dvn * vhat)
            dvh = dvn * g_v
            dvg = rstd * (dvh - jnp.mean(dvh, axis=-1, keepdims=True)
                          - vhat * jnp.mean(dvh * vhat, axis=-1, keepdims=True))
            res_s[1] = (dvg * dv).astype(BF16)

        dz_ref[...] = res_s[p]

    n_in = dproj.shape[1]
    return pl.pallas_call(
        body, name="sgu_bwd", grid=(nblk, 2),
        in_specs=[pl.BlockSpec((HEAD, width), lambda i, p: (i, z_block)),
                  pl.BlockSpec((HEAD, width), lambda i, p: (i, z_block + 1)),
                  pl.BlockSpec((HEAD, width), lambda i, p: (i, 0)),
                  pl.BlockSpec((1, width), lambda i, p: (0, 0)),
                  pl.BlockSpec((groups, HEAD, HEAD), lambda i, p: (0, 0, 0)),
                  pl.BlockSpec((groups, HEAD, HEAD), lambda i, p: (0, 0, 0)),
                  pl.BlockSpec((HEAD, groups), lambda i, p: (0, 0)),
                  pl.BlockSpec(memory_space=pl.ANY)],
        out_specs=[pl.BlockSpec((HEAD, width), lambda i, p: (i, z_block + p)),
                   pl.BlockSpec((1, width), lambda i, p: (0, 0)),
                   pl.BlockSpec((groups, HEAD, HEAD), lambda i, p: (0, 0, 0)),
                   pl.BlockSpec((HEAD, groups), lambda i, p: (0, 0))],
        out_shape=[jax.ShapeDtypeStruct((t, n_in), BF16), jax.ShapeDtypeStruct((1, width), F32),
                   jax.ShapeDtypeStruct((groups, HEAD, HEAD), F32), jax.ShapeDtypeStruct((HEAD, groups), F32)],
        scratch_shapes=[pltpu.VMEM((2, HEAD, width), BF16)],
        input_output_aliases={7: 0},
        compiler_params=_params(2),
    )(proj, proj, dout_b, g_v, w_s, w_st, bst, dproj)


def _hgrn_bwd(proj, osum, dout_a, dproj, lb, g_norm, width, exchange):
    t = proj.shape[0]
    heads = width // HEAD
    nb = t // HEAD
    cpb = HEAD // A_CHUNK
    q_scale = HEAD ** -0.5
    xin, xout = len(exchange.arrays), len(exchange.out_shapes)

    def body(q_ref, ffw_ref, fbw_ref, v_ref, og_ref, osum_ref, douta_ref, lb_ref, g_ref, dproj_hbm, *rest):
        xin_refs, rest = rest[:xin], rest[xin:]
        out_ref, dgh_ref, dlb_ref = rest[:3]
        xout_refs, rest = rest[3:3 + xout], rest[3 + xout:]
        do_s, dq_s, dv_s, res_s, ck_s = rest[:5]
        sem_refs = rest[5:]
        h, p = pl.program_id(0), pl.program_id(1)
        f_refs = (ffw_ref, fbw_ref)

        @pl.when((h == 0) & (p == 0))
        def _():
            exchange.start(xin_refs, xout_refs, sem_refs)

        @pl.when(p == 0)
        def _():
            tril, triu = _chunk_masks()
            cum2 = jnp.concatenate([_ones(tril), _ones(triu)], axis=0)
            g_row = g_ref[...]

            def pass_norm(i, dgh):
                rows = pl.ds(pl.multiple_of(i * HEAD, HEAD), HEAD)
                o = osum_ref[rows, :]
                r = _rms(o)
                oh = o * r
                og = og_ref[rows, :]
                sg = _sigmoid(og)
                dout = douta_ref[rows, :].astype(F32)
                don = dout * (og * sg)
                res_s[4, rows, :] = (dout * (oh * g_row) * (sg * (1.0 + og * (1.0 - sg)))).astype(BF16)
                doh = don * g_row
                do_s[rows, :] = r * (doh - oh * jnp.mean(doh * oh, axis=-1, keepdims=True))
                return dgh + _colsum(don * oh)

            dgh_ref[...] = lax.fori_loop(0, nb, pass_norm, jnp.zeros((1, HEAD), F32))

            lbs = (lb_ref[0:1, :], lb_ref[1:2, :])
            zero_state = jnp.zeros((HEAD, HEAD), F32)

            def chunk_order(d):
                return list(range(cpb)) if d == 0 else list(range(cpb - 1, -1, -1))

            def chunk(x, j):
                return x[j * A_CHUNK:(j + 1) * A_CHUNK, :]

            def decay_row(e_big, j):
                return e_big[j * A_CHUNK:j * A_CHUNK + 1, :]

            def cat(parts):
                return jnp.concatenate([parts[j] for j in range(cpb)], axis=0)

            def block_states(d, start, incs, e_big):
                befores, st = {}, start
                for j in chunk_order(d):
                    befores[j] = st
                    st = st * decay_row(e_big, j) + incs[j]
                return befores, st

            def pass_states(it, states):
                loaded = []
                for d in range(2):
                    blk = it if d == 0 else nb - 1 - it
                    rows = pl.ds(pl.multiple_of(blk * HEAD, HEAD), HEAD)
                    loaded.append((d, blk, f_refs[d][rows, :], v_ref[rows, :]))
                blocks = [_hgrn_block(d, fv, lbs[d], cum2) for d, _, fv, _ in loaded]
                incs = [{j: _bdot(chunk(vv, j), chunk(k * erest, j), _TN) for j in range(cpb)}
                        for (_, _, _, vv), (k, _, _, _, _, erest) in zip(loaded, blocks)]
                after = [block_states(d, states[d], inc, eb * erest)[1]
                         for (d, _, _, _), (_, _, _, eb, _, erest), inc in zip(loaded, blocks, incs)]
                for d, blk, _, _ in loaded:
                    ck_s[d, blk] = states[d]
                return tuple(after)

            lax.fori_loop(0, nb, pass_states, (zero_state, zero_state))

            def pass_back(it, carry):
                gts, dlb = [carry[0], carry[1]], carry[2]
                loaded = []
                for d in range(2):
                    blk = nb - 1 - it if d == 0 else it
                    rows = pl.ds(pl.multiple_of(blk * HEAD, HEAD), HEAD)
                    loaded.append((d, rows, f_refs[d][rows, :], q_ref[rows, :], v_ref[rows, :], do_s[rows, :], ck_s[d, blk]))
                blocks = [_hgrn_block(d, fv, lbs[d], cum2) for d, _, fv, _, _, _, _ in loaded]
                scaled = []
                for (_, _, _, qv, _, _, _), (k, _, _, eb, enb, erest) in zip(loaded, blocks):
                    qh = qv * q_scale
                    scaled.append((qh, qh * eb, k * enb, k * erest, eb * erest))
                masks = [tril if d == 0 else triu for d, *_ in loaded]
                atts = [jnp.where(m, _bdot(qd, kd, _NT), 0.0) for m, (_, qd, kd, _, _) in zip(masks, scaled)]
                datts = [jnp.where(m, _bdot(do, vv, _NT), 0.0) for m, (_, _, _, _, vv, do, _) in zip(masks, loaded)]
                dvs = [_bdot(att, do, _TN) for att, (_, _, _, _, _, do, _) in zip(atts, loaded)]
                dqds = [_bdot(datt, kd) for datt, (_, _, kd, _, _) in zip(datts, scaled)]
                dkds = [_bdot(datt, qd, _TN) for datt, (_, qd, _, _, _) in zip(datts, scaled)]
                s_incs = [{j: _bdot(chunk(vv, j), chunk(ke, j), _TN) for j in range(cpb)}
                          for (_, _, _, _, vv, _, _), (_, _, _, ke, _) in zip(loaded, scaled)]
                g_incs = [{j: _bdot(chunk(do, j), chunk(qd, j), _TN) for j in range(cpb)}
                          for (_, _, _, _, _, do, _), (_, qd, _, _, _) in zip(loaded, scaled)]
                befores, afters, g_at = [], [], []
                for (d, _, _, _, _, _, ck), (_, _, _, _, e_big), s_inc, g_inc in zip(loaded, scaled, s_incs, g_incs):
                    order = chunk_order(d)
                    before, after = block_states(d, ck, s_inc, e_big)
                    befores.append(before)
                    afters.append({j: (before[order[n + 1]] if n + 1 < cpb else after) for n, j in enumerate(order)})
                    at, gt = {}, gts[d]
                    for j in reversed(order):
                        at[j] = gt
                        gt = gt * decay_row(e_big, j) + g_inc[j]
                    gts[d] = gt
                    g_at.append(at)
                dqd_i = [{j: _bdot(chunk(do, j), before[j]) for j in range(cpb)}
                         for (_, _, _, _, _, do, _), before in zip(loaded, befores)]
                dv_i = [{j: _bdot(chunk(ke, j), at[j], _NT) for j in range(cpb)}
                        for (_, _, _, ke, _), at in zip(scaled, g_at)]
                dke = [{j: _bdot(chunk(vv, j), at[j]) for j in range(cpb)}
                       for (_, _, _, _, vv, _, _), at in zip(loaded, g_at)]
                results, new = [], []
                for n, ((d, rows, _, _, _, _, _), (k, sf, big_f, eb, enb, erest), (qh, _, _, _, _)) in enumerate(
                        zip(loaded, blocks, scaled)):
                    dqh = (dqds[n] + cat(dqd_i[n])) * eb
                    dk = dkds[n] * enb + cat(dke[n]) * erest
                    carry_rows = {j: jnp.broadcast_to(_colsum(g_at[n][j] * afters[n][j]), (A_CHUNK, HEAD))
                                  for j in range(cpb)}
                    dlf = _dot_split(_ones(triu if d == 0 else tril), qh * dqh - k * dk) + cat(carry_rows)
                    common = dlf / big_f - dk
                    results.append((d, rows, (k * sf * common).astype(BF16), dqh.astype(BF16),
                                    (dvs[n] + cat(dv_i[n])).astype(BF16)))
                    new.append(_colsum((1.0 - sf) * common))
                for d, rows, df16, dq16, dv16 in results:
                    res_s[1 + d, rows, :] = df16
                    dq_s[d, rows, :] = dq16
                    dv_s[d, rows, :] = dv16
                return gts[0], gts[1], dlb + jnp.concatenate(new, axis=0)

            dlb_ref[...] = lax.fori_loop(0, nb, pass_back, (zero_state, zero_state, jnp.zeros((2, HEAD), F32)))[2]

            def pass_out(i, carry):
                rows = pl.ds(pl.multiple_of(i * HEAD, HEAD), HEAD)
                dq = dq_s[0, rows, :].astype(F32) + dq_s[1, rows, :].astype(F32)
                res_s[0, rows, :] = (dq * q_scale).astype(BF16)
                res_s[3, rows, :] = (dv_s[0, rows, :].astype(F32) + dv_s[1, rows, :].astype(F32)).astype(BF16)
                return carry

            lax.fori_loop(0, nb, pass_out, 0)

        out_ref[...] = res_s[p]

        @pl.when((h == heads - 1) & (p == 4))
        def _():
            exchange.finish(xin_refs, xout_refs, sem_refs)

    def col(pp):
        return pl.BlockSpec((t, HEAD), lambda h, p: (0, pp * heads + h))

    n_in = dproj.shape[1]
    any_spec = pl.BlockSpec(memory_space=pl.ANY)
    results = pl.pallas_call(
        body, name="hgrn_bwd", grid=(heads, 5),
        in_specs=[col(0), col(1), col(2), col(3), col(4),
                  pl.BlockSpec((t, HEAD), lambda h, p: (0, h)), pl.BlockSpec((t, HEAD), lambda h, p: (0, h)),
                  pl.BlockSpec((2, HEAD), lambda h, p: (0, h)), pl.BlockSpec((1, HEAD), lambda h, p: (0, 0)),
                  any_spec] + [any_spec] * xin,
        out_specs=[pl.BlockSpec((t, HEAD), lambda h, p: (0, p * heads + h)),
                   pl.BlockSpec((None, 1, HEAD), lambda h, p: (h, 0, 0)),
                   pl.BlockSpec((2, HEAD), lambda h, p: (0, h))] + [any_spec] * xout,
        out_shape=[jax.ShapeDtypeStruct((t, n_in), BF16), jax.ShapeDtypeStruct((heads, 1, HEAD), F32),
                   jax.ShapeDtypeStruct((2, width), F32)] + list(exchange.out_shapes),
        scratch_shapes=[pltpu.VMEM((t, HEAD), F32), pltpu.VMEM((2, t, HEAD), BF16), pltpu.VMEM((2, t, HEAD), BF16),
                        pltpu.VMEM((5, t, HEAD), BF16), pltpu.VMEM((2, nb, HEAD, HEAD), F32)] + list(exchange.sems),
        input_output_aliases={9: 0},
        compiler_params=_params(2),
    )(proj, proj, proj, proj, proj, osum, dout_a, lb, g_norm, dproj, *exchange.arrays)
    return results[0], results[1], results[2], results[3:]


def _adamw(w, g, m, v):
    m = ADAM_B1 * m + (1.0 - ADAM_B1) * g
    v = ADAM_B2 * v + (1.0 - ADAM_B2) * (g * g)
    m_hat = m / (1.0 - ADAM_B1 ** ADAM_STEP)
    v_hat = v / (1.0 - ADAM_B2 ** ADAM_STEP)
    delta = -ADAM_LR * (m_hat / (jnp.sqrt(v_hat) + ADAM_EPS) + ADAM_WD * w)
    return delta, m, v


def _adamw_big(name, me, w, m, v, g_full, landing, axis):
    r, c = w.shape
    tr = _tile(r, 128)

    def body(me_ref, w_ref, m_ref, v_ref, g_ref, l_ref, og_ref, od_ref, om_ref, ov_ref):
        g = g_ref[...]
        for s in range(N_DEV - 1):
            g = g + l_ref[s].astype(F32)
        og_ref[...] = g
        od_ref[...], om_ref[...], ov_ref[...] = _adamw(w_ref[...], g, m_ref[...], v_ref[...])

    shard = pl.BlockSpec((tr, c), lambda i, me_ref: (i, 0))
    if axis == 1:
        own = pl.BlockSpec((tr, c), lambda i, me_ref: (i, me_ref[0]))
    else:
        own = pl.BlockSpec((tr, c), lambda i, me_ref: (me_ref[0] * (r // tr) + i, 0))
    grid_spec = pltpu.PrefetchScalarGridSpec(
        num_scalar_prefetch=1, grid=(r // tr,),
        in_specs=[shard, shard, shard, own, pl.BlockSpec((N_DEV - 1, tr, c), lambda i, me_ref: (0, i, 0))],
        out_specs=[shard] * 4)
    return pl.pallas_call(
        body, name=name, grid_spec=grid_spec, out_shape=[jax.ShapeDtypeStruct((r, c), F32)] * 4,
        compiler_params=_params(1),
    )(me, w, m, v, g_full, landing)


def _adamw_ada(sct, dmod_mine, w, m, v):
    d, n = w.shape
    tr = _tile(d, 256)

    def body(s_ref, dm_ref, w_ref, m_ref, v_ref, og_ref, od_ref, om_ref, ov_ref):
        g = _dot(s_ref[...], dm_ref[...], precision=HIGHEST)
        og_ref[...] = g
        od_ref[...], om_ref[...], ov_ref[...] = _adamw(w_ref[...], g, m_ref[...], v_ref[...])

    blk = pl.BlockSpec((tr, n), lambda i: (i, 0))
    return pl.pallas_call(
        body, name="adamw_ada", grid=(d // tr,),
        in_specs=[pl.BlockSpec((tr, N_DEV), lambda i: (i, 0)), pl.BlockSpec((N_DEV, n), lambda i: (0, 0)), blk, blk, blk],
        out_specs=[blk] * 4, out_shape=[jax.ShapeDtypeStruct((d, n), F32)] * 4, compiler_params=_params(1),
    )(sct, dmod_mine, w, m, v)


def _adamw_small(gathered, w, m, v):
    def body(g_ref, w_ref, m_ref, v_ref, og_ref, od_ref, om_ref, ov_ref):
        g = g_ref[0]
        for s in range(1, N_DEV):
            g = g + g_ref[s]
        og_ref[...] = g
        od_ref[...], om_ref[...], ov_ref[...] = _adamw(w_ref[...], g, m_ref[...], v_ref[...])

    return pl.pallas_call(
        body, name="adamw_small", out_shape=[jax.ShapeDtypeStruct(w.shape, F32)] * 4,
        compiler_params=pltpu.CompilerParams(vmem_limit_bytes=VMEM_LIMIT),
    )(gathered, w, m, v)


def _adamw_lb(dlb_mine, lb_logits, m, v):
    def body(d_ref, l_ref, m_ref, v_ref, og_ref, od_ref, om_ref, ov_ref):
        dlb = d_ref[0]
        for s in range(1, N_DEV):
            dlb = dlb + d_ref[s]
        for dr in range(2):
            lb = _sigmoid(l_ref[dr][0:1, :] - l_ref[dr][1:2, :])
            d0 = dlb[dr:dr + 1] * lb * (1.0 - lb)
            g = jnp.concatenate([d0, -d0], axis=0)
            og_ref[dr] = g
            od_ref[dr], om_ref[dr], ov_ref[dr] = _adamw(l_ref[dr], g, m_ref[dr], v_ref[dr])

    return pl.pallas_call(body, name="adamw_lb", out_shape=[jax.ShapeDtypeStruct(lb_logits.shape, F32)] * 4,
                          )(dlb_mine, lb_logits, m, v)


def _rows(a, pad_to=8):
    flat = a.reshape(-1, LANE)
    pad = (-flat.shape[0]) % pad_to
    return jnp.pad(flat, ((0, pad), (0, 0))) if pad else flat


def kernel(x, c, w_ada, b_ada, g_pre_mix, g_post_mix, g_pre_ffn, g_post_ffn, w_in, lb_logits, g_hgrn_norm, w_a_out, g_sgu_norm, w_spatial, b_spatial, w_b_out, w_o, w_ff1, w_ff2, loss_target, m_w_ada, m_b_ada, m_g_pre_mix, m_g_post_mix, m_g_pre_ffn, m_g_post_ffn, m_w_in, m_lb_logits, m_g_hgrn_norm, m_w_a_out, m_g_sgu_norm, m_w_spatial, m_b_spatial, m_w_b_out, m_w_o, m_w_ff1, m_w_ff2, v_w_ada, v_b_ada, v_g_pre_mix, v_g_post_mix, v_g_pre_ffn, v_g_post_ffn, v_w_in, v_lb_logits, v_g_hgrn_norm, v_w_a_out, v_g_sgu_norm, v_w_spatial, v_b_spatial, v_w_b_out, v_w_o, v_w_ff1, v_w_ff2):
    t, d = x.shape[1], x.shape[2]
    n_in = w_in.shape[2] * N_DEV
    width = (n_in - 2 * d) // 7
    heads = width // HEAD
    assert heads == N_DEV and width % LANE == 0
    d_ff = w_ff1.shape[2] * N_DEV
    n_ada = w_ada.shape[2]
    me = _dev_index()
    me_arr = me.reshape(1).astype(jnp.int32)
    x2, tgt = x[0], loss_target[0]

    big = [w_in[0], w_a_out[0], w_b_out[0], w_o[0], w_ff1[0], w_ff2[0]]
    big_axes = [1, 1, 1, 0, 1, 0]
    big_names = ["w_in", "w_a_out", "w_b_out", "w_o", "w_ff1", "w_ff2"]
    shards16 = [_cast_bf16("cast_" + nm, w) for nm, w in zip(big_names, big)]
    wf_in, = _run_exchange("gather_w_in", _gather_plan(shards16[:1], big_axes[:1]))
    wf_in, later = lax.optimization_barrier((wf_in, shards16[1:]))
    gathers = {}
    for key, lo, hi in (("mid", 1, 4), ("ff1", 4, 5), ("ff2", 5, 6)):
        plan = _direct_gather_plan(later[lo - 1:hi - 1], big_axes[lo:hi])
        gathers[key] = (plan,) + _split_start("gather_%s_start" % key, plan)

    def gathered_weights(key, lo, hi, after):
        plan, sems, thru, _ = gathers[key]
        mine, fulls = _split_wait("gather_%s_wait" % key, plan, sems, thru, after)
        return _place_own("place_" + key, fulls, mine, big_axes[lo:hi])

    c_rows = d // LANE
    small = _all_gather_small("gather_c_lb", _prep_small(c[0:1], lb_logits))
    sc_all = small[:, :c_rows, :].reshape(N_DEV, d)
    lb = jnp.transpose(small[:, c_rows:c_rows + 2, :], (1, 0, 2)).reshape(2, width)
    b_shard = lax.dynamic_slice_in_dim(b_ada, me * n_ada, n_ada, axis=1)
    mod_sh = _mod_shard(sc_all, w_ada[0], b_shard)
    mod_all = _all_gather_small("gather_mod", _rows(mod_sh))
    mod_all = mod_all[:, :N_DEV * n_ada // LANE, :].reshape(N_DEV, N_DEV, n_ada)
    mod6 = lax.dynamic_index_in_dim(mod_all, me, axis=1, keepdims=False).reshape(N_MOD, d)
    sh1, sc1, gt1, sh2, sc2, gt2 = [mod6[i:i + 1] for i in range(N_MOD)]

    a1 = _norm_mod(x2, g_pre_mix, sh1, sc1)
    tm = _tile(t, 512)

    def store_f32(acc, i, j, extra_refs, out_refs, rows):
        out_refs[0][...] = acc

    tn_in = _tile(n_in, 1024)
    started = [tok for key in ("mid", "ff1", "ff2") for tok in _after(gathers[key][3])]
    proj, = _mm("proj", a1, wf_in, _NN, t, n_in, d, tm, tn_in, d, started,
                [(jax.ShapeDtypeStruct((t, n_in), F32), (tm, tn_in), lambda i, j: (i, j))], store_f32)

    out_a, osum, _ = _hgrn_fwd(proj, lb, g_hgrn_norm, width, _NO_EXCHANGE)
    z_block = 5
    bst = b_spatial[0].T
    out_b = _sgu_fwd(proj, g_sgu_norm, w_spatial[0], bst, width, z_block)
    wf_a, wf_b, wf_o = gathered_weights("mid", 1, 4, out_b)

    tn_d = _tile(d, 512)
    blk_d = ((tm, tn_d), lambda i, j: (i, j))
    y_a, = _mm("y_a", out_a, wf_a, _NN, t, d, width, tm, tn_d, width, [],
               [(jax.ShapeDtypeStruct((t, d), F32),) + blk_d], store_f32)
    ga_blk = (5 * width + 2 * width) // tn_d
    gb_blk = ga_blk + d // tn_d

    def merge(acc, i, j, extra_refs, out_refs, rows):
        ga, gb, ya = extra_refs
        out_refs[0][...] = acc
        out_refs[1][...] = (_sigmoid(ga[...]) * ya[...] + _sigmoid(gb[...]) * acc).astype(BF16)

    y_b, merged = _mm("y_b_merge", out_b, wf_b, _NN, t, d, width, tm, tn_d, width,
                      [(proj, (tm, tn_d), lambda i, j: (i, ga_blk + j)), (proj, (tm, tn_d), lambda i, j: (i, gb_blk + j)),
                       (y_a,) + blk_d],
                      [(jax.ShapeDtypeStruct((t, d), F32),) + blk_d, (jax.ShapeDtypeStruct((t, d), BF16),) + blk_d], merge)

    tr = _tile(t, 256)
    rc = 64 if tr % 64 == 0 else None
    row_d = ((tr, d), lambda i, j: (i, 0))
    vec_d = ((1, d), lambda i, j: (0, 0))

    def post_mix(acc, i, j, extra_refs, out_refs, rows):
        x_r, gt1_r, g2_r, g3_r, sc2_r, sh2_r = extra_refs
        h1 = x_r[rows, :] + gt1_r[...] * (acc * _rms(acc) * g2_r[...])
        out_refs[0][rows, :] = acc
        out_refs[1][rows, :] = h1
        out_refs[2][rows, :] = ((h1 * _rms(h1) * g3_r[...]) * (1.0 + sc2_r[...]) + sh2_r[...]).astype(BF16)

    mo, h1, a2 = _mm("w_o_post_mix", merged, wf_o, _NN, t, d, d, tr, d, d,
                     [(x2,) + row_d, (gt1,) + vec_d, (g_post_mix,) + vec_d, (g_pre_ffn,) + vec_d, (sc2,) + vec_d, (sh2,) + vec_d],
                     [(jax.ShapeDtypeStruct((t, d), F32),) + row_d, (jax.ShapeDtypeStruct((t, d), F32),) + row_d,
                      (jax.ShapeDtypeStruct((t, d), BF16),) + row_d], post_mix, row_chunk=rc)

    tn_f = _tile(d_ff, 1024)
    blk_f = ((tm, tn_f), lambda i, j: (i, j))

    def relu_sq(acc, i, j, extra_refs, out_refs, rows):
        r = jnp.maximum(acc, 0.0)
        out_refs[0][...] = acc.astype(BF16)
        out_refs[1][...] = (r * r).astype(BF16)

    wf_1, = gathered_weights("ff1", 4, 5, a2)
    hff, act = _mm(
        "ff1", a2, wf_1, _NN, t, d_ff, d, tm, tn_f, d, [],
        [(jax.ShapeDtypeStruct((t, d_ff), BF16),) + blk_f, (jax.ShapeDtypeStruct((t, d_ff), BF16),) + blk_f], relu_sq)
    wf_2, = gathered_weights("ff2", 5, 6, act)

    sums_d = ((8, d), lambda i, j: (0, 0))

    def zero_first(sums_r, i, rows):
        if rows.start in (None, 0):
            @pl.when(i == 0)
            def _():
                sums_r[...] = jnp.zeros_like(sums_r)

    def loss_head(acc, i, j, extra_refs, out_refs, rows):
        h1_r, tgt_r, gt2_r, g4_r = extra_refs
        dy_r, dff_r, sums_r = out_refs
        r4 = _rms(acc)
        ffn = acc * r4
        n4 = ffn * g4_r[...]
        err = h1_r[rows, :] + gt2_r[...] * n4 - tgt_r[rows, :]
        dy = err * (1.0 / d)
        dy_r[rows, :] = dy
        dn4 = dy * gt2_r[...]
        dffn = dn4 * g4_r[...]
        dff_r[rows, :] = (r4 * (dffn - ffn * jnp.mean(dffn * ffn, axis=-1, keepdims=True))).astype(BF16)
        zero_first(sums_r, i, rows)

        sums_r[0:1, :] += _colsum(err * err)
        sums_r[1:2, :] += _colsum(dy * n4)
        sums_r[2:3, :] += _colsum(dn4 * ffn)

    tk_f = _tile(d_ff, 2048)
    dy, dff, sums_f = _mm("ff2_loss", act, wf_2, _NN, t, d, d_ff, tr, d, tk_f,
                          [(h1,) + row_d, (tgt,) + row_d, (gt2,) + vec_d, (g_post_ffn,) + vec_d],
                          [(jax.ShapeDtypeStruct((t, d), F32),) + row_d, (jax.ShapeDtypeStruct((t, d), BF16),) + row_d,
                           (jax.ShapeDtypeStruct((8, d), F32),) + sums_d], loss_head, row_chunk=rc)
    loss = lax.psum((0.5 / d) * jnp.sum(sums_f[0]), ("x", "y", "c"))

    def relu_sq_bwd(acc, i, j, extra_refs, out_refs, rows):
        out_refs[0][...] = (acc * (2.0 * jnp.maximum(extra_refs[0][...].astype(F32), 0.0))).astype(BF16)

    dhff, = _mm("d_hff", dff, wf_2, _NT, t, d_ff, d, tm, tn_f, d, [(hff,) + blk_f],
                [(jax.ShapeDtypeStruct((t, d_ff), BF16),) + blk_f], relu_sq_bwd)
    scatters = {}

    def send_grads(key, grads16, axes):
        plan = _scatter_plan(grads16, axes)
        scatters[key] = (plan,) + _split_start("scatter_%s_start" % key, plan)
        return scatters[key][3]

    def received_grads(key, after):
        plan, sems, thru, _ = scatters[key]
        return _split_wait("scatter_%s_wait" % key, plan, sems, thru, after)[1]

    gw_ff2, gw_ff2_16 = _grad_w("grad_w_ff2", act, dff)
    sent_ff2 = send_grads("ff2", [gw_ff2_16], big_axes[5:6])
    gw_ff1, gw_ff1_16 = _grad_w("grad_w_ff1", a2, dhff, token=sent_ff2)
    sent_ff1 = send_grads("ff1", [gw_ff1_16], big_axes[4:5])

    def pre_ffn_bwd(acc, i, j, extra_refs, out_refs, rows):
        h1_r, dy_r, mo_r, sc2_r, g3_r, gt1_r, g2_r = extra_refs[:7]
        dh1_r, dmo_r, sums_r = out_refs
        h1v = h1_r[rows, :]
        r3 = _rms(h1v)
        h1n = h1v * r3
        dn3 = acc * (1.0 + sc2_r[...])
        dh1n = dn3 * g3_r[...]
        dh1 = dy_r[rows, :] + r3 * (dh1n - h1n * jnp.mean(dh1n * h1n, axis=-1, keepdims=True))
        dh1_r[rows, :] = dh1
        mov = mo_r[rows, :]
        r2 = _rms(mov)
        mon = mov * r2
        dn2 = dh1 * gt1_r[...]
        dmon = dn2 * g2_r[...]
        dmo_r[rows, :] = (r2 * (dmon - mon * jnp.mean(dmon * mon, axis=-1, keepdims=True))).astype(BF16)
        zero_first(sums_r, i, rows)

        sums_r[0:1, :] += _colsum(acc)
        sums_r[1:2, :] += _colsum(acc * (h1n * g3_r[...]))
        sums_r[2:3, :] += _colsum(dn3 * h1n)
        sums_r[3:4, :] += _colsum(dh1 * (mon * g2_r[...]))
        sums_r[4:5, :] += _colsum(dn2 * mon)

    dh1, dmo, sums_m = _mm("d_a2_pre_ffn", dhff, wf_1, _NT, t, d, d_ff, tr, d, tk_f,
                           [(h1,) + row_d, (dy,) + row_d, (mo,) + row_d, (sc2,) + vec_d, (g_pre_ffn,) + vec_d,
                            (gt1,) + vec_d, (g_post_mix,) + vec_d] + _after(sent_ff1),
                           [(jax.ShapeDtypeStruct((t, d), F32),) + row_d, (jax.ShapeDtypeStruct((t, d), BF16),) + row_d,
                            (jax.ShapeDtypeStruct((8, d), F32),) + sums_d], pre_ffn_bwd, row_chunk=rc)
    gw_o, gw_o_16 = _grad_w("grad_w_o", merged, dmo)

    n_j = d // tn_d

    def merge_bwd_body(dmo_ref, wo_ref, ga_ref, gb_ref, ya_ref, yb_ref, dya_ref, dyb_ref, dproj_ref, acc_s):
        g = pl.program_id(2)

        @pl.when(g == 0)
        def _():
            dm = _dot(dmo_ref[...], wo_ref[...], _NT)
            acc_s[...] = dm
            sa = _sigmoid(ga_ref[...])
            dya_ref[...] = (dm * sa).astype(BF16)
            dproj_ref[...] = (dm * ya_ref[...] * sa * (1.0 - sa)).astype(BF16)

        @pl.when(g == 1)
        def _():
            dm = acc_s[...]
            sb = _sigmoid(gb_ref[...])
            dyb_ref[...] = (dm * sb).astype(BF16)
            dproj_ref[...] = (dm * yb_ref[...] * sb * (1.0 - sb)).astype(BF16)

    tile3 = pl.BlockSpec((tm, tn_d), lambda i, j, g: (i, j))
    dy_a, dy_b, dproj = pl.pallas_call(
        merge_bwd_body, name="d_merged", grid=(t // tm, n_j, 2),
        in_specs=[pl.BlockSpec((tm, d), lambda i, j, g: (i, 0)), pl.BlockSpec((tn_d, d), lambda i, j, g: (j, 0)),
                  pl.BlockSpec((tm, tn_d), lambda i, j, g: (i, ga_blk + j)),
                  pl.BlockSpec((tm, tn_d), lambda i, j, g: (i, gb_blk + j)), tile3, tile3],
        out_specs=[tile3, tile3, pl.BlockSpec((tm, tn_d), lambda i, j, g: (i, ga_blk + g * n_j + j))],
        out_shape=[jax.ShapeDtypeStruct((t, d), BF16), jax.ShapeDtypeStruct((t, d), BF16),
                   jax.ShapeDtypeStruct((t, n_in), BF16)],
        scratch_shapes=[pltpu.VMEM((tm, tn_d), F32)], compiler_params=_params(3),
    )(dmo, wf_o, proj, proj, y_a, y_b)

    def store_bf16(acc, i, j, extra_refs, out_refs, rows):
        out_refs[0][...] = acc.astype(BF16)

    tn_w = _tile(width, 512)
    blk_w = ((tm, tn_w), lambda i, j: (i, j))
    dout_a, = _mm("d_out_a", dy_a, wf_a, _NT, t, width, d, tm, tn_w, d, [],
                  [(jax.ShapeDtypeStruct((t, width), BF16),) + blk_w], store_bf16)
    dout_b, = _mm("d_out_b", dy_b, wf_b, _NT, t, width, d, tm, tn_w, d, [],
                  [(jax.ShapeDtypeStruct((t, width), BF16),) + blk_w], store_bf16)
    gw_a, gw_a_16 = _grad_w("grad_w_a_out", out_a, dy_a)
    gw_b, gw_b_16 = _grad_w("grad_w_b_out", out_b, dy_b)

    w_st = jnp.swapaxes(w_spatial[0], 1, 2)
    dproj, dg_sgu, dw_sp, dbst = _sgu_bwd(proj, dout_b, dproj, g_sgu_norm, w_spatial[0], w_st, bst, width, z_block)
    sent_mid = send_grads("mid", [gw_a_16, gw_b_16, gw_o_16], big_axes[1:4])
    dproj, dgh_heads, dlb, _ = _hgrn_bwd(proj, osum, dout_a, dproj, lb, g_hgrn_norm, width, _NO_EXCHANGE)
    gw_in, gw_in_16 = _grad_w("grad_w_in", a1, dproj, token=sent_mid)
    sent_in = send_grads("in", [gw_in_16], big_axes[:1])

    def pre_mix_bwd(acc, i, j, extra_refs, out_refs, rows):
        x_r, dh1_r, sc1_r, g1_r = extra_refs[:4]
        dx_r, sums_r = out_refs
        xv = x_r[rows, :]
        r1 = _rms(xv)
        xn = xv * r1
        dn1 = acc * (1.0 + sc1_r[...])
        dxn = dn1 * g1_r[...]
        dx_r[rows, :] = dh1_r[rows, :] + r1 * (dxn - xn * jnp.mean(dxn * xn, axis=-1, keepdims=True))
        zero_first(sums_r, i, rows)

        sums_r[0:1, :] += _colsum(acc)
        sums_r[1:2, :] += _colsum(acc * (xn * g1_r[...]))
        sums_r[2:3, :] += _colsum(dn1 * xn)

    tk_in = _tile(n_in, 2816)
    grad_x, sums_x = _mm(
        "d_a1_pre_mix", dproj, wf_in, _NT, t, d, n_in, tr, d, tk_in,
        [(x2,) + row_d, (dh1,) + row_d, (sc1,) + vec_d, (g_pre_mix,) + vec_d] + _after(sent_in),
        [(jax.ShapeDtypeStruct((t, d), F32),) + row_d, (jax.ShapeDtypeStruct((8, d), F32),) + sums_d],
        pre_mix_bwd, row_chunk=rc)

    dmod = jnp.concatenate([sums_x[0:2], sums_m[3:4], sums_m[0:2], sums_f[1:2]], axis=0).reshape(N_DEV, n_ada // LANE, LANE)
    ada_rows = -(-(n_ada // LANE) // 8) * 8
    dmod = jnp.pad(dmod, ((0, 0), (0, ada_rows - n_ada // LANE), (0, 0))).reshape(N_DEV * ada_rows, LANE)
    parts = [dmod, _rows(sums_x[2:3]), _rows(sums_m[4:5]), _rows(sums_m[2:3]), _rows(sums_f[2:3]),
             _rows(jnp.sum(dgh_heads, axis=0)), _rows(dg_sgu), _rows(dw_sp), _rows(dbst.T)]
    n_common = sum(p.shape[0] for p in parts)
    payload = jnp.concatenate(parts + [_rows(dlb)], axis=0)
    gathered = _all_gather_small("gather_small_grads", payload)

    dmod_mine = lax.dynamic_slice_in_dim(gathered[:, :N_DEV * ada_rows, :].reshape(N_DEV, N_DEV, ada_rows * LANE),
                                         me, 1, axis=1)[:, 0, :n_ada]
    ada_out = [o[None] for o in _adamw_ada(sc_all.T, dmod_mine, w_ada[0], m_w_ada[0], v_w_ada[0])]

    def pack(b_, g1_, g2_, g3_, g4_, gh_, gs_, ws_, bs_):
        b3 = b_.reshape(N_DEV, n_ada // LANE, LANE)
        b3 = jnp.pad(b3, ((0, 0), (0, ada_rows - n_ada // LANE), (0, 0))).reshape(N_DEV * ada_rows, LANE)
        return jnp.concatenate([b3, _rows(g1_), _rows(g2_), _rows(g3_), _rows(g4_), _rows(gh_), _rows(gs_),
                                _rows(ws_), _rows(bs_)], axis=0)

    small_w = (b_ada, g_pre_mix, g_post_mix, g_pre_ffn, g_post_ffn, g_hgrn_norm, g_sgu_norm, w_spatial, b_spatial)
    small_m = (m_b_ada, m_g_pre_mix, m_g_post_mix, m_g_pre_ffn, m_g_post_ffn, m_g_hgrn_norm, m_g_sgu_norm, m_w_spatial, m_b_spatial)
    small_v = (v_b_ada, v_g_pre_mix, v_g_post_mix, v_g_pre_ffn, v_g_post_ffn, v_g_hgrn_norm, v_g_sgu_norm, v_w_spatial, v_b_spatial)
    packed = _adamw_small(gathered[:, :n_common, :], pack(*small_w), pack(*small_m), pack(*small_v))

    def unpack(slab):
        outs, at = [], 0
        b3 = slab[:N_DEV * ada_rows].reshape(N_DEV, ada_rows, LANE)[:, :n_ada // LANE, :]
        outs.append(b3.reshape(b_ada.shape))
        at = N_DEV * ada_rows
        for ref in small_w[1:]:
            n_el = ref.size
            n_r = -(-(n_el // LANE) // 8) * 8
            outs.append(slab[at:at + n_el // LANE].reshape(ref.shape))
            at += n_r
        return outs

    small_out = [unpack(s) for s in packed]

    dlb_all = gathered[:, n_common:n_common + 2 * heads, :].reshape(N_DEV, 2, heads, LANE)
    dlb_mine = lax.dynamic_index_in_dim(dlb_all, me, axis=2, keepdims=False)
    lb_out = _adamw_lb(dlb_mine, lb_logits, m_lb_logits, v_lb_logits)

    moms = [m_w_in, m_w_a_out, m_w_b_out, m_w_o, m_w_ff1, m_w_ff2]
    vars_ = [v_w_in, v_w_a_out, v_w_b_out, v_w_o, v_w_ff1, v_w_ff2]
    big_out = {}

    def big_update(nm, g_full, landing):
        k = big_names.index(nm)
        outs = _adamw_big("adamw_" + nm, me_arr, big[k], moms[k][0], vars_[k][0], g_full, landing, big_axes[k])
        big_out[nm] = [o[None] for o in outs]
        return outs[0]

    land_ff2, = received_grads("ff2", gathered)
    done = big_update("w_ff2", gw_ff2, land_ff2)
    land_ff1, = received_grads("ff1", done)
    done = big_update("w_ff1", gw_ff1, land_ff1)
    land_a, land_b, land_o = received_grads("mid", done)
    big_update("w_a_out", gw_a, land_a)
    big_update("w_b_out", gw_b, land_b)
    done = big_update("w_o", gw_o, land_o)
    land_in, = received_grads("in", done)
    big_update("w_in", gw_in, land_in)

    order = ["w_ada", "b_ada", "g_pre_mix", "g_post_mix", "g_pre_ffn", "g_post_ffn", "w_in", "lb_logits", "g_hgrn_norm",
             "w_a_out", "g_sgu_norm", "w_spatial", "b_spatial", "w_b_out", "w_o", "w_ff1", "w_ff2"]
    small_names = ["b_ada", "g_pre_mix", "g_post_mix", "g_pre_ffn", "g_post_ffn", "g_hgrn_norm", "g_sgu_norm", "w_spatial", "b_spatial"]

    def leaf(kind, nm):
        if nm == "w_ada":
            return ada_out[kind]
        if nm == "lb_logits":
            return lb_out[kind]
        if nm in big_out:
            return big_out[nm][kind]
        return small_out[kind][small_names.index(nm)]

    result = [loss, grad_x[None]]
    for kind in range(4):
        result += [leaf(kind, nm) for nm in order]
    return tuple(result)
```

```python
import functools
import math

import jax
import jax.numpy as jnp
from jax import lax
from jax.experimental import pallas as pl
from jax.experimental.pallas import tpu as pltpu

F32 = jnp.float32
BF16 = jnp.bfloat16
MESH = pl.DeviceIdType.MESH
HIGHEST = lax.Precision.HIGHEST

N_DEV = 8
HEAD = 128
A_CHUNK = 32
N_MOD = 6
EPS = 1e-6
LANE = 128
VMEM_LIMIT = 56 * 1024 * 1024

ADAM_LR = 0.001
ADAM_B1 = 0.9
ADAM_B2 = 0.999
ADAM_EPS = 1e-08
ADAM_WD = 0.01
ADAM_STEP = 10

_NN = (((1,), (0,)), ((), ()))
_NT = (((1,), (1,)), ((), ()))
_TN = (((0,), (0,)), ((), ()))


def _dot(a, b, dims=_NN, precision=None):
    return lax.dot_general(a, b, dims, preferred_element_type=F32, precision=precision)


def _bdot(a, b, dims=_NN):
    return _dot(a.astype(BF16), b.astype(BF16), dims)


def _params(n_grid):
    return pltpu.CompilerParams(dimension_semantics=("arbitrary",) * n_grid, vmem_limit_bytes=VMEM_LIMIT)


def _dev_index():
    return lax.axis_index("x") * 4 + lax.axis_index("y") * 2 + lax.axis_index("c")


def _dev_coords(i):
    return (i // 4, (i // 2) % 2, i % 2)


def _sigmoid(x):
    return 1.0 / (1.0 + jnp.exp(-x))


def _erf(x):
    ax = jnp.abs(x)
    t = 1.0 / (1.0 + 0.3275911 * ax)
    poly = ((((1.061405429 * t - 1.453152027) * t + 1.421413741) * t - 0.284496736) * t + 0.254829592) * t
    y = 1.0 - poly * jnp.exp(-ax * ax)
    return jnp.where(x < 0, -y, y)


def _gelu_and_grad(x):
    cdf = 0.5 * (1.0 + _erf(x * (2.0 ** -0.5)))
    pdf = jnp.exp(-0.5 * x * x) * (1.0 / math.sqrt(2.0 * math.pi))
    return x * cdf, cdf + x * pdf


def _rms(x):
    return lax.rsqrt(jnp.mean(x * x, axis=-1, keepdims=True) + EPS)


def _colsum(x):
    return jnp.sum(x, axis=0, keepdims=True)


def _tile(n, want):
    if n <= want:
        return n
    t = (want // LANE) * LANE
    while n % t:
        t -= LANE
    assert t > 0, (n, want)
    return t


def _all_gather_small(name, payload):
    rows = payload.shape[0]

    def body(p_ref, out_ref, send_sems, recv_sems, local_sem):
        me = _dev_index()
        mine = pltpu.make_async_copy(p_ref, out_ref.at[me], local_sem)
        mine.start()
        sends = []
        for r in range(1, N_DEV):
            peer = (me + r) % N_DEV
            cp = pltpu.make_async_remote_copy(
                src_ref=p_ref, dst_ref=out_ref.at[me], send_sem=send_sems.at[r - 1], recv_sem=recv_sems.at[r - 1],
                device_id=_dev_coords(peer), device_id_type=MESH)
            cp.start()
            sends.append(cp)
        for r in range(1, N_DEV):
            src = (me + N_DEV - r) % N_DEV
            pltpu.make_async_remote_copy(
                src_ref=p_ref, dst_ref=out_ref.at[src], send_sem=send_sems.at[r - 1], recv_sem=recv_sems.at[r - 1],
                device_id=_dev_coords(src), device_id_type=MESH).wait_recv()
        for cp in sends:
            cp.wait_send()
        mine.wait()

    return pl.pallas_call(
        body, name=name,
        out_shape=jax.ShapeDtypeStruct((N_DEV, rows, LANE), F32),
        in_specs=[pl.BlockSpec(memory_space=pltpu.VMEM)],
        out_specs=pl.BlockSpec(memory_space=pltpu.VMEM),
        scratch_shapes=[pltpu.SemaphoreType.DMA((N_DEV - 1,)), pltpu.SemaphoreType.DMA((N_DEV - 1,)),
                        pltpu.SemaphoreType.DMA],
        compiler_params=pltpu.CompilerParams(vmem_limit_bytes=VMEM_LIMIT),
    )(payload)


def _region(ref, dev, axis, n):
    start = pl.multiple_of(dev * n, LANE if axis == 1 else 16)
    return ref.at[:, pl.ds(start, n)] if axis == 1 else ref.at[pl.ds(start, n), :]


class _Exchange:
    def __init__(self, arrays, out_shapes, sems, start, finish):
        self.arrays, self.out_shapes, self.sems, self.start, self.finish = arrays, out_shapes, sems, start, finish


def _gather_plan(shards, axes):
    n_w = len(shards)
    fulls = []
    for s, ax in zip(shards, axes):
        shp = (s.shape[0], s.shape[1] * N_DEV) if ax == 1 else (s.shape[0] * N_DEV, s.shape[1])
        fulls.append(jax.ShapeDtypeStruct(shp, BF16))
    widths = [s.shape[ax] for s, ax in zip(shards, axes)]

    def places():
        x, y, c = lax.axis_index("x"), lax.axis_index("y"), lax.axis_index("c")
        chips = [(1 - x, y), (x, 1 - y), (1 - x, 1 - y)]
        return (x, y, c), (x, y, 1 - c), chips

    def index(p):
        return p[0] * 4 + p[1] * 2 + p[2]

    def copy(w, k, s_refs, f_refs, sems, block, to, from_shard):
        send_sems, recv_sems, _ = sems
        dst = _region(f_refs[w], index(block), axes[w], widths[w])
        return pltpu.make_async_remote_copy(
            src_ref=s_refs[w] if from_shard else dst, dst_ref=dst,
            send_sem=send_sems.at[w, k], recv_sem=recv_sems.at[w, k], device_id=to, device_id_type=MESH)

    def local(w, s_refs, f_refs, sems, me):
        return pltpu.make_async_copy(s_refs[w], _region(f_refs[w], index(me), axes[w], widths[w]), sems[2].at[w])

    def start(s_refs, f_refs, sems):
        me, sib, chips = places()
        for w in range(n_w):
            local(w, s_refs, f_refs, sems, me).start()
            copy(w, 0, s_refs, f_refs, sems, me, sib, True).start()
            for j, chip in enumerate(chips):
                copy(w, 1 + j, s_refs, f_refs, sems, me, (*chip, me[2]), True).start()

    def finish(s_refs, f_refs, sems):
        me, sib, chips = places()
        for w in range(n_w):
            for j, chip in enumerate(chips):
                copy(w, 1 + j, s_refs, f_refs, sems, (*chip, me[2]), me, True).wait_recv()
                copy(w, 4 + j, s_refs, f_refs, sems, (*chip, me[2]), sib, False).start()
        for w in range(n_w):
            copy(w, 0, s_refs, f_refs, sems, sib, me, True).wait_recv()
            for j, chip in enumerate(chips):
                copy(w, 4 + j, s_refs, f_refs, sems, (*chip, sib[2]), me, False).wait_recv()
        for w in range(n_w):
            for k in range(N_DEV - 1):
                copy(w, k, s_refs, f_refs, sems, me, sib, True).wait_send()
            local(w, s_refs, f_refs, sems, me).wait()

    sems = [pltpu.SemaphoreType.DMA((n_w, N_DEV - 1)), pltpu.SemaphoreType.DMA((n_w, N_DEV - 1)),
            pltpu.SemaphoreType.DMA((n_w,))]
    return _Exchange(list(shards), fulls, sems, start, finish)


def _scatter_plan(grads, axes):
    n_w = len(grads)
    lands = []
    for g, ax in zip(grads, axes):
        shp = (g.shape[0], g.shape[1] // N_DEV) if ax == 1 else (g.shape[0] // N_DEV, g.shape[1])
        lands.append(jax.ShapeDtypeStruct((N_DEV - 1,) + shp, BF16))
    widths = [ld.shape[1 + ax] for ld, ax in zip(lands, axes)]

    def copy(w, r, g_refs, l_refs, sems, block, to):
        return pltpu.make_async_remote_copy(
            src_ref=_region(g_refs[w], block, axes[w], widths[w]), dst_ref=l_refs[w].at[r - 1],
            send_sem=sems[0].at[w * (N_DEV - 1) + r - 1], recv_sem=sems[1].at[w * (N_DEV - 1) + r - 1],
            device_id=_dev_coords(to), device_id_type=MESH)

    def start(g_refs, l_refs, sems):
        me = _dev_index()
        for w in range(n_w):
            for r in range(1, N_DEV):
                owner = (me + r) % N_DEV
                copy(w, r, g_refs, l_refs, sems, owner, owner).start()

    def finish(g_refs, l_refs, sems):
        me = _dev_index()
        for w in range(n_w):
            for r in range(1, N_DEV):
                copy(w, r, g_refs, l_refs, sems, me, (me + N_DEV - r) % N_DEV).wait_recv()
        for w in range(n_w):
            for r in range(1, N_DEV):
                copy(w, r, g_refs, l_refs, sems, me, (me + r) % N_DEV).wait_send()

    sems = [pltpu.SemaphoreType.DMA((n_w * (N_DEV - 1),)), pltpu.SemaphoreType.DMA((n_w * (N_DEV - 1),))]
    return _Exchange(list(grads), lands, sems, start, finish)


def _run_exchange(name, plan):
    n_in, n_out = len(plan.arrays), len(plan.out_shapes)

    def body(*refs):
        ins, outs, sems = refs[:n_in], refs[n_in:n_in + n_out], refs[n_in + n_out:]
        plan.start(ins, outs, sems)
        plan.finish(ins, outs, sems)

    any_spec = pl.BlockSpec(memory_space=pl.ANY)
    return pl.pallas_call(
        body, name=name, out_shape=plan.out_shapes,
        in_specs=[any_spec] * n_in, out_specs=[any_spec] * n_out, scratch_shapes=plan.sems,
    )(*plan.arrays)


_NO_EXCHANGE = _Exchange([], [], [], lambda i, o, s: None, lambda i, o, s: None)


def _direct_gather_plan(fulls, axes):
    n_w = len(fulls)
    widths = [f.shape[ax] // N_DEV for f, ax in zip(fulls, axes)]
    fulls = [jax.ShapeDtypeStruct(f.shape, f.dtype) for f in fulls]

    def copy(w, r, s_refs, f_refs, sems, block, to):
        part = _region(f_refs[w], block, axes[w], widths[w])
        return pltpu.make_async_remote_copy(
            src_ref=part, dst_ref=part,
            send_sem=sems[0].at[w * (N_DEV - 1) + r - 1], recv_sem=sems[1].at[w * (N_DEV - 1) + r - 1],
            device_id=_dev_coords(to), device_id_type=MESH)

    def start(s_refs, f_refs, sems):
        me = _dev_index()
        for w in range(n_w):
            for r in range(1, N_DEV):
                copy(w, r, s_refs, f_refs, sems, me, (me + r) % N_DEV).start()

    def finish(s_refs, f_refs, sems):
        me = _dev_index()
        for w in range(n_w):
            for r in range(1, N_DEV):
                src = (me + N_DEV - r) % N_DEV
                copy(w, r, s_refs, f_refs, sems, src, src).wait_recv()
        for w in range(n_w):
            for r in range(1, N_DEV):
                copy(w, r, s_refs, f_refs, sems, me, (me + r) % N_DEV).wait_send()

    sems = [pltpu.SemaphoreType.DMA((n_w * (N_DEV - 1),)), pltpu.SemaphoreType.DMA((n_w * (N_DEV - 1),))]
    return _Exchange([], fulls, sems, start, finish)


_HBM = pl.BlockSpec(memory_space=pltpu.HBM)
_SEM = pl.BlockSpec(memory_space=pltpu.SEMAPHORE)
_EFFECT = pltpu.SideEffectType.DATAFLOW_SIDE_EFFECTING


def _split_start(name, plan, landing=None):
    n_in, n_out, n_sem = len(plan.arrays), len(plan.out_shapes), len(plan.sems)

    def body(*refs):
        ins, lands = refs[:n_in], refs[n_in:n_in + n_out]
        sems = refs[n_in + n_out:n_in + n_out + n_sem]
        token = refs[-1]
        plan.start(ins, lands, sems)
        token[...] = jnp.zeros_like(token)

    hbm = lambda a: pltpu.HBM(a.shape, a.dtype)
    results = pl.pallas_call(
        body, name=name,
        out_shape=tuple(plan.sems) + tuple(hbm(a) for a in plan.arrays) + tuple(hbm(a) for a in plan.out_shapes)
        + (jax.ShapeDtypeStruct((8, LANE), F32),),
        in_specs=(_HBM,) * (n_in + n_out),
        out_specs=(_SEM,) * n_sem + (_HBM,) * (n_in + n_out) + (pl.BlockSpec(memory_space=pltpu.VMEM),),
        input_output_aliases={i: n_sem + i for i in range(n_in + n_out)},
        compiler_params=pltpu.CompilerParams(has_side_effects=_EFFECT),
    )(*[pltpu.with_memory_space_constraint(a, pltpu.HBM) for a in plan.arrays],
      *[pltpu.with_memory_space_constraint(a, pltpu.HBM)
        for a in (landing if landing is not None else [lax.empty(a.shape, a.dtype) for a in plan.out_shapes])])
    return results[:n_sem], results[n_sem:n_sem + n_in + n_out], results[-1]


def _split_wait(name, plan, sems, thru, after):
    n_in, n_out, n_sem = len(plan.arrays), len(plan.out_shapes), len(plan.sems)

    def body(*refs):
        ins, lands = refs[:n_in], refs[n_in:n_in + n_out]
        sem_refs = refs[n_in + n_out:n_in + n_out + n_sem]
        plan.finish(ins, lands, sem_refs)

    hbm = lambda a: pltpu.HBM(a.shape, a.dtype)
    results = pl.pallas_call(
        body, name=name,
        out_shape=tuple(hbm(a) for a in plan.arrays) + tuple(hbm(a) for a in plan.out_shapes),
        in_specs=(_HBM,) * (n_in + n_out) + (_SEM,) * n_sem + (pl.BlockSpec(memory_space=pl.ANY),),
        out_specs=(_HBM,) * (n_in + n_out),
        input_output_aliases={i: i for i in range(n_in + n_out)},
        compiler_params=pltpu.CompilerParams(has_side_effects=_EFFECT),
    )(*thru, *sems, after)
    return results[:n_in], results[n_in:]


def _cast_into_full(name, me, w, axis):
    r, c = w.shape
    tr = _tile(r, 256)
    if axis == 1:
        shape, place = (r, c * N_DEV), pl.BlockSpec((tr, c), lambda i, me_ref: (i, me_ref[0]))
    else:
        shape, place = (r * N_DEV, c), pl.BlockSpec((tr, c), lambda i, me_ref: (me_ref[0] * (r // tr) + i, 0))

    def body(me_ref, w_ref, o_ref):
        o_ref[...] = w_ref[...].astype(BF16)

    grid_spec = pltpu.PrefetchScalarGridSpec(
        num_scalar_prefetch=1, grid=(r // tr,),
        in_specs=[pl.BlockSpec((tr, c), lambda i, me_ref: (i, 0))], out_specs=place)
    return pl.pallas_call(body, name=name, grid_spec=grid_spec, out_shape=jax.ShapeDtypeStruct(shape, BF16),
                          compiler_params=_params(1))(me, w)


def _mm(name, a, b, dims, m, n, k, tm, tn, tk, extras, outs, epilogue, row_chunk=None, exchange=None):
    ni, nj, nk = m // tm, n // tn, k // tk
    ne, no = len(extras), len(outs)
    xin = len(exchange.arrays) if exchange else 0
    xout = len(exchange.out_shapes) if exchange else 0
    if dims == _TN:
        a_spec = pl.BlockSpec((tk, tm), lambda i, j, kk: (kk, i))
    else:
        a_spec = pl.BlockSpec((tm, tk), lambda i, j, kk: (i, kk))
    if dims == _NT:
        b_spec = pl.BlockSpec((tn, tk), lambda i, j, kk: (j, kk))
    else:
        b_spec = pl.BlockSpec((tk, tn), lambda i, j, kk: (kk, j))
    chunks = [slice(None)] if row_chunk is None else [slice(r, r + row_chunk) for r in range(0, tm, row_chunk)]

    def lift(index_map):
        return lambda i, j, kk: index_map(i, j)

    def body(a_ref, b_ref, *rest):
        extra_refs, rest = rest[:ne], rest[ne:]
        xin_refs, rest = rest[:xin], rest[xin:]
        out_refs, rest = rest[:no], rest[no:]
        xout_refs, rest = rest[:xout], rest[xout:]
        i, j, kk = pl.program_id(0), pl.program_id(1), pl.program_id(2)
        if exchange:
            sem_refs = rest[1:] if nk > 1 else rest

            @pl.when((i == 0) & (j == 0) & (kk == 0))
            def _():
                exchange.start(xin_refs, xout_refs, sem_refs)

        part = _dot(a_ref[...], b_ref[...], dims)
        if nk == 1:
            for rows in chunks:
                epilogue(part[rows], i, j, extra_refs, out_refs, rows)
        else:
            acc_ref = rest[0]

            @pl.when(kk == 0)
            def _():
                acc_ref[...] = part

            @pl.when(kk > 0)
            def _():
                acc_ref[...] += part

            @pl.when(kk == nk - 1)
            def _():
                for rows in chunks:
                    epilogue(acc_ref[rows, :], i, j, extra_refs, out_refs, rows)

        if exchange:
            @pl.when((i == ni - 1) & (j == nj - 1) & (kk == nk - 1))
            def _():
                exchange.finish(xin_refs, xout_refs, sem_refs)

    any_spec = pl.BlockSpec(memory_space=pl.ANY)
    results = pl.pallas_call(
        body, name=name,
        grid=(ni, nj, nk),
        in_specs=[a_spec, b_spec] + [pl.BlockSpec(bs, lift(im)) for _, bs, im in extras] + [any_spec] * xin,
        out_specs=[pl.BlockSpec(bs, lift(im)) for _, bs, im in outs] + [any_spec] * xout,
        out_shape=[sd for sd, _, _ in outs] + (list(exchange.out_shapes) if exchange else []),
        scratch_shapes=([pltpu.VMEM((tm, tn), F32)] if nk > 1 else []) + (list(exchange.sems) if exchange else []),
        compiler_params=_params(3),
    )(a, b, *[arr for arr, _, _ in extras], *(exchange.arrays if exchange else []))
    return (results[:no], results[no:]) if exchange else results


def _after(token):
    return [(token, (8, LANE), lambda i, j: (0, 0))]


def _grad_w(name, a, dc, token=None, tm=512, tn=1024):
    t, m = a.shape
    n = dc.shape[1]
    tm, tn = _tile(m, tm), _tile(n, tn)

    def epilogue(acc, i, j, extra_refs, out_refs, rows):
        out_refs[0][...] = acc
        out_refs[1][...] = acc.astype(BF16)

    blk = ((tm, tn), lambda i, j: (i, j))
    return _mm(name, a, dc, _TN, m, n, t, tm, tn, t, _after(token) if token is not None else [],
               [(jax.ShapeDtypeStruct((m, n), F32),) + blk, (jax.ShapeDtypeStruct((m, n), BF16),) + blk], epilogue)


def _cast_bf16(name, w):
    r, c = w.shape
    tr = _tile(r, 256)
    return pl.pallas_call(
        lambda w_ref, o_ref: o_ref.__setitem__(Ellipsis, w_ref[...].astype(BF16)), name=name,
        grid=(r // tr,), in_specs=[pl.BlockSpec((tr, c), lambda i: (i, 0))],
        out_specs=pl.BlockSpec((tr, c), lambda i: (i, 0)), out_shape=jax.ShapeDtypeStruct((r, c), BF16),
        compiler_params=_params(1),
    )(w)


def _prep_small(c_row, lb_logits):
    d = c_row.shape[1]
    rows = d // LANE

    def body(c_ref, l_ref, o_ref):
        cv = c_ref[...]
        o_ref[0:rows, :] = cv * _sigmoid(cv)
        lbs = [_sigmoid(l_ref[dr][0:1, :] - l_ref[dr][1:2, :]) for dr in range(2)]
        o_ref[rows:rows + 8, :] = jnp.concatenate(lbs + [jnp.zeros((6, LANE), F32)], axis=0)

    return pl.pallas_call(
        body, name="prep_small", out_shape=jax.ShapeDtypeStruct((rows + 8, LANE), F32),
    )(c_row.reshape(rows, LANE), lb_logits)


def _mod_shard(sc_all, w_ada_shard, b_shard):
    d, n = w_ada_shard.shape
    tn = _tile(n, 512)

    def body(s_ref, w_ref, b_ref, o_ref):
        o_ref[...] = _dot(s_ref[...], w_ref[...], precision=HIGHEST) + b_ref[...]

    return pl.pallas_call(
        body, name="mod_shard", grid=(n // tn,),
        in_specs=[pl.BlockSpec((N_DEV, d), lambda j: (0, 0)), pl.BlockSpec((d, tn), lambda j: (0, j)),
                  pl.BlockSpec((1, tn), lambda j: (0, j))],
        out_specs=pl.BlockSpec((N_DEV, tn), lambda j: (0, j)),
        out_shape=jax.ShapeDtypeStruct((N_DEV, n), F32), compiler_params=_params(1),
    )(sc_all, w_ada_shard, b_shard)


def _norm_mod(x, gain, shift, scale):
    t, d = x.shape
    tm = _tile(t, 512)

    def body(x_ref, g_ref, sh_ref, sc_ref, o_ref):
        xv = x_ref[...]
        o_ref[...] = ((xv * _rms(xv) * g_ref[...]) * (1.0 + sc_ref[...]) + sh_ref[...]).astype(BF16)

    vec = pl.BlockSpec((1, d), lambda i: (0, 0))
    return pl.pallas_call(
        body, name="norm_mod", grid=(t // tm,),
        in_specs=[pl.BlockSpec((tm, d), lambda i: (i, 0)), vec, vec, vec],
        out_specs=pl.BlockSpec((tm, d), lambda i: (i, 0)), out_shape=jax.ShapeDtypeStruct((t, d), BF16),
        compiler_params=_params(1),
    )(x, gain, shift, scale)


def _chunk_masks():
    row = lax.broadcasted_iota(jnp.int32, (HEAD, HEAD), 0)
    col = lax.broadcasted_iota(jnp.int32, (HEAD, HEAD), 1)
    same = (row // A_CHUNK) == (col // A_CHUNK)
    return same & (col <= row), same & (col >= row)


def _ones(mask):
    return jnp.where(mask, 1.0, 0.0).astype(BF16)


def _dot_split(ones_bf16, x):
    hi = x.astype(BF16)
    lo = (x - hi.astype(F32)).astype(BF16)
    return _dot(ones_bf16, hi) + _dot(ones_bf16, lo)


def _hgrn_block(direction, f, lb, cum2):
    sf = _sigmoid(f)
    big_f = lb + (1.0 - lb) * sf
    k = (1.0 - lb) * (1.0 - sf)
    lf = jnp.log(big_f)
    both = _dot_split(cum2, lf)
    cf, cr = both[:HEAD], both[HEAD:]
    b, rest = (cf, cr - lf) if direction == 0 else (cr, cf - lf)
    return k, sf, big_f, jnp.exp(b), jnp.exp(-b), jnp.exp(rest)


def _hgrn_fwd(proj, lb, g_norm, width, exchange):
    t = proj.shape[0]
    heads = width // HEAD
    nb, nc = t // HEAD, t // A_CHUNK
    ua = 2 if nb % 2 == 0 else 1
    ub = 8 if nc % 8 == 0 else 4
    q_scale = HEAD ** -0.5
    xin, xout = len(exchange.arrays), len(exchange.out_shapes)

    def body(q_ref, ffw_ref, fbw_ref, v_ref, og_ref, lb_ref, g_ref, *rest):
        xin_refs, rest = rest[:xin], rest[xin:]
        outa_ref, osum_ref = rest[:2]
        xout_refs, rest = rest[2:2 + xout], rest[2 + xout:]
        qd_s, ke_s, dc_s, o_s = rest[:4]
        sem_refs = rest[4:]
        h = pl.program_id(0)

        @pl.when(h == 0)
        def _():
            exchange.start(xin_refs, xout_refs, sem_refs)

        tril, triu = _chunk_masks()
        cum2 = jnp.concatenate([_ones(tril), _ones(triu)], axis=0)
        f_refs = (ffw_ref, fbw_ref)
        lbs = (lb_ref[0:1, :], lb_ref[1:2, :])

        def phase_a(it, carry):
            loaded = []
            for u in range(ua):
                rows = pl.ds(pl.multiple_of((it * ua + u) * HEAD, HEAD), HEAD)
                loaded.append((rows, q_ref[rows, :], v_ref[rows, :], ffw_ref[rows, :], fbw_ref[rows, :]))
            chains = [(d, rows, qv * q_scale, vv.astype(BF16), fv)
                      for rows, qv, vv, f0, f1 in loaded for d, fv in ((0, f0), (1, f1))]
            blocks = [_hgrn_block(d, fv, lbs[d], cum2) for d, _, _, _, fv in chains]
            scaled = [(qv * eb, k * enb, k * erest, eb * erest)
                      for (_, _, qv, _, _), (k, _, _, eb, enb, erest) in zip(chains, blocks)]
            atts = [jnp.where(tril if d == 0 else triu, _bdot(qd, kd, _NT), 0.0)
                    for (d, _, _, _, _), (qd, kd, _, _) in zip(chains, scaled)]
            intras = [_bdot(att, vv) for att, (_, _, _, vv, _) in zip(atts, chains)]
            results = [(d, rows, o_intra, qd.astype(BF16), ke.astype(BF16), decay)
                       for (d, rows, _, _, _), (qd, _, ke, decay), o_intra in zip(chains, scaled, intras)]
            for d, rows, o_intra, qd16, ke16, decay in results:
                o_s[d, rows, :] = o_intra
                qd_s[d, rows, :] = qd16
                ke_s[d, rows, :] = ke16
                dc_s[d, rows, :] = decay
            return carry

        lax.fori_loop(0, nb // ua, phase_a, 0)

        def phase_b(it, states):
            loaded = []
            for u in range(ub):
                n = it * ub + u
                for d in range(2):
                    c = n if d == 0 else nc - 1 - n
                    start = pl.multiple_of(c * A_CHUNK, A_CHUNK)
                    rows = pl.ds(start, A_CHUNK)
                    loaded.append((d, rows, qd_s[d, rows, :], ke_s[d, rows, :], v_ref[rows, :],
                                   dc_s[d, pl.ds(start, 1), :], o_s[d, rows, :]))
            increments = [_dot(vv.astype(BF16), ke16, _TN) for _, _, _, ke16, vv, _, _ in loaded]
            states = list(states)
            befores = []
            for (d, _, _, _, _, decay, _), inc in zip(loaded, increments):
                befores.append(states[d].astype(BF16))
                states[d] = states[d] * decay + inc
            inters = [_dot(qd16, before, _NT) for (_, _, qd16, _, _, _, _), before in zip(loaded, befores)]
            for (d, rows, _, _, _, _, o_intra), o_inter in zip(loaded, inters):
                o_s[d, rows, :] = o_intra + o_inter
            return tuple(states)

        zero_state = jnp.zeros((HEAD, HEAD), F32)
        lax.fori_loop(0, nc // ub, phase_b, (zero_state, zero_state))

        def phase_c(i, carry):
            rows = pl.ds(pl.multiple_of(i * HEAD, HEAD), HEAD)
            o = o_s[0, rows, :] + o_s[1, rows, :]
            osum_ref[rows, :] = o
            og = og_ref[rows, :]
            outa_ref[rows, :] = (o * _rms(o) * g_ref[...] * (og * _sigmoid(og))).astype(BF16)
            return carry

        lax.fori_loop(0, nb, phase_c, 0)

        @pl.when(h == heads - 1)
        def _():
            exchange.finish(xin_refs, xout_refs, sem_refs)

    def col(p):
        return pl.BlockSpec((t, HEAD), lambda h: (0, p * heads + h))

    any_spec = pl.BlockSpec(memory_space=pl.ANY)
    results = pl.pallas_call(
        body, name="hgrn_fwd", grid=(heads,),
        in_specs=[col(0), col(1), col(2), col(3), col(4),
                  pl.BlockSpec((2, HEAD), lambda h: (0, h)), pl.BlockSpec((1, HEAD), lambda h: (0, 0))] + [any_spec] * xin,
        out_specs=[pl.BlockSpec((t, HEAD), lambda h: (0, h)), pl.BlockSpec((t, HEAD), lambda h: (0, h))] + [any_spec] * xout,
        out_shape=[jax.ShapeDtypeStruct((t, width), BF16), jax.ShapeDtypeStruct((t, width), F32)] + list(exchange.out_shapes),
        scratch_shapes=[pltpu.VMEM((2, t, HEAD), BF16), pltpu.VMEM((2, t, HEAD), BF16), pltpu.VMEM((2, t, HEAD), F32),
                        pltpu.VMEM((2, t, HEAD), F32)] + list(exchange.sems),
        compiler_params=_params(1),
    )(proj, proj, proj, proj, proj, lb, g_norm, *exchange.arrays)
    return results[0], results[1], results[2:]


def _sgu_core(u_pre, v_pre, g_v, ws_ref, bst):
    u, du = _gelu_and_grad(u_pre)
    v, dv = _gelu_and_grad(v_pre)
    mu = jnp.mean(v, axis=-1, keepdims=True)
    dlt = v - mu
    rstd = lax.rsqrt(jnp.mean(dlt * dlt, axis=-1, keepdims=True) + EPS)
    vhat = dlt * rstd
    vn = vhat * g_v
    groups = vn.shape[1] // HEAD
    cols = []
    for g in range(groups):
        vm_g = _bdot(ws_ref[g], vn[:, g * HEAD:(g + 1) * HEAD]) + bst[:, g:g + 1]
        cols.append(vm_g)
    return u, du, dv, vhat, rstd, vn, jnp.concatenate(cols, axis=1)


def _sgu_fwd(proj, g_v, w_s, bst, width, z_block):
    t = proj.shape[0]

    def body(u_ref, v_ref, g_ref, ws_ref, bst_ref, o_ref):
        u, _, _, _, _, _, vm = _sgu_core(u_ref[...], v_ref[...], g_ref[...], ws_ref, bst_ref[...])
        o_ref[...] = (u * vm).astype(BF16)

    groups = width // HEAD
    return pl.pallas_call(
        body, name="sgu_fwd", grid=(t // HEAD,),
        in_specs=[pl.BlockSpec((HEAD, width), lambda i: (i, z_block)), pl.BlockSpec((HEAD, width), lambda i: (i, z_block + 1)),
                  pl.BlockSpec((1, width), lambda i: (0, 0)), pl.BlockSpec((groups, HEAD, HEAD), lambda i: (0, 0, 0)),
                  pl.BlockSpec((HEAD, groups), lambda i: (0, 0))],
        out_specs=pl.BlockSpec((HEAD, width), lambda i: (i, 0)),
        out_shape=jax.ShapeDtypeStruct((t, width), BF16), compiler_params=_params(1),
    )(proj, proj, g_v, w_s, bst)


def _sgu_bwd(proj, dout_b, dproj, g_v, w_s, w_st, bst, width, z_block):
    t = proj.shape[0]
    groups = width // HEAD
    nblk = t // HEAD

    def body(u_ref, v_ref, do_ref, g_ref, ws_ref, wst_ref, bst_ref, dproj_hbm,
             dz_ref, dg_ref, dws_ref, dbst_ref, res_s):
        i, p = pl.program_id(0), pl.program_id(1)

        @pl.when((i == 0) & (p == 0))
        def _():
            dg_ref[...] = jnp.zeros_like(dg_ref)
            dws_ref[...] = jnp.zeros_like(dws_ref)
            dbst_ref[...] = jnp.zeros_like(dbst_ref)

        @pl.when(p == 0)
        def _():
            g_v = g_ref[...]
            u, du, dv, vhat, rstd, vn, vm = _sgu_core(u_ref[...], v_ref[...], g_v, ws_ref, bst_ref[...])
            dout = do_ref[...].astype(F32)
            res_s[0] = (dout * vm * du).astype(BF16)
            dvm = dout * u
            dvn_cols = []
            for g in range(groups):
                sl = slice(g * HEAD, (g + 1) * HEAD)
                dvm_g = dvm[:, sl]
                dbst_ref[:, g:g + 1] += jnp.sum(dvm_g, axis=1, keepdims=True)
                dws_ref[g] += _bdot(dvm_g, vn[:, sl], _NT)
                dvn_cols.append(_bdot(wst_ref[g], dvm_g))
            dvn = jnp.concatenate(dvn_cols, axis=1)
            dg_ref[...] += _colsum(dvn * vhat)
            dvh = dvn * g_v
            dvg = rstd * (dvh - jnp.mean(dvh, axis=-1, keepdims=True)
                          - vhat * jnp.mean(dvh * vhat, axis=-1, keepdims=True))
            res_s[1] = (dvg * dv).astype(BF16)

        dz_ref[...] = res_s[p]

    n_in = dproj.shape[1]
    return pl.pallas_call(
        body, name="sgu_bwd", grid=(nblk, 2),
        in_specs=[pl.BlockSpec((HEAD, width), lambda i, p: (i, z_block)),
                  pl.BlockSpec((HEAD, width), lambda i, p: (i, z_block + 1)),
                  pl.BlockSpec((HEAD, width), lambda i, p: (i, 0)),
                  pl.BlockSpec((1, width), lambda i, p: (0, 0)),
                  pl.BlockSpec((groups, HEAD, HEAD), lambda i, p: (0, 0, 0)),
                  pl.BlockSpec((groups, HEAD, HEAD), lambda i, p: (0, 0, 0)),
                  pl.BlockSpec((HEAD, groups), lambda i, p: (0, 0)),
                  pl.BlockSpec(memory_space=pl.ANY)],
        out_specs=[pl.BlockSpec((HEAD, width), lambda i, p: (i, z_block + p)),
                   pl.BlockSpec((1, width), lambda i, p: (0, 0)),
                   pl.BlockSpec((groups, HEAD, HEAD), lambda i, p: (0, 0, 0)),
                   pl.BlockSpec((HEAD, groups), lambda i, p: (0, 0))],
        out_shape=[jax.ShapeDtypeStruct((t, n_in), BF16), jax.ShapeDtypeStruct((1, width), F32),
                   jax.ShapeDtypeStruct((groups, HEAD, HEAD), F32), jax.ShapeDtypeStruct((HEAD, groups), F32)],
        scratch_shapes=[pltpu.VMEM((2, HEAD, width), BF16)],
        input_output_aliases={7: 0},
        compiler_params=_params(2),
    )(proj, proj, dout_b, g_v, w_s, w_st, bst, dproj)


def _hgrn_bwd(proj, osum, dout_a, dproj, lb, g_norm, width, exchange):
    t = proj.shape[0]
    heads = width // HEAD
    nb = t // HEAD
    cpb = HEAD // A_CHUNK
    q_scale = HEAD ** -0.5
    xin, xout = len(exchange.arrays), len(exchange.out_shapes)

    def body(q_ref, ffw_ref, fbw_ref, v_ref, og_ref, osum_ref, douta_ref, lb_ref, g_ref, dproj_hbm, *rest):
        xin_refs, rest = rest[:xin], rest[xin:]
        out_ref, dgh_ref, dlb_ref = rest[:3]
        xout_refs, rest = rest[3:3 + xout], rest[3 + xout:]
        do_s, dq_s, dv_s, res_s, ck_s = rest[:5]
        sem_refs = rest[5:]
        h, p = pl.program_id(0), pl.program_id(1)
        f_refs = (ffw_ref, fbw_ref)

        @pl.when((h == 0) & (p == 0))
        def _():
            exchange.start(xin_refs, xout_refs, sem_refs)

        @pl.when(p == 0)
        def _():
            tril, triu = _chunk_masks()
            cum2 = jnp.concatenate([_ones(tril), _ones(triu)], axis=0)
            g_row = g_ref[...]

            def pass_norm(i, dgh):
                rows = pl.ds(pl.multiple_of(i * HEAD, HEAD), HEAD)
                o = osum_ref[rows, :]
                r = _rms(o)
                oh = o * r
                og = og_ref[rows, :]
                sg = _sigmoid(og)
                dout = douta_ref[rows, :].astype(F32)
                don = dout * (og * sg)
                res_s[4, rows, :] = (dout * (oh * g_row) * (sg * (1.0 + og * (1.0 - sg)))).astype(BF16)
                doh = don * g_row
                do_s[rows, :] = r * (doh - oh * jnp.mean(doh * oh, axis=-1, keepdims=True))
                return dgh + _colsum(don * oh)

            dgh_ref[...] = lax.fori_loop(0, nb, pass_norm, jnp.zeros((1, HEAD), F32))

            lbs = (lb_ref[0:1, :], lb_ref[1:2, :])
            zero_state = jnp.zeros((HEAD, HEAD), F32)

            def chunk_order(d):
                return list(range(cpb)) if d == 0 else list(range(cpb - 1, -1, -1))

            def chunk(x, j):
                return x[j * A_CHUNK:(j + 1) * A_CHUNK, :]

            def decay_row(e_big, j):
                return e_big[j * A_CHUNK:j * A_CHUNK + 1, :]

            def cat(parts):
                return jnp.concatenate([parts[j] for j in range(cpb)], axis=0)

            def block_states(d, start, incs, e_big):
                befores, st = {}, start
                for j in chunk_order(d):
                    befores[j] = st
                    st = st * decay_row(e_big, j) + incs[j]
                return befores, st

            def pass_states(it, states):
                loaded = []
                for d in range(2):
                    blk = it if d == 0 else nb - 1 - it
                    rows = pl.ds(pl.multiple_of(blk * HEAD, HEAD), HEAD)
                    loaded.append((d, blk, f_refs[d][rows, :], v_ref[rows, :]))
                blocks = [_hgrn_block(d, fv, lbs[d], cum2) for d, _, fv, _ in loaded]
                incs = [{j: _bdot(chunk(vv, j), chunk(k * erest, j), _TN) for j in range(cpb)}
                        for (_, _, _, vv), (k, _, _, _, _, erest) in zip(loaded, blocks)]
                after = [block_states(d, states[d], inc, eb * erest)[1]
                         for (d, _, _, _), (_, _, _, eb, _, erest), inc in zip(loaded, blocks, incs)]
                for d, blk, _, _ in loaded:
                    ck_s[d, blk] = states[d]
                return tuple(after)

            lax.fori_loop(0, nb, pass_states, (zero_state, zero_state))

            def pass_back(it, carry):
                gts, dlb = [carry[0], carry[1]], carry[2]
                loaded = []
                for d in range(2):
                    blk = nb - 1 - it if d == 0 else it
                    rows = pl.ds(pl.multiple_of(blk * HEAD, HEAD), HEAD)
                    loaded.append((d, rows, f_refs[d][rows, :], q_ref[rows, :], v_ref[rows, :], do_s[rows, :], ck_s[d, blk]))
                blocks = [_hgrn_block(d, fv, lbs[d], cum2) for d, _, fv, _, _, _, _ in loaded]
                scaled = []
                for (_, _, _, qv, _, _, _), (k, _, _, eb, enb, erest) in zip(loaded, blocks):
                    qh = qv * q_scale
                    scaled.append((qh, qh * eb, k * enb, k * erest, eb * erest))
                masks = [tril if d == 0 else triu for d, *_ in loaded]
                atts = [jnp.where(m, _bdot(qd, kd, _NT), 0.0) for m, (_, qd, kd, _, _) in zip(masks, scaled)]
                datts = [jnp.where(m, _bdot(do, vv, _NT), 0.0) for m, (_, _, _, _, vv, do, _) in zip(masks, loaded)]
                dvs = [_bdot(att, do, _TN) for att, (_, _, _, _, _, do, _) in zip(atts, loaded)]
                dqds = [_bdot(datt, kd) for datt, (_, _, kd, _, _) in zip(datts, scaled)]
                dkds = [_bdot(datt, qd, _TN) for datt, (_, qd, _, _, _) in zip(datts, scaled)]
                s_incs = [{j: _bdot(chunk(vv, j), chunk(ke, j), _TN) for j in range(cpb)}
                          for (_, _, _, _, vv, _, _), (_, _, _, ke, _) in zip(loaded, scaled)]
                g_incs = [{j: _bdot(chunk(do, j), chunk(qd, j), _TN) for j in range(cpb)}
                          for (_, _, _, _, _, do, _), (_, qd, _, _, _) in zip(loaded, scaled)]
                befores, afters, g_at = [], [], []
                for (d, _, _, _, _, _, ck), (_, _, _, _, e_big), s_inc, g_inc in zip(loaded, scaled, s_incs, g_incs):
                    order = chunk_order(d)
                    before, after = block_states(d, ck, s_inc, e_big)
                    befores.append(before)
                    afters.append({j: (before[order[n + 1]] if n + 1 < cpb else after) for n, j in enumerate(order)})
                    at, gt = {}, gts[d]
                    for j in reversed(order):
                        at[j] = gt
                        gt = gt * decay_row(e_big, j) + g_inc[j]
                    gts[d] = gt
                    g_at.append(at)
                dqd_i = [{j: _bdot(chunk(do, j), before[j]) for j in range(cpb)}
                         for (_, _, _, _, _, do, _), before in zip(loaded, befores)]
                dv_i = [{j: _bdot(chunk(ke, j), at[j], _NT) for j in range(cpb)}
                        for (_, _, _, ke, _), at in zip(scaled, g_at)]
                dke = [{j: _bdot(chunk(vv, j), at[j]) for j in range(cpb)}
                       for (_, _, _, _, vv, _, _), at in zip(loaded, g_at)]
                results, new = [], []
                for n, ((d, rows, _, _, _, _, _), (k, sf, big_f, eb, enb, erest), (qh, _, _, _, _)) in enumerate(
                        zip(loaded, blocks, scaled)):
                    dqh = (dqds[n] + cat(dqd_i[n])) * eb
                    dk = dkds[n] * enb + cat(dke[n]) * erest
                    carry_rows = {j: jnp.broadcast_to(_colsum(g_at[n][j] * afters[n][j]), (A_CHUNK, HEAD))
                                  for j in range(cpb)}
                    dlf = _dot_split(_ones(triu if d == 0 else tril), qh * dqh - k * dk) + cat(carry_rows)
                    common = dlf / big_f - dk
                    results.append((d, rows, (k * sf * common).astype(BF16), dqh.astype(BF16),
                                    (dvs[n] + cat(dv_i[n])).astype(BF16)))
                    new.append(_colsum((1.0 - sf) * common))
                for d, rows, df16, dq16, dv16 in results:
                    res_s[1 + d, rows, :] = df16
                    dq_s[d, rows, :] = dq16
                    dv_s[d, rows, :] = dv16
                return gts[0], gts[1], dlb + jnp.concatenate(new, axis=0)

            dlb_ref[...] = lax.fori_loop(0, nb, pass_back, (zero_state, zero_state, jnp.zeros((2, HEAD), F32)))[2]

            def pass_out(i, carry):
                rows = pl.ds(pl.multiple_of(i * HEAD, HEAD), HEAD)
                dq = dq_s[0, rows, :].astype(F32) + dq_s[1, rows, :].astype(F32)
                res_s[0, rows, :] = (dq * q_scale).astype(BF16)
                res_s[3, rows, :] = (dv_s[0, rows, :].astype(F32) + dv_s[1, rows, :].astype(F32)).astype(BF16)
                return carry

            lax.fori_loop(0, nb, pass_out, 0)

        out_ref[...] = res_s[p]

        @pl.when((h == heads - 1) & (p == 4))
        def _():
            exchange.finish(xin_refs, xout_refs, sem_refs)

    def col(pp):
        return pl.BlockSpec((t, HEAD), lambda h, p: (0, pp * heads + h))

    n_in = dproj.shape[1]
    any_spec = pl.BlockSpec(memory_space=pl.ANY)
    results = pl.pallas_call(
        body, name="hgrn_bwd", grid=(heads, 5),
        in_specs=[col(0), col(1), col(2), col(3), col(4),
                  pl.BlockSpec((t, HEAD), lambda h, p: (0, h)), pl.BlockSpec((t, HEAD), lambda h, p: (0, h)),
                  pl.BlockSpec((2, HEAD), lambda h, p: (0, h)), pl.BlockSpec((1, HEAD), lambda h, p: (0, 0)),
                  any_spec] + [any_spec] * xin,
        out_specs=[pl.BlockSpec((t, HEAD), lambda h, p: (0, p * heads + h)),
                   pl.BlockSpec((None, 1, HEAD), lambda h, p: (h, 0, 0)),
                   pl.BlockSpec((2, HEAD), lambda h, p: (0, h))] + [any_spec] * xout,
        out_shape=[jax.ShapeDtypeStruct((t, n_in), BF16), jax.ShapeDtypeStruct((heads, 1, HEAD), F32),
                   jax.ShapeDtypeStruct((2, width), F32)] + list(exchange.out_shapes),
        scratch_shapes=[pltpu.VMEM((t, HEAD), F32), pltpu.VMEM((2, t, HEAD), BF16), pltpu.VMEM((2, t, HEAD), BF16),
                        pltpu.VMEM((5, t, HEAD), BF16), pltpu.VMEM((2, nb, HEAD, HEAD), F32)] + list(exchange.sems),
        input_output_aliases={9: 0},
        compiler_params=_params(2),
    )(proj, proj, proj, proj, proj, osum, dout_a, lb, g_norm, dproj, *exchange.arrays)
    return results[0], results[1], results[2], results[3:]


def _adamw(w, g, m, v):
    m = ADAM_B1 * m + (1.0 - ADAM_B1) * g
    v = ADAM_B2 * v + (1.0 - ADAM_B2) * (g * g)
    m_hat = m / (1.0 - ADAM_B1 ** ADAM_STEP)
    v_hat = v / (1.0 - ADAM_B2 ** ADAM_STEP)
    delta = -ADAM_LR * (m_hat / (jnp.sqrt(v_hat) + ADAM_EPS) + ADAM_WD * w)
    return delta, m, v


def _adamw_big(name, me, w, m, v, g_full, landing, axis):
    r, c = w.shape
    tr = _tile(r, 128)

    def body(me_ref, w_ref, m_ref, v_ref, g_ref, l_ref, og_ref, od_ref, om_ref, ov_ref):
        g = g_ref[...]
        for s in range(N_DEV - 1):
            g = g + l_ref[s].astype(F32)
        og_ref[...] = g
        od_ref[...], om_ref[...], ov_ref[...] = _adamw(w_ref[...], g, m_ref[...], v_ref[...])

    shard = pl.BlockSpec((tr, c), lambda i, me_ref: (i, 0))
    if axis == 1:
        own = pl.BlockSpec((tr, c), lambda i, me_ref: (i, me_ref[0]))
    else:
        own = pl.BlockSpec((tr, c), lambda i, me_ref: (me_ref[0] * (r // tr) + i, 0))
    grid_spec = pltpu.PrefetchScalarGridSpec(
        num_scalar_prefetch=1, grid=(r // tr,),
        in_specs=[shard, shard, shard, own, pl.BlockSpec((N_DEV - 1, tr, c), lambda i, me_ref: (0, i, 0))],
        out_specs=[shard] * 4)
    return pl.pallas_call(
        body, name=name, grid_spec=grid_spec, out_shape=[jax.ShapeDtypeStruct((r, c), F32)] * 4,
        compiler_params=_params(1),
    )(me, w, m, v, g_full, landing)


def _adamw_ada(sct, dmod_mine, w, m, v):
    d, n = w.shape
    tr = _tile(d, 256)

    def body(s_ref, dm_ref, w_ref, m_ref, v_ref, og_ref, od_ref, om_ref, ov_ref):
        g = _dot(s_ref[...], dm_ref[...], precision=HIGHEST)
        og_ref[...] = g
        od_ref[...], om_ref[...], ov_ref[...] = _adamw(w_ref[...], g, m_ref[...], v_ref[...])

    blk = pl.BlockSpec((tr, n), lambda i: (i, 0))
    return pl.pallas_call(
        body, name="adamw_ada", grid=(d // tr,),
        in_specs=[pl.BlockSpec((tr, N_DEV), lambda i: (i, 0)), pl.BlockSpec((N_DEV, n), lambda i: (0, 0)), blk, blk, blk],
        out_specs=[blk] * 4, out_shape=[jax.ShapeDtypeStruct((d, n), F32)] * 4, compiler_params=_params(1),
    )(sct, dmod_mine, w, m, v)


def _adamw_small(gathered, w, m, v):
    def body(g_ref, w_ref, m_ref, v_ref, og_ref, od_ref, om_ref, ov_ref):
        g = g_ref[0]
        for s in range(1, N_DEV):
            g = g + g_ref[s]
        og_ref[...] = g
        od_ref[...], om_ref[...], ov_ref[...] = _adamw(w_ref[...], g, m_ref[...], v_ref[...])

    return pl.pallas_call(
        body, name="adamw_small", out_shape=[jax.ShapeDtypeStruct(w.shape, F32)] * 4,
        compiler_params=pltpu.CompilerParams(vmem_limit_bytes=VMEM_LIMIT),
    )(gathered, w, m, v)


def _adamw_lb(dlb_mine, lb_logits, m, v):
    def body(d_ref, l_ref, m_ref, v_ref, og_ref, od_ref, om_ref, ov_ref):
        dlb = d_ref[0]
        for s in range(1, N_DEV):
            dlb = dlb + d_ref[s]
        for dr in range(2):
            lb = _sigmoid(l_ref[dr][0:1, :] - l_ref[dr][1:2, :])
            d0 = dlb[dr:dr + 1] * lb * (1.0 - lb)
            g = jnp.concatenate([d0, -d0], axis=0)
            og_ref[dr] = g
            od_ref[dr], om_ref[dr], ov_ref[dr] = _adamw(l_ref[dr], g, m_ref[dr], v_ref[dr])

    return pl.pallas_call(body, name="adamw_lb", out_shape=[jax.ShapeDtypeStruct(lb_logits.shape, F32)] * 4,
                          )(dlb_mine, lb_logits, m, v)


def _rows(a, pad_to=8):
    flat = a.reshape(-1, LANE)
    pad = (-flat.shape[0]) % pad_to
    return jnp.pad(flat, ((0, pad), (0, 0))) if pad else flat


def kernel(x, c, w_ada, b_ada, g_pre_mix, g_post_mix, g_pre_ffn, g_post_ffn, w_in, lb_logits, g_hgrn_norm, w_a_out, g_sgu_norm, w_spatial, b_spatial, w_b_out, w_o, w_ff1, w_ff2, loss_target, m_w_ada, m_b_ada, m_g_pre_mix, m_g_post_mix, m_g_pre_ffn, m_g_post_ffn, m_w_in, m_lb_logits, m_g_hgrn_norm, m_w_a_out, m_g_sgu_norm, m_w_spatial, m_b_spatial, m_w_b_out, m_w_o, m_w_ff1, m_w_ff2, v_w_ada, v_b_ada, v_g_pre_mix, v_g_post_mix, v_g_pre_ffn, v_g_post_ffn, v_w_in, v_lb_logits, v_g_hgrn_norm, v_w_a_out, v_g_sgu_norm, v_w_spatial, v_b_spatial, v_w_b_out, v_w_o, v_w_ff1, v_w_ff2):
    t, d = x.shape[1], x.shape[2]
    n_in = w_in.shape[2] * N_DEV
    width = (n_in - 2 * d) // 7
    heads = width // HEAD
    assert heads == N_DEV and width % LANE == 0
    d_ff = w_ff1.shape[2] * N_DEV
    n_ada = w_ada.shape[2]
    me = _dev_index()
    me_arr = me.reshape(1).astype(jnp.int32)
    x2, tgt = x[0], loss_target[0]

    big = [w_in[0], w_a_out[0], w_b_out[0], w_o[0], w_ff1[0], w_ff2[0]]
    big_axes = [1, 1, 1, 0, 1, 0]
    big_names = ["w_in", "w_a_out", "w_b_out", "w_o", "w_ff1", "w_ff2"]
    wf_in, = _run_exchange("gather_w_in", _gather_plan([_cast_bf16("cast_w_in", big[0])], big_axes[:1]))
    own_parts = [_cast_into_full("cast_" + nm, me_arr, w, ax) for nm, w, ax in zip(big_names[1:], big[1:], big_axes[1:])]
    wf_in, own_parts = lax.optimization_barrier((wf_in, own_parts))
    gathers = {}
    for key, lo, hi in (("mid", 1, 4), ("ff1", 4, 5), ("ff2", 5, 6)):
        plan = _direct_gather_plan(own_parts[lo - 1:hi - 1], big_axes[lo:hi])
        gathers[key] = (plan,) + _split_start("gather_%s_start" % key, plan, landing=own_parts[lo - 1:hi - 1])

    def gathered_weights(key, after):
        plan, sems, thru, _ = gathers[key]
        return _split_wait("gather_%s_wait" % key, plan, sems, thru, after)[1]

    c_rows = d // LANE
    small = _all_gather_small("gather_c_lb", _prep_small(c[0:1], lb_logits))
    sc_all = small[:, :c_rows, :].reshape(N_DEV, d)
    lb = jnp.transpose(small[:, c_rows:c_rows + 2, :], (1, 0, 2)).reshape(2, width)
    b_shard = lax.dynamic_slice_in_dim(b_ada, me * n_ada, n_ada, axis=1)
    mod_sh = _mod_shard(sc_all, w_ada[0], b_shard)
    mod_all = _all_gather_small("gather_mod", _rows(mod_sh))
    mod_all = mod_all[:, :N_DEV * n_ada // LANE, :].reshape(N_DEV, N_DEV, n_ada)
    mod6 = lax.dynamic_index_in_dim(mod_all, me, axis=1, keepdims=False).reshape(N_MOD, d)
    sh1, sc1, gt1, sh2, sc2, gt2 = [mod6[i:i + 1] for i in range(N_MOD)]

    a1 = _norm_mod(x2, g_pre_mix, sh1, sc1)
    tm = _tile(t, 512)

    def store_f32(acc, i, j, extra_refs, out_refs, rows):
        out_refs[0][...] = acc

    tn_in = _tile(n_in, 1024)
    started = [tok for key in ("mid", "ff1", "ff2") for tok in _after(gathers[key][3])]
    proj, = _mm("proj", a1, wf_in, _NN, t, n_in, d, tm, tn_in, d, started,
                [(jax.ShapeDtypeStruct((t, n_in), F32), (tm, tn_in), lambda i, j: (i, j))], store_f32)

    out_a, osum, _ = _hgrn_fwd(proj, lb, g_hgrn_norm, width, _NO_EXCHANGE)
    z_block = 5
    bst = b_spatial[0].T
    out_b = _sgu_fwd(proj, g_sgu_norm, w_spatial[0], bst, width, z_block)
    wf_a, wf_b, wf_o = gathered_weights("mid", out_b)

    tn_d = _tile(d, 512)
    blk_d = ((tm, tn_d), lambda i, j: (i, j))
    y_a, = _mm("y_a", out_a, wf_a, _NN, t, d, width, tm, tn_d, width, [],
               [(jax.ShapeDtypeStruct((t, d), F32),) + blk_d], store_f32)
    ga_blk = (5 * width + 2 * width) // tn_d
    gb_blk = ga_blk + d // tn_d

    def merge(acc, i, j, extra_refs, out_refs, rows):
        ga, gb, ya = extra_refs
        out_refs[0][...] = acc
        out_refs[1][...] = (_sigmoid(ga[...]) * ya[...] + _sigmoid(gb[...]) * acc).astype(BF16)

    y_b, merged = _mm("y_b_merge", out_b, wf_b, _NN, t, d, width, tm, tn_d, width,
                      [(proj, (tm, tn_d), lambda i, j: (i, ga_blk + j)), (proj, (tm, tn_d), lambda i, j: (i, gb_blk + j)),
                       (y_a,) + blk_d],
                      [(jax.ShapeDtypeStruct((t, d), F32),) + blk_d, (jax.ShapeDtypeStruct((t, d), BF16),) + blk_d], merge)

    tr = _tile(t, 256)
    rc = 64 if tr % 64 == 0 else None
    row_d = ((tr, d), lambda i, j: (i, 0))
    vec_d = ((1, d), lambda i, j: (0, 0))

    def post_mix(acc, i, j, extra_refs, out_refs, rows):
        x_r, gt1_r, g2_r, g3_r, sc2_r, sh2_r = extra_refs
        h1 = x_r[rows, :] + gt1_r[...] * (acc * _rms(acc) * g2_r[...])
        out_refs[0][rows, :] = acc
        out_refs[1][rows, :] = h1
        out_refs[2][rows, :] = ((h1 * _rms(h1) * g3_r[...]) * (1.0 + sc2_r[...]) + sh2_r[...]).astype(BF16)

    mo, h1, a2 = _mm("w_o_post_mix", merged, wf_o, _NN, t, d, d, tr, d, d,
                     [(x2,) + row_d, (gt1,) + vec_d, (g_post_mix,) + vec_d, (g_pre_ffn,) + vec_d, (sc2,) + vec_d, (sh2,) + vec_d],
                     [(jax.ShapeDtypeStruct((t, d), F32),) + row_d, (jax.ShapeDtypeStruct((t, d), F32),) + row_d,
                      (jax.ShapeDtypeStruct((t, d), BF16),) + row_d], post_mix, row_chunk=rc)

    tn_f = _tile(d_ff, 1024)
    blk_f = ((tm, tn_f), lambda i, j: (i, j))

    def relu_sq(acc, i, j, extra_refs, out_refs, rows):
        r = jnp.maximum(acc, 0.0)
        out_refs[0][...] = acc.astype(BF16)
        out_refs[1][...] = (r * r).astype(BF16)

    wf_1, = gathered_weights("ff1", a2)
    hff, act = _mm(
        "ff1", a2, wf_1, _NN, t, d_ff, d, tm, tn_f, d, [],
        [(jax.ShapeDtypeStruct((t, d_ff), BF16),) + blk_f, (jax.ShapeDtypeStruct((t, d_ff), BF16),) + blk_f], relu_sq)
    wf_2, = gathered_weights("ff2", act)

    sums_d = ((8, d), lambda i, j: (0, 0))

    def zero_first(sums_r, i, rows):
        if rows.start in (None, 0):
            @pl.when(i == 0)
            def _():
                sums_r[...] = jnp.zeros_like(sums_r)

    def loss_head(acc, i, j, extra_refs, out_refs, rows):
        h1_r, tgt_r, gt2_r, g4_r = extra_refs
        dy_r, dff_r, sums_r = out_refs
        r4 = _rms(acc)
        ffn = acc * r4
        n4 = ffn * g4_r[...]
        err = h1_r[rows, :] + gt2_r[...] * n4 - tgt_r[rows, :]
        dy = err * (1.0 / d)
        dy_r[rows, :] = dy
        dn4 = dy * gt2_r[...]
        dffn = dn4 * g4_r[...]
        dff_r[rows, :] = (r4 * (dffn - ffn * jnp.mean(dffn * ffn, axis=-1, keepdims=True))).astype(BF16)
        zero_first(sums_r, i, rows)

        sums_r[0:1, :] += _colsum(err * err)
        sums_r[1:2, :] += _colsum(dy * n4)
        sums_r[2:3, :] += _colsum(dn4 * ffn)

    tk_f = _tile(d_ff, 2048)
    dy, dff, sums_f = _mm("ff2_loss", act, wf_2, _NN, t, d, d_ff, tr, d, tk_f,
                          [(h1,) + row_d, (tgt,) + row_d, (gt2,) + vec_d, (g_post_ffn,) + vec_d],
                          [(jax.ShapeDtypeStruct((t, d), F32),) + row_d, (jax.ShapeDtypeStruct((t, d), BF16),) + row_d,
                           (jax.ShapeDtypeStruct((8, d), F32),) + sums_d], loss_head, row_chunk=rc)
    loss_mine = (0.5 / d) * jnp.sum(sums_f[0])

    def relu_sq_bwd(acc, i, j, extra_refs, out_refs, rows):
        out_refs[0][...] = (acc * (2.0 * jnp.maximum(extra_refs[0][...].astype(F32), 0.0))).astype(BF16)

    dhff, = _mm("d_hff", dff, wf_2, _NT, t, d_ff, d, tm, tn_f, d, [(hff,) + blk_f],
                [(jax.ShapeDtypeStruct((t, d_ff), BF16),) + blk_f], relu_sq_bwd)
    scatters = {}

    def send_grads(key, grads16, axes):
        plan = _scatter_plan(grads16, axes)
        scatters[key] = (plan,) + _split_start("scatter_%s_start" % key, plan)
        return scatters[key][3]

    def received_grads(key, after):
        plan, sems, thru, _ = scatters[key]
        return _split_wait("scatter_%s_wait" % key, plan, sems, thru, after)[1]

    gw_ff2, gw_ff2_16 = _grad_w("grad_w_ff2", act, dff)
    sent_ff2 = send_grads("ff2", [gw_ff2_16], big_axes[5:6])
    gw_ff1, gw_ff1_16 = _grad_w("grad_w_ff1", a2, dhff, token=sent_ff2)
    sent_ff1 = send_grads("ff1", [gw_ff1_16], big_axes[4:5])

    def pre_ffn_bwd(acc, i, j, extra_refs, out_refs, rows):
        h1_r, dy_r, mo_r, sc2_r, g3_r, gt1_r, g2_r = extra_refs[:7]
        dh1_r, dmo_r, sums_r = out_refs
        h1v = h1_r[rows, :]
        r3 = _rms(h1v)
        h1n = h1v * r3
        dn3 = acc * (1.0 + sc2_r[...])
        dh1n = dn3 * g3_r[...]
        dh1 = dy_r[rows, :] + r3 * (dh1n - h1n * jnp.mean(dh1n * h1n, axis=-1, keepdims=True))
        dh1_r[rows, :] = dh1
        mov = mo_r[rows, :]
        r2 = _rms(mov)
        mon = mov * r2
        dn2 = dh1 * gt1_r[...]
        dmon = dn2 * g2_r[...]
        dmo_r[rows, :] = (r2 * (dmon - mon * jnp.mean(dmon * mon, axis=-1, keepdims=True))).astype(BF16)
        zero_first(sums_r, i, rows)

        sums_r[0:1, :] += _colsum(acc)
        sums_r[1:2, :] += _colsum(acc * (h1n * g3_r[...]))
        sums_r[2:3, :] += _colsum(dn3 * h1n)
        sums_r[3:4, :] += _colsum(dh1 * (mon * g2_r[...]))
        sums_r[4:5, :] += _colsum(dn2 * mon)

    dh1, dmo, sums_m = _mm("d_a2_pre_ffn", dhff, wf_1, _NT, t, d, d_ff, tr, d, tk_f,
                           [(h1,) + row_d, (dy,) + row_d, (mo,) + row_d, (sc2,) + vec_d, (g_pre_ffn,) + vec_d,
                            (gt1,) + vec_d, (g_post_mix,) + vec_d] + _after(sent_ff1),
                           [(jax.ShapeDtypeStruct((t, d), F32),) + row_d, (jax.ShapeDtypeStruct((t, d), BF16),) + row_d,
                            (jax.ShapeDtypeStruct((8, d), F32),) + sums_d], pre_ffn_bwd, row_chunk=rc)
    gw_o, gw_o_16 = _grad_w("grad_w_o", merged, dmo)

    n_j = d // tn_d

    def merge_bwd_body(dmo_ref, wo_ref, ga_ref, gb_ref, ya_ref, yb_ref, dya_ref, dyb_ref, dproj_ref, acc_s):
        g = pl.program_id(2)

        @pl.when(g == 0)
        def _():
            dm = _dot(dmo_ref[...], wo_ref[...], _NT)
            acc_s[...] = dm
            sa = _sigmoid(ga_ref[...])
            dya_ref[...] = (dm * sa).astype(BF16)
            dproj_ref[...] = (dm * ya_ref[...] * sa * (1.0 - sa)).astype(BF16)

        @pl.when(g == 1)
        def _():
            dm = acc_s[...]
            sb = _sigmoid(gb_ref[...])
            dyb_ref[...] = (dm * sb).astype(BF16)
            dproj_ref[...] = (dm * yb_ref[...] * sb * (1.0 - sb)).astype(BF16)

    tile3 = pl.BlockSpec((tm, tn_d), lambda i, j, g: (i, j))
    dy_a, dy_b, dproj = pl.pallas_call(
        merge_bwd_body, name="d_merged", grid=(t // tm, n_j, 2),
        in_specs=[pl.BlockSpec((tm, d), lambda i, j, g: (i, 0)), pl.BlockSpec((tn_d, d), lambda i, j, g: (j, 0)),
                  pl.BlockSpec((tm, tn_d), lambda i, j, g: (i, ga_blk + j)),
                  pl.BlockSpec((tm, tn_d), lambda i, j, g: (i, gb_blk + j)), tile3, tile3],
        out_specs=[tile3, tile3, pl.BlockSpec((tm, tn_d), lambda i, j, g: (i, ga_blk + g * n_j + j))],
        out_shape=[jax.ShapeDtypeStruct((t, d), BF16), jax.ShapeDtypeStruct((t, d), BF16),
                   jax.ShapeDtypeStruct((t, n_in), BF16)],
        scratch_shapes=[pltpu.VMEM((tm, tn_d), F32)], compiler_params=_params(3),
    )(dmo, wf_o, proj, proj, y_a, y_b)

    def store_bf16(acc, i, j, extra_refs, out_refs, rows):
        out_refs[0][...] = acc.astype(BF16)

    tn_w = _tile(width, 512)
    blk_w = ((tm, tn_w), lambda i, j: (i, j))
    dout_a, = _mm("d_out_a", dy_a, wf_a, _NT, t, width, d, tm, tn_w, d, [],
                  [(jax.ShapeDtypeStruct((t, width), BF16),) + blk_w], store_bf16)
    dout_b, = _mm("d_out_b", dy_b, wf_b, _NT, t, width, d, tm, tn_w, d, [],
                  [(jax.ShapeDtypeStruct((t, width), BF16),) + blk_w], store_bf16)
    gw_a, gw_a_16 = _grad_w("grad_w_a_out", out_a, dy_a)
    gw_b, gw_b_16 = _grad_w("grad_w_b_out", out_b, dy_b)

    w_st = jnp.swapaxes(w_spatial[0], 1, 2)
    dproj, dg_sgu, dw_sp, dbst = _sgu_bwd(proj, dout_b, dproj, g_sgu_norm, w_spatial[0], w_st, bst, width, z_block)
    sent_mid = send_grads("mid", [gw_a_16, gw_b_16, gw_o_16], big_axes[1:4])
    dproj, dgh_heads, dlb, _ = _hgrn_bwd(proj, osum, dout_a, dproj, lb, g_hgrn_norm, width, _NO_EXCHANGE)
    gw_in, gw_in_16 = _grad_w("grad_w_in", a1, dproj, token=sent_mid)
    sent_in = send_grads("in", [gw_in_16], big_axes[:1])

    def pre_mix_bwd(acc, i, j, extra_refs, out_refs, rows):
        x_r, dh1_r, sc1_r, g1_r = extra_refs[:4]
        dx_r, sums_r = out_refs
        xv = x_r[rows, :]
        r1 = _rms(xv)
        xn = xv * r1
        dn1 = acc * (1.0 + sc1_r[...])
        dxn = dn1 * g1_r[...]
        dx_r[rows, :] = dh1_r[rows, :] + r1 * (dxn - xn * jnp.mean(dxn * xn, axis=-1, keepdims=True))
        zero_first(sums_r, i, rows)

        sums_r[0:1, :] += _colsum(acc)
        sums_r[1:2, :] += _colsum(acc * (xn * g1_r[...]))
        sums_r[2:3, :] += _colsum(dn1 * xn)

    tk_in = _tile(n_in, 2816)
    grad_x, sums_x = _mm(
        "d_a1_pre_mix", dproj, wf_in, _NT, t, d, n_in, tr, d, tk_in,
        [(x2,) + row_d, (dh1,) + row_d, (sc1,) + vec_d, (g_pre_mix,) + vec_d] + _after(sent_in),
        [(jax.ShapeDtypeStruct((t, d), F32),) + row_d, (jax.ShapeDtypeStruct((8, d), F32),) + sums_d],
        pre_mix_bwd, row_chunk=rc)

    dmod = jnp.concatenate([sums_x[0:2], sums_m[3:4], sums_m[0:2], sums_f[1:2]], axis=0).reshape(N_DEV, n_ada // LANE, LANE)
    ada_rows = -(-(n_ada // LANE) // 8) * 8
    dmod = jnp.pad(dmod, ((0, 0), (0, ada_rows - n_ada // LANE), (0, 0))).reshape(N_DEV * ada_rows, LANE)
    parts = [dmod, _rows(sums_x[2:3]), _rows(sums_m[4:5]), _rows(sums_m[2:3]), _rows(sums_f[2:3]),
             _rows(jnp.sum(dgh_heads, axis=0)), _rows(dg_sgu), _rows(dw_sp), _rows(dbst.T)]
    n_params = sum(p.shape[0] for p in parts)
    parts.append(jnp.full((8, LANE), loss_mine, F32))
    n_common = n_params + 8
    payload = jnp.concatenate(parts + [_rows(dlb)], axis=0)

    moms = [m_w_in, m_w_a_out, m_w_b_out, m_w_o, m_w_ff1, m_w_ff2]
    vars_ = [v_w_in, v_w_a_out, v_w_b_out, v_w_o, v_w_ff1, v_w_ff2]
    big_out = {}

    def big_update(nm, g_full, landing):
        k = big_names.index(nm)
        outs = _adamw_big("adamw_" + nm, me_arr, big[k], moms[k][0], vars_[k][0], g_full, landing, big_axes[k])
        big_out[nm] = [o[None] for o in outs]
        return outs[0]

    land_ff2, = received_grads("ff2", grad_x)
    done = big_update("w_ff2", gw_ff2, land_ff2)
    land_ff1, = received_grads("ff1", done)
    done = big_update("w_ff1", gw_ff1, land_ff1)
    land_a, land_b, land_o = received_grads("mid", done)
    big_update("w_a_out", gw_a, land_a)
    big_update("w_b_out", gw_b, land_b)
    done = big_update("w_o", gw_o, land_o)

    payload, _ = lax.optimization_barrier((payload, done))
    gathered = _all_gather_small("gather_small_grads", payload)

    dmod_mine = lax.dynamic_slice_in_dim(gathered[:, :N_DEV * ada_rows, :].reshape(N_DEV, N_DEV, ada_rows * LANE),
                                         me, 1, axis=1)[:, 0, :n_ada]
    ada_out = [o[None] for o in _adamw_ada(sc_all.T, dmod_mine, w_ada[0], m_w_ada[0], v_w_ada[0])]

    def pack(b_, g1_, g2_, g3_, g4_, gh_, gs_, ws_, bs_):
        b3 = b_.reshape(N_DEV, n_ada // LANE, LANE)
        b3 = jnp.pad(b3, ((0, 0), (0, ada_rows - n_ada // LANE), (0, 0))).reshape(N_DEV * ada_rows, LANE)
        return jnp.concatenate([b3, _rows(g1_), _rows(g2_), _rows(g3_), _rows(g4_), _rows(gh_), _rows(gs_),
                                _rows(ws_), _rows(bs_), jnp.zeros((8, LANE), F32)], axis=0)

    small_w = (b_ada, g_pre_mix, g_post_mix, g_pre_ffn, g_post_ffn, g_hgrn_norm, g_sgu_norm, w_spatial, b_spatial)
    small_m = (m_b_ada, m_g_pre_mix, m_g_post_mix, m_g_pre_ffn, m_g_post_ffn, m_g_hgrn_norm, m_g_sgu_norm, m_w_spatial, m_b_spatial)
    small_v = (v_b_ada, v_g_pre_mix, v_g_post_mix, v_g_pre_ffn, v_g_post_ffn, v_g_hgrn_norm, v_g_sgu_norm, v_w_spatial, v_b_spatial)
    packed = _adamw_small(gathered[:, :n_common, :], pack(*small_w), pack(*small_m), pack(*small_v))

    def unpack(slab):
        outs, at = [], 0
        b3 = slab[:N_DEV * ada_rows].reshape(N_DEV, ada_rows, LANE)[:, :n_ada // LANE, :]
        outs.append(b3.reshape(b_ada.shape))
        at = N_DEV * ada_rows
        for ref in small_w[1:]:
            n_el = ref.size
            n_r = -(-(n_el // LANE) // 8) * 8
            outs.append(slab[at:at + n_el // LANE].reshape(ref.shape))
            at += n_r
        return outs

    small_out = [unpack(s) for s in packed]
    loss = packed[0][n_params, 0]

    dlb_all = gathered[:, n_common:n_common + 2 * heads, :].reshape(N_DEV, 2, heads, LANE)
    dlb_mine = lax.dynamic_index_in_dim(dlb_all, me, axis=2, keepdims=False)
    lb_out = _adamw_lb(dlb_mine, lb_logits, m_lb_logits, v_lb_logits)

    land_in, = received_grads("in", ada_out[0])
    big_update("w_in", gw_in, land_in)

    order = ["w_ada", "b_ada", "g_pre_mix", "g_post_mix", "g_pre_ffn", "g_post_ffn", "w_in", "lb_logits", "g_hgrn_norm",
             "w_a_out", "g_sgu_norm", "w_spatial", "b_spatial", "w_b_out", "w_o", "w_ff1", "w_ff2"]
    small_names = ["b_ada", "g_pre_mix", "g_post_mix", "g_pre_ffn", "g_post_ffn", "g_hgrn_norm", "g_sgu_norm", "w_spatial", "b_spatial"]

    def leaf(kind, nm):
        if nm == "w_ada":
            return ada_out[kind]
        if nm == "lb_logits":
            return lb_out[kind]
        if nm in big_out:
            return big_out[nm][kind]
        return small_out[kind][small_names.index(nm)]

    result = [loss, grad_x[None]]
    for kind in range(4):
        result += [leaf(kind, nm) for nm in order]
    return tuple(result)
```

```python
import functools
import math

import jax
import jax.numpy as jnp
from jax import lax
from jax.experimental import pallas as pl
from jax.experimental.pallas import tpu as pltpu

F32 = jnp.float32
BF16 = jnp.bfloat16
MESH = pl.DeviceIdType.MESH
HIGHEST = lax.Precision.HIGHEST

N_DEV = 8
HEAD = 128
A_CHUNK = 32
N_MOD = 6
EPS = 1e-6
LANE = 128
VMEM_LIMIT = 60 * 1024 * 1024

ADAM_LR = 0.001
ADAM_B1 = 0.9
ADAM_B2 = 0.999
ADAM_EPS = 1e-08
ADAM_WD = 0.01
ADAM_STEP = 10

_NN = (((1,), (0,)), ((), ()))
_NT = (((1,), (1,)), ((), ()))
_TN = (((0,), (0,)), ((), ()))


def _dot(a, b, dims=_NN, precision=None):
    return lax.dot_general(a, b, dims, preferred_element_type=F32, precision=precision)


def _bdot(a, b, dims=_NN):
    return _dot(a.astype(BF16), b.astype(BF16), dims)


def _params(n_grid):
    return pltpu.CompilerParams(dimension_semantics=("arbitrary",) * n_grid, vmem_limit_bytes=VMEM_LIMIT)


def _dev_index():
    return lax.axis_index("x") * 4 + lax.axis_index("y") * 2 + lax.axis_index("c")


def _dev_coords(i):
    return (i // 4, (i // 2) % 2, i % 2)


def _sigmoid(x):
    return 1.0 / (1.0 + jnp.exp(-x))


def _erf(x):
    ax = jnp.abs(x)
    t = 1.0 / (1.0 + 0.3275911 * ax)
    poly = ((((1.061405429 * t - 1.453152027) * t + 1.421413741) * t - 0.284496736) * t + 0.254829592) * t
    y = 1.0 - poly * jnp.exp(-ax * ax)
    return jnp.where(x < 0, -y, y)


def _gelu_and_grad(x):
    cdf = 0.5 * (1.0 + _erf(x * (2.0 ** -0.5)))
    pdf = jnp.exp(-0.5 * x * x) * (1.0 / math.sqrt(2.0 * math.pi))
    return x * cdf, cdf + x * pdf


def _rms(x):
    return lax.rsqrt(jnp.mean(x * x, axis=-1, keepdims=True) + EPS)


def _colsum(x):
    return jnp.sum(x, axis=0, keepdims=True)


def _tile(n, want):
    if n <= want:
        return n
    t = (want // LANE) * LANE
    while n % t:
        t -= LANE
    assert t > 0, (n, want)
    return t


def _all_gather_small(name, payload):
    rows = payload.shape[0]

    def body(p_ref, out_ref, send_sems, recv_sems, local_sem):
        me = _dev_index()
        mine = pltpu.make_async_copy(p_ref, out_ref.at[me], local_sem)
        mine.start()
        sends = []
        for r in range(1, N_DEV):
            peer = (me + r) % N_DEV
            cp = pltpu.make_async_remote_copy(
                src_ref=p_ref, dst_ref=out_ref.at[me], send_sem=send_sems.at[r - 1], recv_sem=recv_sems.at[r - 1],
                device_id=_dev_coords(peer), device_id_type=MESH)
            cp.start()
            sends.append(cp)
        for r in range(1, N_DEV):
            src = (me + N_DEV - r) % N_DEV
            pltpu.make_async_remote_copy(
                src_ref=p_ref, dst_ref=out_ref.at[src], send_sem=send_sems.at[r - 1], recv_sem=recv_sems.at[r - 1],
                device_id=_dev_coords(src), device_id_type=MESH).wait_recv()
        for cp in sends:
            cp.wait_send()
        mine.wait()

    return pl.pallas_call(
        body, name=name,
        out_shape=jax.ShapeDtypeStruct((N_DEV, rows, LANE), F32),
        in_specs=[pl.BlockSpec(memory_space=pltpu.VMEM)],
        out_specs=pl.BlockSpec(memory_space=pltpu.VMEM),
        scratch_shapes=[pltpu.SemaphoreType.DMA((N_DEV - 1,)), pltpu.SemaphoreType.DMA((N_DEV - 1,)),
                        pltpu.SemaphoreType.DMA],
        compiler_params=pltpu.CompilerParams(vmem_limit_bytes=VMEM_LIMIT),
    )(payload)


def _region(ref, dev, axis, n):
    start = pl.multiple_of(dev * n, LANE if axis == 1 else 16)
    return ref.at[:, pl.ds(start, n)] if axis == 1 else ref.at[pl.ds(start, n), :]


class _Exchange:
    def __init__(self, arrays, out_shapes, sems, start, finish):
        self.arrays, self.out_shapes, self.sems, self.start, self.finish = arrays, out_shapes, sems, start, finish


def _gather_plan(shards, axes):
    n_w = len(shards)
    fulls = []
    for s, ax in zip(shards, axes):
        shp = (s.shape[0], s.shape[1] * N_DEV) if ax == 1 else (s.shape[0] * N_DEV, s.shape[1])
        fulls.append(jax.ShapeDtypeStruct(shp, BF16))
    widths = [s.shape[ax] for s, ax in zip(shards, axes)]

    def places():
        x, y, c = lax.axis_index("x"), lax.axis_index("y"), lax.axis_index("c")
        chips = [(1 - x, y), (x, 1 - y), (1 - x, 1 - y)]
        return (x, y, c), (x, y, 1 - c), chips

    def index(p):
        return p[0] * 4 + p[1] * 2 + p[2]

    def copy(w, k, s_refs, f_refs, sems, block, to, from_shard):
        send_sems, recv_sems, _ = sems
        dst = _region(f_refs[w], index(block), axes[w], widths[w])
        return pltpu.make_async_remote_copy(
            src_ref=s_refs[w] if from_shard else dst, dst_ref=dst,
            send_sem=send_sems.at[w, k], recv_sem=recv_sems.at[w, k], device_id=to, device_id_type=MESH)

    def local(w, s_refs, f_refs, sems, me):
        return pltpu.make_async_copy(s_refs[w], _region(f_refs[w], index(me), axes[w], widths[w]), sems[2].at[w])

    def start(s_refs, f_refs, sems):
        me, sib, chips = places()
        for w in range(n_w):
            local(w, s_refs, f_refs, sems, me).start()
            copy(w, 0, s_refs, f_refs, sems, me, sib, True).start()
            for j, chip in enumerate(chips):
                copy(w, 1 + j, s_refs, f_refs, sems, me, (*chip, me[2]), True).start()

    def finish(s_refs, f_refs, sems):
        me, sib, chips = places()
        for w in range(n_w):
            for j, chip in enumerate(chips):
                copy(w, 1 + j, s_refs, f_refs, sems, (*chip, me[2]), me, True).wait_recv()
                copy(w, 4 + j, s_refs, f_refs, sems, (*chip, me[2]), sib, False).start()
        for w in range(n_w):
            copy(w, 0, s_refs, f_refs, sems, sib, me, True).wait_recv()
            for j, chip in enumerate(chips):
                copy(w, 4 + j, s_refs, f_refs, sems, (*chip, sib[2]), me, False).wait_recv()
        for w in range(n_w):
            for k in range(N_DEV - 1):
                copy(w, k, s_refs, f_refs, sems, me, sib, True).wait_send()
            local(w, s_refs, f_refs, sems, me).wait()

    sems = [pltpu.SemaphoreType.DMA((n_w, N_DEV - 1)), pltpu.SemaphoreType.DMA((n_w, N_DEV - 1)),
            pltpu.SemaphoreType.DMA((n_w,))]
    return _Exchange(list(shards), fulls, sems, start, finish)


def _scatter_plan(grads, axes):
    n_w = len(grads)
    lands = []
    for g, ax in zip(grads, axes):
        shp = (g.shape[0], g.shape[1] // N_DEV) if ax == 1 else (g.shape[0] // N_DEV, g.shape[1])
        lands.append(jax.ShapeDtypeStruct((N_DEV - 1,) + shp, BF16))
    widths = [ld.shape[1 + ax] for ld, ax in zip(lands, axes)]

    def copy(w, r, g_refs, l_refs, sems, block, to):
        return pltpu.make_async_remote_copy(
            src_ref=_region(g_refs[w], block, axes[w], widths[w]), dst_ref=l_refs[w].at[r - 1],
            send_sem=sems[0].at[w * (N_DEV - 1) + r - 1], recv_sem=sems[1].at[w * (N_DEV - 1) + r - 1],
            device_id=_dev_coords(to), device_id_type=MESH)

    def start(g_refs, l_refs, sems):
        me = _dev_index()
        for w in range(n_w):
            for r in range(1, N_DEV):
                owner = (me + r) % N_DEV
                copy(w, r, g_refs, l_refs, sems, owner, owner).start()

    def finish(g_refs, l_refs, sems):
        me = _dev_index()
        for w in range(n_w):
            for r in range(1, N_DEV):
                copy(w, r, g_refs, l_refs, sems, me, (me + N_DEV - r) % N_DEV).wait_recv()
        for w in range(n_w):
            for r in range(1, N_DEV):
                copy(w, r, g_refs, l_refs, sems, me, (me + r) % N_DEV).wait_send()

    sems = [pltpu.SemaphoreType.DMA((n_w * (N_DEV - 1),)), pltpu.SemaphoreType.DMA((n_w * (N_DEV - 1),))]
    return _Exchange(list(grads), lands, sems, start, finish)


def _run_exchange(name, plan):
    n_in, n_out = len(plan.arrays), len(plan.out_shapes)

    def body(*refs):
        ins, outs, sems = refs[:n_in], refs[n_in:n_in + n_out], refs[n_in + n_out:]
        plan.start(ins, outs, sems)
        plan.finish(ins, outs, sems)

    any_spec = pl.BlockSpec(memory_space=pl.ANY)
    return pl.pallas_call(
        body, name=name, out_shape=plan.out_shapes,
        in_specs=[any_spec] * n_in, out_specs=[any_spec] * n_out, scratch_shapes=plan.sems,
    )(*plan.arrays)


_NO_EXCHANGE = _Exchange([], [], [], lambda i, o, s: None, lambda i, o, s: None)


def _direct_gather_plan(fulls, axes):
    n_w = len(fulls)
    widths = [f.shape[ax] // N_DEV for f, ax in zip(fulls, axes)]
    fulls = [jax.ShapeDtypeStruct(f.shape, f.dtype) for f in fulls]

    def copy(w, r, s_refs, f_refs, sems, block, to):
        part = _region(f_refs[w], block, axes[w], widths[w])
        return pltpu.make_async_remote_copy(
            src_ref=part, dst_ref=part,
            send_sem=sems[0].at[w * (N_DEV - 1) + r - 1], recv_sem=sems[1].at[w * (N_DEV - 1) + r - 1],
            device_id=_dev_coords(to), device_id_type=MESH)

    def start(s_refs, f_refs, sems):
        me = _dev_index()
        for w in range(n_w):
            for r in range(1, N_DEV):
                copy(w, r, s_refs, f_refs, sems, me, (me + r) % N_DEV).start()

    def finish(s_refs, f_refs, sems):
        me = _dev_index()
        for w in range(n_w):
            for r in range(1, N_DEV):
                src = (me + N_DEV - r) % N_DEV
                copy(w, r, s_refs, f_refs, sems, src, src).wait_recv()
        for w in range(n_w):
            for r in range(1, N_DEV):
                copy(w, r, s_refs, f_refs, sems, me, (me + r) % N_DEV).wait_send()

    sems = [pltpu.SemaphoreType.DMA((n_w * (N_DEV - 1),)), pltpu.SemaphoreType.DMA((n_w * (N_DEV - 1),))]
    return _Exchange([], fulls, sems, start, finish)


_HBM = pl.BlockSpec(memory_space=pltpu.HBM)
_SEM = pl.BlockSpec(memory_space=pltpu.SEMAPHORE)
_EFFECT = pltpu.SideEffectType.DATAFLOW_SIDE_EFFECTING


def _split_start(name, plan, landing=None):
    n_in, n_out, n_sem = len(plan.arrays), len(plan.out_shapes), len(plan.sems)

    def body(*refs):
        ins, lands = refs[:n_in], refs[n_in:n_in + n_out]
        sems = refs[n_in + n_out:n_in + n_out + n_sem]
        token = refs[-1]
        plan.start(ins, lands, sems)
        token[...] = jnp.zeros_like(token)

    hbm = lambda a: pltpu.HBM(a.shape, a.dtype)
    results = pl.pallas_call(
        body, name=name,
        out_shape=tuple(plan.sems) + tuple(hbm(a) for a in plan.arrays) + tuple(hbm(a) for a in plan.out_shapes)
        + (jax.ShapeDtypeStruct((8, LANE), F32),),
        in_specs=(_HBM,) * (n_in + n_out),
        out_specs=(_SEM,) * n_sem + (_HBM,) * (n_in + n_out) + (pl.BlockSpec(memory_space=pltpu.VMEM),),
        input_output_aliases={i: n_sem + i for i in range(n_in + n_out)},
        compiler_params=pltpu.CompilerParams(has_side_effects=_EFFECT),
    )(*[pltpu.with_memory_space_constraint(a, pltpu.HBM) for a in plan.arrays],
      *[pltpu.with_memory_space_constraint(a, pltpu.HBM)
        for a in (landing if landing is not None else [lax.empty(a.shape, a.dtype) for a in plan.out_shapes])])
    return results[:n_sem], results[n_sem:n_sem + n_in + n_out], results[-1]


def _split_wait(name, plan, sems, thru, after):
    n_in, n_out, n_sem = len(plan.arrays), len(plan.out_shapes), len(plan.sems)

    def body(*refs):
        ins, lands = refs[:n_in], refs[n_in:n_in + n_out]
        sem_refs = refs[n_in + n_out:n_in + n_out + n_sem]
        plan.finish(ins, lands, sem_refs)

    hbm = lambda a: pltpu.HBM(a.shape, a.dtype)
    results = pl.pallas_call(
        body, name=name,
        out_shape=tuple(hbm(a) for a in plan.arrays) + tuple(hbm(a) for a in plan.out_shapes),
        in_specs=(_HBM,) * (n_in + n_out) + (_SEM,) * n_sem + (pl.BlockSpec(memory_space=pl.ANY),),
        out_specs=(_HBM,) * (n_in + n_out),
        input_output_aliases={i: i for i in range(n_in + n_out)},
        compiler_params=pltpu.CompilerParams(has_side_effects=_EFFECT),
    )(*thru, *sems, after)
    return results[:n_in], results[n_in:]


def _cast_into_full(name, me, w, axis):
    r, c = w.shape
    tr = _tile(r, 256)
    if axis == 1:
        shape, place = (r, c * N_DEV), pl.BlockSpec((tr, c), lambda i, me_ref: (i, me_ref[0]))
    else:
        shape, place = (r * N_DEV, c), pl.BlockSpec((tr, c), lambda i, me_ref: (me_ref[0] * (r // tr) + i, 0))

    def body(me_ref, w_ref, o_ref):
        o_ref[...] = w_ref[...].astype(BF16)

    grid_spec = pltpu.PrefetchScalarGridSpec(
        num_scalar_prefetch=1, grid=(r // tr,),
        in_specs=[pl.BlockSpec((tr, c), lambda i, me_ref: (i, 0))], out_specs=place)
    return pl.pallas_call(body, name=name, grid_spec=grid_spec, out_shape=jax.ShapeDtypeStruct(shape, BF16),
                          compiler_params=_params(1))(me, w)


def _mm(name, a, b, dims, m, n, k, tm, tn, tk, extras, outs, epilogue, row_chunk=None, exchange=None):
    ni, nj, nk = m // tm, n // tn, k // tk
    ne, no = len(extras), len(outs)
    xin = len(exchange.arrays) if exchange else 0
    xout = len(exchange.out_shapes) if exchange else 0
    if dims == _TN:
        a_spec = pl.BlockSpec((tk, tm), lambda i, j, kk: (kk, i))
    else:
        a_spec = pl.BlockSpec((tm, tk), lambda i, j, kk: (i, kk))
    if dims == _NT:
        b_spec = pl.BlockSpec((tn, tk), lambda i, j, kk: (j, kk))
    else:
        b_spec = pl.BlockSpec((tk, tn), lambda i, j, kk: (kk, j))
    chunks = [slice(None)] if row_chunk is None else [slice(r, r + row_chunk) for r in range(0, tm, row_chunk)]

    def lift(index_map):
        return lambda i, j, kk: index_map(i, j)

    def body(a_ref, b_ref, *rest):
        extra_refs, rest = rest[:ne], rest[ne:]
        xin_refs, rest = rest[:xin], rest[xin:]
        out_refs, rest = rest[:no], rest[no:]
        xout_refs, rest = rest[:xout], rest[xout:]
        i, j, kk = pl.program_id(0), pl.program_id(1), pl.program_id(2)
        if exchange:
            sem_refs = rest[1:] if nk > 1 else rest

            @pl.when((i == 0) & (j == 0) & (kk == 0))
            def _():
                exchange.start(xin_refs, xout_refs, sem_refs)

        if nk == 1:
            part = _dot(a_ref[...], b_ref[...], dims)
            for rows in chunks:
                epilogue(part[rows], i, j, extra_refs, out_refs, rows)
        else:
            acc_ref = rest[0]

            @pl.when(kk == 0)
            def _():
                acc_ref[...] = _dot(a_ref[...], b_ref[...], dims)

            @pl.when(kk > 0)
            def _():
                acc_ref[...] += _dot(a_ref[...], b_ref[...], dims)

            @pl.when(kk == nk - 1)
            def _():
                for rows in chunks:
                    epilogue(acc_ref[rows, :], i, j, extra_refs, out_refs, rows)

        if exchange:
            @pl.when((i == ni - 1) & (j == nj - 1) & (kk == nk - 1))
            def _():
                exchange.finish(xin_refs, xout_refs, sem_refs)

    any_spec = pl.BlockSpec(memory_space=pl.ANY)
    once = dict(pipeline_mode=pl.Buffered(1)) if (row_chunk is not None and nk > 1) else {}
    results = pl.pallas_call(
        body, name=name,
        grid=(ni, nj, nk),
        in_specs=[a_spec, b_spec] + [pl.BlockSpec(bs, lift(im), **once) for _, bs, im in extras] + [any_spec] * xin,
        out_specs=[pl.BlockSpec(bs, lift(im), **once) for _, bs, im in outs] + [any_spec] * xout,
        out_shape=[sd for sd, _, _ in outs] + (list(exchange.out_shapes) if exchange else []),
        scratch_shapes=([pltpu.VMEM((tm, tn), F32)] if nk > 1 else []) + (list(exchange.sems) if exchange else []),
        compiler_params=_params(3),
    )(a, b, *[arr for arr, _, _ in extras], *(exchange.arrays if exchange else []))
    return (results[:no], results[no:]) if exchange else results


def _after(token):
    return [(token, (8, LANE), lambda i, j: (0, 0))]


def _grad_w(name, a, dc, token=None, tm=512, tn=1024):
    t, m = a.shape
    n = dc.shape[1]
    tm, tn = _tile(m, tm), _tile(n, tn)

    def epilogue(acc, i, j, extra_refs, out_refs, rows):
        out_refs[0][...] = acc
        out_refs[1][...] = acc.astype(BF16)

    blk = ((tm, tn), lambda i, j: (i, j))
    return _mm(name, a, dc, _TN, m, n, t, tm, tn, t, _after(token) if token is not None else [],
               [(jax.ShapeDtypeStruct((m, n), F32),) + blk, (jax.ShapeDtypeStruct((m, n), BF16),) + blk], epilogue)


def _cast_bf16(name, w):
    r, c = w.shape
    tr = _tile(r, 256)
    return pl.pallas_call(
        lambda w_ref, o_ref: o_ref.__setitem__(Ellipsis, w_ref[...].astype(BF16)), name=name,
        grid=(r // tr,), in_specs=[pl.BlockSpec((tr, c), lambda i: (i, 0))],
        out_specs=pl.BlockSpec((tr, c), lambda i: (i, 0)), out_shape=jax.ShapeDtypeStruct((r, c), BF16),
        compiler_params=_params(1),
    )(w)


def _prep_small(c_row, lb_logits):
    d = c_row.shape[1]
    rows = d // LANE

    def body(c_ref, l_ref, o_ref):
        cv = c_ref[...]
        o_ref[0:rows, :] = cv * _sigmoid(cv)
        lbs = [_sigmoid(l_ref[dr][0:1, :] - l_ref[dr][1:2, :]) for dr in range(2)]
        o_ref[rows:rows + 8, :] = jnp.concatenate(lbs + [jnp.zeros((6, LANE), F32)], axis=0)

    return pl.pallas_call(
        body, name="prep_small", out_shape=jax.ShapeDtypeStruct((rows + 8, LANE), F32),
    )(c_row.reshape(rows, LANE), lb_logits)


def _mod_shard(sc_all, w_ada_shard, b_shard):
    d, n = w_ada_shard.shape
    tn = _tile(n, 512)

    def body(s_ref, w_ref, b_ref, o_ref):
        o_ref[...] = _dot(s_ref[...], w_ref[...], precision=HIGHEST) + b_ref[...]

    return pl.pallas_call(
        body, name="mod_shard", grid=(n // tn,),
        in_specs=[pl.BlockSpec((N_DEV, d), lambda j: (0, 0)), pl.BlockSpec((d, tn), lambda j: (0, j)),
                  pl.BlockSpec((1, tn), lambda j: (0, j))],
        out_specs=pl.BlockSpec((N_DEV, tn), lambda j: (0, j)),
        out_shape=jax.ShapeDtypeStruct((N_DEV, n), F32), compiler_params=_params(1),
    )(sc_all, w_ada_shard, b_shard)


def _norm_mod(x, gain, shift, scale):
    t, d = x.shape
    tm = _tile(t, 512)

    def body(x_ref, g_ref, sh_ref, sc_ref, o_ref):
        xv = x_ref[...]
        o_ref[...] = ((xv * _rms(xv) * g_ref[...]) * (1.0 + sc_ref[...]) + sh_ref[...]).astype(BF16)

    vec = pl.BlockSpec((1, d), lambda i: (0, 0))
    return pl.pallas_call(
        body, name="norm_mod", grid=(t // tm,),
        in_specs=[pl.BlockSpec((tm, d), lambda i: (i, 0)), vec, vec, vec],
        out_specs=pl.BlockSpec((tm, d), lambda i: (i, 0)), out_shape=jax.ShapeDtypeStruct((t, d), BF16),
        compiler_params=_params(1),
    )(x, gain, shift, scale)


def _chunk_masks():
    row = lax.broadcasted_iota(jnp.int32, (HEAD, HEAD), 0)
    col = lax.broadcasted_iota(jnp.int32, (HEAD, HEAD), 1)
    same = (row // A_CHUNK) == (col // A_CHUNK)
    return same & (col <= row), same & (col >= row)


def _ones(mask):
    return jnp.where(mask, 1.0, 0.0).astype(BF16)


def _dot_split(ones_bf16, x):
    hi = x.astype(BF16)
    lo = (x - hi.astype(F32)).astype(BF16)
    return _dot(ones_bf16, hi) + _dot(ones_bf16, lo)


def _hgrn_block(direction, f, lb, cum2):
    sf = _sigmoid(f)
    big_f = lb + (1.0 - lb) * sf
    k = (1.0 - lb) * (1.0 - sf)
    lf = jnp.log(big_f)
    both = _dot_split(cum2, lf)
    cf, cr = both[:HEAD], both[HEAD:]
    b, rest = (cf, cr - lf) if direction == 0 else (cr, cf - lf)
    return k, sf, big_f, jnp.exp(b), jnp.exp(-b), jnp.exp(rest)


def _hgrn_fwd(proj, lb, g_norm, width, exchange):
    t = proj.shape[0]
    heads = width // HEAD
    nb, nc = t // HEAD, t // A_CHUNK
    ua = 2 if nb % 2 == 0 else 1
    ub = 8 if nc % 8 == 0 else 4
    q_scale = HEAD ** -0.5
    xin, xout = len(exchange.arrays), len(exchange.out_shapes)

    def body(q_ref, ffw_ref, fbw_ref, v_ref, og_ref, lb_ref, g_ref, *rest):
        xin_refs, rest = rest[:xin], rest[xin:]
        outa_ref, osum_ref = rest[:2]
        xout_refs, rest = rest[2:2 + xout], rest[2 + xout:]
        qd_s, ke_s, dc_s, o_s = rest[:4]
        sem_refs = rest[4:]
        h = pl.program_id(0)

        @pl.when(h == 0)
        def _():
            exchange.start(xin_refs, xout_refs, sem_refs)

        tril, triu = _chunk_masks()
        cum2 = jnp.concatenate([_ones(tril), _ones(triu)], axis=0)
        f_refs = (ffw_ref, fbw_ref)
        lbs = (lb_ref[0:1, :], lb_ref[1:2, :])

        def phase_a(it, carry):
            loaded = []
            for u in range(ua):
                rows = pl.ds(pl.multiple_of((it * ua + u) * HEAD, HEAD), HEAD)
                loaded.append((rows, q_ref[rows, :], v_ref[rows, :], ffw_ref[rows, :], fbw_ref[rows, :]))
            chains = [(d, rows, qv * q_scale, vv.astype(BF16), fv)
                      for rows, qv, vv, f0, f1 in loaded for d, fv in ((0, f0), (1, f1))]
            blocks = [_hgrn_block(d, fv, lbs[d], cum2) for d, _, _, _, fv in chains]
            scaled = [(qv * eb, k * enb, k * erest, eb * erest)
                      for (_, _, qv, _, _), (k, _, _, eb, enb, erest) in zip(chains, blocks)]
            atts = [jnp.where(tril if d == 0 else triu, _bdot(qd, kd, _NT), 0.0)
                    for (d, _, _, _, _), (qd, kd, _, _) in zip(chains, scaled)]
            intras = [_bdot(att, vv) for att, (_, _, _, vv, _) in zip(atts, chains)]
            results = [(d, rows, o_intra, qd.astype(BF16), ke.astype(BF16), decay)
                       for (d, rows, _, _, _), (qd, _, ke, decay), o_intra in zip(chains, scaled, intras)]
            for d, rows, o_intra, qd16, ke16, decay in results:
                o_s[d, rows, :] = o_intra
                qd_s[d, rows, :] = qd16
                ke_s[d, rows, :] = ke16
                dc_s[d, rows, :] = decay
            return carry

        lax.fori_loop(0, nb // ua, phase_a, 0)

        def phase_b(it, states):
            loaded = []
            for u in range(ub):
                n = it * ub + u
                for d in range(2):
                    c = n if d == 0 else nc - 1 - n
                    start = pl.multiple_of(c * A_CHUNK, A_CHUNK)
                    rows = pl.ds(start, A_CHUNK)
                    loaded.append((d, rows, qd_s[d, rows, :], ke_s[d, rows, :], v_ref[rows, :],
                                   dc_s[d, pl.ds(start, 1), :], o_s[d, rows, :]))
            increments = [_dot(vv.astype(BF16), ke16, _TN) for _, _, _, ke16, vv, _, _ in loaded]
            states = list(states)
            befores = []
            for (d, _, _, _, _, decay, _), inc in zip(loaded, increments):
                befores.append(states[d].astype(BF16))
                states[d] = states[d] * decay + inc
            inters = [_dot(qd16, before, _NT) for (_, _, qd16, _, _, _, _), before in zip(loaded, befores)]
            for (d, rows, _, _, _, _, o_intra), o_inter in zip(loaded, inters):
                o_s[d, rows, :] = o_intra + o_inter
            return tuple(states)

        zero_state = jnp.zeros((HEAD, HEAD), F32)
        lax.fori_loop(0, nc // ub, phase_b, (zero_state, zero_state))

        def phase_c(i, carry):
            rows = pl.ds(pl.multiple_of(i * HEAD, HEAD), HEAD)
            o = o_s[0, rows, :] + o_s[1, rows, :]
            osum_ref[rows, :] = o
            og = og_ref[rows, :]
            outa_ref[rows, :] = (o * _rms(o) * g_ref[...] * (og * _sigmoid(og))).astype(BF16)
            return carry

        lax.fori_loop(0, nb, phase_c, 0)

        @pl.when(h == heads - 1)
        def _():
            exchange.finish(xin_refs, xout_refs, sem_refs)

    def col(p):
        return pl.BlockSpec((t, HEAD), lambda h: (0, p * heads + h))

    any_spec = pl.BlockSpec(memory_space=pl.ANY)
    results = pl.pallas_call(
        body, name="hgrn_fwd", grid=(heads,),
        in_specs=[col(0), col(1), col(2), col(3), col(4),
                  pl.BlockSpec((2, HEAD), lambda h: (0, h)), pl.BlockSpec((1, HEAD), lambda h: (0, 0))] + [any_spec] * xin,
        out_specs=[pl.BlockSpec((t, HEAD), lambda h: (0, h)), pl.BlockSpec((t, HEAD), lambda h: (0, h))] + [any_spec] * xout,
        out_shape=[jax.ShapeDtypeStruct((t, width), BF16), jax.ShapeDtypeStruct((t, width), F32)] + list(exchange.out_shapes),
        scratch_shapes=[pltpu.VMEM((2, t, HEAD), BF16), pltpu.VMEM((2, t, HEAD), BF16), pltpu.VMEM((2, t, HEAD), F32),
                        pltpu.VMEM((2, t, HEAD), F32)] + list(exchange.sems),
        compiler_params=_params(1),
    )(proj, proj, proj, proj, proj, lb, g_norm, *exchange.arrays)
    return results[0], results[1], results[2:]


def _sgu_core(u_pre, v_pre, g_v, ws_ref, bst):
    u, du = _gelu_and_grad(u_pre)
    v, dv = _gelu_and_grad(v_pre)
    mu = jnp.mean(v, axis=-1, keepdims=True)
    dlt = v - mu
    rstd = lax.rsqrt(jnp.mean(dlt * dlt, axis=-1, keepdims=True) + EPS)
    vhat = dlt * rstd
    vn = vhat * g_v
    groups = vn.shape[1] // HEAD
    cols = []
    for g in range(groups):
        vm_g = _bdot(ws_ref[g], vn[:, g * HEAD:(g + 1) * HEAD]) + bst[:, g:g + 1]
        cols.append(vm_g)
    return u, du, dv, vhat, rstd, vn, jnp.concatenate(cols, axis=1)


def _sgu_fwd(proj, g_v, w_s, bst, width, z_block):
    t = proj.shape[0]

    def body(u_ref, v_ref, g_ref, ws_ref, bst_ref, o_ref):
        u, _, _, _, _, _, vm = _sgu_core(u_ref[...], v_ref[...], g_ref[...], ws_ref, bst_ref[...])
        o_ref[...] = (u * vm).astype(BF16)

    groups = width // HEAD
    return pl.pallas_call(
        body, name="sgu_fwd", grid=(t // HEAD,),
        in_specs=[pl.BlockSpec((HEAD, width), lambda i: (i, z_block)), pl.BlockSpec((HEAD, width), lambda i: (i, z_block + 1)),
                  pl.BlockSpec((1, width), lambda i: (0, 0)), pl.BlockSpec((groups, HEAD, HEAD), lambda i: (0, 0, 0)),
                  pl.BlockSpec((HEAD, groups), lambda i: (0, 0))],
        out_specs=pl.BlockSpec((HEAD, width), lambda i: (i, 0)),
        out_shape=jax.ShapeDtypeStruct((t, width), BF16), compiler_params=_params(1),
    )(proj, proj, g_v, w_s, bst)


def _sgu_bwd(proj, dout_b, dproj, g_v, w_s, w_st, bst, width, z_block):
    t = proj.shape[0]
    groups = width // HEAD
    nblk = t // HEAD

    def body(u_ref, v_ref, do_ref, g_ref, ws_ref, wst_ref, bst_ref, dproj_hbm,
             dz_ref, dg_ref, dws_ref, dbst_ref, res_s):
        i, p = pl.program_id(0), pl.program_id(1)

        @pl.when((i == 0) & (p == 0))
        def _():
            dg_ref[...] = jnp.zeros_like(dg_ref)
            dws_ref[...] = jnp.zeros_like(dws_ref)
            dbst_ref[...] = jnp.zeros_like(dbst_ref)

        @pl.when(p == 0)
        def _():
            g_v = g_ref[...]
            u, du, dv, vhat, rstd, vn, vm = _sgu_core(u_ref[...], v_ref[...], g_v, ws_ref, bst_ref[...])
            dout = do_ref[...].astype(F32)
            res_s[0] = (dout * vm * du).astype(BF16)
            dvm = dout * u
            dvn_cols = []
            for g in range(groups):
                sl = slice(g * HEAD, (g + 1) * HEAD)
                dvm_g = dvm[:, sl]
                dbst_ref[:, g:g + 1] += jnp.sum(dvm_g, axis=1, keepdims=True)
                dws_ref[g] += _bdot(dvm_g, vn[:, sl], _NT)
                dvn_cols.append(_bdot(wst_ref[g], dvm_g))
            dvn = jnp.concatenate(dvn_cols, axis=1)
            dg_ref[...] += _colsum(dvn * vhat)
            dvh = dvn * g_v
            dvg = rstd * (dvh - jnp.mean(dvh, axis=-1, keepdims=True)
                          - vhat * jnp.mean(dvh * vhat, axis=-1, keepdims=True))
            res_s[1] = (dvg * dv).astype(BF16)

        dz_ref[...] = res_s[p]

    n_in = dproj.shape[1]
    return pl.pallas_call(
        body, name="sgu_bwd", grid=(nblk, 2),
        in_specs=[pl.BlockSpec((HEAD, width), lambda i, p: (i, z_block)),
                  pl.BlockSpec((HEAD, width), lambda i, p: (i, z_block + 1)),
                  pl.BlockSpec((HEAD, width), lambda i, p: (i, 0)),
                  pl.BlockSpec((1, width), lambda i, p: (0, 0)),
                  pl.BlockSpec((groups, HEAD, HEAD), lambda i, p: (0, 0, 0)),
                  pl.BlockSpec((groups, HEAD, HEAD), lambda i, p: (0, 0, 0)),
                  pl.BlockSpec((HEAD, groups), lambda i, p: (0, 0)),
                  pl.BlockSpec(memory_space=pl.ANY)],
        out_specs=[pl.BlockSpec((HEAD, width), lambda i, p: (i, z_block + p)),
                   pl.BlockSpec((1, width), lambda i, p: (0, 0)),
                   pl.BlockSpec((groups, HEAD, HEAD), lambda i, p: (0, 0, 0)),
                   pl.BlockSpec((HEAD, groups), lambda i, p: (0, 0))],
        out_shape=[jax.ShapeDtypeStruct((t, n_in), BF16), jax.ShapeDtypeStruct((1, width), F32),
                   jax.ShapeDtypeStruct((groups, HEAD, HEAD), F32), jax.ShapeDtypeStruct((HEAD, groups), F32)],
        scratch_shapes=[pltpu.VMEM((2, HEAD, width), BF16)],
        input_output_aliases={7: 0},
        compiler_params=_params(2),
    )(proj, proj, dout_b, g_v, w_s, w_st, bst, dproj)


def _hgrn_bwd(proj, osum, dout_a, dproj, lb, g_norm, width, exchange):
    t = proj.shape[0]
    heads = width // HEAD
    nb = t // HEAD
    cpb = HEAD // A_CHUNK
    ubk = 2 if nb % 2 == 0 else 1
    q_scale = HEAD ** -0.5
    xin, xout = len(exchange.arrays), len(exchange.out_shapes)

    def body(q_ref, ffw_ref, fbw_ref, v_ref, og_ref, osum_ref, douta_ref, lb_ref, g_ref, dproj_hbm, *rest):
        xin_refs, rest = rest[:xin], rest[xin:]
        out_ref, dgh_ref, dlb_ref = rest[:3]
        xout_refs, rest = rest[3:3 + xout], rest[3 + xout:]
        do_s, dq_s, dv_s, res_s, ck_s = rest[:5]
        sem_refs = rest[5:]
        h, p = pl.program_id(0), pl.program_id(1)
        f_refs = (ffw_ref, fbw_ref)

        @pl.when((h == 0) & (p == 0))
        def _():
            exchange.start(xin_refs, xout_refs, sem_refs)

        @pl.when(p == 0)
        def _():
            tril, triu = _chunk_masks()
            cum2 = jnp.concatenate([_ones(tril), _ones(triu)], axis=0)
            g_row = g_ref[...]

            def pass_norm(i, dgh):
                rows = pl.ds(pl.multiple_of(i * HEAD, HEAD), HEAD)
                o = osum_ref[rows, :]
                r = _rms(o)
                oh = o * r
                og = og_ref[rows, :]
                sg = _sigmoid(og)
                dout = douta_ref[rows, :].astype(F32)
                don = dout * (og * sg)
                res_s[4, rows, :] = (dout * (oh * g_row) * (sg * (1.0 + og * (1.0 - sg)))).astype(BF16)
                doh = don * g_row
                do_s[rows, :] = r * (doh - oh * jnp.mean(doh * oh, axis=-1, keepdims=True))
                return dgh + _colsum(don * oh)

            dgh_ref[...] = lax.fori_loop(0, nb, pass_norm, jnp.zeros((1, HEAD), F32))

            lbs = (lb_ref[0:1, :], lb_ref[1:2, :])
            zero_state = jnp.zeros((HEAD, HEAD), F32)

            def chunk_order(d):
                return list(range(cpb)) if d == 0 else list(range(cpb - 1, -1, -1))

            def chunk(x, j):
                return x[j * A_CHUNK:(j + 1) * A_CHUNK, :]

            def decay_row(e_big, j):
                return e_big[j * A_CHUNK:j * A_CHUNK + 1, :]

            def cat(parts):
                return jnp.concatenate([parts[j] for j in range(cpb)], axis=0)

            def block_states(d, start, incs, e_big):
                befores, st = {}, start
                for j in chunk_order(d):
                    befores[j] = st
                    st = st * decay_row(e_big, j) + incs[j]
                return befores, st

            def pass_states(it, states):
                loaded = []
                for u in range(ubk):
                    for d in range(2):
                        blk = it * ubk + u if d == 0 else nb - 1 - (it * ubk + u)
                        rows = pl.ds(pl.multiple_of(blk * HEAD, HEAD), HEAD)
                        loaded.append((d, blk, f_refs[d][rows, :], v_ref[rows, :]))
                blocks = [_hgrn_block(d, fv, lbs[d], cum2) for d, _, fv, _ in loaded]
                incs = [{j: _bdot(chunk(vv, j), chunk(k * erest, j), _TN) for j in range(cpb)}
                        for (_, _, _, vv), (k, _, _, _, _, erest) in zip(loaded, blocks)]
                states, starts = list(states), []
                for (d, _, _, _), (_, _, _, eb, _, erest), inc in zip(loaded, blocks, incs):
                    starts.append(states[d])
                    states[d] = block_states(d, states[d], inc, eb * erest)[1]
                for (d, blk, _, _), start in zip(loaded, starts):
                    ck_s[d, blk] = start
                return tuple(states)

            lax.fori_loop(0, nb // ubk, pass_states, (zero_state, zero_state))

            def pass_back(it, carry):
                gts, dlb = [carry[0], carry[1]], carry[2]
                loaded = []
                for u, d in ((u, d) for u in range(ubk) for d in range(2)):
                    blk = nb - 1 - (it * ubk + u) if d == 0 else it * ubk + u
                    rows = pl.ds(pl.multiple_of(blk * HEAD, HEAD), HEAD)
                    loaded.append((d, rows, f_refs[d][rows, :], q_ref[rows, :], v_ref[rows, :], do_s[rows, :], ck_s[d, blk]))
                blocks = [_hgrn_block(d, fv, lbs[d], cum2) for d, _, fv, _, _, _, _ in loaded]
                scaled = []
                for (_, _, _, qv, _, _, _), (k, _, _, eb, enb, erest) in zip(loaded, blocks):
                    qh = qv * q_scale
                    scaled.append((qh, qh * eb, k * enb, k * erest, eb * erest))
                masks = [tril if d == 0 else triu for d, *_ in loaded]
                atts = [jnp.where(m, _bdot(qd, kd, _NT), 0.0) for m, (_, qd, kd, _, _) in zip(masks, scaled)]
                datts = [jnp.where(m, _bdot(do, vv, _NT), 0.0) for m, (_, _, _, _, vv, do, _) in zip(masks, loaded)]
                dvs = [_bdot(att, do, _TN) for att, (_, _, _, _, _, do, _) in zip(atts, loaded)]
                dqds = [_bdot(datt, kd) for datt, (_, _, kd, _, _) in zip(datts, scaled)]
                dkds = [_bdot(datt, qd, _TN) for datt, (_, qd, _, _, _) in zip(datts, scaled)]
                s_incs = [{j: _bdot(chunk(vv, j), chunk(ke, j), _TN) for j in range(cpb)}
                          for (_, _, _, _, vv, _, _), (_, _, _, ke, _) in zip(loaded, scaled)]
                g_incs = [{j: _bdot(chunk(do, j), chunk(qd, j), _TN) for j in range(cpb)}
                          for (_, _, _, _, _, do, _), (_, qd, _, _, _) in zip(loaded, scaled)]
                befores, afters, g_at = [], [], []
                for (d, _, _, _, _, _, ck), (_, _, _, _, e_big), s_inc, g_inc in zip(loaded, scaled, s_incs, g_incs):
                    order = chunk_order(d)
                    before, after = block_states(d, ck, s_inc, e_big)
                    befores.append(before)
                    afters.append({j: (before[order[n + 1]] if n + 1 < cpb else after) for n, j in enumerate(order)})
                    at, gt = {}, gts[d]
                    for j in reversed(order):
                        at[j] = gt
                        gt = gt * decay_row(e_big, j) + g_inc[j]
                    gts[d] = gt
                    g_at.append(at)
                dqd_i = [{j: _bdot(chunk(do, j), before[j]) for j in range(cpb)}
                         for (_, _, _, _, _, do, _), before in zip(loaded, befores)]
                dv_i = [{j: _bdot(chunk(ke, j), at[j], _NT) for j in range(cpb)}
                        for (_, _, _, ke, _), at in zip(scaled, g_at)]
                dke = [{j: _bdot(chunk(vv, j), at[j]) for j in range(cpb)}
                       for (_, _, _, _, vv, _, _), at in zip(loaded, g_at)]
                results, new = [], []
                for n, ((d, rows, _, _, _, _, _), (k, sf, big_f, eb, enb, erest), (qh, _, _, _, _)) in enumerate(
                        zip(loaded, blocks, scaled)):
                    dqh = (dqds[n] + cat(dqd_i[n])) * eb
                    dk = dkds[n] * enb + cat(dke[n]) * erest
                    carry_rows = {j: jnp.broadcast_to(_colsum(g_at[n][j] * afters[n][j]), (A_CHUNK, HEAD))
                                  for j in range(cpb)}
                    dlf = _dot_split(_ones(triu if d == 0 else tril), qh * dqh - k * dk) + cat(carry_rows)
                    common = dlf / big_f - dk
                    results.append((d, rows, (k * sf * common).astype(BF16), dqh.astype(BF16),
                                    (dvs[n] + cat(dv_i[n])).astype(BF16)))
                    new.append(_colsum((1.0 - sf) * common))
                for d, rows, df16, dq16, dv16 in results:
                    res_s[1 + d, rows, :] = df16
                    dq_s[d, rows, :] = dq16
                    dv_s[d, rows, :] = dv16
                per_dir = [sum(c for (d, *_), c in zip(loaded, new) if d == dd) for dd in range(2)]
                return gts[0], gts[1], dlb + jnp.concatenate(per_dir, axis=0)

            dlb_ref[...] = lax.fori_loop(0, nb // ubk, pass_back,
                                         (zero_state, zero_state, jnp.zeros((2, HEAD), F32)))[2]

            def pass_out(i, carry):
                rows = pl.ds(pl.multiple_of(i * HEAD, HEAD), HEAD)
                dq = dq_s[0, rows, :].astype(F32) + dq_s[1, rows, :].astype(F32)
                res_s[0, rows, :] = (dq * q_scale).astype(BF16)
                res_s[3, rows, :] = (dv_s[0, rows, :].astype(F32) + dv_s[1, rows, :].astype(F32)).astype(BF16)
                return carry

            lax.fori_loop(0, nb, pass_out, 0)

        out_ref[...] = res_s[p]

        @pl.when((h == heads - 1) & (p == 4))
        def _():
            exchange.finish(xin_refs, xout_refs, sem_refs)

    def col(pp):
        return pl.BlockSpec((t, HEAD), lambda h, p: (0, pp * heads + h))

    n_in = dproj.shape[1]
    any_spec = pl.BlockSpec(memory_space=pl.ANY)
    results = pl.pallas_call(
        body, name="hgrn_bwd", grid=(heads, 5),
        in_specs=[col(0), col(1), col(2), col(3), col(4),
                  pl.BlockSpec((t, HEAD), lambda h, p: (0, h)), pl.BlockSpec((t, HEAD), lambda h, p: (0, h)),
                  pl.BlockSpec((2, HEAD), lambda h, p: (0, h)), pl.BlockSpec((1, HEAD), lambda h, p: (0, 0)),
                  any_spec] + [any_spec] * xin,
        out_specs=[pl.BlockSpec((t, HEAD), lambda h, p: (0, p * heads + h)),
                   pl.BlockSpec((None, 1, HEAD), lambda h, p: (h, 0, 0)),
                   pl.BlockSpec((2, HEAD), lambda h, p: (0, h))] + [any_spec] * xout,
        out_shape=[jax.ShapeDtypeStruct((t, n_in), BF16), jax.ShapeDtypeStruct((heads, 1, HEAD), F32),
                   jax.ShapeDtypeStruct((2, width), F32)] + list(exchange.out_shapes),
        scratch_shapes=[pltpu.VMEM((t, HEAD), F32), pltpu.VMEM((2, t, HEAD), BF16), pltpu.VMEM((2, t, HEAD), BF16),
                        pltpu.VMEM((5, t, HEAD), BF16), pltpu.VMEM((2, nb, HEAD, HEAD), F32)] + list(exchange.sems),
        input_output_aliases={9: 0},
        compiler_params=_params(2),
    )(proj, proj, proj, proj, proj, osum, dout_a, lb, g_norm, dproj, *exchange.arrays)
    return results[0], results[1], results[2], results[3:]


def _adamw(w, g, m, v):
    m = ADAM_B1 * m + (1.0 - ADAM_B1) * g
    v = ADAM_B2 * v + (1.0 - ADAM_B2) * (g * g)
    m_hat = m / (1.0 - ADAM_B1 ** ADAM_STEP)
    v_hat = v / (1.0 - ADAM_B2 ** ADAM_STEP)
    delta = -ADAM_LR * (m_hat / (jnp.sqrt(v_hat) + ADAM_EPS) + ADAM_WD * w)
    return delta, m, v


def _adamw_big(name, me, w, m, v, g_full, landing, axis):
    r, c = w.shape
    tr = _tile(r, 128)

    def body(me_ref, w_ref, m_ref, v_ref, g_ref, l_ref, og_ref, od_ref, om_ref, ov_ref):
        g = g_ref[...]
        for s in range(N_DEV - 1):
            g = g + l_ref[s].astype(F32)
        og_ref[...] = g
        od_ref[...], om_ref[...], ov_ref[...] = _adamw(w_ref[...], g, m_ref[...], v_ref[...])

    shard = pl.BlockSpec((tr, c), lambda i, me_ref: (i, 0))
    if axis == 1:
        own = pl.BlockSpec((tr, c), lambda i, me_ref: (i, me_ref[0]))
    else:
        own = pl.BlockSpec((tr, c), lambda i, me_ref: (me_ref[0] * (r // tr) + i, 0))
    grid_spec = pltpu.PrefetchScalarGridSpec(
        num_scalar_prefetch=1, grid=(r // tr,),
        in_specs=[shard, shard, shard, own, pl.BlockSpec((N_DEV - 1, tr, c), lambda i, me_ref: (0, i, 0))],
        out_specs=[shard] * 4)
    return pl.pallas_call(
        body, name=name, grid_spec=grid_spec, out_shape=[jax.ShapeDtypeStruct((r, c), F32)] * 4,
        compiler_params=_params(1),
    )(me, w, m, v, g_full, landing)


def _adamw_ada(sct, dmod_mine, w, m, v):
    d, n = w.shape
    tr = _tile(d, 256)

    def body(s_ref, dm_ref, w_ref, m_ref, v_ref, og_ref, od_ref, om_ref, ov_ref):
        g = _dot(s_ref[...], dm_ref[...], precision=HIGHEST)
        og_ref[...] = g
        od_ref[...], om_ref[...], ov_ref[...] = _adamw(w_ref[...], g, m_ref[...], v_ref[...])

    blk = pl.BlockSpec((tr, n), lambda i: (i, 0))
    return pl.pallas_call(
        body, name="adamw_ada", grid=(d // tr,),
        in_specs=[pl.BlockSpec((tr, N_DEV), lambda i: (i, 0)), pl.BlockSpec((N_DEV, n), lambda i: (0, 0)), blk, blk, blk],
        out_specs=[blk] * 4, out_shape=[jax.ShapeDtypeStruct((d, n), F32)] * 4, compiler_params=_params(1),
    )(sct, dmod_mine, w, m, v)


def _adamw_small(gathered, w, m, v):
    def body(g_ref, w_ref, m_ref, v_ref, og_ref, od_ref, om_ref, ov_ref):
        g = g_ref[0]
        for s in range(1, N_DEV):
            g = g + g_ref[s]
        og_ref[...] = g
        od_ref[...], om_ref[...], ov_ref[...] = _adamw(w_ref[...], g, m_ref[...], v_ref[...])

    return pl.pallas_call(
        body, name="adamw_small", out_shape=[jax.ShapeDtypeStruct(w.shape, F32)] * 4,
        compiler_params=pltpu.CompilerParams(vmem_limit_bytes=VMEM_LIMIT),
    )(gathered, w, m, v)


def _adamw_lb(dlb_mine, lb_logits, m, v):
    def body(d_ref, l_ref, m_ref, v_ref, og_ref, od_ref, om_ref, ov_ref):
        dlb = d_ref[0]
        for s in range(1, N_DEV):
            dlb = dlb + d_ref[s]
        for dr in range(2):
            lb = _sigmoid(l_ref[dr][0:1, :] - l_ref[dr][1:2, :])
            d0 = dlb[dr:dr + 1] * lb * (1.0 - lb)
            g = jnp.concatenate([d0, -d0], axis=0)
            og_ref[dr] = g
            od_ref[dr], om_ref[dr], ov_ref[dr] = _adamw(l_ref[dr], g, m_ref[dr], v_ref[dr])

    return pl.pallas_call(body, name="adamw_lb", out_shape=[jax.ShapeDtypeStruct(lb_logits.shape, F32)] * 4,
                          )(dlb_mine, lb_logits, m, v)


def _rows(a, pad_to=8):
    flat = a.reshape(-1, LANE)
    pad = (-flat.shape[0]) % pad_to
    return jnp.pad(flat, ((0, pad), (0, 0))) if pad else flat


def kernel(x, c, w_ada, b_ada, g_pre_mix, g_post_mix, g_pre_ffn, g_post_ffn, w_in, lb_logits, g_hgrn_norm, w_a_out, g_sgu_norm, w_spatial, b_spatial, w_b_out, w_o, w_ff1, w_ff2, loss_target, m_w_ada, m_b_ada, m_g_pre_mix, m_g_post_mix, m_g_pre_ffn, m_g_post_ffn, m_w_in, m_lb_logits, m_g_hgrn_norm, m_w_a_out, m_g_sgu_norm, m_w_spatial, m_b_spatial, m_w_b_out, m_w_o, m_w_ff1, m_w_ff2, v_w_ada, v_b_ada, v_g_pre_mix, v_g_post_mix, v_g_pre_ffn, v_g_post_ffn, v_w_in, v_lb_logits, v_g_hgrn_norm, v_w_a_out, v_g_sgu_norm, v_w_spatial, v_b_spatial, v_w_b_out, v_w_o, v_w_ff1, v_w_ff2):
    t, d = x.shape[1], x.shape[2]
    n_in = w_in.shape[2] * N_DEV
    width = (n_in - 2 * d) // 7
    heads = width // HEAD
    assert heads == N_DEV and width % LANE == 0
    d_ff = w_ff1.shape[2] * N_DEV
    n_ada = w_ada.shape[2]
    me = _dev_index()
    me_arr = me.reshape(1).astype(jnp.int32)
    x2, tgt = x[0], loss_target[0]

    big = [w_in[0], w_a_out[0], w_b_out[0], w_o[0], w_ff1[0], w_ff2[0]]
    big_axes = [1, 1, 1, 0, 1, 0]
    big_names = ["w_in", "w_a_out", "w_b_out", "w_o", "w_ff1", "w_ff2"]
    wf_in, = _run_exchange("gather_w_in", _gather_plan([_cast_bf16("cast_w_in", big[0])], big_axes[:1]))
    own_parts = [_cast_into_full("cast_" + nm, me_arr, w, ax) for nm, w, ax in zip(big_names[1:], big[1:], big_axes[1:])]
    wf_in, own_parts = lax.optimization_barrier((wf_in, own_parts))
    gathers = {}
    for key, lo, hi in (("mid", 1, 4), ("ff1", 4, 5), ("ff2", 5, 6)):
        plan = _direct_gather_plan(own_parts[lo - 1:hi - 1], big_axes[lo:hi])
        gathers[key] = (plan,) + _split_start("gather_%s_start" % key, plan, landing=own_parts[lo - 1:hi - 1])

    def gathered_weights(key, after):
        plan, sems, thru, _ = gathers[key]
        return _split_wait("gather_%s_wait" % key, plan, sems, thru, after)[1]

    c_rows = d // LANE
    small = _all_gather_small("gather_c_lb", _prep_small(c[0:1], lb_logits))
    sc_all = small[:, :c_rows, :].reshape(N_DEV, d)
    lb = jnp.transpose(small[:, c_rows:c_rows + 2, :], (1, 0, 2)).reshape(2, width)
    b_shard = lax.dynamic_slice_in_dim(b_ada, me * n_ada, n_ada, axis=1)
    mod_sh = _mod_shard(sc_all, w_ada[0], b_shard)
    mod_all = _all_gather_small("gather_mod", _rows(mod_sh))
    mod_all = mod_all[:, :N_DEV * n_ada // LANE, :].reshape(N_DEV, N_DEV, n_ada)
    mod6 = lax.dynamic_index_in_dim(mod_all, me, axis=1, keepdims=False).reshape(N_MOD, d)
    sh1, sc1, gt1, sh2, sc2, gt2 = [mod6[i:i + 1] for i in range(N_MOD)]

    a1 = _norm_mod(x2, g_pre_mix, sh1, sc1)
    tm = _tile(t, 512)

    def store_f32(acc, i, j, extra_refs, out_refs, rows):
        out_refs[0][...] = acc

    tn_in = _tile(n_in, 1024)
    started = [tok for key in ("mid", "ff1", "ff2") for tok in _after(gathers[key][3])]
    proj, = _mm("proj", a1, wf_in, _NN, t, n_in, d, tm, tn_in, d, started,
                [(jax.ShapeDtypeStruct((t, n_in), F32), (tm, tn_in), lambda i, j: (i, j))], store_f32)

    out_a, osum, _ = _hgrn_fwd(proj, lb, g_hgrn_norm, width, _NO_EXCHANGE)
    z_block = 5
    bst = b_spatial[0].T
    out_b = _sgu_fwd(proj, g_sgu_norm, w_spatial[0], bst, width, z_block)
    wf_a, wf_b, wf_o = gathered_weights("mid", out_b)

    tn_d = _tile(d, 512)
    blk_d = ((tm, tn_d), lambda i, j: (i, j))
    y_a, = _mm("y_a", out_a, wf_a, _NN, t, d, width, tm, tn_d, width, [],
               [(jax.ShapeDtypeStruct((t, d), F32),) + blk_d], store_f32)
    ga_blk = (5 * width + 2 * width) // tn_d
    gb_blk = ga_blk + d // tn_d

    def merge(acc, i, j, extra_refs, out_refs, rows):
        ga, gb, ya = extra_refs
        out_refs[0][...] = acc
        out_refs[1][...] = (_sigmoid(ga[...]) * ya[...] + _sigmoid(gb[...]) * acc).astype(BF16)

    y_b, merged = _mm("y_b_merge", out_b, wf_b, _NN, t, d, width, tm, tn_d, width,
                      [(proj, (tm, tn_d), lambda i, j: (i, ga_blk + j)), (proj, (tm, tn_d), lambda i, j: (i, gb_blk + j)),
                       (y_a,) + blk_d],
                      [(jax.ShapeDtypeStruct((t, d), F32),) + blk_d, (jax.ShapeDtypeStruct((t, d), BF16),) + blk_d], merge)

    tr = _tile(t, 512)
    rc = 32 if tr % 32 == 0 else None
    row_d = ((tr, d), lambda i, j: (i, 0))
    vec_d = ((1, d), lambda i, j: (0, 0))

    def post_mix(acc, i, j, extra_refs, out_refs, rows):
        x_r, gt1_r, g2_r, g3_r, sc2_r, sh2_r = extra_refs
        h1 = x_r[rows, :] + gt1_r[...] * (acc * _rms(acc) * g2_r[...])
        out_refs[0][rows, :] = acc
        out_refs[1][rows, :] = h1
        out_refs[2][rows, :] = ((h1 * _rms(h1) * g3_r[...]) * (1.0 + sc2_r[...]) + sh2_r[...]).astype(BF16)

    mo, h1, a2 = _mm("w_o_post_mix", merged, wf_o, _NN, t, d, d, tr, d, d,
                     [(x2,) + row_d, (gt1,) + vec_d, (g_post_mix,) + vec_d, (g_pre_ffn,) + vec_d, (sc2,) + vec_d, (sh2,) + vec_d],
                     [(jax.ShapeDtypeStruct((t, d), F32),) + row_d, (jax.ShapeDtypeStruct((t, d), F32),) + row_d,
                      (jax.ShapeDtypeStruct((t, d), BF16),) + row_d], post_mix, row_chunk=rc)

    tn_f = _tile(d_ff, 1024)
    blk_f = ((tm, tn_f), lambda i, j: (i, j))

    def relu_sq(acc, i, j, extra_refs, out_refs, rows):
        r = jnp.maximum(acc, 0.0)
        out_refs[0][...] = acc.astype(BF16)
        out_refs[1][...] = (r * r).astype(BF16)

    wf_1, = gathered_weights("ff1", a2)
    hff, act = _mm(
        "ff1", a2, wf_1, _NN, t, d_ff, d, tm, tn_f, d, [],
        [(jax.ShapeDtypeStruct((t, d_ff), BF16),) + blk_f, (jax.ShapeDtypeStruct((t, d_ff), BF16),) + blk_f], relu_sq)
    wf_2, = gathered_weights("ff2", act)

    sums_d = ((8, d), lambda i, j: (0, 0))

    def zero_first(sums_r, i, rows):
        if rows.start in (None, 0):
            @pl.when(i == 0)
            def _():
                sums_r[...] = jnp.zeros_like(sums_r)

    def loss_head(acc, i, j, extra_refs, out_refs, rows):
        h1_r, tgt_r, gt2_r, g4_r = extra_refs
        dy_r, dff_r, sums_r = out_refs
        r4 = _rms(acc)
        ffn = acc * r4
        n4 = ffn * g4_r[...]
        err = h1_r[rows, :] + gt2_r[...] * n4 - tgt_r[rows, :]
        dy = err * (1.0 / d)
        dy_r[rows, :] = dy
        dn4 = dy * gt2_r[...]
        dffn = dn4 * g4_r[...]
        dff_r[rows, :] = (r4 * (dffn - ffn * jnp.mean(dffn * ffn, axis=-1, keepdims=True))).astype(BF16)
        zero_first(sums_r, i, rows)

        sums_r[0:1, :] += _colsum(err * err)
        sums_r[1:2, :] += _colsum(dy * n4)
        sums_r[2:3, :] += _colsum(dn4 * ffn)

    tk_f = _tile(d_ff, 1024)
    dy, dff, sums_f = _mm("ff2_loss", act, wf_2, _NN, t, d, d_ff, tr, d, tk_f,
                          [(h1,) + row_d, (tgt,) + row_d, (gt2,) + vec_d, (g_post_ffn,) + vec_d],
                          [(jax.ShapeDtypeStruct((t, d), F32),) + row_d, (jax.ShapeDtypeStruct((t, d), BF16),) + row_d,
                           (jax.ShapeDtypeStruct((8, d), F32),) + sums_d], loss_head, row_chunk=rc)
    loss_mine = (0.5 / d) * jnp.sum(sums_f[0])

    def relu_sq_bwd(acc, i, j, extra_refs, out_refs, rows):
        out_refs[0][...] = (acc * (2.0 * jnp.maximum(extra_refs[0][...].astype(F32), 0.0))).astype(BF16)

    dhff, = _mm("d_hff", dff, wf_2, _NT, t, d_ff, d, tm, tn_f, d, [(hff,) + blk_f],
                [(jax.ShapeDtypeStruct((t, d_ff), BF16),) + blk_f], relu_sq_bwd)
    scatters = {}

    def send_grads(key, grads16, axes):
        plan = _scatter_plan(grads16, axes)
        scatters[key] = (plan,) + _split_start("scatter_%s_start" % key, plan)
        return scatters[key][3]

    def received_grads(key, after):
        plan, sems, thru, _ = scatters[key]
        return _split_wait("scatter_%s_wait" % key, plan, sems, thru, after)[1]

    gw_ff2, gw_ff2_16 = _grad_w("grad_w_ff2", act, dff)
    sent_ff2 = send_grads("ff2", [gw_ff2_16], big_axes[5:6])
    gw_ff1, gw_ff1_16 = _grad_w("grad_w_ff1", a2, dhff, token=sent_ff2)
    sent_ff1 = send_grads("ff1", [gw_ff1_16], big_axes[4:5])

    def pre_ffn_bwd(acc, i, j, extra_refs, out_refs, rows):
        h1_r, dy_r, mo_r, sc2_r, g3_r, gt1_r, g2_r = extra_refs[:7]
        dh1_r, dmo_r, sums_r = out_refs
        h1v = h1_r[rows, :]
        r3 = _rms(h1v)
        h1n = h1v * r3
        dn3 = acc * (1.0 + sc2_r[...])
        dh1n = dn3 * g3_r[...]
        dh1 = dy_r[rows, :] + r3 * (dh1n - h1n * jnp.mean(dh1n * h1n, axis=-1, keepdims=True))
        dh1_r[rows, :] = dh1
        mov = mo_r[rows, :]
        r2 = _rms(mov)
        mon = mov * r2
        dn2 = dh1 * gt1_r[...]
        dmon = dn2 * g2_r[...]
        dmo_r[rows, :] = (r2 * (dmon - mon * jnp.mean(dmon * mon, axis=-1, keepdims=True))).astype(BF16)
        zero_first(sums_r, i, rows)

        sums_r[0:1, :] += _colsum(acc)
        sums_r[1:2, :] += _colsum(acc * (h1n * g3_r[...]))
        sums_r[2:3, :] += _colsum(dn3 * h1n)
        sums_r[3:4, :] += _colsum(dh1 * (mon * g2_r[...]))
        sums_r[4:5, :] += _colsum(dn2 * mon)

    dh1, dmo, sums_m = _mm("d_a2_pre_ffn", dhff, wf_1, _NT, t, d, d_ff, tr, d, tk_f,
                           [(h1,) + row_d, (dy,) + row_d, (mo,) + row_d, (sc2,) + vec_d, (g_pre_ffn,) + vec_d,
                            (gt1,) + vec_d, (g_post_mix,) + vec_d] + _after(sent_ff1),
                           [(jax.ShapeDtypeStruct((t, d), F32),) + row_d, (jax.ShapeDtypeStruct((t, d), BF16),) + row_d,
                            (jax.ShapeDtypeStruct((8, d), F32),) + sums_d], pre_ffn_bwd, row_chunk=rc)
    gw_o, gw_o_16 = _grad_w("grad_w_o", merged, dmo)

    n_j = d // tn_d

    def merge_bwd_body(dmo_ref, wo_ref, ga_ref, gb_ref, ya_ref, yb_ref, dya_ref, dyb_ref, dproj_ref, acc_s):
        g = pl.program_id(2)

        @pl.when(g == 0)
        def _():
            dm = _dot(dmo_ref[...], wo_ref[...], _NT)
            acc_s[...] = dm
            sa = _sigmoid(ga_ref[...])
            dya_ref[...] = (dm * sa).astype(BF16)
            dproj_ref[...] = (dm * ya_ref[...] * sa * (1.0 - sa)).astype(BF16)

        @pl.when(g == 1)
        def _():
            dm = acc_s[...]
            sb = _sigmoid(gb_ref[...])
            dyb_ref[...] = (dm * sb).astype(BF16)
            dproj_ref[...] = (dm * yb_ref[...] * sb * (1.0 - sb)).astype(BF16)

    tile3 = pl.BlockSpec((tm, tn_d), lambda i, j, g: (i, j))
    dy_a, dy_b, dproj = pl.pallas_call(
        merge_bwd_body, name="d_merged", grid=(t // tm, n_j, 2),
        in_specs=[pl.BlockSpec((tm, d), lambda i, j, g: (i, 0)), pl.BlockSpec((tn_d, d), lambda i, j, g: (j, 0)),
                  pl.BlockSpec((tm, tn_d), lambda i, j, g: (i, ga_blk + j)),
                  pl.BlockSpec((tm, tn_d), lambda i, j, g: (i, gb_blk + j)), tile3, tile3],
        out_specs=[tile3, tile3, pl.BlockSpec((tm, tn_d), lambda i, j, g: (i, ga_blk + g * n_j + j))],
        out_shape=[jax.ShapeDtypeStruct((t, d), BF16), jax.ShapeDtypeStruct((t, d), BF16),
                   jax.ShapeDtypeStruct((t, n_in), BF16)],
        scratch_shapes=[pltpu.VMEM((tm, tn_d), F32)], compiler_params=_params(3),
    )(dmo, wf_o, proj, proj, y_a, y_b)

    def store_bf16(acc, i, j, extra_refs, out_refs, rows):
        out_refs[0][...] = acc.astype(BF16)

    tn_w = _tile(width, 512)
    blk_w = ((tm, tn_w), lambda i, j: (i, j))
    dout_a, = _mm("d_out_a", dy_a, wf_a, _NT, t, width, d, tm, tn_w, d, [],
                  [(jax.ShapeDtypeStruct((t, width), BF16),) + blk_w], store_bf16)
    dout_b, = _mm("d_out_b", dy_b, wf_b, _NT, t, width, d, tm, tn_w, d, [],
                  [(jax.ShapeDtypeStruct((t, width), BF16),) + blk_w], store_bf16)
    gw_a, gw_a_16 = _grad_w("grad_w_a_out", out_a, dy_a)
    gw_b, gw_b_16 = _grad_w("grad_w_b_out", out_b, dy_b)

    w_st = jnp.swapaxes(w_spatial[0], 1, 2)
    dproj, dg_sgu, dw_sp, dbst = _sgu_bwd(proj, dout_b, dproj, g_sgu_norm, w_spatial[0], w_st, bst, width, z_block)
    sent_mid = send_grads("mid", [gw_a_16, gw_b_16, gw_o_16], big_axes[1:4])
    dproj, dgh_heads, dlb, _ = _hgrn_bwd(proj, osum, dout_a, dproj, lb, g_hgrn_norm, width, _NO_EXCHANGE)
    gw_in, gw_in_16 = _grad_w("grad_w_in", a1, dproj, token=sent_mid)
    sent_in = send_grads("in", [gw_in_16], big_axes[:1])

    def pre_mix_bwd(acc, i, j, extra_refs, out_refs, rows):
        x_r, dh1_r, sc1_r, g1_r = extra_refs[:4]
        dx_r, sums_r = out_refs
        xv = x_r[rows, :]
        r1 = _rms(xv)
        xn = xv * r1
        dn1 = acc * (1.0 + sc1_r[...])
        dxn = dn1 * g1_r[...]
        dx_r[rows, :] = dh1_r[rows, :] + r1 * (dxn - xn * jnp.mean(dxn * xn, axis=-1, keepdims=True))
        zero_first(sums_r, i, rows)

        sums_r[0:1, :] += _colsum(acc)
        sums_r[1:2, :] += _colsum(acc * (xn * g1_r[...]))
        sums_r[2:3, :] += _colsum(dn1 * xn)

    tk_in = _tile(n_in, 1024)
    grad_x, sums_x = _mm(
        "d_a1_pre_mix", dproj, wf_in, _NT, t, d, n_in, tr, d, tk_in,
        [(x2,) + row_d, (dh1,) + row_d, (sc1,) + vec_d, (g_pre_mix,) + vec_d] + _after(sent_in),
        [(jax.ShapeDtypeStruct((t, d), F32),) + row_d, (jax.ShapeDtypeStruct((8, d), F32),) + sums_d],
        pre_mix_bwd, row_chunk=rc)

    dmod = jnp.concatenate([sums_x[0:2], sums_m[3:4], sums_m[0:2], sums_f[1:2]], axis=0).reshape(N_DEV, n_ada // LANE, LANE)
    ada_rows = -(-(n_ada // LANE) // 8) * 8
    dmod = jnp.pad(dmod, ((0, 0), (0, ada_rows - n_ada // LANE), (0, 0))).reshape(N_DEV * ada_rows, LANE)
    parts = [dmod, _rows(sums_x[2:3]), _rows(sums_m[4:5]), _rows(sums_m[2:3]), _rows(sums_f[2:3]),
             _rows(jnp.sum(dgh_heads, axis=0)), _rows(dg_sgu), _rows(dw_sp), _rows(dbst.T)]
    n_params = sum(p.shape[0] for p in parts)
    parts.append(jnp.full((8, LANE), loss_mine, F32))
    n_common = n_params + 8
    payload = jnp.concatenate(parts + [_rows(dlb)], axis=0)

    moms = [m_w_in, m_w_a_out, m_w_b_out, m_w_o, m_w_ff1, m_w_ff2]
    vars_ = [v_w_in, v_w_a_out, v_w_b_out, v_w_o, v_w_ff1, v_w_ff2]
    big_out = {}

    def big_update(nm, g_full, landing):
        k = big_names.index(nm)
        outs = _adamw_big("adamw_" + nm, me_arr, big[k], moms[k][0], vars_[k][0], g_full, landing, big_axes[k])
        big_out[nm] = [o[None] for o in outs]
        return outs[0]

    land_ff2, = received_grads("ff2", grad_x)
    done = big_update("w_ff2", gw_ff2, land_ff2)
    land_ff1, = received_grads("ff1", done)
    done = big_update("w_ff1", gw_ff1, land_ff1)
    land_a, land_b, land_o = received_grads("mid", done)
    big_update("w_a_out", gw_a, land_a)
    big_update("w_b_out", gw_b, land_b)
    done = big_update("w_o", gw_o, land_o)

    payload, _ = lax.optimization_barrier((payload, done))
    gathered = _all_gather_small("gather_small_grads", payload)

    dmod_mine = lax.dynamic_slice_in_dim(gathered[:, :N_DEV * ada_rows, :].reshape(N_DEV, N_DEV, ada_rows * LANE),
                                         me, 1, axis=1)[:, 0, :n_ada]
    ada_out = [o[None] for o in _adamw_ada(sc_all.T, dmod_mine, w_ada[0], m_w_ada[0], v_w_ada[0])]

    def pack(b_, g1_, g2_, g3_, g4_, gh_, gs_, ws_, bs_):
        b3 = b_.reshape(N_DEV, n_ada // LANE, LANE)
        b3 = jnp.pad(b3, ((0, 0), (0, ada_rows - n_ada // LANE), (0, 0))).reshape(N_DEV * ada_rows, LANE)
        return jnp.concatenate([b3, _rows(g1_), _rows(g2_), _rows(g3_), _rows(g4_), _rows(gh_), _rows(gs_),
                                _rows(ws_), _rows(bs_), jnp.zeros((8, LANE), F32)], axis=0)

    small_w = (b_ada, g_pre_mix, g_post_mix, g_pre_ffn, g_post_ffn, g_hgrn_norm, g_sgu_norm, w_spatial, b_spatial)
    small_m = (m_b_ada, m_g_pre_mix, m_g_post_mix, m_g_pre_ffn, m_g_post_ffn, m_g_hgrn_norm, m_g_sgu_norm, m_w_spatial, m_b_spatial)
    small_v = (v_b_ada, v_g_pre_mix, v_g_post_mix, v_g_pre_ffn, v_g_post_ffn, v_g_hgrn_norm, v_g_sgu_norm, v_w_spatial, v_b_spatial)
    packed = _adamw_small(gathered[:, :n_common, :], pack(*small_w), pack(*small_m), pack(*small_v))

    def unpack(slab):
        outs, at = [], 0
        b3 = slab[:N_DEV * ada_rows].reshape(N_DEV, ada_rows, LANE)[:, :n_ada // LANE, :]
        outs.append(b3.reshape(b_ada.shape))
        at = N_DEV * ada_rows
        for ref in small_w[1:]:
            n_el = ref.size
            n_r = -(-(n_el // LANE) // 8) * 8
            outs.append(slab[at:at + n_el // LANE].reshape(ref.shape))
            at += n_r
        return outs

    small_out = [unpack(s) for s in packed]
    loss = packed[0][n_params, 0]

    dlb_all = gathered[:, n_common:n_common + 2 * heads, :].reshape(N_DEV, 2, heads, LANE)
    dlb_mine = lax.dynamic_index_in_dim(dlb_all, me, axis=2, keepdims=False)
    lb_out = _adamw_lb(dlb_mine, lb_logits, m_lb_logits, v_lb_logits)

    land_in, = received_grads("in", ada_out[0])
    big_update("w_in", gw_in, land_in)

    order = ["w_ada", "b_ada", "g_pre_mix", "g_post_mix", "g_pre_ffn", "g_post_ffn", "w_in", "lb_logits", "g_hgrn_norm",
             "w_a_out", "g_sgu_norm", "w_spatial", "b_spatial", "w_b_out", "w_o", "w_ff1", "w_ff2"]
    small_names = ["b_ada", "g_pre_mix", "g_post_mix", "g_pre_ffn", "g_post_ffn", "g_hgrn_norm", "g_sgu_norm", "w_spatial", "b_spatial"]

    def leaf(kind, nm):
        if nm == "w_ada":
            return ada_out[kind]
        if nm == "lb_logits":
            return lb_out[kind]
        if nm in big_out:
            return big_out[nm][kind]
        return small_out[kind][small_names.index(nm)]

    result = [loss, grad_x[None]]
    for kind in range(4):
        result += [leaf(kind, nm) for nm in order]
    return tuple(result)
```

```python
import functools
import math

import jax
import jax.numpy as jnp
from jax import lax
from jax.experimental import pallas as pl
from jax.experimental.pallas import tpu as pltpu

F32 = jnp.float32
BF16 = jnp.bfloat16
MESH = pl.DeviceIdType.MESH
HIGHEST = lax.Precision.HIGHEST

N_DEV = 8
HEAD = 128
A_CHUNK = 32
N_MOD = 6
EPS = 1e-6
LANE = 128
VMEM_LIMIT = 60 * 1024 * 1024

ADAM_LR = 0.001
ADAM_B1 = 0.9
ADAM_B2 = 0.999
ADAM_EPS = 1e-08
ADAM_WD = 0.01
ADAM_STEP = 10

_NN = (((1,), (0,)), ((), ()))
_NT = (((1,), (1,)), ((), ()))
_TN = (((0,), (0,)), ((), ()))


def _dot(a, b, dims=_NN, precision=None):
    return lax.dot_general(a, b, dims, preferred_element_type=F32, precision=precision)


def _bdot(a, b, dims=_NN):
    return _dot(a.astype(BF16), b.astype(BF16), dims)


def _params(n_grid):
    return pltpu.CompilerParams(dimension_semantics=("arbitrary",) * n_grid, vmem_limit_bytes=VMEM_LIMIT)


def _dev_index():
    return lax.axis_index("x") * 4 + lax.axis_index("y") * 2 + lax.axis_index("c")


def _dev_coords(i):
    return (i // 4, (i // 2) % 2, i % 2)


def _sigmoid(x):
    return 1.0 / (1.0 + jnp.exp(-x))


def _erf(x):
    ax = jnp.abs(x)
    t = 1.0 / (1.0 + 0.3275911 * ax)
    poly = ((((1.061405429 * t - 1.453152027) * t + 1.421413741) * t - 0.284496736) * t + 0.254829592) * t
    y = 1.0 - poly * jnp.exp(-ax * ax)
    return jnp.where(x < 0, -y, y)


def _gelu_and_grad(x):
    cdf = 0.5 * (1.0 + _erf(x * (2.0 ** -0.5)))
    pdf = jnp.exp(-0.5 * x * x) * (1.0 / math.sqrt(2.0 * math.pi))
    return x * cdf, cdf + x * pdf


def _rms(x):
    return lax.rsqrt(jnp.mean(x * x, axis=-1, keepdims=True) + EPS)


def _colsum(x):
    return jnp.sum(x, axis=0, keepdims=True)


def _tile(n, want):
    if n <= want:
        return n
    t = (want // LANE) * LANE
    while n % t:
        t -= LANE
    assert t > 0, (n, want)
    return t


def _all_gather_small(name, payload):
    rows = payload.shape[0]

    def body(p_ref, out_ref, send_sems, recv_sems, local_sem):
        me = _dev_index()
        mine = pltpu.make_async_copy(p_ref, out_ref.at[me], local_sem)
        mine.start()
        sends = []
        for r in range(1, N_DEV):
            peer = (me + r) % N_DEV
            cp = pltpu.make_async_remote_copy(
                src_ref=p_ref, dst_ref=out_ref.at[me], send_sem=send_sems.at[r - 1], recv_sem=recv_sems.at[r - 1],
                device_id=_dev_coords(peer), device_id_type=MESH)
            cp.start()
            sends.append(cp)
        for r in range(1, N_DEV):
            src = (me + N_DEV - r) % N_DEV
            pltpu.make_async_remote_copy(
                src_ref=p_ref, dst_ref=out_ref.at[src], send_sem=send_sems.at[r - 1], recv_sem=recv_sems.at[r - 1],
                device_id=_dev_coords(src), device_id_type=MESH).wait_recv()
        for cp in sends:
            cp.wait_send()
        mine.wait()

    return pl.pallas_call(
        body, name=name,
        out_shape=jax.ShapeDtypeStruct((N_DEV, rows, LANE), F32),
        in_specs=[pl.BlockSpec(memory_space=pltpu.VMEM)],
        out_specs=pl.BlockSpec(memory_space=pltpu.VMEM),
        scratch_shapes=[pltpu.SemaphoreType.DMA((N_DEV - 1,)), pltpu.SemaphoreType.DMA((N_DEV - 1,)),
                        pltpu.SemaphoreType.DMA],
        compiler_params=pltpu.CompilerParams(vmem_limit_bytes=VMEM_LIMIT),
    )(payload)


def _region(ref, dev, axis, n):
    start = pl.multiple_of(dev * n, LANE if axis == 1 else 16)
    return ref.at[:, pl.ds(start, n)] if axis == 1 else ref.at[pl.ds(start, n), :]


class _Exchange:
    def __init__(self, arrays, out_shapes, sems, start, finish):
        self.arrays, self.out_shapes, self.sems, self.start, self.finish = arrays, out_shapes, sems, start, finish


def _gather_plan(shards, axes):
    n_w = len(shards)
    fulls = []
    for s, ax in zip(shards, axes):
        shp = (s.shape[0], s.shape[1] * N_DEV) if ax == 1 else (s.shape[0] * N_DEV, s.shape[1])
        fulls.append(jax.ShapeDtypeStruct(shp, BF16))
    widths = [s.shape[ax] for s, ax in zip(shards, axes)]

    def places():
        x, y, c = lax.axis_index("x"), lax.axis_index("y"), lax.axis_index("c")
        chips = [(1 - x, y), (x, 1 - y), (1 - x, 1 - y)]
        return (x, y, c), (x, y, 1 - c), chips

    def index(p):
        return p[0] * 4 + p[1] * 2 + p[2]

    def copy(w, k, s_refs, f_refs, sems, block, to, from_shard):
        send_sems, recv_sems, _ = sems
        dst = _region(f_refs[w], index(block), axes[w], widths[w])
        return pltpu.make_async_remote_copy(
            src_ref=s_refs[w] if from_shard else dst, dst_ref=dst,
            send_sem=send_sems.at[w, k], recv_sem=recv_sems.at[w, k], device_id=to, device_id_type=MESH)

    def local(w, s_refs, f_refs, sems, me):
        return pltpu.make_async_copy(s_refs[w], _region(f_refs[w], index(me), axes[w], widths[w]), sems[2].at[w])

    def start(s_refs, f_refs, sems):
        me, sib, chips = places()
        for w in range(n_w):
            local(w, s_refs, f_refs, sems, me).start()
            copy(w, 0, s_refs, f_refs, sems, me, sib, True).start()
            for j, chip in enumerate(chips):
                copy(w, 1 + j, s_refs, f_refs, sems, me, (*chip, me[2]), True).start()

    def finish(s_refs, f_refs, sems):
        me, sib, chips = places()
        for w in range(n_w):
            for j, chip in enumerate(chips):
                copy(w, 1 + j, s_refs, f_refs, sems, (*chip, me[2]), me, True).wait_recv()
                copy(w, 4 + j, s_refs, f_refs, sems, (*chip, me[2]), sib, False).start()
        for w in range(n_w):
            copy(w, 0, s_refs, f_refs, sems, sib, me, True).wait_recv()
            for j, chip in enumerate(chips):
                copy(w, 4 + j, s_refs, f_refs, sems, (*chip, sib[2]), me, False).wait_recv()
        for w in range(n_w):
            for k in range(N_DEV - 1):
                copy(w, k, s_refs, f_refs, sems, me, sib, True).wait_send()
            local(w, s_refs, f_refs, sems, me).wait()

    sems = [pltpu.SemaphoreType.DMA((n_w, N_DEV - 1)), pltpu.SemaphoreType.DMA((n_w, N_DEV - 1)),
            pltpu.SemaphoreType.DMA((n_w,))]
    return _Exchange(list(shards), fulls, sems, start, finish)


def _scatter_plan(grads, axes):
    n_w = len(grads)
    lands = []
    for g, ax in zip(grads, axes):
        shp = (g.shape[0], g.shape[1] // N_DEV) if ax == 1 else (g.shape[0] // N_DEV, g.shape[1])
        lands.append(jax.ShapeDtypeStruct((N_DEV - 1,) + shp, BF16))
    widths = [ld.shape[1 + ax] for ld, ax in zip(lands, axes)]

    def copy(w, r, g_refs, l_refs, sems, block, to):
        return pltpu.make_async_remote_copy(
            src_ref=_region(g_refs[w], block, axes[w], widths[w]), dst_ref=l_refs[w].at[r - 1],
            send_sem=sems[0].at[w * (N_DEV - 1) + r - 1], recv_sem=sems[1].at[w * (N_DEV - 1) + r - 1],
            device_id=_dev_coords(to), device_id_type=MESH)

    def start(g_refs, l_refs, sems):
        me = _dev_index()
        for w in range(n_w):
            for r in range(1, N_DEV):
                owner = (me + r) % N_DEV
                copy(w, r, g_refs, l_refs, sems, owner, owner).start()

    def finish(g_refs, l_refs, sems):
        me = _dev_index()
        for w in range(n_w):
            for r in range(1, N_DEV):
                copy(w, r, g_refs, l_refs, sems, me, (me + N_DEV - r) % N_DEV).wait_recv()
        for w in range(n_w):
            for r in range(1, N_DEV):
                copy(w, r, g_refs, l_refs, sems, me, (me + r) % N_DEV).wait_send()

    sems = [pltpu.SemaphoreType.DMA((n_w * (N_DEV - 1),)), pltpu.SemaphoreType.DMA((n_w * (N_DEV - 1),))]
    return _Exchange(list(grads), lands, sems, start, finish)


def _scatter_half_plan(grad_half, half):
    width = grad_half.shape[1] // (N_DEV // 2)
    land = jax.ShapeDtypeStruct((N_DEV - 1, grad_half.shape[0], width), BF16)

    def copy(r, g_refs, l_refs, sems, owner, to):
        return pltpu.make_async_remote_copy(
            src_ref=_region(g_refs[0], owner % (N_DEV // 2), 1, width), dst_ref=l_refs[0].at[r - 1],
            send_sem=sems[0].at[r - 1], recv_sem=sems[1].at[r - 1],
            device_id=_dev_coords(to), device_id_type=MESH)

    def in_half(dev):
        return dev // (N_DEV // 2) == half

    def start(g_refs, l_refs, sems):
        me = _dev_index()
        for r in range(1, N_DEV):
            owner = (me + r) % N_DEV

            @pl.when(in_half(owner))
            def _():
                copy(r, g_refs, l_refs, sems, owner, owner).start()

    def finish(g_refs, l_refs, sems):
        me = _dev_index()

        @pl.when(in_half(me))
        def _():
            for r in range(1, N_DEV):
                copy(r, g_refs, l_refs, sems, me, (me + N_DEV - r) % N_DEV).wait_recv()

        for r in range(1, N_DEV):
            owner = (me + r) % N_DEV

            @pl.when(in_half(owner))
            def _():
                copy(r, g_refs, l_refs, sems, owner, owner).wait_send()

    sems = [pltpu.SemaphoreType.DMA((N_DEV - 1,)), pltpu.SemaphoreType.DMA((N_DEV - 1,))]
    return _Exchange([grad_half], [land], sems, start, finish)


def _run_exchange(name, plan):
    n_in, n_out = len(plan.arrays), len(plan.out_shapes)

    def body(*refs):
        ins, outs, sems = refs[:n_in], refs[n_in:n_in + n_out], refs[n_in + n_out:]
        plan.start(ins, outs, sems)
        plan.finish(ins, outs, sems)

    any_spec = pl.BlockSpec(memory_space=pl.ANY)
    return pl.pallas_call(
        body, name=name, out_shape=plan.out_shapes,
        in_specs=[any_spec] * n_in, out_specs=[any_spec] * n_out, scratch_shapes=plan.sems,
    )(*plan.arrays)


_NO_EXCHANGE = _Exchange([], [], [], lambda i, o, s: None, lambda i, o, s: None)


def _direct_gather_plan(fulls, axes):
    n_w = len(fulls)
    widths = [f.shape[ax] // N_DEV for f, ax in zip(fulls, axes)]
    fulls = [jax.ShapeDtypeStruct(f.shape, f.dtype) for f in fulls]

    def copy(w, r, s_refs, f_refs, sems, block, to):
        part = _region(f_refs[w], block, axes[w], widths[w])
        return pltpu.make_async_remote_copy(
            src_ref=part, dst_ref=part,
            send_sem=sems[0].at[w * (N_DEV - 1) + r - 1], recv_sem=sems[1].at[w * (N_DEV - 1) + r - 1],
            device_id=_dev_coords(to), device_id_type=MESH)

    def start(s_refs, f_refs, sems):
        me = _dev_index()
        for w in range(n_w):
            for r in range(1, N_DEV):
                copy(w, r, s_refs, f_refs, sems, me, (me + r) % N_DEV).start()

    def finish(s_refs, f_refs, sems):
        me = _dev_index()
        for w in range(n_w):
            for r in range(1, N_DEV):
                src = (me + N_DEV - r) % N_DEV
                copy(w, r, s_refs, f_refs, sems, src, src).wait_recv()
        for w in range(n_w):
            for r in range(1, N_DEV):
                copy(w, r, s_refs, f_refs, sems, me, (me + r) % N_DEV).wait_send()

    sems = [pltpu.SemaphoreType.DMA((n_w * (N_DEV - 1),)), pltpu.SemaphoreType.DMA((n_w * (N_DEV - 1),))]
    return _Exchange([], fulls, sems, start, finish)


_HBM = pl.BlockSpec(memory_space=pltpu.HBM)
_SEM = pl.BlockSpec(memory_space=pltpu.SEMAPHORE)
_EFFECT = pltpu.SideEffectType.DATAFLOW_SIDE_EFFECTING


def _split_start(name, plan, landing=None):
    n_in, n_out, n_sem = len(plan.arrays), len(plan.out_shapes), len(plan.sems)

    def body(*refs):
        ins, lands = refs[:n_in], refs[n_in:n_in + n_out]
        sems = refs[n_in + n_out:n_in + n_out + n_sem]
        token = refs[-1]
        plan.start(ins, lands, sems)
        token[...] = jnp.zeros_like(token)

    hbm = lambda a: pltpu.HBM(a.shape, a.dtype)
    results = pl.pallas_call(
        body, name=name,
        out_shape=tuple(plan.sems) + tuple(hbm(a) for a in plan.arrays) + tuple(hbm(a) for a in plan.out_shapes)
        + (jax.ShapeDtypeStruct((8, LANE), F32),),
        in_specs=(_HBM,) * (n_in + n_out),
        out_specs=(_SEM,) * n_sem + (_HBM,) * (n_in + n_out) + (pl.BlockSpec(memory_space=pltpu.VMEM),),
        input_output_aliases={i: n_sem + i for i in range(n_in + n_out)},
        compiler_params=pltpu.CompilerParams(has_side_effects=_EFFECT),
    )(*[pltpu.with_memory_space_constraint(a, pltpu.HBM) for a in plan.arrays],
      *[pltpu.with_memory_space_constraint(a, pltpu.HBM)
        for a in (landing if landing is not None else [lax.empty(a.shape, a.dtype) for a in plan.out_shapes])])
    return results[:n_sem], results[n_sem:n_sem + n_in + n_out], results[-1]


def _split_wait(name, plan, sems, thru, after):
    n_in, n_out, n_sem = len(plan.arrays), len(plan.out_shapes), len(plan.sems)

    def body(*refs):
        ins, lands = refs[:n_in], refs[n_in:n_in + n_out]
        sem_refs = refs[n_in + n_out:n_in + n_out + n_sem]
        plan.finish(ins, lands, sem_refs)

    hbm = lambda a: pltpu.HBM(a.shape, a.dtype)
    results = pl.pallas_call(
        body, name=name,
        out_shape=tuple(hbm(a) for a in plan.arrays) + tuple(hbm(a) for a in plan.out_shapes),
        in_specs=(_HBM,) * (n_in + n_out) + (_SEM,) * n_sem + (pl.BlockSpec(memory_space=pl.ANY),),
        out_specs=(_HBM,) * (n_in + n_out),
        input_output_aliases={i: i for i in range(n_in + n_out)},
        compiler_params=pltpu.CompilerParams(has_side_effects=_EFFECT),
    )(*thru, *sems, after)
    return results[:n_in], results[n_in:]


def _cast_into_full(name, me, w, axis):
    r, c = w.shape
    tr = _tile(r, 256)
    if axis == 1:
        shape, place = (r, c * N_DEV), pl.BlockSpec((tr, c), lambda i, me_ref: (i, me_ref[0]))
    else:
        shape, place = (r * N_DEV, c), pl.BlockSpec((tr, c), lambda i, me_ref: (me_ref[0] * (r // tr) + i, 0))

    def body(me_ref, w_ref, o_ref):
        o_ref[...] = w_ref[...].astype(BF16)

    grid_spec = pltpu.PrefetchScalarGridSpec(
        num_scalar_prefetch=1, grid=(r // tr,),
        in_specs=[pl.BlockSpec((tr, c), lambda i, me_ref: (i, 0))], out_specs=place)
    return pl.pallas_call(body, name=name, grid_spec=grid_spec, out_shape=jax.ShapeDtypeStruct(shape, BF16),
                          compiler_params=_params(1))(me, w)


def _mm(name, a, b, dims, m, n, k, tm, tn, tk, extras, outs, epilogue, row_chunk=None, exchange=None,
        b_col_block=0):
    ni, nj, nk = m // tm, n // tn, k // tk
    ne, no = len(extras), len(outs)
    xin = len(exchange.arrays) if exchange else 0
    xout = len(exchange.out_shapes) if exchange else 0
    if dims == _TN:
        a_spec = pl.BlockSpec((tk, tm), lambda i, j, kk: (kk, i))
    else:
        a_spec = pl.BlockSpec((tm, tk), lambda i, j, kk: (i, kk))
    if dims == _NT:
        b_spec = pl.BlockSpec((tn, tk), lambda i, j, kk: (j, kk))
    else:
        b_spec = pl.BlockSpec((tk, tn), lambda i, j, kk: (kk, j + b_col_block))
    chunks = [slice(None)] if row_chunk is None else [slice(r, r + row_chunk) for r in range(0, tm, row_chunk)]

    def lift(index_map):
        return lambda i, j, kk: index_map(i, j)

    def body(a_ref, b_ref, *rest):
        extra_refs, rest = rest[:ne], rest[ne:]
        xin_refs, rest = rest[:xin], rest[xin:]
        out_refs, rest = rest[:no], rest[no:]
        xout_refs, rest = rest[:xout], rest[xout:]
        i, j, kk = pl.program_id(0), pl.program_id(1), pl.program_id(2)
        if exchange:
            sem_refs = rest[1:] if nk > 1 else rest

            @pl.when((i == 0) & (j == 0) & (kk == 0))
            def _():
                exchange.start(xin_refs, xout_refs, sem_refs)

        if nk == 1:
            part = _dot(a_ref[...], b_ref[...], dims)
            for rows in chunks:
                epilogue(part[rows], i, j, extra_refs, out_refs, rows)
        else:
            acc_ref = rest[0]

            @pl.when(kk == 0)
            def _():
                acc_ref[...] = _dot(a_ref[...], b_ref[...], dims)

            @pl.when(kk > 0)
            def _():
                acc_ref[...] += _dot(a_ref[...], b_ref[...], dims)

            @pl.when(kk == nk - 1)
            def _():
                for rows in chunks:
                    epilogue(acc_ref[rows, :], i, j, extra_refs, out_refs, rows)

        if exchange:
            @pl.when((i == ni - 1) & (j == nj - 1) & (kk == nk - 1))
            def _():
                exchange.finish(xin_refs, xout_refs, sem_refs)

    any_spec = pl.BlockSpec(memory_space=pl.ANY)
    once = dict(pipeline_mode=pl.Buffered(1)) if (row_chunk is not None and nk > 1) else {}
    results = pl.pallas_call(
        body, name=name,
        grid=(ni, nj, nk),
        in_specs=[a_spec, b_spec] + [pl.BlockSpec(bs, lift(im), **once) for _, bs, im in extras] + [any_spec] * xin,
        out_specs=[pl.BlockSpec(bs, lift(im), **once) for _, bs, im in outs] + [any_spec] * xout,
        out_shape=[sd for sd, _, _ in outs] + (list(exchange.out_shapes) if exchange else []),
        scratch_shapes=([pltpu.VMEM((tm, tn), F32)] if nk > 1 else []) + (list(exchange.sems) if exchange else []),
        compiler_params=_params(3),
    )(a, b, *[arr for arr, _, _ in extras], *(exchange.arrays if exchange else []))
    return (results[:no], results[no:]) if exchange else results


def _after(token):
    return [(token, (8, LANE), lambda i, j: (0, 0))]


def _grad_w(name, a, dc, token=None, tm=512, tn=1024, cols=None):
    t, m = a.shape
    first, n = cols if cols is not None else (0, dc.shape[1])
    tm, tn = _tile(m, tm), _tile(n, tn)
    assert first % tn == 0

    def epilogue(acc, i, j, extra_refs, out_refs, rows):
        out_refs[0][...] = acc
        out_refs[1][...] = acc.astype(BF16)

    blk = ((tm, tn), lambda i, j: (i, j))
    return _mm(name, a, dc, _TN, m, n, t, tm, tn, t, _after(token) if token is not None else [],
               [(jax.ShapeDtypeStruct((m, n), F32),) + blk, (jax.ShapeDtypeStruct((m, n), BF16),) + blk], epilogue,
               b_col_block=first // tn)


def _cast_bf16(name, w):
    r, c = w.shape
    tr = _tile(r, 256)
    return pl.pallas_call(
        lambda w_ref, o_ref: o_ref.__setitem__(Ellipsis, w_ref[...].astype(BF16)), name=name,
        grid=(r // tr,), in_specs=[pl.BlockSpec((tr, c), lambda i: (i, 0))],
        out_specs=pl.BlockSpec((tr, c), lambda i: (i, 0)), out_shape=jax.ShapeDtypeStruct((r, c), BF16),
        compiler_params=_params(1),
    )(w)


def _prep_small(c_row, lb_logits):
    d = c_row.shape[1]
    rows = d // LANE

    def body(c_ref, l_ref, o_ref):
        cv = c_ref[...]
        o_ref[0:rows, :] = cv * _sigmoid(cv)
        lbs = [_sigmoid(l_ref[dr][0:1, :] - l_ref[dr][1:2, :]) for dr in range(2)]
        o_ref[rows:rows + 8, :] = jnp.concatenate(lbs + [jnp.zeros((6, LANE), F32)], axis=0)

    return pl.pallas_call(
        body, name="prep_small", out_shape=jax.ShapeDtypeStruct((rows + 8, LANE), F32),
    )(c_row.reshape(rows, LANE), lb_logits)


def _mod_shard(sc_all, w_ada_shard, b_shard):
    d, n = w_ada_shard.shape
    tn = _tile(n, 512)

    def body(s_ref, w_ref, b_ref, o_ref):
        o_ref[...] = _dot(s_ref[...], w_ref[...], precision=HIGHEST) + b_ref[...]

    return pl.pallas_call(
        body, name="mod_shard", grid=(n // tn,),
        in_specs=[pl.BlockSpec((N_DEV, d), lambda j: (0, 0)), pl.BlockSpec((d, tn), lambda j: (0, j)),
                  pl.BlockSpec((1, tn), lambda j: (0, j))],
        out_specs=pl.BlockSpec((N_DEV, tn), lambda j: (0, j)),
        out_shape=jax.ShapeDtypeStruct((N_DEV, n), F32), compiler_params=_params(1),
    )(sc_all, w_ada_shard, b_shard)


def _norm_mod(x, gain, shift, scale):
    t, d = x.shape
    tm = _tile(t, 512)

    def body(x_ref, g_ref, sh_ref, sc_ref, o_ref):
        xv = x_ref[...]
        o_ref[...] = ((xv * _rms(xv) * g_ref[...]) * (1.0 + sc_ref[...]) + sh_ref[...]).astype(BF16)

    vec = pl.BlockSpec((1, d), lambda i: (0, 0))
    return pl.pallas_call(
        body, name="norm_mod", grid=(t // tm,),
        in_specs=[pl.BlockSpec((tm, d), lambda i: (i, 0)), vec, vec, vec],
        out_specs=pl.BlockSpec((tm, d), lambda i: (i, 0)), out_shape=jax.ShapeDtypeStruct((t, d), BF16),
        compiler_params=_params(1),
    )(x, gain, shift, scale)


def _chunk_masks():
    row = lax.broadcasted_iota(jnp.int32, (HEAD, HEAD), 0)
    col = lax.broadcasted_iota(jnp.int32, (HEAD, HEAD), 1)
    same = (row // A_CHUNK) == (col // A_CHUNK)
    return same & (col <= row), same & (col >= row)


def _ones(mask):
    return jnp.where(mask, 1.0, 0.0).astype(BF16)


def _dot_split(ones_bf16, x):
    hi = x.astype(BF16)
    lo = (x - hi.astype(F32)).astype(BF16)
    return _dot(ones_bf16, hi) + _dot(ones_bf16, lo)


def _hgrn_block(direction, f, lb, cum2):
    sf = _sigmoid(f)
    big_f = lb + (1.0 - lb) * sf
    k = (1.0 - lb) * (1.0 - sf)
    lf = jnp.log(big_f)
    both = _dot_split(cum2, lf)
    cf, cr = both[:HEAD], both[HEAD:]
    b, rest = (cf, cr - lf) if direction == 0 else (cr, cf - lf)
    return k, sf, big_f, jnp.exp(b), jnp.exp(-b), jnp.exp(rest)


def _hgrn_fwd(proj, lb, g_norm, width, exchange):
    t = proj.shape[0]
    heads = width // HEAD
    nb, nc = t // HEAD, t // A_CHUNK
    ua = 4 if nb % 4 == 0 else (2 if nb % 2 == 0 else 1)
    ub = 16 if nc % 16 == 0 else (8 if nc % 8 == 0 else 4)
    q_scale = HEAD ** -0.5
    xin, xout = len(exchange.arrays), len(exchange.out_shapes)

    def body(q_ref, ffw_ref, fbw_ref, v_ref, og_ref, lb_ref, g_ref, *rest):
        xin_refs, rest = rest[:xin], rest[xin:]
        outa_ref, osum_ref = rest[:2]
        xout_refs, rest = rest[2:2 + xout], rest[2 + xout:]
        qd_s, ke_s, dc_s, o_s = rest[:4]
        sem_refs = rest[4:]
        h = pl.program_id(0)

        @pl.when(h == 0)
        def _():
            exchange.start(xin_refs, xout_refs, sem_refs)

        tril, triu = _chunk_masks()
        cum2 = jnp.concatenate([_ones(tril), _ones(triu)], axis=0)
        f_refs = (ffw_ref, fbw_ref)
        lbs = (lb_ref[0:1, :], lb_ref[1:2, :])

        def phase_a(it, carry):
            loaded = []
            for u in range(ua):
                rows = pl.ds(pl.multiple_of((it * ua + u) * HEAD, HEAD), HEAD)
                loaded.append((rows, q_ref[rows, :], v_ref[rows, :], ffw_ref[rows, :], fbw_ref[rows, :]))
            chains = [(d, rows, qv * q_scale, vv.astype(BF16), fv)
                      for rows, qv, vv, f0, f1 in loaded for d, fv in ((0, f0), (1, f1))]
            blocks = [_hgrn_block(d, fv, lbs[d], cum2) for d, _, _, _, fv in chains]
            scaled = [(qv * eb, k * enb, k * erest, eb * erest)
                      for (_, _, qv, _, _), (k, _, _, eb, enb, erest) in zip(chains, blocks)]
            atts = [jnp.where(tril if d == 0 else triu, _bdot(qd, kd, _NT), 0.0)
                    for (d, _, _, _, _), (qd, kd, _, _) in zip(chains, scaled)]
            intras = [_bdot(att, vv) for att, (_, _, _, vv, _) in zip(atts, chains)]
            results = [(d, rows, o_intra, qd.astype(BF16), ke.astype(BF16), decay)
                       for (d, rows, _, _, _), (qd, _, ke, decay), o_intra in zip(chains, scaled, intras)]
            for d, rows, o_intra, qd16, ke16, decay in results:
                o_s[d, rows, :] = o_intra
                qd_s[d, rows, :] = qd16
                ke_s[d, rows, :] = ke16
                dc_s[d, rows, :] = decay
            return carry

        lax.fori_loop(0, nb // ua, phase_a, 0)

        def phase_b(it, states):
            loaded = []
            for u in range(ub):
                n = it * ub + u
                for d in range(2):
                    c = n if d == 0 else nc - 1 - n
                    start = pl.multiple_of(c * A_CHUNK, A_CHUNK)
                    rows = pl.ds(start, A_CHUNK)
                    loaded.append((d, rows, qd_s[d, rows, :], ke_s[d, rows, :], v_ref[rows, :],
                                   dc_s[d, pl.ds(start, 1), :], o_s[d, rows, :]))
            increments = [_dot(vv.astype(BF16), ke16, _TN) for _, _, _, ke16, vv, _, _ in loaded]
            states = list(states)
            befores = []
            for (d, _, _, _, _, decay, _), inc in zip(loaded, increments):
                befores.append(states[d].astype(BF16))
                states[d] = states[d] * decay + inc
            inters = [_dot(qd16, before, _NT) for (_, _, qd16, _, _, _, _), before in zip(loaded, befores)]
            for (d, rows, _, _, _, _, o_intra), o_inter in zip(loaded, inters):
                o_s[d, rows, :] = o_intra + o_inter
            return tuple(states)

        zero_state = jnp.zeros((HEAD, HEAD), F32)
        lax.fori_loop(0, nc // ub, phase_b, (zero_state, zero_state))

        def phase_c(i, carry):
            rows = pl.ds(pl.multiple_of(i * HEAD, HEAD), HEAD)
            o = o_s[0, rows, :] + o_s[1, rows, :]
            osum_ref[rows, :] = o
            og = og_ref[rows, :]
            outa_ref[rows, :] = (o * _rms(o) * g_ref[...] * (og * _sigmoid(og))).astype(BF16)
            return carry

        lax.fori_loop(0, nb, phase_c, 0)

        @pl.when(h == heads - 1)
        def _():
            exchange.finish(xin_refs, xout_refs, sem_refs)

    def col(p):
        return pl.BlockSpec((t, HEAD), lambda h: (0, p * heads + h))

    any_spec = pl.BlockSpec(memory_space=pl.ANY)
    results = pl.pallas_call(
        body, name="hgrn_fwd", grid=(heads,),
        in_specs=[col(0), col(1), col(2), col(3), col(4),
                  pl.BlockSpec((2, HEAD), lambda h: (0, h)), pl.BlockSpec((1, HEAD), lambda h: (0, 0))] + [any_spec] * xin,
        out_specs=[pl.BlockSpec((t, HEAD), lambda h: (0, h)), pl.BlockSpec((t, HEAD), lambda h: (0, h))] + [any_spec] * xout,
        out_shape=[jax.ShapeDtypeStruct((t, width), BF16), jax.ShapeDtypeStruct((t, width), F32)] + list(exchange.out_shapes),
        scratch_shapes=[pltpu.VMEM((2, t, HEAD), BF16), pltpu.VMEM((2, t, HEAD), BF16), pltpu.VMEM((2, t, HEAD), F32),
                        pltpu.VMEM((2, t, HEAD), F32)] + list(exchange.sems),
        compiler_params=_params(1),
    )(proj, proj, proj, proj, proj, lb, g_norm, *exchange.arrays)
    return results[0], results[1], results[2:]


def _sgu_core(u_pre, v_pre, g_v, ws_ref, bst):
    u, du = _gelu_and_grad(u_pre)
    v, dv = _gelu_and_grad(v_pre)
    mu = jnp.mean(v, axis=-1, keepdims=True)
    dlt = v - mu
    rstd = lax.rsqrt(jnp.mean(dlt * dlt, axis=-1, keepdims=True) + EPS)
    vhat = dlt * rstd
    vn = vhat * g_v
    groups = vn.shape[1] // HEAD
    cols = []
    for g in range(groups):
        vm_g = _bdot(ws_ref[g], vn[:, g * HEAD:(g + 1) * HEAD]) + bst[:, g:g + 1]
        cols.append(vm_g)
    return u, du, dv, vhat, rstd, vn, jnp.concatenate(cols, axis=1)


def _sgu_fwd(proj, g_v, w_s, bst, width, z_block):
    t = proj.shape[0]

    def body(u_ref, v_ref, g_ref, ws_ref, bst_ref, o_ref):
        u, _, _, _, _, _, vm = _sgu_core(u_ref[...], v_ref[...], g_ref[...], ws_ref, bst_ref[...])
        o_ref[...] = (u * vm).astype(BF16)

    groups = width // HEAD
    return pl.pallas_call(
        body, name="sgu_fwd", grid=(t // HEAD,),
        in_specs=[pl.BlockSpec((HEAD, width), lambda i: (i, z_block)), pl.BlockSpec((HEAD, width), lambda i: (i, z_block + 1)),
                  pl.BlockSpec((1, width), lambda i: (0, 0)), pl.BlockSpec((groups, HEAD, HEAD), lambda i: (0, 0, 0)),
                  pl.BlockSpec((HEAD, groups), lambda i: (0, 0))],
        out_specs=pl.BlockSpec((HEAD, width), lambda i: (i, 0)),
        out_shape=jax.ShapeDtypeStruct((t, width), BF16), compiler_params=_params(1),
    )(proj, proj, g_v, w_s, bst)


def _sgu_bwd(proj, dout_b, dproj, g_v, w_s, w_st, bst, width, z_block):
    t = proj.shape[0]
    groups = width // HEAD
    nblk = t // HEAD

    def body(u_ref, v_ref, do_ref, g_ref, ws_ref, wst_ref, bst_ref, dproj_hbm,
             dz_ref, dg_ref, dws_ref, dbst_ref, res_s):
        i, p = pl.program_id(0), pl.program_id(1)

        @pl.when((i == 0) & (p == 0))
        def _():
            dg_ref[...] = jnp.zeros_like(dg_ref)
            dws_ref[...] = jnp.zeros_like(dws_ref)
            dbst_ref[...] = jnp.zeros_like(dbst_ref)

        @pl.when(p == 0)
        def _():
            g_v = g_ref[...]
            u, du, dv, vhat, rstd, vn, vm = _sgu_core(u_ref[...], v_ref[...], g_v, ws_ref, bst_ref[...])
            dout = do_ref[...].astype(F32)
            res_s[0] = (dout * vm * du).astype(BF16)
            dvm = dout * u
            dvn_cols = []
            for g in range(groups):
                sl = slice(g * HEAD, (g + 1) * HEAD)
                dvm_g = dvm[:, sl]
                dbst_ref[:, g:g + 1] += jnp.sum(dvm_g, axis=1, keepdims=True)
                dws_ref[g] += _bdot(dvm_g, vn[:, sl], _NT)
                dvn_cols.append(_bdot(wst_ref[g], dvm_g))
            dvn = jnp.concatenate(dvn_cols, axis=1)
            dg_ref[...] += _colsum(dvn * vhat)
            dvh = dvn * g_v
            dvg = rstd * (dvh - jnp.mean(dvh, axis=-1, keepdims=True)
                          - vhat * jnp.mean(dvh * vhat, axis=-1, keepdims=True))
            res_s[1] = (dvg * dv).astype(BF16)

        dz_ref[...] = res_s[p]

    n_in = dproj.shape[1]
    return pl.pallas_call(
        body, name="sgu_bwd", grid=(nblk, 2),
        in_specs=[pl.BlockSpec((HEAD, width), lambda i, p: (i, z_block)),
                  pl.BlockSpec((HEAD, width), lambda i, p: (i, z_block + 1)),
                  pl.BlockSpec((HEAD, width), lambda i, p: (i, 0)),
                  pl.BlockSpec((1, width), lambda i, p: (0, 0)),
                  pl.BlockSpec((groups, HEAD, HEAD), lambda i, p: (0, 0, 0)),
                  pl.BlockSpec((groups, HEAD, HEAD), lambda i, p: (0, 0, 0)),
                  pl.BlockSpec((HEAD, groups), lambda i, p: (0, 0)),
                  pl.BlockSpec(memory_space=pl.ANY)],
        out_specs=[pl.BlockSpec((HEAD, width), lambda i, p: (i, z_block + p)),
                   pl.BlockSpec((1, width), lambda i, p: (0, 0)),
                   pl.BlockSpec((groups, HEAD, HEAD), lambda i, p: (0, 0, 0)),
                   pl.BlockSpec((HEAD, groups), lambda i, p: (0, 0))],
        out_shape=[jax.ShapeDtypeStruct((t, n_in), BF16), jax.ShapeDtypeStruct((1, width), F32),
                   jax.ShapeDtypeStruct((groups, HEAD, HEAD), F32), jax.ShapeDtypeStruct((HEAD, groups), F32)],
        scratch_shapes=[pltpu.VMEM((2, HEAD, width), BF16)],
        input_output_aliases={7: 0},
        compiler_params=_params(2),
    )(proj, proj, dout_b, g_v, w_s, w_st, bst, dproj)


def _hgrn_bwd(proj, osum, dout_a, dproj, lb, g_norm, width, exchange):
    t = proj.shape[0]
    heads = width // HEAD
    nb = t // HEAD
    cpb = HEAD // A_CHUNK
    ubk = 2 if nb % 2 == 0 else 1
    q_scale = HEAD ** -0.5
    xin, xout = len(exchange.arrays), len(exchange.out_shapes)

    def body(q_ref, ffw_ref, fbw_ref, v_ref, og_ref, osum_ref, douta_ref, lb_ref, g_ref, dproj_hbm, *rest):
        xin_refs, rest = rest[:xin], rest[xin:]
        out_ref, dgh_ref, dlb_ref = rest[:3]
        xout_refs, rest = rest[3:3 + xout], rest[3 + xout:]
        do_s, dq_s, dv_s, res_s, ck_s = rest[:5]
        sem_refs = rest[5:]
        h, p = pl.program_id(0), pl.program_id(1)
        f_refs = (ffw_ref, fbw_ref)

        @pl.when((h == 0) & (p == 0))
        def _():
            exchange.start(xin_refs, xout_refs, sem_refs)

        @pl.when(p == 0)
        def _():
            tril, triu = _chunk_masks()
            cum2 = jnp.concatenate([_ones(tril), _ones(triu)], axis=0)
            g_row = g_ref[...]

            def pass_norm(i, dgh):
                rows = pl.ds(pl.multiple_of(i * HEAD, HEAD), HEAD)
                o = osum_ref[rows, :]
                r = _rms(o)
                oh = o * r
                og = og_ref[rows, :]
                sg = _sigmoid(og)
                dout = douta_ref[rows, :].astype(F32)
                don = dout * (og * sg)
                res_s[4, rows, :] = (dout * (oh * g_row) * (sg * (1.0 + og * (1.0 - sg)))).astype(BF16)
                doh = don * g_row
                do_s[rows, :] = r * (doh - oh * jnp.mean(doh * oh, axis=-1, keepdims=True))
                return dgh + _colsum(don * oh)

            dgh_ref[...] = lax.fori_loop(0, nb, pass_norm, jnp.zeros((1, HEAD), F32))

            lbs = (lb_ref[0:1, :], lb_ref[1:2, :])
            zero_state = jnp.zeros((HEAD, HEAD), F32)

            def chunk_order(d):
                return list(range(cpb)) if d == 0 else list(range(cpb - 1, -1, -1))

            def chunk(x, j):
                return x[j * A_CHUNK:(j + 1) * A_CHUNK, :]

            def decay_row(e_big, j):
                return e_big[j * A_CHUNK:j * A_CHUNK + 1, :]

            def cat(parts):
                return jnp.concatenate([parts[j] for j in range(cpb)], axis=0)

            def block_states(d, start, incs, e_big):
                befores, st = {}, start
                for j in chunk_order(d):
                    befores[j] = st
                    st = st * decay_row(e_big, j) + incs[j]
                return befores, st

            def pass_states(it, states):
                loaded = []
                for u in range(ubk):
                    for d in range(2):
                        blk = it * ubk + u if d == 0 else nb - 1 - (it * ubk + u)
                        rows = pl.ds(pl.multiple_of(blk * HEAD, HEAD), HEAD)
                        loaded.append((d, blk, f_refs[d][rows, :], v_ref[rows, :]))
                blocks = [_hgrn_block(d, fv, lbs[d], cum2) for d, _, fv, _ in loaded]
                incs = [{j: _bdot(chunk(vv, j), chunk(k * erest, j), _TN) for j in range(cpb)}
                        for (_, _, _, vv), (k, _, _, _, _, erest) in zip(loaded, blocks)]
                states, starts = list(states), []
                for (d, _, _, _), (_, _, _, eb, _, erest), inc in zip(loaded, blocks, incs):
                    starts.append(states[d])
                    states[d] = block_states(d, states[d], inc, eb * erest)[1]
                for (d, blk, _, _), start in zip(loaded, starts):
                    ck_s[d, blk] = start
                return tuple(states)

            lax.fori_loop(0, nb // ubk, pass_states, (zero_state, zero_state))

            def pass_back(it, carry):
                gts, dlb = [carry[0], carry[1]], carry[2]
                loaded = []
                for u, d in ((u, d) for u in range(ubk) for d in range(2)):
                    blk = nb - 1 - (it * ubk + u) if d == 0 else it * ubk + u
                    rows = pl.ds(pl.multiple_of(blk * HEAD, HEAD), HEAD)
                    loaded.append((d, rows, f_refs[d][rows, :], q_ref[rows, :], v_ref[rows, :], do_s[rows, :], ck_s[d, blk]))
                blocks = [_hgrn_block(d, fv, lbs[d], cum2) for d, _, fv, _, _, _, _ in loaded]
                scaled = []
                for (_, _, _, qv, _, _, _), (k, _, _, eb, enb, erest) in zip(loaded, blocks):
                    qh = qv * q_scale
                    scaled.append((qh, qh * eb, k * enb, k * erest, eb * erest))
                masks = [tril if d == 0 else triu for d, *_ in loaded]
                atts = [jnp.where(m, _bdot(qd, kd, _NT), 0.0) for m, (_, qd, kd, _, _) in zip(masks, scaled)]
                datts = [jnp.where(m, _bdot(do, vv, _NT), 0.0) for m, (_, _, _, _, vv, do, _) in zip(masks, loaded)]
                dvs = [_bdot(att, do, _TN) for att, (_, _, _, _, _, do, _) in zip(atts, loaded)]
                dqds = [_bdot(datt, kd) for datt, (_, _, kd, _, _) in zip(datts, scaled)]
                dkds = [_bdot(datt, qd, _TN) for datt, (_, qd, _, _, _) in zip(datts, scaled)]
                s_incs = [{j: _bdot(chunk(vv, j), chunk(ke, j), _TN) for j in range(cpb)}
                          for (_, _, _, _, vv, _, _), (_, _, _, ke, _) in zip(loaded, scaled)]
                g_incs = [{j: _bdot(chunk(do, j), chunk(qd, j), _TN) for j in range(cpb)}
                          for (_, _, _, _, _, do, _), (_, qd, _, _, _) in zip(loaded, scaled)]
                befores, afters, g_at = [], [], []
                for (d, _, _, _, _, _, ck), (_, _, _, _, e_big), s_inc, g_inc in zip(loaded, scaled, s_incs, g_incs):
                    order = chunk_order(d)
                    before, after = block_states(d, ck, s_inc, e_big)
                    befores.append(before)
                    afters.append({j: (before[order[n + 1]] if n + 1 < cpb else after) for n, j in enumerate(order)})
                    at, gt = {}, gts[d]
                    for j in reversed(order):
                        at[j] = gt
                        gt = gt * decay_row(e_big, j) + g_inc[j]
                    gts[d] = gt
                    g_at.append(at)
                dqd_i = [{j: _bdot(chunk(do, j), before[j]) for j in range(cpb)}
                         for (_, _, _, _, _, do, _), before in zip(loaded, befores)]
                dv_i = [{j: _bdot(chunk(ke, j), at[j], _NT) for j in range(cpb)}
                        for (_, _, _, ke, _), at in zip(scaled, g_at)]
                dke = [{j: _bdot(chunk(vv, j), at[j]) for j in range(cpb)}
                       for (_, _, _, _, vv, _, _), at in zip(loaded, g_at)]
                results, new = [], []
                for n, ((d, rows, _, _, _, _, _), (k, sf, big_f, eb, enb, erest), (qh, _, _, _, _)) in enumerate(
                        zip(loaded, blocks, scaled)):
                    dqh = (dqds[n] + cat(dqd_i[n])) * eb
                    dk = dkds[n] * enb + cat(dke[n]) * erest
                    carry_rows = {j: jnp.broadcast_to(_colsum(g_at[n][j] * afters[n][j]), (A_CHUNK, HEAD))
                                  for j in range(cpb)}
                    dlf = _dot_split(_ones(triu if d == 0 else tril), qh * dqh - k * dk) + cat(carry_rows)
                    common = dlf / big_f - dk
                    results.append((d, rows, (k * sf * common).astype(BF16), dqh.astype(BF16),
                                    (dvs[n] + cat(dv_i[n])).astype(BF16)))
                    new.append(_colsum((1.0 - sf) * common))
                for d, rows, df16, dq16, dv16 in results:
                    res_s[1 + d, rows, :] = df16
                    dq_s[d, rows, :] = dq16
                    dv_s[d, rows, :] = dv16
                per_dir = [sum(c for (d, *_), c in zip(loaded, new) if d == dd) for dd in range(2)]
                return gts[0], gts[1], dlb + jnp.concatenate(per_dir, axis=0)

            dlb_ref[...] = lax.fori_loop(0, nb // ubk, pass_back,
                                         (zero_state, zero_state, jnp.zeros((2, HEAD), F32)))[2]

            def pass_out(i, carry):
                rows = pl.ds(pl.multiple_of(i * HEAD, HEAD), HEAD)
                dq = dq_s[0, rows, :].astype(F32) + dq_s[1, rows, :].astype(F32)
                res_s[0, rows, :] = (dq * q_scale).astype(BF16)
                res_s[3, rows, :] = (dv_s[0, rows, :].astype(F32) + dv_s[1, rows, :].astype(F32)).astype(BF16)
                return carry

            lax.fori_loop(0, nb, pass_out, 0)

        out_ref[...] = res_s[p]

        @pl.when((h == heads - 1) & (p == 4))
        def _():
            exchange.finish(xin_refs, xout_refs, sem_refs)

    def col(pp):
        return pl.BlockSpec((t, HEAD), lambda h, p: (0, pp * heads + h))

    n_in = dproj.shape[1]
    any_spec = pl.BlockSpec(memory_space=pl.ANY)
    results = pl.pallas_call(
        body, name="hgrn_bwd", grid=(heads, 5),
        in_specs=[col(0), col(1), col(2), col(3), col(4),
                  pl.BlockSpec((t, HEAD), lambda h, p: (0, h)), pl.BlockSpec((t, HEAD), lambda h, p: (0, h)),
                  pl.BlockSpec((2, HEAD), lambda h, p: (0, h)), pl.BlockSpec((1, HEAD), lambda h, p: (0, 0)),
                  any_spec] + [any_spec] * xin,
        out_specs=[pl.BlockSpec((t, HEAD), lambda h, p: (0, p * heads + h)),
                   pl.BlockSpec((None, 1, HEAD), lambda h, p: (h, 0, 0)),
                   pl.BlockSpec((2, HEAD), lambda h, p: (0, h))] + [any_spec] * xout,
        out_shape=[jax.ShapeDtypeStruct((t, n_in), BF16), jax.ShapeDtypeStruct((heads, 1, HEAD), F32),
                   jax.ShapeDtypeStruct((2, width), F32)] + list(exchange.out_shapes),
        scratch_shapes=[pltpu.VMEM((t, HEAD), F32), pltpu.VMEM((2, t, HEAD), BF16), pltpu.VMEM((2, t, HEAD), BF16),
                        pltpu.VMEM((5, t, HEAD), BF16), pltpu.VMEM((2, nb, HEAD, HEAD), F32)] + list(exchange.sems),
        input_output_aliases={9: 0},
        compiler_params=_params(2),
    )(proj, proj, proj, proj, proj, osum, dout_a, lb, g_norm, dproj, *exchange.arrays)
    return results[0], results[1], results[2], results[3:]


def _adamw(w, g, m, v):
    m = ADAM_B1 * m + (1.0 - ADAM_B1) * g
    v = ADAM_B2 * v + (1.0 - ADAM_B2) * (g * g)
    m_hat = m / (1.0 - ADAM_B1 ** ADAM_STEP)
    v_hat = v / (1.0 - ADAM_B2 ** ADAM_STEP)
    delta = -ADAM_LR * (m_hat / (jnp.sqrt(v_hat) + ADAM_EPS) + ADAM_WD * w)
    return delta, m, v


def _adamw_big(name, me, w, m, v, g_parts, landing, axis):
    r, c = w.shape
    tr = _tile(r, 128)
    n_parts = len(g_parts)
    per = N_DEV // n_parts

    def body(me_ref, w_ref, m_ref, v_ref, *rest):
        g_refs, (l_ref, og_ref, od_ref, om_ref, ov_ref) = rest[:n_parts], rest[n_parts:]
        g = g_refs[0][...]
        for p in range(1, n_parts):
            g = jnp.where(me_ref[0] // per == p, g_refs[p][...], g)
        for s in range(N_DEV - 1):
            g = g + l_ref[s].astype(F32)
        og_ref[...] = g
        od_ref[...], om_ref[...], ov_ref[...] = _adamw(w_ref[...], g, m_ref[...], v_ref[...])

    shard = pl.BlockSpec((tr, c), lambda i, me_ref: (i, 0))
    if axis == 1:
        own = pl.BlockSpec((tr, c), lambda i, me_ref: (i, me_ref[0] % per))
    else:
        assert n_parts == 1
        own = pl.BlockSpec((tr, c), lambda i, me_ref: (me_ref[0] * (r // tr) + i, 0))
    grid_spec = pltpu.PrefetchScalarGridSpec(
        num_scalar_prefetch=1, grid=(r // tr,),
        in_specs=[shard, shard, shard] + [own] * n_parts + [pl.BlockSpec((N_DEV - 1, tr, c), lambda i, me_ref: (0, i, 0))],
        out_specs=[shard] * 4)
    return pl.pallas_call(
        body, name=name, grid_spec=grid_spec, out_shape=[jax.ShapeDtypeStruct((r, c), F32)] * 4,
        compiler_params=_params(1),
    )(me, w, m, v, *g_parts, landing)


def _adamw_ada(sct, dmod_mine, w, m, v):
    d, n = w.shape
    tr = _tile(d, 256)

    def body(s_ref, dm_ref, w_ref, m_ref, v_ref, og_ref, od_ref, om_ref, ov_ref):
        g = _dot(s_ref[...], dm_ref[...], precision=HIGHEST)
        og_ref[...] = g
        od_ref[...], om_ref[...], ov_ref[...] = _adamw(w_ref[...], g, m_ref[...], v_ref[...])

    blk = pl.BlockSpec((tr, n), lambda i: (i, 0))
    return pl.pallas_call(
        body, name="adamw_ada", grid=(d // tr,),
        in_specs=[pl.BlockSpec((tr, N_DEV), lambda i: (i, 0)), pl.BlockSpec((N_DEV, n), lambda i: (0, 0)), blk, blk, blk],
        out_specs=[blk] * 4, out_shape=[jax.ShapeDtypeStruct((d, n), F32)] * 4, compiler_params=_params(1),
    )(sct, dmod_mine, w, m, v)


def _adamw_small(gathered, w, m, v):
    def body(g_ref, w_ref, m_ref, v_ref, og_ref, od_ref, om_ref, ov_ref):
        g = g_ref[0]
        for s in range(1, N_DEV):
            g = g + g_ref[s]
        og_ref[...] = g
        od_ref[...], om_ref[...], ov_ref[...] = _adamw(w_ref[...], g, m_ref[...], v_ref[...])

    return pl.pallas_call(
        body, name="adamw_small", out_shape=[jax.ShapeDtypeStruct(w.shape, F32)] * 4,
        compiler_params=pltpu.CompilerParams(vmem_limit_bytes=VMEM_LIMIT),
    )(gathered, w, m, v)


def _adamw_lb(dlb_mine, lb_logits, m, v):
    def body(d_ref, l_ref, m_ref, v_ref, og_ref, od_ref, om_ref, ov_ref):
        dlb = d_ref[0]
        for s in range(1, N_DEV):
            dlb = dlb + d_ref[s]
        for dr in range(2):
            lb = _sigmoid(l_ref[dr][0:1, :] - l_ref[dr][1:2, :])
            d0 = dlb[dr:dr + 1] * lb * (1.0 - lb)
            g = jnp.concatenate([d0, -d0], axis=0)
            og_ref[dr] = g
            od_ref[dr], om_ref[dr], ov_ref[dr] = _adamw(l_ref[dr], g, m_ref[dr], v_ref[dr])

    return pl.pallas_call(body, name="adamw_lb", out_shape=[jax.ShapeDtypeStruct(lb_logits.shape, F32)] * 4,
                          )(dlb_mine, lb_logits, m, v)


def _rows(a, pad_to=8):
    flat = a.reshape(-1, LANE)
    pad = (-flat.shape[0]) % pad_to
    return jnp.pad(flat, ((0, pad), (0, 0))) if pad else flat


def kernel(x, c, w_ada, b_ada, g_pre_mix, g_post_mix, g_pre_ffn, g_post_ffn, w_in, lb_logits, g_hgrn_norm, w_a_out, g_sgu_norm, w_spatial, b_spatial, w_b_out, w_o, w_ff1, w_ff2, loss_target, m_w_ada, m_b_ada, m_g_pre_mix, m_g_post_mix, m_g_pre_ffn, m_g_post_ffn, m_w_in, m_lb_logits, m_g_hgrn_norm, m_w_a_out, m_g_sgu_norm, m_w_spatial, m_b_spatial, m_w_b_out, m_w_o, m_w_ff1, m_w_ff2, v_w_ada, v_b_ada, v_g_pre_mix, v_g_post_mix, v_g_pre_ffn, v_g_post_ffn, v_w_in, v_lb_logits, v_g_hgrn_norm, v_w_a_out, v_g_sgu_norm, v_w_spatial, v_b_spatial, v_w_b_out, v_w_o, v_w_ff1, v_w_ff2):
    t, d = x.shape[1], x.shape[2]
    n_in = w_in.shape[2] * N_DEV
    width = (n_in - 2 * d) // 7
    heads = width // HEAD
    assert heads == N_DEV and width % LANE == 0
    d_ff = w_ff1.shape[2] * N_DEV
    n_ada = w_ada.shape[2]
    me = _dev_index()
    me_arr = me.reshape(1).astype(jnp.int32)
    x2, tgt = x[0], loss_target[0]

    big = [w_in[0], w_a_out[0], w_b_out[0], w_o[0], w_ff1[0], w_ff2[0]]
    big_axes = [1, 1, 1, 0, 1, 0]
    big_names = ["w_in", "w_a_out", "w_b_out", "w_o", "w_ff1", "w_ff2"]
    wf_in, = _run_exchange("gather_w_in", _gather_plan([_cast_bf16("cast_w_in", big[0])], big_axes[:1]))
    own_parts = [_cast_into_full("cast_" + nm, me_arr, w, ax) for nm, w, ax in zip(big_names[1:], big[1:], big_axes[1:])]
    wf_in, own_parts = lax.optimization_barrier((wf_in, own_parts))
    gathers = {}
    for key, lo, hi in (("mid", 1, 4), ("ff1", 4, 5), ("ff2", 5, 6)):
        plan = _direct_gather_plan(own_parts[lo - 1:hi - 1], big_axes[lo:hi])
        gathers[key] = (plan,) + _split_start("gather_%s_start" % key, plan, landing=own_parts[lo - 1:hi - 1])

    def gathered_weights(key, after):
        plan, sems, thru, _ = gathers[key]
        return _split_wait("gather_%s_wait" % key, plan, sems, thru, after)[1]

    c_rows = d // LANE
    small = _all_gather_small("gather_c_lb", _prep_small(c[0:1], lb_logits))
    sc_all = small[:, :c_rows, :].reshape(N_DEV, d)
    lb = jnp.transpose(small[:, c_rows:c_rows + 2, :], (1, 0, 2)).reshape(2, width)
    b_shard = lax.dynamic_slice_in_dim(b_ada, me * n_ada, n_ada, axis=1)
    mod_sh = _mod_shard(sc_all, w_ada[0], b_shard)
    mod_all = _all_gather_small("gather_mod", _rows(mod_sh))
    mod_all = mod_all[:, :N_DEV * n_ada // LANE, :].reshape(N_DEV, N_DEV, n_ada)
    mod6 = lax.dynamic_index_in_dim(mod_all, me, axis=1, keepdims=False).reshape(N_MOD, d)
    sh1, sc1, gt1, sh2, sc2, gt2 = [mod6[i:i + 1] for i in range(N_MOD)]

    a1 = _norm_mod(x2, g_pre_mix, sh1, sc1)
    tm = _tile(t, 512)

    def store_f32(acc, i, j, extra_refs, out_refs, rows):
        out_refs[0][...] = acc

    tn_in = _tile(n_in, 1024)
    started = [tok for key in ("mid", "ff1", "ff2") for tok in _after(gathers[key][3])]
    proj, = _mm("proj", a1, wf_in, _NN, t, n_in, d, tm, tn_in, d, started,
                [(jax.ShapeDtypeStruct((t, n_in), F32), (tm, tn_in), lambda i, j: (i, j))], store_f32)

    out_a, osum, _ = _hgrn_fwd(proj, lb, g_hgrn_norm, width, _NO_EXCHANGE)
    z_block = 5
    bst = b_spatial[0].T
    out_b = _sgu_fwd(proj, g_sgu_norm, w_spatial[0], bst, width, z_block)
    wf_a, wf_b, wf_o = gathered_weights("mid", out_b)

    tn_d = _tile(d, 512)
    blk_d = ((tm, tn_d), lambda i, j: (i, j))
    y_a, = _mm("y_a", out_a, wf_a, _NN, t, d, width, tm, tn_d, width, [],
               [(jax.ShapeDtypeStruct((t, d), F32),) + blk_d], store_f32)
    ga_blk = (5 * width + 2 * width) // tn_d
    gb_blk = ga_blk + d // tn_d

    def merge(acc, i, j, extra_refs, out_refs, rows):
        ga, gb, ya = extra_refs
        out_refs[0][...] = acc
        out_refs[1][...] = (_sigmoid(ga[...]) * ya[...] + _sigmoid(gb[...]) * acc).astype(BF16)

    y_b, merged = _mm("y_b_merge", out_b, wf_b, _NN, t, d, width, tm, tn_d, width,
                      [(proj, (tm, tn_d), lambda i, j: (i, ga_blk + j)), (proj, (tm, tn_d), lambda i, j: (i, gb_blk + j)),
                       (y_a,) + blk_d],
                      [(jax.ShapeDtypeStruct((t, d), F32),) + blk_d, (jax.ShapeDtypeStruct((t, d), BF16),) + blk_d], merge)

    tr = _tile(t, 512)
    rc = 32 if tr % 32 == 0 else None
    row_d = ((tr, d), lambda i, j: (i, 0))
    vec_d = ((1, d), lambda i, j: (0, 0))

    def post_mix(acc, i, j, extra_refs, out_refs, rows):
        x_r, gt1_r, g2_r, g3_r, sc2_r, sh2_r = extra_refs
        h1 = x_r[rows, :] + gt1_r[...] * (acc * _rms(acc) * g2_r[...])
        out_refs[0][rows, :] = acc
        out_refs[1][rows, :] = h1
        out_refs[2][rows, :] = ((h1 * _rms(h1) * g3_r[...]) * (1.0 + sc2_r[...]) + sh2_r[...]).astype(BF16)

    mo, h1, a2 = _mm("w_o_post_mix", merged, wf_o, _NN, t, d, d, tr, d, d,
                     [(x2,) + row_d, (gt1,) + vec_d, (g_post_mix,) + vec_d, (g_pre_ffn,) + vec_d, (sc2,) + vec_d, (sh2,) + vec_d],
                     [(jax.ShapeDtypeStruct((t, d), F32),) + row_d, (jax.ShapeDtypeStruct((t, d), F32),) + row_d,
                      (jax.ShapeDtypeStruct((t, d), BF16),) + row_d], post_mix, row_chunk=rc)

    tn_f = _tile(d_ff, 1024)
    blk_f = ((tm, tn_f), lambda i, j: (i, j))

    def relu_sq(acc, i, j, extra_refs, out_refs, rows):
        r = jnp.maximum(acc, 0.0)
        out_refs[0][...] = acc.astype(BF16)
        out_refs[1][...] = (r * r).astype(BF16)

    wf_1, = gathered_weights("ff1", a2)
    hff, act = _mm(
        "ff1", a2, wf_1, _NN, t, d_ff, d, tm, tn_f, d, [],
        [(jax.ShapeDtypeStruct((t, d_ff), BF16),) + blk_f, (jax.ShapeDtypeStruct((t, d_ff), BF16),) + blk_f], relu_sq)
    wf_2, = gathered_weights("ff2", act)

    sums_d = ((8, d), lambda i, j: (0, 0))

    def zero_first(sums_r, i, rows):
        if rows.start in (None, 0):
            @pl.when(i == 0)
            def _():
                sums_r[...] = jnp.zeros_like(sums_r)

    def loss_head(acc, i, j, extra_refs, out_refs, rows):
        h1_r, tgt_r, gt2_r, g4_r = extra_refs
        dy_r, dff_r, sums_r = out_refs
        r4 = _rms(acc)
        ffn = acc * r4
        n4 = ffn * g4_r[...]
        err = h1_r[rows, :] + gt2_r[...] * n4 - tgt_r[rows, :]
        dy = err * (1.0 / d)
        dy_r[rows, :] = dy
        dn4 = dy * gt2_r[...]
        dffn = dn4 * g4_r[...]
        dff_r[rows, :] = (r4 * (dffn - ffn * jnp.mean(dffn * ffn, axis=-1, keepdims=True))).astype(BF16)
        zero_first(sums_r, i, rows)

        sums_r[0:1, :] += _colsum(err * err)
        sums_r[1:2, :] += _colsum(dy * n4)
        sums_r[2:3, :] += _colsum(dn4 * ffn)

    tk_f = _tile(d_ff, 1024)
    dy, dff, sums_f = _mm("ff2_loss", act, wf_2, _NN, t, d, d_ff, tr, d, tk_f,
                          [(h1,) + row_d, (tgt,) + row_d, (gt2,) + vec_d, (g_post_ffn,) + vec_d],
                          [(jax.ShapeDtypeStruct((t, d), F32),) + row_d, (jax.ShapeDtypeStruct((t, d), BF16),) + row_d,
                           (jax.ShapeDtypeStruct((8, d), F32),) + sums_d], loss_head, row_chunk=rc)
    loss_mine = (0.5 / d) * jnp.sum(sums_f[0])

    def relu_sq_bwd(acc, i, j, extra_refs, out_refs, rows):
        out_refs[0][...] = (acc * (2.0 * jnp.maximum(extra_refs[0][...].astype(F32), 0.0))).astype(BF16)

    dhff, = _mm("d_hff", dff, wf_2, _NT, t, d_ff, d, tm, tn_f, d, [(hff,) + blk_f],
                [(jax.ShapeDtypeStruct((t, d_ff), BF16),) + blk_f], relu_sq_bwd)
    scatters = {}

    def send_grads(key, grads16, axes):
        plan = _scatter_plan(grads16, axes)
        scatters[key] = (plan,) + _split_start("scatter_%s_start" % key, plan)
        return scatters[key][3]

    def received_grads(key, after):
        plan, sems, thru, _ = scatters[key]
        return _split_wait("scatter_%s_wait" % key, plan, sems, thru, after)[1]

    gw_ff2, gw_ff2_16 = _grad_w("grad_w_ff2", act, dff)
    sent_ff2 = send_grads("ff2", [gw_ff2_16], big_axes[5:6])
    gw_ff1, gw_ff1_16 = _grad_w("grad_w_ff1", a2, dhff, token=sent_ff2)
    sent_ff1 = send_grads("ff1", [gw_ff1_16], big_axes[4:5])

    def pre_ffn_bwd(acc, i, j, extra_refs, out_refs, rows):
        h1_r, dy_r, mo_r, sc2_r, g3_r, gt1_r, g2_r = extra_refs[:7]
        dh1_r, dmo_r, sums_r = out_refs
        h1v = h1_r[rows, :]
        r3 = _rms(h1v)
        h1n = h1v * r3
        dn3 = acc * (1.0 + sc2_r[...])
        dh1n = dn3 * g3_r[...]
        dh1 = dy_r[rows, :] + r3 * (dh1n - h1n * jnp.mean(dh1n * h1n, axis=-1, keepdims=True))
        dh1_r[rows, :] = dh1
        mov = mo_r[rows, :]
        r2 = _rms(mov)
        mon = mov * r2
        dn2 = dh1 * gt1_r[...]
        dmon = dn2 * g2_r[...]
        dmo_r[rows, :] = (r2 * (dmon - mon * jnp.mean(dmon * mon, axis=-1, keepdims=True))).astype(BF16)
        zero_first(sums_r, i, rows)

        sums_r[0:1, :] += _colsum(acc)
        sums_r[1:2, :] += _colsum(acc * (h1n * g3_r[...]))
        sums_r[2:3, :] += _colsum(dn3 * h1n)
        sums_r[3:4, :] += _colsum(dh1 * (mon * g2_r[...]))
        sums_r[4:5, :] += _colsum(dn2 * mon)

    dh1, dmo, sums_m = _mm("d_a2_pre_ffn", dhff, wf_1, _NT, t, d, d_ff, tr, d, tk_f,
                           [(h1,) + row_d, (dy,) + row_d, (mo,) + row_d, (sc2,) + vec_d, (g_pre_ffn,) + vec_d,
                            (gt1,) + vec_d, (g_post_mix,) + vec_d] + _after(sent_ff1),
                           [(jax.ShapeDtypeStruct((t, d), F32),) + row_d, (jax.ShapeDtypeStruct((t, d), BF16),) + row_d,
                            (jax.ShapeDtypeStruct((8, d), F32),) + sums_d], pre_ffn_bwd, row_chunk=rc)
    gw_o, gw_o_16 = _grad_w("grad_w_o", merged, dmo)

    n_j = d // tn_d

    def merge_bwd_body(dmo_ref, wo_ref, ga_ref, gb_ref, ya_ref, yb_ref, dya_ref, dyb_ref, dproj_ref, acc_s):
        g = pl.program_id(2)

        @pl.when(g == 0)
        def _():
            dm = _dot(dmo_ref[...], wo_ref[...], _NT)
            acc_s[...] = dm
            sa = _sigmoid(ga_ref[...])
            dya_ref[...] = (dm * sa).astype(BF16)
            dproj_ref[...] = (dm * ya_ref[...] * sa * (1.0 - sa)).astype(BF16)

        @pl.when(g == 1)
        def _():
            dm = acc_s[...]
            sb = _sigmoid(gb_ref[...])
            dyb_ref[...] = (dm * sb).astype(BF16)
            dproj_ref[...] = (dm * yb_ref[...] * sb * (1.0 - sb)).astype(BF16)

    tile3 = pl.BlockSpec((tm, tn_d), lambda i, j, g: (i, j))
    dy_a, dy_b, dproj = pl.pallas_call(
        merge_bwd_body, name="d_merged", grid=(t // tm, n_j, 2),
        in_specs=[pl.BlockSpec((tm, d), lambda i, j, g: (i, 0)), pl.BlockSpec((tn_d, d), lambda i, j, g: (j, 0)),
                  pl.BlockSpec((tm, tn_d), lambda i, j, g: (i, ga_blk + j)),
                  pl.BlockSpec((tm, tn_d), lambda i, j, g: (i, gb_blk + j)), tile3, tile3],
        out_specs=[tile3, tile3, pl.BlockSpec((tm, tn_d), lambda i, j, g: (i, ga_blk + g * n_j + j))],
        out_shape=[jax.ShapeDtypeStruct((t, d), BF16), jax.ShapeDtypeStruct((t, d), BF16),
                   jax.ShapeDtypeStruct((t, n_in), BF16)],
        scratch_shapes=[pltpu.VMEM((tm, tn_d), F32)], compiler_params=_params(3),
    )(dmo, wf_o, proj, proj, y_a, y_b)

    def store_bf16(acc, i, j, extra_refs, out_refs, rows):
        out_refs[0][...] = acc.astype(BF16)

    tn_w = _tile(width, 512)
    blk_w = ((tm, tn_w), lambda i, j: (i, j))
    dout_a, = _mm("d_out_a", dy_a, wf_a, _NT, t, width, d, tm, tn_w, d, [],
                  [(jax.ShapeDtypeStruct((t, width), BF16),) + blk_w], store_bf16)
    dout_b, = _mm("d_out_b", dy_b, wf_b, _NT, t, width, d, tm, tn_w, d, [],
                  [(jax.ShapeDtypeStruct((t, width), BF16),) + blk_w], store_bf16)
    gw_a, gw_a_16 = _grad_w("grad_w_a_out", out_a, dy_a)
    gw_b, gw_b_16 = _grad_w("grad_w_b_out", out_b, dy_b)

    w_st = jnp.swapaxes(w_spatial[0], 1, 2)
    dproj, dg_sgu, dw_sp, dbst = _sgu_bwd(proj, dout_b, dproj, g_sgu_norm, w_spatial[0], w_st, bst, width, z_block)
    sent_mid = send_grads("mid", [gw_a_16, gw_b_16, gw_o_16], big_axes[1:4])
    half = n_in // 2
    assert half >= 5 * width, "the upper half of w_in's columns must lie past the HGRN2 columns"
    gw_in_hi, gw_in_hi16 = _grad_w("grad_w_in_hi", a1, dproj, token=sent_mid, cols=(half, half))
    plan_hi = _scatter_half_plan(gw_in_hi16, 1)
    scatters["in_hi"] = (plan_hi,) + _split_start("scatter_in_hi_start", plan_hi)
    dproj, _ = lax.optimization_barrier((dproj, scatters["in_hi"][3]))
    dproj, dgh_heads, dlb, _ = _hgrn_bwd(proj, osum, dout_a, dproj, lb, g_hgrn_norm, width, _NO_EXCHANGE)
    gw_in_lo, gw_in_lo16 = _grad_w("grad_w_in_lo", a1, dproj, cols=(0, half))
    land_in_hi, = received_grads("in_hi", gw_in_lo)
    plan_lo = _scatter_half_plan(gw_in_lo16, 0)
    scatters["in_lo"] = (plan_lo,) + _split_start("scatter_in_lo_start", plan_lo, landing=[land_in_hi])
    sent_in = scatters["in_lo"][3]

    def pre_mix_bwd(acc, i, j, extra_refs, out_refs, rows):
        x_r, dh1_r, sc1_r, g1_r = extra_refs[:4]
        dx_r, sums_r = out_refs
        xv = x_r[rows, :]
        r1 = _rms(xv)
        xn = xv * r1
        dn1 = acc * (1.0 + sc1_r[...])
        dxn = dn1 * g1_r[...]
        dx_r[rows, :] = dh1_r[rows, :] + r1 * (dxn - xn * jnp.mean(dxn * xn, axis=-1, keepdims=True))
        zero_first(sums_r, i, rows)

        sums_r[0:1, :] += _colsum(acc)
        sums_r[1:2, :] += _colsum(acc * (xn * g1_r[...]))
        sums_r[2:3, :] += _colsum(dn1 * xn)

    tk_in = _tile(n_in, 1024)
    grad_x, sums_x = _mm(
        "d_a1_pre_mix", dproj, wf_in, _NT, t, d, n_in, tr, d, tk_in,
        [(x2,) + row_d, (dh1,) + row_d, (sc1,) + vec_d, (g_pre_mix,) + vec_d] + _after(sent_in),
        [(jax.ShapeDtypeStruct((t, d), F32),) + row_d, (jax.ShapeDtypeStruct((8, d), F32),) + sums_d],
        pre_mix_bwd, row_chunk=rc)

    dmod = jnp.concatenate([sums_x[0:2], sums_m[3:4], sums_m[0:2], sums_f[1:2]], axis=0).reshape(N_DEV, n_ada // LANE, LANE)
    ada_rows = -(-(n_ada // LANE) // 8) * 8
    dmod = jnp.pad(dmod, ((0, 0), (0, ada_rows - n_ada // LANE), (0, 0))).reshape(N_DEV * ada_rows, LANE)
    parts = [dmod, _rows(sums_x[2:3]), _rows(sums_m[4:5]), _rows(sums_m[2:3]), _rows(sums_f[2:3]),
             _rows(jnp.sum(dgh_heads, axis=0)), _rows(dg_sgu), _rows(dw_sp), _rows(dbst.T)]
    n_params = sum(p.shape[0] for p in parts)
    parts.append(jnp.full((8, LANE), loss_mine, F32))
    n_common = n_params + 8
    payload = jnp.concatenate(parts + [_rows(dlb)], axis=0)

    moms = [m_w_in, m_w_a_out, m_w_b_out, m_w_o, m_w_ff1, m_w_ff2]
    vars_ = [v_w_in, v_w_a_out, v_w_b_out, v_w_o, v_w_ff1, v_w_ff2]
    big_out = {}

    def big_update(nm, g_full, landing):
        k = big_names.index(nm)
        outs = _adamw_big("adamw_" + nm, me_arr, big[k], moms[k][0], vars_[k][0], g_full, landing, big_axes[k])
        big_out[nm] = [o[None] for o in outs]
        return outs[0]

    land_ff2, = received_grads("ff2", grad_x)
    done = big_update("w_ff2", [gw_ff2], land_ff2)
    land_ff1, = received_grads("ff1", done)
    done = big_update("w_ff1", [gw_ff1], land_ff1)
    land_a, land_b, land_o = received_grads("mid", done)
    big_update("w_a_out", [gw_a], land_a)
    big_update("w_b_out", [gw_b], land_b)
    done = big_update("w_o", [gw_o], land_o)

    payload, _ = lax.optimization_barrier((payload, done))
    gathered = _all_gather_small("gather_small_grads", payload)

    dmod_mine = lax.dynamic_slice_in_dim(gathered[:, :N_DEV * ada_rows, :].reshape(N_DEV, N_DEV, ada_rows * LANE),
                                         me, 1, axis=1)[:, 0, :n_ada]
    ada_out = [o[None] for o in _adamw_ada(sc_all.T, dmod_mine, w_ada[0], m_w_ada[0], v_w_ada[0])]

    def pack(b_, g1_, g2_, g3_, g4_, gh_, gs_, ws_, bs_):
        b3 = b_.reshape(N_DEV, n_ada // LANE, LANE)
        b3 = jnp.pad(b3, ((0, 0), (0, ada_rows - n_ada // LANE), (0, 0))).reshape(N_DEV * ada_rows, LANE)
        return jnp.concatenate([b3, _rows(g1_), _rows(g2_), _rows(g3_), _rows(g4_), _rows(gh_), _rows(gs_),
                                _rows(ws_), _rows(bs_), jnp.zeros((8, LANE), F32)], axis=0)

    small_w = (b_ada, g_pre_mix, g_post_mix, g_pre_ffn, g_post_ffn, g_hgrn_norm, g_sgu_norm, w_spatial, b_spatial)
    small_m = (m_b_ada, m_g_pre_mix, m_g_post_mix, m_g_pre_ffn, m_g_post_ffn, m_g_hgrn_norm, m_g_sgu_norm, m_w_spatial, m_b_spatial)
    small_v = (v_b_ada, v_g_pre_mix, v_g_post_mix, v_g_pre_ffn, v_g_post_ffn, v_g_hgrn_norm, v_g_sgu_norm, v_w_spatial, v_b_spatial)
    packed = _adamw_small(gathered[:, :n_common, :], pack(*small_w), pack(*small_m), pack(*small_v))

    def unpack(slab):
        outs, at = [], 0
        b3 = slab[:N_DEV * ada_rows].reshape(N_DEV, ada_rows, LANE)[:, :n_ada // LANE, :]
        outs.append(b3.reshape(b_ada.shape))
        at = N_DEV * ada_rows
        for ref in small_w[1:]:
            n_el = ref.size
            n_r = -(-(n_el // LANE) // 8) * 8
            outs.append(slab[at:at + n_el // LANE].reshape(ref.shape))
            at += n_r
        return outs

    small_out = [unpack(s) for s in packed]
    loss = packed[0][n_params, 0]

    dlb_all = gathered[:, n_common:n_common + 2 * heads, :].reshape(N_DEV, 2, heads, LANE)
    dlb_mine = lax.dynamic_index_in_dim(dlb_all, me, axis=2, keepdims=False)
    lb_out = _adamw_lb(dlb_mine, lb_logits, m_lb_logits, v_lb_logits)

    land_in, = received_grads("in_lo", ada_out[0])
    big_update("w_in", [gw_in_lo, gw_in_hi], land_in)

    order = ["w_ada", "b_ada", "g_pre_mix", "g_post_mix", "g_pre_ffn", "g_post_ffn", "w_in", "lb_logits", "g_hgrn_norm",
             "w_a_out", "g_sgu_norm", "w_spatial", "b_spatial", "w_b_out", "w_o", "w_ff1", "w_ff2"]
    small_names = ["b_ada", "g_pre_mix", "g_post_mix", "g_pre_ffn", "g_post_ffn", "g_hgrn_norm", "g_sgu_norm", "w_spatial", "b_spatial"]

    def leaf(kind, nm):
        if nm == "w_ada":
            return ada_out[kind]
        if nm == "lb_logits":
            return lb_out[kind]
        if nm in big_out:
            return big_out[nm][kind]
        return small_out[kind][small_names.index(nm)]

    result = [loss, grad_x[None]]
    for kind in range(4):
        result += [leaf(kind, nm) for nm in order]
    return tuple(result)
```

```python
import functools
import math

import jax
import jax.numpy as jnp
from jax import lax
from jax.experimental import pallas as pl
from jax.experimental.pallas import tpu as pltpu

F32 = jnp.float32
BF16 = jnp.bfloat16
MESH = pl.DeviceIdType.MESH
HIGHEST = lax.Precision.HIGHEST

N_DEV = 8
HEAD = 128
A_CHUNK = 32
N_MOD = 6
EPS = 1e-6
LANE = 128
VMEM_LIMIT = 60 * 1024 * 1024

ADAM_LR = 0.001
ADAM_B1 = 0.9
ADAM_B2 = 0.999
ADAM_EPS = 1e-08
ADAM_WD = 0.01
ADAM_STEP = 10

_NN = (((1,), (0,)), ((), ()))
_NT = (((1,), (1,)), ((), ()))
_TN = (((0,), (0,)), ((), ()))


def _dot(a, b, dims=_NN, precision=None):
    return lax.dot_general(a, b, dims, preferred_element_type=F32, precision=precision)


def _bdot(a, b, dims=_NN):
    return _dot(a.astype(BF16), b.astype(BF16), dims)


def _params(n_grid):
    return pltpu.CompilerParams(dimension_semantics=("arbitrary",) * n_grid, vmem_limit_bytes=VMEM_LIMIT)


def _dev_index():
    return lax.axis_index("x") * 4 + lax.axis_index("y") * 2 + lax.axis_index("c")


def _dev_coords(i):
    return (i // 4, (i // 2) % 2, i % 2)


def _sigmoid(x):
    return 1.0 / (1.0 + jnp.exp(-x))


def _erf(x):
    ax = jnp.abs(x)
    t = 1.0 / (1.0 + 0.3275911 * ax)
    poly = ((((1.061405429 * t - 1.453152027) * t + 1.421413741) * t - 0.284496736) * t + 0.254829592) * t
    y = 1.0 - poly * jnp.exp(-ax * ax)
    return jnp.where(x < 0, -y, y)


def _gelu_and_grad(x):
    cdf = 0.5 * (1.0 + _erf(x * (2.0 ** -0.5)))
    pdf = jnp.exp(-0.5 * x * x) * (1.0 / math.sqrt(2.0 * math.pi))
    return x * cdf, cdf + x * pdf


def _rms(x):
    return lax.rsqrt(jnp.mean(x * x, axis=-1, keepdims=True) + EPS)


def _colsum(x):
    return jnp.sum(x, axis=0, keepdims=True)


def _tile(n, want):
    if n <= want:
        return n
    t = (want // LANE) * LANE
    while n % t:
        t -= LANE
    assert t > 0, (n, want)
    return t


def _all_gather_small(name, payload):
    rows = payload.shape[0]

    def body(p_ref, out_ref, send_sems, recv_sems, local_sem):
        me = _dev_index()
        mine = pltpu.make_async_copy(p_ref, out_ref.at[me], local_sem)
        mine.start()
        sends = []
        for r in range(1, N_DEV):
            peer = (me + r) % N_DEV
            cp = pltpu.make_async_remote_copy(
                src_ref=p_ref, dst_ref=out_ref.at[me], send_sem=send_sems.at[r - 1], recv_sem=recv_sems.at[r - 1],
                device_id=_dev_coords(peer), device_id_type=MESH)
            cp.start()
            sends.append(cp)
        for r in range(1, N_DEV):
            src = (me + N_DEV - r) % N_DEV
            pltpu.make_async_remote_copy(
                src_ref=p_ref, dst_ref=out_ref.at[src], send_sem=send_sems.at[r - 1], recv_sem=recv_sems.at[r - 1],
                device_id=_dev_coords(src), device_id_type=MESH).wait_recv()
        for cp in sends:
            cp.wait_send()
        mine.wait()

    return pl.pallas_call(
        body, name=name,
        out_shape=jax.ShapeDtypeStruct((N_DEV, rows, LANE), F32),
        in_specs=[pl.BlockSpec(memory_space=pltpu.VMEM)],
        out_specs=pl.BlockSpec(memory_space=pltpu.VMEM),
        scratch_shapes=[pltpu.SemaphoreType.DMA((N_DEV - 1,)), pltpu.SemaphoreType.DMA((N_DEV - 1,)),
                        pltpu.SemaphoreType.DMA],
        compiler_params=pltpu.CompilerParams(vmem_limit_bytes=VMEM_LIMIT),
    )(payload)


def _region(ref, dev, axis, n):
    start = pl.multiple_of(dev * n, LANE if axis == 1 else 16)
    return ref.at[:, pl.ds(start, n)] if axis == 1 else ref.at[pl.ds(start, n), :]


class _Exchange:
    def __init__(self, arrays, out_shapes, sems, start, finish):
        self.arrays, self.out_shapes, self.sems, self.start, self.finish = arrays, out_shapes, sems, start, finish


def _gather_plan(shards, axes):
    n_w = len(shards)
    fulls = []
    for s, ax in zip(shards, axes):
        shp = (s.shape[0], s.shape[1] * N_DEV) if ax == 1 else (s.shape[0] * N_DEV, s.shape[1])
        fulls.append(jax.ShapeDtypeStruct(shp, BF16))
    widths = [s.shape[ax] for s, ax in zip(shards, axes)]

    def places():
        x, y, c = lax.axis_index("x"), lax.axis_index("y"), lax.axis_index("c")
        chips = [(1 - x, y), (x, 1 - y), (1 - x, 1 - y)]
        return (x, y, c), (x, y, 1 - c), chips

    def index(p):
        return p[0] * 4 + p[1] * 2 + p[2]

    def copy(w, k, s_refs, f_refs, sems, block, to, from_shard):
        send_sems, recv_sems, _ = sems
        dst = _region(f_refs[w], index(block), axes[w], widths[w])
        return pltpu.make_async_remote_copy(
            src_ref=s_refs[w] if from_shard else dst, dst_ref=dst,
            send_sem=send_sems.at[w, k], recv_sem=recv_sems.at[w, k], device_id=to, device_id_type=MESH)

    def local(w, s_refs, f_refs, sems, me):
        return pltpu.make_async_copy(s_refs[w], _region(f_refs[w], index(me), axes[w], widths[w]), sems[2].at[w])

    def start(s_refs, f_refs, sems):
        me, sib, chips = places()
        for w in range(n_w):
            local(w, s_refs, f_refs, sems, me).start()
            copy(w, 0, s_refs, f_refs, sems, me, sib, True).start()
            for j, chip in enumerate(chips):
                copy(w, 1 + j, s_refs, f_refs, sems, me, (*chip, me[2]), True).start()

    def finish(s_refs, f_refs, sems):
        me, sib, chips = places()
        for w in range(n_w):
            for j, chip in enumerate(chips):
                copy(w, 1 + j, s_refs, f_refs, sems, (*chip, me[2]), me, True).wait_recv()
                copy(w, 4 + j, s_refs, f_refs, sems, (*chip, me[2]), sib, False).start()
        for w in range(n_w):
            copy(w, 0, s_refs, f_refs, sems, sib, me, True).wait_recv()
            for j, chip in enumerate(chips):
                copy(w, 4 + j, s_refs, f_refs, sems, (*chip, sib[2]), me, False).wait_recv()
        for w in range(n_w):
            for k in range(N_DEV - 1):
                copy(w, k, s_refs, f_refs, sems, me, sib, True).wait_send()
            local(w, s_refs, f_refs, sems, me).wait()

    sems = [pltpu.SemaphoreType.DMA((n_w, N_DEV - 1)), pltpu.SemaphoreType.DMA((n_w, N_DEV - 1)),
            pltpu.SemaphoreType.DMA((n_w,))]
    return _Exchange(list(shards), fulls, sems, start, finish)


def _scatter_plan(grads, axes):
    n_w = len(grads)
    lands = []
    for g, ax in zip(grads, axes):
        shp = (g.shape[0], g.shape[1] // N_DEV) if ax == 1 else (g.shape[0] // N_DEV, g.shape[1])
        lands.append(jax.ShapeDtypeStruct((N_DEV - 1,) + shp, BF16))
    widths = [ld.shape[1 + ax] for ld, ax in zip(lands, axes)]

    def copy(w, r, g_refs, l_refs, sems, block, to):
        return pltpu.make_async_remote_copy(
            src_ref=_region(g_refs[w], block, axes[w], widths[w]), dst_ref=l_refs[w].at[r - 1],
            send_sem=sems[0].at[w * (N_DEV - 1) + r - 1], recv_sem=sems[1].at[w * (N_DEV - 1) + r - 1],
            device_id=_dev_coords(to), device_id_type=MESH)

    def start(g_refs, l_refs, sems):
        me = _dev_index()
        for w in range(n_w):
            for r in range(1, N_DEV):
                owner = (me + r) % N_DEV
                copy(w, r, g_refs, l_refs, sems, owner, owner).start()

    def finish(g_refs, l_refs, sems):
        me = _dev_index()
        for w in range(n_w):
            for r in range(1, N_DEV):
                copy(w, r, g_refs, l_refs, sems, me, (me + N_DEV - r) % N_DEV).wait_recv()
        for w in range(n_w):
            for r in range(1, N_DEV):
                copy(w, r, g_refs, l_refs, sems, me, (me + r) % N_DEV).wait_send()

    sems = [pltpu.SemaphoreType.DMA((n_w * (N_DEV - 1),)), pltpu.SemaphoreType.DMA((n_w * (N_DEV - 1),))]
    return _Exchange(list(grads), lands, sems, start, finish)


def _scatter_half_plan(grad_half, half):
    width = grad_half.shape[1] // (N_DEV // 2)
    land = jax.ShapeDtypeStruct((N_DEV - 1, grad_half.shape[0], width), BF16)

    def copy(r, g_refs, l_refs, sems, owner, to):
        return pltpu.make_async_remote_copy(
            src_ref=_region(g_refs[0], owner % (N_DEV // 2), 1, width), dst_ref=l_refs[0].at[r - 1],
            send_sem=sems[0].at[r - 1], recv_sem=sems[1].at[r - 1],
            device_id=_dev_coords(to), device_id_type=MESH)

    def in_half(dev):
        return dev // (N_DEV // 2) == half

    def start(g_refs, l_refs, sems):
        me = _dev_index()
        for r in range(1, N_DEV):
            owner = (me + r) % N_DEV

            @pl.when(in_half(owner))
            def _():
                copy(r, g_refs, l_refs, sems, owner, owner).start()

    def finish(g_refs, l_refs, sems):
        me = _dev_index()

        @pl.when(in_half(me))
        def _():
            for r in range(1, N_DEV):
                copy(r, g_refs, l_refs, sems, me, (me + N_DEV - r) % N_DEV).wait_recv()

        for r in range(1, N_DEV):
            owner = (me + r) % N_DEV

            @pl.when(in_half(owner))
            def _():
                copy(r, g_refs, l_refs, sems, owner, owner).wait_send()

    sems = [pltpu.SemaphoreType.DMA((N_DEV - 1,)), pltpu.SemaphoreType.DMA((N_DEV - 1,))]
    return _Exchange([grad_half], [land], sems, start, finish)


def _run_exchange(name, plan):
    n_in, n_out = len(plan.arrays), len(plan.out_shapes)

    def body(*refs):
        ins, outs, sems = refs[:n_in], refs[n_in:n_in + n_out], refs[n_in + n_out:]
        plan.start(ins, outs, sems)
        plan.finish(ins, outs, sems)

    any_spec = pl.BlockSpec(memory_space=pl.ANY)
    return pl.pallas_call(
        body, name=name, out_shape=plan.out_shapes,
        in_specs=[any_spec] * n_in, out_specs=[any_spec] * n_out, scratch_shapes=plan.sems,
    )(*plan.arrays)


_NO_EXCHANGE = _Exchange([], [], [], lambda i, o, s: None, lambda i, o, s: None)


def _direct_gather_plan(fulls, axes):
    n_w = len(fulls)
    widths = [f.shape[ax] // N_DEV for f, ax in zip(fulls, axes)]
    fulls = [jax.ShapeDtypeStruct(f.shape, f.dtype) for f in fulls]

    def copy(w, r, s_refs, f_refs, sems, block, to):
        part = _region(f_refs[w], block, axes[w], widths[w])
        return pltpu.make_async_remote_copy(
            src_ref=part, dst_ref=part,
            send_sem=sems[0].at[w * (N_DEV - 1) + r - 1], recv_sem=sems[1].at[w * (N_DEV - 1) + r - 1],
            device_id=_dev_coords(to), device_id_type=MESH)

    def start(s_refs, f_refs, sems):
        me = _dev_index()
        for w in range(n_w):
            for r in range(1, N_DEV):
                copy(w, r, s_refs, f_refs, sems, me, (me + r) % N_DEV).start()

    def finish(s_refs, f_refs, sems):
        me = _dev_index()
        for w in range(n_w):
            for r in range(1, N_DEV):
                src = (me + N_DEV - r) % N_DEV
                copy(w, r, s_refs, f_refs, sems, src, src).wait_recv()
        for w in range(n_w):
            for r in range(1, N_DEV):
                copy(w, r, s_refs, f_refs, sems, me, (me + r) % N_DEV).wait_send()

    sems = [pltpu.SemaphoreType.DMA((n_w * (N_DEV - 1),)), pltpu.SemaphoreType.DMA((n_w * (N_DEV - 1),))]
    return _Exchange([], fulls, sems, start, finish)


_HBM = pl.BlockSpec(memory_space=pltpu.HBM)
_SEM = pl.BlockSpec(memory_space=pltpu.SEMAPHORE)
_EFFECT = pltpu.SideEffectType.DATAFLOW_SIDE_EFFECTING


def _split_start(name, plan, landing=None):
    n_in, n_out, n_sem = len(plan.arrays), len(plan.out_shapes), len(plan.sems)

    def body(*refs):
        ins, lands = refs[:n_in], refs[n_in:n_in + n_out]
        sems = refs[n_in + n_out:n_in + n_out + n_sem]
        token = refs[-1]
        plan.start(ins, lands, sems)
        token[...] = jnp.zeros_like(token)

    hbm = lambda a: pltpu.HBM(a.shape, a.dtype)
    results = pl.pallas_call(
        body, name=name,
        out_shape=tuple(plan.sems) + tuple(hbm(a) for a in plan.arrays) + tuple(hbm(a) for a in plan.out_shapes)
        + (jax.ShapeDtypeStruct((8, LANE), F32),),
        in_specs=(_HBM,) * (n_in + n_out),
        out_specs=(_SEM,) * n_sem + (_HBM,) * (n_in + n_out) + (pl.BlockSpec(memory_space=pltpu.VMEM),),
        input_output_aliases={i: n_sem + i for i in range(n_in + n_out)},
        compiler_params=pltpu.CompilerParams(has_side_effects=_EFFECT),
    )(*[pltpu.with_memory_space_constraint(a, pltpu.HBM) for a in plan.arrays],
      *[pltpu.with_memory_space_constraint(a, pltpu.HBM)
        for a in (landing if landing is not None else [lax.empty(a.shape, a.dtype) for a in plan.out_shapes])])
    return results[:n_sem], results[n_sem:n_sem + n_in + n_out], results[-1]


def _split_wait(name, plan, sems, thru, after):
    n_in, n_out, n_sem = len(plan.arrays), len(plan.out_shapes), len(plan.sems)

    def body(*refs):
        ins, lands = refs[:n_in], refs[n_in:n_in + n_out]
        sem_refs = refs[n_in + n_out:n_in + n_out + n_sem]
        plan.finish(ins, lands, sem_refs)

    hbm = lambda a: pltpu.HBM(a.shape, a.dtype)
    results = pl.pallas_call(
        body, name=name,
        out_shape=tuple(hbm(a) for a in plan.arrays) + tuple(hbm(a) for a in plan.out_shapes),
        in_specs=(_HBM,) * (n_in + n_out) + (_SEM,) * n_sem + (pl.BlockSpec(memory_space=pl.ANY),),
        out_specs=(_HBM,) * (n_in + n_out),
        input_output_aliases={i: i for i in range(n_in + n_out)},
        compiler_params=pltpu.CompilerParams(has_side_effects=_EFFECT),
    )(*thru, *sems, after)
    return results[:n_in], results[n_in:]


def _cast_into_full(name, me, w, axis):
    r, c = w.shape
    tr = _tile(r, 256)
    if axis == 1:
        shape, place = (r, c * N_DEV), pl.BlockSpec((tr, c), lambda i, me_ref: (i, me_ref[0]))
    else:
        shape, place = (r * N_DEV, c), pl.BlockSpec((tr, c), lambda i, me_ref: (me_ref[0] * (r // tr) + i, 0))

    def body(me_ref, w_ref, o_ref):
        o_ref[...] = w_ref[...].astype(BF16)

    grid_spec = pltpu.PrefetchScalarGridSpec(
        num_scalar_prefetch=1, grid=(r // tr,),
        in_specs=[pl.BlockSpec((tr, c), lambda i, me_ref: (i, 0))], out_specs=place)
    return pl.pallas_call(body, name=name, grid_spec=grid_spec, out_shape=jax.ShapeDtypeStruct(shape, BF16),
                          compiler_params=_params(1))(me, w)


def _mm(name, a, b, dims, m, n, k, tm, tn, tk, extras, outs, epilogue, row_chunk=None, exchange=None,
        b_col_block=0):
    ni, nj, nk = m // tm, n // tn, k // tk
    ne, no = len(extras), len(outs)
    xin = len(exchange.arrays) if exchange else 0
    xout = len(exchange.out_shapes) if exchange else 0
    if dims == _TN:
        a_spec = pl.BlockSpec((tk, tm), lambda i, j, kk: (kk, i))
    else:
        a_spec = pl.BlockSpec((tm, tk), lambda i, j, kk: (i, kk))
    if dims == _NT:
        b_spec = pl.BlockSpec((tn, tk), lambda i, j, kk: (j, kk))
    else:
        b_spec = pl.BlockSpec((tk, tn), lambda i, j, kk: (kk, j + b_col_block))
    chunks = [slice(None)] if row_chunk is None else [slice(r, r + row_chunk) for r in range(0, tm, row_chunk)]

    def lift(index_map):
        return lambda i, j, kk: index_map(i, j)

    def body(a_ref, b_ref, *rest):
        extra_refs, rest = rest[:ne], rest[ne:]
        xin_refs, rest = rest[:xin], rest[xin:]
        out_refs, rest = rest[:no], rest[no:]
        xout_refs, rest = rest[:xout], rest[xout:]
        i, j, kk = pl.program_id(0), pl.program_id(1), pl.program_id(2)
        if exchange:
            sem_refs = rest[1:] if nk > 1 else rest

            @pl.when((i == 0) & (j == 0) & (kk == 0))
            def _():
                exchange.start(xin_refs, xout_refs, sem_refs)

        if nk == 1:
            part = _dot(a_ref[...], b_ref[...], dims)
            for rows in chunks:
                epilogue(part[rows], i, j, extra_refs, out_refs, rows)
        else:
            acc_ref = rest[0]

            @pl.when(kk == 0)
            def _():
                acc_ref[...] = _dot(a_ref[...], b_ref[...], dims)

            @pl.when(kk > 0)
            def _():
                acc_ref[...] += _dot(a_ref[...], b_ref[...], dims)

            @pl.when(kk == nk - 1)
            def _():
                for rows in chunks:
                    epilogue(acc_ref[rows, :], i, j, extra_refs, out_refs, rows)

        if exchange:
            @pl.when((i == ni - 1) & (j == nj - 1) & (kk == nk - 1))
            def _():
                exchange.finish(xin_refs, xout_refs, sem_refs)

    any_spec = pl.BlockSpec(memory_space=pl.ANY)
    once = dict(pipeline_mode=pl.Buffered(1)) if (row_chunk is not None and nk > 1) else {}
    results = pl.pallas_call(
        body, name=name,
        grid=(ni, nj, nk),
        in_specs=[a_spec, b_spec] + [pl.BlockSpec(bs, lift(im), **once) for _, bs, im in extras] + [any_spec] * xin,
        out_specs=[pl.BlockSpec(bs, lift(im), **once) for _, bs, im in outs] + [any_spec] * xout,
        out_shape=[sd for sd, _, _ in outs] + (list(exchange.out_shapes) if exchange else []),
        scratch_shapes=([pltpu.VMEM((tm, tn), F32)] if nk > 1 else []) + (list(exchange.sems) if exchange else []),
        compiler_params=_params(3),
    )(a, b, *[arr for arr, _, _ in extras], *(exchange.arrays if exchange else []))
    return (results[:no], results[no:]) if exchange else results


def _after(token):
    return [(token, (8, LANE), lambda i, j: (0, 0))]


def _grad_w(name, a, dc, token=None, tm=512, tn=1024, cols=None):
    t, m = a.shape
    first, n = cols if cols is not None else (0, dc.shape[1])
    tm, tn = _tile(m, tm), _tile(n, tn)
    assert first % tn == 0

    def epilogue(acc, i, j, extra_refs, out_refs, rows):
        out_refs[0][...] = acc
        out_refs[1][...] = acc.astype(BF16)

    blk = ((tm, tn), lambda i, j: (i, j))
    return _mm(name, a, dc, _TN, m, n, t, tm, tn, t, _after(token) if token is not None else [],
               [(jax.ShapeDtypeStruct((m, n), F32),) + blk, (jax.ShapeDtypeStruct((m, n), BF16),) + blk], epilogue,
               b_col_block=first // tn)


def _cast_bf16(name, w):
    r, c = w.shape
    tr = _tile(r, 256)
    return pl.pallas_call(
        lambda w_ref, o_ref: o_ref.__setitem__(Ellipsis, w_ref[...].astype(BF16)), name=name,
        grid=(r // tr,), in_specs=[pl.BlockSpec((tr, c), lambda i: (i, 0))],
        out_specs=pl.BlockSpec((tr, c), lambda i: (i, 0)), out_shape=jax.ShapeDtypeStruct((r, c), BF16),
        compiler_params=_params(1),
    )(w)


def _prep_small(c_row, lb_logits):
    d = c_row.shape[1]
    rows = d // LANE

    def body(c_ref, l_ref, o_ref):
        cv = c_ref[...]
        o_ref[0:rows, :] = cv * _sigmoid(cv)
        lbs = [_sigmoid(l_ref[dr][0:1, :] - l_ref[dr][1:2, :]) for dr in range(2)]
        o_ref[rows:rows + 8, :] = jnp.concatenate(lbs + [jnp.zeros((6, LANE), F32)], axis=0)

    return pl.pallas_call(
        body, name="prep_small", out_shape=jax.ShapeDtypeStruct((rows + 8, LANE), F32),
    )(c_row.reshape(rows, LANE), lb_logits)


def _mod_shard(sc_all, w_ada_shard, b_shard):
    d, n = w_ada_shard.shape
    tn = _tile(n, 512)

    def body(s_ref, w_ref, b_ref, o_ref):
        o_ref[...] = _dot(s_ref[...], w_ref[...], precision=HIGHEST) + b_ref[...]

    return pl.pallas_call(
        body, name="mod_shard", grid=(n // tn,),
        in_specs=[pl.BlockSpec((N_DEV, d), lambda j: (0, 0)), pl.BlockSpec((d, tn), lambda j: (0, j)),
                  pl.BlockSpec((1, tn), lambda j: (0, j))],
        out_specs=pl.BlockSpec((N_DEV, tn), lambda j: (0, j)),
        out_shape=jax.ShapeDtypeStruct((N_DEV, n), F32), compiler_params=_params(1),
    )(sc_all, w_ada_shard, b_shard)


def _norm_mod(x, gain, shift, scale):
    t, d = x.shape
    tm = _tile(t, 512)

    def body(x_ref, g_ref, sh_ref, sc_ref, o_ref):
        xv = x_ref[...]
        o_ref[...] = ((xv * _rms(xv) * g_ref[...]) * (1.0 + sc_ref[...]) + sh_ref[...]).astype(BF16)

    vec = pl.BlockSpec((1, d), lambda i: (0, 0))
    return pl.pallas_call(
        body, name="norm_mod", grid=(t // tm,),
        in_specs=[pl.BlockSpec((tm, d), lambda i: (i, 0)), vec, vec, vec],
        out_specs=pl.BlockSpec((tm, d), lambda i: (i, 0)), out_shape=jax.ShapeDtypeStruct((t, d), BF16),
        compiler_params=_params(1),
    )(x, gain, shift, scale)


def _chunk_masks():
    row = lax.broadcasted_iota(jnp.int32, (HEAD, HEAD), 0)
    col = lax.broadcasted_iota(jnp.int32, (HEAD, HEAD), 1)
    same = (row // A_CHUNK) == (col // A_CHUNK)
    return same & (col <= row), same & (col >= row)


def _ones(mask):
    return jnp.where(mask, 1.0, 0.0).astype(BF16)


def _dot_split(ones_bf16, x):
    hi = x.astype(BF16)
    lo = (x - hi.astype(F32)).astype(BF16)
    return _dot(ones_bf16, hi) + _dot(ones_bf16, lo)


def _hgrn_block(direction, f, lb, cum2):
    sf = _sigmoid(f)
    big_f = lb + (1.0 - lb) * sf
    k = (1.0 - lb) * (1.0 - sf)
    lf = jnp.log(big_f)
    both = _dot_split(cum2, lf)
    cf, cr = both[:HEAD], both[HEAD:]
    b, rest = (cf, cr - lf) if direction == 0 else (cr, cf - lf)
    return k, sf, big_f, jnp.exp(b), jnp.exp(-b), jnp.exp(rest)


def _hgrn_fwd(proj, lb, g_norm, width, exchange):
    t = proj.shape[0]
    heads = width // HEAD
    nb, nc = t // HEAD, t // A_CHUNK
    ua = 4 if nb % 4 == 0 else (2 if nb % 2 == 0 else 1)
    ub = 16 if nc % 16 == 0 else (8 if nc % 8 == 0 else 4)
    q_scale = HEAD ** -0.5
    xin, xout = len(exchange.arrays), len(exchange.out_shapes)

    def body(q_ref, ffw_ref, fbw_ref, v_ref, og_ref, lb_ref, g_ref, *rest):
        xin_refs, rest = rest[:xin], rest[xin:]
        outa_ref, osum_ref = rest[:2]
        xout_refs, rest = rest[2:2 + xout], rest[2 + xout:]
        qd_s, ke_s, dc_s, o_s = rest[:4]
        sem_refs = rest[4:]
        h = pl.program_id(0)

        @pl.when(h == 0)
        def _():
            exchange.start(xin_refs, xout_refs, sem_refs)

        tril, triu = _chunk_masks()
        cum2 = jnp.concatenate([_ones(tril), _ones(triu)], axis=0)
        f_refs = (ffw_ref, fbw_ref)
        lbs = (lb_ref[0:1, :], lb_ref[1:2, :])

        def phase_a(it, carry):
            loaded = []
            for u in range(ua):
                rows = pl.ds(pl.multiple_of((it * ua + u) * HEAD, HEAD), HEAD)
                loaded.append((rows, q_ref[rows, :], v_ref[rows, :], ffw_ref[rows, :], fbw_ref[rows, :]))
            chains = [(d, rows, qv * q_scale, vv.astype(BF16), fv)
                      for rows, qv, vv, f0, f1 in loaded for d, fv in ((0, f0), (1, f1))]
            blocks = [_hgrn_block(d, fv, lbs[d], cum2) for d, _, _, _, fv in chains]
            scaled = [(qv * eb, k * enb, k * erest, eb * erest)
                      for (_, _, qv, _, _), (k, _, _, eb, enb, erest) in zip(chains, blocks)]
            atts = [jnp.where(tril if d == 0 else triu, _bdot(qd, kd, _NT), 0.0)
                    for (d, _, _, _, _), (qd, kd, _, _) in zip(chains, scaled)]
            intras = [_bdot(att, vv) for att, (_, _, _, vv, _) in zip(atts, chains)]
            results = [(d, rows, o_intra, qd.astype(BF16), ke.astype(BF16), decay)
                       for (d, rows, _, _, _), (qd, _, ke, decay), o_intra in zip(chains, scaled, intras)]
            for d, rows, o_intra, qd16, ke16, decay in results:
                o_s[d, rows, :] = o_intra
                qd_s[d, rows, :] = qd16
                ke_s[d, rows, :] = ke16
                dc_s[d, rows, :] = decay
            return carry

        lax.fori_loop(0, nb // ua, phase_a, 0)

        def phase_b(it, states):
            loaded = []
            for u in range(ub):
                n = it * ub + u
                for d in range(2):
                    c = n if d == 0 else nc - 1 - n
                    start = pl.multiple_of(c * A_CHUNK, A_CHUNK)
                    rows = pl.ds(start, A_CHUNK)
                    loaded.append((d, rows, qd_s[d, rows, :], ke_s[d, rows, :], v_ref[rows, :],
                                   dc_s[d, pl.ds(start, 1), :], o_s[d, rows, :]))
            increments = [_dot(vv.astype(BF16), ke16, _TN) for _, _, _, ke16, vv, _, _ in loaded]
            states = list(states)
            befores = []
            for (d, _, _, _, _, decay, _), inc in zip(loaded, increments):
                befores.append(states[d].astype(BF16))
                states[d] = states[d] * decay + inc
            inters = [_dot(qd16, before, _NT) for (_, _, qd16, _, _, _, _), before in zip(loaded, befores)]
            for (d, rows, _, _, _, _, o_intra), o_inter in zip(loaded, inters):
                o_s[d, rows, :] = o_intra + o_inter
            return tuple(states)

        zero_state = jnp.zeros((HEAD, HEAD), F32)
        lax.fori_loop(0, nc // ub, phase_b, (zero_state, zero_state))

        def phase_c(i, carry):
            rows = pl.ds(pl.multiple_of(i * HEAD, HEAD), HEAD)
            o = o_s[0, rows, :] + o_s[1, rows, :]
            osum_ref[rows, :] = o
            og = og_ref[rows, :]
            outa_ref[rows, :] = (o * _rms(o) * g_ref[...] * (og * _sigmoid(og))).astype(BF16)
            return carry

        lax.fori_loop(0, nb, phase_c, 0)

        @pl.when(h == heads - 1)
        def _():
            exchange.finish(xin_refs, xout_refs, sem_refs)

    def col(p):
        return pl.BlockSpec((t, HEAD), lambda h: (0, p * heads + h))

    any_spec = pl.BlockSpec(memory_space=pl.ANY)
    results = pl.pallas_call(
        body, name="hgrn_fwd", grid=(heads,),
        in_specs=[col(0), col(1), col(2), col(3), col(4),
                  pl.BlockSpec((2, HEAD), lambda h: (0, h)), pl.BlockSpec((1, HEAD), lambda h: (0, 0))] + [any_spec] * xin,
        out_specs=[pl.BlockSpec((t, HEAD), lambda h: (0, h)), pl.BlockSpec((t, HEAD), lambda h: (0, h))] + [any_spec] * xout,
        out_shape=[jax.ShapeDtypeStruct((t, width), BF16), jax.ShapeDtypeStruct((t, width), F32)] + list(exchange.out_shapes),
        scratch_shapes=[pltpu.VMEM((2, t, HEAD), BF16), pltpu.VMEM((2, t, HEAD), BF16), pltpu.VMEM((2, t, HEAD), F32),
                        pltpu.VMEM((2, t, HEAD), F32)] + list(exchange.sems),
        compiler_params=_params(1),
    )(proj, proj, proj, proj, proj, lb, g_norm, *exchange.arrays)
    return results[0], results[1], results[2:]


def _sgu_core(u_pre, v_pre, g_v, ws_ref, bst):
    u, du = _gelu_and_grad(u_pre)
    v, dv = _gelu_and_grad(v_pre)
    mu = jnp.mean(v, axis=-1, keepdims=True)
    dlt = v - mu
    rstd = lax.rsqrt(jnp.mean(dlt * dlt, axis=-1, keepdims=True) + EPS)
    vhat = dlt * rstd
    vn = vhat * g_v
    groups = vn.shape[1] // HEAD
    cols = []
    for g in range(groups):
        vm_g = _bdot(ws_ref[g], vn[:, g * HEAD:(g + 1) * HEAD]) + bst[:, g:g + 1]
        cols.append(vm_g)
    return u, du, dv, vhat, rstd, vn, jnp.concatenate(cols, axis=1)


def _sgu_fwd(proj, g_v, w_s, bst, width, z_block):
    t = proj.shape[0]

    def body(u_ref, v_ref, g_ref, ws_ref, bst_ref, o_ref):
        u, _, _, _, _, _, vm = _sgu_core(u_ref[...], v_ref[...], g_ref[...], ws_ref, bst_ref[...])
        o_ref[...] = (u * vm).astype(BF16)

    groups = width // HEAD
    return pl.pallas_call(
        body, name="sgu_fwd", grid=(t // HEAD,),
        in_specs=[pl.BlockSpec((HEAD, width), lambda i: (i, z_block)), pl.BlockSpec((HEAD, width), lambda i: (i, z_block + 1)),
                  pl.BlockSpec((1, width), lambda i: (0, 0)), pl.BlockSpec((groups, HEAD, HEAD), lambda i: (0, 0, 0)),
                  pl.BlockSpec((HEAD, groups), lambda i: (0, 0))],
        out_specs=pl.BlockSpec((HEAD, width), lambda i: (i, 0)),
        out_shape=jax.ShapeDtypeStruct((t, width), BF16), compiler_params=_params(1),
    )(proj, proj, g_v, w_s, bst)


def _sgu_bwd(proj, dout_b, dproj, g_v, w_s, w_st, bst, width, z_block):
    t = proj.shape[0]
    groups = width // HEAD
    nblk = t // HEAD

    def body(u_ref, v_ref, do_ref, g_ref, ws_ref, wst_ref, bst_ref, dproj_hbm,
             dz_ref, dg_ref, dws_ref, dbst_ref, res_s):
        i, p = pl.program_id(0), pl.program_id(1)

        @pl.when((i == 0) & (p == 0))
        def _():
            dg_ref[...] = jnp.zeros_like(dg_ref)
            dws_ref[...] = jnp.zeros_like(dws_ref)
            dbst_ref[...] = jnp.zeros_like(dbst_ref)

        @pl.when(p == 0)
        def _():
            g_v = g_ref[...]
            u, du, dv, vhat, rstd, vn, vm = _sgu_core(u_ref[...], v_ref[...], g_v, ws_ref, bst_ref[...])
            dout = do_ref[...].astype(F32)
            res_s[0] = (dout * vm * du).astype(BF16)
            dvm = dout * u
            dvn_cols = []
            for g in range(groups):
                sl = slice(g * HEAD, (g + 1) * HEAD)
                dvm_g = dvm[:, sl]
                dbst_ref[:, g:g + 1] += jnp.sum(dvm_g, axis=1, keepdims=True)
                dws_ref[g] += _bdot(dvm_g, vn[:, sl], _NT)
                dvn_cols.append(_bdot(wst_ref[g], dvm_g))
            dvn = jnp.concatenate(dvn_cols, axis=1)
            dg_ref[...] += _colsum(dvn * vhat)
            dvh = dvn * g_v
            dvg = rstd * (dvh - jnp.mean(dvh, axis=-1, keepdims=True)
                          - vhat * jnp.mean(dvh * vhat, axis=-1, keepdims=True))
            res_s[1] = (dvg * dv).astype(BF16)

        dz_ref[...] = res_s[p]

    n_in = dproj.shape[1]
    return pl.pallas_call(
        body, name="sgu_bwd", grid=(nblk, 2),
        in_specs=[pl.BlockSpec((HEAD, width), lambda i, p: (i, z_block)),
                  pl.BlockSpec((HEAD, width), lambda i, p: (i, z_block + 1)),
                  pl.BlockSpec((HEAD, width), lambda i, p: (i, 0)),
                  pl.BlockSpec((1, width), lambda i, p: (0, 0)),
                  pl.BlockSpec((groups, HEAD, HEAD), lambda i, p: (0, 0, 0)),
                  pl.BlockSpec((groups, HEAD, HEAD), lambda i, p: (0, 0, 0)),
                  pl.BlockSpec((HEAD, groups), lambda i, p: (0, 0)),
                  pl.BlockSpec(memory_space=pl.ANY)],
        out_specs=[pl.BlockSpec((HEAD, width), lambda i, p: (i, z_block + p)),
                   pl.BlockSpec((1, width), lambda i, p: (0, 0)),
                   pl.BlockSpec((groups, HEAD, HEAD), lambda i, p: (0, 0, 0)),
                   pl.BlockSpec((HEAD, groups), lambda i, p: (0, 0))],
        out_shape=[jax.ShapeDtypeStruct((t, n_in), BF16), jax.ShapeDtypeStruct((1, width), F32),
                   jax.ShapeDtypeStruct((groups, HEAD, HEAD), F32), jax.ShapeDtypeStruct((HEAD, groups), F32)],
        scratch_shapes=[pltpu.VMEM((2, HEAD, width), BF16)],
        input_output_aliases={7: 0},
        compiler_params=_params(2),
    )(proj, proj, dout_b, g_v, w_s, w_st, bst, dproj)


def _hgrn_bwd(proj, osum, dout_a, dproj, lb, g_norm, width, exchange, after):
    t = proj.shape[0]
    heads = width // HEAD
    nb = t // HEAD
    cpb = HEAD // A_CHUNK
    ubk = 2 if nb % 2 == 0 else 1
    q_scale = HEAD ** -0.5
    xin, xout = len(exchange.arrays), len(exchange.out_shapes)

    def body(q_ref, ffw_ref, fbw_ref, v_ref, og_ref, osum_ref, douta_ref, lb_ref, g_ref, dproj_hbm, *rest):
        xin_refs, rest = rest[:xin], rest[xin + 1:]
        out_ref, dgh_ref, dlb_ref = rest[:3]
        xout_refs, rest = rest[3:3 + xout], rest[3 + xout:]
        do_s, dq_s, dv_s, res_s, ck_s = rest[:5]
        sem_refs = rest[5:]
        h, p = pl.program_id(0), pl.program_id(1)
        f_refs = (ffw_ref, fbw_ref)

        @pl.when((h == 0) & (p == 0))
        def _():
            exchange.start(xin_refs, xout_refs, sem_refs)

        @pl.when(p == 0)
        def _():
            tril, triu = _chunk_masks()
            cum2 = jnp.concatenate([_ones(tril), _ones(triu)], axis=0)
            g_row = g_ref[...]

            def pass_norm(i, dgh):
                rows = pl.ds(pl.multiple_of(i * HEAD, HEAD), HEAD)
                o = osum_ref[rows, :]
                r = _rms(o)
                oh = o * r
                og = og_ref[rows, :]
                sg = _sigmoid(og)
                dout = douta_ref[rows, :].astype(F32)
                don = dout * (og * sg)
                res_s[4, rows, :] = (dout * (oh * g_row) * (sg * (1.0 + og * (1.0 - sg)))).astype(BF16)
                doh = don * g_row
                do_s[rows, :] = r * (doh - oh * jnp.mean(doh * oh, axis=-1, keepdims=True))
                return dgh + _colsum(don * oh)

            dgh_ref[...] = lax.fori_loop(0, nb, pass_norm, jnp.zeros((1, HEAD), F32))

            lbs = (lb_ref[0:1, :], lb_ref[1:2, :])
            zero_state = jnp.zeros((HEAD, HEAD), F32)

            def chunk_order(d):
                return list(range(cpb)) if d == 0 else list(range(cpb - 1, -1, -1))

            def chunk(x, j):
                return x[j * A_CHUNK:(j + 1) * A_CHUNK, :]

            def decay_row(e_big, j):
                return e_big[j * A_CHUNK:j * A_CHUNK + 1, :]

            def cat(parts):
                return jnp.concatenate([parts[j] for j in range(cpb)], axis=0)

            def block_states(d, start, incs, e_big):
                befores, st = {}, start
                for j in chunk_order(d):
                    befores[j] = st
                    st = st * decay_row(e_big, j) + incs[j]
                return befores, st

            def pass_states(it, states):
                loaded = []
                for u in range(ubk):
                    for d in range(2):
                        blk = it * ubk + u if d == 0 else nb - 1 - (it * ubk + u)
                        rows = pl.ds(pl.multiple_of(blk * HEAD, HEAD), HEAD)
                        loaded.append((d, blk, f_refs[d][rows, :], v_ref[rows, :]))
                blocks = [_hgrn_block(d, fv, lbs[d], cum2) for d, _, fv, _ in loaded]
                incs = [{j: _bdot(chunk(vv, j), chunk(k * erest, j), _TN) for j in range(cpb)}
                        for (_, _, _, vv), (k, _, _, _, _, erest) in zip(loaded, blocks)]
                states, starts = list(states), []
                for (d, _, _, _), (_, _, _, eb, _, erest), inc in zip(loaded, blocks, incs):
                    starts.append(states[d])
                    states[d] = block_states(d, states[d], inc, eb * erest)[1]
                for (d, blk, _, _), start in zip(loaded, starts):
                    ck_s[d, blk] = start
                return tuple(states)

            lax.fori_loop(0, nb // ubk, pass_states, (zero_state, zero_state))

            def pass_back(it, carry):
                gts, dlb = [carry[0], carry[1]], carry[2]
                loaded = []
                for u, d in ((u, d) for u in range(ubk) for d in range(2)):
                    blk = nb - 1 - (it * ubk + u) if d == 0 else it * ubk + u
                    rows = pl.ds(pl.multiple_of(blk * HEAD, HEAD), HEAD)
                    loaded.append((d, rows, f_refs[d][rows, :], q_ref[rows, :], v_ref[rows, :], do_s[rows, :], ck_s[d, blk]))
                blocks = [_hgrn_block(d, fv, lbs[d], cum2) for d, _, fv, _, _, _, _ in loaded]
                scaled = []
                for (_, _, _, qv, _, _, _), (k, _, _, eb, enb, erest) in zip(loaded, blocks):
                    qh = qv * q_scale
                    scaled.append((qh, qh * eb, k * enb, k * erest, eb * erest))
                masks = [tril if d == 0 else triu for d, *_ in loaded]
                atts = [jnp.where(m, _bdot(qd, kd, _NT), 0.0) for m, (_, qd, kd, _, _) in zip(masks, scaled)]
                datts = [jnp.where(m, _bdot(do, vv, _NT), 0.0) for m, (_, _, _, _, vv, do, _) in zip(masks, loaded)]
                dvs = [_bdot(att, do, _TN) for att, (_, _, _, _, _, do, _) in zip(atts, loaded)]
                dqds = [_bdot(datt, kd) for datt, (_, _, kd, _, _) in zip(datts, scaled)]
                dkds = [_bdot(datt, qd, _TN) for datt, (_, qd, _, _, _) in zip(datts, scaled)]
                s_incs = [{j: _bdot(chunk(vv, j), chunk(ke, j), _TN) for j in range(cpb)}
                          for (_, _, _, _, vv, _, _), (_, _, _, ke, _) in zip(loaded, scaled)]
                g_incs = [{j: _bdot(chunk(do, j), chunk(qd, j), _TN) for j in range(cpb)}
                          for (_, _, _, _, _, do, _), (_, qd, _, _, _) in zip(loaded, scaled)]
                befores, afters, g_at = [], [], []
                for (d, _, _, _, _, _, ck), (_, _, _, _, e_big), s_inc, g_inc in zip(loaded, scaled, s_incs, g_incs):
                    order = chunk_order(d)
                    before, after = block_states(d, ck, s_inc, e_big)
                    befores.append(before)
                    afters.append({j: (before[order[n + 1]] if n + 1 < cpb else after) for n, j in enumerate(order)})
                    at, gt = {}, gts[d]
                    for j in reversed(order):
                        at[j] = gt
                        gt = gt * decay_row(e_big, j) + g_inc[j]
                    gts[d] = gt
                    g_at.append(at)
                dqd_i = [{j: _bdot(chunk(do, j), before[j]) for j in range(cpb)}
                         for (_, _, _, _, _, do, _), before in zip(loaded, befores)]
                dv_i = [{j: _bdot(chunk(ke, j), at[j], _NT) for j in range(cpb)}
                        for (_, _, _, ke, _), at in zip(scaled, g_at)]
                dke = [{j: _bdot(chunk(vv, j), at[j]) for j in range(cpb)}
                       for (_, _, _, _, vv, _, _), at in zip(loaded, g_at)]
                results, new = [], []
                for n, ((d, rows, _, _, _, _, _), (k, sf, big_f, eb, enb, erest), (qh, _, _, _, _)) in enumerate(
                        zip(loaded, blocks, scaled)):
                    dqh = (dqds[n] + cat(dqd_i[n])) * eb
                    dk = dkds[n] * enb + cat(dke[n]) * erest
                    carry_rows = {j: jnp.broadcast_to(_colsum(g_at[n][j] * afters[n][j]), (A_CHUNK, HEAD))
                                  for j in range(cpb)}
                    dlf = _dot_split(_ones(triu if d == 0 else tril), qh * dqh - k * dk) + cat(carry_rows)
                    common = dlf / big_f - dk
                    results.append((d, rows, (k * sf * common).astype(BF16), dqh.astype(BF16),
                                    (dvs[n] + cat(dv_i[n])).astype(BF16)))
                    new.append(_colsum((1.0 - sf) * common))
                for d, rows, df16, dq16, dv16 in results:
                    res_s[1 + d, rows, :] = df16
                    dq_s[d, rows, :] = dq16
                    dv_s[d, rows, :] = dv16
                per_dir = [sum(c for (d, *_), c in zip(loaded, new) if d == dd) for dd in range(2)]
                return gts[0], gts[1], dlb + jnp.concatenate(per_dir, axis=0)

            dlb_ref[...] = lax.fori_loop(0, nb // ubk, pass_back,
                                         (zero_state, zero_state, jnp.zeros((2, HEAD), F32)))[2]

            def pass_out(i, carry):
                rows = pl.ds(pl.multiple_of(i * HEAD, HEAD), HEAD)
                dq = dq_s[0, rows, :].astype(F32) + dq_s[1, rows, :].astype(F32)
                res_s[0, rows, :] = (dq * q_scale).astype(BF16)
                res_s[3, rows, :] = (dv_s[0, rows, :].astype(F32) + dv_s[1, rows, :].astype(F32)).astype(BF16)
                return carry

            lax.fori_loop(0, nb, pass_out, 0)

        out_ref[...] = res_s[p]

        @pl.when((h == heads - 1) & (p == 4))
        def _():
            exchange.finish(xin_refs, xout_refs, sem_refs)

    def col(pp):
        return pl.BlockSpec((t, HEAD), lambda h, p: (0, pp * heads + h))

    n_in = dproj.shape[1]
    any_spec = pl.BlockSpec(memory_space=pl.ANY)
    results = pl.pallas_call(
        body, name="hgrn_bwd", grid=(heads, 5),
        in_specs=[col(0), col(1), col(2), col(3), col(4),
                  pl.BlockSpec((t, HEAD), lambda h, p: (0, h)), pl.BlockSpec((t, HEAD), lambda h, p: (0, h)),
                  pl.BlockSpec((2, HEAD), lambda h, p: (0, h)), pl.BlockSpec((1, HEAD), lambda h, p: (0, 0)),
                  any_spec] + [any_spec] * (xin + 1),
        out_specs=[pl.BlockSpec((t, HEAD), lambda h, p: (0, p * heads + h)),
                   pl.BlockSpec((None, 1, HEAD), lambda h, p: (h, 0, 0)),
                   pl.BlockSpec((2, HEAD), lambda h, p: (0, h))] + [any_spec] * xout,
        out_shape=[jax.ShapeDtypeStruct((t, n_in), BF16), jax.ShapeDtypeStruct((heads, 1, HEAD), F32),
                   jax.ShapeDtypeStruct((2, width), F32)] + list(exchange.out_shapes),
        scratch_shapes=[pltpu.VMEM((t, HEAD), F32), pltpu.VMEM((2, t, HEAD), BF16), pltpu.VMEM((2, t, HEAD), BF16),
                        pltpu.VMEM((5, t, HEAD), BF16), pltpu.VMEM((2, nb, HEAD, HEAD), F32)] + list(exchange.sems),
        input_output_aliases={9: 0},
        compiler_params=_params(2),
    )(proj, proj, proj, proj, proj, osum, dout_a, lb, g_norm, dproj, *exchange.arrays, after)
    return results[0], results[1], results[2], results[3:]


def _adamw(w, g, m, v):
    m = ADAM_B1 * m + (1.0 - ADAM_B1) * g
    v = ADAM_B2 * v + (1.0 - ADAM_B2) * (g * g)
    m_hat = m / (1.0 - ADAM_B1 ** ADAM_STEP)
    v_hat = v / (1.0 - ADAM_B2 ** ADAM_STEP)
    delta = -ADAM_LR * (m_hat / (jnp.sqrt(v_hat) + ADAM_EPS) + ADAM_WD * w)
    return delta, m, v


def _adamw_big(name, me, w, m, v, g_parts, landing, axis):
    r, c = w.shape
    tr = _tile(r, 128)
    n_parts = len(g_parts)
    per = N_DEV // n_parts

    def body(me_ref, w_ref, m_ref, v_ref, *rest):
        g_refs, (l_ref, og_ref, od_ref, om_ref, ov_ref) = rest[:n_parts], rest[n_parts:]
        g = g_refs[0][...]
        for p in range(1, n_parts):
            g = jnp.where(me_ref[0] // per == p, g_refs[p][...], g)
        for s in range(N_DEV - 1):
            g = g + l_ref[s].astype(F32)
        og_ref[...] = g
        od_ref[...], om_ref[...], ov_ref[...] = _adamw(w_ref[...], g, m_ref[...], v_ref[...])

    shard = pl.BlockSpec((tr, c), lambda i, me_ref: (i, 0))
    if axis == 1:
        own = pl.BlockSpec((tr, c), lambda i, me_ref: (i, me_ref[0] % per))
    else:
        assert n_parts == 1
        own = pl.BlockSpec((tr, c), lambda i, me_ref: (me_ref[0] * (r // tr) + i, 0))
    grid_spec = pltpu.PrefetchScalarGridSpec(
        num_scalar_prefetch=1, grid=(r // tr,),
        in_specs=[shard, shard, shard] + [own] * n_parts + [pl.BlockSpec((N_DEV - 1, tr, c), lambda i, me_ref: (0, i, 0))],
        out_specs=[shard] * 4)
    return pl.pallas_call(
        body, name=name, grid_spec=grid_spec, out_shape=[jax.ShapeDtypeStruct((r, c), F32)] * 4,
        compiler_params=_params(1),
    )(me, w, m, v, *g_parts, landing)


def _adamw_ada(sct, dmod_mine, w, m, v):
    d, n = w.shape
    tr = _tile(d, 256)

    def body(s_ref, dm_ref, w_ref, m_ref, v_ref, og_ref, od_ref, om_ref, ov_ref):
        g = _dot(s_ref[...], dm_ref[...], precision=HIGHEST)
        og_ref[...] = g
        od_ref[...], om_ref[...], ov_ref[...] = _adamw(w_ref[...], g, m_ref[...], v_ref[...])

    blk = pl.BlockSpec((tr, n), lambda i: (i, 0))
    return pl.pallas_call(
        body, name="adamw_ada", grid=(d // tr,),
        in_specs=[pl.BlockSpec((tr, N_DEV), lambda i: (i, 0)), pl.BlockSpec((N_DEV, n), lambda i: (0, 0)), blk, blk, blk],
        out_specs=[blk] * 4, out_shape=[jax.ShapeDtypeStruct((d, n), F32)] * 4, compiler_params=_params(1),
    )(sct, dmod_mine, w, m, v)


def _adamw_small(gathered, w, m, v):
    def body(g_ref, w_ref, m_ref, v_ref, og_ref, od_ref, om_ref, ov_ref):
        g = g_ref[0]
        for s in range(1, N_DEV):
            g = g + g_ref[s]
        og_ref[...] = g
        od_ref[...], om_ref[...], ov_ref[...] = _adamw(w_ref[...], g, m_ref[...], v_ref[...])

    return pl.pallas_call(
        body, name="adamw_small", out_shape=[jax.ShapeDtypeStruct(w.shape, F32)] * 4,
        compiler_params=pltpu.CompilerParams(vmem_limit_bytes=VMEM_LIMIT),
    )(gathered, w, m, v)


def _adamw_lb(dlb_mine, lb_logits, m, v):
    def body(d_ref, l_ref, m_ref, v_ref, og_ref, od_ref, om_ref, ov_ref):
        dlb = d_ref[0]
        for s in range(1, N_DEV):
            dlb = dlb + d_ref[s]
        for dr in range(2):
            lb = _sigmoid(l_ref[dr][0:1, :] - l_ref[dr][1:2, :])
            d0 = dlb[dr:dr + 1] * lb * (1.0 - lb)
            g = jnp.concatenate([d0, -d0], axis=0)
            og_ref[dr] = g
            od_ref[dr], om_ref[dr], ov_ref[dr] = _adamw(l_ref[dr], g, m_ref[dr], v_ref[dr])

    return pl.pallas_call(body, name="adamw_lb", out_shape=[jax.ShapeDtypeStruct(lb_logits.shape, F32)] * 4,
                          )(dlb_mine, lb_logits, m, v)


def _rows(a, pad_to=8):
    flat = a.reshape(-1, LANE)
    pad = (-flat.shape[0]) % pad_to
    return jnp.pad(flat, ((0, pad), (0, 0))) if pad else flat


def kernel(x, c, w_ada, b_ada, g_pre_mix, g_post_mix, g_pre_ffn, g_post_ffn, w_in, lb_logits, g_hgrn_norm, w_a_out, g_sgu_norm, w_spatial, b_spatial, w_b_out, w_o, w_ff1, w_ff2, loss_target, m_w_ada, m_b_ada, m_g_pre_mix, m_g_post_mix, m_g_pre_ffn, m_g_post_ffn, m_w_in, m_lb_logits, m_g_hgrn_norm, m_w_a_out, m_g_sgu_norm, m_w_spatial, m_b_spatial, m_w_b_out, m_w_o, m_w_ff1, m_w_ff2, v_w_ada, v_b_ada, v_g_pre_mix, v_g_post_mix, v_g_pre_ffn, v_g_post_ffn, v_w_in, v_lb_logits, v_g_hgrn_norm, v_w_a_out, v_g_sgu_norm, v_w_spatial, v_b_spatial, v_w_b_out, v_w_o, v_w_ff1, v_w_ff2):
    t, d = x.shape[1], x.shape[2]
    n_in = w_in.shape[2] * N_DEV
    width = (n_in - 2 * d) // 7
    heads = width // HEAD
    assert heads == N_DEV and width % LANE == 0
    d_ff = w_ff1.shape[2] * N_DEV
    n_ada = w_ada.shape[2]
    me = _dev_index()
    me_arr = me.reshape(1).astype(jnp.int32)
    x2, tgt = x[0], loss_target[0]

    big = [w_in[0], w_a_out[0], w_b_out[0], w_o[0], w_ff1[0], w_ff2[0]]
    big_axes = [1, 1, 1, 0, 1, 0]
    big_names = ["w_in", "w_a_out", "w_b_out", "w_o", "w_ff1", "w_ff2"]
    wf_in, = _run_exchange("gather_w_in", _gather_plan([_cast_bf16("cast_w_in", big[0])], big_axes[:1]))
    own_parts = [_cast_into_full("cast_" + nm, me_arr, w, ax) for nm, w, ax in zip(big_names[1:], big[1:], big_axes[1:])]
    wf_in, own_parts = lax.optimization_barrier((wf_in, own_parts))
    gathers = {}
    for key, lo, hi in (("mid", 1, 4), ("ff1", 4, 5), ("ff2", 5, 6)):
        plan = _direct_gather_plan(own_parts[lo - 1:hi - 1], big_axes[lo:hi])
        gathers[key] = (plan,) + _split_start("gather_%s_start" % key, plan, landing=own_parts[lo - 1:hi - 1])

    def gathered_weights(key, after):
        plan, sems, thru, _ = gathers[key]
        return _split_wait("gather_%s_wait" % key, plan, sems, thru, after)[1]

    c_rows = d // LANE
    small = _all_gather_small("gather_c_lb", _prep_small(c[0:1], lb_logits))
    sc_all = small[:, :c_rows, :].reshape(N_DEV, d)
    lb = jnp.transpose(small[:, c_rows:c_rows + 2, :], (1, 0, 2)).reshape(2, width)
    b_shard = lax.dynamic_slice_in_dim(b_ada, me * n_ada, n_ada, axis=1)
    mod_sh = _mod_shard(sc_all, w_ada[0], b_shard)
    mod_all = _all_gather_small("gather_mod", _rows(mod_sh))
    mod_all = mod_all[:, :N_DEV * n_ada // LANE, :].reshape(N_DEV, N_DEV, n_ada)
    mod6 = lax.dynamic_index_in_dim(mod_all, me, axis=1, keepdims=False).reshape(N_MOD, d)
    sh1, sc1, gt1, sh2, sc2, gt2 = [mod6[i:i + 1] for i in range(N_MOD)]

    a1 = _norm_mod(x2, g_pre_mix, sh1, sc1)
    tm = _tile(t, 512)

    def store_f32(acc, i, j, extra_refs, out_refs, rows):
        out_refs[0][...] = acc

    tn_in = _tile(n_in, 1024)
    started = [tok for key in ("mid", "ff1", "ff2") for tok in _after(gathers[key][3])]
    proj, = _mm("proj", a1, wf_in, _NN, t, n_in, d, tm, tn_in, d, started,
                [(jax.ShapeDtypeStruct((t, n_in), F32), (tm, tn_in), lambda i, j: (i, j))], store_f32)

    out_a, osum, _ = _hgrn_fwd(proj, lb, g_hgrn_norm, width, _NO_EXCHANGE)
    z_block = 5
    bst = b_spatial[0].T
    out_b = _sgu_fwd(proj, g_sgu_norm, w_spatial[0], bst, width, z_block)
    wf_a, wf_b, wf_o = gathered_weights("mid", out_b)

    tn_d = _tile(d, 512)
    blk_d = ((tm, tn_d), lambda i, j: (i, j))
    y_a, = _mm("y_a", out_a, wf_a, _NN, t, d, width, tm, tn_d, width, [],
               [(jax.ShapeDtypeStruct((t, d), F32),) + blk_d], store_f32)
    ga_blk = (5 * width + 2 * width) // tn_d
    gb_blk = ga_blk + d // tn_d

    def merge(acc, i, j, extra_refs, out_refs, rows):
        ga, gb, ya = extra_refs
        out_refs[0][...] = acc
        out_refs[1][...] = (_sigmoid(ga[...]) * ya[...] + _sigmoid(gb[...]) * acc).astype(BF16)

    y_b, merged = _mm("y_b_merge", out_b, wf_b, _NN, t, d, width, tm, tn_d, width,
                      [(proj, (tm, tn_d), lambda i, j: (i, ga_blk + j)), (proj, (tm, tn_d), lambda i, j: (i, gb_blk + j)),
                       (y_a,) + blk_d],
                      [(jax.ShapeDtypeStruct((t, d), F32),) + blk_d, (jax.ShapeDtypeStruct((t, d), BF16),) + blk_d], merge)

    tr = _tile(t, 512)
    rc = 32 if tr % 32 == 0 else None
    row_d = ((tr, d), lambda i, j: (i, 0))
    vec_d = ((1, d), lambda i, j: (0, 0))

    def post_mix(acc, i, j, extra_refs, out_refs, rows):
        x_r, gt1_r, g2_r, g3_r, sc2_r, sh2_r = extra_refs
        h1 = x_r[rows, :] + gt1_r[...] * (acc * _rms(acc) * g2_r[...])
        out_refs[0][rows, :] = acc
        out_refs[1][rows, :] = h1
        out_refs[2][rows, :] = ((h1 * _rms(h1) * g3_r[...]) * (1.0 + sc2_r[...]) + sh2_r[...]).astype(BF16)

    mo, h1, a2 = _mm("w_o_post_mix", merged, wf_o, _NN, t, d, d, tr, d, d,
                     [(x2,) + row_d, (gt1,) + vec_d, (g_post_mix,) + vec_d, (g_pre_ffn,) + vec_d, (sc2,) + vec_d, (sh2,) + vec_d],
                     [(jax.ShapeDtypeStruct((t, d), F32),) + row_d, (jax.ShapeDtypeStruct((t, d), F32),) + row_d,
                      (jax.ShapeDtypeStruct((t, d), BF16),) + row_d], post_mix, row_chunk=rc)

    tn_f = _tile(d_ff, 1024)
    blk_f = ((tm, tn_f), lambda i, j: (i, j))

    def relu_sq(acc, i, j, extra_refs, out_refs, rows):
        r = jnp.maximum(acc, 0.0)
        out_refs[0][...] = acc.astype(BF16)
        out_refs[1][...] = (r * r).astype(BF16)

    wf_1, = gathered_weights("ff1", a2)
    hff, act = _mm(
        "ff1", a2, wf_1, _NN, t, d_ff, d, tm, tn_f, d, [],
        [(jax.ShapeDtypeStruct((t, d_ff), BF16),) + blk_f, (jax.ShapeDtypeStruct((t, d_ff), BF16),) + blk_f], relu_sq)
    wf_2, = gathered_weights("ff2", act)

    sums_d = ((8, d), lambda i, j: (0, 0))

    def zero_first(sums_r, i, rows):
        if rows.start in (None, 0):
            @pl.when(i == 0)
            def _():
                sums_r[...] = jnp.zeros_like(sums_r)

    def loss_head(acc, i, j, extra_refs, out_refs, rows):
        h1_r, tgt_r, gt2_r, g4_r = extra_refs
        dy_r, dff_r, sums_r = out_refs
        r4 = _rms(acc)
        ffn = acc * r4
        n4 = ffn * g4_r[...]
        err = h1_r[rows, :] + gt2_r[...] * n4 - tgt_r[rows, :]
        dy = err * (1.0 / d)
        dy_r[rows, :] = dy
        dn4 = dy * gt2_r[...]
        dffn = dn4 * g4_r[...]
        dff_r[rows, :] = (r4 * (dffn - ffn * jnp.mean(dffn * ffn, axis=-1, keepdims=True))).astype(BF16)
        zero_first(sums_r, i, rows)

        sums_r[0:1, :] += _colsum(err * err)
        sums_r[1:2, :] += _colsum(dy * n4)
        sums_r[2:3, :] += _colsum(dn4 * ffn)

    tk_f = _tile(d_ff, 1024)
    dy, dff, sums_f = _mm("ff2_loss", act, wf_2, _NN, t, d, d_ff, tr, d, tk_f,
                          [(h1,) + row_d, (tgt,) + row_d, (gt2,) + vec_d, (g_post_ffn,) + vec_d],
                          [(jax.ShapeDtypeStruct((t, d), F32),) + row_d, (jax.ShapeDtypeStruct((t, d), BF16),) + row_d,
                           (jax.ShapeDtypeStruct((8, d), F32),) + sums_d], loss_head, row_chunk=rc)
    loss_mine = (0.5 / d) * jnp.sum(sums_f[0])

    def relu_sq_bwd(acc, i, j, extra_refs, out_refs, rows):
        out_refs[0][...] = (acc * (2.0 * jnp.maximum(extra_refs[0][...].astype(F32), 0.0))).astype(BF16)

    dhff, = _mm("d_hff", dff, wf_2, _NT, t, d_ff, d, tm, tn_f, d, [(hff,) + blk_f],
                [(jax.ShapeDtypeStruct((t, d_ff), BF16),) + blk_f], relu_sq_bwd)
    scatters = {}

    def send_grads(key, grads16, axes):
        plan = _scatter_plan(grads16, axes)
        scatters[key] = (plan,) + _split_start("scatter_%s_start" % key, plan)
        return scatters[key][3]

    def received_grads(key, after):
        plan, sems, thru, _ = scatters[key]
        return _split_wait("scatter_%s_wait" % key, plan, sems, thru, after)[1]

    gw_ff2, gw_ff2_16 = _grad_w("grad_w_ff2", act, dff)
    sent_ff2 = send_grads("ff2", [gw_ff2_16], big_axes[5:6])
    gw_ff1, gw_ff1_16 = _grad_w("grad_w_ff1", a2, dhff, token=sent_ff2)
    sent_ff1 = send_grads("ff1", [gw_ff1_16], big_axes[4:5])

    def pre_ffn_bwd(acc, i, j, extra_refs, out_refs, rows):
        h1_r, dy_r, mo_r, sc2_r, g3_r, gt1_r, g2_r = extra_refs[:7]
        dh1_r, dmo_r, sums_r = out_refs
        h1v = h1_r[rows, :]
        r3 = _rms(h1v)
        h1n = h1v * r3
        dn3 = acc * (1.0 + sc2_r[...])
        dh1n = dn3 * g3_r[...]
        dh1 = dy_r[rows, :] + r3 * (dh1n - h1n * jnp.mean(dh1n * h1n, axis=-1, keepdims=True))
        dh1_r[rows, :] = dh1
        mov = mo_r[rows, :]
        r2 = _rms(mov)
        mon = mov * r2
        dn2 = dh1 * gt1_r[...]
        dmon = dn2 * g2_r[...]
        dmo_r[rows, :] = (r2 * (dmon - mon * jnp.mean(dmon * mon, axis=-1, keepdims=True))).astype(BF16)
        zero_first(sums_r, i, rows)

        sums_r[0:1, :] += _colsum(acc)
        sums_r[1:2, :] += _colsum(acc * (h1n * g3_r[...]))
        sums_r[2:3, :] += _colsum(dn3 * h1n)
        sums_r[3:4, :] += _colsum(dh1 * (mon * g2_r[...]))
        sums_r[4:5, :] += _colsum(dn2 * mon)

    dh1, dmo, sums_m = _mm("d_a2_pre_ffn", dhff, wf_1, _NT, t, d, d_ff, tr, d, tk_f,
                           [(h1,) + row_d, (dy,) + row_d, (mo,) + row_d, (sc2,) + vec_d, (g_pre_ffn,) + vec_d,
                            (gt1,) + vec_d, (g_post_mix,) + vec_d] + _after(sent_ff1),
                           [(jax.ShapeDtypeStruct((t, d), F32),) + row_d, (jax.ShapeDtypeStruct((t, d), BF16),) + row_d,
                            (jax.ShapeDtypeStruct((8, d), F32),) + sums_d], pre_ffn_bwd, row_chunk=rc)
    gw_o, gw_o_16 = _grad_w("grad_w_o", merged, dmo)

    n_j = d // tn_d

    def merge_bwd_body(dmo_ref, wo_ref, ga_ref, gb_ref, ya_ref, yb_ref, dya_ref, dyb_ref, dproj_ref, acc_s):
        g = pl.program_id(2)

        @pl.when(g == 0)
        def _():
            dm = _dot(dmo_ref[...], wo_ref[...], _NT)
            acc_s[...] = dm
            sa = _sigmoid(ga_ref[...])
            dya_ref[...] = (dm * sa).astype(BF16)
            dproj_ref[...] = (dm * ya_ref[...] * sa * (1.0 - sa)).astype(BF16)

        @pl.when(g == 1)
        def _():
            dm = acc_s[...]
            sb = _sigmoid(gb_ref[...])
            dyb_ref[...] = (dm * sb).astype(BF16)
            dproj_ref[...] = (dm * yb_ref[...] * sb * (1.0 - sb)).astype(BF16)

    tile3 = pl.BlockSpec((tm, tn_d), lambda i, j, g: (i, j))
    dy_a, dy_b, dproj = pl.pallas_call(
        merge_bwd_body, name="d_merged", grid=(t // tm, n_j, 2),
        in_specs=[pl.BlockSpec((tm, d), lambda i, j, g: (i, 0)), pl.BlockSpec((tn_d, d), lambda i, j, g: (j, 0)),
                  pl.BlockSpec((tm, tn_d), lambda i, j, g: (i, ga_blk + j)),
                  pl.BlockSpec((tm, tn_d), lambda i, j, g: (i, gb_blk + j)), tile3, tile3],
        out_specs=[tile3, tile3, pl.BlockSpec((tm, tn_d), lambda i, j, g: (i, ga_blk + g * n_j + j))],
        out_shape=[jax.ShapeDtypeStruct((t, d), BF16), jax.ShapeDtypeStruct((t, d), BF16),
                   jax.ShapeDtypeStruct((t, n_in), BF16)],
        scratch_shapes=[pltpu.VMEM((tm, tn_d), F32)], compiler_params=_params(3),
    )(dmo, wf_o, proj, proj, y_a, y_b)

    def store_bf16(acc, i, j, extra_refs, out_refs, rows):
        out_refs[0][...] = acc.astype(BF16)

    tn_w = _tile(width, 512)
    blk_w = ((tm, tn_w), lambda i, j: (i, j))
    dout_a, = _mm("d_out_a", dy_a, wf_a, _NT, t, width, d, tm, tn_w, d, [],
                  [(jax.ShapeDtypeStruct((t, width), BF16),) + blk_w], store_bf16)
    dout_b, = _mm("d_out_b", dy_b, wf_b, _NT, t, width, d, tm, tn_w, d, [],
                  [(jax.ShapeDtypeStruct((t, width), BF16),) + blk_w], store_bf16)
    gw_a, gw_a_16 = _grad_w("grad_w_a_out", out_a, dy_a)
    gw_b, gw_b_16 = _grad_w("grad_w_b_out", out_b, dy_b)

    w_st = jnp.swapaxes(w_spatial[0], 1, 2)
    dproj, dg_sgu, dw_sp, dbst = _sgu_bwd(proj, dout_b, dproj, g_sgu_norm, w_spatial[0], w_st, bst, width, z_block)
    sent_mid = send_grads("mid", [gw_a_16, gw_b_16, gw_o_16], big_axes[1:4])
    half = n_in // 2
    assert half >= 5 * width, "the upper half of w_in's columns must lie past the HGRN2 columns"
    gw_in_hi, gw_in_hi16 = _grad_w("grad_w_in_hi", a1, dproj, token=sent_mid, cols=(half, half))
    plan_hi = _scatter_half_plan(gw_in_hi16, 1)
    scatters["in_hi"] = (plan_hi,) + _split_start("scatter_in_hi_start", plan_hi)
    dproj, dgh_heads, dlb, _ = _hgrn_bwd(proj, osum, dout_a, dproj, lb, g_hgrn_norm, width, _NO_EXCHANGE,
                                         after=scatters["in_hi"][3])
    gw_in_lo, gw_in_lo16 = _grad_w("grad_w_in_lo", a1, dproj, cols=(0, half))
    land_in_hi, = received_grads("in_hi", gw_in_lo)
    plan_lo = _scatter_half_plan(gw_in_lo16, 0)
    scatters["in_lo"] = (plan_lo,) + _split_start("scatter_in_lo_start", plan_lo, landing=[land_in_hi])
    sent_in = scatters["in_lo"][3]

    def pre_mix_bwd(acc, i, j, extra_refs, out_refs, rows):
        x_r, dh1_r, sc1_r, g1_r = extra_refs[:4]
        dx_r, sums_r = out_refs
        xv = x_r[rows, :]
        r1 = _rms(xv)
        xn = xv * r1
        dn1 = acc * (1.0 + sc1_r[...])
        dxn = dn1 * g1_r[...]
        dx_r[rows, :] = dh1_r[rows, :] + r1 * (dxn - xn * jnp.mean(dxn * xn, axis=-1, keepdims=True))
        zero_first(sums_r, i, rows)

        sums_r[0:1, :] += _colsum(acc)
        sums_r[1:2, :] += _colsum(acc * (xn * g1_r[...]))
        sums_r[2:3, :] += _colsum(dn1 * xn)

    tk_in = _tile(n_in, 1024)
    grad_x, sums_x = _mm(
        "d_a1_pre_mix", dproj, wf_in, _NT, t, d, n_in, tr, d, tk_in,
        [(x2,) + row_d, (dh1,) + row_d, (sc1,) + vec_d, (g_pre_mix,) + vec_d] + _after(sent_in),
        [(jax.ShapeDtypeStruct((t, d), F32),) + row_d, (jax.ShapeDtypeStruct((8, d), F32),) + sums_d],
        pre_mix_bwd, row_chunk=rc)

    dmod = jnp.concatenate([sums_x[0:2], sums_m[3:4], sums_m[0:2], sums_f[1:2]], axis=0).reshape(N_DEV, n_ada // LANE, LANE)
    ada_rows = -(-(n_ada // LANE) // 8) * 8
    dmod = jnp.pad(dmod, ((0, 0), (0, ada_rows - n_ada // LANE), (0, 0))).reshape(N_DEV * ada_rows, LANE)
    parts = [dmod, _rows(sums_x[2:3]), _rows(sums_m[4:5]), _rows(sums_m[2:3]), _rows(sums_f[2:3]),
             _rows(jnp.sum(dgh_heads, axis=0)), _rows(dg_sgu), _rows(dw_sp), _rows(dbst.T)]
    n_params = sum(p.shape[0] for p in parts)
    parts.append(jnp.full((8, LANE), loss_mine, F32))
    n_common = n_params + 8
    payload = jnp.concatenate(parts + [_rows(dlb)], axis=0)

    moms = [m_w_in, m_w_a_out, m_w_b_out, m_w_o, m_w_ff1, m_w_ff2]
    vars_ = [v_w_in, v_w_a_out, v_w_b_out, v_w_o, v_w_ff1, v_w_ff2]
    big_out = {}

    def big_update(nm, g_full, landing):
        k = big_names.index(nm)
        outs = _adamw_big("adamw_" + nm, me_arr, big[k], moms[k][0], vars_[k][0], g_full, landing, big_axes[k])
        big_out[nm] = [o[None] for o in outs]
        return outs[0]

    land_ff2, = received_grads("ff2", grad_x)
    done = big_update("w_ff2", [gw_ff2], land_ff2)
    land_ff1, = received_grads("ff1", done)
    done = big_update("w_ff1", [gw_ff1], land_ff1)
    land_a, land_b, land_o = received_grads("mid", done)
    big_update("w_a_out", [gw_a], land_a)
    big_update("w_b_out", [gw_b], land_b)
    done = big_update("w_o", [gw_o], land_o)

    payload, _ = lax.optimization_barrier((payload, done))
    gathered = _all_gather_small("gather_small_grads", payload)

    dmod_mine = lax.dynamic_slice_in_dim(gathered[:, :N_DEV * ada_rows, :].reshape(N_DEV, N_DEV, ada_rows * LANE),
                                         me, 1, axis=1)[:, 0, :n_ada]
    ada_out = [o[None] for o in _adamw_ada(sc_all.T, dmod_mine, w_ada[0], m_w_ada[0], v_w_ada[0])]

    def pack(b_, g1_, g2_, g3_, g4_, gh_, gs_, ws_, bs_):
        b3 = b_.reshape(N_DEV, n_ada // LANE, LANE)
        b3 = jnp.pad(b3, ((0, 0), (0, ada_rows - n_ada // LANE), (0, 0))).reshape(N_DEV * ada_rows, LANE)
        return jnp.concatenate([b3, _rows(g1_), _rows(g2_), _rows(g3_), _rows(g4_), _rows(gh_), _rows(gs_),
                                _rows(ws_), _rows(bs_), jnp.zeros((8, LANE), F32)], axis=0)

    small_w = (b_ada, g_pre_mix, g_post_mix, g_pre_ffn, g_post_ffn, g_hgrn_norm, g_sgu_norm, w_spatial, b_spatial)
    small_m = (m_b_ada, m_g_pre_mix, m_g_post_mix, m_g_pre_ffn, m_g_post_ffn, m_g_hgrn_norm, m_g_sgu_norm, m_w_spatial, m_b_spatial)
    small_v = (v_b_ada, v_g_pre_mix, v_g_post_mix, v_g_pre_ffn, v_g_post_ffn, v_g_hgrn_norm, v_g_sgu_norm, v_w_spatial, v_b_spatial)
    packed = _adamw_small(gathered[:, :n_common, :], pack(*small_w), pack(*small_m), pack(*small_v))

    def unpack(slab):
        outs, at = [], 0
        b3 = slab[:N_DEV * ada_rows].reshape(N_DEV, ada_rows, LANE)[:, :n_ada // LANE, :]
        outs.append(b3.reshape(b_ada.shape))
        at = N_DEV * ada_rows
        for ref in small_w[1:]:
            n_el = ref.size
            n_r = -(-(n_el // LANE) // 8) * 8
            outs.append(slab[at:at + n_el // LANE].reshape(ref.shape))
            at += n_r
        return outs

    small_out = [unpack(s) for s in packed]
    loss = packed[0][n_params, 0]

    dlb_all = gathered[:, n_common:n_common + 2 * heads, :].reshape(N_DEV, 2, heads, LANE)
    dlb_mine = lax.dynamic_index_in_dim(dlb_all, me, axis=2, keepdims=False)
    lb_out = _adamw_lb(dlb_mine, lb_logits, m_lb_logits, v_lb_logits)

    land_in, = received_grads("in_lo", ada_out[0])
    big_update("w_in", [gw_in_lo, gw_in_hi], land_in)

    order = ["w_ada", "b_ada", "g_pre_mix", "g_post_mix", "g_pre_ffn", "g_post_ffn", "w_in", "lb_logits", "g_hgrn_norm",
             "w_a_out", "g_sgu_norm", "w_spatial", "b_spatial", "w_b_out", "w_o", "w_ff1", "w_ff2"]
    small_names = ["b_ada", "g_pre_mix", "g_post_mix", "g_pre_ffn", "g_post_ffn", "g_hgrn_norm", "g_sgu_norm", "w_spatial", "b_spatial"]

    def leaf(kind, nm):
        if nm == "w_ada":
            return ada_out[kind]
        if nm == "lb_logits":
            return lb_out[kind]
        if nm in big_out:
            return big_out[nm][kind]
        return small_out[kind][small_names.index(nm)]

    result = [loss, grad_x[None]]
    for kind in range(4):
        result += [leaf(kind, nm) for nm in order]
    return tuple(result)
```

```python
import functools
import math

import jax
import jax.numpy as jnp
from jax import lax
from jax.experimental import pallas as pl
from jax.experimental.pallas import tpu as pltpu

F32 = jnp.float32
BF16 = jnp.bfloat16
MESH = pl.DeviceIdType.MESH
HIGHEST = lax.Precision.HIGHEST

N_DEV = 8
HEAD = 128
A_CHUNK = 32
N_MOD = 6
EPS = 1e-6
LANE = 128
VMEM_LIMIT = 60 * 1024 * 1024

ADAM_LR = 0.001
ADAM_B1 = 0.9
ADAM_B2 = 0.999
ADAM_EPS = 1e-08
ADAM_WD = 0.01
ADAM_STEP = 10

_NN = (((1,), (0,)), ((), ()))
_NT = (((1,), (1,)), ((), ()))
_TN = (((0,), (0,)), ((), ()))


def _dot(a, b, dims=_NN, precision=None):
    return lax.dot_general(a, b, dims, preferred_element_type=F32, precision=precision)


def _bdot(a, b, dims=_NN):
    return _dot(a.astype(BF16), b.astype(BF16), dims)


def _params(n_grid):
    return pltpu.CompilerParams(dimension_semantics=("arbitrary",) * n_grid, vmem_limit_bytes=VMEM_LIMIT)


def _dev_index():
    return lax.axis_index("x") * 4 + lax.axis_index("y") * 2 + lax.axis_index("c")


def _dev_coords(i):
    return (i // 4, (i // 2) % 2, i % 2)


def _sigmoid(x):
    return 1.0 / (1.0 + jnp.exp(-x))


def _erf(x):
    ax = jnp.abs(x)
    t = 1.0 / (1.0 + 0.3275911 * ax)
    poly = ((((1.061405429 * t - 1.453152027) * t + 1.421413741) * t - 0.284496736) * t + 0.254829592) * t
    y = 1.0 - poly * jnp.exp(-ax * ax)
    return jnp.where(x < 0, -y, y)


def _gelu_and_grad(x):
    cdf = 0.5 * (1.0 + _erf(x * (2.0 ** -0.5)))
    pdf = jnp.exp(-0.5 * x * x) * (1.0 / math.sqrt(2.0 * math.pi))
    return x * cdf, cdf + x * pdf


def _rms(x):
    return lax.rsqrt(jnp.mean(x * x, axis=-1, keepdims=True) + EPS)


def _colsum(x):
    return jnp.sum(x, axis=0, keepdims=True)


def _tile(n, want):
    if n <= want:
        return n
    t = (want // LANE) * LANE
    while n % t:
        t -= LANE
    assert t > 0, (n, want)
    return t


def _all_gather_small(name, payload):
    rows = payload.shape[0]

    def body(p_ref, out_ref, send_sems, recv_sems, local_sem):
        me = _dev_index()
        mine = pltpu.make_async_copy(p_ref, out_ref.at[me], local_sem)
        mine.start()
        sends = []
        for r in range(1, N_DEV):
            peer = (me + r) % N_DEV
            cp = pltpu.make_async_remote_copy(
                src_ref=p_ref, dst_ref=out_ref.at[me], send_sem=send_sems.at[r - 1], recv_sem=recv_sems.at[r - 1],
                device_id=_dev_coords(peer), device_id_type=MESH)
            cp.start()
            sends.append(cp)
        for r in range(1, N_DEV):
            src = (me + N_DEV - r) % N_DEV
            pltpu.make_async_remote_copy(
                src_ref=p_ref, dst_ref=out_ref.at[src], send_sem=send_sems.at[r - 1], recv_sem=recv_sems.at[r - 1],
                device_id=_dev_coords(src), device_id_type=MESH).wait_recv()
        for cp in sends:
            cp.wait_send()
        mine.wait()

    return pl.pallas_call(
        body, name=name,
        out_shape=jax.ShapeDtypeStruct((N_DEV, rows, LANE), F32),
        in_specs=[pl.BlockSpec(memory_space=pltpu.VMEM)],
        out_specs=pl.BlockSpec(memory_space=pltpu.VMEM),
        scratch_shapes=[pltpu.SemaphoreType.DMA((N_DEV - 1,)), pltpu.SemaphoreType.DMA((N_DEV - 1,)),
                        pltpu.SemaphoreType.DMA],
        compiler_params=pltpu.CompilerParams(vmem_limit_bytes=VMEM_LIMIT),
    )(payload)


def _region(ref, dev, axis, n):
    start = pl.multiple_of(dev * n, LANE if axis == 1 else 16)
    return ref.at[:, pl.ds(start, n)] if axis == 1 else ref.at[pl.ds(start, n), :]


class _Exchange:
    def __init__(self, arrays, out_shapes, sems, start, finish):
        self.arrays, self.out_shapes, self.sems, self.start, self.finish = arrays, out_shapes, sems, start, finish


def _gather_plan(shards, axes):
    n_w = len(shards)
    fulls = []
    for s, ax in zip(shards, axes):
        shp = (s.shape[0], s.shape[1] * N_DEV) if ax == 1 else (s.shape[0] * N_DEV, s.shape[1])
        fulls.append(jax.ShapeDtypeStruct(shp, BF16))
    widths = [s.shape[ax] for s, ax in zip(shards, axes)]

    def places():
        x, y, c = lax.axis_index("x"), lax.axis_index("y"), lax.axis_index("c")
        chips = [(1 - x, y), (x, 1 - y), (1 - x, 1 - y)]
        return (x, y, c), (x, y, 1 - c), chips

    def index(p):
        return p[0] * 4 + p[1] * 2 + p[2]

    def copy(w, k, s_refs, f_refs, sems, block, to, from_shard):
        send_sems, recv_sems, _ = sems
        dst = _region(f_refs[w], index(block), axes[w], widths[w])
        return pltpu.make_async_remote_copy(
            src_ref=s_refs[w] if from_shard else dst, dst_ref=dst,
            send_sem=send_sems.at[w, k], recv_sem=recv_sems.at[w, k], device_id=to, device_id_type=MESH)

    def local(w, s_refs, f_refs, sems, me):
        return pltpu.make_async_copy(s_refs[w], _region(f_refs[w], index(me), axes[w], widths[w]), sems[2].at[w])

    def start(s_refs, f_refs, sems):
        me, sib, chips = places()
        for w in range(n_w):
            local(w, s_refs, f_refs, sems, me).start()
            copy(w, 0, s_refs, f_refs, sems, me, sib, True).start()
            for j, chip in enumerate(chips):
                copy(w, 1 + j, s_refs, f_refs, sems, me, (*chip, me[2]), True).start()

    def finish(s_refs, f_refs, sems):
        me, sib, chips = places()
        for w in range(n_w):
            for j, chip in enumerate(chips):
                copy(w, 1 + j, s_refs, f_refs, sems, (*chip, me[2]), me, True).wait_recv()
                copy(w, 4 + j, s_refs, f_refs, sems, (*chip, me[2]), sib, False).start()
        for w in range(n_w):
            copy(w, 0, s_refs, f_refs, sems, sib, me, True).wait_recv()
            for j, chip in enumerate(chips):
                copy(w, 4 + j, s_refs, f_refs, sems, (*chip, sib[2]), me, False).wait_recv()
        for w in range(n_w):
            for k in range(N_DEV - 1):
                copy(w, k, s_refs, f_refs, sems, me, sib, True).wait_send()
            local(w, s_refs, f_refs, sems, me).wait()

    sems = [pltpu.SemaphoreType.DMA((n_w, N_DEV - 1)), pltpu.SemaphoreType.DMA((n_w, N_DEV - 1)),
            pltpu.SemaphoreType.DMA((n_w,))]
    return _Exchange(list(shards), fulls, sems, start, finish)


def _scatter_plan(grads, axes):
    n_w = len(grads)
    lands = []
    for g, ax in zip(grads, axes):
        shp = (g.shape[0], g.shape[1] // N_DEV) if ax == 1 else (g.shape[0] // N_DEV, g.shape[1])
        lands.append(jax.ShapeDtypeStruct((N_DEV - 1,) + shp, BF16))
    widths = [ld.shape[1 + ax] for ld, ax in zip(lands, axes)]

    def copy(w, r, g_refs, l_refs, sems, block, to):
        return pltpu.make_async_remote_copy(
            src_ref=_region(g_refs[w], block, axes[w], widths[w]), dst_ref=l_refs[w].at[r - 1],
            send_sem=sems[0].at[w * (N_DEV - 1) + r - 1], recv_sem=sems[1].at[w * (N_DEV - 1) + r - 1],
            device_id=_dev_coords(to), device_id_type=MESH)

    def start(g_refs, l_refs, sems):
        me = _dev_index()
        for w in range(n_w):
            for r in range(1, N_DEV):
                owner = (me + r) % N_DEV
                copy(w, r, g_refs, l_refs, sems, owner, owner).start()

    def finish(g_refs, l_refs, sems):
        me = _dev_index()
        for w in range(n_w):
            for r in range(1, N_DEV):
                copy(w, r, g_refs, l_refs, sems, me, (me + N_DEV - r) % N_DEV).wait_recv()
        for w in range(n_w):
            for r in range(1, N_DEV):
                copy(w, r, g_refs, l_refs, sems, me, (me + r) % N_DEV).wait_send()

    sems = [pltpu.SemaphoreType.DMA((n_w * (N_DEV - 1),)), pltpu.SemaphoreType.DMA((n_w * (N_DEV - 1),))]
    return _Exchange(list(grads), lands, sems, start, finish)


def _places():
    x, y, c = lax.axis_index("x"), lax.axis_index("y"), lax.axis_index("c")
    return (x, y, c), (x, y, 1 - c), [(1 - x, y), (x, 1 - y), (1 - x, 1 - y)]


def _place_index(p):
    return p[0] * 4 + p[1] * 2 + p[2]


def _gather_stage_plans(fulls, axes):
    n_w = len(fulls)
    widths = [f.shape[ax] // N_DEV for f, ax in zip(fulls, axes)]
    shapes = [jax.ShapeDtypeStruct(f.shape, f.dtype) for f in fulls]

    def copy(per, w, k, f_refs, sems, block, to):
        part = _region(f_refs[w], _place_index(block), axes[w], widths[w])
        return pltpu.make_async_remote_copy(
            src_ref=part, dst_ref=part, send_sem=sems[0].at[w * per + k], recv_sem=sems[1].at[w * per + k],
            device_id=to, device_id_type=MESH)

    def start1(_, f_refs, sems):
        me, sib, chips = _places()
        for w in range(n_w):
            copy(4, w, 0, f_refs, sems, me, sib).start()
            for j, chip in enumerate(chips):
                copy(4, w, 1 + j, f_refs, sems, me, (*chip, me[2])).start()

    def finish1(_, f_refs, sems):
        me, sib, chips = _places()
        for w in range(n_w):
            copy(4, w, 0, f_refs, sems, sib, me).wait_recv()
            for j, chip in enumerate(chips):
                copy(4, w, 1 + j, f_refs, sems, (*chip, me[2]), me).wait_recv()
        for w in range(n_w):
            for k in range(4):
                copy(4, w, k, f_refs, sems, me, sib).wait_send()

    def start2(_, f_refs, sems):
        me, sib, chips = _places()
        for w in range(n_w):
            for j, chip in enumerate(chips):
                copy(3, w, j, f_refs, sems, (*chip, me[2]), sib).start()

    def finish2(_, f_refs, sems):
        me, sib, chips = _places()
        for w in range(n_w):
            for j, chip in enumerate(chips):
                copy(3, w, j, f_refs, sems, (*chip, sib[2]), me).wait_recv()
        for w in range(n_w):
            for j, chip in enumerate(chips):
                copy(3, w, j, f_refs, sems, (*chip, me[2]), sib).wait_send()

    sems1 = [pltpu.SemaphoreType.DMA((n_w * 4,)), pltpu.SemaphoreType.DMA((n_w * 4,))]
    sems2 = [pltpu.SemaphoreType.DMA((n_w * 3,)), pltpu.SemaphoreType.DMA((n_w * 3,))]
    return _Exchange([], shapes, sems1, start1, finish1), _Exchange([], shapes, sems2, start2, finish2)


def _run_exchange(name, plan):
    n_in, n_out = len(plan.arrays), len(plan.out_shapes)

    def body(*refs):
        ins, outs, sems = refs[:n_in], refs[n_in:n_in + n_out], refs[n_in + n_out:]
        plan.start(ins, outs, sems)
        plan.finish(ins, outs, sems)

    any_spec = pl.BlockSpec(memory_space=pl.ANY)
    return pl.pallas_call(
        body, name=name, out_shape=plan.out_shapes,
        in_specs=[any_spec] * n_in, out_specs=[any_spec] * n_out, scratch_shapes=plan.sems,
    )(*plan.arrays)


_NO_EXCHANGE = _Exchange([], [], [], lambda i, o, s: None, lambda i, o, s: None)


def _direct_gather_plan(fulls, axes):
    n_w = len(fulls)
    widths = [f.shape[ax] // N_DEV for f, ax in zip(fulls, axes)]
    fulls = [jax.ShapeDtypeStruct(f.shape, f.dtype) for f in fulls]

    def copy(w, r, s_refs, f_refs, sems, block, to):
        part = _region(f_refs[w], block, axes[w], widths[w])
        return pltpu.make_async_remote_copy(
            src_ref=part, dst_ref=part,
            send_sem=sems[0].at[w * (N_DEV - 1) + r - 1], recv_sem=sems[1].at[w * (N_DEV - 1) + r - 1],
            device_id=_dev_coords(to), device_id_type=MESH)

    def start(s_refs, f_refs, sems):
        me = _dev_index()
        for w in range(n_w):
            for r in range(1, N_DEV):
                copy(w, r, s_refs, f_refs, sems, me, (me + r) % N_DEV).start()

    def finish(s_refs, f_refs, sems):
        me = _dev_index()
        for w in range(n_w):
            for r in range(1, N_DEV):
                src = (me + N_DEV - r) % N_DEV
                copy(w, r, s_refs, f_refs, sems, src, src).wait_recv()
        for w in range(n_w):
            for r in range(1, N_DEV):
                copy(w, r, s_refs, f_refs, sems, me, (me + r) % N_DEV).wait_send()

    sems = [pltpu.SemaphoreType.DMA((n_w * (N_DEV - 1),)), pltpu.SemaphoreType.DMA((n_w * (N_DEV - 1),))]
    return _Exchange([], fulls, sems, start, finish)


_HBM = pl.BlockSpec(memory_space=pltpu.HBM)
_SEM = pl.BlockSpec(memory_space=pltpu.SEMAPHORE)
_EFFECT = pltpu.SideEffectType.DATAFLOW_SIDE_EFFECTING


def _split_start(name, plan, landing=None):
    n_in, n_out, n_sem = len(plan.arrays), len(plan.out_shapes), len(plan.sems)

    def body(*refs):
        ins, lands = refs[:n_in], refs[n_in:n_in + n_out]
        sems = refs[n_in + n_out:n_in + n_out + n_sem]
        token = refs[-1]
        plan.start(ins, lands, sems)
        token[...] = jnp.zeros_like(token)

    hbm = lambda a: pltpu.HBM(a.shape, a.dtype)
    results = pl.pallas_call(
        body, name=name,
        out_shape=tuple(plan.sems) + tuple(hbm(a) for a in plan.arrays) + tuple(hbm(a) for a in plan.out_shapes)
        + (jax.ShapeDtypeStruct((8, LANE), F32),),
        in_specs=(_HBM,) * (n_in + n_out),
        out_specs=(_SEM,) * n_sem + (_HBM,) * (n_in + n_out) + (pl.BlockSpec(memory_space=pltpu.VMEM),),
        input_output_aliases={i: n_sem + i for i in range(n_in + n_out)},
        compiler_params=pltpu.CompilerParams(has_side_effects=_EFFECT),
    )(*[pltpu.with_memory_space_constraint(a, pltpu.HBM) for a in plan.arrays],
      *[pltpu.with_memory_space_constraint(a, pltpu.HBM)
        for a in (landing if landing is not None else [lax.empty(a.shape, a.dtype) for a in plan.out_shapes])])
    return results[:n_sem], results[n_sem:n_sem + n_in + n_out], results[-1]


def _split_wait(name, plan, sems, thru, after):
    n_in, n_out, n_sem = len(plan.arrays), len(plan.out_shapes), len(plan.sems)

    def body(*refs):
        ins, lands = refs[:n_in], refs[n_in:n_in + n_out]
        sem_refs = refs[n_in + n_out:n_in + n_out + n_sem]
        plan.finish(ins, lands, sem_refs)

    hbm = lambda a: pltpu.HBM(a.shape, a.dtype)
    results = pl.pallas_call(
        body, name=name,
        out_shape=tuple(hbm(a) for a in plan.arrays) + tuple(hbm(a) for a in plan.out_shapes),
        in_specs=(_HBM,) * (n_in + n_out) + (_SEM,) * n_sem + (pl.BlockSpec(memory_space=pl.ANY),),
        out_specs=(_HBM,) * (n_in + n_out),
        input_output_aliases={i: i for i in range(n_in + n_out)},
        compiler_params=pltpu.CompilerParams(has_side_effects=_EFFECT),
    )(*thru, *sems, after)
    return results[:n_in], results[n_in:]


def _cast_into_full(name, me, w, axis):
    r, c = w.shape
    tr = _tile(r, 256)
    if axis == 1:
        shape, place = (r, c * N_DEV), pl.BlockSpec((tr, c), lambda i, me_ref: (i, me_ref[0]))
    else:
        shape, place = (r * N_DEV, c), pl.BlockSpec((tr, c), lambda i, me_ref: (me_ref[0] * (r // tr) + i, 0))

    def body(me_ref, w_ref, o_ref):
        o_ref[...] = w_ref[...].astype(BF16)

    grid_spec = pltpu.PrefetchScalarGridSpec(
        num_scalar_prefetch=1, grid=(r // tr,),
        in_specs=[pl.BlockSpec((tr, c), lambda i, me_ref: (i, 0))], out_specs=place)
    return pl.pallas_call(body, name=name, grid_spec=grid_spec, out_shape=jax.ShapeDtypeStruct(shape, BF16),
                          compiler_params=_params(1))(me, w)


def _mm(name, a, b, dims, m, n, k, tm, tn, tk, extras, outs, epilogue, row_chunk=None, exchange=None,
        b_col_block=0):
    ni, nj, nk = m // tm, n // tn, k // tk
    ne, no = len(extras), len(outs)
    xin = len(exchange.arrays) if exchange else 0
    xout = len(exchange.out_shapes) if exchange else 0
    if dims == _TN:
        a_spec = pl.BlockSpec((tk, tm), lambda i, j, kk: (kk, i))
    else:
        a_spec = pl.BlockSpec((tm, tk), lambda i, j, kk: (i, kk))
    if dims == _NT:
        b_spec = pl.BlockSpec((tn, tk), lambda i, j, kk: (j, kk))
    else:
        b_spec = pl.BlockSpec((tk, tn), lambda i, j, kk: (kk, j + b_col_block))
    chunks = [slice(None)] if row_chunk is None else [slice(r, r + row_chunk) for r in range(0, tm, row_chunk)]

    def lift(index_map):
        return lambda i, j, kk: index_map(i, j)

    def body(a_ref, b_ref, *rest):
        extra_refs, rest = rest[:ne], rest[ne:]
        xin_refs, rest = rest[:xin], rest[xin:]
        out_refs, rest = rest[:no], rest[no:]
        xout_refs, rest = rest[:xout], rest[xout:]
        i, j, kk = pl.program_id(0), pl.program_id(1), pl.program_id(2)
        if exchange:
            sem_refs = rest[1:] if nk > 1 else rest

            @pl.when((i == 0) & (j == 0) & (kk == 0))
            def _():
                exchange.start(xin_refs, xout_refs, sem_refs)

        if nk == 1:
            part = _dot(a_ref[...], b_ref[...], dims)
            for rows in chunks:
                epilogue(part[rows], i, j, extra_refs, out_refs, rows)
        else:
            acc_ref = rest[0]

            @pl.when(kk == 0)
            def _():
                acc_ref[...] = _dot(a_ref[...], b_ref[...], dims)

            @pl.when(kk > 0)
            def _():
                acc_ref[...] += _dot(a_ref[...], b_ref[...], dims)

            @pl.when(kk == nk - 1)
            def _():
                for rows in chunks:
                    epilogue(acc_ref[rows, :], i, j, extra_refs, out_refs, rows)

        if exchange:
            @pl.when((i == ni - 1) & (j == nj - 1) & (kk == nk - 1))
            def _():
                exchange.finish(xin_refs, xout_refs, sem_refs)

    any_spec = pl.BlockSpec(memory_space=pl.ANY)
    once = dict(pipeline_mode=pl.Buffered(1)) if (row_chunk is not None and nk > 1) else {}
    results = pl.pallas_call(
        body, name=name,
        grid=(ni, nj, nk),
        in_specs=[a_spec, b_spec] + [pl.BlockSpec(bs, lift(im), **once) for _, bs, im in extras] + [any_spec] * xin,
        out_specs=[pl.BlockSpec(bs, lift(im), **once) for _, bs, im in outs] + [any_spec] * xout,
        out_shape=[sd for sd, _, _ in outs] + (list(exchange.out_shapes) if exchange else []),
        scratch_shapes=([pltpu.VMEM((tm, tn), F32)] if nk > 1 else []) + (list(exchange.sems) if exchange else []),
        compiler_params=_params(3),
    )(a, b, *[arr for arr, _, _ in extras], *(exchange.arrays if exchange else []))
    return (results[:no], results[no:]) if exchange else results


def _after(token):
    return [(token, (8, LANE), lambda i, j: (0, 0))]


def _grad_w(name, a, dc, token=None, tm=512, tn=1024, cols=None):
    t, m = a.shape
    first, n = cols if cols is not None else (0, dc.shape[1])
    tm, tn = _tile(m, tm), _tile(n, tn)
    assert first % tn == 0

    def epilogue(acc, i, j, extra_refs, out_refs, rows):
        out_refs[0][...] = acc
        out_refs[1][...] = acc.astype(BF16)

    blk = ((tm, tn), lambda i, j: (i, j))
    return _mm(name, a, dc, _TN, m, n, t, tm, tn, t, _after(token) if token is not None else [],
               [(jax.ShapeDtypeStruct((m, n), F32),) + blk, (jax.ShapeDtypeStruct((m, n), BF16),) + blk], epilogue,
               b_col_block=first // tn)


def _proj_gather(a1, w_shard, order):
    t, d = a1.shape
    nsh = w_shard.shape[1]
    tm = _tile(t, 512)
    n_i = t // tm

    def body(ord_ref, a_ref, wsh_ref, proj_ref, full_ref, bbuf, bsem, send_sems, recv_sems, own_sem):
        s, i = pl.program_id(0), pl.program_id(1)
        me, sib, chips = _places()
        blocks = [me, sib] + [(*ch, me[2]) for ch in chips] + [(*ch, sib[2]) for ch in chips]

        def part(block):
            return _region(full_ref, _place_index(block), 1, nsh)

        def remote(k, block, to, from_shard=False):
            return pltpu.make_async_remote_copy(
                src_ref=wsh_ref if from_shard else part(block), dst_ref=part(block),
                send_sem=send_sems.at[k], recv_sem=recv_sems.at[k], device_id=to, device_id_type=MESH)

        def load(pos):
            src = wsh_ref if pos == 0 else part(blocks[pos])
            return pltpu.make_async_copy(src, bbuf.at[pos % 2], bsem.at[pos % 2])

        own = pltpu.make_async_copy(wsh_ref, part(me), own_sem)

        @pl.when((s == 0) & (i == 0))
        def _():
            own.start()
            remote(0, me, sib, True).start()
            for j, ch in enumerate(chips):
                remote(1 + j, me, (*ch, me[2]), True).start()
            load(0).start()
            load(0).wait()

        for pos in range(1, N_DEV):
            @pl.when((s == pos) & (i == 0))
            def _():
                load(pos).wait()

        for pos in range(N_DEV - 1):
            @pl.when((s == pos) & (i == n_i - 1))
            def _():
                nxt = pos + 1
                if nxt == 1:
                    remote(0, sib, me).wait_recv()
                elif nxt <= 4:
                    remote(nxt - 1, blocks[nxt], me).wait_recv()
                    remote(nxt + 2, blocks[nxt], sib).start()
                else:
                    remote(nxt - 1, blocks[nxt], me).wait_recv()
                load(nxt).start()

        proj_ref[...] = _dot(a_ref[...], bbuf[s % 2])

        @pl.when((s == N_DEV - 1) & (i == n_i - 1))
        def _():
            for k in range(N_DEV - 1):
                remote(k, me, sib, True).wait_send()
            own.wait()

    grid_spec = pltpu.PrefetchScalarGridSpec(
        num_scalar_prefetch=1, grid=(N_DEV, n_i),
        in_specs=[pl.BlockSpec((tm, d), lambda s, i, ord_ref: (i, 0)), pl.BlockSpec(memory_space=pl.ANY)],
        out_specs=[pl.BlockSpec((tm, nsh), lambda s, i, ord_ref: (i, ord_ref[s])), pl.BlockSpec(memory_space=pl.ANY)],
        scratch_shapes=[pltpu.VMEM((2, d, nsh), BF16), pltpu.SemaphoreType.DMA((2,)),
                        pltpu.SemaphoreType.DMA((N_DEV - 1,)), pltpu.SemaphoreType.DMA((N_DEV - 1,)),
                        pltpu.SemaphoreType.DMA])
    return pl.pallas_call(
        body, name="proj_gather", grid_spec=grid_spec,
        out_shape=[jax.ShapeDtypeStruct((t, nsh * N_DEV), F32), jax.ShapeDtypeStruct((d, nsh * N_DEV), BF16)],
        compiler_params=_params(2),
    )(order, a1, w_shard)


def _cast_bf16(name, w):
    r, c = w.shape
    tr = _tile(r, 256)
    return pl.pallas_call(
        lambda w_ref, o_ref: o_ref.__setitem__(Ellipsis, w_ref[...].astype(BF16)), name=name,
        grid=(r // tr,), in_specs=[pl.BlockSpec((tr, c), lambda i: (i, 0))],
        out_specs=pl.BlockSpec((tr, c), lambda i: (i, 0)), out_shape=jax.ShapeDtypeStruct((r, c), BF16),
        compiler_params=_params(1),
    )(w)


def _prep_small(c_row, lb_logits):
    d = c_row.shape[1]
    rows = d // LANE

    def body(c_ref, l_ref, o_ref):
        cv = c_ref[...]
        o_ref[0:rows, :] = cv * _sigmoid(cv)
        lbs = [_sigmoid(l_ref[dr][0:1, :] - l_ref[dr][1:2, :]) for dr in range(2)]
        o_ref[rows:rows + 8, :] = jnp.concatenate(lbs + [jnp.zeros((6, LANE), F32)], axis=0)

    return pl.pallas_call(
        body, name="prep_small", out_shape=jax.ShapeDtypeStruct((rows + 8, LANE), F32),
    )(c_row.reshape(rows, LANE), lb_logits)


def _mod_shard(sc_all, w_ada_shard, b_shard):
    d, n = w_ada_shard.shape
    tn = _tile(n, 512)

    def body(s_ref, w_ref, b_ref, o_ref):
        o_ref[...] = _dot(s_ref[...], w_ref[...], precision=HIGHEST) + b_ref[...]

    return pl.pallas_call(
        body, name="mod_shard", grid=(n // tn,),
        in_specs=[pl.BlockSpec((N_DEV, d), lambda j: (0, 0)), pl.BlockSpec((d, tn), lambda j: (0, j)),
                  pl.BlockSpec((1, tn), lambda j: (0, j))],
        out_specs=pl.BlockSpec((N_DEV, tn), lambda j: (0, j)),
        out_shape=jax.ShapeDtypeStruct((N_DEV, n), F32), compiler_params=_params(1),
    )(sc_all, w_ada_shard, b_shard)


def _norm_mod(x, gain, shift, scale):
    t, d = x.shape
    tm = _tile(t, 512)

    def body(x_ref, g_ref, sh_ref, sc_ref, o_ref):
        xv = x_ref[...]
        o_ref[...] = ((xv * _rms(xv) * g_ref[...]) * (1.0 + sc_ref[...]) + sh_ref[...]).astype(BF16)

    vec = pl.BlockSpec((1, d), lambda i: (0, 0))
    return pl.pallas_call(
        body, name="norm_mod", grid=(t // tm,),
        in_specs=[pl.BlockSpec((tm, d), lambda i: (i, 0)), vec, vec, vec],
        out_specs=pl.BlockSpec((tm, d), lambda i: (i, 0)), out_shape=jax.ShapeDtypeStruct((t, d), BF16),
        compiler_params=_params(1),
    )(x, gain, shift, scale)


def _chunk_masks():
    row = lax.broadcasted_iota(jnp.int32, (HEAD, HEAD), 0)
    col = lax.broadcasted_iota(jnp.int32, (HEAD, HEAD), 1)
    same = (row // A_CHUNK) == (col // A_CHUNK)
    return same & (col <= row), same & (col >= row)


def _ones(mask):
    return jnp.where(mask, 1.0, 0.0).astype(BF16)


def _dot_split(ones_bf16, x):
    hi = x.astype(BF16)
    lo = (x - hi.astype(F32)).astype(BF16)
    return _dot(ones_bf16, hi) + _dot(ones_bf16, lo)


def _hgrn_block(direction, f, lb, cum2):
    sf = _sigmoid(f)
    big_f = lb + (1.0 - lb) * sf
    k = (1.0 - lb) * (1.0 - sf)
    lf = jnp.log(big_f)
    both = _dot_split(cum2, lf)
    cf, cr = both[:HEAD], both[HEAD:]
    b, rest = (cf, cr - lf) if direction == 0 else (cr, cf - lf)
    return k, sf, big_f, jnp.exp(b), jnp.exp(-b), jnp.exp(rest)


def _hgrn_fwd(proj, lb, g_norm, width, exchange):
    t = proj.shape[0]
    heads = width // HEAD
    nb, nc = t // HEAD, t // A_CHUNK
    ua = 4 if nb % 4 == 0 else (2 if nb % 2 == 0 else 1)
    ub = 16 if nc % 16 == 0 else (8 if nc % 8 == 0 else 4)
    q_scale = HEAD ** -0.5
    xin, xout = len(exchange.arrays), len(exchange.out_shapes)

    def body(q_ref, ffw_ref, fbw_ref, v_ref, og_ref, lb_ref, g_ref, *rest):
        xin_refs, rest = rest[:xin], rest[xin:]
        outa_ref, osum_ref = rest[:2]
        xout_refs, rest = rest[2:2 + xout], rest[2 + xout:]
        qd_s, ke_s, dc_s, o_s = rest[:4]
        sem_refs = rest[4:]
        h = pl.program_id(0)

        @pl.when(h == 0)
        def _():
            exchange.start(xin_refs, xout_refs, sem_refs)

        tril, triu = _chunk_masks()
        cum2 = jnp.concatenate([_ones(tril), _ones(triu)], axis=0)
        f_refs = (ffw_ref, fbw_ref)
        lbs = (lb_ref[0:1, :], lb_ref[1:2, :])

        def phase_a(it, carry):
            loaded = []
            for u in range(ua):
                rows = pl.ds(pl.multiple_of((it * ua + u) * HEAD, HEAD), HEAD)
                loaded.append((rows, q_ref[rows, :], v_ref[rows, :], ffw_ref[rows, :], fbw_ref[rows, :]))
            chains = [(d, rows, qv * q_scale, vv.astype(BF16), fv)
                      for rows, qv, vv, f0, f1 in loaded for d, fv in ((0, f0), (1, f1))]
            blocks = [_hgrn_block(d, fv, lbs[d], cum2) for d, _, _, _, fv in chains]
            scaled = [(qv * eb, k * enb, k * erest, eb * erest)
                      for (_, _, qv, _, _), (k, _, _, eb, enb, erest) in zip(chains, blocks)]
            atts = [jnp.where(tril if d == 0 else triu, _bdot(qd, kd, _NT), 0.0)
                    for (d, _, _, _, _), (qd, kd, _, _) in zip(chains, scaled)]
            intras = [_bdot(att, vv) for att, (_, _, _, vv, _) in zip(atts, chains)]
            results = [(d, rows, o_intra, qd.astype(BF16), ke.astype(BF16), decay)
                       for (d, rows, _, _, _), (qd, _, ke, decay), o_intra in zip(chains, scaled, intras)]
            for d, rows, o_intra, qd16, ke16, decay in results:
                o_s[d, rows, :] = o_intra
                qd_s[d, rows, :] = qd16
                ke_s[d, rows, :] = ke16
                dc_s[d, rows, :] = decay
            return carry

        lax.fori_loop(0, nb // ua, phase_a, 0)

        def phase_b(it, states):
            loaded = []
            for u in range(ub):
                n = it * ub + u
                for d in range(2):
                    c = n if d == 0 else nc - 1 - n
                    start = pl.multiple_of(c * A_CHUNK, A_CHUNK)
                    rows = pl.ds(start, A_CHUNK)
                    loaded.append((d, rows, qd_s[d, rows, :], ke_s[d, rows, :], v_ref[rows, :],
                                   dc_s[d, pl.ds(start, 1), :], o_s[d, rows, :]))
            increments = [_dot(vv.astype(BF16), ke16, _TN) for _, _, _, ke16, vv, _, _ in loaded]
            states = list(states)
            befores = []
            for (d, _, _, _, _, decay, _), inc in zip(loaded, increments):
                befores.append(states[d].astype(BF16))
                states[d] = states[d] * decay + inc
            inters = [_dot(qd16, before, _NT) for (_, _, qd16, _, _, _, _), before in zip(loaded, befores)]
            for (d, rows, _, _, _, _, o_intra), o_inter in zip(loaded, inters):
                o_s[d, rows, :] = o_intra + o_inter
            return tuple(states)

        zero_state = jnp.zeros((HEAD, HEAD), F32)
        lax.fori_loop(0, nc // ub, phase_b, (zero_state, zero_state))

        def phase_c(i, carry):
            rows = pl.ds(pl.multiple_of(i * HEAD, HEAD), HEAD)
            o = o_s[0, rows, :] + o_s[1, rows, :]
            osum_ref[rows, :] = o
            og = og_ref[rows, :]
            outa_ref[rows, :] = (o * _rms(o) * g_ref[...] * (og * _sigmoid(og))).astype(BF16)
            return carry

        lax.fori_loop(0, nb, phase_c, 0)

        @pl.when(h == heads - 1)
        def _():
            exchange.finish(xin_refs, xout_refs, sem_refs)

    def col(p):
        return pl.BlockSpec((t, HEAD), lambda h: (0, p * heads + h))

    any_spec = pl.BlockSpec(memory_space=pl.ANY)
    results = pl.pallas_call(
        body, name="hgrn_fwd", grid=(heads,),
        in_specs=[col(0), col(1), col(2), col(3), col(4),
                  pl.BlockSpec((2, HEAD), lambda h: (0, h)), pl.BlockSpec((1, HEAD), lambda h: (0, 0))] + [any_spec] * xin,
        out_specs=[pl.BlockSpec((t, HEAD), lambda h: (0, h)), pl.BlockSpec((t, HEAD), lambda h: (0, h))] + [any_spec] * xout,
        out_shape=[jax.ShapeDtypeStruct((t, width), BF16), jax.ShapeDtypeStruct((t, width), F32)] + list(exchange.out_shapes),
        scratch_shapes=[pltpu.VMEM((2, t, HEAD), BF16), pltpu.VMEM((2, t, HEAD), BF16), pltpu.VMEM((2, t, HEAD), F32),
                        pltpu.VMEM((2, t, HEAD), F32)] + list(exchange.sems),
        compiler_params=_params(1),
    )(proj, proj, proj, proj, proj, lb, g_norm, *exchange.arrays)
    return results[0], results[1], results[2:]


def _sgu_core(u_pre, v_pre, g_v, ws_ref, bst):
    u, du = _gelu_and_grad(u_pre)
    v, dv = _gelu_and_grad(v_pre)
    mu = jnp.mean(v, axis=-1, keepdims=True)
    dlt = v - mu
    rstd = lax.rsqrt(jnp.mean(dlt * dlt, axis=-1, keepdims=True) + EPS)
    vhat = dlt * rstd
    vn = vhat * g_v
    groups = vn.shape[1] // HEAD
    cols = []
    for g in range(groups):
        vm_g = _bdot(ws_ref[g], vn[:, g * HEAD:(g + 1) * HEAD]) + bst[:, g:g + 1]
        cols.append(vm_g)
    return u, du, dv, vhat, rstd, vn, jnp.concatenate(cols, axis=1)


def _sgu_fwd(proj, g_v, w_s, bst, width, z_block):
    t = proj.shape[0]

    def body(u_ref, v_ref, g_ref, ws_ref, bst_ref, o_ref):
        u, _, _, _, _, _, vm = _sgu_core(u_ref[...], v_ref[...], g_ref[...], ws_ref, bst_ref[...])
        o_ref[...] = (u * vm).astype(BF16)

    groups = width // HEAD
    return pl.pallas_call(
        body, name="sgu_fwd", grid=(t // HEAD,),
        in_specs=[pl.BlockSpec((HEAD, width), lambda i: (i, z_block)), pl.BlockSpec((HEAD, width), lambda i: (i, z_block + 1)),
                  pl.BlockSpec((1, width), lambda i: (0, 0)), pl.BlockSpec((groups, HEAD, HEAD), lambda i: (0, 0, 0)),
                  pl.BlockSpec((HEAD, groups), lambda i: (0, 0))],
        out_specs=pl.BlockSpec((HEAD, width), lambda i: (i, 0)),
        out_shape=jax.ShapeDtypeStruct((t, width), BF16), compiler_params=_params(1),
    )(proj, proj, g_v, w_s, bst)


def _sgu_bwd(proj, dout_b, dproj, g_v, w_s, w_st, bst, width, z_block):
    t = proj.shape[0]
    groups = width // HEAD
    nblk = t // HEAD

    def body(u_ref, v_ref, do_ref, g_ref, ws_ref, wst_ref, bst_ref, dproj_hbm,
             dz_ref, dg_ref, dws_ref, dbst_ref, res_s):
        i, p = pl.program_id(0), pl.program_id(1)

        @pl.when((i == 0) & (p == 0))
        def _():
            dg_ref[...] = jnp.zeros_like(dg_ref)
            dws_ref[...] = jnp.zeros_like(dws_ref)
            dbst_ref[...] = jnp.zeros_like(dbst_ref)

        @pl.when(p == 0)
        def _():
            g_v = g_ref[...]
            u, du, dv, vhat, rstd, vn, vm = _sgu_core(u_ref[...], v_ref[...], g_v, ws_ref, bst_ref[...])
            dout = do_ref[...].astype(F32)
            res_s[0] = (dout * vm * du).astype(BF16)
            dvm = dout * u
            dvn_cols = []
            for g in range(groups):
                sl = slice(g * HEAD, (g + 1) * HEAD)
                dvm_g = dvm[:, sl]
                dbst_ref[:, g:g + 1] += jnp.sum(dvm_g, axis=1, keepdims=True)
                dws_ref[g] += _bdot(dvm_g, vn[:, sl], _NT)
                dvn_cols.append(_bdot(wst_ref[g], dvm_g))
            dvn = jnp.concatenate(dvn_cols, axis=1)
            dg_ref[...] += _colsum(dvn * vhat)
            dvh = dvn * g_v
            dvg = rstd * (dvh - jnp.mean(dvh, axis=-1, keepdims=True)
                          - vhat * jnp.mean(dvh * vhat, axis=-1, keepdims=True))
            res_s[1] = (dvg * dv).astype(BF16)

        dz_ref[...] = res_s[p]

    n_in = dproj.shape[1]
    return pl.pallas_call(
        body, name="sgu_bwd", grid=(nblk, 2),
        in_specs=[pl.BlockSpec((HEAD, width), lambda i, p: (i, z_block)),
                  pl.BlockSpec((HEAD, width), lambda i, p: (i, z_block + 1)),
                  pl.BlockSpec((HEAD, width), lambda i, p: (i, 0)),
                  pl.BlockSpec((1, width), lambda i, p: (0, 0)),
                  pl.BlockSpec((groups, HEAD, HEAD), lambda i, p: (0, 0, 0)),
                  pl.BlockSpec((groups, HEAD, HEAD), lambda i, p: (0, 0, 0)),
                  pl.BlockSpec((HEAD, groups), lambda i, p: (0, 0)),
                  pl.BlockSpec(memory_space=pl.ANY)],
        out_specs=[pl.BlockSpec((HEAD, width), lambda i, p: (i, z_block + p)),
                   pl.BlockSpec((1, width), lambda i, p: (0, 0)),
                   pl.BlockSpec((groups, HEAD, HEAD), lambda i, p: (0, 0, 0)),
                   pl.BlockSpec((HEAD, groups), lambda i, p: (0, 0))],
        out_shape=[jax.ShapeDtypeStruct((t, n_in), BF16), jax.ShapeDtypeStruct((1, width), F32),
                   jax.ShapeDtypeStruct((groups, HEAD, HEAD), F32), jax.ShapeDtypeStruct((HEAD, groups), F32)],
        scratch_shapes=[pltpu.VMEM((2, HEAD, width), BF16)],
        input_output_aliases={7: 0},
        compiler_params=_params(2),
    )(proj, proj, dout_b, g_v, w_s, w_st, bst, dproj)


def _hgrn_bwd(proj, osum, dout_a, dproj, lb, g_norm, width, exchange, after):
    t = proj.shape[0]
    heads = width // HEAD
    nb = t // HEAD
    cpb = HEAD // A_CHUNK
    ubk = 2 if nb % 2 == 0 else 1
    q_scale = HEAD ** -0.5
    xin, xout = len(exchange.arrays), len(exchange.out_shapes)

    def body(q_ref, ffw_ref, fbw_ref, v_ref, og_ref, osum_ref, douta_ref, lb_ref, g_ref, dproj_hbm, *rest):
        xin_refs, rest = rest[:xin], rest[xin + 1:]
        out_ref, dgh_ref, dlb_ref = rest[:3]
        xout_refs, rest = rest[3:3 + xout], rest[3 + xout:]
        do_s, dq_s, dv_s, res_s, ck_s = rest[:5]
        sem_refs = rest[5:]
        h, p = pl.program_id(0), pl.program_id(1)
        f_refs = (ffw_ref, fbw_ref)

        @pl.when((h == 0) & (p == 0))
        def _():
            exchange.start(xin_refs, xout_refs, sem_refs)

        @pl.when(p == 0)
        def _():
            tril, triu = _chunk_masks()
            cum2 = jnp.concatenate([_ones(tril), _ones(triu)], axis=0)
            g_row = g_ref[...]

            def pass_norm(i, dgh):
                rows = pl.ds(pl.multiple_of(i * HEAD, HEAD), HEAD)
                o = osum_ref[rows, :]
                r = _rms(o)
                oh = o * r
                og = og_ref[rows, :]
                sg = _sigmoid(og)
                dout = douta_ref[rows, :].astype(F32)
                don = dout * (og * sg)
                res_s[4, rows, :] = (dout * (oh * g_row) * (sg * (1.0 + og * (1.0 - sg)))).astype(BF16)
                doh = don * g_row
                do_s[rows, :] = r * (doh - oh * jnp.mean(doh * oh, axis=-1, keepdims=True))
                return dgh + _colsum(don * oh)

            dgh_ref[...] = lax.fori_loop(0, nb, pass_norm, jnp.zeros((1, HEAD), F32))

            lbs = (lb_ref[0:1, :], lb_ref[1:2, :])
            zero_state = jnp.zeros((HEAD, HEAD), F32)

            def chunk_order(d):
                return list(range(cpb)) if d == 0 else list(range(cpb - 1, -1, -1))

            def chunk(x, j):
                return x[j * A_CHUNK:(j + 1) * A_CHUNK, :]

            def decay_row(e_big, j):
                return e_big[j * A_CHUNK:j * A_CHUNK + 1, :]

            def cat(parts):
                return jnp.concatenate([parts[j] for j in range(cpb)], axis=0)

            def block_states(d, start, incs, e_big):
                befores, st = {}, start
                for j in chunk_order(d):
                    befores[j] = st
                    st = st * decay_row(e_big, j) + incs[j]
                return befores, st

            def pass_states(it, states):
                loaded = []
                for u in range(ubk):
                    for d in range(2):
                        blk = it * ubk + u if d == 0 else nb - 1 - (it * ubk + u)
                        rows = pl.ds(pl.multiple_of(blk * HEAD, HEAD), HEAD)
                        loaded.append((d, blk, f_refs[d][rows, :], v_ref[rows, :]))
                blocks = [_hgrn_block(d, fv, lbs[d], cum2) for d, _, fv, _ in loaded]
                incs = [{j: _bdot(chunk(vv, j), chunk(k * erest, j), _TN) for j in range(cpb)}
                        for (_, _, _, vv), (k, _, _, _, _, erest) in zip(loaded, blocks)]
                states, starts = list(states), []
                for (d, _, _, _), (_, _, _, eb, _, erest), inc in zip(loaded, blocks, incs):
                    starts.append(states[d])
                    states[d] = block_states(d, states[d], inc, eb * erest)[1]
                for (d, blk, _, _), start in zip(loaded, starts):
                    ck_s[d, blk] = start
                return tuple(states)

            lax.fori_loop(0, nb // ubk, pass_states, (zero_state, zero_state))

            def pass_back(it, carry):
                gts, dlb = [carry[0], carry[1]], carry[2]
                loaded = []
                for u, d in ((u, d) for u in range(ubk) for d in range(2)):
                    blk = nb - 1 - (it * ubk + u) if d == 0 else it * ubk + u
                    rows = pl.ds(pl.multiple_of(blk * HEAD, HEAD), HEAD)
                    loaded.append((d, rows, f_refs[d][rows, :], q_ref[rows, :], v_ref[rows, :], do_s[rows, :], ck_s[d, blk]))
                blocks = [_hgrn_block(d, fv, lbs[d], cum2) for d, _, fv, _, _, _, _ in loaded]
                scaled = []
                for (_, _, _, qv, _, _, _), (k, _, _, eb, enb, erest) in zip(loaded, blocks):
                    qh = qv * q_scale
                    scaled.append((qh, qh * eb, k * enb, k * erest, eb * erest))
                masks = [tril if d == 0 else triu for d, *_ in loaded]
                atts = [jnp.where(m, _bdot(qd, kd, _NT), 0.0) for m, (_, qd, kd, _, _) in zip(masks, scaled)]
                datts = [jnp.where(m, _bdot(do, vv, _NT), 0.0) for m, (_, _, _, _, vv, do, _) in zip(masks, loaded)]
                dvs = [_bdot(att, do, _TN) for att, (_, _, _, _, _, do, _) in zip(atts, loaded)]
                dqds = [_bdot(datt, kd) for datt, (_, _, kd, _, _) in zip(datts, scaled)]
                dkds = [_bdot(datt, qd, _TN) for datt, (_, qd, _, _, _) in zip(datts, scaled)]
                s_incs = [{j: _bdot(chunk(vv, j), chunk(ke, j), _TN) for j in range(cpb)}
                          for (_, _, _, _, vv, _, _), (_, _, _, ke, _) in zip(loaded, scaled)]
                g_incs = [{j: _bdot(chunk(do, j), chunk(qd, j), _TN) for j in range(cpb)}
                          for (_, _, _, _, _, do, _), (_, qd, _, _, _) in zip(loaded, scaled)]
                befores, afters, g_at = [], [], []
                for (d, _, _, _, _, _, ck), (_, _, _, _, e_big), s_inc, g_inc in zip(loaded, scaled, s_incs, g_incs):
                    order = chunk_order(d)
                    before, after = block_states(d, ck, s_inc, e_big)
                    befores.append(before)
                    afters.append({j: (before[order[n + 1]] if n + 1 < cpb else after) for n, j in enumerate(order)})
                    at, gt = {}, gts[d]
                    for j in reversed(order):
                        at[j] = gt
                        gt = gt * decay_row(e_big, j) + g_inc[j]
                    gts[d] = gt
                    g_at.append(at)
                dqd_i = [{j: _bdot(chunk(do, j), before[j]) for j in range(cpb)}
                         for (_, _, _, _, _, do, _), before in zip(loaded, befores)]
                dv_i = [{j: _bdot(chunk(ke, j), at[j], _NT) for j in range(cpb)}
                        for (_, _, _, ke, _), at in zip(scaled, g_at)]
                dke = [{j: _bdot(chunk(vv, j), at[j]) for j in range(cpb)}
                       for (_, _, _, _, vv, _, _), at in zip(loaded, g_at)]
                results, new = [], []
                for n, ((d, rows, _, _, _, _, _), (k, sf, big_f, eb, enb, erest), (qh, _, _, _, _)) in enumerate(
                        zip(loaded, blocks, scaled)):
                    dqh = (dqds[n] + cat(dqd_i[n])) * eb
                    dk = dkds[n] * enb + cat(dke[n]) * erest
                    carry_rows = {j: jnp.broadcast_to(_colsum(g_at[n][j] * afters[n][j]), (A_CHUNK, HEAD))
                                  for j in range(cpb)}
                    dlf = _dot_split(_ones(triu if d == 0 else tril), qh * dqh - k * dk) + cat(carry_rows)
                    common = dlf / big_f - dk
                    results.append((d, rows, (k * sf * common).astype(BF16), dqh.astype(BF16),
                                    (dvs[n] + cat(dv_i[n])).astype(BF16)))
                    new.append(_colsum((1.0 - sf) * common))
                for d, rows, df16, dq16, dv16 in results:
                    res_s[1 + d, rows, :] = df16
                    dq_s[d, rows, :] = dq16
                    dv_s[d, rows, :] = dv16
                per_dir = [sum(c for (d, *_), c in zip(loaded, new) if d == dd) for dd in range(2)]
                return gts[0], gts[1], dlb + jnp.concatenate(per_dir, axis=0)

            dlb_ref[...] = lax.fori_loop(0, nb // ubk, pass_back,
                                         (zero_state, zero_state, jnp.zeros((2, HEAD), F32)))[2]

            def pass_out(i, carry):
                rows = pl.ds(pl.multiple_of(i * HEAD, HEAD), HEAD)
                dq = dq_s[0, rows, :].astype(F32) + dq_s[1, rows, :].astype(F32)
                res_s[0, rows, :] = (dq * q_scale).astype(BF16)
                res_s[3, rows, :] = (dv_s[0, rows, :].astype(F32) + dv_s[1, rows, :].astype(F32)).astype(BF16)
                return carry

            lax.fori_loop(0, nb, pass_out, 0)

        out_ref[...] = res_s[p]

        @pl.when((h == heads - 1) & (p == 4))
        def _():
            exchange.finish(xin_refs, xout_refs, sem_refs)

    def col(pp):
        return pl.BlockSpec((t, HEAD), lambda h, p: (0, pp * heads + h))

    n_in = dproj.shape[1]
    any_spec = pl.BlockSpec(memory_space=pl.ANY)
    results = pl.pallas_call(
        body, name="hgrn_bwd", grid=(heads, 5),
        in_specs=[col(0), col(1), col(2), col(3), col(4),
                  pl.BlockSpec((t, HEAD), lambda h, p: (0, h)), pl.BlockSpec((t, HEAD), lambda h, p: (0, h)),
                  pl.BlockSpec((2, HEAD), lambda h, p: (0, h)), pl.BlockSpec((1, HEAD), lambda h, p: (0, 0)),
                  any_spec] + [any_spec] * (xin + 1),
        out_specs=[pl.BlockSpec((t, HEAD), lambda h, p: (0, p * heads + h)),
                   pl.BlockSpec((None, 1, HEAD), lambda h, p: (h, 0, 0)),
                   pl.BlockSpec((2, HEAD), lambda h, p: (0, h))] + [any_spec] * xout,
        out_shape=[jax.ShapeDtypeStruct((t, n_in), BF16), jax.ShapeDtypeStruct((heads, 1, HEAD), F32),
                   jax.ShapeDtypeStruct((2, width), F32)] + list(exchange.out_shapes),
        scratch_shapes=[pltpu.VMEM((t, HEAD), F32), pltpu.VMEM((2, t, HEAD), BF16), pltpu.VMEM((2, t, HEAD), BF16),
                        pltpu.VMEM((5, t, HEAD), BF16), pltpu.VMEM((2, nb, HEAD, HEAD), F32)] + list(exchange.sems),
        input_output_aliases={9: 0},
        compiler_params=_params(2),
    )(proj, proj, proj, proj, proj, osum, dout_a, lb, g_norm, dproj, *exchange.arrays, after)
    return results[0], results[1], results[2], results[3:]


def _adamw(w, g, m, v):
    m = ADAM_B1 * m + (1.0 - ADAM_B1) * g
    v = ADAM_B2 * v + (1.0 - ADAM_B2) * (g * g)
    m_hat = m / (1.0 - ADAM_B1 ** ADAM_STEP)
    v_hat = v / (1.0 - ADAM_B2 ** ADAM_STEP)
    delta = -ADAM_LR * (m_hat / (jnp.sqrt(v_hat) + ADAM_EPS) + ADAM_WD * w)
    return delta, m, v


def _adamw_big(name, me, w, m, v, g_parts, landing, axis):
    r, c = w.shape
    tr = _tile(r, 128)
    n_parts = len(g_parts)
    per = N_DEV // n_parts

    def body(me_ref, w_ref, m_ref, v_ref, *rest):
        g_refs, (l_ref, og_ref, od_ref, om_ref, ov_ref) = rest[:n_parts], rest[n_parts:]
        g = g_refs[0][...]
        for p in range(1, n_parts):
            g = jnp.where(me_ref[0] // per == p, g_refs[p][...], g)
        for s in range(N_DEV - 1):
            g = g + l_ref[s].astype(F32)
        og_ref[...] = g
        od_ref[...], om_ref[...], ov_ref[...] = _adamw(w_ref[...], g, m_ref[...], v_ref[...])

    shard = pl.BlockSpec((tr, c), lambda i, me_ref: (i, 0))
    if axis == 1:
        own = pl.BlockSpec((tr, c), lambda i, me_ref: (i, me_ref[0] % per))
    else:
        assert n_parts == 1
        own = pl.BlockSpec((tr, c), lambda i, me_ref: (me_ref[0] * (r // tr) + i, 0))
    grid_spec = pltpu.PrefetchScalarGridSpec(
        num_scalar_prefetch=1, grid=(r // tr,),
        in_specs=[shard, shard, shard] + [own] * n_parts + [pl.BlockSpec((N_DEV - 1, tr, c), lambda i, me_ref: (0, i, 0))],
        out_specs=[shard] * 4)
    return pl.pallas_call(
        body, name=name, grid_spec=grid_spec, out_shape=[jax.ShapeDtypeStruct((r, c), F32)] * 4,
        compiler_params=_params(1),
    )(me, w, m, v, *g_parts, landing)


def _adamw_ada(sct, dmod_mine, w, m, v):
    d, n = w.shape
    tr = _tile(d, 256)

    def body(s_ref, dm_ref, w_ref, m_ref, v_ref, og_ref, od_ref, om_ref, ov_ref):
        g = _dot(s_ref[...], dm_ref[...], precision=HIGHEST)
        og_ref[...] = g
        od_ref[...], om_ref[...], ov_ref[...] = _adamw(w_ref[...], g, m_ref[...], v_ref[...])

    blk = pl.BlockSpec((tr, n), lambda i: (i, 0))
    return pl.pallas_call(
        body, name="adamw_ada", grid=(d // tr,),
        in_specs=[pl.BlockSpec((tr, N_DEV), lambda i: (i, 0)), pl.BlockSpec((N_DEV, n), lambda i: (0, 0)), blk, blk, blk],
        out_specs=[blk] * 4, out_shape=[jax.ShapeDtypeStruct((d, n), F32)] * 4, compiler_params=_params(1),
    )(sct, dmod_mine, w, m, v)


def _adamw_small(gathered, w, m, v):
    def body(g_ref, w_ref, m_ref, v_ref, og_ref, od_ref, om_ref, ov_ref):
        g = g_ref[0]
        for s in range(1, N_DEV):
            g = g + g_ref[s]
        og_ref[...] = g
        od_ref[...], om_ref[...], ov_ref[...] = _adamw(w_ref[...], g, m_ref[...], v_ref[...])

    return pl.pallas_call(
        body, name="adamw_small", out_shape=[jax.ShapeDtypeStruct(w.shape, F32)] * 4,
        compiler_params=pltpu.CompilerParams(vmem_limit_bytes=VMEM_LIMIT),
    )(gathered, w, m, v)


def _adamw_lb(dlb_mine, lb_logits, m, v):
    def body(d_ref, l_ref, m_ref, v_ref, og_ref, od_ref, om_ref, ov_ref):
        dlb = d_ref[0]
        for s in range(1, N_DEV):
            dlb = dlb + d_ref[s]
        for dr in range(2):
            lb = _sigmoid(l_ref[dr][0:1, :] - l_ref[dr][1:2, :])
            d0 = dlb[dr:dr + 1] * lb * (1.0 - lb)
            g = jnp.concatenate([d0, -d0], axis=0)
            og_ref[dr] = g
            od_ref[dr], om_ref[dr], ov_ref[dr] = _adamw(l_ref[dr], g, m_ref[dr], v_ref[dr])

    return pl.pallas_call(body, name="adamw_lb", out_shape=[jax.ShapeDtypeStruct(lb_logits.shape, F32)] * 4,
                          )(dlb_mine, lb_logits, m, v)


def _rows(a, pad_to=8):
    flat = a.reshape(-1, LANE)
    pad = (-flat.shape[0]) % pad_to
    return jnp.pad(flat, ((0, pad), (0, 0))) if pad else flat


def kernel(x, c, w_ada, b_ada, g_pre_mix, g_post_mix, g_pre_ffn, g_post_ffn, w_in, lb_logits, g_hgrn_norm, w_a_out, g_sgu_norm, w_spatial, b_spatial, w_b_out, w_o, w_ff1, w_ff2, loss_target, m_w_ada, m_b_ada, m_g_pre_mix, m_g_post_mix, m_g_pre_ffn, m_g_post_ffn, m_w_in, m_lb_logits, m_g_hgrn_norm, m_w_a_out, m_g_sgu_norm, m_w_spatial, m_b_spatial, m_w_b_out, m_w_o, m_w_ff1, m_w_ff2, v_w_ada, v_b_ada, v_g_pre_mix, v_g_post_mix, v_g_pre_ffn, v_g_post_ffn, v_w_in, v_lb_logits, v_g_hgrn_norm, v_w_a_out, v_g_sgu_norm, v_w_spatial, v_b_spatial, v_w_b_out, v_w_o, v_w_ff1, v_w_ff2):
    t, d = x.shape[1], x.shape[2]
    n_in = w_in.shape[2] * N_DEV
    width = (n_in - 2 * d) // 7
    heads = width // HEAD
    assert heads == N_DEV and width % LANE == 0
    d_ff = w_ff1.shape[2] * N_DEV
    n_ada = w_ada.shape[2]
    me = _dev_index()
    me_arr = me.reshape(1).astype(jnp.int32)
    x2, tgt = x[0], loss_target[0]

    big = [w_in[0], w_a_out[0], w_b_out[0], w_o[0], w_ff1[0], w_ff2[0]]
    big_axes = [1, 1, 1, 0, 1, 0]
    big_names = ["w_in", "w_a_out", "w_b_out", "w_o", "w_ff1", "w_ff2"]
    w_in16 = _cast_bf16("cast_w_in", big[0])
    own_parts = [_cast_into_full("cast_" + nm, me_arr, w, ax) for nm, w, ax in zip(big_names[1:], big[1:], big_axes[1:])]

    c_rows = d // LANE
    small = _all_gather_small("gather_c_lb", _prep_small(c[0:1], lb_logits))
    sc_all = small[:, :c_rows, :].reshape(N_DEV, d)
    lb = jnp.transpose(small[:, c_rows:c_rows + 2, :], (1, 0, 2)).reshape(2, width)
    b_shard = lax.dynamic_slice_in_dim(b_ada, me * n_ada, n_ada, axis=1)
    mod_sh = _mod_shard(sc_all, w_ada[0], b_shard)
    mod_all = _all_gather_small("gather_mod", _rows(mod_sh))
    mod_all = mod_all[:, :N_DEV * n_ada // LANE, :].reshape(N_DEV, N_DEV, n_ada)
    mod6 = lax.dynamic_index_in_dim(mod_all, me, axis=1, keepdims=False).reshape(N_MOD, d)
    sh1, sc1, gt1, sh2, sc2, gt2 = [mod6[i:i + 1] for i in range(N_MOD)]

    a1 = _norm_mod(x2, g_pre_mix, sh1, sc1)
    tm = _tile(t, 512)

    def store_f32(acc, i, j, extra_refs, out_refs, rows):
        out_refs[0][...] = acc

    xq, yq, cq = lax.axis_index("x"), lax.axis_index("y"), lax.axis_index("c")
    chips = [(1 - xq, yq), (xq, 1 - yq), (1 - xq, 1 - yq)]
    order = jnp.stack([me, 4 * xq + 2 * yq + 1 - cq] + [4 * a + 2 * b + cq for a, b in chips]
                      + [4 * a + 2 * b + 1 - cq for a, b in chips]).astype(jnp.int32)
    proj, wf_in = _proj_gather(a1, w_in16, order)

    proj, own_parts = lax.optimization_barrier((proj, own_parts))
    gathers = {}
    for key, lo, hi in (("mid", 1, 4), ("ff1", 4, 5), ("ff2", 5, 6)):
        far, near = _gather_stage_plans(own_parts[lo - 1:hi - 1], big_axes[lo:hi])
        gathers[key] = [far, near, _split_start("gather_%s_start" % key, far, landing=own_parts[lo - 1:hi - 1])]

    def pass_on(key, after):
        far, near, (sems, thru, _) = gathers[key]
        parts = _split_wait("gather_%s_wait" % key, far, sems, thru, after)[1]
        gathers[key].append(_split_start("pass_%s_start" % key, near, landing=list(parts)))
        return gathers[key][3][2]

    def gathered_weights(key, after):
        near, (sems, thru, _) = gathers[key][1], gathers[key][3]
        return _split_wait("pass_%s_wait" % key, near, sems, thru, after)[1]

    out_a, osum, _ = _hgrn_fwd(proj, lb, g_hgrn_norm, width, _NO_EXCHANGE)
    passed_mid = pass_on("mid", out_a)
    z_block = 5
    bst = b_spatial[0].T
    out_b = _sgu_fwd(proj, g_sgu_norm, w_spatial[0], bst, width, z_block)
    wf_a, wf_b, wf_o = gathered_weights("mid", out_b)

    tn_d = _tile(d, 512)
    blk_d = ((tm, tn_d), lambda i, j: (i, j))
    y_a, = _mm("y_a", out_a, wf_a, _NN, t, d, width, tm, tn_d, width, _after(passed_mid),
               [(jax.ShapeDtypeStruct((t, d), F32),) + blk_d], store_f32)
    ga_blk = (5 * width + 2 * width) // tn_d
    gb_blk = ga_blk + d // tn_d

    def merge(acc, i, j, extra_refs, out_refs, rows):
        ga, gb, ya = extra_refs
        out_refs[0][...] = acc
        out_refs[1][...] = (_sigmoid(ga[...]) * ya[...] + _sigmoid(gb[...]) * acc).astype(BF16)

    y_b, merged = _mm("y_b_merge", out_b, wf_b, _NN, t, d, width, tm, tn_d, width,
                      [(proj, (tm, tn_d), lambda i, j: (i, ga_blk + j)), (proj, (tm, tn_d), lambda i, j: (i, gb_blk + j)),
                       (y_a,) + blk_d],
                      [(jax.ShapeDtypeStruct((t, d), F32),) + blk_d, (jax.ShapeDtypeStruct((t, d), BF16),) + blk_d], merge)

    tr = _tile(t, 512)
    rc = 32 if tr % 32 == 0 else None
    row_d = ((tr, d), lambda i, j: (i, 0))
    vec_d = ((1, d), lambda i, j: (0, 0))

    passed_ff1 = pass_on("ff1", merged)

    def post_mix(acc, i, j, extra_refs, out_refs, rows):
        x_r, gt1_r, g2_r, g3_r, sc2_r, sh2_r = extra_refs[:6]
        h1 = x_r[rows, :] + gt1_r[...] * (acc * _rms(acc) * g2_r[...])
        out_refs[0][rows, :] = acc
        out_refs[1][rows, :] = h1
        out_refs[2][rows, :] = ((h1 * _rms(h1) * g3_r[...]) * (1.0 + sc2_r[...]) + sh2_r[...]).astype(BF16)

    mo, h1, a2 = _mm("w_o_post_mix", merged, wf_o, _NN, t, d, d, tr, d, d,
                     [(x2,) + row_d, (gt1,) + vec_d, (g_post_mix,) + vec_d, (g_pre_ffn,) + vec_d, (sc2,) + vec_d, (sh2,) + vec_d]
                     + _after(passed_ff1),
                     [(jax.ShapeDtypeStruct((t, d), F32),) + row_d, (jax.ShapeDtypeStruct((t, d), F32),) + row_d,
                      (jax.ShapeDtypeStruct((t, d), BF16),) + row_d], post_mix, row_chunk=rc)

    tn_f = _tile(d_ff, 1024)
    blk_f = ((tm, tn_f), lambda i, j: (i, j))

    def relu_sq(acc, i, j, extra_refs, out_refs, rows):
        r = jnp.maximum(acc, 0.0)
        out_refs[0][...] = acc.astype(BF16)
        out_refs[1][...] = (r * r).astype(BF16)

    wf_1, = gathered_weights("ff1", a2)
    hff, act = _mm(
        "ff1", a2, wf_1, _NN, t, d_ff, d, tm, tn_f, d, [],
        [(jax.ShapeDtypeStruct((t, d_ff), BF16),) + blk_f, (jax.ShapeDtypeStruct((t, d_ff), BF16),) + blk_f], relu_sq)
    pass_on("ff2", hff)
    wf_2, = gathered_weights("ff2", act)

    sums_d = ((8, d), lambda i, j: (0, 0))

    def zero_first(sums_r, i, rows):
        if rows.start in (None, 0):
            @pl.when(i == 0)
            def _():
                sums_r[...] = jnp.zeros_like(sums_r)

    def loss_head(acc, i, j, extra_refs, out_refs, rows):
        h1_r, tgt_r, gt2_r, g4_r = extra_refs
        dy_r, dff_r, sums_r = out_refs
        r4 = _rms(acc)
        ffn = acc * r4
        n4 = ffn * g4_r[...]
        err = h1_r[rows, :] + gt2_r[...] * n4 - tgt_r[rows, :]
        dy = err * (1.0 / d)
        dy_r[rows, :] = dy
        dn4 = dy * gt2_r[...]
        dffn = dn4 * g4_r[...]
        dff_r[rows, :] = (r4 * (dffn - ffn * jnp.mean(dffn * ffn, axis=-1, keepdims=True))).astype(BF16)
        zero_first(sums_r, i, rows)

        sums_r[0:1, :] += _colsum(err * err)
        sums_r[1:2, :] += _colsum(dy * n4)
        sums_r[2:3, :] += _colsum(dn4 * ffn)

    tk_f = _tile(d_ff, 1024)
    dy, dff, sums_f = _mm("ff2_loss", act, wf_2, _NN, t, d, d_ff, tr, d, tk_f,
                          [(h1,) + row_d, (tgt,) + row_d, (gt2,) + vec_d, (g_post_ffn,) + vec_d],
                          [(jax.ShapeDtypeStruct((t, d), F32),) + row_d, (jax.ShapeDtypeStruct((t, d), BF16),) + row_d,
                           (jax.ShapeDtypeStruct((8, d), F32),) + sums_d], loss_head, row_chunk=rc)
    loss_mine = (0.5 / d) * jnp.sum(sums_f[0])

    def relu_sq_bwd(acc, i, j, extra_refs, out_refs, rows):
        out_refs[0][...] = (acc * (2.0 * jnp.maximum(extra_refs[0][...].astype(F32), 0.0))).astype(BF16)

    dhff, = _mm("d_hff", dff, wf_2, _NT, t, d_ff, d, tm, tn_f, d, [(hff,) + blk_f],
                [(jax.ShapeDtypeStruct((t, d_ff), BF16),) + blk_f], relu_sq_bwd)
    scatters = {}

    def send_grads(key, grads16, axes):
        plan = _scatter_plan(grads16, axes)
        scatters[key] = (plan,) + _split_start("scatter_%s_start" % key, plan)
        return scatters[key][3]

    def received_grads(key, after):
        plan, sems, thru, _ = scatters[key]
        return _split_wait("scatter_%s_wait" % key, plan, sems, thru, after)[1]

    gw_ff2, gw_ff2_16 = _grad_w("grad_w_ff2", act, dff)
    sent_ff2 = send_grads("ff2", [gw_ff2_16], big_axes[5:6])
    gw_ff1, gw_ff1_16 = _grad_w("grad_w_ff1", a2, dhff, token=sent_ff2)
    sent_ff1 = send_grads("ff1", [gw_ff1_16], big_axes[4:5])

    def pre_ffn_bwd(acc, i, j, extra_refs, out_refs, rows):
        h1_r, dy_r, mo_r, sc2_r, g3_r, gt1_r, g2_r = extra_refs[:7]
        dh1_r, dmo_r, sums_r = out_refs
        h1v = h1_r[rows, :]
        r3 = _rms(h1v)
        h1n = h1v * r3
        dn3 = acc * (1.0 + sc2_r[...])
        dh1n = dn3 * g3_r[...]
        dh1 = dy_r[rows, :] + r3 * (dh1n - h1n * jnp.mean(dh1n * h1n, axis=-1, keepdims=True))
        dh1_r[rows, :] = dh1
        mov = mo_r[rows, :]
        r2 = _rms(mov)
        mon = mov * r2
        dn2 = dh1 * gt1_r[...]
        dmon = dn2 * g2_r[...]
        dmo_r[rows, :] = (r2 * (dmon - mon * jnp.mean(dmon * mon, axis=-1, keepdims=True))).astype(BF16)
        zero_first(sums_r, i, rows)

        sums_r[0:1, :] += _colsum(acc)
        sums_r[1:2, :] += _colsum(acc * (h1n * g3_r[...]))
        sums_r[2:3, :] += _colsum(dn3 * h1n)
        sums_r[3:4, :] += _colsum(dh1 * (mon * g2_r[...]))
        sums_r[4:5, :] += _colsum(dn2 * mon)

    dh1, dmo, sums_m = _mm("d_a2_pre_ffn", dhff, wf_1, _NT, t, d, d_ff, tr, d, tk_f,
                           [(h1,) + row_d, (dy,) + row_d, (mo,) + row_d, (sc2,) + vec_d, (g_pre_ffn,) + vec_d,
                            (gt1,) + vec_d, (g_post_mix,) + vec_d] + _after(sent_ff1),
                           [(jax.ShapeDtypeStruct((t, d), F32),) + row_d, (jax.ShapeDtypeStruct((t, d), BF16),) + row_d,
                            (jax.ShapeDtypeStruct((8, d), F32),) + sums_d], pre_ffn_bwd, row_chunk=rc)
    gw_o, gw_o_16 = _grad_w("grad_w_o", merged, dmo)

    n_j = d // tn_d

    def merge_bwd_body(dmo_ref, wo_ref, ga_ref, gb_ref, ya_ref, yb_ref, dya_ref, dyb_ref, dproj_ref, acc_s):
        g = pl.program_id(2)

        @pl.when(g == 0)
        def _():
            dm = _dot(dmo_ref[...], wo_ref[...], _NT)
            acc_s[...] = dm
            sa = _sigmoid(ga_ref[...])
            dya_ref[...] = (dm * sa).astype(BF16)
            dproj_ref[...] = (dm * ya_ref[...] * sa * (1.0 - sa)).astype(BF16)

        @pl.when(g == 1)
        def _():
            dm = acc_s[...]
            sb = _sigmoid(gb_ref[...])
            dyb_ref[...] = (dm * sb).astype(BF16)
            dproj_ref[...] = (dm * yb_ref[...] * sb * (1.0 - sb)).astype(BF16)

    tile3 = pl.BlockSpec((tm, tn_d), lambda i, j, g: (i, j))
    dy_a, dy_b, dproj = pl.pallas_call(
        merge_bwd_body, name="d_merged", grid=(t // tm, n_j, 2),
        in_specs=[pl.BlockSpec((tm, d), lambda i, j, g: (i, 0)), pl.BlockSpec((tn_d, d), lambda i, j, g: (j, 0)),
                  pl.BlockSpec((tm, tn_d), lambda i, j, g: (i, ga_blk + j)),
                  pl.BlockSpec((tm, tn_d), lambda i, j, g: (i, gb_blk + j)), tile3, tile3],
        out_specs=[tile3, tile3, pl.BlockSpec((tm, tn_d), lambda i, j, g: (i, ga_blk + g * n_j + j))],
        out_shape=[jax.ShapeDtypeStruct((t, d), BF16), jax.ShapeDtypeStruct((t, d), BF16),
                   jax.ShapeDtypeStruct((t, n_in), BF16)],
        scratch_shapes=[pltpu.VMEM((tm, tn_d), F32)], compiler_params=_params(3),
    )(dmo, wf_o, proj, proj, y_a, y_b)

    def store_bf16(acc, i, j, extra_refs, out_refs, rows):
        out_refs[0][...] = acc.astype(BF16)

    tn_w = _tile(width, 512)
    blk_w = ((tm, tn_w), lambda i, j: (i, j))
    dout_a, = _mm("d_out_a", dy_a, wf_a, _NT, t, width, d, tm, tn_w, d, [],
                  [(jax.ShapeDtypeStruct((t, width), BF16),) + blk_w], store_bf16)
    dout_b, = _mm("d_out_b", dy_b, wf_b, _NT, t, width, d, tm, tn_w, d, [],
                  [(jax.ShapeDtypeStruct((t, width), BF16),) + blk_w], store_bf16)
    gw_a, gw_a_16 = _grad_w("grad_w_a_out", out_a, dy_a)
    gw_b, gw_b_16 = _grad_w("grad_w_b_out", out_b, dy_b)

    w_st = jnp.swapaxes(w_spatial[0], 1, 2)
    dproj, dg_sgu, dw_sp, dbst = _sgu_bwd(proj, dout_b, dproj, g_sgu_norm, w_spatial[0], w_st, bst, width, z_block)
    sent_mid = send_grads("mid", [gw_a_16, gw_b_16, gw_o_16], big_axes[1:4])
    dproj, dgh_heads, dlb, _ = _hgrn_bwd(proj, osum, dout_a, dproj, lb, g_hgrn_norm, width, _NO_EXCHANGE,
                                         after=sent_mid)
    gw_in, gw_in_16 = _grad_w("grad_w_in", a1, dproj)
    sent_in = send_grads("in", [gw_in_16], big_axes[:1])

    def pre_mix_bwd(acc, i, j, extra_refs, out_refs, rows):
        x_r, dh1_r, sc1_r, g1_r = extra_refs[:4]
        dx_r, sums_r = out_refs
        xv = x_r[rows, :]
        r1 = _rms(xv)
        xn = xv * r1
        dn1 = acc * (1.0 + sc1_r[...])
        dxn = dn1 * g1_r[...]
        dx_r[rows, :] = dh1_r[rows, :] + r1 * (dxn - xn * jnp.mean(dxn * xn, axis=-1, keepdims=True))
        zero_first(sums_r, i, rows)

        sums_r[0:1, :] += _colsum(acc)
        sums_r[1:2, :] += _colsum(acc * (xn * g1_r[...]))
        sums_r[2:3, :] += _colsum(dn1 * xn)

    tk_in = _tile(n_in, 1024)
    grad_x, sums_x = _mm(
        "d_a1_pre_mix", dproj, wf_in, _NT, t, d, n_in, tr, d, tk_in,
        [(x2,) + row_d, (dh1,) + row_d, (sc1,) + vec_d, (g_pre_mix,) + vec_d] + _after(sent_in),
        [(jax.ShapeDtypeStruct((t, d), F32),) + row_d, (jax.ShapeDtypeStruct((8, d), F32),) + sums_d],
        pre_mix_bwd, row_chunk=rc)

    dmod = jnp.concatenate([sums_x[0:2], sums_m[3:4], sums_m[0:2], sums_f[1:2]], axis=0).reshape(N_DEV, n_ada // LANE, LANE)
    ada_rows = -(-(n_ada // LANE) // 8) * 8
    dmod = jnp.pad(dmod, ((0, 0), (0, ada_rows - n_ada // LANE), (0, 0))).reshape(N_DEV * ada_rows, LANE)
    parts = [dmod, _rows(sums_x[2:3]), _rows(sums_m[4:5]), _rows(sums_m[2:3]), _rows(sums_f[2:3]),
             _rows(jnp.sum(dgh_heads, axis=0)), _rows(dg_sgu), _rows(dw_sp), _rows(dbst.T)]
    n_params = sum(p.shape[0] for p in parts)
    parts.append(jnp.full((8, LANE), loss_mine, F32))
    n_common = n_params + 8
    payload = jnp.concatenate(parts + [_rows(dlb)], axis=0)

    moms = [m_w_in, m_w_a_out, m_w_b_out, m_w_o, m_w_ff1, m_w_ff2]
    vars_ = [v_w_in, v_w_a_out, v_w_b_out, v_w_o, v_w_ff1, v_w_ff2]
    big_out = {}

    def big_update(nm, g_full, landing):
        k = big_names.index(nm)
        outs = _adamw_big("adamw_" + nm, me_arr, big[k], moms[k][0], vars_[k][0], g_full, landing, big_axes[k])
        big_out[nm] = [o[None] for o in outs]
        return outs[0]

    land_ff2, = received_grads("ff2", grad_x)
    done = big_update("w_ff2", [gw_ff2], land_ff2)
    land_ff1, = received_grads("ff1", done)
    done = big_update("w_ff1", [gw_ff1], land_ff1)
    land_a, land_b, land_o = received_grads("mid", done)
    big_update("w_a_out", [gw_a], land_a)
    big_update("w_b_out", [gw_b], land_b)
    done = big_update("w_o", [gw_o], land_o)

    payload, _ = lax.optimization_barrier((payload, done))
    gathered = _all_gather_small("gather_small_grads", payload)

    dmod_mine = lax.dynamic_slice_in_dim(gathered[:, :N_DEV * ada_rows, :].reshape(N_DEV, N_DEV, ada_rows * LANE),
                                         me, 1, axis=1)[:, 0, :n_ada]
    ada_out = [o[None] for o in _adamw_ada(sc_all.T, dmod_mine, w_ada[0], m_w_ada[0], v_w_ada[0])]

    def pack(b_, g1_, g2_, g3_, g4_, gh_, gs_, ws_, bs_):
        b3 = b_.reshape(N_DEV, n_ada // LANE, LANE)
        b3 = jnp.pad(b3, ((0, 0), (0, ada_rows - n_ada // LANE), (0, 0))).reshape(N_DEV * ada_rows, LANE)
        return jnp.concatenate([b3, _rows(g1_), _rows(g2_), _rows(g3_), _rows(g4_), _rows(gh_), _rows(gs_),
                                _rows(ws_), _rows(bs_), jnp.zeros((8, LANE), F32)], axis=0)

    small_w = (b_ada, g_pre_mix, g_post_mix, g_pre_ffn, g_post_ffn, g_hgrn_norm, g_sgu_norm, w_spatial, b_spatial)
    small_m = (m_b_ada, m_g_pre_mix, m_g_post_mix, m_g_pre_ffn, m_g_post_ffn, m_g_hgrn_norm, m_g_sgu_norm, m_w_spatial, m_b_spatial)
    small_v = (v_b_ada, v_g_pre_mix, v_g_post_mix, v_g_pre_ffn, v_g_post_ffn, v_g_hgrn_norm, v_g_sgu_norm, v_w_spatial, v_b_spatial)
    packed = _adamw_small(gathered[:, :n_common, :], pack(*small_w), pack(*small_m), pack(*small_v))

    def unpack(slab):
        outs, at = [], 0
        b3 = slab[:N_DEV * ada_rows].reshape(N_DEV, ada_rows, LANE)[:, :n_ada // LANE, :]
        outs.append(b3.reshape(b_ada.shape))
        at = N_DEV * ada_rows
        for ref in small_w[1:]:
            n_el = ref.size
            n_r = -(-(n_el // LANE) // 8) * 8
            outs.append(slab[at:at + n_el // LANE].reshape(ref.shape))
            at += n_r
        return outs

    small_out = [unpack(s) for s in packed]
    loss = packed[0][n_params, 0]

    dlb_all = gathered[:, n_common:n_common + 2 * heads, :].reshape(N_DEV, 2, heads, LANE)
    dlb_mine = lax.dynamic_index_in_dim(dlb_all, me, axis=2, keepdims=False)
    lb_out = _adamw_lb(dlb_mine, lb_logits, m_lb_logits, v_lb_logits)

    land_in, = received_grads("in", ada_out[0])
    big_update("w_in", [gw_in], land_in)

    order = ["w_ada", "b_ada", "g_pre_mix", "g_post_mix", "g_pre_ffn", "g_post_ffn", "w_in", "lb_logits", "g_hgrn_norm",
             "w_a_out", "g_sgu_norm", "w_spatial", "b_spatial", "w_b_out", "w_o", "w_ff1", "w_ff2"]
    small_names = ["b_ada", "g_pre_mix", "g_post_mix", "g_pre_ffn", "g_post_ffn", "g_hgrn_norm", "g_sgu_norm", "w_spatial", "b_spatial"]

    def leaf(kind, nm):
        if nm == "w_ada":
            return ada_out[kind]
        if nm == "lb_logits":
            return lb_out[kind]
        if nm in big_out:
            return big_out[nm][kind]
        return small_out[kind][small_names.index(nm)]

    result = [loss, grad_x[None]]
    for kind in range(4):
        result += [leaf(kind, nm) for nm in order]
    return tuple(result)
```

```python
import functools
import math

import jax
import jax.numpy as jnp
from jax import lax
from jax.experimental import pallas as pl
from jax.experimental.pallas import tpu as pltpu

F32 = jnp.float32
BF16 = jnp.bfloat16
MESH = pl.DeviceIdType.MESH
HIGHEST = lax.Precision.HIGHEST

N_DEV = 8
HEAD = 128
A_CHUNK = 32
N_MOD = 6
EPS = 1e-6
LANE = 128
VMEM_LIMIT = 60 * 1024 * 1024

ADAM_LR = 0.001
ADAM_B1 = 0.9
ADAM_B2 = 0.999
ADAM_EPS = 1e-08
ADAM_WD = 0.01
ADAM_STEP = 10

_NN = (((1,), (0,)), ((), ()))
_NT = (((1,), (1,)), ((), ()))
_TN = (((0,), (0,)), ((), ()))


def _dot(a, b, dims=_NN, precision=None):
    return lax.dot_general(a, b, dims, preferred_element_type=F32, precision=precision)


def _bdot(a, b, dims=_NN):
    return _dot(a.astype(BF16), b.astype(BF16), dims)


def _params(n_grid):
    return pltpu.CompilerParams(dimension_semantics=("arbitrary",) * n_grid, vmem_limit_bytes=VMEM_LIMIT)


def _dev_index():
    return lax.axis_index("x") * 4 + lax.axis_index("y") * 2 + lax.axis_index("c")


def _dev_coords(i):
    return (i // 4, (i // 2) % 2, i % 2)


def _sigmoid(x):
    return 1.0 / (1.0 + jnp.exp(-x))


def _erf(x):
    ax = jnp.abs(x)
    t = 1.0 / (1.0 + 0.3275911 * ax)
    poly = ((((1.061405429 * t - 1.453152027) * t + 1.421413741) * t - 0.284496736) * t + 0.254829592) * t
    y = 1.0 - poly * jnp.exp(-ax * ax)
    return jnp.where(x < 0, -y, y)


def _gelu_and_grad(x):
    cdf = 0.5 * (1.0 + _erf(x * (2.0 ** -0.5)))
    pdf = jnp.exp(-0.5 * x * x) * (1.0 / math.sqrt(2.0 * math.pi))
    return x * cdf, cdf + x * pdf


def _rms(x):
    return lax.rsqrt(jnp.mean(x * x, axis=-1, keepdims=True) + EPS)


def _colsum(x):
    return jnp.sum(x, axis=0, keepdims=True)


def _tile(n, want):
    if n <= want:
        return n
    t = (want // LANE) * LANE
    while n % t:
        t -= LANE
    assert t > 0, (n, want)
    return t


def _all_gather_small(name, payload):
    rows = payload.shape[0]

    def body(p_ref, out_ref, send_sems, recv_sems, local_sem):
        me = _dev_index()
        mine = pltpu.make_async_copy(p_ref, out_ref.at[me], local_sem)
        mine.start()
        sends = []
        for r in range(1, N_DEV):
            peer = (me + r) % N_DEV
            cp = pltpu.make_async_remote_copy(
                src_ref=p_ref, dst_ref=out_ref.at[me], send_sem=send_sems.at[r - 1], recv_sem=recv_sems.at[r - 1],
                device_id=_dev_coords(peer), device_id_type=MESH)
            cp.start()
            sends.append(cp)
        for r in range(1, N_DEV):
            src = (me + N_DEV - r) % N_DEV
            pltpu.make_async_remote_copy(
                src_ref=p_ref, dst_ref=out_ref.at[src], send_sem=send_sems.at[r - 1], recv_sem=recv_sems.at[r - 1],
                device_id=_dev_coords(src), device_id_type=MESH).wait_recv()
        for cp in sends:
            cp.wait_send()
        mine.wait()

    return pl.pallas_call(
        body, name=name,
        out_shape=jax.ShapeDtypeStruct((N_DEV, rows, LANE), F32),
        in_specs=[pl.BlockSpec(memory_space=pltpu.VMEM)],
        out_specs=pl.BlockSpec(memory_space=pltpu.VMEM),
        scratch_shapes=[pltpu.SemaphoreType.DMA((N_DEV - 1,)), pltpu.SemaphoreType.DMA((N_DEV - 1,)),
                        pltpu.SemaphoreType.DMA],
        compiler_params=pltpu.CompilerParams(vmem_limit_bytes=VMEM_LIMIT),
    )(payload)


def _region(ref, dev, axis, n):
    start = pl.multiple_of(dev * n, LANE if axis == 1 else 16)
    return ref.at[:, pl.ds(start, n)] if axis == 1 else ref.at[pl.ds(start, n), :]


class _Exchange:
    def __init__(self, arrays, out_shapes, sems, start, finish):
        self.arrays, self.out_shapes, self.sems, self.start, self.finish = arrays, out_shapes, sems, start, finish


def _gather_plan(shards, axes):
    n_w = len(shards)
    fulls = []
    for s, ax in zip(shards, axes):
        shp = (s.shape[0], s.shape[1] * N_DEV) if ax == 1 else (s.shape[0] * N_DEV, s.shape[1])
        fulls.append(jax.ShapeDtypeStruct(shp, BF16))
    widths = [s.shape[ax] for s, ax in zip(shards, axes)]

    def places():
        x, y, c = lax.axis_index("x"), lax.axis_index("y"), lax.axis_index("c")
        chips = [(1 - x, y), (x, 1 - y), (1 - x, 1 - y)]
        return (x, y, c), (x, y, 1 - c), chips

    def index(p):
        return p[0] * 4 + p[1] * 2 + p[2]

    def copy(w, k, s_refs, f_refs, sems, block, to, from_shard):
        send_sems, recv_sems, _ = sems
        dst = _region(f_refs[w], index(block), axes[w], widths[w])
        return pltpu.make_async_remote_copy(
            src_ref=s_refs[w] if from_shard else dst, dst_ref=dst,
            send_sem=send_sems.at[w, k], recv_sem=recv_sems.at[w, k], device_id=to, device_id_type=MESH)

    def local(w, s_refs, f_refs, sems, me):
        return pltpu.make_async_copy(s_refs[w], _region(f_refs[w], index(me), axes[w], widths[w]), sems[2].at[w])

    def start(s_refs, f_refs, sems):
        me, sib, chips = places()
        for w in range(n_w):
            local(w, s_refs, f_refs, sems, me).start()
            copy(w, 0, s_refs, f_refs, sems, me, sib, True).start()
            for j, chip in enumerate(chips):
                copy(w, 1 + j, s_refs, f_refs, sems, me, (*chip, me[2]), True).start()

    def finish(s_refs, f_refs, sems):
        me, sib, chips = places()
        for w in range(n_w):
            for j, chip in enumerate(chips):
                copy(w, 1 + j, s_refs, f_refs, sems, (*chip, me[2]), me, True).wait_recv()
                copy(w, 4 + j, s_refs, f_refs, sems, (*chip, me[2]), sib, False).start()
        for w in range(n_w):
            copy(w, 0, s_refs, f_refs, sems, sib, me, True).wait_recv()
            for j, chip in enumerate(chips):
                copy(w, 4 + j, s_refs, f_refs, sems, (*chip, sib[2]), me, False).wait_recv()
        for w in range(n_w):
            for k in range(N_DEV - 1):
                copy(w, k, s_refs, f_refs, sems, me, sib, True).wait_send()
            local(w, s_refs, f_refs, sems, me).wait()

    sems = [pltpu.SemaphoreType.DMA((n_w, N_DEV - 1)), pltpu.SemaphoreType.DMA((n_w, N_DEV - 1)),
            pltpu.SemaphoreType.DMA((n_w,))]
    return _Exchange(list(shards), fulls, sems, start, finish)


def _scatter_plan(grads, axes):
    n_w = len(grads)
    lands = []
    for g, ax in zip(grads, axes):
        shp = (g.shape[0], g.shape[1] // N_DEV) if ax == 1 else (g.shape[0] // N_DEV, g.shape[1])
        lands.append(jax.ShapeDtypeStruct((N_DEV - 1,) + shp, BF16))
    widths = [ld.shape[1 + ax] for ld, ax in zip(lands, axes)]

    def copy(w, r, g_refs, l_refs, sems, block, to):
        return pltpu.make_async_remote_copy(
            src_ref=_region(g_refs[w], block, axes[w], widths[w]), dst_ref=l_refs[w].at[r - 1],
            send_sem=sems[0].at[w * (N_DEV - 1) + r - 1], recv_sem=sems[1].at[w * (N_DEV - 1) + r - 1],
            device_id=_dev_coords(to), device_id_type=MESH)

    def start(g_refs, l_refs, sems):
        me = _dev_index()
        for w in range(n_w):
            for r in range(1, N_DEV):
                owner = (me + r) % N_DEV
                copy(w, r, g_refs, l_refs, sems, owner, owner).start()

    def finish(g_refs, l_refs, sems):
        me = _dev_index()
        for w in range(n_w):
            for r in range(1, N_DEV):
                copy(w, r, g_refs, l_refs, sems, me, (me + N_DEV - r) % N_DEV).wait_recv()
        for w in range(n_w):
            for r in range(1, N_DEV):
                copy(w, r, g_refs, l_refs, sems, me, (me + r) % N_DEV).wait_send()

    sems = [pltpu.SemaphoreType.DMA((n_w * (N_DEV - 1),)), pltpu.SemaphoreType.DMA((n_w * (N_DEV - 1),))]
    return _Exchange(list(grads), lands, sems, start, finish)


def _places():
    x, y, c = lax.axis_index("x"), lax.axis_index("y"), lax.axis_index("c")
    return (x, y, c), (x, y, 1 - c), [(1 - x, y), (x, 1 - y), (1 - x, 1 - y)]


def _place_index(p):
    return p[0] * 4 + p[1] * 2 + p[2]


def _gather_stage_plans(fulls, axes):
    n_w = len(fulls)
    widths = [f.shape[ax] // N_DEV for f, ax in zip(fulls, axes)]
    shapes = [jax.ShapeDtypeStruct(f.shape, f.dtype) for f in fulls]

    def copy(per, w, k, f_refs, sems, block, to):
        part = _region(f_refs[w], _place_index(block), axes[w], widths[w])
        return pltpu.make_async_remote_copy(
            src_ref=part, dst_ref=part, send_sem=sems[0].at[w * per + k], recv_sem=sems[1].at[w * per + k],
            device_id=to, device_id_type=MESH)

    def start1(_, f_refs, sems):
        me, sib, chips = _places()
        for w in range(n_w):
            copy(4, w, 0, f_refs, sems, me, sib).start()
            for j, chip in enumerate(chips):
                copy(4, w, 1 + j, f_refs, sems, me, (*chip, me[2])).start()

    def finish1(_, f_refs, sems):
        me, sib, chips = _places()
        for w in range(n_w):
            copy(4, w, 0, f_refs, sems, sib, me).wait_recv()
            for j, chip in enumerate(chips):
                copy(4, w, 1 + j, f_refs, sems, (*chip, me[2]), me).wait_recv()
        for w in range(n_w):
            for k in range(4):
                copy(4, w, k, f_refs, sems, me, sib).wait_send()

    def start2(_, f_refs, sems):
        me, sib, chips = _places()
        for w in range(n_w):
            for j, chip in enumerate(chips):
                copy(3, w, j, f_refs, sems, (*chip, me[2]), sib).start()

    def finish2(_, f_refs, sems):
        me, sib, chips = _places()
        for w in range(n_w):
            for j, chip in enumerate(chips):
                copy(3, w, j, f_refs, sems, (*chip, sib[2]), me).wait_recv()
        for w in range(n_w):
            for j, chip in enumerate(chips):
                copy(3, w, j, f_refs, sems, (*chip, me[2]), sib).wait_send()

    sems1 = [pltpu.SemaphoreType.DMA((n_w * 4,)), pltpu.SemaphoreType.DMA((n_w * 4,))]
    sems2 = [pltpu.SemaphoreType.DMA((n_w * 3,)), pltpu.SemaphoreType.DMA((n_w * 3,))]
    return _Exchange([], shapes, sems1, start1, finish1), _Exchange([], shapes, sems2, start2, finish2)


def _run_exchange(name, plan):
    n_in, n_out = len(plan.arrays), len(plan.out_shapes)

    def body(*refs):
        ins, outs, sems = refs[:n_in], refs[n_in:n_in + n_out], refs[n_in + n_out:]
        plan.start(ins, outs, sems)
        plan.finish(ins, outs, sems)

    any_spec = pl.BlockSpec(memory_space=pl.ANY)
    return pl.pallas_call(
        body, name=name, out_shape=plan.out_shapes,
        in_specs=[any_spec] * n_in, out_specs=[any_spec] * n_out, scratch_shapes=plan.sems,
    )(*plan.arrays)


_NO_EXCHANGE = _Exchange([], [], [], lambda i, o, s: None, lambda i, o, s: None)


def _direct_gather_plan(fulls, axes):
    n_w = len(fulls)
    widths = [f.shape[ax] // N_DEV for f, ax in zip(fulls, axes)]
    fulls = [jax.ShapeDtypeStruct(f.shape, f.dtype) for f in fulls]

    def copy(w, r, s_refs, f_refs, sems, block, to):
        part = _region(f_refs[w], block, axes[w], widths[w])
        return pltpu.make_async_remote_copy(
            src_ref=part, dst_ref=part,
            send_sem=sems[0].at[w * (N_DEV - 1) + r - 1], recv_sem=sems[1].at[w * (N_DEV - 1) + r - 1],
            device_id=_dev_coords(to), device_id_type=MESH)

    def start(s_refs, f_refs, sems):
        me = _dev_index()
        for w in range(n_w):
            for r in range(1, N_DEV):
                copy(w, r, s_refs, f_refs, sems, me, (me + r) % N_DEV).start()

    def finish(s_refs, f_refs, sems):
        me = _dev_index()
        for w in range(n_w):
            for r in range(1, N_DEV):
                src = (me + N_DEV - r) % N_DEV
                copy(w, r, s_refs, f_refs, sems, src, src).wait_recv()
        for w in range(n_w):
            for r in range(1, N_DEV):
                copy(w, r, s_refs, f_refs, sems, me, (me + r) % N_DEV).wait_send()

    sems = [pltpu.SemaphoreType.DMA((n_w * (N_DEV - 1),)), pltpu.SemaphoreType.DMA((n_w * (N_DEV - 1),))]
    return _Exchange([], fulls, sems, start, finish)


_HBM = pl.BlockSpec(memory_space=pltpu.HBM)
_SEM = pl.BlockSpec(memory_space=pltpu.SEMAPHORE)
_EFFECT = pltpu.SideEffectType.DATAFLOW_SIDE_EFFECTING


def _split_start(name, plan, landing=None):
    n_in, n_out, n_sem = len(plan.arrays), len(plan.out_shapes), len(plan.sems)

    def body(*refs):
        ins, lands = refs[:n_in], refs[n_in:n_in + n_out]
        sems = refs[n_in + n_out:n_in + n_out + n_sem]
        token = refs[-1]
        plan.start(ins, lands, sems)
        token[...] = jnp.zeros_like(token)

    hbm = lambda a: pltpu.HBM(a.shape, a.dtype)
    results = pl.pallas_call(
        body, name=name,
        out_shape=tuple(plan.sems) + tuple(hbm(a) for a in plan.arrays) + tuple(hbm(a) for a in plan.out_shapes)
        + (jax.ShapeDtypeStruct((8, LANE), F32),),
        in_specs=(_HBM,) * (n_in + n_out),
        out_specs=(_SEM,) * n_sem + (_HBM,) * (n_in + n_out) + (pl.BlockSpec(memory_space=pltpu.VMEM),),
        input_output_aliases={i: n_sem + i for i in range(n_in + n_out)},
        compiler_params=pltpu.CompilerParams(has_side_effects=_EFFECT),
    )(*[pltpu.with_memory_space_constraint(a, pltpu.HBM) for a in plan.arrays],
      *[pltpu.with_memory_space_constraint(a, pltpu.HBM)
        for a in (landing if landing is not None else [lax.empty(a.shape, a.dtype) for a in plan.out_shapes])])
    return results[:n_sem], results[n_sem:n_sem + n_in + n_out], results[-1]


def _split_wait(name, plan, sems, thru, after):
    n_in, n_out, n_sem = len(plan.arrays), len(plan.out_shapes), len(plan.sems)

    def body(*refs):
        ins, lands = refs[:n_in], refs[n_in:n_in + n_out]
        sem_refs = refs[n_in + n_out:n_in + n_out + n_sem]
        plan.finish(ins, lands, sem_refs)

    hbm = lambda a: pltpu.HBM(a.shape, a.dtype)
    results = pl.pallas_call(
        body, name=name,
        out_shape=tuple(hbm(a) for a in plan.arrays) + tuple(hbm(a) for a in plan.out_shapes),
        in_specs=(_HBM,) * (n_in + n_out) + (_SEM,) * n_sem + (pl.BlockSpec(memory_space=pl.ANY),),
        out_specs=(_HBM,) * (n_in + n_out),
        input_output_aliases={i: i for i in range(n_in + n_out)},
        compiler_params=pltpu.CompilerParams(has_side_effects=_EFFECT),
    )(*thru, *sems, after)
    return results[:n_in], results[n_in:]


def _cast_into_full(name, me, w, axis):
    r, c = w.shape
    tr = _tile(r, 256)
    if axis == 1:
        shape, place = (r, c * N_DEV), pl.BlockSpec((tr, c), lambda i, me_ref: (i, me_ref[0]))
    else:
        shape, place = (r * N_DEV, c), pl.BlockSpec((tr, c), lambda i, me_ref: (me_ref[0] * (r // tr) + i, 0))

    def body(me_ref, w_ref, o_ref):
        o_ref[...] = w_ref[...].astype(BF16)

    grid_spec = pltpu.PrefetchScalarGridSpec(
        num_scalar_prefetch=1, grid=(r // tr,),
        in_specs=[pl.BlockSpec((tr, c), lambda i, me_ref: (i, 0))], out_specs=place)
    return pl.pallas_call(body, name=name, grid_spec=grid_spec, out_shape=jax.ShapeDtypeStruct(shape, BF16),
                          compiler_params=_params(1))(me, w)


def _mm(name, a, b, dims, m, n, k, tm, tn, tk, extras, outs, epilogue, row_chunk=None, exchange=None,
        b_col_block=0):
    ni, nj, nk = m // tm, n // tn, k // tk
    ne, no = len(extras), len(outs)
    xin = len(exchange.arrays) if exchange else 0
    xout = len(exchange.out_shapes) if exchange else 0
    if dims == _TN:
        a_spec = pl.BlockSpec((tk, tm), lambda i, j, kk: (kk, i))
    else:
        a_spec = pl.BlockSpec((tm, tk), lambda i, j, kk: (i, kk))
    if dims == _NT:
        b_spec = pl.BlockSpec((tn, tk), lambda i, j, kk: (j, kk))
    else:
        b_spec = pl.BlockSpec((tk, tn), lambda i, j, kk: (kk, j + b_col_block))
    chunks = [slice(None)] if row_chunk is None else [slice(r, r + row_chunk) for r in range(0, tm, row_chunk)]

    def lift(index_map):
        return lambda i, j, kk: index_map(i, j)

    def body(a_ref, b_ref, *rest):
        extra_refs, rest = rest[:ne], rest[ne:]
        xin_refs, rest = rest[:xin], rest[xin:]
        out_refs, rest = rest[:no], rest[no:]
        xout_refs, rest = rest[:xout], rest[xout:]
        i, j, kk = pl.program_id(0), pl.program_id(1), pl.program_id(2)
        if exchange:
            sem_refs = rest[1:] if nk > 1 else rest

            @pl.when((i == 0) & (j == 0) & (kk == 0))
            def _():
                exchange.start(xin_refs, xout_refs, sem_refs)

        if nk == 1:
            part = _dot(a_ref[...], b_ref[...], dims)
            for rows in chunks:
                epilogue(part[rows], i, j, extra_refs, out_refs, rows)
        else:
            acc_ref = rest[0]

            @pl.when(kk == 0)
            def _():
                acc_ref[...] = _dot(a_ref[...], b_ref[...], dims)

            @pl.when(kk > 0)
            def _():
                acc_ref[...] += _dot(a_ref[...], b_ref[...], dims)

            @pl.when(kk == nk - 1)
            def _():
                for rows in chunks:
                    epilogue(acc_ref[rows, :], i, j, extra_refs, out_refs, rows)

        if exchange:
            @pl.when((i == ni - 1) & (j == nj - 1) & (kk == nk - 1))
            def _():
                exchange.finish(xin_refs, xout_refs, sem_refs)

    any_spec = pl.BlockSpec(memory_space=pl.ANY)
    once = dict(pipeline_mode=pl.Buffered(1)) if (row_chunk is not None and nk > 1) else {}
    results = pl.pallas_call(
        body, name=name,
        grid=(ni, nj, nk),
        in_specs=[a_spec, b_spec] + [pl.BlockSpec(bs, lift(im), **once) for _, bs, im in extras] + [any_spec] * xin,
        out_specs=[pl.BlockSpec(bs, lift(im), **once) for _, bs, im in outs] + [any_spec] * xout,
        out_shape=[sd for sd, _, _ in outs] + (list(exchange.out_shapes) if exchange else []),
        scratch_shapes=([pltpu.VMEM((tm, tn), F32)] if nk > 1 else []) + (list(exchange.sems) if exchange else []),
        compiler_params=_params(3),
    )(a, b, *[arr for arr, _, _ in extras], *(exchange.arrays if exchange else []))
    return (results[:no], results[no:]) if exchange else results


def _after(token):
    return [(token, (8, LANE), lambda i, j: (0, 0))]


def _grad_w(name, a, dc, token=None, tm=512, tn=1024, cols=None):
    t, m = a.shape
    first, n = cols if cols is not None else (0, dc.shape[1])
    tm, tn = _tile(m, tm), _tile(n, tn)
    assert first % tn == 0

    def epilogue(acc, i, j, extra_refs, out_refs, rows):
        out_refs[0][...] = acc
        out_refs[1][...] = acc.astype(BF16)

    blk = ((tm, tn), lambda i, j: (i, j))
    return _mm(name, a, dc, _TN, m, n, t, tm, tn, t, _after(token) if token is not None else [],
               [(jax.ShapeDtypeStruct((m, n), F32),) + blk, (jax.ShapeDtypeStruct((m, n), BF16),) + blk], epilogue,
               b_col_block=first // tn)


def _proj_gather(a1, w_shard, order):
    t, d = a1.shape
    nsh = w_shard.shape[1]
    tm = _tile(t, 512)
    n_i = t // tm

    def body(ord_ref, a_ref, wsh_ref, proj_ref, full_ref, bbuf, bsem, send_sems, recv_sems, own_sem):
        s, i = pl.program_id(0), pl.program_id(1)
        me, sib, chips = _places()
        blocks = [me, sib] + [(*ch, me[2]) for ch in chips] + [(*ch, sib[2]) for ch in chips]

        def part(block):
            return _region(full_ref, _place_index(block), 1, nsh)

        def remote(k, block, to, from_shard=False):
            return pltpu.make_async_remote_copy(
                src_ref=wsh_ref if from_shard else part(block), dst_ref=part(block),
                send_sem=send_sems.at[k], recv_sem=recv_sems.at[k], device_id=to, device_id_type=MESH)

        def load(pos):
            src = wsh_ref if pos == 0 else part(blocks[pos])
            return pltpu.make_async_copy(src, bbuf.at[pos % 2], bsem.at[pos % 2])

        own = pltpu.make_async_copy(wsh_ref, part(me), own_sem)

        @pl.when((s == 0) & (i == 0))
        def _():
            own.start()
            remote(0, me, sib, True).start()
            for j, ch in enumerate(chips):
                remote(1 + j, me, (*ch, me[2]), True).start()
            load(0).start()
            load(0).wait()

        for pos in range(1, N_DEV):
            @pl.when((s == pos) & (i == 0))
            def _():
                load(pos).wait()

        for pos in range(N_DEV - 1):
            @pl.when((s == pos) & (i == n_i - 1))
            def _():
                nxt = pos + 1
                if nxt == 1:
                    remote(0, sib, me).wait_recv()
                elif nxt <= 4:
                    remote(nxt - 1, blocks[nxt], me).wait_recv()
                    remote(nxt + 2, blocks[nxt], sib).start()
                else:
                    remote(nxt - 1, blocks[nxt], me).wait_recv()
                load(nxt).start()

        proj_ref[...] = _dot(a_ref[...], bbuf[s % 2])

        @pl.when((s == N_DEV - 1) & (i == n_i - 1))
        def _():
            for k in range(N_DEV - 1):
                remote(k, me, sib, True).wait_send()
            own.wait()

    grid_spec = pltpu.PrefetchScalarGridSpec(
        num_scalar_prefetch=1, grid=(N_DEV, n_i),
        in_specs=[pl.BlockSpec((tm, d), lambda s, i, ord_ref: (i, 0)), pl.BlockSpec(memory_space=pl.ANY)],
        out_specs=[pl.BlockSpec((tm, nsh), lambda s, i, ord_ref: (i, ord_ref[s])), pl.BlockSpec(memory_space=pl.ANY)],
        scratch_shapes=[pltpu.VMEM((2, d, nsh), BF16), pltpu.SemaphoreType.DMA((2,)),
                        pltpu.SemaphoreType.DMA((N_DEV - 1,)), pltpu.SemaphoreType.DMA((N_DEV - 1,)),
                        pltpu.SemaphoreType.DMA])
    return pl.pallas_call(
        body, name="proj_gather", grid_spec=grid_spec,
        out_shape=[jax.ShapeDtypeStruct((t, nsh * N_DEV), F32), jax.ShapeDtypeStruct((d, nsh * N_DEV), BF16)],
        compiler_params=_params(2),
    )(order, a1, w_shard)


def _cast_bf16(name, w):
    r, c = w.shape
    tr = _tile(r, 256)
    return pl.pallas_call(
        lambda w_ref, o_ref: o_ref.__setitem__(Ellipsis, w_ref[...].astype(BF16)), name=name,
        grid=(r // tr,), in_specs=[pl.BlockSpec((tr, c), lambda i: (i, 0))],
        out_specs=pl.BlockSpec((tr, c), lambda i: (i, 0)), out_shape=jax.ShapeDtypeStruct((r, c), BF16),
        compiler_params=_params(1),
    )(w)


def _prep_small(c_row, lb_logits):
    d = c_row.shape[1]
    rows = d // LANE

    def body(c_ref, l_ref, o_ref):
        cv = c_ref[...]
        o_ref[0:rows, :] = cv * _sigmoid(cv)
        lbs = [_sigmoid(l_ref[dr][0:1, :] - l_ref[dr][1:2, :]) for dr in range(2)]
        o_ref[rows:rows + 8, :] = jnp.concatenate(lbs + [jnp.zeros((6, LANE), F32)], axis=0)

    return pl.pallas_call(
        body, name="prep_small", out_shape=jax.ShapeDtypeStruct((rows + 8, LANE), F32),
    )(c_row.reshape(rows, LANE), lb_logits)


def _mod_shard(sc_all, w_ada_shard, b_shard):
    d, n = w_ada_shard.shape
    tn = _tile(n, 512)

    def body(s_ref, w_ref, b_ref, o_ref):
        o_ref[...] = _dot(s_ref[...], w_ref[...], precision=HIGHEST) + b_ref[...]

    return pl.pallas_call(
        body, name="mod_shard", grid=(n // tn,),
        in_specs=[pl.BlockSpec((N_DEV, d), lambda j: (0, 0)), pl.BlockSpec((d, tn), lambda j: (0, j)),
                  pl.BlockSpec((1, tn), lambda j: (0, j))],
        out_specs=pl.BlockSpec((N_DEV, tn), lambda j: (0, j)),
        out_shape=jax.ShapeDtypeStruct((N_DEV, n), F32), compiler_params=_params(1),
    )(sc_all, w_ada_shard, b_shard)


def _norm_mod(x, gain, shift, scale):
    t, d = x.shape
    tm = _tile(t, 512)

    def body(x_ref, g_ref, sh_ref, sc_ref, o_ref):
        xv = x_ref[...]
        o_ref[...] = ((xv * _rms(xv) * g_ref[...]) * (1.0 + sc_ref[...]) + sh_ref[...]).astype(BF16)

    vec = pl.BlockSpec((1, d), lambda i: (0, 0))
    return pl.pallas_call(
        body, name="norm_mod", grid=(t // tm,),
        in_specs=[pl.BlockSpec((tm, d), lambda i: (i, 0)), vec, vec, vec],
        out_specs=pl.BlockSpec((tm, d), lambda i: (i, 0)), out_shape=jax.ShapeDtypeStruct((t, d), BF16),
        compiler_params=_params(1),
    )(x, gain, shift, scale)


def _chunk_masks():
    row = lax.broadcasted_iota(jnp.int32, (HEAD, HEAD), 0)
    col = lax.broadcasted_iota(jnp.int32, (HEAD, HEAD), 1)
    same = (row // A_CHUNK) == (col // A_CHUNK)
    return same & (col <= row), same & (col >= row)


def _ones(mask):
    return jnp.where(mask, 1.0, 0.0).astype(BF16)


def _dot_split(ones_bf16, x):
    hi = x.astype(BF16)
    lo = (x - hi.astype(F32)).astype(BF16)
    return _dot(ones_bf16, hi) + _dot(ones_bf16, lo)


def _hgrn_block(direction, f, lb, cum2):
    sf = _sigmoid(f)
    big_f = lb + (1.0 - lb) * sf
    k = (1.0 - lb) * (1.0 - sf)
    lf = jnp.log(big_f)
    both = _dot_split(cum2, lf)
    cf, cr = both[:HEAD], both[HEAD:]
    b, rest = (cf, cr - lf) if direction == 0 else (cr, cf - lf)
    return k, sf, big_f, jnp.exp(b), jnp.exp(-b), jnp.exp(rest)


def _hgrn_fwd(proj, lb, g_norm, width, exchange, after):
    t = proj.shape[0]
    heads = width // HEAD
    nb, nc = t // HEAD, t // A_CHUNK
    ua = 4 if nb % 4 == 0 else (2 if nb % 2 == 0 else 1)
    ub = 16 if nc % 16 == 0 else (8 if nc % 8 == 0 else 4)
    q_scale = HEAD ** -0.5
    xin, xout = len(exchange.arrays), len(exchange.out_shapes)

    def body(q_ref, ffw_ref, fbw_ref, v_ref, og_ref, lb_ref, g_ref, *rest):
        xin_refs, rest = rest[:xin], rest[xin + len(after):]
        outa_ref, osum_ref = rest[:2]
        xout_refs, rest = rest[2:2 + xout], rest[2 + xout:]
        qd_s, ke_s, dc_s, o_s = rest[:4]
        sem_refs = rest[4:]
        h = pl.program_id(0)

        @pl.when(h == 0)
        def _():
            exchange.start(xin_refs, xout_refs, sem_refs)

        tril, triu = _chunk_masks()
        cum2 = jnp.concatenate([_ones(tril), _ones(triu)], axis=0)
        f_refs = (ffw_ref, fbw_ref)
        lbs = (lb_ref[0:1, :], lb_ref[1:2, :])

        def phase_a(it, carry):
            loaded = []
            for u in range(ua):
                rows = pl.ds(pl.multiple_of((it * ua + u) * HEAD, HEAD), HEAD)
                loaded.append((rows, q_ref[rows, :], v_ref[rows, :], ffw_ref[rows, :], fbw_ref[rows, :]))
            chains = [(d, rows, qv * q_scale, vv.astype(BF16), fv)
                      for rows, qv, vv, f0, f1 in loaded for d, fv in ((0, f0), (1, f1))]
            blocks = [_hgrn_block(d, fv, lbs[d], cum2) for d, _, _, _, fv in chains]
            scaled = [(qv * eb, k * enb, k * erest, eb * erest)
                      for (_, _, qv, _, _), (k, _, _, eb, enb, erest) in zip(chains, blocks)]
            atts = [jnp.where(tril if d == 0 else triu, _bdot(qd, kd, _NT), 0.0)
                    for (d, _, _, _, _), (qd, kd, _, _) in zip(chains, scaled)]
            intras = [_bdot(att, vv) for att, (_, _, _, vv, _) in zip(atts, chains)]
            results = [(d, rows, o_intra, qd.astype(BF16), ke.astype(BF16), decay)
                       for (d, rows, _, _, _), (qd, _, ke, decay), o_intra in zip(chains, scaled, intras)]
            for d, rows, o_intra, qd16, ke16, decay in results:
                o_s[d, rows, :] = o_intra
                qd_s[d, rows, :] = qd16
                ke_s[d, rows, :] = ke16
                dc_s[d, rows, :] = decay
            return carry

        lax.fori_loop(0, nb // ua, phase_a, 0)

        def phase_b(it, states):
            loaded = []
            for u in range(ub):
                n = it * ub + u
                for d in range(2):
                    c = n if d == 0 else nc - 1 - n
                    start = pl.multiple_of(c * A_CHUNK, A_CHUNK)
                    rows = pl.ds(start, A_CHUNK)
                    loaded.append((d, rows, qd_s[d, rows, :], ke_s[d, rows, :], v_ref[rows, :],
                                   dc_s[d, pl.ds(start, 1), :], o_s[d, rows, :]))
            increments = [_dot(vv.astype(BF16), ke16, _TN) for _, _, _, ke16, vv, _, _ in loaded]
            states = list(states)
            befores = []
            for (d, _, _, _, _, decay, _), inc in zip(loaded, increments):
                befores.append(states[d].astype(BF16))
                states[d] = states[d] * decay + inc
            inters = [_dot(qd16, before, _NT) for (_, _, qd16, _, _, _, _), before in zip(loaded, befores)]
            for (d, rows, _, _, _, _, o_intra), o_inter in zip(loaded, inters):
                o_s[d, rows, :] = o_intra + o_inter
            return tuple(states)

        zero_state = jnp.zeros((HEAD, HEAD), F32)
        lax.fori_loop(0, nc // ub, phase_b, (zero_state, zero_state))

        def phase_c(i, carry):
            rows = pl.ds(pl.multiple_of(i * HEAD, HEAD), HEAD)
            o = o_s[0, rows, :] + o_s[1, rows, :]
            osum_ref[rows, :] = o
            og = og_ref[rows, :]
            outa_ref[rows, :] = (o * _rms(o) * g_ref[...] * (og * _sigmoid(og))).astype(BF16)
            return carry

        lax.fori_loop(0, nb, phase_c, 0)

        @pl.when(h == heads - 1)
        def _():
            exchange.finish(xin_refs, xout_refs, sem_refs)

    def col(p):
        return pl.BlockSpec((t, HEAD), lambda h: (0, p * heads + h))

    any_spec = pl.BlockSpec(memory_space=pl.ANY)
    results = pl.pallas_call(
        body, name="hgrn_fwd", grid=(heads,),
        in_specs=[col(0), col(1), col(2), col(3), col(4),
                  pl.BlockSpec((2, HEAD), lambda h: (0, h)), pl.BlockSpec((1, HEAD), lambda h: (0, 0))]
        + [any_spec] * (xin + len(after)),
        out_specs=[pl.BlockSpec((t, HEAD), lambda h: (0, h)), pl.BlockSpec((t, HEAD), lambda h: (0, h))] + [any_spec] * xout,
        out_shape=[jax.ShapeDtypeStruct((t, width), BF16), jax.ShapeDtypeStruct((t, width), F32)] + list(exchange.out_shapes),
        scratch_shapes=[pltpu.VMEM((2, t, HEAD), BF16), pltpu.VMEM((2, t, HEAD), BF16), pltpu.VMEM((2, t, HEAD), F32),
                        pltpu.VMEM((2, t, HEAD), F32)] + list(exchange.sems),
        compiler_params=_params(1),
    )(proj, proj, proj, proj, proj, lb, g_norm, *exchange.arrays, *after)
    return results[0], results[1], results[2:]


def _sgu_core(u_pre, v_pre, g_v, ws_ref, bst):
    u, du = _gelu_and_grad(u_pre)
    v, dv = _gelu_and_grad(v_pre)
    mu = jnp.mean(v, axis=-1, keepdims=True)
    dlt = v - mu
    rstd = lax.rsqrt(jnp.mean(dlt * dlt, axis=-1, keepdims=True) + EPS)
    vhat = dlt * rstd
    vn = vhat * g_v
    groups = vn.shape[1] // HEAD
    cols = []
    for g in range(groups):
        vm_g = _bdot(ws_ref[g], vn[:, g * HEAD:(g + 1) * HEAD]) + bst[:, g:g + 1]
        cols.append(vm_g)
    return u, du, dv, vhat, rstd, vn, jnp.concatenate(cols, axis=1)


def _sgu_fwd(proj, g_v, w_s, bst, width, z_block):
    t = proj.shape[0]

    def body(u_ref, v_ref, g_ref, ws_ref, bst_ref, o_ref):
        u, _, _, _, _, _, vm = _sgu_core(u_ref[...], v_ref[...], g_ref[...], ws_ref, bst_ref[...])
        o_ref[...] = (u * vm).astype(BF16)

    groups = width // HEAD
    return pl.pallas_call(
        body, name="sgu_fwd", grid=(t // HEAD,),
        in_specs=[pl.BlockSpec((HEAD, width), lambda i: (i, z_block)), pl.BlockSpec((HEAD, width), lambda i: (i, z_block + 1)),
                  pl.BlockSpec((1, width), lambda i: (0, 0)), pl.BlockSpec((groups, HEAD, HEAD), lambda i: (0, 0, 0)),
                  pl.BlockSpec((HEAD, groups), lambda i: (0, 0))],
        out_specs=pl.BlockSpec((HEAD, width), lambda i: (i, 0)),
        out_shape=jax.ShapeDtypeStruct((t, width), BF16), compiler_params=_params(1),
    )(proj, proj, g_v, w_s, bst)


def _sgu_bwd(proj, dout_b, dproj, g_v, w_s, w_st, bst, width, z_block):
    t = proj.shape[0]
    groups = width // HEAD
    nblk = t // HEAD

    def body(u_ref, v_ref, do_ref, g_ref, ws_ref, wst_ref, bst_ref, dproj_hbm,
             dz_ref, dg_ref, dws_ref, dbst_ref, res_s):
        i, p = pl.program_id(0), pl.program_id(1)

        @pl.when((i == 0) & (p == 0))
        def _():
            dg_ref[...] = jnp.zeros_like(dg_ref)
            dws_ref[...] = jnp.zeros_like(dws_ref)
            dbst_ref[...] = jnp.zeros_like(dbst_ref)

        @pl.when(p == 0)
        def _():
            g_v = g_ref[...]
            u, du, dv, vhat, rstd, vn, vm = _sgu_core(u_ref[...], v_ref[...], g_v, ws_ref, bst_ref[...])
            dout = do_ref[...].astype(F32)
            res_s[0] = (dout * vm * du).astype(BF16)
            dvm = dout * u
            dvn_cols = []
            for g in range(groups):
                sl = slice(g * HEAD, (g + 1) * HEAD)
                dvm_g = dvm[:, sl]
                dbst_ref[:, g:g + 1] += jnp.sum(dvm_g, axis=1, keepdims=True)
                dws_ref[g] += _bdot(dvm_g, vn[:, sl], _NT)
                dvn_cols.append(_bdot(wst_ref[g], dvm_g))
            dvn = jnp.concatenate(dvn_cols, axis=1)
            dg_ref[...] += _colsum(dvn * vhat)
            dvh = dvn * g_v
            dvg = rstd * (dvh - jnp.mean(dvh, axis=-1, keepdims=True)
                          - vhat * jnp.mean(dvh * vhat, axis=-1, keepdims=True))
            res_s[1] = (dvg * dv).astype(BF16)

        dz_ref[...] = res_s[p]

    n_in = dproj.shape[1]
    return pl.pallas_call(
        body, name="sgu_bwd", grid=(nblk, 2),
        in_specs=[pl.BlockSpec((HEAD, width), lambda i, p: (i, z_block)),
                  pl.BlockSpec((HEAD, width), lambda i, p: (i, z_block + 1)),
                  pl.BlockSpec((HEAD, width), lambda i, p: (i, 0)),
                  pl.BlockSpec((1, width), lambda i, p: (0, 0)),
                  pl.BlockSpec((groups, HEAD, HEAD), lambda i, p: (0, 0, 0)),
                  pl.BlockSpec((groups, HEAD, HEAD), lambda i, p: (0, 0, 0)),
                  pl.BlockSpec((HEAD, groups), lambda i, p: (0, 0)),
                  pl.BlockSpec(memory_space=pl.ANY)],
        out_specs=[pl.BlockSpec((HEAD, width), lambda i, p: (i, z_block + p)),
                   pl.BlockSpec((1, width), lambda i, p: (0, 0)),
                   pl.BlockSpec((groups, HEAD, HEAD), lambda i, p: (0, 0, 0)),
                   pl.BlockSpec((HEAD, groups), lambda i, p: (0, 0))],
        out_shape=[jax.ShapeDtypeStruct((t, n_in), BF16), jax.ShapeDtypeStruct((1, width), F32),
                   jax.ShapeDtypeStruct((groups, HEAD, HEAD), F32), jax.ShapeDtypeStruct((HEAD, groups), F32)],
        scratch_shapes=[pltpu.VMEM((2, HEAD, width), BF16)],
        input_output_aliases={7: 0},
        compiler_params=_params(2),
    )(proj, proj, dout_b, g_v, w_s, w_st, bst, dproj)


def _hgrn_bwd(proj, osum, dout_a, dproj, lb, g_norm, width, exchange, after):
    t = proj.shape[0]
    heads = width // HEAD
    nb = t // HEAD
    cpb = HEAD // A_CHUNK
    ubk = 2 if nb % 2 == 0 else 1
    q_scale = HEAD ** -0.5
    xin, xout = len(exchange.arrays), len(exchange.out_shapes)

    def body(q_ref, ffw_ref, fbw_ref, v_ref, og_ref, osum_ref, douta_ref, lb_ref, g_ref, dproj_hbm, *rest):
        xin_refs, rest = rest[:xin], rest[xin + 1:]
        out_ref, dgh_ref, dlb_ref = rest[:3]
        xout_refs, rest = rest[3:3 + xout], rest[3 + xout:]
        do_s, dq_s, dv_s, res_s, ck_s = rest[:5]
        sem_refs = rest[5:]
        h, p = pl.program_id(0), pl.program_id(1)
        f_refs = (ffw_ref, fbw_ref)

        @pl.when((h == 0) & (p == 0))
        def _():
            exchange.start(xin_refs, xout_refs, sem_refs)

        @pl.when(p == 0)
        def _():
            tril, triu = _chunk_masks()
            cum2 = jnp.concatenate([_ones(tril), _ones(triu)], axis=0)
            g_row = g_ref[...]

            def pass_norm(i, dgh):
                rows = pl.ds(pl.multiple_of(i * HEAD, HEAD), HEAD)
                o = osum_ref[rows, :]
                r = _rms(o)
                oh = o * r
                og = og_ref[rows, :]
                sg = _sigmoid(og)
                dout = douta_ref[rows, :].astype(F32)
                don = dout * (og * sg)
                res_s[4, rows, :] = (dout * (oh * g_row) * (sg * (1.0 + og * (1.0 - sg)))).astype(BF16)
                doh = don * g_row
                do_s[rows, :] = r * (doh - oh * jnp.mean(doh * oh, axis=-1, keepdims=True))
                return dgh + _colsum(don * oh)

            dgh_ref[...] = lax.fori_loop(0, nb, pass_norm, jnp.zeros((1, HEAD), F32))

            lbs = (lb_ref[0:1, :], lb_ref[1:2, :])
            zero_state = jnp.zeros((HEAD, HEAD), F32)

            def chunk_order(d):
                return list(range(cpb)) if d == 0 else list(range(cpb - 1, -1, -1))

            def chunk(x, j):
                return x[j * A_CHUNK:(j + 1) * A_CHUNK, :]

            def decay_row(e_big, j):
                return e_big[j * A_CHUNK:j * A_CHUNK + 1, :]

            def cat(parts):
                return jnp.concatenate([parts[j] for j in range(cpb)], axis=0)

            def block_states(d, start, incs, e_big):
                befores, st = {}, start
                for j in chunk_order(d):
                    befores[j] = st
                    st = st * decay_row(e_big, j) + incs[j]
                return befores, st

            def pass_states(it, states):
                loaded = []
                for u in range(ubk):
                    for d in range(2):
                        blk = it * ubk + u if d == 0 else nb - 1 - (it * ubk + u)
                        rows = pl.ds(pl.multiple_of(blk * HEAD, HEAD), HEAD)
                        loaded.append((d, blk, f_refs[d][rows, :], v_ref[rows, :]))
                blocks = [_hgrn_block(d, fv, lbs[d], cum2) for d, _, fv, _ in loaded]
                incs = [{j: _bdot(chunk(vv, j), chunk(k * erest, j), _TN) for j in range(cpb)}
                        for (_, _, _, vv), (k, _, _, _, _, erest) in zip(loaded, blocks)]
                states, starts = list(states), []
                for (d, _, _, _), (_, _, _, eb, _, erest), inc in zip(loaded, blocks, incs):
                    starts.append(states[d])
                    states[d] = block_states(d, states[d], inc, eb * erest)[1]
                for (d, blk, _, _), start in zip(loaded, starts):
                    ck_s[d, blk] = start
                return tuple(states)

            lax.fori_loop(0, nb // ubk, pass_states, (zero_state, zero_state))

            def pass_back(it, carry):
                gts, dlb = [carry[0], carry[1]], carry[2]
                loaded = []
                for u, d in ((u, d) for u in range(ubk) for d in range(2)):
                    blk = nb - 1 - (it * ubk + u) if d == 0 else it * ubk + u
                    rows = pl.ds(pl.multiple_of(blk * HEAD, HEAD), HEAD)
                    loaded.append((d, rows, f_refs[d][rows, :], q_ref[rows, :], v_ref[rows, :], do_s[rows, :], ck_s[d, blk]))
                blocks = [_hgrn_block(d, fv, lbs[d], cum2) for d, _, fv, _, _, _, _ in loaded]
                scaled = []
                for (_, _, _, qv, _, _, _), (k, _, _, eb, enb, erest) in zip(loaded, blocks):
                    qh = qv * q_scale
                    scaled.append((qh, qh * eb, k * enb, k * erest, eb * erest))
                masks = [tril if d == 0 else triu for d, *_ in loaded]
                atts = [jnp.where(m, _bdot(qd, kd, _NT), 0.0) for m, (_, qd, kd, _, _) in zip(masks, scaled)]
                datts = [jnp.where(m, _bdot(do, vv, _NT), 0.0) for m, (_, _, _, _, vv, do, _) in zip(masks, loaded)]
                dvs = [_bdot(att, do, _TN) for att, (_, _, _, _, _, do, _) in zip(atts, loaded)]
                dqds = [_bdot(datt, kd) for datt, (_, _, kd, _, _) in zip(datts, scaled)]
                dkds = [_bdot(datt, qd, _TN) for datt, (_, qd, _, _, _) in zip(datts, scaled)]
                s_incs = [{j: _bdot(chunk(vv, j), chunk(ke, j), _TN) for j in range(cpb)}
                          for (_, _, _, _, vv, _, _), (_, _, _, ke, _) in zip(loaded, scaled)]
                g_incs = [{j: _bdot(chunk(do, j), chunk(qd, j), _TN) for j in range(cpb)}
                          for (_, _, _, _, _, do, _), (_, qd, _, _, _) in zip(loaded, scaled)]
                befores, afters, g_at = [], [], []
                for (d, _, _, _, _, _, ck), (_, _, _, _, e_big), s_inc, g_inc in zip(loaded, scaled, s_incs, g_incs):
                    order = chunk_order(d)
                    before, after = block_states(d, ck, s_inc, e_big)
                    befores.append(before)
                    afters.append({j: (before[order[n + 1]] if n + 1 < cpb else after) for n, j in enumerate(order)})
                    at, gt = {}, gts[d]
                    for j in reversed(order):
                        at[j] = gt
                        gt = gt * decay_row(e_big, j) + g_inc[j]
                    gts[d] = gt
                    g_at.append(at)
                dqd_i = [{j: _bdot(chunk(do, j), before[j]) for j in range(cpb)}
                         for (_, _, _, _, _, do, _), before in zip(loaded, befores)]
                dv_i = [{j: _bdot(chunk(ke, j), at[j], _NT) for j in range(cpb)}
                        for (_, _, _, ke, _), at in zip(scaled, g_at)]
                dke = [{j: _bdot(chunk(vv, j), at[j]) for j in range(cpb)}
                       for (_, _, _, _, vv, _, _), at in zip(loaded, g_at)]
                results, new = [], []
                for n, ((d, rows, _, _, _, _, _), (k, sf, big_f, eb, enb, erest), (qh, _, _, _, _)) in enumerate(
                        zip(loaded, blocks, scaled)):
                    dqh = (dqds[n] + cat(dqd_i[n])) * eb
                    dk = dkds[n] * enb + cat(dke[n]) * erest
                    carry_rows = {j: jnp.broadcast_to(_colsum(g_at[n][j] * afters[n][j]), (A_CHUNK, HEAD))
                                  for j in range(cpb)}
                    dlf = _dot_split(_ones(triu if d == 0 else tril), qh * dqh - k * dk) + cat(carry_rows)
                    common = dlf / big_f - dk
                    results.append((d, rows, (k * sf * common).astype(BF16), dqh.astype(BF16),
                                    (dvs[n] + cat(dv_i[n])).astype(BF16)))
                    new.append(_colsum((1.0 - sf) * common))
                for d, rows, df16, dq16, dv16 in results:
                    res_s[1 + d, rows, :] = df16
                    dq_s[d, rows, :] = dq16
                    dv_s[d, rows, :] = dv16
                per_dir = [sum(c for (d, *_), c in zip(loaded, new) if d == dd) for dd in range(2)]
                return gts[0], gts[1], dlb + jnp.concatenate(per_dir, axis=0)

            dlb_ref[...] = lax.fori_loop(0, nb // ubk, pass_back,
                                         (zero_state, zero_state, jnp.zeros((2, HEAD), F32)))[2]

            def pass_out(i, carry):
                rows = pl.ds(pl.multiple_of(i * HEAD, HEAD), HEAD)
                dq = dq_s[0, rows, :].astype(F32) + dq_s[1, rows, :].astype(F32)
                res_s[0, rows, :] = (dq * q_scale).astype(BF16)
                res_s[3, rows, :] = (dv_s[0, rows, :].astype(F32) + dv_s[1, rows, :].astype(F32)).astype(BF16)
                return carry

            lax.fori_loop(0, nb, pass_out, 0)

        out_ref[...] = res_s[p]

        @pl.when((h == heads - 1) & (p == 4))
        def _():
            exchange.finish(xin_refs, xout_refs, sem_refs)

    def col(pp):
        return pl.BlockSpec((t, HEAD), lambda h, p: (0, pp * heads + h))

    n_in = dproj.shape[1]
    any_spec = pl.BlockSpec(memory_space=pl.ANY)
    results = pl.pallas_call(
        body, name="hgrn_bwd", grid=(heads, 5),
        in_specs=[col(0), col(1), col(2), col(3), col(4),
                  pl.BlockSpec((t, HEAD), lambda h, p: (0, h)), pl.BlockSpec((t, HEAD), lambda h, p: (0, h)),
                  pl.BlockSpec((2, HEAD), lambda h, p: (0, h)), pl.BlockSpec((1, HEAD), lambda h, p: (0, 0)),
                  any_spec] + [any_spec] * (xin + 1),
        out_specs=[pl.BlockSpec((t, HEAD), lambda h, p: (0, p * heads + h)),
                   pl.BlockSpec((None, 1, HEAD), lambda h, p: (h, 0, 0)),
                   pl.BlockSpec((2, HEAD), lambda h, p: (0, h))] + [any_spec] * xout,
        out_shape=[jax.ShapeDtypeStruct((t, n_in), BF16), jax.ShapeDtypeStruct((heads, 1, HEAD), F32),
                   jax.ShapeDtypeStruct((2, width), F32)] + list(exchange.out_shapes),
        scratch_shapes=[pltpu.VMEM((t, HEAD), F32), pltpu.VMEM((2, t, HEAD), BF16), pltpu.VMEM((2, t, HEAD), BF16),
                        pltpu.VMEM((5, t, HEAD), BF16), pltpu.VMEM((2, nb, HEAD, HEAD), F32)] + list(exchange.sems),
        input_output_aliases={9: 0},
        compiler_params=_params(2),
    )(proj, proj, proj, proj, proj, osum, dout_a, lb, g_norm, dproj, *exchange.arrays, after)
    return results[0], results[1], results[2], results[3:]


def _adamw(w, g, m, v):
    m = ADAM_B1 * m + (1.0 - ADAM_B1) * g
    v = ADAM_B2 * v + (1.0 - ADAM_B2) * (g * g)
    m_hat = m / (1.0 - ADAM_B1 ** ADAM_STEP)
    v_hat = v / (1.0 - ADAM_B2 ** ADAM_STEP)
    delta = -ADAM_LR * (m_hat / (jnp.sqrt(v_hat) + ADAM_EPS) + ADAM_WD * w)
    return delta, m, v


def _adamw_big(name, me, w, m, v, g_parts, landing, axis):
    r, c = w.shape
    tr = _tile(r, 128)
    n_parts = len(g_parts)
    per = N_DEV // n_parts

    def body(me_ref, w_ref, m_ref, v_ref, *rest):
        g_refs, (l_ref, og_ref, od_ref, om_ref, ov_ref) = rest[:n_parts], rest[n_parts:]
        g = g_refs[0][...]
        for p in range(1, n_parts):
            g = jnp.where(me_ref[0] // per == p, g_refs[p][...], g)
        for s in range(N_DEV - 1):
            g = g + l_ref[s].astype(F32)
        og_ref[...] = g
        od_ref[...], om_ref[...], ov_ref[...] = _adamw(w_ref[...], g, m_ref[...], v_ref[...])

    shard = pl.BlockSpec((tr, c), lambda i, me_ref: (i, 0))
    if axis == 1:
        own = pl.BlockSpec((tr, c), lambda i, me_ref: (i, me_ref[0] % per))
    else:
        assert n_parts == 1
        own = pl.BlockSpec((tr, c), lambda i, me_ref: (me_ref[0] * (r // tr) + i, 0))
    grid_spec = pltpu.PrefetchScalarGridSpec(
        num_scalar_prefetch=1, grid=(r // tr,),
        in_specs=[shard, shard, shard] + [own] * n_parts + [pl.BlockSpec((N_DEV - 1, tr, c), lambda i, me_ref: (0, i, 0))],
        out_specs=[shard] * 4)
    return pl.pallas_call(
        body, name=name, grid_spec=grid_spec, out_shape=[jax.ShapeDtypeStruct((r, c), F32)] * 4,
        compiler_params=_params(1),
    )(me, w, m, v, *g_parts, landing)


def _adamw_ada(sct, dmod_mine, w, m, v):
    d, n = w.shape
    tr = _tile(d, 256)

    def body(s_ref, dm_ref, w_ref, m_ref, v_ref, og_ref, od_ref, om_ref, ov_ref):
        g = _dot(s_ref[...], dm_ref[...], precision=HIGHEST)
        og_ref[...] = g
        od_ref[...], om_ref[...], ov_ref[...] = _adamw(w_ref[...], g, m_ref[...], v_ref[...])

    blk = pl.BlockSpec((tr, n), lambda i: (i, 0))
    return pl.pallas_call(
        body, name="adamw_ada", grid=(d // tr,),
        in_specs=[pl.BlockSpec((tr, N_DEV), lambda i: (i, 0)), pl.BlockSpec((N_DEV, n), lambda i: (0, 0)), blk, blk, blk],
        out_specs=[blk] * 4, out_shape=[jax.ShapeDtypeStruct((d, n), F32)] * 4, compiler_params=_params(1),
    )(sct, dmod_mine, w, m, v)


def _adamw_small(gathered, w, m, v):
    def body(g_ref, w_ref, m_ref, v_ref, og_ref, od_ref, om_ref, ov_ref):
        g = g_ref[0]
        for s in range(1, N_DEV):
            g = g + g_ref[s]
        og_ref[...] = g
        od_ref[...], om_ref[...], ov_ref[...] = _adamw(w_ref[...], g, m_ref[...], v_ref[...])

    return pl.pallas_call(
        body, name="adamw_small", out_shape=[jax.ShapeDtypeStruct(w.shape, F32)] * 4,
        compiler_params=pltpu.CompilerParams(vmem_limit_bytes=VMEM_LIMIT),
    )(gathered, w, m, v)


def _adamw_lb(dlb_mine, lb_logits, m, v):
    def body(d_ref, l_ref, m_ref, v_ref, og_ref, od_ref, om_ref, ov_ref):
        dlb = d_ref[0]
        for s in range(1, N_DEV):
            dlb = dlb + d_ref[s]
        for dr in range(2):
            lb = _sigmoid(l_ref[dr][0:1, :] - l_ref[dr][1:2, :])
            d0 = dlb[dr:dr + 1] * lb * (1.0 - lb)
            g = jnp.concatenate([d0, -d0], axis=0)
            og_ref[dr] = g
            od_ref[dr], om_ref[dr], ov_ref[dr] = _adamw(l_ref[dr], g, m_ref[dr], v_ref[dr])

    return pl.pallas_call(body, name="adamw_lb", out_shape=[jax.ShapeDtypeStruct(lb_logits.shape, F32)] * 4,
                          )(dlb_mine, lb_logits, m, v)


def _rows(a, pad_to=8):
    flat = a.reshape(-1, LANE)
    pad = (-flat.shape[0]) % pad_to
    return jnp.pad(flat, ((0, pad), (0, 0))) if pad else flat


def kernel(x, c, w_ada, b_ada, g_pre_mix, g_post_mix, g_pre_ffn, g_post_ffn, w_in, lb_logits, g_hgrn_norm, w_a_out, g_sgu_norm, w_spatial, b_spatial, w_b_out, w_o, w_ff1, w_ff2, loss_target, m_w_ada, m_b_ada, m_g_pre_mix, m_g_post_mix, m_g_pre_ffn, m_g_post_ffn, m_w_in, m_lb_logits, m_g_hgrn_norm, m_w_a_out, m_g_sgu_norm, m_w_spatial, m_b_spatial, m_w_b_out, m_w_o, m_w_ff1, m_w_ff2, v_w_ada, v_b_ada, v_g_pre_mix, v_g_post_mix, v_g_pre_ffn, v_g_post_ffn, v_w_in, v_lb_logits, v_g_hgrn_norm, v_w_a_out, v_g_sgu_norm, v_w_spatial, v_b_spatial, v_w_b_out, v_w_o, v_w_ff1, v_w_ff2):
    t, d = x.shape[1], x.shape[2]
    n_in = w_in.shape[2] * N_DEV
    width = (n_in - 2 * d) // 7
    heads = width // HEAD
    assert heads == N_DEV and width % LANE == 0
    d_ff = w_ff1.shape[2] * N_DEV
    n_ada = w_ada.shape[2]
    me = _dev_index()
    me_arr = me.reshape(1).astype(jnp.int32)
    x2, tgt = x[0], loss_target[0]

    big = [w_in[0], w_a_out[0], w_b_out[0], w_o[0], w_ff1[0], w_ff2[0]]
    big_axes = [1, 1, 1, 0, 1, 0]
    big_names = ["w_in", "w_a_out", "w_b_out", "w_o", "w_ff1", "w_ff2"]
    w_in16 = _cast_bf16("cast_w_in", big[0])
    own_parts = [_cast_into_full("cast_" + nm, me_arr, w, ax) for nm, w, ax in zip(big_names[1:], big[1:], big_axes[1:])]

    c_rows = d // LANE
    small = _all_gather_small("gather_c_lb", _prep_small(c[0:1], lb_logits))
    sc_all = small[:, :c_rows, :].reshape(N_DEV, d)
    lb = jnp.transpose(small[:, c_rows:c_rows + 2, :], (1, 0, 2)).reshape(2, width)
    b_shard = lax.dynamic_slice_in_dim(b_ada, me * n_ada, n_ada, axis=1)
    mod_sh = _mod_shard(sc_all, w_ada[0], b_shard)
    mod_all = _all_gather_small("gather_mod", _rows(mod_sh))
    mod_all = mod_all[:, :N_DEV * n_ada // LANE, :].reshape(N_DEV, N_DEV, n_ada)
    mod6 = lax.dynamic_index_in_dim(mod_all, me, axis=1, keepdims=False).reshape(N_MOD, d)
    sh1, sc1, gt1, sh2, sc2, gt2 = [mod6[i:i + 1] for i in range(N_MOD)]

    a1 = _norm_mod(x2, g_pre_mix, sh1, sc1)
    tm = _tile(t, 512)

    def store_f32(acc, i, j, extra_refs, out_refs, rows):
        out_refs[0][...] = acc

    xq, yq, cq = lax.axis_index("x"), lax.axis_index("y"), lax.axis_index("c")
    chips = [(1 - xq, yq), (xq, 1 - yq), (1 - xq, 1 - yq)]
    order = jnp.stack([me, 4 * xq + 2 * yq + 1 - cq] + [4 * a + 2 * b + cq for a, b in chips]
                      + [4 * a + 2 * b + 1 - cq for a, b in chips]).astype(jnp.int32)
    proj, wf_in = _proj_gather(a1, w_in16, order)

    proj, own_parts = lax.optimization_barrier((proj, own_parts))
    gathers = {}
    for key, lo, hi in (("mid", 1, 4), ("ff1", 4, 5), ("ff2", 5, 6)):
        far, near = _gather_stage_plans(own_parts[lo - 1:hi - 1], big_axes[lo:hi])
        gathers[key] = [far, near, _split_start("gather_%s_start" % key, far, landing=own_parts[lo - 1:hi - 1])]

    def pass_on(key, after):
        far, near, (sems, thru, _) = gathers[key]
        parts = _split_wait("gather_%s_wait" % key, far, sems, thru, after)[1]
        gathers[key].append(_split_start("pass_%s_start" % key, near, landing=list(parts)))
        return gathers[key][3][2]

    def gathered_weights(key, after):
        near, (sems, thru, _) = gathers[key][1], gathers[key][3]
        return _split_wait("pass_%s_wait" % key, near, sems, thru, after)[1]

    out_a, osum, _ = _hgrn_fwd(proj, lb, g_hgrn_norm, width, _NO_EXCHANGE,
                               after=[gathers[key][2][2] for key in ("mid", "ff1", "ff2")])
    passed_mid = pass_on("mid", out_a)
    z_block = 5
    bst = b_spatial[0].T
    out_b = _sgu_fwd(proj, g_sgu_norm, w_spatial[0], bst, width, z_block)
    wf_a, wf_b, wf_o = gathered_weights("mid", out_b)

    tn_d = _tile(d, 512)
    blk_d = ((tm, tn_d), lambda i, j: (i, j))
    y_a, = _mm("y_a", out_a, wf_a, _NN, t, d, width, tm, tn_d, width, _after(passed_mid),
               [(jax.ShapeDtypeStruct((t, d), F32),) + blk_d], store_f32)
    ga_blk = (5 * width + 2 * width) // tn_d
    gb_blk = ga_blk + d // tn_d

    def merge(acc, i, j, extra_refs, out_refs, rows):
        ga, gb, ya = extra_refs
        out_refs[0][...] = acc
        out_refs[1][...] = (_sigmoid(ga[...]) * ya[...] + _sigmoid(gb[...]) * acc).astype(BF16)

    y_b, merged = _mm("y_b_merge", out_b, wf_b, _NN, t, d, width, tm, tn_d, width,
                      [(proj, (tm, tn_d), lambda i, j: (i, ga_blk + j)), (proj, (tm, tn_d), lambda i, j: (i, gb_blk + j)),
                       (y_a,) + blk_d],
                      [(jax.ShapeDtypeStruct((t, d), F32),) + blk_d, (jax.ShapeDtypeStruct((t, d), BF16),) + blk_d], merge)

    tr = _tile(t, 512)
    rc = 32 if tr % 32 == 0 else None
    row_d = ((tr, d), lambda i, j: (i, 0))
    vec_d = ((1, d), lambda i, j: (0, 0))

    passed_ff1 = pass_on("ff1", merged)

    def post_mix(acc, i, j, extra_refs, out_refs, rows):
        x_r, gt1_r, g2_r, g3_r, sc2_r, sh2_r = extra_refs[:6]
        h1 = x_r[rows, :] + gt1_r[...] * (acc * _rms(acc) * g2_r[...])
        out_refs[0][rows, :] = acc
        out_refs[1][rows, :] = h1
        out_refs[2][rows, :] = ((h1 * _rms(h1) * g3_r[...]) * (1.0 + sc2_r[...]) + sh2_r[...]).astype(BF16)

    mo, h1, a2 = _mm("w_o_post_mix", merged, wf_o, _NN, t, d, d, tr, d, d,
                     [(x2,) + row_d, (gt1,) + vec_d, (g_post_mix,) + vec_d, (g_pre_ffn,) + vec_d, (sc2,) + vec_d, (sh2,) + vec_d]
                     + _after(passed_ff1),
                     [(jax.ShapeDtypeStruct((t, d), F32),) + row_d, (jax.ShapeDtypeStruct((t, d), F32),) + row_d,
                      (jax.ShapeDtypeStruct((t, d), BF16),) + row_d], post_mix, row_chunk=rc)

    tn_f = _tile(d_ff, 1024)
    blk_f = ((tm, tn_f), lambda i, j: (i, j))

    def relu_sq(acc, i, j, extra_refs, out_refs, rows):
        r = jnp.maximum(acc, 0.0)
        out_refs[0][...] = acc.astype(BF16)
        out_refs[1][...] = (r * r).astype(BF16)

    wf_1, = gathered_weights("ff1", a2)
    hff, act = _mm(
        "ff1", a2, wf_1, _NN, t, d_ff, d, tm, tn_f, d, [],
        [(jax.ShapeDtypeStruct((t, d_ff), BF16),) + blk_f, (jax.ShapeDtypeStruct((t, d_ff), BF16),) + blk_f], relu_sq)
    pass_on("ff2", hff)
    wf_2, = gathered_weights("ff2", act)

    sums_d = ((8, d), lambda i, j: (0, 0))

    def zero_first(sums_r, i, rows):
        if rows.start in (None, 0):
            @pl.when(i == 0)
            def _():
                sums_r[...] = jnp.zeros_like(sums_r)

    def loss_head(acc, i, j, extra_refs, out_refs, rows):
        h1_r, tgt_r, gt2_r, g4_r = extra_refs
        dy_r, dff_r, sums_r = out_refs
        r4 = _rms(acc)
        ffn = acc * r4
        n4 = ffn * g4_r[...]
        err = h1_r[rows, :] + gt2_r[...] * n4 - tgt_r[rows, :]
        dy = err * (1.0 / d)
        dy_r[rows, :] = dy
        dn4 = dy * gt2_r[...]
        dffn = dn4 * g4_r[...]
        dff_r[rows, :] = (r4 * (dffn - ffn * jnp.mean(dffn * ffn, axis=-1, keepdims=True))).astype(BF16)
        zero_first(sums_r, i, rows)

        sums_r[0:1, :] += _colsum(err * err)
        sums_r[1:2, :] += _colsum(dy * n4)
        sums_r[2:3, :] += _colsum(dn4 * ffn)

    tk_f = _tile(d_ff, 1024)
    dy, dff, sums_f = _mm("ff2_loss", act, wf_2, _NN, t, d, d_ff, tr, d, tk_f,
                          [(h1,) + row_d, (tgt,) + row_d, (gt2,) + vec_d, (g_post_ffn,) + vec_d],
                          [(jax.ShapeDtypeStruct((t, d), F32),) + row_d, (jax.ShapeDtypeStruct((t, d), BF16),) + row_d,
                           (jax.ShapeDtypeStruct((8, d), F32),) + sums_d], loss_head, row_chunk=rc)
    loss_mine = (0.5 / d) * jnp.sum(sums_f[0])

    def relu_sq_bwd(acc, i, j, extra_refs, out_refs, rows):
        out_refs[0][...] = (acc * (2.0 * jnp.maximum(extra_refs[0][...].astype(F32), 0.0))).astype(BF16)

    dhff, = _mm("d_hff", dff, wf_2, _NT, t, d_ff, d, tm, tn_f, d, [(hff,) + blk_f],
                [(jax.ShapeDtypeStruct((t, d_ff), BF16),) + blk_f], relu_sq_bwd)
    scatters = {}

    def send_grads(key, grads16, axes):
        plan = _scatter_plan(grads16, axes)
        scatters[key] = (plan,) + _split_start("scatter_%s_start" % key, plan)
        return scatters[key][3]

    def received_grads(key, after):
        plan, sems, thru, _ = scatters[key]
        return _split_wait("scatter_%s_wait" % key, plan, sems, thru, after)[1]

    gw_ff2, gw_ff2_16 = _grad_w("grad_w_ff2", act, dff)
    sent_ff2 = send_grads("ff2", [gw_ff2_16], big_axes[5:6])
    gw_ff1, gw_ff1_16 = _grad_w("grad_w_ff1", a2, dhff, token=sent_ff2)
    sent_ff1 = send_grads("ff1", [gw_ff1_16], big_axes[4:5])

    def pre_ffn_bwd(acc, i, j, extra_refs, out_refs, rows):
        h1_r, dy_r, mo_r, sc2_r, g3_r, gt1_r, g2_r = extra_refs[:7]
        dh1_r, dmo_r, sums_r = out_refs
        h1v = h1_r[rows, :]
        r3 = _rms(h1v)
        h1n = h1v * r3
        dn3 = acc * (1.0 + sc2_r[...])
        dh1n = dn3 * g3_r[...]
        dh1 = dy_r[rows, :] + r3 * (dh1n - h1n * jnp.mean(dh1n * h1n, axis=-1, keepdims=True))
        dh1_r[rows, :] = dh1
        mov = mo_r[rows, :]
        r2 = _rms(mov)
        mon = mov * r2
        dn2 = dh1 * gt1_r[...]
        dmon = dn2 * g2_r[...]
        dmo_r[rows, :] = (r2 * (dmon - mon * jnp.mean(dmon * mon, axis=-1, keepdims=True))).astype(BF16)
        zero_first(sums_r, i, rows)

        sums_r[0:1, :] += _colsum(acc)
        sums_r[1:2, :] += _colsum(acc * (h1n * g3_r[...]))
        sums_r[2:3, :] += _colsum(dn3 * h1n)
        sums_r[3:4, :] += _colsum(dh1 * (mon * g2_r[...]))
        sums_r[4:5, :] += _colsum(dn2 * mon)

    dh1, dmo, sums_m = _mm("d_a2_pre_ffn", dhff, wf_1, _NT, t, d, d_ff, tr, d, tk_f,
                           [(h1,) + row_d, (dy,) + row_d, (mo,) + row_d, (sc2,) + vec_d, (g_pre_ffn,) + vec_d,
                            (gt1,) + vec_d, (g_post_mix,) + vec_d] + _after(sent_ff1),
                           [(jax.ShapeDtypeStruct((t, d), F32),) + row_d, (jax.ShapeDtypeStruct((t, d), BF16),) + row_d,
                            (jax.ShapeDtypeStruct((8, d), F32),) + sums_d], pre_ffn_bwd, row_chunk=rc)
    gw_o, gw_o_16 = _grad_w("grad_w_o", merged, dmo)

    n_j = d // tn_d

    def merge_bwd_body(dmo_ref, wo_ref, ga_ref, gb_ref, ya_ref, yb_ref, dya_ref, dyb_ref, dproj_ref, acc_s):
        g = pl.program_id(2)

        @pl.when(g == 0)
        def _():
            dm = _dot(dmo_ref[...], wo_ref[...], _NT)
            acc_s[...] = dm
            sa = _sigmoid(ga_ref[...])
            dya_ref[...] = (dm * sa).astype(BF16)
            dproj_ref[...] = (dm * ya_ref[...] * sa * (1.0 - sa)).astype(BF16)

        @pl.when(g == 1)
        def _():
            dm = acc_s[...]
            sb = _sigmoid(gb_ref[...])
            dyb_ref[...] = (dm * sb).astype(BF16)
            dproj_ref[...] = (dm * yb_ref[...] * sb * (1.0 - sb)).astype(BF16)

    tile3 = pl.BlockSpec((tm, tn_d), lambda i, j, g: (i, j))
    dy_a, dy_b, dproj = pl.pallas_call(
        merge_bwd_body, name="d_merged", grid=(t // tm, n_j, 2),
        in_specs=[pl.BlockSpec((tm, d), lambda i, j, g: (i, 0)), pl.BlockSpec((tn_d, d), lambda i, j, g: (j, 0)),
                  pl.BlockSpec((tm, tn_d), lambda i, j, g: (i, ga_blk + j)),
                  pl.BlockSpec((tm, tn_d), lambda i, j, g: (i, gb_blk + j)), tile3, tile3],
        out_specs=[tile3, tile3, pl.BlockSpec((tm, tn_d), lambda i, j, g: (i, ga_blk + g * n_j + j))],
        out_shape=[jax.ShapeDtypeStruct((t, d), BF16), jax.ShapeDtypeStruct((t, d), BF16),
                   jax.ShapeDtypeStruct((t, n_in), BF16)],
        scratch_shapes=[pltpu.VMEM((tm, tn_d), F32)], compiler_params=_params(3),
    )(dmo, wf_o, proj, proj, y_a, y_b)

    def store_bf16(acc, i, j, extra_refs, out_refs, rows):
        out_refs[0][...] = acc.astype(BF16)

    tn_w = _tile(width, 512)
    blk_w = ((tm, tn_w), lambda i, j: (i, j))
    dout_a, = _mm("d_out_a", dy_a, wf_a, _NT, t, width, d, tm, tn_w, d, [],
                  [(jax.ShapeDtypeStruct((t, width), BF16),) + blk_w], store_bf16)
    dout_b, = _mm("d_out_b", dy_b, wf_b, _NT, t, width, d, tm, tn_w, d, [],
                  [(jax.ShapeDtypeStruct((t, width), BF16),) + blk_w], store_bf16)
    gw_a, gw_a_16 = _grad_w("grad_w_a_out", out_a, dy_a)
    gw_b, gw_b_16 = _grad_w("grad_w_b_out", out_b, dy_b)

    w_st = jnp.swapaxes(w_spatial[0], 1, 2)
    dproj, dg_sgu, dw_sp, dbst = _sgu_bwd(proj, dout_b, dproj, g_sgu_norm, w_spatial[0], w_st, bst, width, z_block)
    sent_mid = send_grads("mid", [gw_a_16, gw_b_16, gw_o_16], big_axes[1:4])
    dproj, dgh_heads, dlb, _ = _hgrn_bwd(proj, osum, dout_a, dproj, lb, g_hgrn_norm, width, _NO_EXCHANGE,
                                         after=sent_mid)
    gw_in, gw_in_16 = _grad_w("grad_w_in", a1, dproj)
    sent_in = send_grads("in", [gw_in_16], big_axes[:1])

    def pre_mix_bwd(acc, i, j, extra_refs, out_refs, rows):
        x_r, dh1_r, sc1_r, g1_r = extra_refs[:4]
        dx_r, sums_r = out_refs
        xv = x_r[rows, :]
        r1 = _rms(xv)
        xn = xv * r1
        dn1 = acc * (1.0 + sc1_r[...])
        dxn = dn1 * g1_r[...]
        dx_r[rows, :] = dh1_r[rows, :] + r1 * (dxn - xn * jnp.mean(dxn * xn, axis=-1, keepdims=True))
        zero_first(sums_r, i, rows)

        sums_r[0:1, :] += _colsum(acc)
        sums_r[1:2, :] += _colsum(acc * (xn * g1_r[...]))
        sums_r[2:3, :] += _colsum(dn1 * xn)

    tk_in = _tile(n_in, 1024)
    grad_x, sums_x = _mm(
        "d_a1_pre_mix", dproj, wf_in, _NT, t, d, n_in, tr, d, tk_in,
        [(x2,) + row_d, (dh1,) + row_d, (sc1,) + vec_d, (g_pre_mix,) + vec_d] + _after(sent_in),
        [(jax.ShapeDtypeStruct((t, d), F32),) + row_d, (jax.ShapeDtypeStruct((8, d), F32),) + sums_d],
        pre_mix_bwd, row_chunk=rc)

    dmod = jnp.concatenate([sums_x[0:2], sums_m[3:4], sums_m[0:2], sums_f[1:2]], axis=0).reshape(N_DEV, n_ada // LANE, LANE)
    ada_rows = -(-(n_ada // LANE) // 8) * 8
    dmod = jnp.pad(dmod, ((0, 0), (0, ada_rows - n_ada // LANE), (0, 0))).reshape(N_DEV * ada_rows, LANE)
    parts = [dmod, _rows(sums_x[2:3]), _rows(sums_m[4:5]), _rows(sums_m[2:3]), _rows(sums_f[2:3]),
             _rows(jnp.sum(dgh_heads, axis=0)), _rows(dg_sgu), _rows(dw_sp), _rows(dbst.T)]
    n_params = sum(p.shape[0] for p in parts)
    parts.append(jnp.full((8, LANE), loss_mine, F32))
    n_common = n_params + 8
    payload = jnp.concatenate(parts + [_rows(dlb)], axis=0)

    moms = [m_w_in, m_w_a_out, m_w_b_out, m_w_o, m_w_ff1, m_w_ff2]
    vars_ = [v_w_in, v_w_a_out, v_w_b_out, v_w_o, v_w_ff1, v_w_ff2]
    big_out = {}

    def big_update(nm, g_full, landing):
        k = big_names.index(nm)
        outs = _adamw_big("adamw_" + nm, me_arr, big[k], moms[k][0], vars_[k][0], g_full, landing, big_axes[k])
        big_out[nm] = [o[None] for o in outs]
        return outs[0]

    land_ff2, = received_grads("ff2", grad_x)
    done = big_update("w_ff2", [gw_ff2], land_ff2)
    land_ff1, = received_grads("ff1", done)
    done = big_update("w_ff1", [gw_ff1], land_ff1)
    land_a, land_b, land_o = received_grads("mid", done)
    big_update("w_a_out", [gw_a], land_a)
    big_update("w_b_out", [gw_b], land_b)
    done = big_update("w_o", [gw_o], land_o)

    payload, _ = lax.optimization_barrier((payload, done))
    gathered = _all_gather_small("gather_small_grads", payload)

    dmod_mine = lax.dynamic_slice_in_dim(gathered[:, :N_DEV * ada_rows, :].reshape(N_DEV, N_DEV, ada_rows * LANE),
                                         me, 1, axis=1)[:, 0, :n_ada]
    ada_out = [o[None] for o in _adamw_ada(sc_all.T, dmod_mine, w_ada[0], m_w_ada[0], v_w_ada[0])]

    def pack(b_, g1_, g2_, g3_, g4_, gh_, gs_, ws_, bs_):
        b3 = b_.reshape(N_DEV, n_ada // LANE, LANE)
        b3 = jnp.pad(b3, ((0, 0), (0, ada_rows - n_ada // LANE), (0, 0))).reshape(N_DEV * ada_rows, LANE)
        return jnp.concatenate([b3, _rows(g1_), _rows(g2_), _rows(g3_), _rows(g4_), _rows(gh_), _rows(gs_),
                                _rows(ws_), _rows(bs_), jnp.zeros((8, LANE), F32)], axis=0)

    small_w = (b_ada, g_pre_mix, g_post_mix, g_pre_ffn, g_post_ffn, g_hgrn_norm, g_sgu_norm, w_spatial, b_spatial)
    small_m = (m_b_ada, m_g_pre_mix, m_g_post_mix, m_g_pre_ffn, m_g_post_ffn, m_g_hgrn_norm, m_g_sgu_norm, m_w_spatial, m_b_spatial)
    small_v = (v_b_ada, v_g_pre_mix, v_g_post_mix, v_g_pre_ffn, v_g_post_ffn, v_g_hgrn_norm, v_g_sgu_norm, v_w_spatial, v_b_spatial)
    packed = _adamw_small(gathered[:, :n_common, :], pack(*small_w), pack(*small_m), pack(*small_v))

    def unpack(slab):
        outs, at = [], 0
        b3 = slab[:N_DEV * ada_rows].reshape(N_DEV, ada_rows, LANE)[:, :n_ada // LANE, :]
        outs.append(b3.reshape(b_ada.shape))
        at = N_DEV * ada_rows
        for ref in small_w[1:]:
            n_el = ref.size
            n_r = -(-(n_el // LANE) // 8) * 8
            outs.append(slab[at:at + n_el // LANE].reshape(ref.shape))
            at += n_r
        return outs

    small_out = [unpack(s) for s in packed]
    loss = packed[0][n_params, 0]

    dlb_all = gathered[:, n_common:n_common + 2 * heads, :].reshape(N_DEV, 2, heads, LANE)
    dlb_mine = lax.dynamic_index_in_dim(dlb_all, me, axis=2, keepdims=False)
    lb_out = _adamw_lb(dlb_mine, lb_logits, m_lb_logits, v_lb_logits)

    land_in, = received_grads("in", ada_out[0])
    big_update("w_in", [gw_in], land_in)

    order = ["w_ada", "b_ada", "g_pre_mix", "g_post_mix", "g_pre_ffn", "g_post_ffn", "w_in", "lb_logits", "g_hgrn_norm",
             "w_a_out", "g_sgu_norm", "w_spatial", "b_spatial", "w_b_out", "w_o", "w_ff1", "w_ff2"]
    small_names = ["b_ada", "g_pre_mix", "g_post_mix", "g_pre_ffn", "g_post_ffn", "g_hgrn_norm", "g_sgu_norm", "w_spatial", "b_spatial"]

    def leaf(kind, nm):
        if nm == "w_ada":
            return ada_out[kind]
        if nm == "lb_logits":
            return lb_out[kind]
        if nm in big_out:
            return big_out[nm][kind]
        return small_out[kind][small_names.index(nm)]

    result = [loss, grad_x[None]]
    for kind in range(4):
        result += [leaf(kind, nm) for nm in order]
    return tuple(result)
```

```python
import functools
import math

import jax
import jax.numpy as jnp
from jax import lax
from jax.experimental import pallas as pl
from jax.experimental.pallas import tpu as pltpu

F32 = jnp.float32
BF16 = jnp.bfloat16
MESH = pl.DeviceIdType.MESH
HIGHEST = lax.Precision.HIGHEST

N_DEV = 8
HEAD = 128
A_CHUNK = 32
N_MOD = 6
EPS = 1e-6
LANE = 128
VMEM_LIMIT = 60 * 1024 * 1024

ADAM_LR = 0.001
ADAM_B1 = 0.9
ADAM_B2 = 0.999
ADAM_EPS = 1e-08
ADAM_WD = 0.01
ADAM_STEP = 10

_NN = (((1,), (0,)), ((), ()))
_NT = (((1,), (1,)), ((), ()))
_TN = (((0,), (0,)), ((), ()))


def _dot(a, b, dims=_NN, precision=None):
    return lax.dot_general(a, b, dims, preferred_element_type=F32, precision=precision)


def _bdot(a, b, dims=_NN):
    return _dot(a.astype(BF16), b.astype(BF16), dims)


def _params(n_grid):
    return pltpu.CompilerParams(dimension_semantics=("arbitrary",) * n_grid, vmem_limit_bytes=VMEM_LIMIT)


def _dev_index():
    return lax.axis_index("x") * 4 + lax.axis_index("y") * 2 + lax.axis_index("c")


def _dev_coords(i):
    return (i // 4, (i // 2) % 2, i % 2)


def _sigmoid(x):
    return 1.0 / (1.0 + jnp.exp(-x))


def _erf(x):
    ax = jnp.abs(x)
    t = 1.0 / (1.0 + 0.3275911 * ax)
    poly = ((((1.061405429 * t - 1.453152027) * t + 1.421413741) * t - 0.284496736) * t + 0.254829592) * t
    y = 1.0 - poly * jnp.exp(-ax * ax)
    return jnp.where(x < 0, -y, y)


def _gelu_and_grad(x):
    cdf = 0.5 * (1.0 + _erf(x * (2.0 ** -0.5)))
    pdf = jnp.exp(-0.5 * x * x) * (1.0 / math.sqrt(2.0 * math.pi))
    return x * cdf, cdf + x * pdf


def _rms(x):
    return lax.rsqrt(jnp.mean(x * x, axis=-1, keepdims=True) + EPS)


def _colsum(x):
    return jnp.sum(x, axis=0, keepdims=True)


def _tile(n, want):
    if n <= want:
        return n
    t = (want // LANE) * LANE
    while n % t:
        t -= LANE
    assert t > 0, (n, want)
    return t


def _all_gather_small(name, payload):
    rows = payload.shape[0]

    def body(p_ref, out_ref, send_sems, recv_sems, local_sem):
        me = _dev_index()
        mine = pltpu.make_async_copy(p_ref, out_ref.at[me], local_sem)
        mine.start()
        sends = []
        for r in range(1, N_DEV):
            peer = (me + r) % N_DEV
            cp = pltpu.make_async_remote_copy(
                src_ref=p_ref, dst_ref=out_ref.at[me], send_sem=send_sems.at[r - 1], recv_sem=recv_sems.at[r - 1],
                device_id=_dev_coords(peer), device_id_type=MESH)
            cp.start()
            sends.append(cp)
        for r in range(1, N_DEV):
            src = (me + N_DEV - r) % N_DEV
            pltpu.make_async_remote_copy(
                src_ref=p_ref, dst_ref=out_ref.at[src], send_sem=send_sems.at[r - 1], recv_sem=recv_sems.at[r - 1],
                device_id=_dev_coords(src), device_id_type=MESH).wait_recv()
        for cp in sends:
            cp.wait_send()
        mine.wait()

    return pl.pallas_call(
        body, name=name,
        out_shape=jax.ShapeDtypeStruct((N_DEV, rows, LANE), F32),
        in_specs=[pl.BlockSpec(memory_space=pltpu.VMEM)],
        out_specs=pl.BlockSpec(memory_space=pltpu.VMEM),
        scratch_shapes=[pltpu.SemaphoreType.DMA((N_DEV - 1,)), pltpu.SemaphoreType.DMA((N_DEV - 1,)),
                        pltpu.SemaphoreType.DMA],
        compiler_params=pltpu.CompilerParams(vmem_limit_bytes=VMEM_LIMIT),
    )(payload)


def _region(ref, dev, axis, n):
    start = pl.multiple_of(dev * n, LANE if axis == 1 else 16)
    return ref.at[:, pl.ds(start, n)] if axis == 1 else ref.at[pl.ds(start, n), :]


class _Exchange:
    def __init__(self, arrays, out_shapes, sems, start, finish):
        self.arrays, self.out_shapes, self.sems, self.start, self.finish = arrays, out_shapes, sems, start, finish


def _gather_plan(shards, axes):
    n_w = len(shards)
    fulls = []
    for s, ax in zip(shards, axes):
        shp = (s.shape[0], s.shape[1] * N_DEV) if ax == 1 else (s.shape[0] * N_DEV, s.shape[1])
        fulls.append(jax.ShapeDtypeStruct(shp, BF16))
    widths = [s.shape[ax] for s, ax in zip(shards, axes)]

    def places():
        x, y, c = lax.axis_index("x"), lax.axis_index("y"), lax.axis_index("c")
        chips = [(1 - x, y), (x, 1 - y), (1 - x, 1 - y)]
        return (x, y, c), (x, y, 1 - c), chips

    def index(p):
        return p[0] * 4 + p[1] * 2 + p[2]

    def copy(w, k, s_refs, f_refs, sems, block, to, from_shard):
        send_sems, recv_sems, _ = sems
        dst = _region(f_refs[w], index(block), axes[w], widths[w])
        return pltpu.make_async_remote_copy(
            src_ref=s_refs[w] if from_shard else dst, dst_ref=dst,
            send_sem=send_sems.at[w, k], recv_sem=recv_sems.at[w, k], device_id=to, device_id_type=MESH)

    def local(w, s_refs, f_refs, sems, me):
        return pltpu.make_async_copy(s_refs[w], _region(f_refs[w], index(me), axes[w], widths[w]), sems[2].at[w])

    def start(s_refs, f_refs, sems):
        me, sib, chips = places()
        for w in range(n_w):
            local(w, s_refs, f_refs, sems, me).start()
            copy(w, 0, s_refs, f_refs, sems, me, sib, True).start()
            for j, chip in enumerate(chips):
                copy(w, 1 + j, s_refs, f_refs, sems, me, (*chip, me[2]), True).start()

    def finish(s_refs, f_refs, sems):
        me, sib, chips = places()
        for w in range(n_w):
            for j, chip in enumerate(chips):
                copy(w, 1 + j, s_refs, f_refs, sems, (*chip, me[2]), me, True).wait_recv()
                copy(w, 4 + j, s_refs, f_refs, sems, (*chip, me[2]), sib, False).start()
        for w in range(n_w):
            copy(w, 0, s_refs, f_refs, sems, sib, me, True).wait_recv()
            for j, chip in enumerate(chips):
                copy(w, 4 + j, s_refs, f_refs, sems, (*chip, sib[2]), me, False).wait_recv()
        for w in range(n_w):
            for k in range(N_DEV - 1):
                copy(w, k, s_refs, f_refs, sems, me, sib, True).wait_send()
            local(w, s_refs, f_refs, sems, me).wait()

    sems = [pltpu.SemaphoreType.DMA((n_w, N_DEV - 1)), pltpu.SemaphoreType.DMA((n_w, N_DEV - 1)),
            pltpu.SemaphoreType.DMA((n_w,))]
    return _Exchange(list(shards), fulls, sems, start, finish)


def _scatter_plan(grads, axes):
    n_w = len(grads)
    lands = []
    for g, ax in zip(grads, axes):
        shp = (g.shape[0], g.shape[1] // N_DEV) if ax == 1 else (g.shape[0] // N_DEV, g.shape[1])
        lands.append(jax.ShapeDtypeStruct((N_DEV - 1,) + shp, BF16))
    widths = [ld.shape[1 + ax] for ld, ax in zip(lands, axes)]

    def copy(w, r, g_refs, l_refs, sems, block, to):
        return pltpu.make_async_remote_copy(
            src_ref=_region(g_refs[w], block, axes[w], widths[w]), dst_ref=l_refs[w].at[r - 1],
            send_sem=sems[0].at[w * (N_DEV - 1) + r - 1], recv_sem=sems[1].at[w * (N_DEV - 1) + r - 1],
            device_id=_dev_coords(to), device_id_type=MESH)

    def start(g_refs, l_refs, sems):
        me = _dev_index()
        for w in range(n_w):
            for r in range(1, N_DEV):
                owner = (me + r) % N_DEV
                copy(w, r, g_refs, l_refs, sems, owner, owner).start()

    def finish(g_refs, l_refs, sems):
        me = _dev_index()
        for w in range(n_w):
            for r in range(1, N_DEV):
                copy(w, r, g_refs, l_refs, sems, me, (me + N_DEV - r) % N_DEV).wait_recv()
        for w in range(n_w):
            for r in range(1, N_DEV):
                copy(w, r, g_refs, l_refs, sems, me, (me + r) % N_DEV).wait_send()

    sems = [pltpu.SemaphoreType.DMA((n_w * (N_DEV - 1),)), pltpu.SemaphoreType.DMA((n_w * (N_DEV - 1),))]
    return _Exchange(list(grads), lands, sems, start, finish)


def _places():
    x, y, c = lax.axis_index("x"), lax.axis_index("y"), lax.axis_index("c")
    return (x, y, c), (x, y, 1 - c), [(1 - x, y), (x, 1 - y), (1 - x, 1 - y)]


def _place_index(p):
    return p[0] * 4 + p[1] * 2 + p[2]


def _gather_stage_plans(fulls, axes):
    n_w = len(fulls)
    widths = [f.shape[ax] // N_DEV for f, ax in zip(fulls, axes)]
    shapes = [jax.ShapeDtypeStruct(f.shape, f.dtype) for f in fulls]

    def copy(per, w, k, f_refs, sems, block, to):
        part = _region(f_refs[w], _place_index(block), axes[w], widths[w])
        return pltpu.make_async_remote_copy(
            src_ref=part, dst_ref=part, send_sem=sems[0].at[w * per + k], recv_sem=sems[1].at[w * per + k],
            device_id=to, device_id_type=MESH)

    def start1(_, f_refs, sems):
        me, sib, chips = _places()
        for w in range(n_w):
            copy(4, w, 0, f_refs, sems, me, sib).start()
            for j, chip in enumerate(chips):
                copy(4, w, 1 + j, f_refs, sems, me, (*chip, me[2])).start()

    def finish1(_, f_refs, sems):
        me, sib, chips = _places()
        for w in range(n_w):
            copy(4, w, 0, f_refs, sems, sib, me).wait_recv()
            for j, chip in enumerate(chips):
                copy(4, w, 1 + j, f_refs, sems, (*chip, me[2]), me).wait_recv()
        for w in range(n_w):
            for k in range(4):
                copy(4, w, k, f_refs, sems, me, sib).wait_send()

    def start2(_, f_refs, sems):
        me, sib, chips = _places()
        for w in range(n_w):
            for j, chip in enumerate(chips):
                copy(3, w, j, f_refs, sems, (*chip, me[2]), sib).start()

    def finish2(_, f_refs, sems):
        me, sib, chips = _places()
        for w in range(n_w):
            for j, chip in enumerate(chips):
                copy(3, w, j, f_refs, sems, (*chip, sib[2]), me).wait_recv()
        for w in range(n_w):
            for j, chip in enumerate(chips):
                copy(3, w, j, f_refs, sems, (*chip, me[2]), sib).wait_send()

    sems1 = [pltpu.SemaphoreType.DMA((n_w * 4,)), pltpu.SemaphoreType.DMA((n_w * 4,))]
    sems2 = [pltpu.SemaphoreType.DMA((n_w * 3,)), pltpu.SemaphoreType.DMA((n_w * 3,))]
    return _Exchange([], shapes, sems1, start1, finish1), _Exchange([], shapes, sems2, start2, finish2)


def _run_exchange(name, plan):
    n_in, n_out = len(plan.arrays), len(plan.out_shapes)

    def body(*refs):
        ins, outs, sems = refs[:n_in], refs[n_in:n_in + n_out], refs[n_in + n_out:]
        plan.start(ins, outs, sems)
        plan.finish(ins, outs, sems)

    any_spec = pl.BlockSpec(memory_space=pl.ANY)
    return pl.pallas_call(
        body, name=name, out_shape=plan.out_shapes,
        in_specs=[any_spec] * n_in, out_specs=[any_spec] * n_out, scratch_shapes=plan.sems,
    )(*plan.arrays)


_NO_EXCHANGE = _Exchange([], [], [], lambda i, o, s: None, lambda i, o, s: None)


def _direct_gather_plan(fulls, axes):
    n_w = len(fulls)
    widths = [f.shape[ax] // N_DEV for f, ax in zip(fulls, axes)]
    fulls = [jax.ShapeDtypeStruct(f.shape, f.dtype) for f in fulls]

    def copy(w, r, s_refs, f_refs, sems, block, to):
        part = _region(f_refs[w], block, axes[w], widths[w])
        return pltpu.make_async_remote_copy(
            src_ref=part, dst_ref=part,
            send_sem=sems[0].at[w * (N_DEV - 1) + r - 1], recv_sem=sems[1].at[w * (N_DEV - 1) + r - 1],
            device_id=_dev_coords(to), device_id_type=MESH)

    def start(s_refs, f_refs, sems):
        me = _dev_index()
        for w in range(n_w):
            for r in range(1, N_DEV):
                copy(w, r, s_refs, f_refs, sems, me, (me + r) % N_DEV).start()

    def finish(s_refs, f_refs, sems):
        me = _dev_index()
        for w in range(n_w):
            for r in range(1, N_DEV):
                src = (me + N_DEV - r) % N_DEV
                copy(w, r, s_refs, f_refs, sems, src, src).wait_recv()
        for w in range(n_w):
            for r in range(1, N_DEV):
                copy(w, r, s_refs, f_refs, sems, me, (me + r) % N_DEV).wait_send()

    sems = [pltpu.SemaphoreType.DMA((n_w * (N_DEV - 1),)), pltpu.SemaphoreType.DMA((n_w * (N_DEV - 1),))]
    return _Exchange([], fulls, sems, start, finish)


_HBM = pl.BlockSpec(memory_space=pltpu.HBM)
_SEM = pl.BlockSpec(memory_space=pltpu.SEMAPHORE)
_EFFECT = pltpu.SideEffectType.DATAFLOW_SIDE_EFFECTING


def _split_start(name, plan, landing=None):
    n_in, n_out, n_sem = len(plan.arrays), len(plan.out_shapes), len(plan.sems)

    def body(*refs):
        ins, lands = refs[:n_in], refs[n_in:n_in + n_out]
        sems = refs[n_in + n_out:n_in + n_out + n_sem]
        token = refs[-1]
        plan.start(ins, lands, sems)
        token[...] = jnp.zeros_like(token)

    hbm = lambda a: pltpu.HBM(a.shape, a.dtype)
    results = pl.pallas_call(
        body, name=name,
        out_shape=tuple(plan.sems) + tuple(hbm(a) for a in plan.arrays) + tuple(hbm(a) for a in plan.out_shapes)
        + (jax.ShapeDtypeStruct((8, LANE), F32),),
        in_specs=(_HBM,) * (n_in + n_out),
        out_specs=(_SEM,) * n_sem + (_HBM,) * (n_in + n_out) + (pl.BlockSpec(memory_space=pltpu.VMEM),),
        input_output_aliases={i: n_sem + i for i in range(n_in + n_out)},
        compiler_params=pltpu.CompilerParams(has_side_effects=_EFFECT),
    )(*[pltpu.with_memory_space_constraint(a, pltpu.HBM) for a in plan.arrays],
      *[pltpu.with_memory_space_constraint(a, pltpu.HBM)
        for a in (landing if landing is not None else [lax.empty(a.shape, a.dtype) for a in plan.out_shapes])])
    return results[:n_sem], results[n_sem:n_sem + n_in + n_out], results[-1]


def _split_wait(name, plan, sems, thru, after):
    n_in, n_out, n_sem = len(plan.arrays), len(plan.out_shapes), len(plan.sems)

    def body(*refs):
        ins, lands = refs[:n_in], refs[n_in:n_in + n_out]
        sem_refs = refs[n_in + n_out:n_in + n_out + n_sem]
        plan.finish(ins, lands, sem_refs)

    hbm = lambda a: pltpu.HBM(a.shape, a.dtype)
    results = pl.pallas_call(
        body, name=name,
        out_shape=tuple(hbm(a) for a in plan.arrays) + tuple(hbm(a) for a in plan.out_shapes),
        in_specs=(_HBM,) * (n_in + n_out) + (_SEM,) * n_sem + (pl.BlockSpec(memory_space=pl.ANY),),
        out_specs=(_HBM,) * (n_in + n_out),
        input_output_aliases={i: i for i in range(n_in + n_out)},
        compiler_params=pltpu.CompilerParams(has_side_effects=_EFFECT),
    )(*thru, *sems, after)
    return results[:n_in], results[n_in:]


def _cast_into_full(name, me, w, axis):
    r, c = w.shape
    tr = _tile(r, 256)
    if axis == 1:
        shape, place = (r, c * N_DEV), pl.BlockSpec((tr, c), lambda i, me_ref: (i, me_ref[0]))
    else:
        shape, place = (r * N_DEV, c), pl.BlockSpec((tr, c), lambda i, me_ref: (me_ref[0] * (r // tr) + i, 0))

    def body(me_ref, w_ref, o_ref):
        o_ref[...] = w_ref[...].astype(BF16)

    grid_spec = pltpu.PrefetchScalarGridSpec(
        num_scalar_prefetch=1, grid=(r // tr,),
        in_specs=[pl.BlockSpec((tr, c), lambda i, me_ref: (i, 0))], out_specs=place)
    return pl.pallas_call(body, name=name, grid_spec=grid_spec, out_shape=jax.ShapeDtypeStruct(shape, BF16),
                          compiler_params=_params(1))(me, w)


def _mm(name, a, b, dims, m, n, k, tm, tn, tk, extras, outs, epilogue, row_chunk=None, exchange=None,
        b_col_block=0):
    ni, nj, nk = m // tm, n // tn, k // tk
    ne, no = len(extras), len(outs)
    xin = len(exchange.arrays) if exchange else 0
    xout = len(exchange.out_shapes) if exchange else 0
    if dims == _TN:
        a_spec = pl.BlockSpec((tk, tm), lambda i, j, kk: (kk, i))
    else:
        a_spec = pl.BlockSpec((tm, tk), lambda i, j, kk: (i, kk))
    if dims == _NT:
        b_spec = pl.BlockSpec((tn, tk), lambda i, j, kk: (j, kk))
    else:
        b_spec = pl.BlockSpec((tk, tn), lambda i, j, kk: (kk, j + b_col_block))
    chunks = [slice(None)] if row_chunk is None else [slice(r, r + row_chunk) for r in range(0, tm, row_chunk)]

    def lift(index_map):
        return lambda i, j, kk: index_map(i, j)

    def body(a_ref, b_ref, *rest):
        extra_refs, rest = rest[:ne], rest[ne:]
        xin_refs, rest = rest[:xin], rest[xin:]
        out_refs, rest = rest[:no], rest[no:]
        xout_refs, rest = rest[:xout], rest[xout:]
        i, j, kk = pl.program_id(0), pl.program_id(1), pl.program_id(2)
        if exchange:
            sem_refs = rest[1:] if nk > 1 else rest

            @pl.when((i == 0) & (j == 0) & (kk == 0))
            def _():
                exchange.start(xin_refs, xout_refs, sem_refs)

        if nk == 1:
            part = _dot(a_ref[...], b_ref[...], dims)
            for rows in chunks:
                epilogue(part[rows], i, j, extra_refs, out_refs, rows)
        else:
            acc_ref = rest[0]

            @pl.when(kk == 0)
            def _():
                acc_ref[...] = _dot(a_ref[...], b_ref[...], dims)

            @pl.when(kk > 0)
            def _():
                acc_ref[...] += _dot(a_ref[...], b_ref[...], dims)

            @pl.when(kk == nk - 1)
            def _():
                for rows in chunks:
                    epilogue(acc_ref[rows, :], i, j, extra_refs, out_refs, rows)

        if exchange:
            @pl.when((i == ni - 1) & (j == nj - 1) & (kk == nk - 1))
            def _():
                exchange.finish(xin_refs, xout_refs, sem_refs)

    any_spec = pl.BlockSpec(memory_space=pl.ANY)
    once = dict(pipeline_mode=pl.Buffered(1)) if (row_chunk is not None and nk > 1) else {}
    results = pl.pallas_call(
        body, name=name,
        grid=(ni, nj, nk),
        in_specs=[a_spec, b_spec] + [pl.BlockSpec(bs, lift(im), **once) for _, bs, im in extras] + [any_spec] * xin,
        out_specs=[pl.BlockSpec(bs, lift(im), **once) for _, bs, im in outs] + [any_spec] * xout,
        out_shape=[sd for sd, _, _ in outs] + (list(exchange.out_shapes) if exchange else []),
        scratch_shapes=([pltpu.VMEM((tm, tn), F32)] if nk > 1 else []) + (list(exchange.sems) if exchange else []),
        compiler_params=_params(3),
    )(a, b, *[arr for arr, _, _ in extras], *(exchange.arrays if exchange else []))
    return (results[:no], results[no:]) if exchange else results


def _after(token):
    return [(token, (8, LANE), lambda i, j: (0, 0))]


def _grad_w(name, a, dc, token=None, tm=512, tn=1024, cols=None):
    t, m = a.shape
    first, n = cols if cols is not None else (0, dc.shape[1])
    tm, tn = _tile(m, tm), _tile(n, tn)
    assert first % tn == 0

    def epilogue(acc, i, j, extra_refs, out_refs, rows):
        out_refs[0][...] = acc
        out_refs[1][...] = acc.astype(BF16)

    blk = ((tm, tn), lambda i, j: (i, j))
    return _mm(name, a, dc, _TN, m, n, t, tm, tn, t, _after(token) if token is not None else [],
               [(jax.ShapeDtypeStruct((m, n), F32),) + blk, (jax.ShapeDtypeStruct((m, n), BF16),) + blk], epilogue,
               b_col_block=first // tn)


def _proj_gather(a1, w_shard, order):
    t, d = a1.shape
    nsh = w_shard.shape[1]
    tm = _tile(t, 512)
    n_i = t // tm

    def body(ord_ref, a_ref, wsh_ref, proj_ref, full_ref, bbuf, bsem, send_sems, recv_sems, own_sem):
        s, i = pl.program_id(0), pl.program_id(1)
        me, sib, chips = _places()
        near, far = chips[:2], chips[2]
        steps = ([(me, None, None), (sib, 0, None)]
                 + [((*ch, me[2]), 1 + j, 4 + j) for j, ch in enumerate(near)]
                 + [((*ch, sib[2]), 4 + j, None) for j, ch in enumerate(near)]
                 + [((*far, me[2]), 3, 6), ((*far, sib[2]), 6, None)])
        blocks = [st[0] for st in steps]

        def part(block):
            return _region(full_ref, _place_index(block), 1, nsh)

        def remote(k, block, to, from_shard=False):
            return pltpu.make_async_remote_copy(
                src_ref=wsh_ref if from_shard else part(block), dst_ref=part(block),
                send_sem=send_sems.at[k], recv_sem=recv_sems.at[k], device_id=to, device_id_type=MESH)

        def load(pos):
            src = wsh_ref if pos == 0 else part(blocks[pos])
            return pltpu.make_async_copy(src, bbuf.at[pos % 2], bsem.at[pos % 2])

        own = pltpu.make_async_copy(wsh_ref, part(me), own_sem)

        @pl.when((s == 0) & (i == 0))
        def _():
            own.start()
            remote(0, me, sib, True).start()
            for j, ch in enumerate(chips):
                remote(1 + j, me, (*ch, me[2]), True).start()
            load(0).start()
            load(0).wait()

        for pos in range(1, N_DEV):
            @pl.when((s == pos) & (i == 0))
            def _():
                load(pos).wait()

        for pos in range(N_DEV - 1):
            @pl.when((s == pos) & (i == n_i - 1))
            def _():
                nxt = pos + 1
                block, arrives_on, pass_on_with = steps[nxt]
                remote(arrives_on, block, me).wait_recv()
                if pass_on_with is not None:
                    remote(pass_on_with, block, sib).start()
                load(nxt).start()

        proj_ref[...] = _dot(a_ref[...], bbuf[s % 2])

        @pl.when((s == N_DEV - 1) & (i == n_i - 1))
        def _():
            for k in range(N_DEV - 1):
                remote(k, me, sib, True).wait_send()
            own.wait()

    grid_spec = pltpu.PrefetchScalarGridSpec(
        num_scalar_prefetch=1, grid=(N_DEV, n_i),
        in_specs=[pl.BlockSpec((tm, d), lambda s, i, ord_ref: (i, 0)), pl.BlockSpec(memory_space=pl.ANY)],
        out_specs=[pl.BlockSpec((tm, nsh), lambda s, i, ord_ref: (i, ord_ref[s])), pl.BlockSpec(memory_space=pl.ANY)],
        scratch_shapes=[pltpu.VMEM((2, d, nsh), BF16), pltpu.SemaphoreType.DMA((2,)),
                        pltpu.SemaphoreType.DMA((N_DEV - 1,)), pltpu.SemaphoreType.DMA((N_DEV - 1,)),
                        pltpu.SemaphoreType.DMA])
    return pl.pallas_call(
        body, name="proj_gather", grid_spec=grid_spec,
        out_shape=[jax.ShapeDtypeStruct((t, nsh * N_DEV), F32), jax.ShapeDtypeStruct((d, nsh * N_DEV), BF16)],
        compiler_params=_params(2),
    )(order, a1, w_shard)


def _cast_bf16(name, w):
    r, c = w.shape
    tr = _tile(r, 256)
    return pl.pallas_call(
        lambda w_ref, o_ref: o_ref.__setitem__(Ellipsis, w_ref[...].astype(BF16)), name=name,
        grid=(r // tr,), in_specs=[pl.BlockSpec((tr, c), lambda i: (i, 0))],
        out_specs=pl.BlockSpec((tr, c), lambda i: (i, 0)), out_shape=jax.ShapeDtypeStruct((r, c), BF16),
        compiler_params=_params(1),
    )(w)


def _prep_small(c_row, lb_logits):
    d = c_row.shape[1]
    rows = d // LANE

    def body(c_ref, l_ref, o_ref):
        cv = c_ref[...]
        o_ref[0:rows, :] = cv * _sigmoid(cv)
        lbs = [_sigmoid(l_ref[dr][0:1, :] - l_ref[dr][1:2, :]) for dr in range(2)]
        o_ref[rows:rows + 8, :] = jnp.concatenate(lbs + [jnp.zeros((6, LANE), F32)], axis=0)

    return pl.pallas_call(
        body, name="prep_small", out_shape=jax.ShapeDtypeStruct((rows + 8, LANE), F32),
    )(c_row.reshape(rows, LANE), lb_logits)


def _mod_shard(sc_all, w_ada_shard, b_shard):
    d, n = w_ada_shard.shape
    tn = _tile(n, 512)

    def body(s_ref, w_ref, b_ref, o_ref):
        o_ref[...] = _dot(s_ref[...], w_ref[...], precision=HIGHEST) + b_ref[...]

    return pl.pallas_call(
        body, name="mod_shard", grid=(n // tn,),
        in_specs=[pl.BlockSpec((N_DEV, d), lambda j: (0, 0)), pl.BlockSpec((d, tn), lambda j: (0, j)),
                  pl.BlockSpec((1, tn), lambda j: (0, j))],
        out_specs=pl.BlockSpec((N_DEV, tn), lambda j: (0, j)),
        out_shape=jax.ShapeDtypeStruct((N_DEV, n), F32), compiler_params=_params(1),
    )(sc_all, w_ada_shard, b_shard)


def _norm_mod(x, gain, shift, scale):
    t, d = x.shape
    tm = _tile(t, 512)

    def body(x_ref, g_ref, sh_ref, sc_ref, o_ref):
        xv = x_ref[...]
        o_ref[...] = ((xv * _rms(xv) * g_ref[...]) * (1.0 + sc_ref[...]) + sh_ref[...]).astype(BF16)

    vec = pl.BlockSpec((1, d), lambda i: (0, 0))
    return pl.pallas_call(
        body, name="norm_mod", grid=(t // tm,),
        in_specs=[pl.BlockSpec((tm, d), lambda i: (i, 0)), vec, vec, vec],
        out_specs=pl.BlockSpec((tm, d), lambda i: (i, 0)), out_shape=jax.ShapeDtypeStruct((t, d), BF16),
        compiler_params=_params(1),
    )(x, gain, shift, scale)


def _chunk_masks():
    row = lax.broadcasted_iota(jnp.int32, (HEAD, HEAD), 0)
    col = lax.broadcasted_iota(jnp.int32, (HEAD, HEAD), 1)
    same = (row // A_CHUNK) == (col // A_CHUNK)
    return same & (col <= row), same & (col >= row)


def _ones(mask):
    return jnp.where(mask, 1.0, 0.0).astype(BF16)


def _dot_split(ones_bf16, x):
    hi = x.astype(BF16)
    lo = (x - hi.astype(F32)).astype(BF16)
    return _dot(ones_bf16, hi) + _dot(ones_bf16, lo)


def _hgrn_block(direction, f, lb, cum2):
    sf = _sigmoid(f)
    big_f = lb + (1.0 - lb) * sf
    k = (1.0 - lb) * (1.0 - sf)
    lf = jnp.log(big_f)
    both = _dot_split(cum2, lf)
    cf, cr = both[:HEAD], both[HEAD:]
    b, rest = (cf, cr - lf) if direction == 0 else (cr, cf - lf)
    return k, sf, big_f, jnp.exp(b), jnp.exp(-b), jnp.exp(rest)


def _hgrn_fwd(proj, lb, g_norm, width, exchange, after):
    t = proj.shape[0]
    heads = width // HEAD
    nb, nc = t // HEAD, t // A_CHUNK
    ua = 4 if nb % 4 == 0 else (2 if nb % 2 == 0 else 1)
    ub = 16 if nc % 16 == 0 else (8 if nc % 8 == 0 else 4)
    q_scale = HEAD ** -0.5
    xin, xout = len(exchange.arrays), len(exchange.out_shapes)

    def body(q_ref, ffw_ref, fbw_ref, v_ref, og_ref, lb_ref, g_ref, *rest):
        xin_refs, rest = rest[:xin], rest[xin + len(after):]
        outa_ref, osum_ref = rest[:2]
        xout_refs, rest = rest[2:2 + xout], rest[2 + xout:]
        qd_s, ke_s, dc_s, o_s = rest[:4]
        sem_refs = rest[4:]
        h = pl.program_id(0)

        @pl.when(h == 0)
        def _():
            exchange.start(xin_refs, xout_refs, sem_refs)

        tril, triu = _chunk_masks()
        cum2 = jnp.concatenate([_ones(tril), _ones(triu)], axis=0)
        f_refs = (ffw_ref, fbw_ref)
        lbs = (lb_ref[0:1, :], lb_ref[1:2, :])

        def phase_a(it, carry):
            loaded = []
            for u in range(ua):
                rows = pl.ds(pl.multiple_of((it * ua + u) * HEAD, HEAD), HEAD)
                loaded.append((rows, q_ref[rows, :], v_ref[rows, :], ffw_ref[rows, :], fbw_ref[rows, :]))
            chains = [(d, rows, qv * q_scale, vv.astype(BF16), fv)
                      for rows, qv, vv, f0, f1 in loaded for d, fv in ((0, f0), (1, f1))]
            blocks = [_hgrn_block(d, fv, lbs[d], cum2) for d, _, _, _, fv in chains]
            scaled = [(qv * eb, k * enb, k * erest, eb * erest)
                      for (_, _, qv, _, _), (k, _, _, eb, enb, erest) in zip(chains, blocks)]
            atts = [jnp.where(tril if d == 0 else triu, _bdot(qd, kd, _NT), 0.0)
                    for (d, _, _, _, _), (qd, kd, _, _) in zip(chains, scaled)]
            intras = [_bdot(att, vv) for att, (_, _, _, vv, _) in zip(atts, chains)]
            results = [(d, rows, o_intra, qd.astype(BF16), ke.astype(BF16), decay)
                       for (d, rows, _, _, _), (qd, _, ke, decay), o_intra in zip(chains, scaled, intras)]
            for d, rows, o_intra, qd16, ke16, decay in results:
                o_s[d, rows, :] = o_intra
                qd_s[d, rows, :] = qd16
                ke_s[d, rows, :] = ke16
                dc_s[d, rows, :] = decay
            return carry

        lax.fori_loop(0, nb // ua, phase_a, 0)

        def phase_b(it, states):
            loaded = []
            for u in range(ub):
                n = it * ub + u
                for d in range(2):
                    c = n if d == 0 else nc - 1 - n
                    start = pl.multiple_of(c * A_CHUNK, A_CHUNK)
                    rows = pl.ds(start, A_CHUNK)
                    loaded.append((d, rows, qd_s[d, rows, :], ke_s[d, rows, :], v_ref[rows, :],
                                   dc_s[d, pl.ds(start, 1), :], o_s[d, rows, :]))
            increments = [_dot(vv.astype(BF16), ke16, _TN) for _, _, _, ke16, vv, _, _ in loaded]
            states = list(states)
            befores = []
            for (d, _, _, _, _, decay, _), inc in zip(loaded, increments):
                befores.append(states[d].astype(BF16))
                states[d] = states[d] * decay + inc
            inters = [_dot(qd16, before, _NT) for (_, _, qd16, _, _, _, _), before in zip(loaded, befores)]
            for (d, rows, _, _, _, _, o_intra), o_inter in zip(loaded, inters):
                o_s[d, rows, :] = o_intra + o_inter
            return tuple(states)

        zero_state = jnp.zeros((HEAD, HEAD), F32)
        lax.fori_loop(0, nc // ub, phase_b, (zero_state, zero_state))

        def phase_c(i, carry):
            rows = pl.ds(pl.multiple_of(i * HEAD, HEAD), HEAD)
            o = o_s[0, rows, :] + o_s[1, rows, :]
            osum_ref[rows, :] = o
            og = og_ref[rows, :]
            outa_ref[rows, :] = (o * _rms(o) * g_ref[...] * (og * _sigmoid(og))).astype(BF16)
            return carry

        lax.fori_loop(0, nb, phase_c, 0)

        @pl.when(h == heads - 1)
        def _():
            exchange.finish(xin_refs, xout_refs, sem_refs)

    def col(p):
        return pl.BlockSpec((t, HEAD), lambda h: (0, p * heads + h))

    any_spec = pl.BlockSpec(memory_space=pl.ANY)
    results = pl.pallas_call(
        body, name="hgrn_fwd", grid=(heads,),
        in_specs=[col(0), col(1), col(2), col(3), col(4),
                  pl.BlockSpec((2, HEAD), lambda h: (0, h)), pl.BlockSpec((1, HEAD), lambda h: (0, 0))]
        + [any_spec] * (xin + len(after)),
        out_specs=[pl.BlockSpec((t, HEAD), lambda h: (0, h)), pl.BlockSpec((t, HEAD), lambda h: (0, h))] + [any_spec] * xout,
        out_shape=[jax.ShapeDtypeStruct((t, width), BF16), jax.ShapeDtypeStruct((t, width), F32)] + list(exchange.out_shapes),
        scratch_shapes=[pltpu.VMEM((2, t, HEAD), BF16), pltpu.VMEM((2, t, HEAD), BF16), pltpu.VMEM((2, t, HEAD), F32),
                        pltpu.VMEM((2, t, HEAD), F32)] + list(exchange.sems),
        compiler_params=_params(1),
    )(proj, proj, proj, proj, proj, lb, g_norm, *exchange.arrays, *after)
    return results[0], results[1], results[2:]


def _sgu_core(u_pre, v_pre, g_v, ws_ref, bst):
    u, du = _gelu_and_grad(u_pre)
    v, dv = _gelu_and_grad(v_pre)
    mu = jnp.mean(v, axis=-1, keepdims=True)
    dlt = v - mu
    rstd = lax.rsqrt(jnp.mean(dlt * dlt, axis=-1, keepdims=True) + EPS)
    vhat = dlt * rstd
    vn = vhat * g_v
    groups = vn.shape[1] // HEAD
    cols = []
    for g in range(groups):
        vm_g = _bdot(ws_ref[g], vn[:, g * HEAD:(g + 1) * HEAD]) + bst[:, g:g + 1]
        cols.append(vm_g)
    return u, du, dv, vhat, rstd, vn, jnp.concatenate(cols, axis=1)


def _sgu_fwd(proj, g_v, w_s, bst, width, z_block):
    t = proj.shape[0]

    def body(u_ref, v_ref, g_ref, ws_ref, bst_ref, o_ref):
        u, _, _, _, _, _, vm = _sgu_core(u_ref[...], v_ref[...], g_ref[...], ws_ref, bst_ref[...])
        o_ref[...] = (u * vm).astype(BF16)

    groups = width // HEAD
    return pl.pallas_call(
        body, name="sgu_fwd", grid=(t // HEAD,),
        in_specs=[pl.BlockSpec((HEAD, width), lambda i: (i, z_block)), pl.BlockSpec((HEAD, width), lambda i: (i, z_block + 1)),
                  pl.BlockSpec((1, width), lambda i: (0, 0)), pl.BlockSpec((groups, HEAD, HEAD), lambda i: (0, 0, 0)),
                  pl.BlockSpec((HEAD, groups), lambda i: (0, 0))],
        out_specs=pl.BlockSpec((HEAD, width), lambda i: (i, 0)),
        out_shape=jax.ShapeDtypeStruct((t, width), BF16), compiler_params=_params(1),
    )(proj, proj, g_v, w_s, bst)


def _sgu_bwd(proj, dout_b, dproj, g_v, w_s, w_st, bst, width, z_block):
    t = proj.shape[0]
    groups = width // HEAD
    nblk = t // HEAD

    def body(u_ref, v_ref, do_ref, g_ref, ws_ref, wst_ref, bst_ref, dproj_hbm,
             dz_ref, dg_ref, dws_ref, dbst_ref, res_s):
        i, p = pl.program_id(0), pl.program_id(1)

        @pl.when((i == 0) & (p == 0))
        def _():
            dg_ref[...] = jnp.zeros_like(dg_ref)
            dws_ref[...] = jnp.zeros_like(dws_ref)
            dbst_ref[...] = jnp.zeros_like(dbst_ref)

        @pl.when(p == 0)
        def _():
            g_v = g_ref[...]
            u, du, dv, vhat, rstd, vn, vm = _sgu_core(u_ref[...], v_ref[...], g_v, ws_ref, bst_ref[...])
            dout = do_ref[...].astype(F32)
            res_s[0] = (dout * vm * du).astype(BF16)
            dvm = dout * u
            dvn_cols = []
            for g in range(groups):
                sl = slice(g * HEAD, (g + 1) * HEAD)
                dvm_g = dvm[:, sl]
                dbst_ref[:, g:g + 1] += jnp.sum(dvm_g, axis=1, keepdims=True)
                dws_ref[g] += _bdot(dvm_g, vn[:, sl], _NT)
                dvn_cols.append(_bdot(wst_ref[g], dvm_g))
            dvn = jnp.concatenate(dvn_cols, axis=1)
            dg_ref[...] += _colsum(dvn * vhat)
            dvh = dvn * g_v
            dvg = rstd * (dvh - jnp.mean(dvh, axis=-1, keepdims=True)
                          - vhat * jnp.mean(dvh * vhat, axis=-1, keepdims=True))
            res_s[1] = (dvg * dv).astype(BF16)

        dz_ref[...] = res_s[p]

    n_in = dproj.shape[1]
    return pl.pallas_call(
        body, name="sgu_bwd", grid=(nblk, 2),
        in_specs=[pl.BlockSpec((HEAD, width), lambda i, p: (i, z_block)),
                  pl.BlockSpec((HEAD, width), lambda i, p: (i, z_block + 1)),
                  pl.BlockSpec((HEAD, width), lambda i, p: (i, 0)),
                  pl.BlockSpec((1, width), lambda i, p: (0, 0)),
                  pl.BlockSpec((groups, HEAD, HEAD), lambda i, p: (0, 0, 0)),
                  pl.BlockSpec((groups, HEAD, HEAD), lambda i, p: (0, 0, 0)),
                  pl.BlockSpec((HEAD, groups), lambda i, p: (0, 0)),
                  pl.BlockSpec(memory_space=pl.ANY)],
        out_specs=[pl.BlockSpec((HEAD, width), lambda i, p: (i, z_block + p)),
                   pl.BlockSpec((1, width), lambda i, p: (0, 0)),
                   pl.BlockSpec((groups, HEAD, HEAD), lambda i, p: (0, 0, 0)),
                   pl.BlockSpec((HEAD, groups), lambda i, p: (0, 0))],
        out_shape=[jax.ShapeDtypeStruct((t, n_in), BF16), jax.ShapeDtypeStruct((1, width), F32),
                   jax.ShapeDtypeStruct((groups, HEAD, HEAD), F32), jax.ShapeDtypeStruct((HEAD, groups), F32)],
        scratch_shapes=[pltpu.VMEM((2, HEAD, width), BF16)],
        input_output_aliases={7: 0},
        compiler_params=_params(2),
    )(proj, proj, dout_b, g_v, w_s, w_st, bst, dproj)


def _hgrn_bwd(proj, osum, dout_a, dproj, lb, g_norm, width, exchange, after):
    t = proj.shape[0]
    heads = width // HEAD
    nb = t // HEAD
    cpb = HEAD // A_CHUNK
    ubk = 2 if nb % 2 == 0 else 1
    q_scale = HEAD ** -0.5
    xin, xout = len(exchange.arrays), len(exchange.out_shapes)

    def body(q_ref, ffw_ref, fbw_ref, v_ref, og_ref, osum_ref, douta_ref, lb_ref, g_ref, dproj_hbm, *rest):
        xin_refs, rest = rest[:xin], rest[xin + 1:]
        out_ref, dgh_ref, dlb_ref = rest[:3]
        xout_refs, rest = rest[3:3 + xout], rest[3 + xout:]
        do_s, dq_s, dv_s, res_s, ck_s = rest[:5]
        sem_refs = rest[5:]
        h, p = pl.program_id(0), pl.program_id(1)
        f_refs = (ffw_ref, fbw_ref)

        @pl.when((h == 0) & (p == 0))
        def _():
            exchange.start(xin_refs, xout_refs, sem_refs)

        @pl.when(p == 0)
        def _():
            tril, triu = _chunk_masks()
            cum2 = jnp.concatenate([_ones(tril), _ones(triu)], axis=0)
            g_row = g_ref[...]

            def pass_norm(i, dgh):
                rows = pl.ds(pl.multiple_of(i * HEAD, HEAD), HEAD)
                o = osum_ref[rows, :]
                r = _rms(o)
                oh = o * r
                og = og_ref[rows, :]
                sg = _sigmoid(og)
                dout = douta_ref[rows, :].astype(F32)
                don = dout * (og * sg)
                res_s[4, rows, :] = (dout * (oh * g_row) * (sg * (1.0 + og * (1.0 - sg)))).astype(BF16)
                doh = don * g_row
                do_s[rows, :] = r * (doh - oh * jnp.mean(doh * oh, axis=-1, keepdims=True))
                return dgh + _colsum(don * oh)

            dgh_ref[...] = lax.fori_loop(0, nb, pass_norm, jnp.zeros((1, HEAD), F32))

            lbs = (lb_ref[0:1, :], lb_ref[1:2, :])
            zero_state = jnp.zeros((HEAD, HEAD), F32)

            def chunk_order(d):
                return list(range(cpb)) if d == 0 else list(range(cpb - 1, -1, -1))

            def chunk(x, j):
                return x[j * A_CHUNK:(j + 1) * A_CHUNK, :]

            def decay_row(e_big, j):
                return e_big[j * A_CHUNK:j * A_CHUNK + 1, :]

            def cat(parts):
                return jnp.concatenate([parts[j] for j in range(cpb)], axis=0)

            def block_states(d, start, incs, e_big):
                befores, st = {}, start
                for j in chunk_order(d):
                    befores[j] = st
                    st = st * decay_row(e_big, j) + incs[j]
                return befores, st

            def pass_states(it, states):
                loaded = []
                for u in range(ubk):
                    for d in range(2):
                        blk = it * ubk + u if d == 0 else nb - 1 - (it * ubk + u)
                        rows = pl.ds(pl.multiple_of(blk * HEAD, HEAD), HEAD)
                        loaded.append((d, blk, f_refs[d][rows, :], v_ref[rows, :]))
                blocks = [_hgrn_block(d, fv, lbs[d], cum2) for d, _, fv, _ in loaded]
                incs = [{j: _bdot(chunk(vv, j), chunk(k * erest, j), _TN) for j in range(cpb)}
                        for (_, _, _, vv), (k, _, _, _, _, erest) in zip(loaded, blocks)]
                states, starts = list(states), []
                for (d, _, _, _), (_, _, _, eb, _, erest), inc in zip(loaded, blocks, incs):
                    starts.append(states[d])
                    states[d] = block_states(d, states[d], inc, eb * erest)[1]
                for (d, blk, _, _), start in zip(loaded, starts):
                    ck_s[d, blk] = start
                return tuple(states)

            lax.fori_loop(0, nb // ubk, pass_states, (zero_state, zero_state))

            def pass_back(it, carry):
                gts, dlb = [carry[0], carry[1]], carry[2]
                loaded = []
                for u, d in ((u, d) for u in range(ubk) for d in range(2)):
                    blk = nb - 1 - (it * ubk + u) if d == 0 else it * ubk + u
                    rows = pl.ds(pl.multiple_of(blk * HEAD, HEAD), HEAD)
                    loaded.append((d, rows, f_refs[d][rows, :], q_ref[rows, :], v_ref[rows, :], do_s[rows, :], ck_s[d, blk]))
                blocks = [_hgrn_block(d, fv, lbs[d], cum2) for d, _, fv, _, _, _, _ in loaded]
                scaled = []
                for (_, _, _, qv, _, _, _), (k, _, _, eb, enb, erest) in zip(loaded, blocks):
                    qh = qv * q_scale
                    scaled.append((qh, qh * eb, k * enb, k * erest, eb * erest))
                masks = [tril if d == 0 else triu for d, *_ in loaded]
                atts = [jnp.where(m, _bdot(qd, kd, _NT), 0.0) for m, (_, qd, kd, _, _) in zip(masks, scaled)]
                datts = [jnp.where(m, _bdot(do, vv, _NT), 0.0) for m, (_, _, _, _, vv, do, _) in zip(masks, loaded)]
                dvs = [_bdot(att, do, _TN) for att, (_, _, _, _, _, do, _) in zip(atts, loaded)]
                dqds = [_bdot(datt, kd) for datt, (_, _, kd, _, _) in zip(datts, scaled)]
                dkds = [_bdot(datt, qd, _TN) for datt, (_, qd, _, _, _) in zip(datts, scaled)]
                s_incs = [{j: _bdot(chunk(vv, j), chunk(ke, j), _TN) for j in range(cpb)}
                          for (_, _, _, _, vv, _, _), (_, _, _, ke, _) in zip(loaded, scaled)]
                g_incs = [{j: _bdot(chunk(do, j), chunk(qd, j), _TN) for j in range(cpb)}
                          for (_, _, _, _, _, do, _), (_, qd, _, _, _) in zip(loaded, scaled)]
                befores, afters, g_at = [], [], []
                for (d, _, _, _, _, _, ck), (_, _, _, _, e_big), s_inc, g_inc in zip(loaded, scaled, s_incs, g_incs):
                    order = chunk_order(d)
                    before, after = block_states(d, ck, s_inc, e_big)
                    befores.append(before)
                    afters.append({j: (before[order[n + 1]] if n + 1 < cpb else after) for n, j in enumerate(order)})
                    at, gt = {}, gts[d]
                    for j in reversed(order):
                        at[j] = gt
                        gt = gt * decay_row(e_big, j) + g_inc[j]
                    gts[d] = gt
                    g_at.append(at)
                dqd_i = [{j: _bdot(chunk(do, j), before[j]) for j in range(cpb)}
                         for (_, _, _, _, _, do, _), before in zip(loaded, befores)]
                dv_i = [{j: _bdot(chunk(ke, j), at[j], _NT) for j in range(cpb)}
                        for (_, _, _, ke, _), at in zip(scaled, g_at)]
                dke = [{j: _bdot(chunk(vv, j), at[j]) for j in range(cpb)}
                       for (_, _, _, _, vv, _, _), at in zip(loaded, g_at)]
                results, new = [], []
                for n, ((d, rows, _, _, _, _, _), (k, sf, big_f, eb, enb, erest), (qh, _, _, _, _)) in enumerate(
                        zip(loaded, blocks, scaled)):
                    dqh = (dqds[n] + cat(dqd_i[n])) * eb
                    dk = dkds[n] * enb + cat(dke[n]) * erest
                    carry_rows = {j: jnp.broadcast_to(_colsum(g_at[n][j] * afters[n][j]), (A_CHUNK, HEAD))
                                  for j in range(cpb)}
                    dlf = _dot_split(_ones(triu if d == 0 else tril), qh * dqh - k * dk) + cat(carry_rows)
                    common = dlf / big_f - dk
                    results.append((d, rows, (k * sf * common).astype(BF16), dqh.astype(BF16),
                                    (dvs[n] + cat(dv_i[n])).astype(BF16)))
                    new.append(_colsum((1.0 - sf) * common))
                for d, rows, df16, dq16, dv16 in results:
                    res_s[1 + d, rows, :] = df16
                    dq_s[d, rows, :] = dq16
                    dv_s[d, rows, :] = dv16
                per_dir = [sum(c for (d, *_), c in zip(loaded, new) if d == dd) for dd in range(2)]
                return gts[0], gts[1], dlb + jnp.concatenate(per_dir, axis=0)

            dlb_ref[...] = lax.fori_loop(0, nb // ubk, pass_back,
                                         (zero_state, zero_state, jnp.zeros((2, HEAD), F32)))[2]

            def pass_out(i, carry):
                rows = pl.ds(pl.multiple_of(i * HEAD, HEAD), HEAD)
                dq = dq_s[0, rows, :].astype(F32) + dq_s[1, rows, :].astype(F32)
                res_s[0, rows, :] = (dq * q_scale).astype(BF16)
                res_s[3, rows, :] = (dv_s[0, rows, :].astype(F32) + dv_s[1, rows, :].astype(F32)).astype(BF16)
                return carry

            lax.fori_loop(0, nb, pass_out, 0)

        out_ref[...] = res_s[p]

        @pl.when((h == heads - 1) & (p == 4))
        def _():
            exchange.finish(xin_refs, xout_refs, sem_refs)

    def col(pp):
        return pl.BlockSpec((t, HEAD), lambda h, p: (0, pp * heads + h))

    n_in = dproj.shape[1]
    any_spec = pl.BlockSpec(memory_space=pl.ANY)
    results = pl.pallas_call(
        body, name="hgrn_bwd", grid=(heads, 5),
        in_specs=[col(0), col(1), col(2), col(3), col(4),
                  pl.BlockSpec((t, HEAD), lambda h, p: (0, h)), pl.BlockSpec((t, HEAD), lambda h, p: (0, h)),
                  pl.BlockSpec((2, HEAD), lambda h, p: (0, h)), pl.BlockSpec((1, HEAD), lambda h, p: (0, 0)),
                  any_spec] + [any_spec] * (xin + 1),
        out_specs=[pl.BlockSpec((t, HEAD), lambda h, p: (0, p * heads + h)),
                   pl.BlockSpec((None, 1, HEAD), lambda h, p: (h, 0, 0)),
                   pl.BlockSpec((2, HEAD), lambda h, p: (0, h))] + [any_spec] * xout,
        out_shape=[jax.ShapeDtypeStruct((t, n_in), BF16), jax.ShapeDtypeStruct((heads, 1, HEAD), F32),
                   jax.ShapeDtypeStruct((2, width), F32)] + list(exchange.out_shapes),
        scratch_shapes=[pltpu.VMEM((t, HEAD), F32), pltpu.VMEM((2, t, HEAD), BF16), pltpu.VMEM((2, t, HEAD), BF16),
                        pltpu.VMEM((5, t, HEAD), BF16), pltpu.VMEM((2, nb, HEAD, HEAD), F32)] + list(exchange.sems),
        input_output_aliases={9: 0},
        compiler_params=_params(2),
    )(proj, proj, proj, proj, proj, osum, dout_a, lb, g_norm, dproj, *exchange.arrays, after)
    return results[0], results[1], results[2], results[3:]


def _adamw(w, g, m, v):
    m = ADAM_B1 * m + (1.0 - ADAM_B1) * g
    v = ADAM_B2 * v + (1.0 - ADAM_B2) * (g * g)
    m_hat = m / (1.0 - ADAM_B1 ** ADAM_STEP)
    v_hat = v / (1.0 - ADAM_B2 ** ADAM_STEP)
    delta = -ADAM_LR * (m_hat / (jnp.sqrt(v_hat) + ADAM_EPS) + ADAM_WD * w)
    return delta, m, v


def _adamw_big(name, me, w, m, v, g_parts, landing, axis):
    r, c = w.shape
    tr = _tile(r, 128)
    n_parts = len(g_parts)
    per = N_DEV // n_parts

    def body(me_ref, w_ref, m_ref, v_ref, *rest):
        g_refs, (l_ref, og_ref, od_ref, om_ref, ov_ref) = rest[:n_parts], rest[n_parts:]
        g = g_refs[0][...]
        for p in range(1, n_parts):
            g = jnp.where(me_ref[0] // per == p, g_refs[p][...], g)
        for s in range(N_DEV - 1):
            g = g + l_ref[s].astype(F32)
        og_ref[...] = g
        od_ref[...], om_ref[...], ov_ref[...] = _adamw(w_ref[...], g, m_ref[...], v_ref[...])

    shard = pl.BlockSpec((tr, c), lambda i, me_ref: (i, 0))
    if axis == 1:
        own = pl.BlockSpec((tr, c), lambda i, me_ref: (i, me_ref[0] % per))
    else:
        assert n_parts == 1
        own = pl.BlockSpec((tr, c), lambda i, me_ref: (me_ref[0] * (r // tr) + i, 0))
    grid_spec = pltpu.PrefetchScalarGridSpec(
        num_scalar_prefetch=1, grid=(r // tr,),
        in_specs=[shard, shard, shard] + [own] * n_parts + [pl.BlockSpec((N_DEV - 1, tr, c), lambda i, me_ref: (0, i, 0))],
        out_specs=[shard] * 4)
    return pl.pallas_call(
        body, name=name, grid_spec=grid_spec, out_shape=[jax.ShapeDtypeStruct((r, c), F32)] * 4,
        compiler_params=_params(1),
    )(me, w, m, v, *g_parts, landing)


def _adamw_ada(sct, dmod_mine, w, m, v):
    d, n = w.shape
    tr = _tile(d, 256)

    def body(s_ref, dm_ref, w_ref, m_ref, v_ref, og_ref, od_ref, om_ref, ov_ref):
        g = _dot(s_ref[...], dm_ref[...], precision=HIGHEST)
        og_ref[...] = g
        od_ref[...], om_ref[...], ov_ref[...] = _adamw(w_ref[...], g, m_ref[...], v_ref[...])

    blk = pl.BlockSpec((tr, n), lambda i: (i, 0))
    return pl.pallas_call(
        body, name="adamw_ada", grid=(d // tr,),
        in_specs=[pl.BlockSpec((tr, N_DEV), lambda i: (i, 0)), pl.BlockSpec((N_DEV, n), lambda i: (0, 0)), blk, blk, blk],
        out_specs=[blk] * 4, out_shape=[jax.ShapeDtypeStruct((d, n), F32)] * 4, compiler_params=_params(1),
    )(sct, dmod_mine, w, m, v)


def _adamw_small(gathered, w, m, v):
    def body(g_ref, w_ref, m_ref, v_ref, og_ref, od_ref, om_ref, ov_ref):
        g = g_ref[0]
        for s in range(1, N_DEV):
            g = g + g_ref[s]
        og_ref[...] = g
        od_ref[...], om_ref[...], ov_ref[...] = _adamw(w_ref[...], g, m_ref[...], v_ref[...])

    return pl.pallas_call(
        body, name="adamw_small", out_shape=[jax.ShapeDtypeStruct(w.shape, F32)] * 4,
        compiler_params=pltpu.CompilerParams(vmem_limit_bytes=VMEM_LIMIT),
    )(gathered, w, m, v)


def _adamw_lb(dlb_mine, lb_logits, m, v):
    def body(d_ref, l_ref, m_ref, v_ref, og_ref, od_ref, om_ref, ov_ref):
        dlb = d_ref[0]
        for s in range(1, N_DEV):
            dlb = dlb + d_ref[s]
        for dr in range(2):
            lb = _sigmoid(l_ref[dr][0:1, :] - l_ref[dr][1:2, :])
            d0 = dlb[dr:dr + 1] * lb * (1.0 - lb)
            g = jnp.concatenate([d0, -d0], axis=0)
            og_ref[dr] = g
            od_ref[dr], om_ref[dr], ov_ref[dr] = _adamw(l_ref[dr], g, m_ref[dr], v_ref[dr])

    return pl.pallas_call(body, name="adamw_lb", out_shape=[jax.ShapeDtypeStruct(lb_logits.shape, F32)] * 4,
                          )(dlb_mine, lb_logits, m, v)


def _rows(a, pad_to=8):
    flat = a.reshape(-1, LANE)
    pad = (-flat.shape[0]) % pad_to
    return jnp.pad(flat, ((0, pad), (0, 0))) if pad else flat


def kernel(x, c, w_ada, b_ada, g_pre_mix, g_post_mix, g_pre_ffn, g_post_ffn, w_in, lb_logits, g_hgrn_norm, w_a_out, g_sgu_norm, w_spatial, b_spatial, w_b_out, w_o, w_ff1, w_ff2, loss_target, m_w_ada, m_b_ada, m_g_pre_mix, m_g_post_mix, m_g_pre_ffn, m_g_post_ffn, m_w_in, m_lb_logits, m_g_hgrn_norm, m_w_a_out, m_g_sgu_norm, m_w_spatial, m_b_spatial, m_w_b_out, m_w_o, m_w_ff1, m_w_ff2, v_w_ada, v_b_ada, v_g_pre_mix, v_g_post_mix, v_g_pre_ffn, v_g_post_ffn, v_w_in, v_lb_logits, v_g_hgrn_norm, v_w_a_out, v_g_sgu_norm, v_w_spatial, v_b_spatial, v_w_b_out, v_w_o, v_w_ff1, v_w_ff2):
    t, d = x.shape[1], x.shape[2]
    n_in = w_in.shape[2] * N_DEV
    width = (n_in - 2 * d) // 7
    heads = width // HEAD
    assert heads == N_DEV and width % LANE == 0
    d_ff = w_ff1.shape[2] * N_DEV
    n_ada = w_ada.shape[2]
    me = _dev_index()
    me_arr = me.reshape(1).astype(jnp.int32)
    x2, tgt = x[0], loss_target[0]

    big = [w_in[0], w_a_out[0], w_b_out[0], w_o[0], w_ff1[0], w_ff2[0]]
    big_axes = [1, 1, 1, 0, 1, 0]
    big_names = ["w_in", "w_a_out", "w_b_out", "w_o", "w_ff1", "w_ff2"]
    w_in16 = _cast_bf16("cast_w_in", big[0])
    own_parts = [_cast_into_full("cast_" + nm, me_arr, w, ax) for nm, w, ax in zip(big_names[1:], big[1:], big_axes[1:])]

    c_rows = d // LANE
    small = _all_gather_small("gather_c_lb", _prep_small(c[0:1], lb_logits))
    sc_all = small[:, :c_rows, :].reshape(N_DEV, d)
    lb = jnp.transpose(small[:, c_rows:c_rows + 2, :], (1, 0, 2)).reshape(2, width)
    b_shard = lax.dynamic_slice_in_dim(b_ada, me * n_ada, n_ada, axis=1)
    mod_sh = _mod_shard(sc_all, w_ada[0], b_shard)
    mod_all = _all_gather_small("gather_mod", _rows(mod_sh))
    mod_all = mod_all[:, :N_DEV * n_ada // LANE, :].reshape(N_DEV, N_DEV, n_ada)
    mod6 = lax.dynamic_index_in_dim(mod_all, me, axis=1, keepdims=False).reshape(N_MOD, d)
    sh1, sc1, gt1, sh2, sc2, gt2 = [mod6[i:i + 1] for i in range(N_MOD)]

    a1 = _norm_mod(x2, g_pre_mix, sh1, sc1)
    tm = _tile(t, 512)

    def store_f32(acc, i, j, extra_refs, out_refs, rows):
        out_refs[0][...] = acc

    xq, yq, cq = lax.axis_index("x"), lax.axis_index("y"), lax.axis_index("c")
    chips = [(1 - xq, yq), (xq, 1 - yq), (1 - xq, 1 - yq)]
    order = jnp.stack([me, 4 * xq + 2 * yq + 1 - cq]
                      + [4 * a + 2 * b + cq for a, b in chips[:2]] + [4 * a + 2 * b + 1 - cq for a, b in chips[:2]]
                      + [4 * chips[2][0] + 2 * chips[2][1] + cq, 4 * chips[2][0] + 2 * chips[2][1] + 1 - cq]).astype(jnp.int32)
    proj, wf_in = _proj_gather(a1, w_in16, order)

    proj, own_parts = lax.optimization_barrier((proj, own_parts))
    gathers = {}
    for key, lo, hi in (("mid", 1, 4), ("ff1", 4, 5), ("ff2", 5, 6)):
        far, near = _gather_stage_plans(own_parts[lo - 1:hi - 1], big_axes[lo:hi])
        gathers[key] = [far, near, _split_start("gather_%s_start" % key, far, landing=own_parts[lo - 1:hi - 1])]

    def pass_on(key, after):
        far, near, (sems, thru, _) = gathers[key]
        parts = _split_wait("gather_%s_wait" % key, far, sems, thru, after)[1]
        gathers[key].append(_split_start("pass_%s_start" % key, near, landing=list(parts)))
        return gathers[key][3][2]

    def gathered_weights(key, after):
        near, (sems, thru, _) = gathers[key][1], gathers[key][3]
        return _split_wait("pass_%s_wait" % key, near, sems, thru, after)[1]

    out_a, osum, _ = _hgrn_fwd(proj, lb, g_hgrn_norm, width, _NO_EXCHANGE,
                               after=[gathers[key][2][2] for key in ("mid", "ff1", "ff2")])
    passed_mid = pass_on("mid", out_a)
    z_block = 5
    bst = b_spatial[0].T
    out_b = _sgu_fwd(proj, g_sgu_norm, w_spatial[0], bst, width, z_block)
    wf_a, wf_b, wf_o = gathered_weights("mid", out_b)

    tn_d = _tile(d, 512)
    blk_d = ((tm, tn_d), lambda i, j: (i, j))
    y_a, = _mm("y_a", out_a, wf_a, _NN, t, d, width, tm, tn_d, width, _after(passed_mid),
               [(jax.ShapeDtypeStruct((t, d), F32),) + blk_d], store_f32)
    ga_blk = (5 * width + 2 * width) // tn_d
    gb_blk = ga_blk + d // tn_d

    def merge(acc, i, j, extra_refs, out_refs, rows):
        ga, gb, ya = extra_refs
        out_refs[0][...] = acc
        out_refs[1][...] = (_sigmoid(ga[...]) * ya[...] + _sigmoid(gb[...]) * acc).astype(BF16)

    y_b, merged = _mm("y_b_merge", out_b, wf_b, _NN, t, d, width, tm, tn_d, width,
                      [(proj, (tm, tn_d), lambda i, j: (i, ga_blk + j)), (proj, (tm, tn_d), lambda i, j: (i, gb_blk + j)),
                       (y_a,) + blk_d],
                      [(jax.ShapeDtypeStruct((t, d), F32),) + blk_d, (jax.ShapeDtypeStruct((t, d), BF16),) + blk_d], merge)

    tr = _tile(t, 512)
    rc = 32 if tr % 32 == 0 else None
    row_d = ((tr, d), lambda i, j: (i, 0))
    vec_d = ((1, d), lambda i, j: (0, 0))

    passed_ff1 = pass_on("ff1", merged)

    def post_mix(acc, i, j, extra_refs, out_refs, rows):
        x_r, gt1_r, g2_r, g3_r, sc2_r, sh2_r = extra_refs[:6]
        h1 = x_r[rows, :] + gt1_r[...] * (acc * _rms(acc) * g2_r[...])
        out_refs[0][rows, :] = acc
        out_refs[1][rows, :] = h1
        out_refs[2][rows, :] = ((h1 * _rms(h1) * g3_r[...]) * (1.0 + sc2_r[...]) + sh2_r[...]).astype(BF16)

    mo, h1, a2 = _mm("w_o_post_mix", merged, wf_o, _NN, t, d, d, tr, d, d,
                     [(x2,) + row_d, (gt1,) + vec_d, (g_post_mix,) + vec_d, (g_pre_ffn,) + vec_d, (sc2,) + vec_d, (sh2,) + vec_d]
                     + _after(passed_ff1),
                     [(jax.ShapeDtypeStruct((t, d), F32),) + row_d, (jax.ShapeDtypeStruct((t, d), F32),) + row_d,
                      (jax.ShapeDtypeStruct((t, d), BF16),) + row_d], post_mix, row_chunk=rc)

    tn_f = _tile(d_ff, 1024)
    blk_f = ((tm, tn_f), lambda i, j: (i, j))

    def relu_sq(acc, i, j, extra_refs, out_refs, rows):
        r = jnp.maximum(acc, 0.0)
        out_refs[0][...] = acc.astype(BF16)
        out_refs[1][...] = (r * r).astype(BF16)

    wf_1, = gathered_weights("ff1", a2)
    hff, act = _mm(
        "ff1", a2, wf_1, _NN, t, d_ff, d, tm, tn_f, d, [],
        [(jax.ShapeDtypeStruct((t, d_ff), BF16),) + blk_f, (jax.ShapeDtypeStruct((t, d_ff), BF16),) + blk_f], relu_sq)
    pass_on("ff2", hff)
    wf_2, = gathered_weights("ff2", act)

    sums_d = ((8, d), lambda i, j: (0, 0))

    def zero_first(sums_r, i, rows):
        if rows.start in (None, 0):
            @pl.when(i == 0)
            def _():
                sums_r[...] = jnp.zeros_like(sums_r)

    def loss_head(acc, i, j, extra_refs, out_refs, rows):
        h1_r, tgt_r, gt2_r, g4_r = extra_refs
        dy_r, dff_r, sums_r = out_refs
        r4 = _rms(acc)
        ffn = acc * r4
        n4 = ffn * g4_r[...]
        err = h1_r[rows, :] + gt2_r[...] * n4 - tgt_r[rows, :]
        dy = err * (1.0 / d)
        dy_r[rows, :] = dy
        dn4 = dy * gt2_r[...]
        dffn = dn4 * g4_r[...]
        dff_r[rows, :] = (r4 * (dffn - ffn * jnp.mean(dffn * ffn, axis=-1, keepdims=True))).astype(BF16)
        zero_first(sums_r, i, rows)

        sums_r[0:1, :] += _colsum(err * err)
        sums_r[1:2, :] += _colsum(dy * n4)
        sums_r[2:3, :] += _colsum(dn4 * ffn)

    tk_f = _tile(d_ff, 1024)
    dy, dff, sums_f = _mm("ff2_loss", act, wf_2, _NN, t, d, d_ff, tr, d, tk_f,
                          [(h1,) + row_d, (tgt,) + row_d, (gt2,) + vec_d, (g_post_ffn,) + vec_d],
                          [(jax.ShapeDtypeStruct((t, d), F32),) + row_d, (jax.ShapeDtypeStruct((t, d), BF16),) + row_d,
                           (jax.ShapeDtypeStruct((8, d), F32),) + sums_d], loss_head, row_chunk=rc)
    loss_mine = (0.5 / d) * jnp.sum(sums_f[0])

    def relu_sq_bwd(acc, i, j, extra_refs, out_refs, rows):
        out_refs[0][...] = (acc * (2.0 * jnp.maximum(extra_refs[0][...].astype(F32), 0.0))).astype(BF16)

    dhff, = _mm("d_hff", dff, wf_2, _NT, t, d_ff, d, tm, tn_f, d, [(hff,) + blk_f],
                [(jax.ShapeDtypeStruct((t, d_ff), BF16),) + blk_f], relu_sq_bwd)
    scatters = {}

    def send_grads(key, grads16, axes):
        plan = _scatter_plan(grads16, axes)
        scatters[key] = (plan,) + _split_start("scatter_%s_start" % key, plan)
        return scatters[key][3]

    def received_grads(key, after):
        plan, sems, thru, _ = scatters[key]
        return _split_wait("scatter_%s_wait" % key, plan, sems, thru, after)[1]

    gw_ff2, gw_ff2_16 = _grad_w("grad_w_ff2", act, dff)
    sent_ff2 = send_grads("ff2", [gw_ff2_16], big_axes[5:6])
    gw_ff1, gw_ff1_16 = _grad_w("grad_w_ff1", a2, dhff, token=sent_ff2)
    sent_ff1 = send_grads("ff1", [gw_ff1_16], big_axes[4:5])

    def pre_ffn_bwd(acc, i, j, extra_refs, out_refs, rows):
        h1_r, dy_r, mo_r, sc2_r, g3_r, gt1_r, g2_r = extra_refs[:7]
        dh1_r, dmo_r, sums_r = out_refs
        h1v = h1_r[rows, :]
        r3 = _rms(h1v)
        h1n = h1v * r3
        dn3 = acc * (1.0 + sc2_r[...])
        dh1n = dn3 * g3_r[...]
        dh1 = dy_r[rows, :] + r3 * (dh1n - h1n * jnp.mean(dh1n * h1n, axis=-1, keepdims=True))
        dh1_r[rows, :] = dh1
        mov = mo_r[rows, :]
        r2 = _rms(mov)
        mon = mov * r2
        dn2 = dh1 * gt1_r[...]
        dmon = dn2 * g2_r[...]
        dmo_r[rows, :] = (r2 * (dmon - mon * jnp.mean(dmon * mon, axis=-1, keepdims=True))).astype(BF16)
        zero_first(sums_r, i, rows)

        sums_r[0:1, :] += _colsum(acc)
        sums_r[1:2, :] += _colsum(acc * (h1n * g3_r[...]))
        sums_r[2:3, :] += _colsum(dn3 * h1n)
        sums_r[3:4, :] += _colsum(dh1 * (mon * g2_r[...]))
        sums_r[4:5, :] += _colsum(dn2 * mon)

    dh1, dmo, sums_m = _mm("d_a2_pre_ffn", dhff, wf_1, _NT, t, d, d_ff, tr, d, tk_f,
                           [(h1,) + row_d, (dy,) + row_d, (mo,) + row_d, (sc2,) + vec_d, (g_pre_ffn,) + vec_d,
                            (gt1,) + vec_d, (g_post_mix,) + vec_d] + _after(sent_ff1),
                           [(jax.ShapeDtypeStruct((t, d), F32),) + row_d, (jax.ShapeDtypeStruct((t, d), BF16),) + row_d,
                            (jax.ShapeDtypeStruct((8, d), F32),) + sums_d], pre_ffn_bwd, row_chunk=rc)
    gw_o, gw_o_16 = _grad_w("grad_w_o", merged, dmo)

    n_j = d // tn_d

    def merge_bwd_body(dmo_ref, wo_ref, ga_ref, gb_ref, ya_ref, yb_ref, dya_ref, dyb_ref, dproj_ref, acc_s):
        g = pl.program_id(2)

        @pl.when(g == 0)
        def _():
            dm = _dot(dmo_ref[...], wo_ref[...], _NT)
            acc_s[...] = dm
            sa = _sigmoid(ga_ref[...])
            dya_ref[...] = (dm * sa).astype(BF16)
            dproj_ref[...] = (dm * ya_ref[...] * sa * (1.0 - sa)).astype(BF16)

        @pl.when(g == 1)
        def _():
            dm = acc_s[...]
            sb = _sigmoid(gb_ref[...])
            dyb_ref[...] = (dm * sb).astype(BF16)
            dproj_ref[...] = (dm * yb_ref[...] * sb * (1.0 - sb)).astype(BF16)

    tile3 = pl.BlockSpec((tm, tn_d), lambda i, j, g: (i, j))
    dy_a, dy_b, dproj = pl.pallas_call(
        merge_bwd_body, name="d_merged", grid=(t // tm, n_j, 2),
        in_specs=[pl.BlockSpec((tm, d), lambda i, j, g: (i, 0)), pl.BlockSpec((tn_d, d), lambda i, j, g: (j, 0)),
                  pl.BlockSpec((tm, tn_d), lambda i, j, g: (i, ga_blk + j)),
                  pl.BlockSpec((tm, tn_d), lambda i, j, g: (i, gb_blk + j)), tile3, tile3],
        out_specs=[tile3, tile3, pl.BlockSpec((tm, tn_d), lambda i, j, g: (i, ga_blk + g * n_j + j))],
        out_shape=[jax.ShapeDtypeStruct((t, d), BF16), jax.ShapeDtypeStruct((t, d), BF16),
                   jax.ShapeDtypeStruct((t, n_in), BF16)],
        scratch_shapes=[pltpu.VMEM((tm, tn_d), F32)], compiler_params=_params(3),
    )(dmo, wf_o, proj, proj, y_a, y_b)

    def store_bf16(acc, i, j, extra_refs, out_refs, rows):
        out_refs[0][...] = acc.astype(BF16)

    tn_w = _tile(width, 512)
    blk_w = ((tm, tn_w), lambda i, j: (i, j))
    dout_a, = _mm("d_out_a", dy_a, wf_a, _NT, t, width, d, tm, tn_w, d, [],
                  [(jax.ShapeDtypeStruct((t, width), BF16),) + blk_w], store_bf16)
    dout_b, = _mm("d_out_b", dy_b, wf_b, _NT, t, width, d, tm, tn_w, d, [],
                  [(jax.ShapeDtypeStruct((t, width), BF16),) + blk_w], store_bf16)
    gw_a, gw_a_16 = _grad_w("grad_w_a_out", out_a, dy_a)
    gw_b, gw_b_16 = _grad_w("grad_w_b_out", out_b, dy_b)

    w_st = jnp.swapaxes(w_spatial[0], 1, 2)
    dproj, dg_sgu, dw_sp, dbst = _sgu_bwd(proj, dout_b, dproj, g_sgu_norm, w_spatial[0], w_st, bst, width, z_block)
    sent_mid = send_grads("mid", [gw_a_16, gw_b_16, gw_o_16], big_axes[1:4])
    dproj, dgh_heads, dlb, _ = _hgrn_bwd(proj, osum, dout_a, dproj, lb, g_hgrn_norm, width, _NO_EXCHANGE,
                                         after=sent_mid)
    gw_in, gw_in_16 = _grad_w("grad_w_in", a1, dproj)
    sent_in = send_grads("in", [gw_in_16], big_axes[:1])

    def pre_mix_bwd(acc, i, j, extra_refs, out_refs, rows):
        x_r, dh1_r, sc1_r, g1_r = extra_refs[:4]
        dx_r, sums_r = out_refs
        xv = x_r[rows, :]
        r1 = _rms(xv)
        xn = xv * r1
        dn1 = acc * (1.0 + sc1_r[...])
        dxn = dn1 * g1_r[...]
        dx_r[rows, :] = dh1_r[rows, :] + r1 * (dxn - xn * jnp.mean(dxn * xn, axis=-1, keepdims=True))
        zero_first(sums_r, i, rows)

        sums_r[0:1, :] += _colsum(acc)
        sums_r[1:2, :] += _colsum(acc * (xn * g1_r[...]))
        sums_r[2:3, :] += _colsum(dn1 * xn)

    tk_in = _tile(n_in, 1024)
    grad_x, sums_x = _mm(
        "d_a1_pre_mix", dproj, wf_in, _NT, t, d, n_in, tr, d, tk_in,
        [(x2,) + row_d, (dh1,) + row_d, (sc1,) + vec_d, (g_pre_mix,) + vec_d] + _after(sent_in),
        [(jax.ShapeDtypeStruct((t, d), F32),) + row_d, (jax.ShapeDtypeStruct((8, d), F32),) + sums_d],
        pre_mix_bwd, row_chunk=rc)

    dmod = jnp.concatenate([sums_x[0:2], sums_m[3:4], sums_m[0:2], sums_f[1:2]], axis=0).reshape(N_DEV, n_ada // LANE, LANE)
    ada_rows = -(-(n_ada // LANE) // 8) * 8
    dmod = jnp.pad(dmod, ((0, 0), (0, ada_rows - n_ada // LANE), (0, 0))).reshape(N_DEV * ada_rows, LANE)
    parts = [dmod, _rows(sums_x[2:3]), _rows(sums_m[4:5]), _rows(sums_m[2:3]), _rows(sums_f[2:3]),
             _rows(jnp.sum(dgh_heads, axis=0)), _rows(dg_sgu), _rows(dw_sp), _rows(dbst.T)]
    n_params = sum(p.shape[0] for p in parts)
    parts.append(jnp.full((8, LANE), loss_mine, F32))
    n_common = n_params + 8
    payload = jnp.concatenate(parts + [_rows(dlb)], axis=0)

    moms = [m_w_in, m_w_a_out, m_w_b_out, m_w_o, m_w_ff1, m_w_ff2]
    vars_ = [v_w_in, v_w_a_out, v_w_b_out, v_w_o, v_w_ff1, v_w_ff2]
    big_out = {}

    def big_update(nm, g_full, landing):
        k = big_names.index(nm)
        outs = _adamw_big("adamw_" + nm, me_arr, big[k], moms[k][0], vars_[k][0], g_full, landing, big_axes[k])
        big_out[nm] = [o[None] for o in outs]
        return outs[0]

    land_ff2, = received_grads("ff2", grad_x)
    done = big_update("w_ff2", [gw_ff2], land_ff2)
    land_ff1, = received_grads("ff1", done)
    done = big_update("w_ff1", [gw_ff1], land_ff1)
    land_a, land_b, land_o = received_grads("mid", done)
    big_update("w_a_out", [gw_a], land_a)
    big_update("w_b_out", [gw_b], land_b)
    done = big_update("w_o", [gw_o], land_o)

    payload, _ = lax.optimization_barrier((payload, done))
    gathered = _all_gather_small("gather_small_grads", payload)

    dmod_mine = lax.dynamic_slice_in_dim(gathered[:, :N_DEV * ada_rows, :].reshape(N_DEV, N_DEV, ada_rows * LANE),
                                         me, 1, axis=1)[:, 0, :n_ada]
    ada_out = [o[None] for o in _adamw_ada(sc_all.T, dmod_mine, w_ada[0], m_w_ada[0], v_w_ada[0])]

    def pack(b_, g1_, g2_, g3_, g4_, gh_, gs_, ws_, bs_):
        b3 = b_.reshape(N_DEV, n_ada // LANE, LANE)
        b3 = jnp.pad(b3, ((0, 0), (0, ada_rows - n_ada // LANE), (0, 0))).reshape(N_DEV * ada_rows, LANE)
        return jnp.concatenate([b3, _rows(g1_), _rows(g2_), _rows(g3_), _rows(g4_), _rows(gh_), _rows(gs_),
                                _rows(ws_), _rows(bs_), jnp.zeros((8, LANE), F32)], axis=0)

    small_w = (b_ada, g_pre_mix, g_post_mix, g_pre_ffn, g_post_ffn, g_hgrn_norm, g_sgu_norm, w_spatial, b_spatial)
    small_m = (m_b_ada, m_g_pre_mix, m_g_post_mix, m_g_pre_ffn, m_g_post_ffn, m_g_hgrn_norm, m_g_sgu_norm, m_w_spatial, m_b_spatial)
    small_v = (v_b_ada, v_g_pre_mix, v_g_post_mix, v_g_pre_ffn, v_g_post_ffn, v_g_hgrn_norm, v_g_sgu_norm, v_w_spatial, v_b_spatial)
    packed = _adamw_small(gathered[:, :n_common, :], pack(*small_w), pack(*small_m), pack(*small_v))

    def unpack(slab):
        outs, at = [], 0
        b3 = slab[:N_DEV * ada_rows].reshape(N_DEV, ada_rows, LANE)[:, :n_ada // LANE, :]
        outs.append(b3.reshape(b_ada.shape))
        at = N_DEV * ada_rows
        for ref in small_w[1:]:
            n_el = ref.size
            n_r = -(-(n_el // LANE) // 8) * 8
            outs.append(slab[at:at + n_el // LANE].reshape(ref.shape))
            at += n_r
        return outs

    small_out = [unpack(s) for s in packed]
    loss = packed[0][n_params, 0]

    dlb_all = gathered[:, n_common:n_common + 2 * heads, :].reshape(N_DEV, 2, heads, LANE)
    dlb_mine = lax.dynamic_index_in_dim(dlb_all, me, axis=2, keepdims=False)
    lb_out = _adamw_lb(dlb_mine, lb_logits, m_lb_logits, v_lb_logits)

    land_in, = received_grads("in", ada_out[0])
    big_update("w_in", [gw_in], land_in)

    order = ["w_ada", "b_ada", "g_pre_mix", "g_post_mix", "g_pre_ffn", "g_post_ffn", "w_in", "lb_logits", "g_hgrn_norm",
             "w_a_out", "g_sgu_norm", "w_spatial", "b_spatial", "w_b_out", "w_o", "w_ff1", "w_ff2"]
    small_names = ["b_ada", "g_pre_mix", "g_post_mix", "g_pre_ffn", "g_post_ffn", "g_hgrn_norm", "g_sgu_norm", "w_spatial", "b_spatial"]

    def leaf(kind, nm):
        if nm == "w_ada":
            return ada_out[kind]
        if nm == "lb_logits":
            return lb_out[kind]
        if nm in big_out:
            return big_out[nm][kind]
        return small_out[kind][small_names.index(nm)]

    result = [loss, grad_x[None]]
    for kind in range(4):
        result += [leaf(kind, nm) for nm in order]
    return tuple(result)
```

```python
import functools
import math

import jax
import jax.numpy as jnp
from jax import lax
from jax.experimental import pallas as pl
from jax.experimental.pallas import tpu as pltpu

F32 = jnp.float32
BF16 = jnp.bfloat16
MESH = pl.DeviceIdType.MESH
HIGHEST = lax.Precision.HIGHEST

N_DEV = 8
HEAD = 128
A_CHUNK = 32
N_MOD = 6
EPS = 1e-6
LANE = 128
VMEM_LIMIT = 60 * 1024 * 1024

ADAM_LR = 0.001
ADAM_B1 = 0.9
ADAM_B2 = 0.999
ADAM_EPS = 1e-08
ADAM_WD = 0.01
ADAM_STEP = 10

_NN = (((1,), (0,)), ((), ()))
_NT = (((1,), (1,)), ((), ()))
_TN = (((0,), (0,)), ((), ()))


def _dot(a, b, dims=_NN, precision=None):
    return lax.dot_general(a, b, dims, preferred_element_type=F32, precision=precision)


def _bdot(a, b, dims=_NN):
    return _dot(a.astype(BF16), b.astype(BF16), dims)


def _params(n_grid):
    return pltpu.CompilerParams(dimension_semantics=("arbitrary",) * n_grid, vmem_limit_bytes=VMEM_LIMIT)


def _dev_index():
    return lax.axis_index("x") * 4 + lax.axis_index("y") * 2 + lax.axis_index("c")


def _dev_coords(i):
    return (i // 4, (i // 2) % 2, i % 2)


def _sigmoid(x):
    return 1.0 / (1.0 + jnp.exp(-x))


def _erf(x):
    ax = jnp.abs(x)
    t = 1.0 / (1.0 + 0.3275911 * ax)
    poly = ((((1.061405429 * t - 1.453152027) * t + 1.421413741) * t - 0.284496736) * t + 0.254829592) * t
    y = 1.0 - poly * jnp.exp(-ax * ax)
    return jnp.where(x < 0, -y, y)


def _gelu_and_grad(x):
    cdf = 0.5 * (1.0 + _erf(x * (2.0 ** -0.5)))
    pdf = jnp.exp(-0.5 * x * x) * (1.0 / math.sqrt(2.0 * math.pi))
    return x * cdf, cdf + x * pdf


def _rms(x):
    return lax.rsqrt(jnp.mean(x * x, axis=-1, keepdims=True) + EPS)


def _colsum(x):
    return jnp.sum(x, axis=0, keepdims=True)


def _tile(n, want):
    if n <= want:
        return n
    t = (want // LANE) * LANE
    while n % t:
        t -= LANE
    assert t > 0, (n, want)
    return t


def _all_gather_small(name, payload):
    rows = payload.shape[0]

    def body(p_ref, out_ref, send_sems, recv_sems, local_sem):
        me = _dev_index()
        mine = pltpu.make_async_copy(p_ref, out_ref.at[me], local_sem)
        mine.start()
        sends = []
        for r in range(1, N_DEV):
            peer = (me + r) % N_DEV
            cp = pltpu.make_async_remote_copy(
                src_ref=p_ref, dst_ref=out_ref.at[me], send_sem=send_sems.at[r - 1], recv_sem=recv_sems.at[r - 1],
                device_id=_dev_coords(peer), device_id_type=MESH)
            cp.start()
            sends.append(cp)
        for r in range(1, N_DEV):
            src = (me + N_DEV - r) % N_DEV
            pltpu.make_async_remote_copy(
                src_ref=p_ref, dst_ref=out_ref.at[src], send_sem=send_sems.at[r - 1], recv_sem=recv_sems.at[r - 1],
                device_id=_dev_coords(src), device_id_type=MESH).wait_recv()
        for cp in sends:
            cp.wait_send()
        mine.wait()

    return pl.pallas_call(
        body, name=name,
        out_shape=jax.ShapeDtypeStruct((N_DEV, rows, LANE), F32),
        in_specs=[pl.BlockSpec(memory_space=pltpu.VMEM)],
        out_specs=pl.BlockSpec(memory_space=pltpu.VMEM),
        scratch_shapes=[pltpu.SemaphoreType.DMA((N_DEV - 1,)), pltpu.SemaphoreType.DMA((N_DEV - 1,)),
                        pltpu.SemaphoreType.DMA],
        compiler_params=pltpu.CompilerParams(vmem_limit_bytes=VMEM_LIMIT),
    )(payload)


def _region(ref, dev, axis, n):
    start = pl.multiple_of(dev * n, LANE if axis == 1 else 16)
    return ref.at[:, pl.ds(start, n)] if axis == 1 else ref.at[pl.ds(start, n), :]


class _Exchange:
    def __init__(self, arrays, out_shapes, sems, start, finish):
        self.arrays, self.out_shapes, self.sems, self.start, self.finish = arrays, out_shapes, sems, start, finish


def _gather_plan(shards, axes):
    n_w = len(shards)
    fulls = []
    for s, ax in zip(shards, axes):
        shp = (s.shape[0], s.shape[1] * N_DEV) if ax == 1 else (s.shape[0] * N_DEV, s.shape[1])
        fulls.append(jax.ShapeDtypeStruct(shp, BF16))
    widths = [s.shape[ax] for s, ax in zip(shards, axes)]

    def places():
        x, y, c = lax.axis_index("x"), lax.axis_index("y"), lax.axis_index("c")
        chips = [(1 - x, y), (x, 1 - y), (1 - x, 1 - y)]
        return (x, y, c), (x, y, 1 - c), chips

    def index(p):
        return p[0] * 4 + p[1] * 2 + p[2]

    def copy(w, k, s_refs, f_refs, sems, block, to, from_shard):
        send_sems, recv_sems, _ = sems
        dst = _region(f_refs[w], index(block), axes[w], widths[w])
        return pltpu.make_async_remote_copy(
            src_ref=s_refs[w] if from_shard else dst, dst_ref=dst,
            send_sem=send_sems.at[w, k], recv_sem=recv_sems.at[w, k], device_id=to, device_id_type=MESH)

    def local(w, s_refs, f_refs, sems, me):
        return pltpu.make_async_copy(s_refs[w], _region(f_refs[w], index(me), axes[w], widths[w]), sems[2].at[w])

    def start(s_refs, f_refs, sems):
        me, sib, chips = places()
        for w in range(n_w):
            local(w, s_refs, f_refs, sems, me).start()
            copy(w, 0, s_refs, f_refs, sems, me, sib, True).start()
            for j, chip in enumerate(chips):
                copy(w, 1 + j, s_refs, f_refs, sems, me, (*chip, me[2]), True).start()

    def finish(s_refs, f_refs, sems):
        me, sib, chips = places()
        for w in range(n_w):
            for j, chip in enumerate(chips):
                copy(w, 1 + j, s_refs, f_refs, sems, (*chip, me[2]), me, True).wait_recv()
                copy(w, 4 + j, s_refs, f_refs, sems, (*chip, me[2]), sib, False).start()
        for w in range(n_w):
            copy(w, 0, s_refs, f_refs, sems, sib, me, True).wait_recv()
            for j, chip in enumerate(chips):
                copy(w, 4 + j, s_refs, f_refs, sems, (*chip, sib[2]), me, False).wait_recv()
        for w in range(n_w):
            for k in range(N_DEV - 1):
                copy(w, k, s_refs, f_refs, sems, me, sib, True).wait_send()
            local(w, s_refs, f_refs, sems, me).wait()

    sems = [pltpu.SemaphoreType.DMA((n_w, N_DEV - 1)), pltpu.SemaphoreType.DMA((n_w, N_DEV - 1)),
            pltpu.SemaphoreType.DMA((n_w,))]
    return _Exchange(list(shards), fulls, sems, start, finish)


def _scatter_plan(grads, axes):
    n_w = len(grads)
    lands = []
    for g, ax in zip(grads, axes):
        shp = (g.shape[0], g.shape[1] // N_DEV) if ax == 1 else (g.shape[0] // N_DEV, g.shape[1])
        lands.append(jax.ShapeDtypeStruct((N_DEV - 1,) + shp, BF16))
    widths = [ld.shape[1 + ax] for ld, ax in zip(lands, axes)]

    def copy(w, r, g_refs, l_refs, sems, block, to):
        return pltpu.make_async_remote_copy(
            src_ref=_region(g_refs[w], block, axes[w], widths[w]), dst_ref=l_refs[w].at[r - 1],
            send_sem=sems[0].at[w * (N_DEV - 1) + r - 1], recv_sem=sems[1].at[w * (N_DEV - 1) + r - 1],
            device_id=_dev_coords(to), device_id_type=MESH)

    def start(g_refs, l_refs, sems):
        me = _dev_index()
        for w in range(n_w):
            for r in range(1, N_DEV):
                owner = (me + r) % N_DEV
                copy(w, r, g_refs, l_refs, sems, owner, owner).start()

    def finish(g_refs, l_refs, sems):
        me = _dev_index()
        for w in range(n_w):
            for r in range(1, N_DEV):
                copy(w, r, g_refs, l_refs, sems, me, (me + N_DEV - r) % N_DEV).wait_recv()
        for w in range(n_w):
            for r in range(1, N_DEV):
                copy(w, r, g_refs, l_refs, sems, me, (me + r) % N_DEV).wait_send()

    sems = [pltpu.SemaphoreType.DMA((n_w * (N_DEV - 1),)), pltpu.SemaphoreType.DMA((n_w * (N_DEV - 1),))]
    return _Exchange(list(grads), lands, sems, start, finish)


def _places():
    x, y, c = lax.axis_index("x"), lax.axis_index("y"), lax.axis_index("c")
    return (x, y, c), (x, y, 1 - c), [(1 - x, y), (x, 1 - y), (1 - x, 1 - y)]


def _place_index(p):
    return p[0] * 4 + p[1] * 2 + p[2]


def _gather_stage_plans(fulls, axes):
    n_w = len(fulls)
    widths = [f.shape[ax] // N_DEV for f, ax in zip(fulls, axes)]
    shapes = [jax.ShapeDtypeStruct(f.shape, f.dtype) for f in fulls]

    def copy(per, w, k, f_refs, sems, block, to):
        part = _region(f_refs[w], _place_index(block), axes[w], widths[w])
        return pltpu.make_async_remote_copy(
            src_ref=part, dst_ref=part, send_sem=sems[0].at[w * per + k], recv_sem=sems[1].at[w * per + k],
            device_id=to, device_id_type=MESH)

    def start1(_, f_refs, sems):
        me, sib, chips = _places()
        for w in range(n_w):
            copy(4, w, 0, f_refs, sems, me, sib).start()
            for j, chip in enumerate(chips):
                copy(4, w, 1 + j, f_refs, sems, me, (*chip, me[2])).start()

    def finish1(_, f_refs, sems):
        me, sib, chips = _places()
        for w in range(n_w):
            copy(4, w, 0, f_refs, sems, sib, me).wait_recv()
            for j, chip in enumerate(chips):
                copy(4, w, 1 + j, f_refs, sems, (*chip, me[2]), me).wait_recv()
        for w in range(n_w):
            for k in range(4):
                copy(4, w, k, f_refs, sems, me, sib).wait_send()

    def start2(_, f_refs, sems):
        me, sib, chips = _places()
        for w in range(n_w):
            for j, chip in enumerate(chips):
                copy(3, w, j, f_refs, sems, (*chip, me[2]), sib).start()

    def finish2(_, f_refs, sems):
        me, sib, chips = _places()
        for w in range(n_w):
            for j, chip in enumerate(chips):
                copy(3, w, j, f_refs, sems, (*chip, sib[2]), me).wait_recv()
        for w in range(n_w):
            for j, chip in enumerate(chips):
                copy(3, w, j, f_refs, sems, (*chip, me[2]), sib).wait_send()

    sems1 = [pltpu.SemaphoreType.DMA((n_w * 4,)), pltpu.SemaphoreType.DMA((n_w * 4,))]
    sems2 = [pltpu.SemaphoreType.DMA((n_w * 3,)), pltpu.SemaphoreType.DMA((n_w * 3,))]
    return _Exchange([], shapes, sems1, start1, finish1), _Exchange([], shapes, sems2, start2, finish2)


def _run_exchange(name, plan):
    n_in, n_out = len(plan.arrays), len(plan.out_shapes)

    def body(*refs):
        ins, outs, sems = refs[:n_in], refs[n_in:n_in + n_out], refs[n_in + n_out:]
        plan.start(ins, outs, sems)
        plan.finish(ins, outs, sems)

    any_spec = pl.BlockSpec(memory_space=pl.ANY)
    return pl.pallas_call(
        body, name=name, out_shape=plan.out_shapes,
        in_specs=[any_spec] * n_in, out_specs=[any_spec] * n_out, scratch_shapes=plan.sems,
    )(*plan.arrays)


_NO_EXCHANGE = _Exchange([], [], [], lambda i, o, s: None, lambda i, o, s: None)


def _direct_gather_plan(fulls, axes):
    n_w = len(fulls)
    widths = [f.shape[ax] // N_DEV for f, ax in zip(fulls, axes)]
    fulls = [jax.ShapeDtypeStruct(f.shape, f.dtype) for f in fulls]

    def copy(w, r, s_refs, f_refs, sems, block, to):
        part = _region(f_refs[w], block, axes[w], widths[w])
        return pltpu.make_async_remote_copy(
            src_ref=part, dst_ref=part,
            send_sem=sems[0].at[w * (N_DEV - 1) + r - 1], recv_sem=sems[1].at[w * (N_DEV - 1) + r - 1],
            device_id=_dev_coords(to), device_id_type=MESH)

    def start(s_refs, f_refs, sems):
        me = _dev_index()
        for w in range(n_w):
            for r in range(1, N_DEV):
                copy(w, r, s_refs, f_refs, sems, me, (me + r) % N_DEV).start()

    def finish(s_refs, f_refs, sems):
        me = _dev_index()
        for w in range(n_w):
            for r in range(1, N_DEV):
                src = (me + N_DEV - r) % N_DEV
                copy(w, r, s_refs, f_refs, sems, src, src).wait_recv()
        for w in range(n_w):
            for r in range(1, N_DEV):
                copy(w, r, s_refs, f_refs, sems, me, (me + r) % N_DEV).wait_send()

    sems = [pltpu.SemaphoreType.DMA((n_w * (N_DEV - 1),)), pltpu.SemaphoreType.DMA((n_w * (N_DEV - 1),))]
    return _Exchange([], fulls, sems, start, finish)


_HBM = pl.BlockSpec(memory_space=pltpu.HBM)
_SEM = pl.BlockSpec(memory_space=pltpu.SEMAPHORE)
_EFFECT = pltpu.SideEffectType.DATAFLOW_SIDE_EFFECTING


def _split_start(name, plan, landing=None):
    n_in, n_out, n_sem = len(plan.arrays), len(plan.out_shapes), len(plan.sems)

    def body(*refs):
        ins, lands = refs[:n_in], refs[n_in:n_in + n_out]
        sems = refs[n_in + n_out:n_in + n_out + n_sem]
        token = refs[-1]
        plan.start(ins, lands, sems)
        token[...] = jnp.zeros_like(token)

    hbm = lambda a: pltpu.HBM(a.shape, a.dtype)
    results = pl.pallas_call(
        body, name=name,
        out_shape=tuple(plan.sems) + tuple(hbm(a) for a in plan.arrays) + tuple(hbm(a) for a in plan.out_shapes)
        + (jax.ShapeDtypeStruct((8, LANE), F32),),
        in_specs=(_HBM,) * (n_in + n_out),
        out_specs=(_SEM,) * n_sem + (_HBM,) * (n_in + n_out) + (pl.BlockSpec(memory_space=pltpu.VMEM),),
        input_output_aliases={i: n_sem + i for i in range(n_in + n_out)},
        compiler_params=pltpu.CompilerParams(has_side_effects=_EFFECT),
    )(*[pltpu.with_memory_space_constraint(a, pltpu.HBM) for a in plan.arrays],
      *[pltpu.with_memory_space_constraint(a, pltpu.HBM)
        for a in (landing if landing is not None else [lax.empty(a.shape, a.dtype) for a in plan.out_shapes])])
    return results[:n_sem], results[n_sem:n_sem + n_in + n_out], results[-1]


def _split_wait(name, plan, sems, thru, after):
    n_in, n_out, n_sem = len(plan.arrays), len(plan.out_shapes), len(plan.sems)

    def body(*refs):
        ins, lands = refs[:n_in], refs[n_in:n_in + n_out]
        sem_refs = refs[n_in + n_out:n_in + n_out + n_sem]
        plan.finish(ins, lands, sem_refs)

    hbm = lambda a: pltpu.HBM(a.shape, a.dtype)
    results = pl.pallas_call(
        body, name=name,
        out_shape=tuple(hbm(a) for a in plan.arrays) + tuple(hbm(a) for a in plan.out_shapes),
        in_specs=(_HBM,) * (n_in + n_out) + (_SEM,) * n_sem + (pl.BlockSpec(memory_space=pl.ANY),),
        out_specs=(_HBM,) * (n_in + n_out),
        input_output_aliases={i: i for i in range(n_in + n_out)},
        compiler_params=pltpu.CompilerParams(has_side_effects=_EFFECT),
    )(*thru, *sems, after)
    return results[:n_in], results[n_in:]


def _cast_into_full(name, me, w, axis):
    r, c = w.shape
    tr = _tile(r, 256)
    if axis == 1:
        shape, place = (r, c * N_DEV), pl.BlockSpec((tr, c), lambda i, me_ref: (i, me_ref[0]))
    else:
        shape, place = (r * N_DEV, c), pl.BlockSpec((tr, c), lambda i, me_ref: (me_ref[0] * (r // tr) + i, 0))

    def body(me_ref, w_ref, o_ref):
        o_ref[...] = w_ref[...].astype(BF16)

    grid_spec = pltpu.PrefetchScalarGridSpec(
        num_scalar_prefetch=1, grid=(r // tr,),
        in_specs=[pl.BlockSpec((tr, c), lambda i, me_ref: (i, 0))], out_specs=place)
    return pl.pallas_call(body, name=name, grid_spec=grid_spec, out_shape=jax.ShapeDtypeStruct(shape, BF16),
                          compiler_params=_params(1))(me, w)


def _mm(name, a, b, dims, m, n, k, tm, tn, tk, extras, outs, epilogue, row_chunk=None, exchange=None,
        b_col_block=0):
    ni, nj, nk = m // tm, n // tn, k // tk
    ne, no = len(extras), len(outs)
    xin = len(exchange.arrays) if exchange else 0
    xout = len(exchange.out_shapes) if exchange else 0
    if dims == _TN:
        a_spec = pl.BlockSpec((tk, tm), lambda i, j, kk: (kk, i))
    else:
        a_spec = pl.BlockSpec((tm, tk), lambda i, j, kk: (i, kk))
    if dims == _NT:
        b_spec = pl.BlockSpec((tn, tk), lambda i, j, kk: (j, kk))
    else:
        b_spec = pl.BlockSpec((tk, tn), lambda i, j, kk: (kk, j + b_col_block))
    chunks = [slice(None)] if row_chunk is None else [slice(r, r + row_chunk) for r in range(0, tm, row_chunk)]

    def lift(index_map):
        return lambda i, j, kk: index_map(i, j)

    def body(a_ref, b_ref, *rest):
        extra_refs, rest = rest[:ne], rest[ne:]
        xin_refs, rest = rest[:xin], rest[xin:]
        out_refs, rest = rest[:no], rest[no:]
        xout_refs, rest = rest[:xout], rest[xout:]
        i, j, kk = pl.program_id(0), pl.program_id(1), pl.program_id(2)
        if exchange:
            sem_refs = rest[1:] if nk > 1 else rest

            @pl.when((i == 0) & (j == 0) & (kk == 0))
            def _():
                exchange.start(xin_refs, xout_refs, sem_refs)

        if nk == 1:
            part = _dot(a_ref[...], b_ref[...], dims)
            for rows in chunks:
                epilogue(part[rows], i, j, extra_refs, out_refs, rows)
        else:
            acc_ref = rest[0]

            @pl.when(kk == 0)
            def _():
                acc_ref[...] = _dot(a_ref[...], b_ref[...], dims)

            @pl.when(kk > 0)
            def _():
                acc_ref[...] += _dot(a_ref[...], b_ref[...], dims)

            @pl.when(kk == nk - 1)
            def _():
                for rows in chunks:
                    epilogue(acc_ref[rows, :], i, j, extra_refs, out_refs, rows)

        if exchange:
            @pl.when((i == ni - 1) & (j == nj - 1) & (kk == nk - 1))
            def _():
                exchange.finish(xin_refs, xout_refs, sem_refs)

    any_spec = pl.BlockSpec(memory_space=pl.ANY)
    once = dict(pipeline_mode=pl.Buffered(1)) if (row_chunk is not None and nk > 1) else {}
    results = pl.pallas_call(
        body, name=name,
        grid=(ni, nj, nk),
        in_specs=[a_spec, b_spec] + [pl.BlockSpec(bs, lift(im), **once) for _, bs, im in extras] + [any_spec] * xin,
        out_specs=[pl.BlockSpec(bs, lift(im), **once) for _, bs, im in outs] + [any_spec] * xout,
        out_shape=[sd for sd, _, _ in outs] + (list(exchange.out_shapes) if exchange else []),
        scratch_shapes=([pltpu.VMEM((tm, tn), F32)] if nk > 1 else []) + (list(exchange.sems) if exchange else []),
        compiler_params=_params(3),
    )(a, b, *[arr for arr, _, _ in extras], *(exchange.arrays if exchange else []))
    return (results[:no], results[no:]) if exchange else results


class _LaggedSums:
    def __init__(self, valid):
        self.valid = valid

    def add(self, sums_ref, row, contribution):
        sums_ref[row:row + 1, :] += jnp.where(self.valid, contribution, 0.0)


def _mm_rows(name, a, b, dims, m, n, k, tm, tk, extras, outs, epilogue):
    ni, nk = m // tm, k // tk
    rc = tm // nk
    pieces = 4 if (rc % 32 == 0 and n % (4 * LANE) == 0) else 1
    assert tm == nk * rc and rc % 8 == 0 and dims in (_NN, _NT)
    ne, no = len(extras), len(outs)
    a_spec = pl.BlockSpec((tm, tk), lambda i, kk: (jnp.minimum(i, ni - 1), kk))
    b_spec = pl.BlockSpec((n, tk), lambda i, kk: (0, kk)) if dims == _NT else pl.BlockSpec((tk, n), lambda i, kk: (kk, 0))

    def by_rows(arr_shape, bs):
        return bs[0] == tm and arr_shape[0] == m

    def spec(arr_shape, bs, im):
        if by_rows(arr_shape, bs):
            return pl.BlockSpec(bs, lambda i, kk: im(jnp.maximum(i - 1, 0), 0), pipeline_mode=pl.Buffered(1))
        return pl.BlockSpec(bs, lambda i, kk: im(0, 0))

    sums_like = [o for o, (sd, bs, _) in enumerate(outs) if not by_rows(sd.shape, bs)]

    def body(a_ref, b_ref, *rest):
        extra_refs, out_refs, acc2 = rest[:ne], rest[ne:ne + no], rest[-1]
        i, kk = pl.program_id(0), pl.program_id(1)
        cur, prev = i % 2, (i + 1) % 2
        rows = pl.ds(pl.multiple_of(kk * rc, rc), rc)
        sums = _LaggedSums(i >= 1)

        @pl.when((i == 0) & (kk == 0))
        def _():
            acc2[1] = jnp.zeros((tm, n), F32)
            for o in sums_like:
                out_refs[o][...] = jnp.zeros_like(out_refs[o])

        def finish_rows(piece):
            sub = pl.ds(pl.multiple_of(kk * rc + piece * (rc // pieces), rc // pieces), rc // pieces)
            epilogue(acc2[prev, sub, :], sums, extra_refs, out_refs, sub)

        def step(first_k):
            for piece in range(pieces):
                cols = slice(piece * (n // pieces), (piece + 1) * (n // pieces))
                part = _dot(a_ref[...], b_ref[cols, :] if dims == _NT else b_ref[:, cols], dims)
                if first_k:
                    acc2[cur, :, cols] = part
                else:
                    acc2[cur, :, cols] += part
                finish_rows(piece)

        @pl.when((i < ni) & (kk == 0))
        def _():
            step(True)

        @pl.when((i < ni) & (kk > 0))
        def _():
            step(False)

        @pl.when(i == ni)
        def _():
            for piece in range(pieces):
                finish_rows(piece)

    return pl.pallas_call(
        body, name=name, grid=(ni + 1, nk),
        in_specs=[a_spec, b_spec] + [spec(arr.shape, bs, im) for arr, bs, im in extras],
        out_specs=[spec(sd.shape, bs, im) for sd, bs, im in outs],
        out_shape=[sd for sd, _, _ in outs],
        scratch_shapes=[pltpu.VMEM((2, tm, n), F32)],
        compiler_params=_params(2),
    )(a, b, *[arr for arr, _, _ in extras])


def _after(token):
    return [(token, (8, LANE), lambda i, j: (0, 0))]


def _grad_w(name, a, dc, token=None, tm=512, tn=1024, cols=None):
    t, m = a.shape
    first, n = cols if cols is not None else (0, dc.shape[1])
    tm, tn = _tile(m, tm), _tile(n, tn)
    assert first % tn == 0

    def epilogue(acc, i, j, extra_refs, out_refs, rows):
        out_refs[0][...] = acc
        out_refs[1][...] = acc.astype(BF16)

    blk = ((tm, tn), lambda i, j: (i, j))
    return _mm(name, a, dc, _TN, m, n, t, tm, tn, t, _after(token) if token is not None else [],
               [(jax.ShapeDtypeStruct((m, n), F32),) + blk, (jax.ShapeDtypeStruct((m, n), BF16),) + blk], epilogue,
               b_col_block=first // tn)


def _proj_gather(a1, w_shard, order):
    t, d = a1.shape
    nsh = w_shard.shape[1]
    tm = _tile(t, 512)
    n_i = t // tm

    def body(ord_ref, a_ref, wsh_ref, proj_ref, full_ref, bbuf, bsem, send_sems, recv_sems, own_sem):
        s, i = pl.program_id(0), pl.program_id(1)
        me, sib, chips = _places()
        near, far = chips[:2], chips[2]
        steps = ([(me, None, None), (sib, 0, None)]
                 + [((*ch, me[2]), 1 + j, 4 + j) for j, ch in enumerate(near)]
                 + [((*ch, sib[2]), 4 + j, None) for j, ch in enumerate(near)]
                 + [((*far, me[2]), 3, 6), ((*far, sib[2]), 6, None)])
        blocks = [st[0] for st in steps]

        def part(block):
            return _region(full_ref, _place_index(block), 1, nsh)

        def remote(k, block, to, from_shard=False):
            return pltpu.make_async_remote_copy(
                src_ref=wsh_ref if from_shard else part(block), dst_ref=part(block),
                send_sem=send_sems.at[k], recv_sem=recv_sems.at[k], device_id=to, device_id_type=MESH)

        def load(pos):
            src = wsh_ref if pos == 0 else part(blocks[pos])
            return pltpu.make_async_copy(src, bbuf.at[pos % 2], bsem.at[pos % 2])

        own = pltpu.make_async_copy(wsh_ref, part(me), own_sem)

        @pl.when((s == 0) & (i == 0))
        def _():
            own.start()
            remote(0, me, sib, True).start()
            for j, ch in enumerate(chips):
                remote(1 + j, me, (*ch, me[2]), True).start()
            load(0).start()
            load(0).wait()

        for pos in range(1, N_DEV):
            @pl.when((s == pos) & (i == 0))
            def _():
                load(pos).wait()

        for pos in range(N_DEV - 1):
            @pl.when((s == pos) & (i == n_i - 1))
            def _():
                nxt = pos + 1
                block, arrives_on, pass_on_with = steps[nxt]
                remote(arrives_on, block, me).wait_recv()
                if pass_on_with is not None:
                    remote(pass_on_with, block, sib).start()
                load(nxt).start()

        proj_ref[...] = _dot(a_ref[...], bbuf[s % 2])

        @pl.when((s == N_DEV - 1) & (i == n_i - 1))
        def _():
            for k in range(N_DEV - 1):
                remote(k, me, sib, True).wait_send()
            own.wait()

    grid_spec = pltpu.PrefetchScalarGridSpec(
        num_scalar_prefetch=1, grid=(N_DEV, n_i),
        in_specs=[pl.BlockSpec((tm, d), lambda s, i, ord_ref: (i, 0)), pl.BlockSpec(memory_space=pl.ANY)],
        out_specs=[pl.BlockSpec((tm, nsh), lambda s, i, ord_ref: (i, ord_ref[s])), pl.BlockSpec(memory_space=pl.ANY)],
        scratch_shapes=[pltpu.VMEM((2, d, nsh), BF16), pltpu.SemaphoreType.DMA((2,)),
                        pltpu.SemaphoreType.DMA((N_DEV - 1,)), pltpu.SemaphoreType.DMA((N_DEV - 1,)),
                        pltpu.SemaphoreType.DMA])
    return pl.pallas_call(
        body, name="proj_gather", grid_spec=grid_spec,
        out_shape=[jax.ShapeDtypeStruct((t, nsh * N_DEV), F32), jax.ShapeDtypeStruct((d, nsh * N_DEV), BF16)],
        compiler_params=_params(2),
    )(order, a1, w_shard)


def _cast_bf16(name, w):
    r, c = w.shape
    tr = _tile(r, 256)
    return pl.pallas_call(
        lambda w_ref, o_ref: o_ref.__setitem__(Ellipsis, w_ref[...].astype(BF16)), name=name,
        grid=(r // tr,), in_specs=[pl.BlockSpec((tr, c), lambda i: (i, 0))],
        out_specs=pl.BlockSpec((tr, c), lambda i: (i, 0)), out_shape=jax.ShapeDtypeStruct((r, c), BF16),
        compiler_params=_params(1),
    )(w)


def _prep_small(c_row, lb_logits):
    d = c_row.shape[1]
    rows = d // LANE

    def body(c_ref, l_ref, o_ref):
        cv = c_ref[...]
        o_ref[0:rows, :] = cv * _sigmoid(cv)
        lbs = [_sigmoid(l_ref[dr][0:1, :] - l_ref[dr][1:2, :]) for dr in range(2)]
        o_ref[rows:rows + 8, :] = jnp.concatenate(lbs + [jnp.zeros((6, LANE), F32)], axis=0)

    return pl.pallas_call(
        body, name="prep_small", out_shape=jax.ShapeDtypeStruct((rows + 8, LANE), F32),
    )(c_row.reshape(rows, LANE), lb_logits)


def _mod_shard(sc_all, w_ada_shard, b_shard):
    d, n = w_ada_shard.shape
    tn = _tile(n, 512)

    def body(s_ref, w_ref, b_ref, o_ref):
        o_ref[...] = _dot(s_ref[...], w_ref[...], precision=HIGHEST) + b_ref[...]

    return pl.pallas_call(
        body, name="mod_shard", grid=(n // tn,),
        in_specs=[pl.BlockSpec((N_DEV, d), lambda j: (0, 0)), pl.BlockSpec((d, tn), lambda j: (0, j)),
                  pl.BlockSpec((1, tn), lambda j: (0, j))],
        out_specs=pl.BlockSpec((N_DEV, tn), lambda j: (0, j)),
        out_shape=jax.ShapeDtypeStruct((N_DEV, n), F32), compiler_params=_params(1),
    )(sc_all, w_ada_shard, b_shard)


def _norm_mod(x, gain, shift, scale):
    t, d = x.shape
    tm = _tile(t, 512)

    def body(x_ref, g_ref, sh_ref, sc_ref, o_ref):
        xv = x_ref[...]
        o_ref[...] = ((xv * _rms(xv) * g_ref[...]) * (1.0 + sc_ref[...]) + sh_ref[...]).astype(BF16)

    vec = pl.BlockSpec((1, d), lambda i: (0, 0))
    return pl.pallas_call(
        body, name="norm_mod", grid=(t // tm,),
        in_specs=[pl.BlockSpec((tm, d), lambda i: (i, 0)), vec, vec, vec],
        out_specs=pl.BlockSpec((tm, d), lambda i: (i, 0)), out_shape=jax.ShapeDtypeStruct((t, d), BF16),
        compiler_params=_params(1),
    )(x, gain, shift, scale)


def _chunk_masks():
    row = lax.broadcasted_iota(jnp.int32, (HEAD, HEAD), 0)
    col = lax.broadcasted_iota(jnp.int32, (HEAD, HEAD), 1)
    same = (row // A_CHUNK) == (col // A_CHUNK)
    return same & (col <= row), same & (col >= row)


def _ones(mask):
    return jnp.where(mask, 1.0, 0.0).astype(BF16)


def _dot_split(ones_bf16, x):
    hi = x.astype(BF16)
    lo = (x - hi.astype(F32)).astype(BF16)
    return _dot(ones_bf16, hi) + _dot(ones_bf16, lo)


def _hgrn_block(direction, f, lb, cum2):
    sf = _sigmoid(f)
    big_f = lb + (1.0 - lb) * sf
    k = (1.0 - lb) * (1.0 - sf)
    lf = jnp.log(big_f)
    both = _dot_split(cum2, lf)
    cf, cr = both[:HEAD], both[HEAD:]
    b, rest = (cf, cr - lf) if direction == 0 else (cr, cf - lf)
    return k, sf, big_f, jnp.exp(b), jnp.exp(-b), jnp.exp(rest)


def _hgrn_fwd(proj, lb, g_norm, width, exchange, after):
    t = proj.shape[0]
    heads = width // HEAD
    nb, nc = t // HEAD, t // A_CHUNK
    ua = 4 if nb % 4 == 0 else (2 if nb % 2 == 0 else 1)
    ub = 16 if nc % 16 == 0 else (8 if nc % 8 == 0 else 4)
    q_scale = HEAD ** -0.5
    xin, xout = len(exchange.arrays), len(exchange.out_shapes)

    def body(q_ref, ffw_ref, fbw_ref, v_ref, og_ref, lb_ref, g_ref, *rest):
        xin_refs, rest = rest[:xin], rest[xin + len(after):]
        outa_ref, osum_ref = rest[:2]
        xout_refs, rest = rest[2:2 + xout], rest[2 + xout:]
        qd_s, ke_s, dc_s, o_s = rest[:4]
        sem_refs = rest[4:]
        h = pl.program_id(0)

        @pl.when(h == 0)
        def _():
            exchange.start(xin_refs, xout_refs, sem_refs)

        tril, triu = _chunk_masks()
        cum2 = jnp.concatenate([_ones(tril), _ones(triu)], axis=0)
        f_refs = (ffw_ref, fbw_ref)
        lbs = (lb_ref[0:1, :], lb_ref[1:2, :])

        def phase_a(it, carry):
            loaded = []
            for u in range(ua):
                rows = pl.ds(pl.multiple_of((it * ua + u) * HEAD, HEAD), HEAD)
                loaded.append((rows, q_ref[rows, :], v_ref[rows, :], ffw_ref[rows, :], fbw_ref[rows, :]))
            chains = [(d, rows, qv * q_scale, vv.astype(BF16), fv)
                      for rows, qv, vv, f0, f1 in loaded for d, fv in ((0, f0), (1, f1))]
            blocks = [_hgrn_block(d, fv, lbs[d], cum2) for d, _, _, _, fv in chains]
            scaled = [(qv * eb, k * enb, k * erest, eb * erest)
                      for (_, _, qv, _, _), (k, _, _, eb, enb, erest) in zip(chains, blocks)]
            atts = [jnp.where(tril if d == 0 else triu, _bdot(qd, kd, _NT), 0.0)
                    for (d, _, _, _, _), (qd, kd, _, _) in zip(chains, scaled)]
            intras = [_bdot(att, vv) for att, (_, _, _, vv, _) in zip(atts, chains)]
            results = [(d, rows, o_intra, qd.astype(BF16), ke.astype(BF16), decay)
                       for (d, rows, _, _, _), (qd, _, ke, decay), o_intra in zip(chains, scaled, intras)]
            for d, rows, o_intra, qd16, ke16, decay in results:
                o_s[d, rows, :] = o_intra
                qd_s[d, rows, :] = qd16
                ke_s[d, rows, :] = ke16
                dc_s[d, rows, :] = decay
            return carry

        lax.fori_loop(0, nb // ua, phase_a, 0)

        def phase_b(it, states):
            loaded = []
            for u in range(ub):
                n = it * ub + u
                for d in range(2):
                    c = n if d == 0 else nc - 1 - n
                    start = pl.multiple_of(c * A_CHUNK, A_CHUNK)
                    rows = pl.ds(start, A_CHUNK)
                    loaded.append((d, rows, qd_s[d, rows, :], ke_s[d, rows, :], v_ref[rows, :],
                                   dc_s[d, pl.ds(start, 1), :], o_s[d, rows, :]))
            increments = [_dot(vv.astype(BF16), ke16, _TN) for _, _, _, ke16, vv, _, _ in loaded]
            states = list(states)
            befores = []
            for (d, _, _, _, _, decay, _), inc in zip(loaded, increments):
                befores.append(states[d].astype(BF16))
                states[d] = states[d] * decay + inc
            inters = [_dot(qd16, before, _NT) for (_, _, qd16, _, _, _, _), before in zip(loaded, befores)]
            for (d, rows, _, _, _, _, o_intra), o_inter in zip(loaded, inters):
                o_s[d, rows, :] = o_intra + o_inter
            return tuple(states)

        zero_state = jnp.zeros((HEAD, HEAD), F32)
        lax.fori_loop(0, nc // ub, phase_b, (zero_state, zero_state))

        def phase_c(i, carry):
            rows = pl.ds(pl.multiple_of(i * HEAD, HEAD), HEAD)
            o = o_s[0, rows, :] + o_s[1, rows, :]
            osum_ref[rows, :] = o
            og = og_ref[rows, :]
            outa_ref[rows, :] = (o * _rms(o) * g_ref[...] * (og * _sigmoid(og))).astype(BF16)
            return carry

        lax.fori_loop(0, nb, phase_c, 0)

        @pl.when(h == heads - 1)
        def _():
            exchange.finish(xin_refs, xout_refs, sem_refs)

    def col(p):
        return pl.BlockSpec((t, HEAD), lambda h: (0, p * heads + h))

    any_spec = pl.BlockSpec(memory_space=pl.ANY)
    results = pl.pallas_call(
        body, name="hgrn_fwd", grid=(heads,),
        in_specs=[col(0), col(1), col(2), col(3), col(4),
                  pl.BlockSpec((2, HEAD), lambda h: (0, h)), pl.BlockSpec((1, HEAD), lambda h: (0, 0))]
        + [any_spec] * (xin + len(after)),
        out_specs=[pl.BlockSpec((t, HEAD), lambda h: (0, h)), pl.BlockSpec((t, HEAD), lambda h: (0, h))] + [any_spec] * xout,
        out_shape=[jax.ShapeDtypeStruct((t, width), BF16), jax.ShapeDtypeStruct((t, width), F32)] + list(exchange.out_shapes),
        scratch_shapes=[pltpu.VMEM((2, t, HEAD), BF16), pltpu.VMEM((2, t, HEAD), BF16), pltpu.VMEM((2, t, HEAD), F32),
                        pltpu.VMEM((2, t, HEAD), F32)] + list(exchange.sems),
        compiler_params=_params(1),
    )(proj, proj, proj, proj, proj, lb, g_norm, *exchange.arrays, *after)
    return results[0], results[1], results[2:]


def _sgu_core(u_pre, v_pre, g_v, ws_ref, bst):
    u, du = _gelu_and_grad(u_pre)
    v, dv = _gelu_and_grad(v_pre)
    mu = jnp.mean(v, axis=-1, keepdims=True)
    dlt = v - mu
    rstd = lax.rsqrt(jnp.mean(dlt * dlt, axis=-1, keepdims=True) + EPS)
    vhat = dlt * rstd
    vn = vhat * g_v
    groups = vn.shape[1] // HEAD
    cols = []
    for g in range(groups):
        vm_g = _bdot(ws_ref[g], vn[:, g * HEAD:(g + 1) * HEAD]) + bst[:, g:g + 1]
        cols.append(vm_g)
    return u, du, dv, vhat, rstd, vn, jnp.concatenate(cols, axis=1)


def _sgu_fwd(proj, g_v, w_s, bst, width, z_block):
    t = proj.shape[0]

    def body(u_ref, v_ref, g_ref, ws_ref, bst_ref, o_ref):
        u, _, _, _, _, _, vm = _sgu_core(u_ref[...], v_ref[...], g_ref[...], ws_ref, bst_ref[...])
        o_ref[...] = (u * vm).astype(BF16)

    groups = width // HEAD
    return pl.pallas_call(
        body, name="sgu_fwd", grid=(t // HEAD,),
        in_specs=[pl.BlockSpec((HEAD, width), lambda i: (i, z_block)), pl.BlockSpec((HEAD, width), lambda i: (i, z_block + 1)),
                  pl.BlockSpec((1, width), lambda i: (0, 0)), pl.BlockSpec((groups, HEAD, HEAD), lambda i: (0, 0, 0)),
                  pl.BlockSpec((HEAD, groups), lambda i: (0, 0))],
        out_specs=pl.BlockSpec((HEAD, width), lambda i: (i, 0)),
        out_shape=jax.ShapeDtypeStruct((t, width), BF16), compiler_params=_params(1),
    )(proj, proj, g_v, w_s, bst)


def _sgu_bwd(proj, dout_b, dproj, g_v, w_s, w_st, bst, width, z_block):
    t = proj.shape[0]
    groups = width // HEAD
    nblk = t // HEAD

    def body(u_ref, v_ref, do_ref, g_ref, ws_ref, wst_ref, bst_ref, dproj_hbm,
             dz_ref, dg_ref, dws_ref, dbst_ref, res_s):
        i, p = pl.program_id(0), pl.program_id(1)

        @pl.when((i == 0) & (p == 0))
        def _():
            dg_ref[...] = jnp.zeros_like(dg_ref)
            dws_ref[...] = jnp.zeros_like(dws_ref)
            dbst_ref[...] = jnp.zeros_like(dbst_ref)

        @pl.when(p == 0)
        def _():
            g_v = g_ref[...]
            u, du, dv, vhat, rstd, vn, vm = _sgu_core(u_ref[...], v_ref[...], g_v, ws_ref, bst_ref[...])
            dout = do_ref[...].astype(F32)
            res_s[0] = (dout * vm * du).astype(BF16)
            dvm = dout * u
            dvn_cols = []
            for g in range(groups):
                sl = slice(g * HEAD, (g + 1) * HEAD)
                dvm_g = dvm[:, sl]
                dbst_ref[:, g:g + 1] += jnp.sum(dvm_g, axis=1, keepdims=True)
                dws_ref[g] += _bdot(dvm_g, vn[:, sl], _NT)
                dvn_cols.append(_bdot(wst_ref[g], dvm_g))
            dvn = jnp.concatenate(dvn_cols, axis=1)
            dg_ref[...] += _colsum(dvn * vhat)
            dvh = dvn * g_v
            dvg = rstd * (dvh - jnp.mean(dvh, axis=-1, keepdims=True)
                          - vhat * jnp.mean(dvh * vhat, axis=-1, keepdims=True))
            res_s[1] = (dvg * dv).astype(BF16)

        dz_ref[...] = res_s[p]

    n_in = dproj.shape[1]
    return pl.pallas_call(
        body, name="sgu_bwd", grid=(nblk, 2),
        in_specs=[pl.BlockSpec((HEAD, width), lambda i, p: (i, z_block)),
                  pl.BlockSpec((HEAD, width), lambda i, p: (i, z_block + 1)),
                  pl.BlockSpec((HEAD, width), lambda i, p: (i, 0)),
                  pl.BlockSpec((1, width), lambda i, p: (0, 0)),
                  pl.BlockSpec((groups, HEAD, HEAD), lambda i, p: (0, 0, 0)),
                  pl.BlockSpec((groups, HEAD, HEAD), lambda i, p: (0, 0, 0)),
                  pl.BlockSpec((HEAD, groups), lambda i, p: (0, 0)),
                  pl.BlockSpec(memory_space=pl.ANY)],
        out_specs=[pl.BlockSpec((HEAD, width), lambda i, p: (i, z_block + p)),
                   pl.BlockSpec((1, width), lambda i, p: (0, 0)),
                   pl.BlockSpec((groups, HEAD, HEAD), lambda i, p: (0, 0, 0)),
                   pl.BlockSpec((HEAD, groups), lambda i, p: (0, 0))],
        out_shape=[jax.ShapeDtypeStruct((t, n_in), BF16), jax.ShapeDtypeStruct((1, width), F32),
                   jax.ShapeDtypeStruct((groups, HEAD, HEAD), F32), jax.ShapeDtypeStruct((HEAD, groups), F32)],
        scratch_shapes=[pltpu.VMEM((2, HEAD, width), BF16)],
        input_output_aliases={7: 0},
        compiler_params=_params(2),
    )(proj, proj, dout_b, g_v, w_s, w_st, bst, dproj)


def _hgrn_bwd(proj, osum, dout_a, dproj, lb, g_norm, width, exchange, after):
    t = proj.shape[0]
    heads = width // HEAD
    nb = t // HEAD
    cpb = HEAD // A_CHUNK
    ubk = 2 if nb % 2 == 0 else 1
    q_scale = HEAD ** -0.5
    xin, xout = len(exchange.arrays), len(exchange.out_shapes)

    def body(q_ref, ffw_ref, fbw_ref, v_ref, og_ref, osum_ref, douta_ref, lb_ref, g_ref, dproj_hbm, *rest):
        xin_refs, rest = rest[:xin], rest[xin + 1:]
        out_ref, dgh_ref, dlb_ref = rest[:3]
        xout_refs, rest = rest[3:3 + xout], rest[3 + xout:]
        do_s, dq_s, dv_s, res_s, ck_s = rest[:5]
        sem_refs = rest[5:]
        h, p = pl.program_id(0), pl.program_id(1)
        f_refs = (ffw_ref, fbw_ref)

        @pl.when((h == 0) & (p == 0))
        def _():
            exchange.start(xin_refs, xout_refs, sem_refs)

        @pl.when(p == 0)
        def _():
            tril, triu = _chunk_masks()
            cum2 = jnp.concatenate([_ones(tril), _ones(triu)], axis=0)
            g_row = g_ref[...]

            def pass_norm(i, dgh):
                rows = pl.ds(pl.multiple_of(i * HEAD, HEAD), HEAD)
                o = osum_ref[rows, :]
                r = _rms(o)
                oh = o * r
                og = og_ref[rows, :]
                sg = _sigmoid(og)
                dout = douta_ref[rows, :].astype(F32)
                don = dout * (og * sg)
                res_s[4, rows, :] = (dout * (oh * g_row) * (sg * (1.0 + og * (1.0 - sg)))).astype(BF16)
                doh = don * g_row
                do_s[rows, :] = r * (doh - oh * jnp.mean(doh * oh, axis=-1, keepdims=True))
                return dgh + _colsum(don * oh)

            dgh_ref[...] = lax.fori_loop(0, nb, pass_norm, jnp.zeros((1, HEAD), F32))

            lbs = (lb_ref[0:1, :], lb_ref[1:2, :])
            zero_state = jnp.zeros((HEAD, HEAD), F32)

            def chunk_order(d):
                return list(range(cpb)) if d == 0 else list(range(cpb - 1, -1, -1))

            def chunk(x, j):
                return x[j * A_CHUNK:(j + 1) * A_CHUNK, :]

            def decay_row(e_big, j):
                return e_big[j * A_CHUNK:j * A_CHUNK + 1, :]

            def cat(parts):
                return jnp.concatenate([parts[j] for j in range(cpb)], axis=0)

            def block_states(d, start, incs, e_big):
                befores, st = {}, start
                for j in chunk_order(d):
                    befores[j] = st
                    st = st * decay_row(e_big, j) + incs[j]
                return befores, st

            def pass_states(it, states):
                loaded = []
                for u in range(ubk):
                    for d in range(2):
                        blk = it * ubk + u if d == 0 else nb - 1 - (it * ubk + u)
                        rows = pl.ds(pl.multiple_of(blk * HEAD, HEAD), HEAD)
                        loaded.append((d, blk, f_refs[d][rows, :], v_ref[rows, :]))
                blocks = [_hgrn_block(d, fv, lbs[d], cum2) for d, _, fv, _ in loaded]
                incs = [{j: _bdot(chunk(vv, j), chunk(k * erest, j), _TN) for j in range(cpb)}
                        for (_, _, _, vv), (k, _, _, _, _, erest) in zip(loaded, blocks)]
                states, starts = list(states), []
                for (d, _, _, _), (_, _, _, eb, _, erest), inc in zip(loaded, blocks, incs):
                    starts.append(states[d])
                    states[d] = block_states(d, states[d], inc, eb * erest)[1]
                for (d, blk, _, _), start in zip(loaded, starts):
                    ck_s[d, blk] = start
                return tuple(states)

            lax.fori_loop(0, nb // ubk, pass_states, (zero_state, zero_state))

            def pass_back(it, carry):
                gts, dlb = [carry[0], carry[1]], carry[2]
                loaded = []
                for u, d in ((u, d) for u in range(ubk) for d in range(2)):
                    blk = nb - 1 - (it * ubk + u) if d == 0 else it * ubk + u
                    rows = pl.ds(pl.multiple_of(blk * HEAD, HEAD), HEAD)
                    loaded.append((d, rows, f_refs[d][rows, :], q_ref[rows, :], v_ref[rows, :], do_s[rows, :], ck_s[d, blk]))
                blocks = [_hgrn_block(d, fv, lbs[d], cum2) for d, _, fv, _, _, _, _ in loaded]
                scaled = []
                for (_, _, _, qv, _, _, _), (k, _, _, eb, enb, erest) in zip(loaded, blocks):
                    qh = qv * q_scale
                    scaled.append((qh, qh * eb, k * enb, k * erest, eb * erest))
                masks = [tril if d == 0 else triu for d, *_ in loaded]
                atts = [jnp.where(m, _bdot(qd, kd, _NT), 0.0) for m, (_, qd, kd, _, _) in zip(masks, scaled)]
                datts = [jnp.where(m, _bdot(do, vv, _NT), 0.0) for m, (_, _, _, _, vv, do, _) in zip(masks, loaded)]
                dvs = [_bdot(att, do, _TN) for att, (_, _, _, _, _, do, _) in zip(atts, loaded)]
                dqds = [_bdot(datt, kd) for datt, (_, _, kd, _, _) in zip(datts, scaled)]
                dkds = [_bdot(datt, qd, _TN) for datt, (_, qd, _, _, _) in zip(datts, scaled)]
                s_incs = [{j: _bdot(chunk(vv, j), chunk(ke, j), _TN) for j in range(cpb)}
                          for (_, _, _, _, vv, _, _), (_, _, _, ke, _) in zip(loaded, scaled)]
                g_incs = [{j: _bdot(chunk(do, j), chunk(qd, j), _TN) for j in range(cpb)}
                          for (_, _, _, _, _, do, _), (_, qd, _, _, _) in zip(loaded, scaled)]
                befores, afters, g_at = [], [], []
                for (d, _, _, _, _, _, ck), (_, _, _, _, e_big), s_inc, g_inc in zip(loaded, scaled, s_incs, g_incs):
                    order = chunk_order(d)
                    before, after = block_states(d, ck, s_inc, e_big)
                    befores.append(before)
                    afters.append({j: (before[order[n + 1]] if n + 1 < cpb else after) for n, j in enumerate(order)})
                    at, gt = {}, gts[d]
                    for j in reversed(order):
                        at[j] = gt
                        gt = gt * decay_row(e_big, j) + g_inc[j]
                    gts[d] = gt
                    g_at.append(at)
                dqd_i = [{j: _bdot(chunk(do, j), before[j]) for j in range(cpb)}
                         for (_, _, _, _, _, do, _), before in zip(loaded, befores)]
                dv_i = [{j: _bdot(chunk(ke, j), at[j], _NT) for j in range(cpb)}
                        for (_, _, _, ke, _), at in zip(scaled, g_at)]
                dke = [{j: _bdot(chunk(vv, j), at[j]) for j in range(cpb)}
                       for (_, _, _, _, vv, _, _), at in zip(loaded, g_at)]
                results, new = [], []
                for n, ((d, rows, _, _, _, _, _), (k, sf, big_f, eb, enb, erest), (qh, _, _, _, _)) in enumerate(
                        zip(loaded, blocks, scaled)):
                    dqh = (dqds[n] + cat(dqd_i[n])) * eb
                    dk = dkds[n] * enb + cat(dke[n]) * erest
                    carry_rows = {j: jnp.broadcast_to(_colsum(g_at[n][j] * afters[n][j]), (A_CHUNK, HEAD))
                                  for j in range(cpb)}
                    dlf = _dot_split(_ones(triu if d == 0 else tril), qh * dqh - k * dk) + cat(carry_rows)
                    common = dlf / big_f - dk
                    results.append((d, rows, (k * sf * common).astype(BF16), dqh.astype(BF16),
                                    (dvs[n] + cat(dv_i[n])).astype(BF16)))
                    new.append(_colsum((1.0 - sf) * common))
                for d, rows, df16, dq16, dv16 in results:
                    res_s[1 + d, rows, :] = df16
                    dq_s[d, rows, :] = dq16
                    dv_s[d, rows, :] = dv16
                per_dir = [sum(c for (d, *_), c in zip(loaded, new) if d == dd) for dd in range(2)]
                return gts[0], gts[1], dlb + jnp.concatenate(per_dir, axis=0)

            dlb_ref[...] = lax.fori_loop(0, nb // ubk, pass_back,
                                         (zero_state, zero_state, jnp.zeros((2, HEAD), F32)))[2]

            def pass_out(i, carry):
                rows = pl.ds(pl.multiple_of(i * HEAD, HEAD), HEAD)
                dq = dq_s[0, rows, :].astype(F32) + dq_s[1, rows, :].astype(F32)
                res_s[0, rows, :] = (dq * q_scale).astype(BF16)
                res_s[3, rows, :] = (dv_s[0, rows, :].astype(F32) + dv_s[1, rows, :].astype(F32)).astype(BF16)
                return carry

            lax.fori_loop(0, nb, pass_out, 0)

        out_ref[...] = res_s[p]

        @pl.when((h == heads - 1) & (p == 4))
        def _():
            exchange.finish(xin_refs, xout_refs, sem_refs)

    def col(pp):
        return pl.BlockSpec((t, HEAD), lambda h, p: (0, pp * heads + h))

    n_in = dproj.shape[1]
    any_spec = pl.BlockSpec(memory_space=pl.ANY)
    results = pl.pallas_call(
        body, name="hgrn_bwd", grid=(heads, 5),
        in_specs=[col(0), col(1), col(2), col(3), col(4),
                  pl.BlockSpec((t, HEAD), lambda h, p: (0, h)), pl.BlockSpec((t, HEAD), lambda h, p: (0, h)),
                  pl.BlockSpec((2, HEAD), lambda h, p: (0, h)), pl.BlockSpec((1, HEAD), lambda h, p: (0, 0)),
                  any_spec] + [any_spec] * (xin + 1),
        out_specs=[pl.BlockSpec((t, HEAD), lambda h, p: (0, p * heads + h)),
                   pl.BlockSpec((None, 1, HEAD), lambda h, p: (h, 0, 0)),
                   pl.BlockSpec((2, HEAD), lambda h, p: (0, h))] + [any_spec] * xout,
        out_shape=[jax.ShapeDtypeStruct((t, n_in), BF16), jax.ShapeDtypeStruct((heads, 1, HEAD), F32),
                   jax.ShapeDtypeStruct((2, width), F32)] + list(exchange.out_shapes),
        scratch_shapes=[pltpu.VMEM((t, HEAD), F32), pltpu.VMEM((2, t, HEAD), BF16), pltpu.VMEM((2, t, HEAD), BF16),
                        pltpu.VMEM((5, t, HEAD), BF16), pltpu.VMEM((2, nb, HEAD, HEAD), F32)] + list(exchange.sems),
        input_output_aliases={9: 0},
        compiler_params=_params(2),
    )(proj, proj, proj, proj, proj, osum, dout_a, lb, g_norm, dproj, *exchange.arrays, after)
    return results[0], results[1], results[2], results[3:]


def _adamw(w, g, m, v):
    m = ADAM_B1 * m + (1.0 - ADAM_B1) * g
    v = ADAM_B2 * v + (1.0 - ADAM_B2) * (g * g)
    m_hat = m / (1.0 - ADAM_B1 ** ADAM_STEP)
    v_hat = v / (1.0 - ADAM_B2 ** ADAM_STEP)
    delta = -ADAM_LR * (m_hat / (jnp.sqrt(v_hat) + ADAM_EPS) + ADAM_WD * w)
    return delta, m, v


def _adamw_big(name, me, w, m, v, g_parts, landing, axis):
    r, c = w.shape
    tr = _tile(r, 128)
    n_parts = len(g_parts)
    per = N_DEV // n_parts

    def body(me_ref, w_ref, m_ref, v_ref, *rest):
        g_refs, (l_ref, og_ref, od_ref, om_ref, ov_ref) = rest[:n_parts], rest[n_parts:]
        g = g_refs[0][...]
        for p in range(1, n_parts):
            g = jnp.where(me_ref[0] // per == p, g_refs[p][...], g)
        for s in range(N_DEV - 1):
            g = g + l_ref[s].astype(F32)
        og_ref[...] = g
        od_ref[...], om_ref[...], ov_ref[...] = _adamw(w_ref[...], g, m_ref[...], v_ref[...])

    shard = pl.BlockSpec((tr, c), lambda i, me_ref: (i, 0))
    if axis == 1:
        own = pl.BlockSpec((tr, c), lambda i, me_ref: (i, me_ref[0] % per))
    else:
        assert n_parts == 1
        own = pl.BlockSpec((tr, c), lambda i, me_ref: (me_ref[0] * (r // tr) + i, 0))
    grid_spec = pltpu.PrefetchScalarGridSpec(
        num_scalar_prefetch=1, grid=(r // tr,),
        in_specs=[shard, shard, shard] + [own] * n_parts + [pl.BlockSpec((N_DEV - 1, tr, c), lambda i, me_ref: (0, i, 0))],
        out_specs=[shard] * 4)
    return pl.pallas_call(
        body, name=name, grid_spec=grid_spec, out_shape=[jax.ShapeDtypeStruct((r, c), F32)] * 4,
        compiler_params=_params(1),
    )(me, w, m, v, *g_parts, landing)


def _adamw_ada(sct, dmod_mine, w, m, v):
    d, n = w.shape
    tr = _tile(d, 256)

    def body(s_ref, dm_ref, w_ref, m_ref, v_ref, og_ref, od_ref, om_ref, ov_ref):
        g = _dot(s_ref[...], dm_ref[...], precision=HIGHEST)
        og_ref[...] = g
        od_ref[...], om_ref[...], ov_ref[...] = _adamw(w_ref[...], g, m_ref[...], v_ref[...])

    blk = pl.BlockSpec((tr, n), lambda i: (i, 0))
    return pl.pallas_call(
        body, name="adamw_ada", grid=(d // tr,),
        in_specs=[pl.BlockSpec((tr, N_DEV), lambda i: (i, 0)), pl.BlockSpec((N_DEV, n), lambda i: (0, 0)), blk, blk, blk],
        out_specs=[blk] * 4, out_shape=[jax.ShapeDtypeStruct((d, n), F32)] * 4, compiler_params=_params(1),
    )(sct, dmod_mine, w, m, v)


def _adamw_small(gathered, w, m, v):
    def body(g_ref, w_ref, m_ref, v_ref, og_ref, od_ref, om_ref, ov_ref):
        g = g_ref[0]
        for s in range(1, N_DEV):
            g = g + g_ref[s]
        og_ref[...] = g
        od_ref[...], om_ref[...], ov_ref[...] = _adamw(w_ref[...], g, m_ref[...], v_ref[...])

    return pl.pallas_call(
        body, name="adamw_small", out_shape=[jax.ShapeDtypeStruct(w.shape, F32)] * 4,
        compiler_params=pltpu.CompilerParams(vmem_limit_bytes=VMEM_LIMIT),
    )(gathered, w, m, v)


def _adamw_lb(dlb_mine, lb_logits, m, v):
    def body(d_ref, l_ref, m_ref, v_ref, og_ref, od_ref, om_ref, ov_ref):
        dlb = d_ref[0]
        for s in range(1, N_DEV):
            dlb = dlb + d_ref[s]
        for dr in range(2):
            lb = _sigmoid(l_ref[dr][0:1, :] - l_ref[dr][1:2, :])
            d0 = dlb[dr:dr + 1] * lb * (1.0 - lb)
            g = jnp.concatenate([d0, -d0], axis=0)
            og_ref[dr] = g
            od_ref[dr], om_ref[dr], ov_ref[dr] = _adamw(l_ref[dr], g, m_ref[dr], v_ref[dr])

    return pl.pallas_call(body, name="adamw_lb", out_shape=[jax.ShapeDtypeStruct(lb_logits.shape, F32)] * 4,
                          )(dlb_mine, lb_logits, m, v)


def _rows(a, pad_to=8):
    flat = a.reshape(-1, LANE)
    pad = (-flat.shape[0]) % pad_to
    return jnp.pad(flat, ((0, pad), (0, 0))) if pad else flat


def kernel(x, c, w_ada, b_ada, g_pre_mix, g_post_mix, g_pre_ffn, g_post_ffn, w_in, lb_logits, g_hgrn_norm, w_a_out, g_sgu_norm, w_spatial, b_spatial, w_b_out, w_o, w_ff1, w_ff2, loss_target, m_w_ada, m_b_ada, m_g_pre_mix, m_g_post_mix, m_g_pre_ffn, m_g_post_ffn, m_w_in, m_lb_logits, m_g_hgrn_norm, m_w_a_out, m_g_sgu_norm, m_w_spatial, m_b_spatial, m_w_b_out, m_w_o, m_w_ff1, m_w_ff2, v_w_ada, v_b_ada, v_g_pre_mix, v_g_post_mix, v_g_pre_ffn, v_g_post_ffn, v_w_in, v_lb_logits, v_g_hgrn_norm, v_w_a_out, v_g_sgu_norm, v_w_spatial, v_b_spatial, v_w_b_out, v_w_o, v_w_ff1, v_w_ff2):
    t, d = x.shape[1], x.shape[2]
    n_in = w_in.shape[2] * N_DEV
    width = (n_in - 2 * d) // 7
    heads = width // HEAD
    assert heads == N_DEV and width % LANE == 0
    d_ff = w_ff1.shape[2] * N_DEV
    n_ada = w_ada.shape[2]
    me = _dev_index()
    me_arr = me.reshape(1).astype(jnp.int32)
    x2, tgt = x[0], loss_target[0]

    big = [w_in[0], w_a_out[0], w_b_out[0], w_o[0], w_ff1[0], w_ff2[0]]
    big_axes = [1, 1, 1, 0, 1, 0]
    big_names = ["w_in", "w_a_out", "w_b_out", "w_o", "w_ff1", "w_ff2"]
    w_in16 = _cast_bf16("cast_w_in", big[0])
    own_parts = [_cast_into_full("cast_" + nm, me_arr, w, ax) for nm, w, ax in zip(big_names[1:], big[1:], big_axes[1:])]

    c_rows = d // LANE
    small = _all_gather_small("gather_c_lb", _prep_small(c[0:1], lb_logits))
    sc_all = small[:, :c_rows, :].reshape(N_DEV, d)
    lb = jnp.transpose(small[:, c_rows:c_rows + 2, :], (1, 0, 2)).reshape(2, width)
    b_shard = lax.dynamic_slice_in_dim(b_ada, me * n_ada, n_ada, axis=1)
    mod_sh = _mod_shard(sc_all, w_ada[0], b_shard)
    mod_all = _all_gather_small("gather_mod", _rows(mod_sh))
    mod_all = mod_all[:, :N_DEV * n_ada // LANE, :].reshape(N_DEV, N_DEV, n_ada)
    mod6 = lax.dynamic_index_in_dim(mod_all, me, axis=1, keepdims=False).reshape(N_MOD, d)
    sh1, sc1, gt1, sh2, sc2, gt2 = [mod6[i:i + 1] for i in range(N_MOD)]

    a1 = _norm_mod(x2, g_pre_mix, sh1, sc1)
    tm = _tile(t, 512)

    def store_f32(acc, i, j, extra_refs, out_refs, rows):
        out_refs[0][...] = acc

    xq, yq, cq = lax.axis_index("x"), lax.axis_index("y"), lax.axis_index("c")
    chips = [(1 - xq, yq), (xq, 1 - yq), (1 - xq, 1 - yq)]
    order = jnp.stack([me, 4 * xq + 2 * yq + 1 - cq]
                      + [4 * a + 2 * b + cq for a, b in chips[:2]] + [4 * a + 2 * b + 1 - cq for a, b in chips[:2]]
                      + [4 * chips[2][0] + 2 * chips[2][1] + cq, 4 * chips[2][0] + 2 * chips[2][1] + 1 - cq]).astype(jnp.int32)
    proj, wf_in = _proj_gather(a1, w_in16, order)

    proj, own_parts = lax.optimization_barrier((proj, own_parts))
    gathers = {}
    for key, lo, hi in (("mid", 1, 4), ("ff1", 4, 5), ("ff2", 5, 6)):
        far, near = _gather_stage_plans(own_parts[lo - 1:hi - 1], big_axes[lo:hi])
        gathers[key] = [far, near, _split_start("gather_%s_start" % key, far, landing=own_parts[lo - 1:hi - 1])]

    def pass_on(key, after):
        far, near, (sems, thru, _) = gathers[key]
        parts = _split_wait("gather_%s_wait" % key, far, sems, thru, after)[1]
        gathers[key].append(_split_start("pass_%s_start" % key, near, landing=list(parts)))
        return gathers[key][3][2]

    def gathered_weights(key, after):
        near, (sems, thru, _) = gathers[key][1], gathers[key][3]
        return _split_wait("pass_%s_wait" % key, near, sems, thru, after)[1]

    out_a, osum, _ = _hgrn_fwd(proj, lb, g_hgrn_norm, width, _NO_EXCHANGE,
                               after=[gathers[key][2][2] for key in ("mid", "ff1", "ff2")])
    passed_mid = pass_on("mid", out_a)
    z_block = 5
    bst = b_spatial[0].T
    out_b = _sgu_fwd(proj, g_sgu_norm, w_spatial[0], bst, width, z_block)
    wf_a, wf_b, wf_o = gathered_weights("mid", out_b)

    tn_d = _tile(d, 512)
    blk_d = ((tm, tn_d), lambda i, j: (i, j))
    y_a, = _mm("y_a", out_a, wf_a, _NN, t, d, width, tm, tn_d, width, _after(passed_mid),
               [(jax.ShapeDtypeStruct((t, d), F32),) + blk_d], store_f32)
    ga_blk = (5 * width + 2 * width) // tn_d
    gb_blk = ga_blk + d // tn_d

    def merge(acc, i, j, extra_refs, out_refs, rows):
        ga, gb, ya = extra_refs
        out_refs[0][...] = acc
        out_refs[1][...] = (_sigmoid(ga[...]) * ya[...] + _sigmoid(gb[...]) * acc).astype(BF16)

    y_b, merged = _mm("y_b_merge", out_b, wf_b, _NN, t, d, width, tm, tn_d, width,
                      [(proj, (tm, tn_d), lambda i, j: (i, ga_blk + j)), (proj, (tm, tn_d), lambda i, j: (i, gb_blk + j)),
                       (y_a,) + blk_d],
                      [(jax.ShapeDtypeStruct((t, d), F32),) + blk_d, (jax.ShapeDtypeStruct((t, d), BF16),) + blk_d], merge)

    tr = _tile(t, 512)
    rc = 32 if tr % 32 == 0 else None
    row_d = ((tr, d), lambda i, j: (i, 0))
    vec_d = ((1, d), lambda i, j: (0, 0))

    passed_ff1 = pass_on("ff1", merged)

    def post_mix(acc, i, j, extra_refs, out_refs, rows):
        x_r, gt1_r, g2_r, g3_r, sc2_r, sh2_r = extra_refs[:6]
        h1 = x_r[rows, :] + gt1_r[...] * (acc * _rms(acc) * g2_r[...])
        out_refs[0][rows, :] = acc
        out_refs[1][rows, :] = h1
        out_refs[2][rows, :] = ((h1 * _rms(h1) * g3_r[...]) * (1.0 + sc2_r[...]) + sh2_r[...]).astype(BF16)

    mo, h1, a2 = _mm("w_o_post_mix", merged, wf_o, _NN, t, d, d, tr, d, d,
                     [(x2,) + row_d, (gt1,) + vec_d, (g_post_mix,) + vec_d, (g_pre_ffn,) + vec_d, (sc2,) + vec_d, (sh2,) + vec_d]
                     + _after(passed_ff1),
                     [(jax.ShapeDtypeStruct((t, d), F32),) + row_d, (jax.ShapeDtypeStruct((t, d), F32),) + row_d,
                      (jax.ShapeDtypeStruct((t, d), BF16),) + row_d], post_mix, row_chunk=rc)

    tn_f = _tile(d_ff, 1024)
    blk_f = ((tm, tn_f), lambda i, j: (i, j))

    def relu_sq(acc, i, j, extra_refs, out_refs, rows):
        r = jnp.maximum(acc, 0.0)
        out_refs[0][...] = acc.astype(BF16)
        out_refs[1][...] = (r * r).astype(BF16)

    wf_1, = gathered_weights("ff1", a2)
    hff, act = _mm(
        "ff1", a2, wf_1, _NN, t, d_ff, d, tm, tn_f, d, [],
        [(jax.ShapeDtypeStruct((t, d_ff), BF16),) + blk_f, (jax.ShapeDtypeStruct((t, d_ff), BF16),) + blk_f], relu_sq)
    pass_on("ff2", hff)
    wf_2, = gathered_weights("ff2", act)

    sums_d = ((8, d), lambda i, j: (0, 0))

    def loss_head(acc, sums, extra_refs, out_refs, rows):
        h1_r, tgt_r, gt2_r, g4_r = extra_refs
        dy_r, dff_r, sums_r = out_refs
        r4 = _rms(acc)
        ffn = acc * r4
        n4 = ffn * g4_r[...]
        err = h1_r[rows, :] + gt2_r[...] * n4 - tgt_r[rows, :]
        dy = err * (1.0 / d)
        dy_r[rows, :] = dy
        dn4 = dy * gt2_r[...]
        dffn = dn4 * g4_r[...]
        dff_r[rows, :] = (r4 * (dffn - ffn * jnp.mean(dffn * ffn, axis=-1, keepdims=True))).astype(BF16)
        sums.add(sums_r, 0, _colsum(err * err))
        sums.add(sums_r, 1, _colsum(dy * n4))
        sums.add(sums_r, 2, _colsum(dn4 * ffn))

    n_chunks = 8
    tk_f = d_ff // n_chunks
    dy, dff, sums_f = _mm_rows("ff2_loss", act, wf_2, _NN, t, d, d_ff, tr, tk_f,
                               [(h1,) + row_d, (tgt,) + row_d, (gt2,) + vec_d, (g_post_ffn,) + vec_d],
                               [(jax.ShapeDtypeStruct((t, d), F32),) + row_d, (jax.ShapeDtypeStruct((t, d), BF16),) + row_d,
                                (jax.ShapeDtypeStruct((8, d), F32),) + sums_d], loss_head)
    loss_mine = (0.5 / d) * jnp.sum(sums_f[0])

    def relu_sq_bwd(acc, i, j, extra_refs, out_refs, rows):
        out_refs[0][...] = (acc * (2.0 * jnp.maximum(extra_refs[0][...].astype(F32), 0.0))).astype(BF16)

    dhff, = _mm("d_hff", dff, wf_2, _NT, t, d_ff, d, tm, tn_f, d, [(hff,) + blk_f],
                [(jax.ShapeDtypeStruct((t, d_ff), BF16),) + blk_f], relu_sq_bwd)
    scatters = {}

    def send_grads(key, grads16, axes):
        plan = _scatter_plan(grads16, axes)
        scatters[key] = (plan,) + _split_start("scatter_%s_start" % key, plan)
        return scatters[key][3]

    def received_grads(key, after):
        plan, sems, thru, _ = scatters[key]
        return _split_wait("scatter_%s_wait" % key, plan, sems, thru, after)[1]

    gw_ff2, gw_ff2_16 = _grad_w("grad_w_ff2", act, dff)
    sent_ff2 = send_grads("ff2", [gw_ff2_16], big_axes[5:6])
    gw_ff1, gw_ff1_16 = _grad_w("grad_w_ff1", a2, dhff, token=sent_ff2)
    sent_ff1 = send_grads("ff1", [gw_ff1_16], big_axes[4:5])

    def pre_ffn_bwd(acc, sums, extra_refs, out_refs, rows):
        h1_r, dy_r, mo_r, sc2_r, g3_r, gt1_r, g2_r = extra_refs[:7]
        dh1_r, dmo_r, sums_r = out_refs
        h1v = h1_r[rows, :]
        r3 = _rms(h1v)
        h1n = h1v * r3
        dn3 = acc * (1.0 + sc2_r[...])
        dh1n = dn3 * g3_r[...]
        dh1 = dy_r[rows, :] + r3 * (dh1n - h1n * jnp.mean(dh1n * h1n, axis=-1, keepdims=True))
        dh1_r[rows, :] = dh1
        mov = mo_r[rows, :]
        r2 = _rms(mov)
        mon = mov * r2
        dn2 = dh1 * gt1_r[...]
        dmon = dn2 * g2_r[...]
        dmo_r[rows, :] = (r2 * (dmon - mon * jnp.mean(dmon * mon, axis=-1, keepdims=True))).astype(BF16)
        sums.add(sums_r, 0, _colsum(acc))
        sums.add(sums_r, 1, _colsum(acc * (h1n * g3_r[...])))
        sums.add(sums_r, 2, _colsum(dn3 * h1n))
        sums.add(sums_r, 3, _colsum(dh1 * (mon * g2_r[...])))
        sums.add(sums_r, 4, _colsum(dn2 * mon))

    dh1, dmo, sums_m = _mm_rows("d_a2_pre_ffn", dhff, wf_1, _NT, t, d, d_ff, tr, tk_f,
                                [(h1,) + row_d, (dy,) + row_d, (mo,) + row_d, (sc2,) + vec_d, (g_pre_ffn,) + vec_d,
                                 (gt1,) + vec_d, (g_post_mix,) + vec_d] + _after(sent_ff1),
                                [(jax.ShapeDtypeStruct((t, d), F32),) + row_d, (jax.ShapeDtypeStruct((t, d), BF16),) + row_d,
                                 (jax.ShapeDtypeStruct((8, d), F32),) + sums_d], pre_ffn_bwd)
    gw_o, gw_o_16 = _grad_w("grad_w_o", merged, dmo)

    n_j = d // tn_d

    def merge_bwd_body(dmo_ref, wo_ref, ga_ref, gb_ref, ya_ref, yb_ref, dya_ref, dyb_ref, dproj_ref, acc_s):
        g = pl.program_id(2)

        @pl.when(g == 0)
        def _():
            dm = _dot(dmo_ref[...], wo_ref[...], _NT)
            acc_s[...] = dm
            sa = _sigmoid(ga_ref[...])
            dya_ref[...] = (dm * sa).astype(BF16)
            dproj_ref[...] = (dm * ya_ref[...] * sa * (1.0 - sa)).astype(BF16)

        @pl.when(g == 1)
        def _():
            dm = acc_s[...]
            sb = _sigmoid(gb_ref[...])
            dyb_ref[...] = (dm * sb).astype(BF16)
            dproj_ref[...] = (dm * yb_ref[...] * sb * (1.0 - sb)).astype(BF16)

    tile3 = pl.BlockSpec((tm, tn_d), lambda i, j, g: (i, j))
    dy_a, dy_b, dproj = pl.pallas_call(
        merge_bwd_body, name="d_merged", grid=(t // tm, n_j, 2),
        in_specs=[pl.BlockSpec((tm, d), lambda i, j, g: (i, 0)), pl.BlockSpec((tn_d, d), lambda i, j, g: (j, 0)),
                  pl.BlockSpec((tm, tn_d), lambda i, j, g: (i, ga_blk + j)),
                  pl.BlockSpec((tm, tn_d), lambda i, j, g: (i, gb_blk + j)), tile3, tile3],
        out_specs=[tile3, tile3, pl.BlockSpec((tm, tn_d), lambda i, j, g: (i, ga_blk + g * n_j + j))],
        out_shape=[jax.ShapeDtypeStruct((t, d), BF16), jax.ShapeDtypeStruct((t, d), BF16),
                   jax.ShapeDtypeStruct((t, n_in), BF16)],
        scratch_shapes=[pltpu.VMEM((tm, tn_d), F32)], compiler_params=_params(3),
    )(dmo, wf_o, proj, proj, y_a, y_b)

    def store_bf16(acc, i, j, extra_refs, out_refs, rows):
        out_refs[0][...] = acc.astype(BF16)

    tn_w = _tile(width, 512)
    blk_w = ((tm, tn_w), lambda i, j: (i, j))
    dout_a, = _mm("d_out_a", dy_a, wf_a, _NT, t, width, d, tm, tn_w, d, [],
                  [(jax.ShapeDtypeStruct((t, width), BF16),) + blk_w], store_bf16)
    dout_b, = _mm("d_out_b", dy_b, wf_b, _NT, t, width, d, tm, tn_w, d, [],
                  [(jax.ShapeDtypeStruct((t, width), BF16),) + blk_w], store_bf16)
    gw_a, gw_a_16 = _grad_w("grad_w_a_out", out_a, dy_a)
    gw_b, gw_b_16 = _grad_w("grad_w_b_out", out_b, dy_b)

    w_st = jnp.swapaxes(w_spatial[0], 1, 2)
    dproj, dg_sgu, dw_sp, dbst = _sgu_bwd(proj, dout_b, dproj, g_sgu_norm, w_spatial[0], w_st, bst, width, z_block)
    sent_mid = send_grads("mid", [gw_a_16, gw_b_16, gw_o_16], big_axes[1:4])
    dproj, dgh_heads, dlb, _ = _hgrn_bwd(proj, osum, dout_a, dproj, lb, g_hgrn_norm, width, _NO_EXCHANGE,
                                         after=sent_mid)
    gw_in, gw_in_16 = _grad_w("grad_w_in", a1, dproj)
    sent_in = send_grads("in", [gw_in_16], big_axes[:1])

    def pre_mix_bwd(acc, sums, extra_refs, out_refs, rows):
        x_r, dh1_r, sc1_r, g1_r = extra_refs[:4]
        dx_r, sums_r = out_refs
        xv = x_r[rows, :]
        r1 = _rms(xv)
        xn = xv * r1
        dn1 = acc * (1.0 + sc1_r[...])
        dxn = dn1 * g1_r[...]
        dx_r[rows, :] = dh1_r[rows, :] + r1 * (dxn - xn * jnp.mean(dxn * xn, axis=-1, keepdims=True))
        sums.add(sums_r, 0, _colsum(acc))
        sums.add(sums_r, 1, _colsum(acc * (xn * g1_r[...])))
        sums.add(sums_r, 2, _colsum(dn1 * xn))

    grad_x, sums_x = _mm_rows(
        "d_a1_pre_mix", dproj, wf_in, _NT, t, d, n_in, tr, n_in // n_chunks,
        [(x2,) + row_d, (dh1,) + row_d, (sc1,) + vec_d, (g_pre_mix,) + vec_d] + _after(sent_in),
        [(jax.ShapeDtypeStruct((t, d), F32),) + row_d, (jax.ShapeDtypeStruct((8, d), F32),) + sums_d],
        pre_mix_bwd)

    dmod = jnp.concatenate([sums_x[0:2], sums_m[3:4], sums_m[0:2], sums_f[1:2]], axis=0).reshape(N_DEV, n_ada // LANE, LANE)
    ada_rows = -(-(n_ada // LANE) // 8) * 8
    dmod = jnp.pad(dmod, ((0, 0), (0, ada_rows - n_ada // LANE), (0, 0))).reshape(N_DEV * ada_rows, LANE)
    parts = [dmod, _rows(sums_x[2:3]), _rows(sums_m[4:5]), _rows(sums_m[2:3]), _rows(sums_f[2:3]),
             _rows(jnp.sum(dgh_heads, axis=0)), _rows(dg_sgu), _rows(dw_sp), _rows(dbst.T)]
    n_params = sum(p.shape[0] for p in parts)
    parts.append(jnp.full((8, LANE), loss_mine, F32))
    n_common = n_params + 8
    payload = jnp.concatenate(parts + [_rows(dlb)], axis=0)

    moms = [m_w_in, m_w_a_out, m_w_b_out, m_w_o, m_w_ff1, m_w_ff2]
    vars_ = [v_w_in, v_w_a_out, v_w_b_out, v_w_o, v_w_ff1, v_w_ff2]
    big_out = {}

    def big_update(nm, g_full, landing):
        k = big_names.index(nm)
        outs = _adamw_big("adamw_" + nm, me_arr, big[k], moms[k][0], vars_[k][0], g_full, landing, big_axes[k])
        big_out[nm] = [o[None] for o in outs]
        return outs[0]

    land_ff2, = received_grads("ff2", grad_x)
    done = big_update("w_ff2", [gw_ff2], land_ff2)
    land_ff1, = received_grads("ff1", done)
    done = big_update("w_ff1", [gw_ff1], land_ff1)
    land_a, land_b, land_o = received_grads("mid", done)
    big_update("w_a_out", [gw_a], land_a)
    big_update("w_b_out", [gw_b], land_b)
    done = big_update("w_o", [gw_o], land_o)

    payload, _ = lax.optimization_barrier((payload, done))
    gathered = _all_gather_small("gather_small_grads", payload)

    dmod_mine = lax.dynamic_slice_in_dim(gathered[:, :N_DEV * ada_rows, :].reshape(N_DEV, N_DEV, ada_rows * LANE),
                                         me, 1, axis=1)[:, 0, :n_ada]
    ada_out = [o[None] for o in _adamw_ada(sc_all.T, dmod_mine, w_ada[0], m_w_ada[0], v_w_ada[0])]

    def pack(b_, g1_, g2_, g3_, g4_, gh_, gs_, ws_, bs_):
        b3 = b_.reshape(N_DEV, n_ada // LANE, LANE)
        b3 = jnp.pad(b3, ((0, 0), (0, ada_rows - n_ada // LANE), (0, 0))).reshape(N_DEV * ada_rows, LANE)
        return jnp.concatenate([b3, _rows(g1_), _rows(g2_), _rows(g3_), _rows(g4_), _rows(gh_), _rows(gs_),
                                _rows(ws_), _rows(bs_), jnp.zeros((8, LANE), F32)], axis=0)

    small_w = (b_ada, g_pre_mix, g_post_mix, g_pre_ffn, g_post_ffn, g_hgrn_norm, g_sgu_norm, w_spatial, b_spatial)
    small_m = (m_b_ada, m_g_pre_mix, m_g_post_mix, m_g_pre_ffn, m_g_post_ffn, m_g_hgrn_norm, m_g_sgu_norm, m_w_spatial, m_b_spatial)
    small_v = (v_b_ada, v_g_pre_mix, v_g_post_mix, v_g_pre_ffn, v_g_post_ffn, v_g_hgrn_norm, v_g_sgu_norm, v_w_spatial, v_b_spatial)
    packed = _adamw_small(gathered[:, :n_common, :], pack(*small_w), pack(*small_m), pack(*small_v))

    def unpack(slab):
        outs, at = [], 0
        b3 = slab[:N_DEV * ada_rows].reshape(N_DEV, ada_rows, LANE)[:, :n_ada // LANE, :]
        outs.append(b3.reshape(b_ada.shape))
        at = N_DEV * ada_rows
        for ref in small_w[1:]:
            n_el = ref.size
            n_r = -(-(n_el // LANE) // 8) * 8
            outs.append(slab[at:at + n_el // LANE].reshape(ref.shape))
            at += n_r
        return outs

    small_out = [unpack(s) for s in packed]
    loss = packed[0][n_params, 0]

    dlb_all = gathered[:, n_common:n_common + 2 * heads, :].reshape(N_DEV, 2, heads, LANE)
    dlb_mine = lax.dynamic_index_in_dim(dlb_all, me, axis=2, keepdims=False)
    lb_out = _adamw_lb(dlb_mine, lb_logits, m_lb_logits, v_lb_logits)

    land_in, = received_grads("in", ada_out[0])
    big_update("w_in", [gw_in], land_in)

    order = ["w_ada", "b_ada", "g_pre_mix", "g_post_mix", "g_pre_ffn", "g_post_ffn", "w_in", "lb_logits", "g_hgrn_norm",
             "w_a_out", "g_sgu_norm", "w_spatial", "b_spatial", "w_b_out", "w_o", "w_ff1", "w_ff2"]
    small_names = ["b_ada", "g_pre_mix", "g_post_mix", "g_pre_ffn", "g_post_ffn", "g_hgrn_norm", "g_sgu_norm", "w_spatial", "b_spatial"]

    def leaf(kind, nm):
        if nm == "w_ada":
            return ada_out[kind]
        if nm == "lb_logits":
            return lb_out[kind]
        if nm in big_out:
            return big_out[nm][kind]
        return small_out[kind][small_names.index(nm)]

    result = [loss, grad_x[None]]
    for kind in range(4):
        result += [leaf(kind, nm) for nm in order]
    return tuple(result)
```

```python
import functools
import math

import jax
import jax.numpy as jnp
from jax import lax
from jax.experimental import pallas as pl
from jax.experimental.pallas import tpu as pltpu

F32 = jnp.float32
BF16 = jnp.bfloat16
MESH = pl.DeviceIdType.MESH
HIGHEST = lax.Precision.HIGHEST

N_DEV = 8
HEAD = 128
A_CHUNK = 32
N_MOD = 6
EPS = 1e-6
LANE = 128
VMEM_LIMIT = 60 * 1024 * 1024

ADAM_LR = 0.001
ADAM_B1 = 0.9
ADAM_B2 = 0.999
ADAM_EPS = 1e-08
ADAM_WD = 0.01
ADAM_STEP = 10

_NN = (((1,), (0,)), ((), ()))
_NT = (((1,), (1,)), ((), ()))
_TN = (((0,), (0,)), ((), ()))


def _dot(a, b, dims=_NN, precision=None):
    return lax.dot_general(a, b, dims, preferred_element_type=F32, precision=precision)


def _bdot(a, b, dims=_NN):
    return _dot(a.astype(BF16), b.astype(BF16), dims)


def _params(n_grid):
    return pltpu.CompilerParams(dimension_semantics=("arbitrary",) * n_grid, vmem_limit_bytes=VMEM_LIMIT)


def _dev_index():
    return lax.axis_index("x") * 4 + lax.axis_index("y") * 2 + lax.axis_index("c")


def _dev_coords(i):
    return (i // 4, (i // 2) % 2, i % 2)


def _sigmoid(x):
    return 1.0 / (1.0 + jnp.exp(-x))


def _erf(x):
    ax = jnp.abs(x)
    t = 1.0 / (1.0 + 0.3275911 * ax)
    poly = ((((1.061405429 * t - 1.453152027) * t + 1.421413741) * t - 0.284496736) * t + 0.254829592) * t
    y = 1.0 - poly * jnp.exp(-ax * ax)
    return jnp.where(x < 0, -y, y)


def _gelu_and_grad(x):
    cdf = 0.5 * (1.0 + _erf(x * (2.0 ** -0.5)))
    pdf = jnp.exp(-0.5 * x * x) * (1.0 / math.sqrt(2.0 * math.pi))
    return x * cdf, cdf + x * pdf


def _rms(x):
    return lax.rsqrt(jnp.mean(x * x, axis=-1, keepdims=True) + EPS)


def _colsum(x):
    return jnp.sum(x, axis=0, keepdims=True)


def _tile(n, want):
    if n <= want:
        return n
    t = (want // LANE) * LANE
    while n % t:
        t -= LANE
    assert t > 0, (n, want)
    return t


def _all_gather_small(name, payload):
    rows = payload.shape[0]

    def body(p_ref, out_ref, send_sems, recv_sems, local_sem):
        me = _dev_index()
        mine = pltpu.make_async_copy(p_ref, out_ref.at[me], local_sem)
        mine.start()
        sends = []
        for r in range(1, N_DEV):
            peer = (me + r) % N_DEV
            cp = pltpu.make_async_remote_copy(
                src_ref=p_ref, dst_ref=out_ref.at[me], send_sem=send_sems.at[r - 1], recv_sem=recv_sems.at[r - 1],
                device_id=_dev_coords(peer), device_id_type=MESH)
            cp.start()
            sends.append(cp)
        for r in range(1, N_DEV):
            src = (me + N_DEV - r) % N_DEV
            pltpu.make_async_remote_copy(
                src_ref=p_ref, dst_ref=out_ref.at[src], send_sem=send_sems.at[r - 1], recv_sem=recv_sems.at[r - 1],
                device_id=_dev_coords(src), device_id_type=MESH).wait_recv()
        for cp in sends:
            cp.wait_send()
        mine.wait()

    return pl.pallas_call(
        body, name=name,
        out_shape=jax.ShapeDtypeStruct((N_DEV, rows, LANE), F32),
        in_specs=[pl.BlockSpec(memory_space=pltpu.VMEM)],
        out_specs=pl.BlockSpec(memory_space=pltpu.VMEM),
        scratch_shapes=[pltpu.SemaphoreType.DMA((N_DEV - 1,)), pltpu.SemaphoreType.DMA((N_DEV - 1,)),
                        pltpu.SemaphoreType.DMA],
        compiler_params=pltpu.CompilerParams(vmem_limit_bytes=VMEM_LIMIT),
    )(payload)


def _region(ref, dev, axis, n):
    start = pl.multiple_of(dev * n, LANE if axis == 1 else 16)
    return ref.at[:, pl.ds(start, n)] if axis == 1 else ref.at[pl.ds(start, n), :]


class _Exchange:
    def __init__(self, arrays, out_shapes, sems, start, finish):
        self.arrays, self.out_shapes, self.sems, self.start, self.finish = arrays, out_shapes, sems, start, finish


def _scatter_plan(grads, axes):
    n_w = len(grads)
    lands = []
    for g, ax in zip(grads, axes):
        shp = (g.shape[0], g.shape[1] // N_DEV) if ax == 1 else (g.shape[0] // N_DEV, g.shape[1])
        lands.append(jax.ShapeDtypeStruct((N_DEV - 1,) + shp, BF16))
    widths = [ld.shape[1 + ax] for ld, ax in zip(lands, axes)]

    def copy(w, r, g_refs, l_refs, sems, block, to):
        return pltpu.make_async_remote_copy(
            src_ref=_region(g_refs[w], block, axes[w], widths[w]), dst_ref=l_refs[w].at[r - 1],
            send_sem=sems[0].at[w * (N_DEV - 1) + r - 1], recv_sem=sems[1].at[w * (N_DEV - 1) + r - 1],
            device_id=_dev_coords(to), device_id_type=MESH)

    def start(g_refs, l_refs, sems):
        me = _dev_index()
        for w in range(n_w):
            for r in range(1, N_DEV):
                owner = (me + r) % N_DEV
                copy(w, r, g_refs, l_refs, sems, owner, owner).start()

    def finish(g_refs, l_refs, sems):
        me = _dev_index()
        for w in range(n_w):
            for r in range(1, N_DEV):
                copy(w, r, g_refs, l_refs, sems, me, (me + N_DEV - r) % N_DEV).wait_recv()
        for w in range(n_w):
            for r in range(1, N_DEV):
                copy(w, r, g_refs, l_refs, sems, me, (me + r) % N_DEV).wait_send()

    sems = [pltpu.SemaphoreType.DMA((n_w * (N_DEV - 1),)), pltpu.SemaphoreType.DMA((n_w * (N_DEV - 1),))]
    return _Exchange(list(grads), lands, sems, start, finish)


def _places():
    x, y, c = lax.axis_index("x"), lax.axis_index("y"), lax.axis_index("c")
    return (x, y, c), (x, y, 1 - c), [(1 - x, y), (x, 1 - y), (1 - x, 1 - y)]


def _place_index(p):
    return p[0] * 4 + p[1] * 2 + p[2]


def _gather_stage_plans(fulls, axes):
    n_w = len(fulls)
    widths = [f.shape[ax] // N_DEV for f, ax in zip(fulls, axes)]
    shapes = [jax.ShapeDtypeStruct(f.shape, f.dtype) for f in fulls]

    def copy(per, w, k, f_refs, sems, block, to):
        part = _region(f_refs[w], _place_index(block), axes[w], widths[w])
        return pltpu.make_async_remote_copy(
            src_ref=part, dst_ref=part, send_sem=sems[0].at[w * per + k], recv_sem=sems[1].at[w * per + k],
            device_id=to, device_id_type=MESH)

    def start1(_, f_refs, sems):
        me, sib, chips = _places()
        for w in range(n_w):
            copy(4, w, 0, f_refs, sems, me, sib).start()
            for j, chip in enumerate(chips):
                copy(4, w, 1 + j, f_refs, sems, me, (*chip, me[2])).start()

    def finish1(_, f_refs, sems):
        me, sib, chips = _places()
        for w in range(n_w):
            copy(4, w, 0, f_refs, sems, sib, me).wait_recv()
            for j, chip in enumerate(chips):
                copy(4, w, 1 + j, f_refs, sems, (*chip, me[2]), me).wait_recv()
        for w in range(n_w):
            for k in range(4):
                copy(4, w, k, f_refs, sems, me, sib).wait_send()

    def start2(_, f_refs, sems):
        me, sib, chips = _places()
        for w in range(n_w):
            for j, chip in enumerate(chips):
                copy(3, w, j, f_refs, sems, (*chip, me[2]), sib).start()

    def finish2(_, f_refs, sems):
        me, sib, chips = _places()
        for w in range(n_w):
            for j, chip in enumerate(chips):
                copy(3, w, j, f_refs, sems, (*chip, sib[2]), me).wait_recv()
        for w in range(n_w):
            for j, chip in enumerate(chips):
                copy(3, w, j, f_refs, sems, (*chip, me[2]), sib).wait_send()

    sems1 = [pltpu.SemaphoreType.DMA((n_w * 4,)), pltpu.SemaphoreType.DMA((n_w * 4,))]
    sems2 = [pltpu.SemaphoreType.DMA((n_w * 3,)), pltpu.SemaphoreType.DMA((n_w * 3,))]
    return _Exchange([], shapes, sems1, start1, finish1), _Exchange([], shapes, sems2, start2, finish2)


_NO_EXCHANGE = _Exchange([], [], [], lambda i, o, s: None, lambda i, o, s: None)

_HBM = pl.BlockSpec(memory_space=pltpu.HBM)
_SEM = pl.BlockSpec(memory_space=pltpu.SEMAPHORE)
_EFFECT = pltpu.SideEffectType.DATAFLOW_SIDE_EFFECTING


def _split_start(name, plan, landing=None):
    n_in, n_out, n_sem = len(plan.arrays), len(plan.out_shapes), len(plan.sems)

    def body(*refs):
        ins, lands = refs[:n_in], refs[n_in:n_in + n_out]
        sems = refs[n_in + n_out:n_in + n_out + n_sem]
        token = refs[-1]
        plan.start(ins, lands, sems)
        token[...] = jnp.zeros_like(token)

    hbm = lambda a: pltpu.HBM(a.shape, a.dtype)
    results = pl.pallas_call(
        body, name=name,
        out_shape=tuple(plan.sems) + tuple(hbm(a) for a in plan.arrays) + tuple(hbm(a) for a in plan.out_shapes)
        + (jax.ShapeDtypeStruct((8, LANE), F32),),
        in_specs=(_HBM,) * (n_in + n_out),
        out_specs=(_SEM,) * n_sem + (_HBM,) * (n_in + n_out) + (pl.BlockSpec(memory_space=pltpu.VMEM),),
        input_output_aliases={i: n_sem + i for i in range(n_in + n_out)},
        compiler_params=pltpu.CompilerParams(has_side_effects=_EFFECT),
    )(*[pltpu.with_memory_space_constraint(a, pltpu.HBM) for a in plan.arrays],
      *[pltpu.with_memory_space_constraint(a, pltpu.HBM)
        for a in (landing if landing is not None else [lax.empty(a.shape, a.dtype) for a in plan.out_shapes])])
    return results[:n_sem], results[n_sem:n_sem + n_in + n_out], results[-1]


def _split_wait(name, plan, sems, thru, after):
    n_in, n_out, n_sem = len(plan.arrays), len(plan.out_shapes), len(plan.sems)

    def body(*refs):
        ins, lands = refs[:n_in], refs[n_in:n_in + n_out]
        sem_refs = refs[n_in + n_out:n_in + n_out + n_sem]
        plan.finish(ins, lands, sem_refs)

    hbm = lambda a: pltpu.HBM(a.shape, a.dtype)
    results = pl.pallas_call(
        body, name=name,
        out_shape=tuple(hbm(a) for a in plan.arrays) + tuple(hbm(a) for a in plan.out_shapes),
        in_specs=(_HBM,) * (n_in + n_out) + (_SEM,) * n_sem + (pl.BlockSpec(memory_space=pl.ANY),),
        out_specs=(_HBM,) * (n_in + n_out),
        input_output_aliases={i: i for i in range(n_in + n_out)},
        compiler_params=pltpu.CompilerParams(has_side_effects=_EFFECT),
    )(*thru, *sems, after)
    return results[:n_in], results[n_in:]


def _cast_into_full(name, me, w, axis):
    r, c = w.shape
    tr = _tile(r, 256)
    if axis == 1:
        shape, place = (r, c * N_DEV), pl.BlockSpec((tr, c), lambda i, me_ref: (i, me_ref[0]))
    else:
        shape, place = (r * N_DEV, c), pl.BlockSpec((tr, c), lambda i, me_ref: (me_ref[0] * (r // tr) + i, 0))

    def body(me_ref, w_ref, o_ref):
        o_ref[...] = w_ref[...].astype(BF16)

    grid_spec = pltpu.PrefetchScalarGridSpec(
        num_scalar_prefetch=1, grid=(r // tr,),
        in_specs=[pl.BlockSpec((tr, c), lambda i, me_ref: (i, 0))], out_specs=place)
    return pl.pallas_call(body, name=name, grid_spec=grid_spec, out_shape=jax.ShapeDtypeStruct(shape, BF16),
                          compiler_params=_params(1))(me, w)


def _mm(name, a, b, dims, m, n, k, tm, tn, tk, extras, outs, epilogue, row_chunk=None, exchange=None,
        b_col_block=0):
    ni, nj, nk = m // tm, n // tn, k // tk
    ne, no = len(extras), len(outs)
    xin = len(exchange.arrays) if exchange else 0
    xout = len(exchange.out_shapes) if exchange else 0
    if dims == _TN:
        a_spec = pl.BlockSpec((tk, tm), lambda i, j, kk: (kk, i))
    else:
        a_spec = pl.BlockSpec((tm, tk), lambda i, j, kk: (i, kk))
    if dims == _NT:
        b_spec = pl.BlockSpec((tn, tk), lambda i, j, kk: (j, kk))
    else:
        b_spec = pl.BlockSpec((tk, tn), lambda i, j, kk: (kk, j + b_col_block))
    chunks = [slice(None)] if row_chunk is None else [slice(r, r + row_chunk) for r in range(0, tm, row_chunk)]

    def lift(index_map):
        return lambda i, j, kk: index_map(i, j)

    def body(a_ref, b_ref, *rest):
        extra_refs, rest = rest[:ne], rest[ne:]
        xin_refs, rest = rest[:xin], rest[xin:]
        out_refs, rest = rest[:no], rest[no:]
        xout_refs, rest = rest[:xout], rest[xout:]
        i, j, kk = pl.program_id(0), pl.program_id(1), pl.program_id(2)
        if exchange:
            sem_refs = rest[1:] if nk > 1 else rest

            @pl.when((i == 0) & (j == 0) & (kk == 0))
            def _():
                exchange.start(xin_refs, xout_refs, sem_refs)

        if nk == 1:
            part = _dot(a_ref[...], b_ref[...], dims)
            for rows in chunks:
                epilogue(part[rows], i, j, extra_refs, out_refs, rows)
        else:
            acc_ref = rest[0]

            @pl.when(kk == 0)
            def _():
                acc_ref[...] = _dot(a_ref[...], b_ref[...], dims)

            @pl.when(kk > 0)
            def _():
                acc_ref[...] += _dot(a_ref[...], b_ref[...], dims)

            @pl.when(kk == nk - 1)
            def _():
                for rows in chunks:
                    epilogue(acc_ref[rows, :], i, j, extra_refs, out_refs, rows)

        if exchange:
            @pl.when((i == ni - 1) & (j == nj - 1) & (kk == nk - 1))
            def _():
                exchange.finish(xin_refs, xout_refs, sem_refs)

    any_spec = pl.BlockSpec(memory_space=pl.ANY)
    once = dict(pipeline_mode=pl.Buffered(1)) if (row_chunk is not None and nk > 1) else {}
    results = pl.pallas_call(
        body, name=name,
        grid=(ni, nj, nk),
        in_specs=[a_spec, b_spec] + [pl.BlockSpec(bs, lift(im), **once) for _, bs, im in extras] + [any_spec] * xin,
        out_specs=[pl.BlockSpec(bs, lift(im), **once) for _, bs, im in outs] + [any_spec] * xout,
        out_shape=[sd for sd, _, _ in outs] + (list(exchange.out_shapes) if exchange else []),
        scratch_shapes=([pltpu.VMEM((tm, tn), F32)] if nk > 1 else []) + (list(exchange.sems) if exchange else []),
        compiler_params=_params(3),
    )(a, b, *[arr for arr, _, _ in extras], *(exchange.arrays if exchange else []))
    return (results[:no], results[no:]) if exchange else results


def _after(token):
    return [(token, (8, LANE), lambda i, j: (0, 0))]


def _grad_w(name, a, dc, token=None, tm=512, tn=1024, cols=None):
    t, m = a.shape
    first, n = cols if cols is not None else (0, dc.shape[1])
    tm, tn = _tile(m, tm), _tile(n, tn)
    assert first % tn == 0

    def epilogue(acc, i, j, extra_refs, out_refs, rows):
        out_refs[0][...] = acc
        out_refs[1][...] = acc.astype(BF16)

    blk = ((tm, tn), lambda i, j: (i, j))
    return _mm(name, a, dc, _TN, m, n, t, tm, tn, t, _after(token) if token is not None else [],
               [(jax.ShapeDtypeStruct((m, n), F32),) + blk, (jax.ShapeDtypeStruct((m, n), BF16),) + blk], epilogue,
               b_col_block=first // tn)


def _proj_gather(a1, w_shard, order):
    t, d = a1.shape
    nsh = w_shard.shape[1]
    tm = _tile(t, 512)
    n_i = t // tm

    def body(ord_ref, a_ref, wsh_ref, proj_ref, full_ref, bbuf, bsem, send_sems, recv_sems, own_sem):
        s, i = pl.program_id(0), pl.program_id(1)
        me, sib, chips = _places()
        near, far = chips[:2], chips[2]
        steps = ([(me, None, None), (sib, 0, None)]
                 + [((*ch, me[2]), 1 + j, 4 + j) for j, ch in enumerate(near)]
                 + [((*ch, sib[2]), 4 + j, None) for j, ch in enumerate(near)]
                 + [((*far, me[2]), 3, 6), ((*far, sib[2]), 6, None)])
        blocks = [st[0] for st in steps]

        def part(block):
            return _region(full_ref, _place_index(block), 1, nsh)

        def remote(k, block, to, from_shard=False):
            return pltpu.make_async_remote_copy(
                src_ref=wsh_ref if from_shard else part(block), dst_ref=part(block),
                send_sem=send_sems.at[k], recv_sem=recv_sems.at[k], device_id=to, device_id_type=MESH)

        def load(pos):
            src = wsh_ref if pos == 0 else part(blocks[pos])
            return pltpu.make_async_copy(src, bbuf.at[pos % 2], bsem.at[pos % 2])

        own = pltpu.make_async_copy(wsh_ref, part(me), own_sem)

        @pl.when((s == 0) & (i == 0))
        def _():
            own.start()
            remote(0, me, sib, True).start()
            for j, ch in enumerate(chips):
                remote(1 + j, me, (*ch, me[2]), True).start()
            load(0).start()
            load(0).wait()

        for pos in range(1, N_DEV):
            @pl.when((s == pos) & (i == 0))
            def _():
                load(pos).wait()

        for pos in range(N_DEV - 1):
            @pl.when((s == pos) & (i == n_i - 1))
            def _():
                nxt = pos + 1
                block, arrives_on, pass_on_with = steps[nxt]
                remote(arrives_on, block, me).wait_recv()
                if pass_on_with is not None:
                    remote(pass_on_with, block, sib).start()
                load(nxt).start()

        proj_ref[...] = _dot(a_ref[...], bbuf[s % 2])

        @pl.when((s == N_DEV - 1) & (i == n_i - 1))
        def _():
            for k in range(N_DEV - 1):
                remote(k, me, sib, True).wait_send()
            own.wait()

    grid_spec = pltpu.PrefetchScalarGridSpec(
        num_scalar_prefetch=1, grid=(N_DEV, n_i),
        in_specs=[pl.BlockSpec((tm, d), lambda s, i, ord_ref: (i, 0)), pl.BlockSpec(memory_space=pl.ANY)],
        out_specs=[pl.BlockSpec((tm, nsh), lambda s, i, ord_ref: (i, ord_ref[s])), pl.BlockSpec(memory_space=pl.ANY)],
        scratch_shapes=[pltpu.VMEM((2, d, nsh), BF16), pltpu.SemaphoreType.DMA((2,)),
                        pltpu.SemaphoreType.DMA((N_DEV - 1,)), pltpu.SemaphoreType.DMA((N_DEV - 1,)),
                        pltpu.SemaphoreType.DMA])
    return pl.pallas_call(
        body, name="proj_gather", grid_spec=grid_spec,
        out_shape=[jax.ShapeDtypeStruct((t, nsh * N_DEV), F32), jax.ShapeDtypeStruct((d, nsh * N_DEV), BF16)],
        compiler_params=_params(2),
    )(order, a1, w_shard)


def _cast_bf16(name, w):
    r, c = w.shape
    tr = _tile(r, 256)
    return pl.pallas_call(
        lambda w_ref, o_ref: o_ref.__setitem__(Ellipsis, w_ref[...].astype(BF16)), name=name,
        grid=(r // tr,), in_specs=[pl.BlockSpec((tr, c), lambda i: (i, 0))],
        out_specs=pl.BlockSpec((tr, c), lambda i: (i, 0)), out_shape=jax.ShapeDtypeStruct((r, c), BF16),
        compiler_params=_params(1),
    )(w)


def _prep_small(c_row, lb_logits):
    d = c_row.shape[1]
    rows = d // LANE

    def body(c_ref, l_ref, o_ref):
        cv = c_ref[...]
        o_ref[0:rows, :] = cv * _sigmoid(cv)
        lbs = [_sigmoid(l_ref[dr][0:1, :] - l_ref[dr][1:2, :]) for dr in range(2)]
        o_ref[rows:rows + 8, :] = jnp.concatenate(lbs + [jnp.zeros((6, LANE), F32)], axis=0)

    return pl.pallas_call(
        body, name="prep_small", out_shape=jax.ShapeDtypeStruct((rows + 8, LANE), F32),
    )(c_row.reshape(rows, LANE), lb_logits)


def _mod_shard(sc_all, w_ada_shard, b_shard):
    d, n = w_ada_shard.shape
    tn = _tile(n, 512)

    def body(s_ref, w_ref, b_ref, o_ref):
        o_ref[...] = _dot(s_ref[...], w_ref[...], precision=HIGHEST) + b_ref[...]

    return pl.pallas_call(
        body, name="mod_shard", grid=(n // tn,),
        in_specs=[pl.BlockSpec((N_DEV, d), lambda j: (0, 0)), pl.BlockSpec((d, tn), lambda j: (0, j)),
                  pl.BlockSpec((1, tn), lambda j: (0, j))],
        out_specs=pl.BlockSpec((N_DEV, tn), lambda j: (0, j)),
        out_shape=jax.ShapeDtypeStruct((N_DEV, n), F32), compiler_params=_params(1),
    )(sc_all, w_ada_shard, b_shard)


def _norm_mod(x, gain, shift, scale):
    t, d = x.shape
    tm = _tile(t, 512)

    def body(x_ref, g_ref, sh_ref, sc_ref, o_ref):
        xv = x_ref[...]
        o_ref[...] = ((xv * _rms(xv) * g_ref[...]) * (1.0 + sc_ref[...]) + sh_ref[...]).astype(BF16)

    vec = pl.BlockSpec((1, d), lambda i: (0, 0))
    return pl.pallas_call(
        body, name="norm_mod", grid=(t // tm,),
        in_specs=[pl.BlockSpec((tm, d), lambda i: (i, 0)), vec, vec, vec],
        out_specs=pl.BlockSpec((tm, d), lambda i: (i, 0)), out_shape=jax.ShapeDtypeStruct((t, d), BF16),
        compiler_params=_params(1),
    )(x, gain, shift, scale)


def _chunk_masks():
    row = lax.broadcasted_iota(jnp.int32, (HEAD, HEAD), 0)
    col = lax.broadcasted_iota(jnp.int32, (HEAD, HEAD), 1)
    same = (row // A_CHUNK) == (col // A_CHUNK)
    return same & (col <= row), same & (col >= row)


def _ones(mask):
    return jnp.where(mask, 1.0, 0.0).astype(BF16)


def _dot_split(ones_bf16, x):
    hi = x.astype(BF16)
    lo = (x - hi.astype(F32)).astype(BF16)
    return _dot(ones_bf16, hi) + _dot(ones_bf16, lo)


def _hgrn_block(direction, f, lb, cum2):
    sf = _sigmoid(f)
    big_f = lb + (1.0 - lb) * sf
    k = (1.0 - lb) * (1.0 - sf)
    lf = jnp.log(big_f)
    both = _dot_split(cum2, lf)
    cf, cr = both[:HEAD], both[HEAD:]
    b, rest = (cf, cr - lf) if direction == 0 else (cr, cf - lf)
    return k, sf, big_f, jnp.exp(b), jnp.exp(-b), jnp.exp(rest)


def _hgrn_fwd(proj, lb, g_norm, width, exchange, after):
    t = proj.shape[0]
    heads = width // HEAD
    nb, nc = t // HEAD, t // A_CHUNK
    ua = 4 if nb % 4 == 0 else (2 if nb % 2 == 0 else 1)
    ub = 16 if nc % 16 == 0 else (8 if nc % 8 == 0 else 4)
    q_scale = HEAD ** -0.5
    xin, xout = len(exchange.arrays), len(exchange.out_shapes)

    def body(q_ref, ffw_ref, fbw_ref, v_ref, og_ref, lb_ref, g_ref, *rest):
        xin_refs, rest = rest[:xin], rest[xin + len(after):]
        outa_ref, osum_ref = rest[:2]
        xout_refs, rest = rest[2:2 + xout], rest[2 + xout:]
        qd_s, ke_s, dc_s, o_s = rest[:4]
        sem_refs = rest[4:]
        h = pl.program_id(0)

        @pl.when(h == 0)
        def _():
            exchange.start(xin_refs, xout_refs, sem_refs)

        tril, triu = _chunk_masks()
        cum2 = jnp.concatenate([_ones(tril), _ones(triu)], axis=0)
        f_refs = (ffw_ref, fbw_ref)
        lbs = (lb_ref[0:1, :], lb_ref[1:2, :])

        def phase_a(it, carry):
            loaded = []
            for u in range(ua):
                rows = pl.ds(pl.multiple_of((it * ua + u) * HEAD, HEAD), HEAD)
                loaded.append((rows, q_ref[rows, :], v_ref[rows, :], ffw_ref[rows, :], fbw_ref[rows, :]))
            chains = [(d, rows, qv * q_scale, vv.astype(BF16), fv)
                      for rows, qv, vv, f0, f1 in loaded for d, fv in ((0, f0), (1, f1))]
            blocks = [_hgrn_block(d, fv, lbs[d], cum2) for d, _, _, _, fv in chains]
            scaled = [(qv * eb, k * enb, k * erest, eb * erest)
                      for (_, _, qv, _, _), (k, _, _, eb, enb, erest) in zip(chains, blocks)]
            atts = [jnp.where(tril if d == 0 else triu, _bdot(qd, kd, _NT), 0.0)
                    for (d, _, _, _, _), (qd, kd, _, _) in zip(chains, scaled)]
            intras = [_bdot(att, vv) for att, (_, _, _, vv, _) in zip(atts, chains)]
            results = [(d, rows, o_intra, qd.astype(BF16), ke.astype(BF16), decay)
                       for (d, rows, _, _, _), (qd, _, ke, decay), o_intra in zip(chains, scaled, intras)]
            for d, rows, o_intra, qd16, ke16, decay in results:
                o_s[d, rows, :] = o_intra
                qd_s[d, rows, :] = qd16
                ke_s[d, rows, :] = ke16
                dc_s[d, rows, :] = decay
            return carry

        lax.fori_loop(0, nb // ua, phase_a, 0)

        def phase_b(it, states):
            loaded = []
            for u in range(ub):
                n = it * ub + u
                for d in range(2):
                    c = n if d == 0 else nc - 1 - n
                    start = pl.multiple_of(c * A_CHUNK, A_CHUNK)
                    rows = pl.ds(start, A_CHUNK)
                    loaded.append((d, rows, qd_s[d, rows, :], ke_s[d, rows, :], v_ref[rows, :],
                                   dc_s[d, pl.ds(start, 1), :], o_s[d, rows, :]))
            increments = [_dot(vv.astype(BF16), ke16, _TN) for _, _, _, ke16, vv, _, _ in loaded]
            states = list(states)
            befores = []
            for (d, _, _, _, _, decay, _), inc in zip(loaded, increments):
                befores.append(states[d].astype(BF16))
                states[d] = states[d] * decay + inc
            inters = [_dot(qd16, before, _NT) for (_, _, qd16, _, _, _, _), before in zip(loaded, befores)]
            for (d, rows, _, _, _, _, o_intra), o_inter in zip(loaded, inters):
                o_s[d, rows, :] = o_intra + o_inter
            return tuple(states)

        zero_state = jnp.zeros((HEAD, HEAD), F32)
        lax.fori_loop(0, nc // ub, phase_b, (zero_state, zero_state))

        def phase_c(i, carry):
            rows = pl.ds(pl.multiple_of(i * HEAD, HEAD), HEAD)
            o = o_s[0, rows, :] + o_s[1, rows, :]
            osum_ref[rows, :] = o
            og = og_ref[rows, :]
            outa_ref[rows, :] = (o * _rms(o) * g_ref[...] * (og * _sigmoid(og))).astype(BF16)
            return carry

        lax.fori_loop(0, nb, phase_c, 0)

        @pl.when(h == heads - 1)
        def _():
            exchange.finish(xin_refs, xout_refs, sem_refs)

    def col(p):
        return pl.BlockSpec((t, HEAD), lambda h: (0, p * heads + h))

    any_spec = pl.BlockSpec(memory_space=pl.ANY)
    results = pl.pallas_call(
        body, name="hgrn_fwd", grid=(heads,),
        in_specs=[col(0), col(1), col(2), col(3), col(4),
                  pl.BlockSpec((2, HEAD), lambda h: (0, h)), pl.BlockSpec((1, HEAD), lambda h: (0, 0))]
        + [any_spec] * (xin + len(after)),
        out_specs=[pl.BlockSpec((t, HEAD), lambda h: (0, h)), pl.BlockSpec((t, HEAD), lambda h: (0, h))] + [any_spec] * xout,
        out_shape=[jax.ShapeDtypeStruct((t, width), BF16), jax.ShapeDtypeStruct((t, width), F32)] + list(exchange.out_shapes),
        scratch_shapes=[pltpu.VMEM((2, t, HEAD), BF16), pltpu.VMEM((2, t, HEAD), BF16), pltpu.VMEM((2, t, HEAD), F32),
                        pltpu.VMEM((2, t, HEAD), F32)] + list(exchange.sems),
        compiler_params=_params(1),
    )(proj, proj, proj, proj, proj, lb, g_norm, *exchange.arrays, *after)
    return results[0], results[1], results[2:]


def _sgu_core(u_pre, v_pre, g_v, ws_ref, bst):
    u, du = _gelu_and_grad(u_pre)
    v, dv = _gelu_and_grad(v_pre)
    mu = jnp.mean(v, axis=-1, keepdims=True)
    dlt = v - mu
    rstd = lax.rsqrt(jnp.mean(dlt * dlt, axis=-1, keepdims=True) + EPS)
    vhat = dlt * rstd
    vn = vhat * g_v
    groups = vn.shape[1] // HEAD
    cols = []
    for g in range(groups):
        vm_g = _bdot(ws_ref[g], vn[:, g * HEAD:(g + 1) * HEAD]) + bst[:, g:g + 1]
        cols.append(vm_g)
    return u, du, dv, vhat, rstd, vn, jnp.concatenate(cols, axis=1)


def _sgu_fwd(proj, g_v, w_s, bst, width, z_block):
    t = proj.shape[0]

    def body(u_ref, v_ref, g_ref, ws_ref, bst_ref, o_ref):
        u, _, _, _, _, _, vm = _sgu_core(u_ref[...], v_ref[...], g_ref[...], ws_ref, bst_ref[...])
        o_ref[...] = (u * vm).astype(BF16)

    groups = width // HEAD
    return pl.pallas_call(
        body, name="sgu_fwd", grid=(t // HEAD,),
        in_specs=[pl.BlockSpec((HEAD, width), lambda i: (i, z_block)), pl.BlockSpec((HEAD, width), lambda i: (i, z_block + 1)),
                  pl.BlockSpec((1, width), lambda i: (0, 0)), pl.BlockSpec((groups, HEAD, HEAD), lambda i: (0, 0, 0)),
                  pl.BlockSpec((HEAD, groups), lambda i: (0, 0))],
        out_specs=pl.BlockSpec((HEAD, width), lambda i: (i, 0)),
        out_shape=jax.ShapeDtypeStruct((t, width), BF16), compiler_params=_params(1),
    )(proj, proj, g_v, w_s, bst)


def _sgu_bwd(proj, dout_b, dproj, g_v, w_s, w_st, bst, width, z_block):
    t = proj.shape[0]
    groups = width // HEAD
    nblk = t // HEAD

    def body(u_ref, v_ref, do_ref, g_ref, ws_ref, wst_ref, bst_ref, dproj_hbm,
             dz_ref, dg_ref, dws_ref, dbst_ref, res_s):
        i, p = pl.program_id(0), pl.program_id(1)

        @pl.when((i == 0) & (p == 0))
        def _():
            dg_ref[...] = jnp.zeros_like(dg_ref)
            dws_ref[...] = jnp.zeros_like(dws_ref)
            dbst_ref[...] = jnp.zeros_like(dbst_ref)

        @pl.when(p == 0)
        def _():
            g_v = g_ref[...]
            u, du, dv, vhat, rstd, vn, vm = _sgu_core(u_ref[...], v_ref[...], g_v, ws_ref, bst_ref[...])
            dout = do_ref[...].astype(F32)
            res_s[0] = (dout * vm * du).astype(BF16)
            dvm = dout * u
            dvn_cols = []
            for g in range(groups):
                sl = slice(g * HEAD, (g + 1) * HEAD)
                dvm_g = dvm[:, sl]
                dbst_ref[:, g:g + 1] += jnp.sum(dvm_g, axis=1, keepdims=True)
                dws_ref[g] += _bdot(dvm_g, vn[:, sl], _NT)
                dvn_cols.append(_bdot(wst_ref[g], dvm_g))
            dvn = jnp.concatenate(dvn_cols, axis=1)
            dg_ref[...] += _colsum(dvn * vhat)
            dvh = dvn * g_v
            dvg = rstd * (dvh - jnp.mean(dvh, axis=-1, keepdims=True)
                          - vhat * jnp.mean(dvh * vhat, axis=-1, keepdims=True))
            res_s[1] = (dvg * dv).astype(BF16)

        dz_ref[...] = res_s[p]

    n_in = dproj.shape[1]
    return pl.pallas_call(
        body, name="sgu_bwd", grid=(nblk, 2),
        in_specs=[pl.BlockSpec((HEAD, width), lambda i, p: (i, z_block)),
                  pl.BlockSpec((HEAD, width), lambda i, p: (i, z_block + 1)),
                  pl.BlockSpec((HEAD, width), lambda i, p: (i, 0)),
                  pl.BlockSpec((1, width), lambda i, p: (0, 0)),
                  pl.BlockSpec((groups, HEAD, HEAD), lambda i, p: (0, 0, 0)),
                  pl.BlockSpec((groups, HEAD, HEAD), lambda i, p: (0, 0, 0)),
                  pl.BlockSpec((HEAD, groups), lambda i, p: (0, 0)),
                  pl.BlockSpec(memory_space=pl.ANY)],
        out_specs=[pl.BlockSpec((HEAD, width), lambda i, p: (i, z_block + p)),
                   pl.BlockSpec((1, width), lambda i, p: (0, 0)),
                   pl.BlockSpec((groups, HEAD, HEAD), lambda i, p: (0, 0, 0)),
                   pl.BlockSpec((HEAD, groups), lambda i, p: (0, 0))],
        out_shape=[jax.ShapeDtypeStruct((t, n_in), BF16), jax.ShapeDtypeStruct((1, width), F32),
                   jax.ShapeDtypeStruct((groups, HEAD, HEAD), F32), jax.ShapeDtypeStruct((HEAD, groups), F32)],
        scratch_shapes=[pltpu.VMEM((2, HEAD, width), BF16)],
        input_output_aliases={7: 0},
        compiler_params=_params(2),
    )(proj, proj, dout_b, g_v, w_s, w_st, bst, dproj)


def _hgrn_bwd(proj, osum, dout_a, dproj, lb, g_norm, width, exchange, after):
    t = proj.shape[0]
    heads = width // HEAD
    nb = t // HEAD
    cpb = HEAD // A_CHUNK
    ubk = 2 if nb % 2 == 0 else 1
    q_scale = HEAD ** -0.5
    xin, xout = len(exchange.arrays), len(exchange.out_shapes)

    def body(q_ref, ffw_ref, fbw_ref, v_ref, og_ref, osum_ref, douta_ref, lb_ref, g_ref, dproj_hbm, *rest):
        xin_refs, rest = rest[:xin], rest[xin + 1:]
        out_ref, dgh_ref, dlb_ref = rest[:3]
        xout_refs, rest = rest[3:3 + xout], rest[3 + xout:]
        do_s, dq_s, dv_s, res_s, ck_s = rest[:5]
        sem_refs = rest[5:]
        h, p = pl.program_id(0), pl.program_id(1)
        f_refs = (ffw_ref, fbw_ref)

        @pl.when((h == 0) & (p == 0))
        def _():
            exchange.start(xin_refs, xout_refs, sem_refs)

        @pl.when(p == 0)
        def _():
            tril, triu = _chunk_masks()
            cum2 = jnp.concatenate([_ones(tril), _ones(triu)], axis=0)
            g_row = g_ref[...]

            def pass_norm(i, dgh):
                rows = pl.ds(pl.multiple_of(i * HEAD, HEAD), HEAD)
                o = osum_ref[rows, :]
                r = _rms(o)
                oh = o * r
                og = og_ref[rows, :]
                sg = _sigmoid(og)
                dout = douta_ref[rows, :].astype(F32)
                don = dout * (og * sg)
                res_s[4, rows, :] = (dout * (oh * g_row) * (sg * (1.0 + og * (1.0 - sg)))).astype(BF16)
                doh = don * g_row
                do_s[rows, :] = r * (doh - oh * jnp.mean(doh * oh, axis=-1, keepdims=True))
                return dgh + _colsum(don * oh)

            dgh_ref[...] = lax.fori_loop(0, nb, pass_norm, jnp.zeros((1, HEAD), F32))

            lbs = (lb_ref[0:1, :], lb_ref[1:2, :])
            zero_state = jnp.zeros((HEAD, HEAD), F32)

            def chunk_order(d):
                return list(range(cpb)) if d == 0 else list(range(cpb - 1, -1, -1))

            def chunk(x, j):
                return x[j * A_CHUNK:(j + 1) * A_CHUNK, :]

            def decay_row(e_big, j):
                return e_big[j * A_CHUNK:j * A_CHUNK + 1, :]

            def cat(parts):
                return jnp.concatenate([parts[j] for j in range(cpb)], axis=0)

            def block_states(d, start, incs, e_big):
                befores, st = {}, start
                for j in chunk_order(d):
                    befores[j] = st
                    st = st * decay_row(e_big, j) + incs[j]
                return befores, st

            def pass_states(it, states):
                loaded = []
                for u in range(ubk):
                    for d in range(2):
                        blk = it * ubk + u if d == 0 else nb - 1 - (it * ubk + u)
                        rows = pl.ds(pl.multiple_of(blk * HEAD, HEAD), HEAD)
                        loaded.append((d, blk, f_refs[d][rows, :], v_ref[rows, :]))
                blocks = [_hgrn_block(d, fv, lbs[d], cum2) for d, _, fv, _ in loaded]
                incs = [{j: _bdot(chunk(vv, j), chunk(k * erest, j), _TN) for j in range(cpb)}
                        for (_, _, _, vv), (k, _, _, _, _, erest) in zip(loaded, blocks)]
                states, starts = list(states), []
                for (d, _, _, _), (_, _, _, eb, _, erest), inc in zip(loaded, blocks, incs):
                    starts.append(states[d])
                    states[d] = block_states(d, states[d], inc, eb * erest)[1]
                for (d, blk, _, _), start in zip(loaded, starts):
                    ck_s[d, blk] = start
                return tuple(states)

            lax.fori_loop(0, nb // ubk, pass_states, (zero_state, zero_state))

            def pass_back(it, carry):
                gts, dlb = [carry[0], carry[1]], carry[2]
                loaded = []
                for u, d in ((u, d) for u in range(ubk) for d in range(2)):
                    blk = nb - 1 - (it * ubk + u) if d == 0 else it * ubk + u
                    rows = pl.ds(pl.multiple_of(blk * HEAD, HEAD), HEAD)
                    loaded.append((d, rows, f_refs[d][rows, :], q_ref[rows, :], v_ref[rows, :], do_s[rows, :], ck_s[d, blk]))
                blocks = [_hgrn_block(d, fv, lbs[d], cum2) for d, _, fv, _, _, _, _ in loaded]
                scaled = []
                for (_, _, _, qv, _, _, _), (k, _, _, eb, enb, erest) in zip(loaded, blocks):
                    qh = qv * q_scale
                    scaled.append((qh, qh * eb, k * enb, k * erest, eb * erest))
                masks = [tril if d == 0 else triu for d, *_ in loaded]
                atts = [jnp.where(m, _bdot(qd, kd, _NT), 0.0) for m, (_, qd, kd, _, _) in zip(masks, scaled)]
                datts = [jnp.where(m, _bdot(do, vv, _NT), 0.0) for m, (_, _, _, _, vv, do, _) in zip(masks, loaded)]
                dvs = [_bdot(att, do, _TN) for att, (_, _, _, _, _, do, _) in zip(atts, loaded)]
                dqds = [_bdot(datt, kd) for datt, (_, _, kd, _, _) in zip(datts, scaled)]
                dkds = [_bdot(datt, qd, _TN) for datt, (_, qd, _, _, _) in zip(datts, scaled)]
                s_incs = [{j: _bdot(chunk(vv, j), chunk(ke, j), _TN) for j in range(cpb)}
                          for (_, _, _, _, vv, _, _), (_, _, _, ke, _) in zip(loaded, scaled)]
                g_incs = [{j: _bdot(chunk(do, j), chunk(qd, j), _TN) for j in range(cpb)}
                          for (_, _, _, _, _, do, _), (_, qd, _, _, _) in zip(loaded, scaled)]
                befores, afters, g_at = [], [], []
                for (d, _, _, _, _, _, ck), (_, _, _, _, e_big), s_inc, g_inc in zip(loaded, scaled, s_incs, g_incs):
                    order = chunk_order(d)
                    before, after = block_states(d, ck, s_inc, e_big)
                    befores.append(before)
                    afters.append({j: (before[order[n + 1]] if n + 1 < cpb else after) for n, j in enumerate(order)})
                    at, gt = {}, gts[d]
                    for j in reversed(order):
                        at[j] = gt
                        gt = gt * decay_row(e_big, j) + g_inc[j]
                    gts[d] = gt
                    g_at.append(at)
                dqd_i = [{j: _bdot(chunk(do, j), before[j]) for j in range(cpb)}
                         for (_, _, _, _, _, do, _), before in zip(loaded, befores)]
                dv_i = [{j: _bdot(chunk(ke, j), at[j], _NT) for j in range(cpb)}
                        for (_, _, _, ke, _), at in zip(scaled, g_at)]
                dke = [{j: _bdot(chunk(vv, j), at[j]) for j in range(cpb)}
                       for (_, _, _, _, vv, _, _), at in zip(loaded, g_at)]
                results, new = [], []
                for n, ((d, rows, _, _, _, _, _), (k, sf, big_f, eb, enb, erest), (qh, _, _, _, _)) in enumerate(
                        zip(loaded, blocks, scaled)):
                    dqh = (dqds[n] + cat(dqd_i[n])) * eb
                    dk = dkds[n] * enb + cat(dke[n]) * erest
                    carry_rows = {j: jnp.broadcast_to(_colsum(g_at[n][j] * afters[n][j]), (A_CHUNK, HEAD))
                                  for j in range(cpb)}
                    dlf = _dot_split(_ones(triu if d == 0 else tril), qh * dqh - k * dk) + cat(carry_rows)
                    common = dlf / big_f - dk
                    results.append((d, rows, (k * sf * common).astype(BF16), dqh.astype(BF16),
                                    (dvs[n] + cat(dv_i[n])).astype(BF16)))
                    new.append(_colsum((1.0 - sf) * common))
                for d, rows, df16, dq16, dv16 in results:
                    res_s[1 + d, rows, :] = df16
                    dq_s[d, rows, :] = dq16
                    dv_s[d, rows, :] = dv16
                per_dir = [sum(c for (d, *_), c in zip(loaded, new) if d == dd) for dd in range(2)]
                return gts[0], gts[1], dlb + jnp.concatenate(per_dir, axis=0)

            dlb_ref[...] = lax.fori_loop(0, nb // ubk, pass_back,
                                         (zero_state, zero_state, jnp.zeros((2, HEAD), F32)))[2]

            def pass_out(i, carry):
                rows = pl.ds(pl.multiple_of(i * HEAD, HEAD), HEAD)
                dq = dq_s[0, rows, :].astype(F32) + dq_s[1, rows, :].astype(F32)
                res_s[0, rows, :] = (dq * q_scale).astype(BF16)
                res_s[3, rows, :] = (dv_s[0, rows, :].astype(F32) + dv_s[1, rows, :].astype(F32)).astype(BF16)
                return carry

            lax.fori_loop(0, nb, pass_out, 0)

        out_ref[...] = res_s[p]

        @pl.when((h == heads - 1) & (p == 4))
        def _():
            exchange.finish(xin_refs, xout_refs, sem_refs)

    def col(pp):
        return pl.BlockSpec((t, HEAD), lambda h, p: (0, pp * heads + h))

    n_in = dproj.shape[1]
    any_spec = pl.BlockSpec(memory_space=pl.ANY)
    results = pl.pallas_call(
        body, name="hgrn_bwd", grid=(heads, 5),
        in_specs=[col(0), col(1), col(2), col(3), col(4),
                  pl.BlockSpec((t, HEAD), lambda h, p: (0, h)), pl.BlockSpec((t, HEAD), lambda h, p: (0, h)),
                  pl.BlockSpec((2, HEAD), lambda h, p: (0, h)), pl.BlockSpec((1, HEAD), lambda h, p: (0, 0)),
                  any_spec] + [any_spec] * (xin + 1),
        out_specs=[pl.BlockSpec((t, HEAD), lambda h, p: (0, p * heads + h)),
                   pl.BlockSpec((None, 1, HEAD), lambda h, p: (h, 0, 0)),
                   pl.BlockSpec((2, HEAD), lambda h, p: (0, h))] + [any_spec] * xout,
        out_shape=[jax.ShapeDtypeStruct((t, n_in), BF16), jax.ShapeDtypeStruct((heads, 1, HEAD), F32),
                   jax.ShapeDtypeStruct((2, width), F32)] + list(exchange.out_shapes),
        scratch_shapes=[pltpu.VMEM((t, HEAD), F32), pltpu.VMEM((2, t, HEAD), BF16), pltpu.VMEM((2, t, HEAD), BF16),
                        pltpu.VMEM((5, t, HEAD), BF16), pltpu.VMEM((2, nb, HEAD, HEAD), F32)] + list(exchange.sems),
        input_output_aliases={9: 0},
        compiler_params=_params(2),
    )(proj, proj, proj, proj, proj, osum, dout_a, lb, g_norm, dproj, *exchange.arrays, after)
    return results[0], results[1], results[2], results[3:]


def _adamw(w, g, m, v):
    m = ADAM_B1 * m + (1.0 - ADAM_B1) * g
    v = ADAM_B2 * v + (1.0 - ADAM_B2) * (g * g)
    m_hat = m / (1.0 - ADAM_B1 ** ADAM_STEP)
    v_hat = v / (1.0 - ADAM_B2 ** ADAM_STEP)
    delta = -ADAM_LR * (m_hat / (jnp.sqrt(v_hat) + ADAM_EPS) + ADAM_WD * w)
    return delta, m, v


def _adamw_big(name, me, w, m, v, g_parts, landing, axis):
    r, c = w.shape
    tr = _tile(r, 128)
    n_parts = len(g_parts)
    per = N_DEV // n_parts

    def body(me_ref, w_ref, m_ref, v_ref, *rest):
        g_refs, (l_ref, og_ref, od_ref, om_ref, ov_ref) = rest[:n_parts], rest[n_parts:]
        g = g_refs[0][...]
        for p in range(1, n_parts):
            g = jnp.where(me_ref[0] // per == p, g_refs[p][...], g)
        for s in range(N_DEV - 1):
            g = g + l_ref[s].astype(F32)
        og_ref[...] = g
        od_ref[...], om_ref[...], ov_ref[...] = _adamw(w_ref[...], g, m_ref[...], v_ref[...])

    shard = pl.BlockSpec((tr, c), lambda i, me_ref: (i, 0))
    if axis == 1:
        own = pl.BlockSpec((tr, c), lambda i, me_ref: (i, me_ref[0] % per))
    else:
        assert n_parts == 1
        own = pl.BlockSpec((tr, c), lambda i, me_ref: (me_ref[0] * (r // tr) + i, 0))
    grid_spec = pltpu.PrefetchScalarGridSpec(
        num_scalar_prefetch=1, grid=(r // tr,),
        in_specs=[shard, shard, shard] + [own] * n_parts + [pl.BlockSpec((N_DEV - 1, tr, c), lambda i, me_ref: (0, i, 0))],
        out_specs=[shard] * 4)
    return pl.pallas_call(
        body, name=name, grid_spec=grid_spec, out_shape=[jax.ShapeDtypeStruct((r, c), F32)] * 4,
        compiler_params=_params(1),
    )(me, w, m, v, *g_parts, landing)


def _adamw_ada(sct, dmod_mine, w, m, v):
    d, n = w.shape
    tr = _tile(d, 256)

    def body(s_ref, dm_ref, w_ref, m_ref, v_ref, og_ref, od_ref, om_ref, ov_ref):
        g = _dot(s_ref[...], dm_ref[...], precision=HIGHEST)
        og_ref[...] = g
        od_ref[...], om_ref[...], ov_ref[...] = _adamw(w_ref[...], g, m_ref[...], v_ref[...])

    blk = pl.BlockSpec((tr, n), lambda i: (i, 0))
    return pl.pallas_call(
        body, name="adamw_ada", grid=(d // tr,),
        in_specs=[pl.BlockSpec((tr, N_DEV), lambda i: (i, 0)), pl.BlockSpec((N_DEV, n), lambda i: (0, 0)), blk, blk, blk],
        out_specs=[blk] * 4, out_shape=[jax.ShapeDtypeStruct((d, n), F32)] * 4, compiler_params=_params(1),
    )(sct, dmod_mine, w, m, v)


def _adamw_small(gathered, w, m, v):
    def body(g_ref, w_ref, m_ref, v_ref, og_ref, od_ref, om_ref, ov_ref):
        g = g_ref[0]
        for s in range(1, N_DEV):
            g = g + g_ref[s]
        og_ref[...] = g
        od_ref[...], om_ref[...], ov_ref[...] = _adamw(w_ref[...], g, m_ref[...], v_ref[...])

    return pl.pallas_call(
        body, name="adamw_small", out_shape=[jax.ShapeDtypeStruct(w.shape, F32)] * 4,
        compiler_params=pltpu.CompilerParams(vmem_limit_bytes=VMEM_LIMIT),
    )(gathered, w, m, v)


def _adamw_lb(dlb_mine, lb_logits, m, v):
    def body(d_ref, l_ref, m_ref, v_ref, og_ref, od_ref, om_ref, ov_ref):
        dlb = d_ref[0]
        for s in range(1, N_DEV):
            dlb = dlb + d_ref[s]
        for dr in range(2):
            lb = _sigmoid(l_ref[dr][0:1, :] - l_ref[dr][1:2, :])
            d0 = dlb[dr:dr + 1] * lb * (1.0 - lb)
            g = jnp.concatenate([d0, -d0], axis=0)
            og_ref[dr] = g
            od_ref[dr], om_ref[dr], ov_ref[dr] = _adamw(l_ref[dr], g, m_ref[dr], v_ref[dr])

    return pl.pallas_call(body, name="adamw_lb", out_shape=[jax.ShapeDtypeStruct(lb_logits.shape, F32)] * 4,
                          )(dlb_mine, lb_logits, m, v)


def _rows(a, pad_to=8):
    flat = a.reshape(-1, LANE)
    pad = (-flat.shape[0]) % pad_to
    return jnp.pad(flat, ((0, pad), (0, 0))) if pad else flat


def kernel(x, c, w_ada, b_ada, g_pre_mix, g_post_mix, g_pre_ffn, g_post_ffn, w_in, lb_logits, g_hgrn_norm, w_a_out, g_sgu_norm, w_spatial, b_spatial, w_b_out, w_o, w_ff1, w_ff2, loss_target, m_w_ada, m_b_ada, m_g_pre_mix, m_g_post_mix, m_g_pre_ffn, m_g_post_ffn, m_w_in, m_lb_logits, m_g_hgrn_norm, m_w_a_out, m_g_sgu_norm, m_w_spatial, m_b_spatial, m_w_b_out, m_w_o, m_w_ff1, m_w_ff2, v_w_ada, v_b_ada, v_g_pre_mix, v_g_post_mix, v_g_pre_ffn, v_g_post_ffn, v_w_in, v_lb_logits, v_g_hgrn_norm, v_w_a_out, v_g_sgu_norm, v_w_spatial, v_b_spatial, v_w_b_out, v_w_o, v_w_ff1, v_w_ff2):
    t, d = x.shape[1], x.shape[2]
    n_in = w_in.shape[2] * N_DEV
    width = (n_in - 2 * d) // 7
    heads = width // HEAD
    assert heads == N_DEV and width % LANE == 0
    d_ff = w_ff1.shape[2] * N_DEV
    n_ada = w_ada.shape[2]
    me = _dev_index()
    me_arr = me.reshape(1).astype(jnp.int32)
    x2, tgt = x[0], loss_target[0]

    big = [w_in[0], w_a_out[0], w_b_out[0], w_o[0], w_ff1[0], w_ff2[0]]
    big_axes = [1, 1, 1, 0, 1, 0]
    big_names = ["w_in", "w_a_out", "w_b_out", "w_o", "w_ff1", "w_ff2"]
    w_in16 = _cast_bf16("cast_w_in", big[0])
    own_parts = [_cast_into_full("cast_" + nm, me_arr, w, ax) for nm, w, ax in zip(big_names[1:], big[1:], big_axes[1:])]

    c_rows = d // LANE
    small = _all_gather_small("gather_c_lb", _prep_small(c[0:1], lb_logits))
    sc_all = small[:, :c_rows, :].reshape(N_DEV, d)
    lb = jnp.transpose(small[:, c_rows:c_rows + 2, :], (1, 0, 2)).reshape(2, width)
    b_shard = lax.dynamic_slice_in_dim(b_ada, me * n_ada, n_ada, axis=1)
    mod_sh = _mod_shard(sc_all, w_ada[0], b_shard)
    mod_all = _all_gather_small("gather_mod", _rows(mod_sh))
    mod_all = mod_all[:, :N_DEV * n_ada // LANE, :].reshape(N_DEV, N_DEV, n_ada)
    mod6 = lax.dynamic_index_in_dim(mod_all, me, axis=1, keepdims=False).reshape(N_MOD, d)
    sh1, sc1, gt1, sh2, sc2, gt2 = [mod6[i:i + 1] for i in range(N_MOD)]

    a1 = _norm_mod(x2, g_pre_mix, sh1, sc1)
    tm = _tile(t, 512)

    def store_bf16(acc, i, j, extra_refs, out_refs, rows):
        out_refs[0][...] = acc.astype(BF16)

    xq, yq, cq = lax.axis_index("x"), lax.axis_index("y"), lax.axis_index("c")
    chips = [(1 - xq, yq), (xq, 1 - yq), (1 - xq, 1 - yq)]
    order = jnp.stack([me, 4 * xq + 2 * yq + 1 - cq]
                      + [4 * a + 2 * b + cq for a, b in chips[:2]] + [4 * a + 2 * b + 1 - cq for a, b in chips[:2]]
                      + [4 * chips[2][0] + 2 * chips[2][1] + cq, 4 * chips[2][0] + 2 * chips[2][1] + 1 - cq]).astype(jnp.int32)
    proj, wf_in = _proj_gather(a1, w_in16, order)

    proj, own_parts = lax.optimization_barrier((proj, own_parts))
    gathers = {}
    for key, lo, hi in (("mid", 1, 4), ("ff1", 4, 5), ("ff2", 5, 6)):
        far, near = _gather_stage_plans(own_parts[lo - 1:hi - 1], big_axes[lo:hi])
        gathers[key] = [far, near, _split_start("gather_%s_start" % key, far, landing=own_parts[lo - 1:hi - 1])]

    def pass_on(key, after):
        far, near, (sems, thru, _) = gathers[key]
        parts = _split_wait("gather_%s_wait" % key, far, sems, thru, after)[1]
        gathers[key].append(_split_start("pass_%s_start" % key, near, landing=list(parts)))
        return gathers[key][3][2]

    def gathered_weights(key, after):
        near, (sems, thru, _) = gathers[key][1], gathers[key][3]
        return _split_wait("pass_%s_wait" % key, near, sems, thru, after)[1]

    out_a, osum, _ = _hgrn_fwd(proj, lb, g_hgrn_norm, width, _NO_EXCHANGE,
                               after=[gathers[key][2][2] for key in ("mid", "ff1", "ff2")])
    passed_mid = pass_on("mid", out_a)
    z_block = 5
    bst = b_spatial[0].T
    out_b = _sgu_fwd(proj, g_sgu_norm, w_spatial[0], bst, width, z_block)
    wf_a, wf_b, wf_o = gathered_weights("mid", out_b)

    tn_d = _tile(d, 512)
    blk_d = ((tm, tn_d), lambda i, j: (i, j))
    y_a, = _mm("y_a", out_a, wf_a, _NN, t, d, width, tm, tn_d, width, _after(passed_mid),
               [(jax.ShapeDtypeStruct((t, d), BF16),) + blk_d], store_bf16)
    ga_blk = (5 * width + 2 * width) // tn_d
    gb_blk = ga_blk + d // tn_d

    def merge(acc, i, j, extra_refs, out_refs, rows):
        ga, gb, ya = extra_refs
        out_refs[0][...] = acc.astype(BF16)
        out_refs[1][...] = (_sigmoid(ga[...]) * ya[...].astype(F32) + _sigmoid(gb[...]) * acc).astype(BF16)

    y_b, merged = _mm("y_b_merge", out_b, wf_b, _NN, t, d, width, tm, tn_d, width,
                      [(proj, (tm, tn_d), lambda i, j: (i, ga_blk + j)), (proj, (tm, tn_d), lambda i, j: (i, gb_blk + j)),
                       (y_a,) + blk_d],
                      [(jax.ShapeDtypeStruct((t, d), BF16),) + blk_d, (jax.ShapeDtypeStruct((t, d), BF16),) + blk_d], merge)

    tr = _tile(t, 512)
    rc = 32 if tr % 32 == 0 else None
    row_d = ((tr, d), lambda i, j: (i, 0))
    vec_d = ((1, d), lambda i, j: (0, 0))

    passed_ff1 = pass_on("ff1", merged)

    def post_mix(acc, i, j, extra_refs, out_refs, rows):
        x_r, gt1_r, g2_r, g3_r, sc2_r, sh2_r = extra_refs[:6]
        h1 = x_r[rows, :] + gt1_r[...] * (acc * _rms(acc) * g2_r[...])
        out_refs[0][rows, :] = acc.astype(BF16)
        out_refs[1][rows, :] = h1
        out_refs[2][rows, :] = ((h1 * _rms(h1) * g3_r[...]) * (1.0 + sc2_r[...]) + sh2_r[...]).astype(BF16)

    mo, h1, a2 = _mm("w_o_post_mix", merged, wf_o, _NN, t, d, d, tr, d, d,
                     [(x2,) + row_d, (gt1,) + vec_d, (g_post_mix,) + vec_d, (g_pre_ffn,) + vec_d, (sc2,) + vec_d, (sh2,) + vec_d]
                     + _after(passed_ff1),
                     [(jax.ShapeDtypeStruct((t, d), BF16),) + row_d, (jax.ShapeDtypeStruct((t, d), F32),) + row_d,
                      (jax.ShapeDtypeStruct((t, d), BF16),) + row_d], post_mix, row_chunk=rc)

    tn_f = _tile(d_ff, 1024)
    blk_f = ((tm, tn_f), lambda i, j: (i, j))

    def relu_sq(acc, i, j, extra_refs, out_refs, rows):
        r = jnp.maximum(acc, 0.0)
        out_refs[0][...] = acc.astype(BF16)
        out_refs[1][...] = (r * r).astype(BF16)

    wf_1, = gathered_weights("ff1", a2)
    hff, act = _mm(
        "ff1", a2, wf_1, _NN, t, d_ff, d, tm, tn_f, d, [],
        [(jax.ShapeDtypeStruct((t, d_ff), BF16),) + blk_f, (jax.ShapeDtypeStruct((t, d_ff), BF16),) + blk_f], relu_sq)
    pass_on("ff2", hff)
    wf_2, = gathered_weights("ff2", act)

    sums_d = ((8, d), lambda i, j: (0, 0))

    def zero_first(sums_r, i, rows):
        if rows.start in (None, 0):
            @pl.when(i == 0)
            def _():
                sums_r[...] = jnp.zeros_like(sums_r)

    def loss_head(acc, i, j, extra_refs, out_refs, rows):
        h1_r, tgt_r, gt2_r, g4_r = extra_refs
        dy_r, dff_r, sums_r = out_refs
        r4 = _rms(acc)
        ffn = acc * r4
        n4 = ffn * g4_r[...]
        err = h1_r[rows, :] + gt2_r[...] * n4 - tgt_r[rows, :]
        dy = err * (1.0 / d)
        dy_r[rows, :] = dy.astype(BF16)
        dn4 = dy * gt2_r[...]
        dffn = dn4 * g4_r[...]
        dff_r[rows, :] = (r4 * (dffn - ffn * jnp.mean(dffn * ffn, axis=-1, keepdims=True))).astype(BF16)
        zero_first(sums_r, i, rows)

        sums_r[0:1, :] += _colsum(err * err)
        sums_r[1:2, :] += _colsum(dy * n4)
        sums_r[2:3, :] += _colsum(dn4 * ffn)

    tk_f = _tile(d_ff, 1024)
    dy, dff, sums_f = _mm("ff2_loss", act, wf_2, _NN, t, d, d_ff, tr, d, tk_f,
                          [(h1,) + row_d, (tgt,) + row_d, (gt2,) + vec_d, (g_post_ffn,) + vec_d],
                          [(jax.ShapeDtypeStruct((t, d), BF16),) + row_d, (jax.ShapeDtypeStruct((t, d), BF16),) + row_d,
                           (jax.ShapeDtypeStruct((8, d), F32),) + sums_d], loss_head, row_chunk=rc)
    loss_mine = (0.5 / d) * jnp.sum(sums_f[0])

    def relu_sq_bwd(acc, i, j, extra_refs, out_refs, rows):
        out_refs[0][...] = (acc * (2.0 * jnp.maximum(extra_refs[0][...].astype(F32), 0.0))).astype(BF16)

    dhff, = _mm("d_hff", dff, wf_2, _NT, t, d_ff, d, tm, tn_f, d, [(hff,) + blk_f],
                [(jax.ShapeDtypeStruct((t, d_ff), BF16),) + blk_f], relu_sq_bwd)
    scatters = {}

    def send_grads(key, grads16, axes):
        plan = _scatter_plan(grads16, axes)
        scatters[key] = (plan,) + _split_start("scatter_%s_start" % key, plan)
        return scatters[key][3]

    def received_grads(key, after):
        plan, sems, thru, _ = scatters[key]
        return _split_wait("scatter_%s_wait" % key, plan, sems, thru, after)[1]

    gw_ff2, gw_ff2_16 = _grad_w("grad_w_ff2", act, dff)
    sent_ff2 = send_grads("ff2", [gw_ff2_16], big_axes[5:6])
    gw_ff1, gw_ff1_16 = _grad_w("grad_w_ff1", a2, dhff, token=sent_ff2)
    sent_ff1 = send_grads("ff1", [gw_ff1_16], big_axes[4:5])

    def pre_ffn_bwd(acc, i, j, extra_refs, out_refs, rows):
        h1_r, dy_r, mo_r, sc2_r, g3_r, gt1_r, g2_r = extra_refs[:7]
        dh1_r, dmo_r, sums_r = out_refs
        h1v = h1_r[rows, :]
        r3 = _rms(h1v)
        h1n = h1v * r3
        dn3 = acc * (1.0 + sc2_r[...])
        dh1n = dn3 * g3_r[...]
        dh1 = dy_r[rows, :].astype(F32) + r3 * (dh1n - h1n * jnp.mean(dh1n * h1n, axis=-1, keepdims=True))
        dh1_r[rows, :] = dh1.astype(BF16)
        mov = mo_r[rows, :].astype(F32)
        r2 = _rms(mov)
        mon = mov * r2
        dn2 = dh1 * gt1_r[...]
        dmon = dn2 * g2_r[...]
        dmo_r[rows, :] = (r2 * (dmon - mon * jnp.mean(dmon * mon, axis=-1, keepdims=True))).astype(BF16)
        zero_first(sums_r, i, rows)

        sums_r[0:1, :] += _colsum(acc)
        sums_r[1:2, :] += _colsum(acc * (h1n * g3_r[...]))
        sums_r[2:3, :] += _colsum(dn3 * h1n)
        sums_r[3:4, :] += _colsum(dh1 * (mon * g2_r[...]))
        sums_r[4:5, :] += _colsum(dn2 * mon)

    dh1, dmo, sums_m = _mm("d_a2_pre_ffn", dhff, wf_1, _NT, t, d, d_ff, tr, d, tk_f,
                           [(h1,) + row_d, (dy,) + row_d, (mo,) + row_d, (sc2,) + vec_d, (g_pre_ffn,) + vec_d,
                            (gt1,) + vec_d, (g_post_mix,) + vec_d] + _after(sent_ff1),
                           [(jax.ShapeDtypeStruct((t, d), BF16),) + row_d, (jax.ShapeDtypeStruct((t, d), BF16),) + row_d,
                            (jax.ShapeDtypeStruct((8, d), F32),) + sums_d], pre_ffn_bwd, row_chunk=rc)
    gw_o, gw_o_16 = _grad_w("grad_w_o", merged, dmo)

    n_j = d // tn_d

    def merge_bwd_body(dmo_ref, wo_ref, ga_ref, gb_ref, ya_ref, yb_ref, dya_ref, dyb_ref, dproj_ref, acc_s):
        g = pl.program_id(2)

        @pl.when(g == 0)
        def _():
            dm = _dot(dmo_ref[...], wo_ref[...], _NT)
            acc_s[...] = dm
            sa = _sigmoid(ga_ref[...])
            dya_ref[...] = (dm * sa).astype(BF16)
            dproj_ref[...] = (dm * ya_ref[...].astype(F32) * sa * (1.0 - sa)).astype(BF16)

        @pl.when(g == 1)
        def _():
            dm = acc_s[...]
            sb = _sigmoid(gb_ref[...])
            dyb_ref[...] = (dm * sb).astype(BF16)
            dproj_ref[...] = (dm * yb_ref[...].astype(F32) * sb * (1.0 - sb)).astype(BF16)

    tile3 = pl.BlockSpec((tm, tn_d), lambda i, j, g: (i, j))
    dy_a, dy_b, dproj = pl.pallas_call(
        merge_bwd_body, name="d_merged", grid=(t // tm, n_j, 2),
        in_specs=[pl.BlockSpec((tm, d), lambda i, j, g: (i, 0)), pl.BlockSpec((tn_d, d), lambda i, j, g: (j, 0)),
                  pl.BlockSpec((tm, tn_d), lambda i, j, g: (i, ga_blk + j)),
                  pl.BlockSpec((tm, tn_d), lambda i, j, g: (i, gb_blk + j)), tile3, tile3],
        out_specs=[tile3, tile3, pl.BlockSpec((tm, tn_d), lambda i, j, g: (i, ga_blk + g * n_j + j))],
        out_shape=[jax.ShapeDtypeStruct((t, d), BF16), jax.ShapeDtypeStruct((t, d), BF16),
                   jax.ShapeDtypeStruct((t, n_in), BF16)],
        scratch_shapes=[pltpu.VMEM((tm, tn_d), F32)], compiler_params=_params(3),
    )(dmo, wf_o, proj, proj, y_a, y_b)

    tn_w = _tile(width, 512)
    blk_w = ((tm, tn_w), lambda i, j: (i, j))
    dout_a, = _mm("d_out_a", dy_a, wf_a, _NT, t, width, d, tm, tn_w, d, [],
                  [(jax.ShapeDtypeStruct((t, width), BF16),) + blk_w], store_bf16)
    dout_b, = _mm("d_out_b", dy_b, wf_b, _NT, t, width, d, tm, tn_w, d, [],
                  [(jax.ShapeDtypeStruct((t, width), BF16),) + blk_w], store_bf16)
    gw_a, gw_a_16 = _grad_w("grad_w_a_out", out_a, dy_a)
    gw_b, gw_b_16 = _grad_w("grad_w_b_out", out_b, dy_b)

    w_st = jnp.swapaxes(w_spatial[0], 1, 2)
    dproj, dg_sgu, dw_sp, dbst = _sgu_bwd(proj, dout_b, dproj, g_sgu_norm, w_spatial[0], w_st, bst, width, z_block)
    sent_mid = send_grads("mid", [gw_a_16, gw_b_16, gw_o_16], big_axes[1:4])
    dproj, dgh_heads, dlb, _ = _hgrn_bwd(proj, osum, dout_a, dproj, lb, g_hgrn_norm, width, _NO_EXCHANGE,
                                         after=sent_mid)
    gw_in, gw_in_16 = _grad_w("grad_w_in", a1, dproj)
    sent_in = send_grads("in", [gw_in_16], big_axes[:1])

    def pre_mix_bwd(acc, i, j, extra_refs, out_refs, rows):
        x_r, dh1_r, sc1_r, g1_r = extra_refs[:4]
        dx_r, sums_r = out_refs
        xv = x_r[rows, :]
        r1 = _rms(xv)
        xn = xv * r1
        dn1 = acc * (1.0 + sc1_r[...])
        dxn = dn1 * g1_r[...]
        dx_r[rows, :] = dh1_r[rows, :].astype(F32) + r1 * (dxn - xn * jnp.mean(dxn * xn, axis=-1, keepdims=True))
        zero_first(sums_r, i, rows)

        sums_r[0:1, :] += _colsum(acc)
        sums_r[1:2, :] += _colsum(acc * (xn * g1_r[...]))
        sums_r[2:3, :] += _colsum(dn1 * xn)

    tk_in = _tile(n_in, 1024)
    grad_x, sums_x = _mm(
        "d_a1_pre_mix", dproj, wf_in, _NT, t, d, n_in, tr, d, tk_in,
        [(x2,) + row_d, (dh1,) + row_d, (sc1,) + vec_d, (g_pre_mix,) + vec_d] + _after(sent_in),
        [(jax.ShapeDtypeStruct((t, d), F32),) + row_d, (jax.ShapeDtypeStruct((8, d), F32),) + sums_d],
        pre_mix_bwd, row_chunk=rc)

    dmod = jnp.concatenate([sums_x[0:2], sums_m[3:4], sums_m[0:2], sums_f[1:2]], axis=0).reshape(N_DEV, n_ada // LANE, LANE)
    ada_rows = -(-(n_ada // LANE) // 8) * 8
    dmod = jnp.pad(dmod, ((0, 0), (0, ada_rows - n_ada // LANE), (0, 0))).reshape(N_DEV * ada_rows, LANE)
    parts = [dmod, _rows(sums_x[2:3]), _rows(sums_m[4:5]), _rows(sums_m[2:3]), _rows(sums_f[2:3]),
             _rows(jnp.sum(dgh_heads, axis=0)), _rows(dg_sgu), _rows(dw_sp), _rows(dbst.T)]
    n_params = sum(p.shape[0] for p in parts)
    parts.append(jnp.full((8, LANE), loss_mine, F32))
    n_common = n_params + 8
    payload = jnp.concatenate(parts + [_rows(dlb)], axis=0)

    moms = [m_w_in, m_w_a_out, m_w_b_out, m_w_o, m_w_ff1, m_w_ff2]
    vars_ = [v_w_in, v_w_a_out, v_w_b_out, v_w_o, v_w_ff1, v_w_ff2]
    big_out = {}

    def big_update(nm, g_full, landing):
        k = big_names.index(nm)
        outs = _adamw_big("adamw_" + nm, me_arr, big[k], moms[k][0], vars_[k][0], g_full, landing, big_axes[k])
        big_out[nm] = [o[None] for o in outs]
        return outs[0]

    land_ff2, = received_grads("ff2", grad_x)
    done = big_update("w_ff2", [gw_ff2], land_ff2)
    land_ff1, = received_grads("ff1", done)
    done = big_update("w_ff1", [gw_ff1], land_ff1)
    land_a, land_b, land_o = received_grads("mid", done)
    big_update("w_a_out", [gw_a], land_a)
    big_update("w_b_out", [gw_b], land_b)
    done = big_update("w_o", [gw_o], land_o)

    payload, _ = lax.optimization_barrier((payload, done))
    gathered = _all_gather_small("gather_small_grads", payload)

    dmod_mine = lax.dynamic_slice_in_dim(gathered[:, :N_DEV * ada_rows, :].reshape(N_DEV, N_DEV, ada_rows * LANE),
                                         me, 1, axis=1)[:, 0, :n_ada]
    ada_out = [o[None] for o in _adamw_ada(sc_all.T, dmod_mine, w_ada[0], m_w_ada[0], v_w_ada[0])]

    def pack(b_, g1_, g2_, g3_, g4_, gh_, gs_, ws_, bs_):
        b3 = b_.reshape(N_DEV, n_ada // LANE, LANE)
        b3 = jnp.pad(b3, ((0, 0), (0, ada_rows - n_ada // LANE), (0, 0))).reshape(N_DEV * ada_rows, LANE)
        return jnp.concatenate([b3, _rows(g1_), _rows(g2_), _rows(g3_), _rows(g4_), _rows(gh_), _rows(gs_),
                                _rows(ws_), _rows(bs_), jnp.zeros((8, LANE), F32)], axis=0)

    small_w = (b_ada, g_pre_mix, g_post_mix, g_pre_ffn, g_post_ffn, g_hgrn_norm, g_sgu_norm, w_spatial, b_spatial)
    small_m = (m_b_ada, m_g_pre_mix, m_g_post_mix, m_g_pre_ffn, m_g_post_ffn, m_g_hgrn_norm, m_g_sgu_norm, m_w_spatial, m_b_spatial)
    small_v = (v_b_ada, v_g_pre_mix, v_g_post_mix, v_g_pre_ffn, v_g_post_ffn, v_g_hgrn_norm, v_g_sgu_norm, v_w_spatial, v_b_spatial)
    packed = _adamw_small(gathered[:, :n_common, :], pack(*small_w), pack(*small_m), pack(*small_v))

    def unpack(slab):
        outs, at = [], 0
        b3 = slab[:N_DEV * ada_rows].reshape(N_DEV, ada_rows, LANE)[:, :n_ada // LANE, :]
        outs.append(b3.reshape(b_ada.shape))
        at = N_DEV * ada_rows
        for ref in small_w[1:]:
            n_el = ref.size
            n_r = -(-(n_el // LANE) // 8) * 8
            outs.append(slab[at:at + n_el // LANE].reshape(ref.shape))
            at += n_r
        return outs

    small_out = [unpack(s) for s in packed]
    loss = packed[0][n_params, 0]

    dlb_all = gathered[:, n_common:n_common + 2 * heads, :].reshape(N_DEV, 2, heads, LANE)
    dlb_mine = lax.dynamic_index_in_dim(dlb_all, me, axis=2, keepdims=False)
    lb_out = _adamw_lb(dlb_mine, lb_logits, m_lb_logits, v_lb_logits)

    land_in, = received_grads("in", ada_out[0])
    big_update("w_in", [gw_in], land_in)

    order = ["w_ada", "b_ada", "g_pre_mix", "g_post_mix", "g_pre_ffn", "g_post_ffn", "w_in", "lb_logits", "g_hgrn_norm",
             "w_a_out", "g_sgu_norm", "w_spatial", "b_spatial", "w_b_out", "w_o", "w_ff1", "w_ff2"]
    small_names = ["b_ada", "g_pre_mix", "g_post_mix", "g_pre_ffn", "g_post_ffn", "g_hgrn_norm", "g_sgu_norm", "w_spatial", "b_spatial"]

    def leaf(kind, nm):
        if nm == "w_ada":
            return ada_out[kind]
        if nm == "lb_logits":
            return lb_out[kind]
        if nm in big_out:
            return big_out[nm][kind]
        return small_out[kind][small_names.index(nm)]

    result = [loss, grad_x[None]]
    for kind in range(4):
        result += [leaf(kind, nm) for nm in order]
    return tuple(result)
```

```python
import functools
import math

import jax
import jax.numpy as jnp
from jax import lax
from jax.experimental import pallas as pl
from jax.experimental.pallas import tpu as pltpu

F32 = jnp.float32
BF16 = jnp.bfloat16
MESH = pl.DeviceIdType.MESH
HIGHEST = lax.Precision.HIGHEST

N_DEV = 8
HEAD = 128
A_CHUNK = 32
N_MOD = 6
EPS = 1e-6
LANE = 128
VMEM_LIMIT = 60 * 1024 * 1024

ADAM_LR = 0.001
ADAM_B1 = 0.9
ADAM_B2 = 0.999
ADAM_EPS = 1e-08
ADAM_WD = 0.01
ADAM_STEP = 10

_NN = (((1,), (0,)), ((), ()))
_NT = (((1,), (1,)), ((), ()))
_TN = (((0,), (0,)), ((), ()))


def _dot(a, b, dims=_NN, precision=None):
    return lax.dot_general(a, b, dims, preferred_element_type=F32, precision=precision)


def _bdot(a, b, dims=_NN):
    return _dot(a.astype(BF16), b.astype(BF16), dims)


def _params(n_grid):
    return pltpu.CompilerParams(dimension_semantics=("arbitrary",) * n_grid, vmem_limit_bytes=VMEM_LIMIT)


def _dev_index():
    return lax.axis_index("x") * 4 + lax.axis_index("y") * 2 + lax.axis_index("c")


def _dev_coords(i):
    return (i // 4, (i // 2) % 2, i % 2)


def _sigmoid(x):
    return 1.0 / (1.0 + jnp.exp(-x))


def _erf(x):
    ax = jnp.abs(x)
    t = 1.0 / (1.0 + 0.3275911 * ax)
    poly = ((((1.061405429 * t - 1.453152027) * t + 1.421413741) * t - 0.284496736) * t + 0.254829592) * t
    y = 1.0 - poly * jnp.exp(-ax * ax)
    return jnp.where(x < 0, -y, y)


def _gelu_and_grad(x):
    cdf = 0.5 * (1.0 + _erf(x * (2.0 ** -0.5)))
    pdf = jnp.exp(-0.5 * x * x) * (1.0 / math.sqrt(2.0 * math.pi))
    return x * cdf, cdf + x * pdf


def _rms(x):
    return lax.rsqrt(jnp.mean(x * x, axis=-1, keepdims=True) + EPS)


def _colsum(x):
    return jnp.sum(x, axis=0, keepdims=True)


def _tile(n, want):
    if n <= want:
        return n
    t = (want // LANE) * LANE
    while n % t:
        t -= LANE
    assert t > 0, (n, want)
    return t


def _all_gather_small(name, payload):
    rows = payload.shape[0]

    def body(p_ref, out_ref, send_sems, recv_sems, local_sem):
        me = _dev_index()
        mine = pltpu.make_async_copy(p_ref, out_ref.at[me], local_sem)
        mine.start()
        sends = []
        for r in range(1, N_DEV):
            peer = (me + r) % N_DEV
            cp = pltpu.make_async_remote_copy(
                src_ref=p_ref, dst_ref=out_ref.at[me], send_sem=send_sems.at[r - 1], recv_sem=recv_sems.at[r - 1],
                device_id=_dev_coords(peer), device_id_type=MESH)
            cp.start()
            sends.append(cp)
        for r in range(1, N_DEV):
            src = (me + N_DEV - r) % N_DEV
            pltpu.make_async_remote_copy(
                src_ref=p_ref, dst_ref=out_ref.at[src], send_sem=send_sems.at[r - 1], recv_sem=recv_sems.at[r - 1],
                device_id=_dev_coords(src), device_id_type=MESH).wait_recv()
        for cp in sends:
            cp.wait_send()
        mine.wait()

    return pl.pallas_call(
        body, name=name,
        out_shape=jax.ShapeDtypeStruct((N_DEV, rows, LANE), F32),
        in_specs=[pl.BlockSpec(memory_space=pltpu.VMEM)],
        out_specs=pl.BlockSpec(memory_space=pltpu.VMEM),
        scratch_shapes=[pltpu.SemaphoreType.DMA((N_DEV - 1,)), pltpu.SemaphoreType.DMA((N_DEV - 1,)),
                        pltpu.SemaphoreType.DMA],
        compiler_params=pltpu.CompilerParams(vmem_limit_bytes=VMEM_LIMIT),
    )(payload)


def _region(ref, dev, axis, n):
    start = pl.multiple_of(dev * n, LANE if axis == 1 else 16)
    return ref.at[:, pl.ds(start, n)] if axis == 1 else ref.at[pl.ds(start, n), :]


class _Exchange:
    def __init__(self, arrays, out_shapes, sems, start, finish):
        self.arrays, self.out_shapes, self.sems, self.start, self.finish = arrays, out_shapes, sems, start, finish


def _scatter_plan(grads, axes):
    n_w = len(grads)
    lands = []
    for g, ax in zip(grads, axes):
        shp = (g.shape[0], g.shape[1] // N_DEV) if ax == 1 else (g.shape[0] // N_DEV, g.shape[1])
        lands.append(jax.ShapeDtypeStruct((N_DEV - 1,) + shp, BF16))
    widths = [ld.shape[1 + ax] for ld, ax in zip(lands, axes)]

    def copy(w, r, g_refs, l_refs, sems, block, to):
        return pltpu.make_async_remote_copy(
            src_ref=_region(g_refs[w], block, axes[w], widths[w]), dst_ref=l_refs[w].at[r - 1],
            send_sem=sems[0].at[w * (N_DEV - 1) + r - 1], recv_sem=sems[1].at[w * (N_DEV - 1) + r - 1],
            device_id=_dev_coords(to), device_id_type=MESH)

    def start(g_refs, l_refs, sems):
        me = _dev_index()
        for w in range(n_w):
            for r in range(1, N_DEV):
                owner = (me + r) % N_DEV
                copy(w, r, g_refs, l_refs, sems, owner, owner).start()

    def finish(g_refs, l_refs, sems):
        me = _dev_index()
        for w in range(n_w):
            for r in range(1, N_DEV):
                copy(w, r, g_refs, l_refs, sems, me, (me + N_DEV - r) % N_DEV).wait_recv()
        for w in range(n_w):
            for r in range(1, N_DEV):
                copy(w, r, g_refs, l_refs, sems, me, (me + r) % N_DEV).wait_send()

    sems = [pltpu.SemaphoreType.DMA((n_w * (N_DEV - 1),)), pltpu.SemaphoreType.DMA((n_w * (N_DEV - 1),))]
    return _Exchange(list(grads), lands, sems, start, finish)


def _places():
    x, y, c = lax.axis_index("x"), lax.axis_index("y"), lax.axis_index("c")
    return (x, y, c), (x, y, 1 - c), [(1 - x, y), (x, 1 - y), (1 - x, 1 - y)]


def _place_index(p):
    return p[0] * 4 + p[1] * 2 + p[2]


def _gather_stage_plans(fulls, axes):
    n_w = len(fulls)
    widths = [f.shape[ax] // N_DEV for f, ax in zip(fulls, axes)]
    shapes = [jax.ShapeDtypeStruct(f.shape, f.dtype) for f in fulls]

    def copy(per, w, k, f_refs, sems, block, to):
        part = _region(f_refs[w], _place_index(block), axes[w], widths[w])
        return pltpu.make_async_remote_copy(
            src_ref=part, dst_ref=part, send_sem=sems[0].at[w * per + k], recv_sem=sems[1].at[w * per + k],
            device_id=to, device_id_type=MESH)

    def start1(_, f_refs, sems):
        me, sib, chips = _places()
        for w in range(n_w):
            copy(4, w, 0, f_refs, sems, me, sib).start()
            for j, chip in enumerate(chips):
                copy(4, w, 1 + j, f_refs, sems, me, (*chip, me[2])).start()

    def finish1(_, f_refs, sems):
        me, sib, chips = _places()
        for w in range(n_w):
            copy(4, w, 0, f_refs, sems, sib, me).wait_recv()
            for j, chip in enumerate(chips):
                copy(4, w, 1 + j, f_refs, sems, (*chip, me[2]), me).wait_recv()
        for w in range(n_w):
            for k in range(4):
                copy(4, w, k, f_refs, sems, me, sib).wait_send()

    def start2(_, f_refs, sems):
        me, sib, chips = _places()
        for w in range(n_w):
            for j, chip in enumerate(chips):
                copy(3, w, j, f_refs, sems, (*chip, me[2]), sib).start()

    def finish2(_, f_refs, sems):
        me, sib, chips = _places()
        for w in range(n_w):
            for j, chip in enumerate(chips):
                copy(3, w, j, f_refs, sems, (*chip, sib[2]), me).wait_recv()
        for w in range(n_w):
            for j, chip in enumerate(chips):
                copy(3, w, j, f_refs, sems, (*chip, me[2]), sib).wait_send()

    sems1 = [pltpu.SemaphoreType.DMA((n_w * 4,)), pltpu.SemaphoreType.DMA((n_w * 4,))]
    sems2 = [pltpu.SemaphoreType.DMA((n_w * 3,)), pltpu.SemaphoreType.DMA((n_w * 3,))]
    return _Exchange([], shapes, sems1, start1, finish1), _Exchange([], shapes, sems2, start2, finish2)


_NO_EXCHANGE = _Exchange([], [], [], lambda i, o, s: None, lambda i, o, s: None)

_HBM = pl.BlockSpec(memory_space=pltpu.HBM)
_SEM = pl.BlockSpec(memory_space=pltpu.SEMAPHORE)
_EFFECT = pltpu.SideEffectType.DATAFLOW_SIDE_EFFECTING


def _split_start(name, plan, landing=None):
    n_in, n_out, n_sem = len(plan.arrays), len(plan.out_shapes), len(plan.sems)

    def body(*refs):
        ins, lands = refs[:n_in], refs[n_in:n_in + n_out]
        sems = refs[n_in + n_out:n_in + n_out + n_sem]
        token = refs[-1]
        plan.start(ins, lands, sems)
        token[...] = jnp.zeros_like(token)

    hbm = lambda a: pltpu.HBM(a.shape, a.dtype)
    results = pl.pallas_call(
        body, name=name,
        out_shape=tuple(plan.sems) + tuple(hbm(a) for a in plan.arrays) + tuple(hbm(a) for a in plan.out_shapes)
        + (jax.ShapeDtypeStruct((8, LANE), F32),),
        in_specs=(_HBM,) * (n_in + n_out),
        out_specs=(_SEM,) * n_sem + (_HBM,) * (n_in + n_out) + (pl.BlockSpec(memory_space=pltpu.VMEM),),
        input_output_aliases={i: n_sem + i for i in range(n_in + n_out)},
        compiler_params=pltpu.CompilerParams(has_side_effects=_EFFECT),
    )(*[pltpu.with_memory_space_constraint(a, pltpu.HBM) for a in plan.arrays],
      *[pltpu.with_memory_space_constraint(a, pltpu.HBM)
        for a in (landing if landing is not None else [lax.empty(a.shape, a.dtype) for a in plan.out_shapes])])
    return results[:n_sem], results[n_sem:n_sem + n_in + n_out], results[-1]


def _split_wait(name, plan, sems, thru, after):
    n_in, n_out, n_sem = len(plan.arrays), len(plan.out_shapes), len(plan.sems)

    def body(*refs):
        ins, lands = refs[:n_in], refs[n_in:n_in + n_out]
        sem_refs = refs[n_in + n_out:n_in + n_out + n_sem]
        plan.finish(ins, lands, sem_refs)

    hbm = lambda a: pltpu.HBM(a.shape, a.dtype)
    results = pl.pallas_call(
        body, name=name,
        out_shape=tuple(hbm(a) for a in plan.arrays) + tuple(hbm(a) for a in plan.out_shapes),
        in_specs=(_HBM,) * (n_in + n_out) + (_SEM,) * n_sem + (pl.BlockSpec(memory_space=pl.ANY),),
        out_specs=(_HBM,) * (n_in + n_out),
        input_output_aliases={i: i for i in range(n_in + n_out)},
        compiler_params=pltpu.CompilerParams(has_side_effects=_EFFECT),
    )(*thru, *sems, after)
    return results[:n_in], results[n_in:]


def _cast_into_full(name, me, w, axis):
    r, c = w.shape
    tr = _tile(r, 256)
    if axis == 1:
        shape, place = (r, c * N_DEV), pl.BlockSpec((tr, c), lambda i, me_ref: (i, me_ref[0]))
    else:
        shape, place = (r * N_DEV, c), pl.BlockSpec((tr, c), lambda i, me_ref: (me_ref[0] * (r // tr) + i, 0))

    def body(me_ref, w_ref, o_ref):
        o_ref[...] = w_ref[...].astype(BF16)

    grid_spec = pltpu.PrefetchScalarGridSpec(
        num_scalar_prefetch=1, grid=(r // tr,),
        in_specs=[pl.BlockSpec((tr, c), lambda i, me_ref: (i, 0))], out_specs=place)
    return pl.pallas_call(body, name=name, grid_spec=grid_spec, out_shape=jax.ShapeDtypeStruct(shape, BF16),
                          compiler_params=_params(1))(me, w)


def _mm(name, a, b, dims, m, n, k, tm, tn, tk, extras, outs, epilogue, row_chunk=None, exchange=None,
        a_col_block=0):
    ni, nj, nk = m // tm, n // tn, k // tk
    ne, no = len(extras), len(outs)
    xin = len(exchange.arrays) if exchange else 0
    xout = len(exchange.out_shapes) if exchange else 0
    if dims == _TN:
        a_spec = pl.BlockSpec((tk, tm), lambda i, j, kk: (kk, i + a_col_block))
    else:
        a_spec = pl.BlockSpec((tm, tk), lambda i, j, kk: (i, kk))
    if dims == _NT:
        b_spec = pl.BlockSpec((tn, tk), lambda i, j, kk: (j, kk))
    else:
        b_spec = pl.BlockSpec((tk, tn), lambda i, j, kk: (kk, j))
    chunks = [slice(None)] if row_chunk is None else [slice(r, r + row_chunk) for r in range(0, tm, row_chunk)]

    def lift(index_map):
        return lambda i, j, kk: index_map(i, j)

    def body(a_ref, b_ref, *rest):
        extra_refs, rest = rest[:ne], rest[ne:]
        xin_refs, rest = rest[:xin], rest[xin:]
        out_refs, rest = rest[:no], rest[no:]
        xout_refs, rest = rest[:xout], rest[xout:]
        i, j, kk = pl.program_id(0), pl.program_id(1), pl.program_id(2)
        if exchange:
            sem_refs = rest[1:] if nk > 1 else rest

            @pl.when((i == 0) & (j == 0) & (kk == 0))
            def _():
                exchange.start(xin_refs, xout_refs, sem_refs)

        if nk == 1:
            part = _dot(a_ref[...], b_ref[...], dims)
            for rows in chunks:
                epilogue(part[rows], i, j, extra_refs, out_refs, rows)
        else:
            acc_ref = rest[0]

            @pl.when(kk == 0)
            def _():
                acc_ref[...] = _dot(a_ref[...], b_ref[...], dims)

            @pl.when(kk > 0)
            def _():
                acc_ref[...] += _dot(a_ref[...], b_ref[...], dims)

            @pl.when(kk == nk - 1)
            def _():
                for rows in chunks:
                    epilogue(acc_ref[rows, :], i, j, extra_refs, out_refs, rows)

        if exchange:
            @pl.when((i == ni - 1) & (j == nj - 1) & (kk == nk - 1))
            def _():
                exchange.finish(xin_refs, xout_refs, sem_refs)

    any_spec = pl.BlockSpec(memory_space=pl.ANY)
    once = dict(pipeline_mode=pl.Buffered(1)) if (row_chunk is not None and nk > 1) else {}
    results = pl.pallas_call(
        body, name=name,
        grid=(ni, nj, nk),
        in_specs=[a_spec, b_spec] + [pl.BlockSpec(bs, lift(im), **once) for _, bs, im in extras] + [any_spec] * xin,
        out_specs=[pl.BlockSpec(bs, lift(im), **once) for _, bs, im in outs] + [any_spec] * xout,
        out_shape=[sd for sd, _, _ in outs] + (list(exchange.out_shapes) if exchange else []),
        scratch_shapes=([pltpu.VMEM((tm, tn), F32)] if nk > 1 else []) + (list(exchange.sems) if exchange else []),
        compiler_params=_params(3),
    )(a, b, *[arr for arr, _, _ in extras], *(exchange.arrays if exchange else []))
    return (results[:no], results[no:]) if exchange else results


def _after(token):
    return [(token, (8, LANE), lambda i, j: (0, 0))]


def _grad_w(name, a, dc, token=None, tm=512, tn=1024, rows=None):
    t = a.shape[0]
    n = dc.shape[1]
    first, m = rows if rows is not None else (0, a.shape[1])
    tm, tn = _tile(m, tm), _tile(n, tn)
    assert first % tm == 0

    def epilogue(acc, i, j, extra_refs, out_refs, rows):
        out_refs[0][...] = acc
        out_refs[1][...] = acc.astype(BF16)

    blk = ((tm, tn), lambda i, j: (i, j))
    return _mm(name, a, dc, _TN, m, n, t, tm, tn, t, _after(token) if token is not None else [],
               [(jax.ShapeDtypeStruct((m, n), F32),) + blk, (jax.ShapeDtypeStruct((m, n), BF16),) + blk], epilogue,
               a_col_block=first // tm)


def _proj_gather(a1, w_shard, order):
    t, d = a1.shape
    nsh = w_shard.shape[1]
    tm = _tile(t, 512)
    n_i = t // tm

    def body(ord_ref, a_ref, wsh_ref, proj_ref, full_ref, bbuf, bsem, send_sems, recv_sems, own_sem):
        s, i = pl.program_id(0), pl.program_id(1)
        me, sib, chips = _places()
        near, far = chips[:2], chips[2]
        steps = ([(me, None, None), (sib, 0, None)]
                 + [((*ch, me[2]), 1 + j, 4 + j) for j, ch in enumerate(near)]
                 + [((*ch, sib[2]), 4 + j, None) for j, ch in enumerate(near)]
                 + [((*far, me[2]), 3, 6), ((*far, sib[2]), 6, None)])
        blocks = [st[0] for st in steps]

        def part(block):
            return _region(full_ref, _place_index(block), 1, nsh)

        def remote(k, block, to, from_shard=False):
            return pltpu.make_async_remote_copy(
                src_ref=wsh_ref if from_shard else part(block), dst_ref=part(block),
                send_sem=send_sems.at[k], recv_sem=recv_sems.at[k], device_id=to, device_id_type=MESH)

        def load(pos):
            src = wsh_ref if pos == 0 else part(blocks[pos])
            return pltpu.make_async_copy(src, bbuf.at[pos % 2], bsem.at[pos % 2])

        own = pltpu.make_async_copy(wsh_ref, part(me), own_sem)

        @pl.when((s == 0) & (i == 0))
        def _():
            own.start()
            remote(0, me, sib, True).start()
            for j, ch in enumerate(chips):
                remote(1 + j, me, (*ch, me[2]), True).start()
            load(0).start()
            load(0).wait()

        for pos in range(1, N_DEV):
            @pl.when((s == pos) & (i == 0))
            def _():
                load(pos).wait()

        for pos in range(N_DEV - 1):
            @pl.when((s == pos) & (i == n_i - 1))
            def _():
                nxt = pos + 1
                block, arrives_on, pass_on_with = steps[nxt]
                remote(arrives_on, block, me).wait_recv()
                if pass_on_with is not None:
                    remote(pass_on_with, block, sib).start()
                load(nxt).start()

        proj_ref[...] = _dot(a_ref[...], bbuf[s % 2])

        @pl.when((s == N_DEV - 1) & (i == n_i - 1))
        def _():
            for k in range(N_DEV - 1):
                remote(k, me, sib, True).wait_send()
            own.wait()

    grid_spec = pltpu.PrefetchScalarGridSpec(
        num_scalar_prefetch=1, grid=(N_DEV, n_i),
        in_specs=[pl.BlockSpec((tm, d), lambda s, i, ord_ref: (i, 0)), pl.BlockSpec(memory_space=pl.ANY)],
        out_specs=[pl.BlockSpec((tm, nsh), lambda s, i, ord_ref: (i, ord_ref[s])), pl.BlockSpec(memory_space=pl.ANY)],
        scratch_shapes=[pltpu.VMEM((2, d, nsh), BF16), pltpu.SemaphoreType.DMA((2,)),
                        pltpu.SemaphoreType.DMA((N_DEV - 1,)), pltpu.SemaphoreType.DMA((N_DEV - 1,)),
                        pltpu.SemaphoreType.DMA])
    return pl.pallas_call(
        body, name="proj_gather", grid_spec=grid_spec,
        out_shape=[jax.ShapeDtypeStruct((t, nsh * N_DEV), F32), jax.ShapeDtypeStruct((d, nsh * N_DEV), BF16)],
        compiler_params=_params(2),
    )(order, a1, w_shard)


def _cast_bf16(name, w):
    r, c = w.shape
    tr = _tile(r, 256)
    return pl.pallas_call(
        lambda w_ref, o_ref: o_ref.__setitem__(Ellipsis, w_ref[...].astype(BF16)), name=name,
        grid=(r // tr,), in_specs=[pl.BlockSpec((tr, c), lambda i: (i, 0))],
        out_specs=pl.BlockSpec((tr, c), lambda i: (i, 0)), out_shape=jax.ShapeDtypeStruct((r, c), BF16),
        compiler_params=_params(1),
    )(w)


def _prep_small(c_row, lb_logits):
    d = c_row.shape[1]
    rows = d // LANE

    def body(c_ref, l_ref, o_ref):
        cv = c_ref[...]
        o_ref[0:rows, :] = cv * _sigmoid(cv)
        lbs = [_sigmoid(l_ref[dr][0:1, :] - l_ref[dr][1:2, :]) for dr in range(2)]
        o_ref[rows:rows + 8, :] = jnp.concatenate(lbs + [jnp.zeros((6, LANE), F32)], axis=0)

    return pl.pallas_call(
        body, name="prep_small", out_shape=jax.ShapeDtypeStruct((rows + 8, LANE), F32),
    )(c_row.reshape(rows, LANE), lb_logits)


def _mod_shard(sc_all, w_ada_shard, b_shard):
    d, n = w_ada_shard.shape
    tn = _tile(n, 512)

    def body(s_ref, w_ref, b_ref, o_ref):
        o_ref[...] = _dot(s_ref[...], w_ref[...], precision=HIGHEST) + b_ref[...]

    return pl.pallas_call(
        body, name="mod_shard", grid=(n // tn,),
        in_specs=[pl.BlockSpec((N_DEV, d), lambda j: (0, 0)), pl.BlockSpec((d, tn), lambda j: (0, j)),
                  pl.BlockSpec((1, tn), lambda j: (0, j))],
        out_specs=pl.BlockSpec((N_DEV, tn), lambda j: (0, j)),
        out_shape=jax.ShapeDtypeStruct((N_DEV, n), F32), compiler_params=_params(1),
    )(sc_all, w_ada_shard, b_shard)


def _norm_mod(x, gain, shift, scale):
    t, d = x.shape
    tm = _tile(t, 512)

    def body(x_ref, g_ref, sh_ref, sc_ref, o_ref):
        xv = x_ref[...]
        o_ref[...] = ((xv * _rms(xv) * g_ref[...]) * (1.0 + sc_ref[...]) + sh_ref[...]).astype(BF16)

    vec = pl.BlockSpec((1, d), lambda i: (0, 0))
    return pl.pallas_call(
        body, name="norm_mod", grid=(t // tm,),
        in_specs=[pl.BlockSpec((tm, d), lambda i: (i, 0)), vec, vec, vec],
        out_specs=pl.BlockSpec((tm, d), lambda i: (i, 0)), out_shape=jax.ShapeDtypeStruct((t, d), BF16),
        compiler_params=_params(1),
    )(x, gain, shift, scale)


def _chunk_masks():
    row = lax.broadcasted_iota(jnp.int32, (HEAD, HEAD), 0)
    col = lax.broadcasted_iota(jnp.int32, (HEAD, HEAD), 1)
    same = (row // A_CHUNK) == (col // A_CHUNK)
    return same & (col <= row), same & (col >= row)


def _ones(mask):
    return jnp.where(mask, 1.0, 0.0).astype(BF16)


def _dot_split(ones_bf16, x):
    hi = x.astype(BF16)
    lo = (x - hi.astype(F32)).astype(BF16)
    return _dot(ones_bf16, hi) + _dot(ones_bf16, lo)


def _hgrn_block(direction, f, lb, cum2):
    sf = _sigmoid(f)
    big_f = lb + (1.0 - lb) * sf
    k = (1.0 - lb) * (1.0 - sf)
    lf = jnp.log(big_f)
    both = _dot_split(cum2, lf)
    cf, cr = both[:HEAD], both[HEAD:]
    b, rest = (cf, cr - lf) if direction == 0 else (cr, cf - lf)
    return k, sf, big_f, jnp.exp(b), jnp.exp(-b), jnp.exp(rest)


def _hgrn_fwd(proj, lb, g_norm, width, exchange, after):
    t = proj.shape[0]
    heads = width // HEAD
    nb, nc = t // HEAD, t // A_CHUNK
    ua = 4 if nb % 4 == 0 else (2 if nb % 2 == 0 else 1)
    ub = 16 if nc % 16 == 0 else (8 if nc % 8 == 0 else 4)
    q_scale = HEAD ** -0.5
    xin, xout = len(exchange.arrays), len(exchange.out_shapes)

    def body(q_ref, ffw_ref, fbw_ref, v_ref, og_ref, lb_ref, g_ref, *rest):
        xin_refs, rest = rest[:xin], rest[xin + len(after):]
        outa_ref, osum_ref = rest[:2]
        xout_refs, rest = rest[2:2 + xout], rest[2 + xout:]
        qd_s, ke_s, dc_s, o_s = rest[:4]
        sem_refs = rest[4:]
        h = pl.program_id(0)

        @pl.when(h == 0)
        def _():
            exchange.start(xin_refs, xout_refs, sem_refs)

        tril, triu = _chunk_masks()
        cum2 = jnp.concatenate([_ones(tril), _ones(triu)], axis=0)
        f_refs = (ffw_ref, fbw_ref)
        lbs = (lb_ref[0:1, :], lb_ref[1:2, :])

        def phase_a(it, carry):
            loaded = []
            for u in range(ua):
                rows = pl.ds(pl.multiple_of((it * ua + u) * HEAD, HEAD), HEAD)
                loaded.append((rows, q_ref[rows, :], v_ref[rows, :], ffw_ref[rows, :], fbw_ref[rows, :]))
            chains = [(d, rows, qv * q_scale, vv.astype(BF16), fv)
                      for rows, qv, vv, f0, f1 in loaded for d, fv in ((0, f0), (1, f1))]
            blocks = [_hgrn_block(d, fv, lbs[d], cum2) for d, _, _, _, fv in chains]
            scaled = [(qv * eb, k * enb, k * erest, eb * erest)
                      for (_, _, qv, _, _), (k, _, _, eb, enb, erest) in zip(chains, blocks)]
            atts = [jnp.where(tril if d == 0 else triu, _bdot(qd, kd, _NT), 0.0)
                    for (d, _, _, _, _), (qd, kd, _, _) in zip(chains, scaled)]
            intras = [_bdot(att, vv) for att, (_, _, _, vv, _) in zip(atts, chains)]
            results = [(d, rows, o_intra, qd.astype(BF16), ke.astype(BF16), decay)
                       for (d, rows, _, _, _), (qd, _, ke, decay), o_intra in zip(chains, scaled, intras)]
            for d, rows, o_intra, qd16, ke16, decay in results:
                o_s[d, rows, :] = o_intra
                qd_s[d, rows, :] = qd16
                ke_s[d, rows, :] = ke16
                dc_s[d, rows, :] = decay
            return carry

        lax.fori_loop(0, nb // ua, phase_a, 0)

        def phase_b(it, states):
            loaded = []
            for u in range(ub):
                n = it * ub + u
                for d in range(2):
                    c = n if d == 0 else nc - 1 - n
                    start = pl.multiple_of(c * A_CHUNK, A_CHUNK)
                    rows = pl.ds(start, A_CHUNK)
                    loaded.append((d, rows, qd_s[d, rows, :], ke_s[d, rows, :], v_ref[rows, :],
                                   dc_s[d, pl.ds(start, 1), :], o_s[d, rows, :]))
            increments = [_dot(vv.astype(BF16), ke16, _TN) for _, _, _, ke16, vv, _, _ in loaded]
            states = list(states)
            befores = []
            for (d, _, _, _, _, decay, _), inc in zip(loaded, increments):
                befores.append(states[d].astype(BF16))
                states[d] = states[d] * decay + inc
            inters = [_dot(qd16, before, _NT) for (_, _, qd16, _, _, _, _), before in zip(loaded, befores)]
            for (d, rows, _, _, _, _, o_intra), o_inter in zip(loaded, inters):
                o_s[d, rows, :] = o_intra + o_inter
            return tuple(states)

        zero_state = jnp.zeros((HEAD, HEAD), F32)
        lax.fori_loop(0, nc // ub, phase_b, (zero_state, zero_state))

        def phase_c(i, carry):
            rows = pl.ds(pl.multiple_of(i * HEAD, HEAD), HEAD)
            o = o_s[0, rows, :] + o_s[1, rows, :]
            osum_ref[rows, :] = o
            og = og_ref[rows, :]
            outa_ref[rows, :] = (o * _rms(o) * g_ref[...] * (og * _sigmoid(og))).astype(BF16)
            return carry

        lax.fori_loop(0, nb, phase_c, 0)

        @pl.when(h == heads - 1)
        def _():
            exchange.finish(xin_refs, xout_refs, sem_refs)

    def col(p):
        return pl.BlockSpec((t, HEAD), lambda h: (0, p * heads + h))

    any_spec = pl.BlockSpec(memory_space=pl.ANY)
    results = pl.pallas_call(
        body, name="hgrn_fwd", grid=(heads,),
        in_specs=[col(0), col(1), col(2), col(3), col(4),
                  pl.BlockSpec((2, HEAD), lambda h: (0, h)), pl.BlockSpec((1, HEAD), lambda h: (0, 0))]
        + [any_spec] * (xin + len(after)),
        out_specs=[pl.BlockSpec((t, HEAD), lambda h: (0, h)), pl.BlockSpec((t, HEAD), lambda h: (0, h))] + [any_spec] * xout,
        out_shape=[jax.ShapeDtypeStruct((t, width), BF16), jax.ShapeDtypeStruct((t, width), F32)] + list(exchange.out_shapes),
        scratch_shapes=[pltpu.VMEM((2, t, HEAD), BF16), pltpu.VMEM((2, t, HEAD), BF16), pltpu.VMEM((2, t, HEAD), F32),
                        pltpu.VMEM((2, t, HEAD), F32)] + list(exchange.sems),
        compiler_params=_params(1),
    )(proj, proj, proj, proj, proj, lb, g_norm, *exchange.arrays, *after)
    return results[0], results[1], results[2:]


def _sgu_core(u_pre, v_pre, g_v, ws_ref, bst):
    u, du = _gelu_and_grad(u_pre)
    v, dv = _gelu_and_grad(v_pre)
    mu = jnp.mean(v, axis=-1, keepdims=True)
    dlt = v - mu
    rstd = lax.rsqrt(jnp.mean(dlt * dlt, axis=-1, keepdims=True) + EPS)
    vhat = dlt * rstd
    vn = vhat * g_v
    groups = vn.shape[1] // HEAD
    cols = []
    for g in range(groups):
        vm_g = _bdot(ws_ref[g], vn[:, g * HEAD:(g + 1) * HEAD]) + bst[:, g:g + 1]
        cols.append(vm_g)
    return u, du, dv, vhat, rstd, vn, jnp.concatenate(cols, axis=1)


def _sgu_fwd(proj, g_v, w_s, bst, width, z_block):
    t = proj.shape[0]

    def body(u_ref, v_ref, g_ref, ws_ref, bst_ref, o_ref):
        u, _, _, _, _, _, vm = _sgu_core(u_ref[...], v_ref[...], g_ref[...], ws_ref, bst_ref[...])
        o_ref[...] = (u * vm).astype(BF16)

    groups = width // HEAD
    return pl.pallas_call(
        body, name="sgu_fwd", grid=(t // HEAD,),
        in_specs=[pl.BlockSpec((HEAD, width), lambda i: (i, z_block)), pl.BlockSpec((HEAD, width), lambda i: (i, z_block + 1)),
                  pl.BlockSpec((1, width), lambda i: (0, 0)), pl.BlockSpec((groups, HEAD, HEAD), lambda i: (0, 0, 0)),
                  pl.BlockSpec((HEAD, groups), lambda i: (0, 0))],
        out_specs=pl.BlockSpec((HEAD, width), lambda i: (i, 0)),
        out_shape=jax.ShapeDtypeStruct((t, width), BF16), compiler_params=_params(1),
    )(proj, proj, g_v, w_s, bst)


def _sgu_bwd(proj, dout_b, dproj, g_v, w_s, w_st, bst, width, z_block):
    t = proj.shape[0]
    groups = width // HEAD
    nblk = t // HEAD

    def body(u_ref, v_ref, do_ref, g_ref, ws_ref, wst_ref, bst_ref, dproj_hbm,
             dz_ref, dg_ref, dws_ref, dbst_ref, res_s):
        i, p = pl.program_id(0), pl.program_id(1)

        @pl.when((i == 0) & (p == 0))
        def _():
            dg_ref[...] = jnp.zeros_like(dg_ref)
            dws_ref[...] = jnp.zeros_like(dws_ref)
            dbst_ref[...] = jnp.zeros_like(dbst_ref)

        @pl.when(p == 0)
        def _():
            g_v = g_ref[...]
            u, du, dv, vhat, rstd, vn, vm = _sgu_core(u_ref[...], v_ref[...], g_v, ws_ref, bst_ref[...])
            dout = do_ref[...].astype(F32)
            res_s[0] = (dout * vm * du).astype(BF16)
            dvm = dout * u
            dvn_cols = []
            for g in range(groups):
                sl = slice(g * HEAD, (g + 1) * HEAD)
                dvm_g = dvm[:, sl]
                dbst_ref[:, g:g + 1] += jnp.sum(dvm_g, axis=1, keepdims=True)
                dws_ref[g] += _bdot(dvm_g, vn[:, sl], _NT)
                dvn_cols.append(_bdot(wst_ref[g], dvm_g))
            dvn = jnp.concatenate(dvn_cols, axis=1)
            dg_ref[...] += _colsum(dvn * vhat)
            dvh = dvn * g_v
            dvg = rstd * (dvh - jnp.mean(dvh, axis=-1, keepdims=True)
                          - vhat * jnp.mean(dvh * vhat, axis=-1, keepdims=True))
            res_s[1] = (dvg * dv).astype(BF16)

        dz_ref[...] = res_s[p]

    n_in = dproj.shape[1]
    return pl.pallas_call(
        body, name="sgu_bwd", grid=(nblk, 2),
        in_specs=[pl.BlockSpec((HEAD, width), lambda i, p: (i, z_block)),
                  pl.BlockSpec((HEAD, width), lambda i, p: (i, z_block + 1)),
                  pl.BlockSpec((HEAD, width), lambda i, p: (i, 0)),
                  pl.BlockSpec((1, width), lambda i, p: (0, 0)),
                  pl.BlockSpec((groups, HEAD, HEAD), lambda i, p: (0, 0, 0)),
                  pl.BlockSpec((groups, HEAD, HEAD), lambda i, p: (0, 0, 0)),
                  pl.BlockSpec((HEAD, groups), lambda i, p: (0, 0)),
                  pl.BlockSpec(memory_space=pl.ANY)],
        out_specs=[pl.BlockSpec((HEAD, width), lambda i, p: (i, z_block + p)),
                   pl.BlockSpec((1, width), lambda i, p: (0, 0)),
                   pl.BlockSpec((groups, HEAD, HEAD), lambda i, p: (0, 0, 0)),
                   pl.BlockSpec((HEAD, groups), lambda i, p: (0, 0))],
        out_shape=[jax.ShapeDtypeStruct((t, n_in), BF16), jax.ShapeDtypeStruct((1, width), F32),
                   jax.ShapeDtypeStruct((groups, HEAD, HEAD), F32), jax.ShapeDtypeStruct((HEAD, groups), F32)],
        scratch_shapes=[pltpu.VMEM((2, HEAD, width), BF16)],
        input_output_aliases={7: 0},
        compiler_params=_params(2),
    )(proj, proj, dout_b, g_v, w_s, w_st, bst, dproj)


def _hgrn_bwd(proj, osum, dout_a, dproj, lb, g_norm, width, exchange, after):
    t = proj.shape[0]
    heads = width // HEAD
    nb = t // HEAD
    cpb = HEAD // A_CHUNK
    ubk = 2 if nb % 2 == 0 else 1
    q_scale = HEAD ** -0.5
    xin, xout = len(exchange.arrays), len(exchange.out_shapes)

    def body(q_ref, ffw_ref, fbw_ref, v_ref, og_ref, osum_ref, douta_ref, lb_ref, g_ref, dproj_hbm, *rest):
        xin_refs, rest = rest[:xin], rest[xin + 1:]
        out_ref, dgh_ref, dlb_ref = rest[:3]
        xout_refs, rest = rest[3:3 + xout], rest[3 + xout:]
        do_s, dq_s, dv_s, res_s, ck_s = rest[:5]
        sem_refs = rest[5:]
        h, p = pl.program_id(0), pl.program_id(1)
        f_refs = (ffw_ref, fbw_ref)

        @pl.when((h == 0) & (p == 0))
        def _():
            exchange.start(xin_refs, xout_refs, sem_refs)

        @pl.when(p == 0)
        def _():
            tril, triu = _chunk_masks()
            cum2 = jnp.concatenate([_ones(tril), _ones(triu)], axis=0)
            g_row = g_ref[...]

            def pass_norm(i, dgh):
                rows = pl.ds(pl.multiple_of(i * HEAD, HEAD), HEAD)
                o = osum_ref[rows, :]
                r = _rms(o)
                oh = o * r
                og = og_ref[rows, :]
                sg = _sigmoid(og)
                dout = douta_ref[rows, :].astype(F32)
                don = dout * (og * sg)
                res_s[4, rows, :] = (dout * (oh * g_row) * (sg * (1.0 + og * (1.0 - sg)))).astype(BF16)
                doh = don * g_row
                do_s[rows, :] = r * (doh - oh * jnp.mean(doh * oh, axis=-1, keepdims=True))
                return dgh + _colsum(don * oh)

            dgh_ref[...] = lax.fori_loop(0, nb, pass_norm, jnp.zeros((1, HEAD), F32))

            lbs = (lb_ref[0:1, :], lb_ref[1:2, :])
            zero_state = jnp.zeros((HEAD, HEAD), F32)

            def chunk_order(d):
                return list(range(cpb)) if d == 0 else list(range(cpb - 1, -1, -1))

            def chunk(x, j):
                return x[j * A_CHUNK:(j + 1) * A_CHUNK, :]

            def decay_row(e_big, j):
                return e_big[j * A_CHUNK:j * A_CHUNK + 1, :]

            def cat(parts):
                return jnp.concatenate([parts[j] for j in range(cpb)], axis=0)

            def block_states(d, start, incs, e_big):
                befores, st = {}, start
                for j in chunk_order(d):
                    befores[j] = st
                    st = st * decay_row(e_big, j) + incs[j]
                return befores, st

            def pass_states(it, states):
                loaded = []
                for u in range(ubk):
                    for d in range(2):
                        blk = it * ubk + u if d == 0 else nb - 1 - (it * ubk + u)
                        rows = pl.ds(pl.multiple_of(blk * HEAD, HEAD), HEAD)
                        loaded.append((d, blk, f_refs[d][rows, :], v_ref[rows, :]))
                blocks = [_hgrn_block(d, fv, lbs[d], cum2) for d, _, fv, _ in loaded]
                incs = [{j: _bdot(chunk(vv, j), chunk(k * erest, j), _TN) for j in range(cpb)}
                        for (_, _, _, vv), (k, _, _, _, _, erest) in zip(loaded, blocks)]
                states, starts = list(states), []
                for (d, _, _, _), (_, _, _, eb, _, erest), inc in zip(loaded, blocks, incs):
                    starts.append(states[d])
                    states[d] = block_states(d, states[d], inc, eb * erest)[1]
                for (d, blk, _, _), start in zip(loaded, starts):
                    ck_s[d, blk] = start
                return tuple(states)

            lax.fori_loop(0, nb // ubk, pass_states, (zero_state, zero_state))

            def pass_back(it, carry):
                gts, dlb = [carry[0], carry[1]], carry[2]
                loaded = []
                for u, d in ((u, d) for u in range(ubk) for d in range(2)):
                    blk = nb - 1 - (it * ubk + u) if d == 0 else it * ubk + u
                    rows = pl.ds(pl.multiple_of(blk * HEAD, HEAD), HEAD)
                    loaded.append((d, rows, f_refs[d][rows, :], q_ref[rows, :], v_ref[rows, :], do_s[rows, :], ck_s[d, blk]))
                blocks = [_hgrn_block(d, fv, lbs[d], cum2) for d, _, fv, _, _, _, _ in loaded]
                scaled = []
                for (_, _, _, qv, _, _, _), (k, _, _, eb, enb, erest) in zip(loaded, blocks):
                    qh = qv * q_scale
                    scaled.append((qh, qh * eb, k * enb, k * erest, eb * erest))
                masks = [tril if d == 0 else triu for d, *_ in loaded]
                atts = [jnp.where(m, _bdot(qd, kd, _NT), 0.0) for m, (_, qd, kd, _, _) in zip(masks, scaled)]
                datts = [jnp.where(m, _bdot(do, vv, _NT), 0.0) for m, (_, _, _, _, vv, do, _) in zip(masks, loaded)]
                dvs = [_bdot(att, do, _TN) for att, (_, _, _, _, _, do, _) in zip(atts, loaded)]
                dqds = [_bdot(datt, kd) for datt, (_, _, kd, _, _) in zip(datts, scaled)]
                dkds = [_bdot(datt, qd, _TN) for datt, (_, qd, _, _, _) in zip(datts, scaled)]
                s_incs = [{j: _bdot(chunk(vv, j), chunk(ke, j), _TN) for j in range(cpb)}
                          for (_, _, _, _, vv, _, _), (_, _, _, ke, _) in zip(loaded, scaled)]
                g_incs = [{j: _bdot(chunk(do, j), chunk(qd, j), _TN) for j in range(cpb)}
                          for (_, _, _, _, _, do, _), (_, qd, _, _, _) in zip(loaded, scaled)]
                befores, afters, g_at = [], [], []
                for (d, _, _, _, _, _, ck), (_, _, _, _, e_big), s_inc, g_inc in zip(loaded, scaled, s_incs, g_incs):
                    order = chunk_order(d)
                    before, after = block_states(d, ck, s_inc, e_big)
                    befores.append(before)
                    afters.append({j: (before[order[n + 1]] if n + 1 < cpb else after) for n, j in enumerate(order)})
                    at, gt = {}, gts[d]
                    for j in reversed(order):
                        at[j] = gt
                        gt = gt * decay_row(e_big, j) + g_inc[j]
                    gts[d] = gt
                    g_at.append(at)
                dqd_i = [{j: _bdot(chunk(do, j), before[j]) for j in range(cpb)}
                         for (_, _, _, _, _, do, _), before in zip(loaded, befores)]
                dv_i = [{j: _bdot(chunk(ke, j), at[j], _NT) for j in range(cpb)}
                        for (_, _, _, ke, _), at in zip(scaled, g_at)]
                dke = [{j: _bdot(chunk(vv, j), at[j]) for j in range(cpb)}
                       for (_, _, _, _, vv, _, _), at in zip(loaded, g_at)]
                results, new = [], []
                for n, ((d, rows, _, _, _, _, _), (k, sf, big_f, eb, enb, erest), (qh, _, _, _, _)) in enumerate(
                        zip(loaded, blocks, scaled)):
                    dqh = (dqds[n] + cat(dqd_i[n])) * eb
                    dk = dkds[n] * enb + cat(dke[n]) * erest
                    carry_rows = {j: jnp.broadcast_to(_colsum(g_at[n][j] * afters[n][j]), (A_CHUNK, HEAD))
                                  for j in range(cpb)}
                    dlf = _dot_split(_ones(triu if d == 0 else tril), qh * dqh - k * dk) + cat(carry_rows)
                    common = dlf / big_f - dk
                    results.append((d, rows, (k * sf * common).astype(BF16), dqh.astype(BF16),
                                    (dvs[n] + cat(dv_i[n])).astype(BF16)))
                    new.append(_colsum((1.0 - sf) * common))
                for d, rows, df16, dq16, dv16 in results:
                    res_s[1 + d, rows, :] = df16
                    dq_s[d, rows, :] = dq16
                    dv_s[d, rows, :] = dv16
                per_dir = [sum(c for (d, *_), c in zip(loaded, new) if d == dd) for dd in range(2)]
                return gts[0], gts[1], dlb + jnp.concatenate(per_dir, axis=0)

            dlb_ref[...] = lax.fori_loop(0, nb // ubk, pass_back,
                                         (zero_state, zero_state, jnp.zeros((2, HEAD), F32)))[2]

            def pass_out(i, carry):
                rows = pl.ds(pl.multiple_of(i * HEAD, HEAD), HEAD)
                dq = dq_s[0, rows, :].astype(F32) + dq_s[1, rows, :].astype(F32)
                res_s[0, rows, :] = (dq * q_scale).astype(BF16)
                res_s[3, rows, :] = (dv_s[0, rows, :].astype(F32) + dv_s[1, rows, :].astype(F32)).astype(BF16)
                return carry

            lax.fori_loop(0, nb, pass_out, 0)

        out_ref[...] = res_s[p]

        @pl.when((h == heads - 1) & (p == 4))
        def _():
            exchange.finish(xin_refs, xout_refs, sem_refs)

    def col(pp):
        return pl.BlockSpec((t, HEAD), lambda h, p: (0, pp * heads + h))

    n_in = dproj.shape[1]
    any_spec = pl.BlockSpec(memory_space=pl.ANY)
    results = pl.pallas_call(
        body, name="hgrn_bwd", grid=(heads, 5),
        in_specs=[col(0), col(1), col(2), col(3), col(4),
                  pl.BlockSpec((t, HEAD), lambda h, p: (0, h)), pl.BlockSpec((t, HEAD), lambda h, p: (0, h)),
                  pl.BlockSpec((2, HEAD), lambda h, p: (0, h)), pl.BlockSpec((1, HEAD), lambda h, p: (0, 0)),
                  any_spec] + [any_spec] * (xin + 1),
        out_specs=[pl.BlockSpec((t, HEAD), lambda h, p: (0, p * heads + h)),
                   pl.BlockSpec((None, 1, HEAD), lambda h, p: (h, 0, 0)),
                   pl.BlockSpec((2, HEAD), lambda h, p: (0, h))] + [any_spec] * xout,
        out_shape=[jax.ShapeDtypeStruct((t, n_in), BF16), jax.ShapeDtypeStruct((heads, 1, HEAD), F32),
                   jax.ShapeDtypeStruct((2, width), F32)] + list(exchange.out_shapes),
        scratch_shapes=[pltpu.VMEM((t, HEAD), F32), pltpu.VMEM((2, t, HEAD), BF16), pltpu.VMEM((2, t, HEAD), BF16),
                        pltpu.VMEM((5, t, HEAD), BF16), pltpu.VMEM((2, nb, HEAD, HEAD), F32)] + list(exchange.sems),
        input_output_aliases={9: 0},
        compiler_params=_params(2),
    )(proj, proj, proj, proj, proj, osum, dout_a, lb, g_norm, dproj, *exchange.arrays, after)
    return results[0], results[1], results[2], results[3:]


def _adamw(w, g, m, v):
    m = ADAM_B1 * m + (1.0 - ADAM_B1) * g
    v = ADAM_B2 * v + (1.0 - ADAM_B2) * (g * g)
    m_hat = m / (1.0 - ADAM_B1 ** ADAM_STEP)
    v_hat = v / (1.0 - ADAM_B2 ** ADAM_STEP)
    delta = -ADAM_LR * (m_hat / (jnp.sqrt(v_hat) + ADAM_EPS) + ADAM_WD * w)
    return delta, m, v


def _adamw_big(name, me, w, m, v, parts, axis):
    r, c = w.shape
    n_parts = len(parts)
    tr = _tile(r // n_parts, 128)
    per = r // n_parts // tr
    assert axis == 1 or n_parts == 1

    def body(me_ref, w_ref, m_ref, v_ref, *rest):
        g_refs, l_refs = rest[:n_parts], rest[n_parts:2 * n_parts]
        og_ref, od_ref, om_ref, ov_ref = rest[2 * n_parts:]
        g = None
        for p in range(n_parts):
            total = g_refs[p][...]
            for s in range(N_DEV - 1):
                total = total + l_refs[p][s].astype(F32)
            g = total if p == 0 else jnp.where(pl.program_id(0) // per == p, total, g)
        og_ref[...] = g
        od_ref[...], om_ref[...], ov_ref[...] = _adamw(w_ref[...], g, m_ref[...], v_ref[...])

    def within(p, i):
        return jnp.clip(i - p * per, 0, per - 1)

    shard = pl.BlockSpec((tr, c), lambda i, me_ref: (i, 0))
    if axis == 1:
        own = [pl.BlockSpec((tr, c), lambda i, me_ref, p=p: (within(p, i), me_ref[0])) for p in range(n_parts)]
    else:
        own = [pl.BlockSpec((tr, c), lambda i, me_ref: (me_ref[0] * (r // tr) + i, 0))]
    landed = [pl.BlockSpec((N_DEV - 1, tr, c), lambda i, me_ref, p=p: (0, within(p, i), 0)) for p in range(n_parts)]
    grid_spec = pltpu.PrefetchScalarGridSpec(
        num_scalar_prefetch=1, grid=(r // tr,),
        in_specs=[shard, shard, shard] + own + landed, out_specs=[shard] * 4)
    return pl.pallas_call(
        body, name=name, grid_spec=grid_spec, out_shape=[jax.ShapeDtypeStruct((r, c), F32)] * 4,
        compiler_params=_params(1),
    )(me, w, m, v, *[g for g, _ in parts], *[ld for _, ld in parts])


def _adamw_ada(sct, dmod_mine, w, m, v):
    d, n = w.shape
    tr = _tile(d, 256)

    def body(s_ref, dm_ref, w_ref, m_ref, v_ref, og_ref, od_ref, om_ref, ov_ref):
        g = _dot(s_ref[...], dm_ref[...], precision=HIGHEST)
        og_ref[...] = g
        od_ref[...], om_ref[...], ov_ref[...] = _adamw(w_ref[...], g, m_ref[...], v_ref[...])

    blk = pl.BlockSpec((tr, n), lambda i: (i, 0))
    return pl.pallas_call(
        body, name="adamw_ada", grid=(d // tr,),
        in_specs=[pl.BlockSpec((tr, N_DEV), lambda i: (i, 0)), pl.BlockSpec((N_DEV, n), lambda i: (0, 0)), blk, blk, blk],
        out_specs=[blk] * 4, out_shape=[jax.ShapeDtypeStruct((d, n), F32)] * 4, compiler_params=_params(1),
    )(sct, dmod_mine, w, m, v)


def _adamw_small(gathered, w, m, v):
    def body(g_ref, w_ref, m_ref, v_ref, og_ref, od_ref, om_ref, ov_ref):
        g = g_ref[0]
        for s in range(1, N_DEV):
            g = g + g_ref[s]
        og_ref[...] = g
        od_ref[...], om_ref[...], ov_ref[...] = _adamw(w_ref[...], g, m_ref[...], v_ref[...])

    return pl.pallas_call(
        body, name="adamw_small", out_shape=[jax.ShapeDtypeStruct(w.shape, F32)] * 4,
        compiler_params=pltpu.CompilerParams(vmem_limit_bytes=VMEM_LIMIT),
    )(gathered, w, m, v)


def _adamw_lb(dlb_mine, lb_logits, m, v):
    def body(d_ref, l_ref, m_ref, v_ref, og_ref, od_ref, om_ref, ov_ref):
        dlb = d_ref[0]
        for s in range(1, N_DEV):
            dlb = dlb + d_ref[s]
        for dr in range(2):
            lb = _sigmoid(l_ref[dr][0:1, :] - l_ref[dr][1:2, :])
            d0 = dlb[dr:dr + 1] * lb * (1.0 - lb)
            g = jnp.concatenate([d0, -d0], axis=0)
            og_ref[dr] = g
            od_ref[dr], om_ref[dr], ov_ref[dr] = _adamw(l_ref[dr], g, m_ref[dr], v_ref[dr])

    return pl.pallas_call(body, name="adamw_lb", out_shape=[jax.ShapeDtypeStruct(lb_logits.shape, F32)] * 4,
                          )(dlb_mine, lb_logits, m, v)


def _rows(a, pad_to=8):
    flat = a.reshape(-1, LANE)
    pad = (-flat.shape[0]) % pad_to
    return jnp.pad(flat, ((0, pad), (0, 0))) if pad else flat


def kernel(x, c, w_ada, b_ada, g_pre_mix, g_post_mix, g_pre_ffn, g_post_ffn, w_in, lb_logits, g_hgrn_norm, w_a_out, g_sgu_norm, w_spatial, b_spatial, w_b_out, w_o, w_ff1, w_ff2, loss_target, m_w_ada, m_b_ada, m_g_pre_mix, m_g_post_mix, m_g_pre_ffn, m_g_post_ffn, m_w_in, m_lb_logits, m_g_hgrn_norm, m_w_a_out, m_g_sgu_norm, m_w_spatial, m_b_spatial, m_w_b_out, m_w_o, m_w_ff1, m_w_ff2, v_w_ada, v_b_ada, v_g_pre_mix, v_g_post_mix, v_g_pre_ffn, v_g_post_ffn, v_w_in, v_lb_logits, v_g_hgrn_norm, v_w_a_out, v_g_sgu_norm, v_w_spatial, v_b_spatial, v_w_b_out, v_w_o, v_w_ff1, v_w_ff2):
    t, d = x.shape[1], x.shape[2]
    n_in = w_in.shape[2] * N_DEV
    width = (n_in - 2 * d) // 7
    heads = width // HEAD
    assert heads == N_DEV and width % LANE == 0
    d_ff = w_ff1.shape[2] * N_DEV
    n_ada = w_ada.shape[2]
    me = _dev_index()
    me_arr = me.reshape(1).astype(jnp.int32)
    x2, tgt = x[0], loss_target[0]

    big = [w_in[0], w_a_out[0], w_b_out[0], w_o[0], w_ff1[0], w_ff2[0]]
    big_axes = [1, 1, 1, 0, 1, 0]
    big_names = ["w_in", "w_a_out", "w_b_out", "w_o", "w_ff1", "w_ff2"]
    w_in16 = _cast_bf16("cast_w_in", big[0])
    own_parts = [_cast_into_full("cast_" + nm, me_arr, w, ax) for nm, w, ax in zip(big_names[1:], big[1:], big_axes[1:])]

    c_rows = d // LANE
    small = _all_gather_small("gather_c_lb", _prep_small(c[0:1], lb_logits))
    sc_all = small[:, :c_rows, :].reshape(N_DEV, d)
    lb = jnp.transpose(small[:, c_rows:c_rows + 2, :], (1, 0, 2)).reshape(2, width)
    b_shard = lax.dynamic_slice_in_dim(b_ada, me * n_ada, n_ada, axis=1)
    mod_sh = _mod_shard(sc_all, w_ada[0], b_shard)
    mod_all = _all_gather_small("gather_mod", _rows(mod_sh))
    mod_all = mod_all[:, :N_DEV * n_ada // LANE, :].reshape(N_DEV, N_DEV, n_ada)
    mod6 = lax.dynamic_index_in_dim(mod_all, me, axis=1, keepdims=False).reshape(N_MOD, d)
    sh1, sc1, gt1, sh2, sc2, gt2 = [mod6[i:i + 1] for i in range(N_MOD)]

    a1 = _norm_mod(x2, g_pre_mix, sh1, sc1)
    tm = _tile(t, 512)

    def store_bf16(acc, i, j, extra_refs, out_refs, rows):
        out_refs[0][...] = acc.astype(BF16)

    xq, yq, cq = lax.axis_index("x"), lax.axis_index("y"), lax.axis_index("c")
    chips = [(1 - xq, yq), (xq, 1 - yq), (1 - xq, 1 - yq)]
    order = jnp.stack([me, 4 * xq + 2 * yq + 1 - cq]
                      + [4 * a + 2 * b + cq for a, b in chips[:2]] + [4 * a + 2 * b + 1 - cq for a, b in chips[:2]]
                      + [4 * chips[2][0] + 2 * chips[2][1] + cq, 4 * chips[2][0] + 2 * chips[2][1] + 1 - cq]).astype(jnp.int32)
    proj, wf_in = _proj_gather(a1, w_in16, order)

    proj, own_parts = lax.optimization_barrier((proj, own_parts))
    gathers = {}
    for key, lo, hi in (("mid", 1, 4), ("ff1", 4, 5), ("ff2", 5, 6)):
        far, near = _gather_stage_plans(own_parts[lo - 1:hi - 1], big_axes[lo:hi])
        gathers[key] = [far, near, _split_start("gather_%s_start" % key, far, landing=own_parts[lo - 1:hi - 1])]

    def pass_on(key, after):
        far, near, (sems, thru, _) = gathers[key]
        parts = _split_wait("gather_%s_wait" % key, far, sems, thru, after)[1]
        gathers[key].append(_split_start("pass_%s_start" % key, near, landing=list(parts)))
        return gathers[key][3][2]

    def gathered_weights(key, after):
        near, (sems, thru, _) = gathers[key][1], gathers[key][3]
        return _split_wait("pass_%s_wait" % key, near, sems, thru, after)[1]

    out_a, osum, _ = _hgrn_fwd(proj, lb, g_hgrn_norm, width, _NO_EXCHANGE,
                               after=[gathers[key][2][2] for key in ("mid", "ff1", "ff2")])
    passed_mid = pass_on("mid", out_a)
    z_block = 5
    bst = b_spatial[0].T
    out_b = _sgu_fwd(proj, g_sgu_norm, w_spatial[0], bst, width, z_block)
    wf_a, wf_b, wf_o = gathered_weights("mid", out_b)

    tn_d = _tile(d, 512)
    blk_d = ((tm, tn_d), lambda i, j: (i, j))
    y_a, = _mm("y_a", out_a, wf_a, _NN, t, d, width, tm, tn_d, width, _after(passed_mid),
               [(jax.ShapeDtypeStruct((t, d), BF16),) + blk_d], store_bf16)
    ga_blk = (5 * width + 2 * width) // tn_d
    gb_blk = ga_blk + d // tn_d

    def merge(acc, i, j, extra_refs, out_refs, rows):
        ga, gb, ya = extra_refs
        out_refs[0][...] = acc.astype(BF16)
        out_refs[1][...] = (_sigmoid(ga[...]) * ya[...].astype(F32) + _sigmoid(gb[...]) * acc).astype(BF16)

    y_b, merged = _mm("y_b_merge", out_b, wf_b, _NN, t, d, width, tm, tn_d, width,
                      [(proj, (tm, tn_d), lambda i, j: (i, ga_blk + j)), (proj, (tm, tn_d), lambda i, j: (i, gb_blk + j)),
                       (y_a,) + blk_d],
                      [(jax.ShapeDtypeStruct((t, d), BF16),) + blk_d, (jax.ShapeDtypeStruct((t, d), BF16),) + blk_d], merge)

    tr = _tile(t, 512)
    rc = 32 if tr % 32 == 0 else None
    row_d = ((tr, d), lambda i, j: (i, 0))
    vec_d = ((1, d), lambda i, j: (0, 0))

    passed_ff1 = pass_on("ff1", merged)

    def post_mix(acc, i, j, extra_refs, out_refs, rows):
        x_r, gt1_r, g2_r, g3_r, sc2_r, sh2_r = extra_refs[:6]
        h1 = x_r[rows, :] + gt1_r[...] * (acc * _rms(acc) * g2_r[...])
        out_refs[0][rows, :] = acc.astype(BF16)
        out_refs[1][rows, :] = h1
        out_refs[2][rows, :] = ((h1 * _rms(h1) * g3_r[...]) * (1.0 + sc2_r[...]) + sh2_r[...]).astype(BF16)

    mo, h1, a2 = _mm("w_o_post_mix", merged, wf_o, _NN, t, d, d, tr, d, d,
                     [(x2,) + row_d, (gt1,) + vec_d, (g_post_mix,) + vec_d, (g_pre_ffn,) + vec_d, (sc2,) + vec_d, (sh2,) + vec_d]
                     + _after(passed_ff1),
                     [(jax.ShapeDtypeStruct((t, d), BF16),) + row_d, (jax.ShapeDtypeStruct((t, d), F32),) + row_d,
                      (jax.ShapeDtypeStruct((t, d), BF16),) + row_d], post_mix, row_chunk=rc)

    tn_f = _tile(d_ff, 1024)
    blk_f = ((tm, tn_f), lambda i, j: (i, j))

    def relu_sq(acc, i, j, extra_refs, out_refs, rows):
        r = jnp.maximum(acc, 0.0)
        out_refs[0][...] = acc.astype(BF16)
        out_refs[1][...] = (r * r).astype(BF16)

    wf_1, = gathered_weights("ff1", a2)
    hff, act = _mm(
        "ff1", a2, wf_1, _NN, t, d_ff, d, tm, tn_f, d, [],
        [(jax.ShapeDtypeStruct((t, d_ff), BF16),) + blk_f, (jax.ShapeDtypeStruct((t, d_ff), BF16),) + blk_f], relu_sq)
    pass_on("ff2", hff)
    wf_2, = gathered_weights("ff2", act)

    sums_d = ((8, d), lambda i, j: (0, 0))

    def zero_first(sums_r, i, rows):
        if rows.start in (None, 0):
            @pl.when(i == 0)
            def _():
                sums_r[...] = jnp.zeros_like(sums_r)

    def loss_head(acc, i, j, extra_refs, out_refs, rows):
        h1_r, tgt_r, gt2_r, g4_r = extra_refs
        dy_r, dff_r, sums_r = out_refs
        r4 = _rms(acc)
        ffn = acc * r4
        n4 = ffn * g4_r[...]
        err = h1_r[rows, :] + gt2_r[...] * n4 - tgt_r[rows, :]
        dy = err * (1.0 / d)
        dy_r[rows, :] = dy.astype(BF16)
        dn4 = dy * gt2_r[...]
        dffn = dn4 * g4_r[...]
        dff_r[rows, :] = (r4 * (dffn - ffn * jnp.mean(dffn * ffn, axis=-1, keepdims=True))).astype(BF16)
        zero_first(sums_r, i, rows)

        sums_r[0:1, :] += _colsum(err * err)
        sums_r[1:2, :] += _colsum(dy * n4)
        sums_r[2:3, :] += _colsum(dn4 * ffn)

    tk_f = _tile(d_ff, 1024)
    dy, dff, sums_f = _mm("ff2_loss", act, wf_2, _NN, t, d, d_ff, tr, d, tk_f,
                          [(h1,) + row_d, (tgt,) + row_d, (gt2,) + vec_d, (g_post_ffn,) + vec_d],
                          [(jax.ShapeDtypeStruct((t, d), BF16),) + row_d, (jax.ShapeDtypeStruct((t, d), BF16),) + row_d,
                           (jax.ShapeDtypeStruct((8, d), F32),) + sums_d], loss_head, row_chunk=rc)
    loss_mine = (0.5 / d) * jnp.sum(sums_f[0])

    def relu_sq_bwd(acc, i, j, extra_refs, out_refs, rows):
        out_refs[0][...] = (acc * (2.0 * jnp.maximum(extra_refs[0][...].astype(F32), 0.0))).astype(BF16)

    dhff, = _mm("d_hff", dff, wf_2, _NT, t, d_ff, d, tm, tn_f, d, [(hff,) + blk_f],
                [(jax.ShapeDtypeStruct((t, d_ff), BF16),) + blk_f], relu_sq_bwd)
    scatters = {}

    def send_grads(key, grads16, axes):
        plan = _scatter_plan(grads16, axes)
        scatters[key] = (plan,) + _split_start("scatter_%s_start" % key, plan)
        return scatters[key][3]

    def received_grads(key, after):
        plan, sems, thru, _ = scatters[key]
        return _split_wait("scatter_%s_wait" % key, plan, sems, thru, after)[1]

    gw_ff2, gw_ff2_16 = _grad_w("grad_w_ff2", act, dff)
    sent_ff2 = send_grads("ff2", [gw_ff2_16], big_axes[5:6])
    gw_ff1, gw_ff1_16 = _grad_w("grad_w_ff1", a2, dhff, token=sent_ff2)
    sent_ff1 = send_grads("ff1", [gw_ff1_16], big_axes[4:5])

    def pre_ffn_bwd(acc, i, j, extra_refs, out_refs, rows):
        h1_r, dy_r, mo_r, sc2_r, g3_r, gt1_r, g2_r = extra_refs[:7]
        dh1_r, dmo_r, sums_r = out_refs
        h1v = h1_r[rows, :]
        r3 = _rms(h1v)
        h1n = h1v * r3
        dn3 = acc * (1.0 + sc2_r[...])
        dh1n = dn3 * g3_r[...]
        dh1 = dy_r[rows, :].astype(F32) + r3 * (dh1n - h1n * jnp.mean(dh1n * h1n, axis=-1, keepdims=True))
        dh1_r[rows, :] = dh1.astype(BF16)
        mov = mo_r[rows, :].astype(F32)
        r2 = _rms(mov)
        mon = mov * r2
        dn2 = dh1 * gt1_r[...]
        dmon = dn2 * g2_r[...]
        dmo_r[rows, :] = (r2 * (dmon - mon * jnp.mean(dmon * mon, axis=-1, keepdims=True))).astype(BF16)
        zero_first(sums_r, i, rows)

        sums_r[0:1, :] += _colsum(acc)
        sums_r[1:2, :] += _colsum(acc * (h1n * g3_r[...]))
        sums_r[2:3, :] += _colsum(dn3 * h1n)
        sums_r[3:4, :] += _colsum(dh1 * (mon * g2_r[...]))
        sums_r[4:5, :] += _colsum(dn2 * mon)

    dh1, dmo, sums_m = _mm("d_a2_pre_ffn", dhff, wf_1, _NT, t, d, d_ff, tr, d, tk_f,
                           [(h1,) + row_d, (dy,) + row_d, (mo,) + row_d, (sc2,) + vec_d, (g_pre_ffn,) + vec_d,
                            (gt1,) + vec_d, (g_post_mix,) + vec_d] + _after(sent_ff1),
                           [(jax.ShapeDtypeStruct((t, d), BF16),) + row_d, (jax.ShapeDtypeStruct((t, d), BF16),) + row_d,
                            (jax.ShapeDtypeStruct((8, d), F32),) + sums_d], pre_ffn_bwd, row_chunk=rc)
    gw_o, gw_o_16 = _grad_w("grad_w_o", merged, dmo)

    n_j = d // tn_d

    def merge_bwd_body(dmo_ref, wo_ref, ga_ref, gb_ref, ya_ref, yb_ref, dya_ref, dyb_ref, dproj_ref, acc_s):
        g = pl.program_id(2)

        @pl.when(g == 0)
        def _():
            dm = _dot(dmo_ref[...], wo_ref[...], _NT)
            acc_s[...] = dm
            sa = _sigmoid(ga_ref[...])
            dya_ref[...] = (dm * sa).astype(BF16)
            dproj_ref[...] = (dm * ya_ref[...].astype(F32) * sa * (1.0 - sa)).astype(BF16)

        @pl.when(g == 1)
        def _():
            dm = acc_s[...]
            sb = _sigmoid(gb_ref[...])
            dyb_ref[...] = (dm * sb).astype(BF16)
            dproj_ref[...] = (dm * yb_ref[...].astype(F32) * sb * (1.0 - sb)).astype(BF16)

    tile3 = pl.BlockSpec((tm, tn_d), lambda i, j, g: (i, j))
    dy_a, dy_b, dproj = pl.pallas_call(
        merge_bwd_body, name="d_merged", grid=(t // tm, n_j, 2),
        in_specs=[pl.BlockSpec((tm, d), lambda i, j, g: (i, 0)), pl.BlockSpec((tn_d, d), lambda i, j, g: (j, 0)),
                  pl.BlockSpec((tm, tn_d), lambda i, j, g: (i, ga_blk + j)),
                  pl.BlockSpec((tm, tn_d), lambda i, j, g: (i, gb_blk + j)), tile3, tile3],
        out_specs=[tile3, tile3, pl.BlockSpec((tm, tn_d), lambda i, j, g: (i, ga_blk + g * n_j + j))],
        out_shape=[jax.ShapeDtypeStruct((t, d), BF16), jax.ShapeDtypeStruct((t, d), BF16),
                   jax.ShapeDtypeStruct((t, n_in), BF16)],
        scratch_shapes=[pltpu.VMEM((tm, tn_d), F32)], compiler_params=_params(3),
    )(dmo, wf_o, proj, proj, y_a, y_b)

    tn_w = _tile(width, 512)
    blk_w = ((tm, tn_w), lambda i, j: (i, j))
    dout_a, = _mm("d_out_a", dy_a, wf_a, _NT, t, width, d, tm, tn_w, d, [],
                  [(jax.ShapeDtypeStruct((t, width), BF16),) + blk_w], store_bf16)
    dout_b, = _mm("d_out_b", dy_b, wf_b, _NT, t, width, d, tm, tn_w, d, [],
                  [(jax.ShapeDtypeStruct((t, width), BF16),) + blk_w], store_bf16)
    gw_a, gw_a_16 = _grad_w("grad_w_a_out", out_a, dy_a)
    gw_b, gw_b_16 = _grad_w("grad_w_b_out", out_b, dy_b)

    w_st = jnp.swapaxes(w_spatial[0], 1, 2)
    dproj, dg_sgu, dw_sp, dbst = _sgu_bwd(proj, dout_b, dproj, g_sgu_norm, w_spatial[0], w_st, bst, width, z_block)
    sent_mid = send_grads("mid", [gw_a_16, gw_b_16, gw_o_16], big_axes[1:4])
    dproj, dgh_heads, dlb, _ = _hgrn_bwd(proj, osum, dout_a, dproj, lb, g_hgrn_norm, width, _NO_EXCHANGE,
                                         after=sent_mid)
    gw_in_top, gw_in_top16 = _grad_w("grad_w_in_top", a1, dproj, rows=(0, d // 2))
    sent_top = send_grads("in_top", [gw_in_top16], big_axes[:1])
    gw_in_bot, gw_in_bot16 = _grad_w("grad_w_in_bot", a1, dproj, token=sent_top, rows=(d // 2, d // 2))
    sent_in = send_grads("in_bot", [gw_in_bot16], big_axes[:1])

    def pre_mix_bwd(acc, i, j, extra_refs, out_refs, rows):
        x_r, dh1_r, sc1_r, g1_r = extra_refs[:4]
        dx_r, sums_r = out_refs
        xv = x_r[rows, :]
        r1 = _rms(xv)
        xn = xv * r1
        dn1 = acc * (1.0 + sc1_r[...])
        dxn = dn1 * g1_r[...]
        dx_r[rows, :] = dh1_r[rows, :].astype(F32) + r1 * (dxn - xn * jnp.mean(dxn * xn, axis=-1, keepdims=True))
        zero_first(sums_r, i, rows)

        sums_r[0:1, :] += _colsum(acc)
        sums_r[1:2, :] += _colsum(acc * (xn * g1_r[...]))
        sums_r[2:3, :] += _colsum(dn1 * xn)

    tk_in = _tile(n_in, 1024)
    grad_x, sums_x = _mm(
        "d_a1_pre_mix", dproj, wf_in, _NT, t, d, n_in, tr, d, tk_in,
        [(x2,) + row_d, (dh1,) + row_d, (sc1,) + vec_d, (g_pre_mix,) + vec_d] + _after(sent_in),
        [(jax.ShapeDtypeStruct((t, d), F32),) + row_d, (jax.ShapeDtypeStruct((8, d), F32),) + sums_d],
        pre_mix_bwd, row_chunk=rc)

    dmod = jnp.concatenate([sums_x[0:2], sums_m[3:4], sums_m[0:2], sums_f[1:2]], axis=0).reshape(N_DEV, n_ada // LANE, LANE)
    ada_rows = -(-(n_ada // LANE) // 8) * 8
    dmod = jnp.pad(dmod, ((0, 0), (0, ada_rows - n_ada // LANE), (0, 0))).reshape(N_DEV * ada_rows, LANE)
    parts = [dmod, _rows(sums_x[2:3]), _rows(sums_m[4:5]), _rows(sums_m[2:3]), _rows(sums_f[2:3]),
             _rows(jnp.sum(dgh_heads, axis=0)), _rows(dg_sgu), _rows(dw_sp), _rows(dbst.T)]
    n_params = sum(p.shape[0] for p in parts)
    parts.append(jnp.full((8, LANE), loss_mine, F32))
    n_common = n_params + 8
    payload = jnp.concatenate(parts + [_rows(dlb)], axis=0)

    moms = [m_w_in, m_w_a_out, m_w_b_out, m_w_o, m_w_ff1, m_w_ff2]
    vars_ = [v_w_in, v_w_a_out, v_w_b_out, v_w_o, v_w_ff1, v_w_ff2]
    big_out = {}

    def big_update(nm, parts):
        k = big_names.index(nm)
        outs = _adamw_big("adamw_" + nm, me_arr, big[k], moms[k][0], vars_[k][0], parts, big_axes[k])
        big_out[nm] = [o[None] for o in outs]
        return outs[0]

    land_ff2, = received_grads("ff2", grad_x)
    done = big_update("w_ff2", [(gw_ff2, land_ff2)])
    land_ff1, = received_grads("ff1", done)
    done = big_update("w_ff1", [(gw_ff1, land_ff1)])
    land_a, land_b, land_o = received_grads("mid", done)
    big_update("w_a_out", [(gw_a, land_a)])
    big_update("w_b_out", [(gw_b, land_b)])
    done = big_update("w_o", [(gw_o, land_o)])

    payload, _ = lax.optimization_barrier((payload, done))
    gathered = _all_gather_small("gather_small_grads", payload)

    dmod_mine = lax.dynamic_slice_in_dim(gathered[:, :N_DEV * ada_rows, :].reshape(N_DEV, N_DEV, ada_rows * LANE),
                                         me, 1, axis=1)[:, 0, :n_ada]
    ada_out = [o[None] for o in _adamw_ada(sc_all.T, dmod_mine, w_ada[0], m_w_ada[0], v_w_ada[0])]

    def pack(b_, g1_, g2_, g3_, g4_, gh_, gs_, ws_, bs_):
        b3 = b_.reshape(N_DEV, n_ada // LANE, LANE)
        b3 = jnp.pad(b3, ((0, 0), (0, ada_rows - n_ada // LANE), (0, 0))).reshape(N_DEV * ada_rows, LANE)
        return jnp.concatenate([b3, _rows(g1_), _rows(g2_), _rows(g3_), _rows(g4_), _rows(gh_), _rows(gs_),
                                _rows(ws_), _rows(bs_), jnp.zeros((8, LANE), F32)], axis=0)

    small_w = (b_ada, g_pre_mix, g_post_mix, g_pre_ffn, g_post_ffn, g_hgrn_norm, g_sgu_norm, w_spatial, b_spatial)
    small_m = (m_b_ada, m_g_pre_mix, m_g_post_mix, m_g_pre_ffn, m_g_post_ffn, m_g_hgrn_norm, m_g_sgu_norm, m_w_spatial, m_b_spatial)
    small_v = (v_b_ada, v_g_pre_mix, v_g_post_mix, v_g_pre_ffn, v_g_post_ffn, v_g_hgrn_norm, v_g_sgu_norm, v_w_spatial, v_b_spatial)
    packed = _adamw_small(gathered[:, :n_common, :], pack(*small_w), pack(*small_m), pack(*small_v))

    def unpack(slab):
        outs, at = [], 0
        b3 = slab[:N_DEV * ada_rows].reshape(N_DEV, ada_rows, LANE)[:, :n_ada // LANE, :]
        outs.append(b3.reshape(b_ada.shape))
        at = N_DEV * ada_rows
        for ref in small_w[1:]:
            n_el = ref.size
            n_r = -(-(n_el // LANE) // 8) * 8
            outs.append(slab[at:at + n_el // LANE].reshape(ref.shape))
            at += n_r
        return outs

    small_out = [unpack(s) for s in packed]
    loss = packed[0][n_params, 0]

    dlb_all = gathered[:, n_common:n_common + 2 * heads, :].reshape(N_DEV, 2, heads, LANE)
    dlb_mine = lax.dynamic_index_in_dim(dlb_all, me, axis=2, keepdims=False)
    lb_out = _adamw_lb(dlb_mine, lb_logits, m_lb_logits, v_lb_logits)

    land_top, = received_grads("in_top", ada_out[0])
    land_bot, = received_grads("in_bot", land_top)
    big_update("w_in", [(gw_in_top, land_top), (gw_in_bot, land_bot)])

    order = ["w_ada", "b_ada", "g_pre_mix", "g_post_mix", "g_pre_ffn", "g_post_ffn", "w_in", "lb_logits", "g_hgrn_norm",
             "w_a_out", "g_sgu_norm", "w_spatial", "b_spatial", "w_b_out", "w_o", "w_ff1", "w_ff2"]
    small_names = ["b_ada", "g_pre_mix", "g_post_mix", "g_pre_ffn", "g_post_ffn", "g_hgrn_norm", "g_sgu_norm", "w_spatial", "b_spatial"]

    def leaf(kind, nm):
        if nm == "w_ada":
            return ada_out[kind]
        if nm == "lb_logits":
            return lb_out[kind]
        if nm in big_out:
            return big_out[nm][kind]
        return small_out[kind][small_names.index(nm)]

    result = [loss, grad_x[None]]
    for kind in range(4):
        result += [leaf(kind, nm) for nm in order]
    return tuple(result)
```

```python
import math

import jax
import jax.numpy as jnp
from jax import lax
from jax.experimental import pallas as pl
from jax.experimental.pallas import tpu as pltpu

F32 = jnp.float32
BF16 = jnp.bfloat16
MESH = pl.DeviceIdType.MESH
HIGHEST = lax.Precision.HIGHEST

N_DEV = 8
HEAD = 128
A_CHUNK = 32
N_MOD = 6
EPS = 1e-6
LANE = 128
VMEM_LIMIT = 60 * 1024 * 1024

ADAM_LR = 0.001
ADAM_B1 = 0.9
ADAM_B2 = 0.999
ADAM_EPS = 1e-08
ADAM_WD = 0.01
ADAM_STEP = 10

_NN = (((1,), (0,)), ((), ()))
_NT = (((1,), (1,)), ((), ()))
_TN = (((0,), (0,)), ((), ()))


def _dot(a, b, dims=_NN, precision=None):
    return lax.dot_general(a, b, dims, preferred_element_type=F32, precision=precision)


def _bdot(a, b, dims=_NN):
    return _dot(a.astype(BF16), b.astype(BF16), dims)


def _params(n_grid):
    return pltpu.CompilerParams(dimension_semantics=("arbitrary",) * n_grid, vmem_limit_bytes=VMEM_LIMIT)


def _dev_index():
    return lax.axis_index("x") * 4 + lax.axis_index("y") * 2 + lax.axis_index("c")


def _dev_coords(i):
    return (i // 4, (i // 2) % 2, i % 2)


def _sigmoid(x):
    return 1.0 / (1.0 + jnp.exp(-x))


def _erf(x):
    ax = jnp.abs(x)
    t = 1.0 / (1.0 + 0.3275911 * ax)
    poly = ((((1.061405429 * t - 1.453152027) * t + 1.421413741) * t - 0.284496736) * t + 0.254829592) * t
    y = 1.0 - poly * jnp.exp(-ax * ax)
    return jnp.where(x < 0, -y, y)


def _gelu_and_grad(x):
    cdf = 0.5 * (1.0 + _erf(x * (2.0 ** -0.5)))
    pdf = jnp.exp(-0.5 * x * x) * (1.0 / math.sqrt(2.0 * math.pi))
    return x * cdf, cdf + x * pdf


def _rms(x):
    return lax.rsqrt(jnp.mean(x * x, axis=-1, keepdims=True) + EPS)


def _colsum(x):
    return jnp.sum(x, axis=0, keepdims=True)


def _tile(n, want):
    if n <= want:
        return n
    t = (want // LANE) * LANE
    while n % t:
        t -= LANE
    assert t > 0, (n, want)
    return t


def _all_gather_small(name, payload):
    rows = payload.shape[0]

    def body(p_ref, out_ref, send_sems, recv_sems, local_sem):
        me = _dev_index()
        mine = pltpu.make_async_copy(p_ref, out_ref.at[me], local_sem)
        mine.start()
        sends = []
        for r in range(1, N_DEV):
            peer = (me + r) % N_DEV
            cp = pltpu.make_async_remote_copy(
                src_ref=p_ref, dst_ref=out_ref.at[me], send_sem=send_sems.at[r - 1], recv_sem=recv_sems.at[r - 1],
                device_id=_dev_coords(peer), device_id_type=MESH)
            cp.start()
            sends.append(cp)
        for r in range(1, N_DEV):
            src = (me + N_DEV - r) % N_DEV
            pltpu.make_async_remote_copy(
                src_ref=p_ref, dst_ref=out_ref.at[src], send_sem=send_sems.at[r - 1], recv_sem=recv_sems.at[r - 1],
                device_id=_dev_coords(src), device_id_type=MESH).wait_recv()
        for cp in sends:
            cp.wait_send()
        mine.wait()

    return pl.pallas_call(
        body, name=name,
        out_shape=jax.ShapeDtypeStruct((N_DEV, rows, LANE), F32),
        in_specs=[pl.BlockSpec(memory_space=pltpu.VMEM)],
        out_specs=pl.BlockSpec(memory_space=pltpu.VMEM),
        scratch_shapes=[pltpu.SemaphoreType.DMA((N_DEV - 1,)), pltpu.SemaphoreType.DMA((N_DEV - 1,)),
                        pltpu.SemaphoreType.DMA],
        compiler_params=pltpu.CompilerParams(vmem_limit_bytes=VMEM_LIMIT),
    )(payload)


def _region(ref, dev, axis, n):
    start = pl.multiple_of(dev * n, LANE if axis == 1 else 16)
    return ref.at[:, pl.ds(start, n)] if axis == 1 else ref.at[pl.ds(start, n), :]


class _Exchange:
    def __init__(self, arrays, out_shapes, sems, start, finish):
        self.arrays, self.out_shapes, self.sems, self.start, self.finish = arrays, out_shapes, sems, start, finish


def _scatter_plan(grads, axes):
    n_w = len(grads)
    lands = []
    for g, ax in zip(grads, axes):
        shp = (g.shape[0], g.shape[1] // N_DEV) if ax == 1 else (g.shape[0] // N_DEV, g.shape[1])
        lands.append(jax.ShapeDtypeStruct((N_DEV - 1,) + shp, BF16))
    widths = [ld.shape[1 + ax] for ld, ax in zip(lands, axes)]

    def copy(w, r, g_refs, l_refs, sems, block, to):
        return pltpu.make_async_remote_copy(
            src_ref=_region(g_refs[w], block, axes[w], widths[w]), dst_ref=l_refs[w].at[r - 1],
            send_sem=sems[0].at[w * (N_DEV - 1) + r - 1], recv_sem=sems[1].at[w * (N_DEV - 1) + r - 1],
            device_id=_dev_coords(to), device_id_type=MESH)

    def start(g_refs, l_refs, sems):
        me = _dev_index()
        for w in range(n_w):
            for r in range(1, N_DEV):
                owner = (me + r) % N_DEV
                copy(w, r, g_refs, l_refs, sems, owner, owner).start()

    def finish(g_refs, l_refs, sems):
        me = _dev_index()
        for w in range(n_w):
            for r in range(1, N_DEV):
                copy(w, r, g_refs, l_refs, sems, me, (me + N_DEV - r) % N_DEV).wait_recv()
        for w in range(n_w):
            for r in range(1, N_DEV):
                copy(w, r, g_refs, l_refs, sems, me, (me + r) % N_DEV).wait_send()

    sems = [pltpu.SemaphoreType.DMA((n_w * (N_DEV - 1),)), pltpu.SemaphoreType.DMA((n_w * (N_DEV - 1),))]
    return _Exchange(list(grads), lands, sems, start, finish)


def _places():
    x, y, c = lax.axis_index("x"), lax.axis_index("y"), lax.axis_index("c")
    return (x, y, c), (x, y, 1 - c), [(1 - x, y), (x, 1 - y), (1 - x, 1 - y)]


def _place_index(p):
    return p[0] * 4 + p[1] * 2 + p[2]


def _gather_stage_plans(fulls, axes):
    n_w = len(fulls)
    widths = [f.shape[ax] // N_DEV for f, ax in zip(fulls, axes)]
    shapes = [jax.ShapeDtypeStruct(f.shape, f.dtype) for f in fulls]

    def copy(per, w, k, f_refs, sems, block, to):
        part = _region(f_refs[w], _place_index(block), axes[w], widths[w])
        return pltpu.make_async_remote_copy(
            src_ref=part, dst_ref=part, send_sem=sems[0].at[w * per + k], recv_sem=sems[1].at[w * per + k],
            device_id=to, device_id_type=MESH)

    def start1(_, f_refs, sems):
        me, sib, chips = _places()
        for w in range(n_w):
            copy(4, w, 0, f_refs, sems, me, sib).start()
            for j, chip in enumerate(chips):
                copy(4, w, 1 + j, f_refs, sems, me, (*chip, me[2])).start()

    def finish1(_, f_refs, sems):
        me, sib, chips = _places()
        for w in range(n_w):
            copy(4, w, 0, f_refs, sems, sib, me).wait_recv()
            for j, chip in enumerate(chips):
                copy(4, w, 1 + j, f_refs, sems, (*chip, me[2]), me).wait_recv()
        for w in range(n_w):
            for k in range(4):
                copy(4, w, k, f_refs, sems, me, sib).wait_send()

    def start2(_, f_refs, sems):
        me, sib, chips = _places()
        for w in range(n_w):
            for j, chip in enumerate(chips):
                copy(3, w, j, f_refs, sems, (*chip, me[2]), sib).start()

    def finish2(_, f_refs, sems):
        me, sib, chips = _places()
        for w in range(n_w):
            for j, chip in enumerate(chips):
                copy(3, w, j, f_refs, sems, (*chip, sib[2]), me).wait_recv()
        for w in range(n_w):
            for j, chip in enumerate(chips):
                copy(3, w, j, f_refs, sems, (*chip, me[2]), sib).wait_send()

    sems1 = [pltpu.SemaphoreType.DMA((n_w * 4,)), pltpu.SemaphoreType.DMA((n_w * 4,))]
    sems2 = [pltpu.SemaphoreType.DMA((n_w * 3,)), pltpu.SemaphoreType.DMA((n_w * 3,))]
    return _Exchange([], shapes, sems1, start1, finish1), _Exchange([], shapes, sems2, start2, finish2)


_HBM = pl.BlockSpec(memory_space=pltpu.HBM)
_SEM = pl.BlockSpec(memory_space=pltpu.SEMAPHORE)
_EFFECT = pltpu.SideEffectType.DATAFLOW_SIDE_EFFECTING


def _split_start(name, plan, landing=None):
    n_in, n_out, n_sem = len(plan.arrays), len(plan.out_shapes), len(plan.sems)

    def body(*refs):
        ins, lands = refs[:n_in], refs[n_in:n_in + n_out]
        sems = refs[n_in + n_out:n_in + n_out + n_sem]
        token = refs[-1]
        plan.start(ins, lands, sems)
        token[...] = jnp.zeros_like(token)

    hbm = lambda a: pltpu.HBM(a.shape, a.dtype)
    results = pl.pallas_call(
        body, name=name,
        out_shape=tuple(plan.sems) + tuple(hbm(a) for a in plan.arrays) + tuple(hbm(a) for a in plan.out_shapes)
        + (jax.ShapeDtypeStruct((8, LANE), F32),),
        in_specs=(_HBM,) * (n_in + n_out),
        out_specs=(_SEM,) * n_sem + (_HBM,) * (n_in + n_out) + (pl.BlockSpec(memory_space=pltpu.VMEM),),
        input_output_aliases={i: n_sem + i for i in range(n_in + n_out)},
        compiler_params=pltpu.CompilerParams(has_side_effects=_EFFECT),
    )(*[pltpu.with_memory_space_constraint(a, pltpu.HBM) for a in plan.arrays],
      *[pltpu.with_memory_space_constraint(a, pltpu.HBM)
        for a in (landing if landing is not None else [lax.empty(a.shape, a.dtype) for a in plan.out_shapes])])
    return results[:n_sem], results[n_sem:n_sem + n_in + n_out], results[-1]


def _split_wait(name, plan, sems, thru, after):
    n_in, n_out, n_sem = len(plan.arrays), len(plan.out_shapes), len(plan.sems)

    def body(*refs):
        ins, lands = refs[:n_in], refs[n_in:n_in + n_out]
        sem_refs = refs[n_in + n_out:n_in + n_out + n_sem]
        plan.finish(ins, lands, sem_refs)

    hbm = lambda a: pltpu.HBM(a.shape, a.dtype)
    results = pl.pallas_call(
        body, name=name,
        out_shape=tuple(hbm(a) for a in plan.arrays) + tuple(hbm(a) for a in plan.out_shapes),
        in_specs=(_HBM,) * (n_in + n_out) + (_SEM,) * n_sem + (pl.BlockSpec(memory_space=pl.ANY),),
        out_specs=(_HBM,) * (n_in + n_out),
        input_output_aliases={i: i for i in range(n_in + n_out)},
        compiler_params=pltpu.CompilerParams(has_side_effects=_EFFECT),
    )(*thru, *sems, after)
    return results[:n_in], results[n_in:]


def _cast_into_full(name, me, w, axis):
    r, c = w.shape
    tr = _tile(r, 256)
    if axis == 1:
        shape, place = (r, c * N_DEV), pl.BlockSpec((tr, c), lambda i, me_ref: (i, me_ref[0]))
    else:
        shape, place = (r * N_DEV, c), pl.BlockSpec((tr, c), lambda i, me_ref: (me_ref[0] * (r // tr) + i, 0))

    def body(me_ref, w_ref, o_ref):
        o_ref[...] = w_ref[...].astype(BF16)

    grid_spec = pltpu.PrefetchScalarGridSpec(
        num_scalar_prefetch=1, grid=(r // tr,),
        in_specs=[pl.BlockSpec((tr, c), lambda i, me_ref: (i, 0))], out_specs=place)
    return pl.pallas_call(body, name=name, grid_spec=grid_spec, out_shape=jax.ShapeDtypeStruct(shape, BF16),
                          compiler_params=_params(1))(me, w)


def _mm(name, a, b, dims, m, n, k, tm, tn, tk, extras, outs, epilogue, row_chunk=None, a_col_block=0):
    ni, nj, nk = m // tm, n // tn, k // tk
    ne, no = len(extras), len(outs)
    if dims == _TN:
        a_spec = pl.BlockSpec((tk, tm), lambda i, j, kk: (kk, i + a_col_block))
    else:
        a_spec = pl.BlockSpec((tm, tk), lambda i, j, kk: (i, kk))
    if dims == _NT:
        b_spec = pl.BlockSpec((tn, tk), lambda i, j, kk: (j, kk))
    else:
        b_spec = pl.BlockSpec((tk, tn), lambda i, j, kk: (kk, j))
    chunks = [slice(None)] if row_chunk is None else [slice(r, r + row_chunk) for r in range(0, tm, row_chunk)]

    def lift(index_map):
        return lambda i, j, kk: index_map(i, j)

    def body(a_ref, b_ref, *rest):
        extra_refs, out_refs, rest = rest[:ne], rest[ne:ne + no], rest[ne + no:]
        i, j, kk = pl.program_id(0), pl.program_id(1), pl.program_id(2)
        if nk == 1:
            part = _dot(a_ref[...], b_ref[...], dims)
            for rows in chunks:
                epilogue(part[rows], i, j, extra_refs, out_refs, rows)
        else:
            acc_ref = rest[0]

            @pl.when(kk == 0)
            def _():
                acc_ref[...] = _dot(a_ref[...], b_ref[...], dims)

            @pl.when(kk > 0)
            def _():
                acc_ref[...] += _dot(a_ref[...], b_ref[...], dims)

            @pl.when(kk == nk - 1)
            def _():
                for rows in chunks:
                    epilogue(acc_ref[rows, :], i, j, extra_refs, out_refs, rows)

    once = dict(pipeline_mode=pl.Buffered(1)) if (row_chunk is not None and nk > 1) else {}
    return pl.pallas_call(
        body, name=name,
        grid=(ni, nj, nk),
        in_specs=[a_spec, b_spec] + [pl.BlockSpec(bs, lift(im), **once) for _, bs, im in extras],
        out_specs=[pl.BlockSpec(bs, lift(im), **once) for _, bs, im in outs],
        out_shape=[sd for sd, _, _ in outs],
        scratch_shapes=[pltpu.VMEM((tm, tn), F32)] if nk > 1 else [],
        compiler_params=_params(3),
    )(a, b, *[arr for arr, _, _ in extras])


def _after(token):
    return [(token, (8, LANE), lambda i, j: (0, 0))]


def _grad_w(name, a, dc, token=None, tm=512, tn=1024, rows=None):
    t = a.shape[0]
    n = dc.shape[1]
    first, m = rows if rows is not None else (0, a.shape[1])
    tm, tn = _tile(m, tm), _tile(n, tn)
    assert first % tm == 0

    def epilogue(acc, i, j, extra_refs, out_refs, rows):
        out_refs[0][...] = acc
        out_refs[1][...] = acc.astype(BF16)

    blk = ((tm, tn), lambda i, j: (i, j))
    return _mm(name, a, dc, _TN, m, n, t, tm, tn, t, _after(token) if token is not None else [],
               [(jax.ShapeDtypeStruct((m, n), F32),) + blk, (jax.ShapeDtypeStruct((m, n), BF16),) + blk], epilogue,
               a_col_block=first // tm)


def _proj_gather(a1, w_shard, order):
    t, d = a1.shape
    nsh = w_shard.shape[1]
    tm = _tile(t, 512)
    n_i = t // tm

    def body(ord_ref, a_ref, wsh_ref, proj_ref, full_ref, bbuf, bsem, send_sems, recv_sems, own_sem):
        s, i = pl.program_id(0), pl.program_id(1)
        me, sib, chips = _places()
        near, far = chips[:2], chips[2]
        steps = ([(me, None, None), (sib, 0, None)]
                 + [((*ch, me[2]), 1 + j, 4 + j) for j, ch in enumerate(near)]
                 + [((*ch, sib[2]), 4 + j, None) for j, ch in enumerate(near)]
                 + [((*far, me[2]), 3, 6), ((*far, sib[2]), 6, None)])
        blocks = [st[0] for st in steps]

        def part(block):
            return _region(full_ref, _place_index(block), 1, nsh)

        def remote(k, block, to, from_shard=False):
            return pltpu.make_async_remote_copy(
                src_ref=wsh_ref if from_shard else part(block), dst_ref=part(block),
                send_sem=send_sems.at[k], recv_sem=recv_sems.at[k], device_id=to, device_id_type=MESH)

        def load(pos):
            src = wsh_ref if pos == 0 else part(blocks[pos])
            return pltpu.make_async_copy(src, bbuf.at[pos % 2], bsem.at[pos % 2])

        own = pltpu.make_async_copy(wsh_ref, part(me), own_sem)

        @pl.when((s == 0) & (i == 0))
        def _():
            own.start()
            remote(0, me, sib, True).start()
            for j, ch in enumerate(chips):
                remote(1 + j, me, (*ch, me[2]), True).start()
            load(0).start()
            load(0).wait()

        for pos in range(1, N_DEV):
            @pl.when((s == pos) & (i == 0))
            def _():
                load(pos).wait()

        for pos in range(N_DEV - 1):
            @pl.when((s == pos) & (i == n_i - 1))
            def _():
                nxt = pos + 1
                block, arrives_on, pass_on_with = steps[nxt]
                remote(arrives_on, block, me).wait_recv()
                if pass_on_with is not None:
                    remote(pass_on_with, block, sib).start()
                load(nxt).start()

        proj_ref[...] = _dot(a_ref[...], bbuf[s % 2])

        @pl.when((s == N_DEV - 1) & (i == n_i - 1))
        def _():
            for k in range(N_DEV - 1):
                remote(k, me, sib, True).wait_send()
            own.wait()

    grid_spec = pltpu.PrefetchScalarGridSpec(
        num_scalar_prefetch=1, grid=(N_DEV, n_i),
        in_specs=[pl.BlockSpec((tm, d), lambda s, i, ord_ref: (i, 0)), pl.BlockSpec(memory_space=pl.ANY)],
        out_specs=[pl.BlockSpec((tm, nsh), lambda s, i, ord_ref: (i, ord_ref[s])), pl.BlockSpec(memory_space=pl.ANY)],
        scratch_shapes=[pltpu.VMEM((2, d, nsh), BF16), pltpu.SemaphoreType.DMA((2,)),
                        pltpu.SemaphoreType.DMA((N_DEV - 1,)), pltpu.SemaphoreType.DMA((N_DEV - 1,)),
                        pltpu.SemaphoreType.DMA])
    return pl.pallas_call(
        body, name="proj_gather", grid_spec=grid_spec,
        out_shape=[jax.ShapeDtypeStruct((t, nsh * N_DEV), F32), jax.ShapeDtypeStruct((d, nsh * N_DEV), BF16)],
        compiler_params=_params(2),
    )(order, a1, w_shard)


def _cast_bf16(name, w):
    r, c = w.shape
    tr = _tile(r, 256)
    return pl.pallas_call(
        lambda w_ref, o_ref: o_ref.__setitem__(Ellipsis, w_ref[...].astype(BF16)), name=name,
        grid=(r // tr,), in_specs=[pl.BlockSpec((tr, c), lambda i: (i, 0))],
        out_specs=pl.BlockSpec((tr, c), lambda i: (i, 0)), out_shape=jax.ShapeDtypeStruct((r, c), BF16),
        compiler_params=_params(1),
    )(w)


def _prep_small(c_row, lb_logits):
    d = c_row.shape[1]
    rows = d // LANE

    def body(c_ref, l_ref, o_ref):
        cv = c_ref[...]
        o_ref[0:rows, :] = cv * _sigmoid(cv)
        lbs = [_sigmoid(l_ref[dr][0:1, :] - l_ref[dr][1:2, :]) for dr in range(2)]
        o_ref[rows:rows + 8, :] = jnp.concatenate(lbs + [jnp.zeros((6, LANE), F32)], axis=0)

    return pl.pallas_call(
        body, name="prep_small", out_shape=jax.ShapeDtypeStruct((rows + 8, LANE), F32),
    )(c_row.reshape(rows, LANE), lb_logits)


def _mod_shard(sc_all, w_ada_shard, b_shard):
    d, n = w_ada_shard.shape
    tn = _tile(n, 512)

    def body(s_ref, w_ref, b_ref, o_ref):
        o_ref[...] = _dot(s_ref[...], w_ref[...], precision=HIGHEST) + b_ref[...]

    return pl.pallas_call(
        body, name="mod_shard", grid=(n // tn,),
        in_specs=[pl.BlockSpec((N_DEV, d), lambda j: (0, 0)), pl.BlockSpec((d, tn), lambda j: (0, j)),
                  pl.BlockSpec((1, tn), lambda j: (0, j))],
        out_specs=pl.BlockSpec((N_DEV, tn), lambda j: (0, j)),
        out_shape=jax.ShapeDtypeStruct((N_DEV, n), F32), compiler_params=_params(1),
    )(sc_all, w_ada_shard, b_shard)


def _norm_mod(x, gain, shift, scale):
    t, d = x.shape
    tm = _tile(t, 512)

    def body(x_ref, g_ref, sh_ref, sc_ref, o_ref):
        xv = x_ref[...]
        o_ref[...] = ((xv * _rms(xv) * g_ref[...]) * (1.0 + sc_ref[...]) + sh_ref[...]).astype(BF16)

    vec = pl.BlockSpec((1, d), lambda i: (0, 0))
    return pl.pallas_call(
        body, name="norm_mod", grid=(t // tm,),
        in_specs=[pl.BlockSpec((tm, d), lambda i: (i, 0)), vec, vec, vec],
        out_specs=pl.BlockSpec((tm, d), lambda i: (i, 0)), out_shape=jax.ShapeDtypeStruct((t, d), BF16),
        compiler_params=_params(1),
    )(x, gain, shift, scale)


def _chunk_masks():
    row = lax.broadcasted_iota(jnp.int32, (HEAD, HEAD), 0)
    col = lax.broadcasted_iota(jnp.int32, (HEAD, HEAD), 1)
    same = (row // A_CHUNK) == (col // A_CHUNK)
    return same & (col <= row), same & (col >= row)


def _ones(mask):
    return jnp.where(mask, 1.0, 0.0).astype(BF16)


def _dot_split(ones_bf16, x):
    hi = x.astype(BF16)
    lo = (x - hi.astype(F32)).astype(BF16)
    return _dot(ones_bf16, hi) + _dot(ones_bf16, lo)


def _hgrn_block(direction, f, lb, cum2):
    sf = _sigmoid(f)
    big_f = lb + (1.0 - lb) * sf
    k = (1.0 - lb) * (1.0 - sf)
    lf = jnp.log(big_f)
    both = _dot_split(cum2, lf)
    cf, cr = both[:HEAD], both[HEAD:]
    b, rest = (cf, cr - lf) if direction == 0 else (cr, cf - lf)
    return k, sf, big_f, jnp.exp(b), jnp.exp(-b), jnp.exp(rest)


def _hgrn_fwd(proj, lb, g_norm, width, after):
    t = proj.shape[0]
    heads = width // HEAD
    nb, nc = t // HEAD, t // A_CHUNK
    ua = 4 if nb % 4 == 0 else (2 if nb % 2 == 0 else 1)
    ub = 16 if nc % 16 == 0 else (8 if nc % 8 == 0 else 4)
    q_scale = HEAD ** -0.5

    def body(q_ref, ffw_ref, fbw_ref, v_ref, og_ref, lb_ref, g_ref, *rest):
        outa_ref, osum_ref, qd_s, ke_s, dc_s, o_s = rest[len(after):]
        tril, triu = _chunk_masks()
        cum2 = jnp.concatenate([_ones(tril), _ones(triu)], axis=0)
        f_refs = (ffw_ref, fbw_ref)
        lbs = (lb_ref[0:1, :], lb_ref[1:2, :])

        def phase_a(it, carry):
            loaded = []
            for u in range(ua):
                rows = pl.ds(pl.multiple_of((it * ua + u) * HEAD, HEAD), HEAD)
                loaded.append((rows, q_ref[rows, :], v_ref[rows, :], ffw_ref[rows, :], fbw_ref[rows, :]))
            chains = [(d, rows, qv * q_scale, vv.astype(BF16), fv)
                      for rows, qv, vv, f0, f1 in loaded for d, fv in ((0, f0), (1, f1))]
            blocks = [_hgrn_block(d, fv, lbs[d], cum2) for d, _, _, _, fv in chains]
            scaled = [(qv * eb, k * enb, k * erest, eb * erest)
                      for (_, _, qv, _, _), (k, _, _, eb, enb, erest) in zip(chains, blocks)]
            atts = [jnp.where(tril if d == 0 else triu, _bdot(qd, kd, _NT), 0.0)
                    for (d, _, _, _, _), (qd, kd, _, _) in zip(chains, scaled)]
            intras = [_bdot(att, vv) for att, (_, _, _, vv, _) in zip(atts, chains)]
            results = [(d, rows, o_intra, qd.astype(BF16), ke.astype(BF16), decay)
                       for (d, rows, _, _, _), (qd, _, ke, decay), o_intra in zip(chains, scaled, intras)]
            for d, rows, o_intra, qd16, ke16, decay in results:
                o_s[d, rows, :] = o_intra
                qd_s[d, rows, :] = qd16
                ke_s[d, rows, :] = ke16
                dc_s[d, rows, :] = decay
            return carry

        lax.fori_loop(0, nb // ua, phase_a, 0)

        def phase_b(it, states):
            loaded = []
            for u in range(ub):
                n = it * ub + u
                for d in range(2):
                    c = n if d == 0 else nc - 1 - n
                    start = pl.multiple_of(c * A_CHUNK, A_CHUNK)
                    rows = pl.ds(start, A_CHUNK)
                    loaded.append((d, rows, qd_s[d, rows, :], ke_s[d, rows, :], v_ref[rows, :],
                                   dc_s[d, pl.ds(start, 1), :], o_s[d, rows, :]))
            increments = [_dot(vv.astype(BF16), ke16, _TN) for _, _, _, ke16, vv, _, _ in loaded]
            states = list(states)
            befores = []
            for (d, _, _, _, _, decay, _), inc in zip(loaded, increments):
                befores.append(states[d].astype(BF16))
                states[d] = states[d] * decay + inc
            inters = [_dot(qd16, before, _NT) for (_, _, qd16, _, _, _, _), before in zip(loaded, befores)]
            for (d, rows, _, _, _, _, o_intra), o_inter in zip(loaded, inters):
                o_s[d, rows, :] = o_intra + o_inter
            return tuple(states)

        zero_state = jnp.zeros((HEAD, HEAD), F32)
        lax.fori_loop(0, nc // ub, phase_b, (zero_state, zero_state))

        def phase_c(i, carry):
            rows = pl.ds(pl.multiple_of(i * HEAD, HEAD), HEAD)
            o = o_s[0, rows, :] + o_s[1, rows, :]
            osum_ref[rows, :] = o
            og = og_ref[rows, :]
            outa_ref[rows, :] = (o * _rms(o) * g_ref[...] * (og * _sigmoid(og))).astype(BF16)
            return carry

        lax.fori_loop(0, nb, phase_c, 0)

    def col(p):
        return pl.BlockSpec((t, HEAD), lambda h: (0, p * heads + h))

    return pl.pallas_call(
        body, name="hgrn_fwd", grid=(heads,),
        in_specs=[col(0), col(1), col(2), col(3), col(4),
                  pl.BlockSpec((2, HEAD), lambda h: (0, h)), pl.BlockSpec((1, HEAD), lambda h: (0, 0))]
        + [pl.BlockSpec(memory_space=pl.ANY)] * len(after),
        out_specs=[pl.BlockSpec((t, HEAD), lambda h: (0, h)), pl.BlockSpec((t, HEAD), lambda h: (0, h))],
        out_shape=[jax.ShapeDtypeStruct((t, width), BF16), jax.ShapeDtypeStruct((t, width), F32)],
        scratch_shapes=[pltpu.VMEM((2, t, HEAD), BF16), pltpu.VMEM((2, t, HEAD), BF16), pltpu.VMEM((2, t, HEAD), F32),
                        pltpu.VMEM((2, t, HEAD), F32)],
        compiler_params=_params(1),
    )(proj, proj, proj, proj, proj, lb, g_norm, *after)


def _sgu_core(u_pre, v_pre, g_v, ws_ref, bst):
    u, du = _gelu_and_grad(u_pre)
    v, dv = _gelu_and_grad(v_pre)
    mu = jnp.mean(v, axis=-1, keepdims=True)
    dlt = v - mu
    rstd = lax.rsqrt(jnp.mean(dlt * dlt, axis=-1, keepdims=True) + EPS)
    vhat = dlt * rstd
    vn = vhat * g_v
    groups = vn.shape[1] // HEAD
    cols = []
    for g in range(groups):
        vm_g = _bdot(ws_ref[g], vn[:, g * HEAD:(g + 1) * HEAD]) + bst[:, g:g + 1]
        cols.append(vm_g)
    return u, du, dv, vhat, rstd, vn, jnp.concatenate(cols, axis=1)


def _sgu_fwd(proj, g_v, w_s, bst, width, z_block):
    t = proj.shape[0]

    def body(u_ref, v_ref, g_ref, ws_ref, bst_ref, o_ref):
        u, _, _, _, _, _, vm = _sgu_core(u_ref[...], v_ref[...], g_ref[...], ws_ref, bst_ref[...])
        o_ref[...] = (u * vm).astype(BF16)

    groups = width // HEAD
    return pl.pallas_call(
        body, name="sgu_fwd", grid=(t // HEAD,),
        in_specs=[pl.BlockSpec((HEAD, width), lambda i: (i, z_block)), pl.BlockSpec((HEAD, width), lambda i: (i, z_block + 1)),
                  pl.BlockSpec((1, width), lambda i: (0, 0)), pl.BlockSpec((groups, HEAD, HEAD), lambda i: (0, 0, 0)),
                  pl.BlockSpec((HEAD, groups), lambda i: (0, 0))],
        out_specs=pl.BlockSpec((HEAD, width), lambda i: (i, 0)),
        out_shape=jax.ShapeDtypeStruct((t, width), BF16), compiler_params=_params(1),
    )(proj, proj, g_v, w_s, bst)


def _sgu_bwd(proj, dout_b, dproj, g_v, w_s, w_st, bst, width, z_block):
    t = proj.shape[0]
    groups = width // HEAD
    nblk = t // HEAD

    def body(u_ref, v_ref, do_ref, g_ref, ws_ref, wst_ref, bst_ref, dproj_hbm,
             dz_ref, dg_ref, dws_ref, dbst_ref, res_s):
        i, p = pl.program_id(0), pl.program_id(1)

        @pl.when((i == 0) & (p == 0))
        def _():
            dg_ref[...] = jnp.zeros_like(dg_ref)
            dws_ref[...] = jnp.zeros_like(dws_ref)
            dbst_ref[...] = jnp.zeros_like(dbst_ref)

        @pl.when(p == 0)
        def _():
            g_v = g_ref[...]
            u, du, dv, vhat, rstd, vn, vm = _sgu_core(u_ref[...], v_ref[...], g_v, ws_ref, bst_ref[...])
            dout = do_ref[...].astype(F32)
            res_s[0] = (dout * vm * du).astype(BF16)
            dvm = dout * u
            dvn_cols = []
            for g in range(groups):
                sl = slice(g * HEAD, (g + 1) * HEAD)
                dvm_g = dvm[:, sl]
                dbst_ref[:, g:g + 1] += jnp.sum(dvm_g, axis=1, keepdims=True)
                dws_ref[g] += _bdot(dvm_g, vn[:, sl], _NT)
                dvn_cols.append(_bdot(wst_ref[g], dvm_g))
            dvn = jnp.concatenate(dvn_cols, axis=1)
            dg_ref[...] += _colsum(dvn * vhat)
            dvh = dvn * g_v
            dvg = rstd * (dvh - jnp.mean(dvh, axis=-1, keepdims=True)
                          - vhat * jnp.mean(dvh * vhat, axis=-1, keepdims=True))
            res_s[1] = (dvg * dv).astype(BF16)

        dz_ref[...] = res_s[p]

    n_in = dproj.shape[1]
    return pl.pallas_call(
        body, name="sgu_bwd", grid=(nblk, 2),
        in_specs=[pl.BlockSpec((HEAD, width), lambda i, p: (i, z_block)),
                  pl.BlockSpec((HEAD, width), lambda i, p: (i, z_block + 1)),
                  pl.BlockSpec((HEAD, width), lambda i, p: (i, 0)),
                  pl.BlockSpec((1, width), lambda i, p: (0, 0)),
                  pl.BlockSpec((groups, HEAD, HEAD), lambda i, p: (0, 0, 0)),
                  pl.BlockSpec((groups, HEAD, HEAD), lambda i, p: (0, 0, 0)),
                  pl.BlockSpec((HEAD, groups), lambda i, p: (0, 0)),
                  pl.BlockSpec(memory_space=pl.ANY)],
        out_specs=[pl.BlockSpec((HEAD, width), lambda i, p: (i, z_block + p)),
                   pl.BlockSpec((1, width), lambda i, p: (0, 0)),
                   pl.BlockSpec((groups, HEAD, HEAD), lambda i, p: (0, 0, 0)),
                   pl.BlockSpec((HEAD, groups), lambda i, p: (0, 0))],
        out_shape=[jax.ShapeDtypeStruct((t, n_in), BF16), jax.ShapeDtypeStruct((1, width), F32),
                   jax.ShapeDtypeStruct((groups, HEAD, HEAD), F32), jax.ShapeDtypeStruct((HEAD, groups), F32)],
        scratch_shapes=[pltpu.VMEM((2, HEAD, width), BF16)],
        input_output_aliases={7: 0},
        compiler_params=_params(2),
    )(proj, proj, dout_b, g_v, w_s, w_st, bst, dproj)


def _hgrn_bwd(proj, osum, dout_a, dproj, lb, g_norm, width, after):
    t = proj.shape[0]
    heads = width // HEAD
    nb = t // HEAD
    cpb = HEAD // A_CHUNK
    ubk = 2 if nb % 2 == 0 else 1
    q_scale = HEAD ** -0.5

    def body(q_ref, ffw_ref, fbw_ref, v_ref, og_ref, osum_ref, douta_ref, lb_ref, g_ref, dproj_hbm, after_hbm,
             out_ref, dgh_ref, dlb_ref, do_s, dq_s, dv_s, res_s, ck_s):
        p = pl.program_id(1)
        f_refs = (ffw_ref, fbw_ref)

        @pl.when(p == 0)
        def _():
            tril, triu = _chunk_masks()
            cum2 = jnp.concatenate([_ones(tril), _ones(triu)], axis=0)
            g_row = g_ref[...]

            def pass_norm(i, dgh):
                rows = pl.ds(pl.multiple_of(i * HEAD, HEAD), HEAD)
                o = osum_ref[rows, :]
                r = _rms(o)
                oh = o * r
                og = og_ref[rows, :]
                sg = _sigmoid(og)
                dout = douta_ref[rows, :].astype(F32)
                don = dout * (og * sg)
                res_s[4, rows, :] = (dout * (oh * g_row) * (sg * (1.0 + og * (1.0 - sg)))).astype(BF16)
                doh = don * g_row
                do_s[rows, :] = r * (doh - oh * jnp.mean(doh * oh, axis=-1, keepdims=True))
                return dgh + _colsum(don * oh)

            dgh_ref[...] = lax.fori_loop(0, nb, pass_norm, jnp.zeros((1, HEAD), F32))

            lbs = (lb_ref[0:1, :], lb_ref[1:2, :])
            zero_state = jnp.zeros((HEAD, HEAD), F32)

            def chunk_order(d):
                return list(range(cpb)) if d == 0 else list(range(cpb - 1, -1, -1))

            def chunk(x, j):
                return x[j * A_CHUNK:(j + 1) * A_CHUNK, :]

            def decay_row(e_big, j):
                return e_big[j * A_CHUNK:j * A_CHUNK + 1, :]

            def cat(parts):
                return jnp.concatenate([parts[j] for j in range(cpb)], axis=0)

            def block_states(d, start, incs, e_big):
                befores, st = {}, start
                for j in chunk_order(d):
                    befores[j] = st
                    st = st * decay_row(e_big, j) + incs[j]
                return befores, st

            def pass_states(it, states):
                loaded = []
                for u in range(ubk):
                    for d in range(2):
                        blk = it * ubk + u if d == 0 else nb - 1 - (it * ubk + u)
                        rows = pl.ds(pl.multiple_of(blk * HEAD, HEAD), HEAD)
                        loaded.append((d, blk, f_refs[d][rows, :], v_ref[rows, :]))
                blocks = [_hgrn_block(d, fv, lbs[d], cum2) for d, _, fv, _ in loaded]
                incs = [{j: _bdot(chunk(vv, j), chunk(k * erest, j), _TN) for j in range(cpb)}
                        for (_, _, _, vv), (k, _, _, _, _, erest) in zip(loaded, blocks)]
                states, starts = list(states), []
                for (d, _, _, _), (_, _, _, eb, _, erest), inc in zip(loaded, blocks, incs):
                    starts.append(states[d])
                    states[d] = block_states(d, states[d], inc, eb * erest)[1]
                for (d, blk, _, _), start in zip(loaded, starts):
                    ck_s[d, blk] = start
                return tuple(states)

            lax.fori_loop(0, nb // ubk, pass_states, (zero_state, zero_state))

            def pass_back(it, carry):
                gts, dlb = [carry[0], carry[1]], carry[2]
                loaded = []
                for u, d in ((u, d) for u in range(ubk) for d in range(2)):
                    blk = nb - 1 - (it * ubk + u) if d == 0 else it * ubk + u
                    rows = pl.ds(pl.multiple_of(blk * HEAD, HEAD), HEAD)
                    loaded.append((d, rows, f_refs[d][rows, :], q_ref[rows, :], v_ref[rows, :], do_s[rows, :], ck_s[d, blk]))
                blocks = [_hgrn_block(d, fv, lbs[d], cum2) for d, _, fv, _, _, _, _ in loaded]
                scaled = []
                for (_, _, _, qv, _, _, _), (k, _, _, eb, enb, erest) in zip(loaded, blocks):
                    qh = qv * q_scale
                    scaled.append((qh, qh * eb, k * enb, k * erest, eb * erest))
                masks = [tril if d == 0 else triu for d, *_ in loaded]
                atts = [jnp.where(m, _bdot(qd, kd, _NT), 0.0) for m, (_, qd, kd, _, _) in zip(masks, scaled)]
                datts = [jnp.where(m, _bdot(do, vv, _NT), 0.0) for m, (_, _, _, _, vv, do, _) in zip(masks, loaded)]
                dvs = [_bdot(att, do, _TN) for att, (_, _, _, _, _, do, _) in zip(atts, loaded)]
                dqds = [_bdot(datt, kd) for datt, (_, _, kd, _, _) in zip(datts, scaled)]
                dkds = [_bdot(datt, qd, _TN) for datt, (_, qd, _, _, _) in zip(datts, scaled)]
                s_incs = [{j: _bdot(chunk(vv, j), chunk(ke, j), _TN) for j in range(cpb)}
                          for (_, _, _, _, vv, _, _), (_, _, _, ke, _) in zip(loaded, scaled)]
                g_incs = [{j: _bdot(chunk(do, j), chunk(qd, j), _TN) for j in range(cpb)}
                          for (_, _, _, _, _, do, _), (_, qd, _, _, _) in zip(loaded, scaled)]
                befores, afters, g_at = [], [], []
                for (d, _, _, _, _, _, ck), (_, _, _, _, e_big), s_inc, g_inc in zip(loaded, scaled, s_incs, g_incs):
                    order = chunk_order(d)
                    before, after = block_states(d, ck, s_inc, e_big)
                    befores.append(before)
                    afters.append({j: (before[order[n + 1]] if n + 1 < cpb else after) for n, j in enumerate(order)})
                    at, gt = {}, gts[d]
                    for j in reversed(order):
                        at[j] = gt
                        gt = gt * decay_row(e_big, j) + g_inc[j]
                    gts[d] = gt
                    g_at.append(at)
                dqd_i = [{j: _bdot(chunk(do, j), before[j]) for j in range(cpb)}
                         for (_, _, _, _, _, do, _), before in zip(loaded, befores)]
                dv_i = [{j: _bdot(chunk(ke, j), at[j], _NT) for j in range(cpb)}
                        for (_, _, _, ke, _), at in zip(scaled, g_at)]
                dke = [{j: _bdot(chunk(vv, j), at[j]) for j in range(cpb)}
                       for (_, _, _, _, vv, _, _), at in zip(loaded, g_at)]
                results, new = [], []
                for n, ((d, rows, _, _, _, _, _), (k, sf, big_f, eb, enb, erest), (qh, _, _, _, _)) in enumerate(
                        zip(loaded, blocks, scaled)):
                    dqh = (dqds[n] + cat(dqd_i[n])) * eb
                    dk = dkds[n] * enb + cat(dke[n]) * erest
                    carry_rows = {j: jnp.broadcast_to(_colsum(g_at[n][j] * afters[n][j]), (A_CHUNK, HEAD))
                                  for j in range(cpb)}
                    dlf = _dot_split(_ones(triu if d == 0 else tril), qh * dqh - k * dk) + cat(carry_rows)
                    common = dlf / big_f - dk
                    results.append((d, rows, (k * sf * common).astype(BF16), dqh.astype(BF16),
                                    (dvs[n] + cat(dv_i[n])).astype(BF16)))
                    new.append(_colsum((1.0 - sf) * common))
                for d, rows, df16, dq16, dv16 in results:
                    res_s[1 + d, rows, :] = df16
                    dq_s[d, rows, :] = dq16
                    dv_s[d, rows, :] = dv16
                per_dir = [sum(c for (d, *_), c in zip(loaded, new) if d == dd) for dd in range(2)]
                return gts[0], gts[1], dlb + jnp.concatenate(per_dir, axis=0)

            dlb_ref[...] = lax.fori_loop(0, nb // ubk, pass_back,
                                         (zero_state, zero_state, jnp.zeros((2, HEAD), F32)))[2]

            def pass_out(i, carry):
                rows = pl.ds(pl.multiple_of(i * HEAD, HEAD), HEAD)
                dq = dq_s[0, rows, :].astype(F32) + dq_s[1, rows, :].astype(F32)
                res_s[0, rows, :] = (dq * q_scale).astype(BF16)
                res_s[3, rows, :] = (dv_s[0, rows, :].astype(F32) + dv_s[1, rows, :].astype(F32)).astype(BF16)
                return carry

            lax.fori_loop(0, nb, pass_out, 0)

        out_ref[...] = res_s[p]

    def col(pp):
        return pl.BlockSpec((t, HEAD), lambda h, p: (0, pp * heads + h))

    n_in = dproj.shape[1]
    any_spec = pl.BlockSpec(memory_space=pl.ANY)
    return pl.pallas_call(
        body, name="hgrn_bwd", grid=(heads, 5),
        in_specs=[col(0), col(1), col(2), col(3), col(4),
                  pl.BlockSpec((t, HEAD), lambda h, p: (0, h)), pl.BlockSpec((t, HEAD), lambda h, p: (0, h)),
                  pl.BlockSpec((2, HEAD), lambda h, p: (0, h)), pl.BlockSpec((1, HEAD), lambda h, p: (0, 0)),
                  any_spec, any_spec],
        out_specs=[pl.BlockSpec((t, HEAD), lambda h, p: (0, p * heads + h)),
                   pl.BlockSpec((None, 1, HEAD), lambda h, p: (h, 0, 0)),
                   pl.BlockSpec((2, HEAD), lambda h, p: (0, h))],
        out_shape=[jax.ShapeDtypeStruct((t, n_in), BF16), jax.ShapeDtypeStruct((heads, 1, HEAD), F32),
                   jax.ShapeDtypeStruct((2, width), F32)],
        scratch_shapes=[pltpu.VMEM((t, HEAD), F32), pltpu.VMEM((2, t, HEAD), BF16), pltpu.VMEM((2, t, HEAD), BF16),
                        pltpu.VMEM((5, t, HEAD), BF16), pltpu.VMEM((2, nb, HEAD, HEAD), F32)],
        input_output_aliases={9: 0},
        compiler_params=_params(2),
    )(proj, proj, proj, proj, proj, osum, dout_a, lb, g_norm, dproj, after)


def _adamw(w, g, m, v):
    m = ADAM_B1 * m + (1.0 - ADAM_B1) * g
    v = ADAM_B2 * v + (1.0 - ADAM_B2) * (g * g)
    m_hat = m / (1.0 - ADAM_B1 ** ADAM_STEP)
    v_hat = v / (1.0 - ADAM_B2 ** ADAM_STEP)
    delta = -ADAM_LR * (m_hat / (jnp.sqrt(v_hat) + ADAM_EPS) + ADAM_WD * w)
    return delta, m, v


def _adamw_big(name, me, w, m, v, parts, axis):
    r, c = w.shape
    n_parts = len(parts)
    tr = _tile(r // n_parts, 128)
    per = r // n_parts // tr
    assert axis == 1 or n_parts == 1

    def body(me_ref, w_ref, m_ref, v_ref, *rest):
        g_refs, l_refs = rest[:n_parts], rest[n_parts:2 * n_parts]
        og_ref, od_ref, om_ref, ov_ref = rest[2 * n_parts:]
        g = None
        for p in range(n_parts):
            total = g_refs[p][...]
            for s in range(N_DEV - 1):
                total = total + l_refs[p][s].astype(F32)
            g = total if p == 0 else jnp.where(pl.program_id(0) // per == p, total, g)
        og_ref[...] = g
        od_ref[...], om_ref[...], ov_ref[...] = _adamw(w_ref[...], g, m_ref[...], v_ref[...])

    def within(p, i):
        return jnp.clip(i - p * per, 0, per - 1)

    shard = pl.BlockSpec((tr, c), lambda i, me_ref: (i, 0))
    if axis == 1:
        own = [pl.BlockSpec((tr, c), lambda i, me_ref, p=p: (within(p, i), me_ref[0])) for p in range(n_parts)]
    else:
        own = [pl.BlockSpec((tr, c), lambda i, me_ref: (me_ref[0] * (r // tr) + i, 0))]
    landed = [pl.BlockSpec((N_DEV - 1, tr, c), lambda i, me_ref, p=p: (0, within(p, i), 0)) for p in range(n_parts)]
    grid_spec = pltpu.PrefetchScalarGridSpec(
        num_scalar_prefetch=1, grid=(r // tr,),
        in_specs=[shard, shard, shard] + own + landed, out_specs=[shard] * 4)
    return pl.pallas_call(
        body, name=name, grid_spec=grid_spec, out_shape=[jax.ShapeDtypeStruct((r, c), F32)] * 4,
        compiler_params=_params(1),
    )(me, w, m, v, *[g for g, _ in parts], *[ld for _, ld in parts])


def _adamw_ada(sct, dmod_mine, w, m, v):
    d, n = w.shape
    tr = _tile(d, 256)

    def body(s_ref, dm_ref, w_ref, m_ref, v_ref, og_ref, od_ref, om_ref, ov_ref):
        g = _dot(s_ref[...], dm_ref[...], precision=HIGHEST)
        og_ref[...] = g
        od_ref[...], om_ref[...], ov_ref[...] = _adamw(w_ref[...], g, m_ref[...], v_ref[...])

    blk = pl.BlockSpec((tr, n), lambda i: (i, 0))
    return pl.pallas_call(
        body, name="adamw_ada", grid=(d // tr,),
        in_specs=[pl.BlockSpec((tr, N_DEV), lambda i: (i, 0)), pl.BlockSpec((N_DEV, n), lambda i: (0, 0)), blk, blk, blk],
        out_specs=[blk] * 4, out_shape=[jax.ShapeDtypeStruct((d, n), F32)] * 4, compiler_params=_params(1),
    )(sct, dmod_mine, w, m, v)


def _adamw_small(gathered, w, m, v):
    def body(g_ref, w_ref, m_ref, v_ref, og_ref, od_ref, om_ref, ov_ref):
        g = g_ref[0]
        for s in range(1, N_DEV):
            g = g + g_ref[s]
        og_ref[...] = g
        od_ref[...], om_ref[...], ov_ref[...] = _adamw(w_ref[...], g, m_ref[...], v_ref[...])

    return pl.pallas_call(
        body, name="adamw_small", out_shape=[jax.ShapeDtypeStruct(w.shape, F32)] * 4,
        compiler_params=pltpu.CompilerParams(vmem_limit_bytes=VMEM_LIMIT),
    )(gathered, w, m, v)


def _adamw_lb(dlb_mine, lb_logits, m, v):
    def body(d_ref, l_ref, m_ref, v_ref, og_ref, od_ref, om_ref, ov_ref):
        dlb = d_ref[0]
        for s in range(1, N_DEV):
            dlb = dlb + d_ref[s]
        for dr in range(2):
            lb = _sigmoid(l_ref[dr][0:1, :] - l_ref[dr][1:2, :])
            d0 = dlb[dr:dr + 1] * lb * (1.0 - lb)
            g = jnp.concatenate([d0, -d0], axis=0)
            og_ref[dr] = g
            od_ref[dr], om_ref[dr], ov_ref[dr] = _adamw(l_ref[dr], g, m_ref[dr], v_ref[dr])

    return pl.pallas_call(body, name="adamw_lb", out_shape=[jax.ShapeDtypeStruct(lb_logits.shape, F32)] * 4,
                          )(dlb_mine, lb_logits, m, v)


def _rows(a, pad_to=8):
    flat = a.reshape(-1, LANE)
    pad = (-flat.shape[0]) % pad_to
    return jnp.pad(flat, ((0, pad), (0, 0))) if pad else flat


def kernel(x, c, w_ada, b_ada, g_pre_mix, g_post_mix, g_pre_ffn, g_post_ffn, w_in, lb_logits, g_hgrn_norm, w_a_out, g_sgu_norm, w_spatial, b_spatial, w_b_out, w_o, w_ff1, w_ff2, loss_target, m_w_ada, m_b_ada, m_g_pre_mix, m_g_post_mix, m_g_pre_ffn, m_g_post_ffn, m_w_in, m_lb_logits, m_g_hgrn_norm, m_w_a_out, m_g_sgu_norm, m_w_spatial, m_b_spatial, m_w_b_out, m_w_o, m_w_ff1, m_w_ff2, v_w_ada, v_b_ada, v_g_pre_mix, v_g_post_mix, v_g_pre_ffn, v_g_post_ffn, v_w_in, v_lb_logits, v_g_hgrn_norm, v_w_a_out, v_g_sgu_norm, v_w_spatial, v_b_spatial, v_w_b_out, v_w_o, v_w_ff1, v_w_ff2):
    t, d = x.shape[1], x.shape[2]
    n_in = w_in.shape[2] * N_DEV
    width = (n_in - 2 * d) // 7
    heads = width // HEAD
    assert heads == N_DEV and width % LANE == 0
    d_ff = w_ff1.shape[2] * N_DEV
    n_ada = w_ada.shape[2]
    me = _dev_index()
    me_arr = me.reshape(1).astype(jnp.int32)
    x2, tgt = x[0], loss_target[0]

    big = [w_in[0], w_a_out[0], w_b_out[0], w_o[0], w_ff1[0], w_ff2[0]]
    big_axes = [1, 1, 1, 0, 1, 0]
    big_names = ["w_in", "w_a_out", "w_b_out", "w_o", "w_ff1", "w_ff2"]
    w_in16 = _cast_bf16("cast_w_in", big[0])
    own_parts = [_cast_into_full("cast_" + nm, me_arr, w, ax) for nm, w, ax in zip(big_names[1:], big[1:], big_axes[1:])]

    c_rows = d // LANE
    small = _all_gather_small("gather_c_lb", _prep_small(c[0:1], lb_logits))
    sc_all = small[:, :c_rows, :].reshape(N_DEV, d)
    lb = jnp.transpose(small[:, c_rows:c_rows + 2, :], (1, 0, 2)).reshape(2, width)
    b_shard = lax.dynamic_slice_in_dim(b_ada, me * n_ada, n_ada, axis=1)
    mod_sh = _mod_shard(sc_all, w_ada[0], b_shard)
    mod_all = _all_gather_small("gather_mod", _rows(mod_sh))
    mod_all = mod_all[:, :N_DEV * n_ada // LANE, :].reshape(N_DEV, N_DEV, n_ada)
    mod6 = lax.dynamic_index_in_dim(mod_all, me, axis=1, keepdims=False).reshape(N_MOD, d)
    sh1, sc1, gt1, sh2, sc2, gt2 = [mod6[i:i + 1] for i in range(N_MOD)]

    a1 = _norm_mod(x2, g_pre_mix, sh1, sc1)
    tm = _tile(t, 512)

    def store_bf16(acc, i, j, extra_refs, out_refs, rows):
        out_refs[0][...] = acc.astype(BF16)

    xq, yq, cq = lax.axis_index("x"), lax.axis_index("y"), lax.axis_index("c")
    chips = [(1 - xq, yq), (xq, 1 - yq), (1 - xq, 1 - yq)]
    order = jnp.stack([me, 4 * xq + 2 * yq + 1 - cq]
                      + [4 * a + 2 * b + cq for a, b in chips[:2]] + [4 * a + 2 * b + 1 - cq for a, b in chips[:2]]
                      + [4 * chips[2][0] + 2 * chips[2][1] + cq, 4 * chips[2][0] + 2 * chips[2][1] + 1 - cq]).astype(jnp.int32)
    proj, wf_in = _proj_gather(a1, w_in16, order)

    proj, own_parts = lax.optimization_barrier((proj, own_parts))
    gathers = {}
    for key, lo, hi in (("mid", 1, 4), ("ff1", 4, 5), ("ff2", 5, 6)):
        far, near = _gather_stage_plans(own_parts[lo - 1:hi - 1], big_axes[lo:hi])
        gathers[key] = [far, near, _split_start("gather_%s_start" % key, far, landing=own_parts[lo - 1:hi - 1])]

    def pass_on(key, after):
        far, near, (sems, thru, _) = gathers[key]
        parts = _split_wait("gather_%s_wait" % key, far, sems, thru, after)[1]
        gathers[key].append(_split_start("pass_%s_start" % key, near, landing=list(parts)))
        return gathers[key][3][2]

    def gathered_weights(key, after):
        near, (sems, thru, _) = gathers[key][1], gathers[key][3]
        return _split_wait("pass_%s_wait" % key, near, sems, thru, after)[1]

    out_a, osum = _hgrn_fwd(proj, lb, g_hgrn_norm, width,
                            after=[gathers[key][2][2] for key in ("mid", "ff1", "ff2")])
    passed_mid = pass_on("mid", out_a)
    z_block = 5
    bst = b_spatial[0].T
    out_b = _sgu_fwd(proj, g_sgu_norm, w_spatial[0], bst, width, z_block)
    wf_a, wf_b, wf_o = gathered_weights("mid", out_b)

    tn_d = _tile(d, 512)
    blk_d = ((tm, tn_d), lambda i, j: (i, j))
    y_a, = _mm("y_a", out_a, wf_a, _NN, t, d, width, tm, tn_d, width, _after(passed_mid),
               [(jax.ShapeDtypeStruct((t, d), BF16),) + blk_d], store_bf16)
    ga_blk = (5 * width + 2 * width) // tn_d
    gb_blk = ga_blk + d // tn_d

    def merge(acc, i, j, extra_refs, out_refs, rows):
        ga, gb, ya = extra_refs
        out_refs[0][...] = acc.astype(BF16)
        out_refs[1][...] = (_sigmoid(ga[...]) * ya[...].astype(F32) + _sigmoid(gb[...]) * acc).astype(BF16)

    y_b, merged = _mm("y_b_merge", out_b, wf_b, _NN, t, d, width, tm, tn_d, width,
                      [(proj, (tm, tn_d), lambda i, j: (i, ga_blk + j)), (proj, (tm, tn_d), lambda i, j: (i, gb_blk + j)),
                       (y_a,) + blk_d],
                      [(jax.ShapeDtypeStruct((t, d), BF16),) + blk_d, (jax.ShapeDtypeStruct((t, d), BF16),) + blk_d], merge)

    tr = _tile(t, 512)
    rc = 32 if tr % 32 == 0 else None
    row_d = ((tr, d), lambda i, j: (i, 0))
    vec_d = ((1, d), lambda i, j: (0, 0))

    passed_ff1 = pass_on("ff1", merged)

    def post_mix(acc, i, j, extra_refs, out_refs, rows):
        x_r, gt1_r, g2_r, g3_r, sc2_r, sh2_r = extra_refs[:6]
        h1 = x_r[rows, :] + gt1_r[...] * (acc * _rms(acc) * g2_r[...])
        out_refs[0][rows, :] = acc.astype(BF16)
        out_refs[1][rows, :] = h1
        out_refs[2][rows, :] = ((h1 * _rms(h1) * g3_r[...]) * (1.0 + sc2_r[...]) + sh2_r[...]).astype(BF16)

    mo, h1, a2 = _mm("w_o_post_mix", merged, wf_o, _NN, t, d, d, tr, d, d,
                     [(x2,) + row_d, (gt1,) + vec_d, (g_post_mix,) + vec_d, (g_pre_ffn,) + vec_d, (sc2,) + vec_d, (sh2,) + vec_d]
                     + _after(passed_ff1),
                     [(jax.ShapeDtypeStruct((t, d), BF16),) + row_d, (jax.ShapeDtypeStruct((t, d), F32),) + row_d,
                      (jax.ShapeDtypeStruct((t, d), BF16),) + row_d], post_mix, row_chunk=rc)

    tn_f = _tile(d_ff, 2048)
    blk_f = ((tm, tn_f), lambda i, j: (i, j))

    def relu_sq(acc, i, j, extra_refs, out_refs, rows):
        r = jnp.maximum(acc, 0.0)
        out_refs[0][...] = acc.astype(BF16)
        out_refs[1][...] = (r * r).astype(BF16)

    wf_1, = gathered_weights("ff1", a2)
    hff, act = _mm(
        "ff1", a2, wf_1, _NN, t, d_ff, d, tm, tn_f, d, [],
        [(jax.ShapeDtypeStruct((t, d_ff), BF16),) + blk_f, (jax.ShapeDtypeStruct((t, d_ff), BF16),) + blk_f], relu_sq)
    pass_on("ff2", hff)
    wf_2, = gathered_weights("ff2", act)

    sums_d = ((8, d), lambda i, j: (0, 0))

    def zero_first(sums_r, i, rows):
        if rows.start in (None, 0):
            @pl.when(i == 0)
            def _():
                sums_r[...] = jnp.zeros_like(sums_r)

    def loss_head(acc, i, j, extra_refs, out_refs, rows):
        h1_r, tgt_r, gt2_r, g4_r = extra_refs
        dy_r, dff_r, sums_r = out_refs
        r4 = _rms(acc)
        ffn = acc * r4
        n4 = ffn * g4_r[...]
        err = h1_r[rows, :] + gt2_r[...] * n4 - tgt_r[rows, :]
        dy = err * (1.0 / d)
        dy_r[rows, :] = dy.astype(BF16)
        dn4 = dy * gt2_r[...]
        dffn = dn4 * g4_r[...]
        dff_r[rows, :] = (r4 * (dffn - ffn * jnp.mean(dffn * ffn, axis=-1, keepdims=True))).astype(BF16)
        zero_first(sums_r, i, rows)

        sums_r[0:1, :] += _colsum(err * err)
        sums_r[1:2, :] += _colsum(dy * n4)
        sums_r[2:3, :] += _colsum(dn4 * ffn)

    tk_f = _tile(d_ff, 1024)
    dy, dff, sums_f = _mm("ff2_loss", act, wf_2, _NN, t, d, d_ff, tr, d, tk_f,
                          [(h1,) + row_d, (tgt,) + row_d, (gt2,) + vec_d, (g_post_ffn,) + vec_d],
                          [(jax.ShapeDtypeStruct((t, d), BF16),) + row_d, (jax.ShapeDtypeStruct((t, d), BF16),) + row_d,
                           (jax.ShapeDtypeStruct((8, d), F32),) + sums_d], loss_head, row_chunk=rc)
    loss_mine = (0.5 / d) * jnp.sum(sums_f[0])

    def relu_sq_bwd(acc, i, j, extra_refs, out_refs, rows):
        out_refs[0][...] = (acc * (2.0 * jnp.maximum(extra_refs[0][...].astype(F32), 0.0))).astype(BF16)

    dhff, = _mm("d_hff", dff, wf_2, _NT, t, d_ff, d, tm, tn_f, d, [(hff,) + blk_f],
                [(jax.ShapeDtypeStruct((t, d_ff), BF16),) + blk_f], relu_sq_bwd)
    scatters = {}

    def send_grads(key, grads16, axes):
        plan = _scatter_plan(grads16, axes)
        scatters[key] = (plan,) + _split_start("scatter_%s_start" % key, plan)
        return scatters[key][3]

    def received_grads(key, after):
        plan, sems, thru, _ = scatters[key]
        return _split_wait("scatter_%s_wait" % key, plan, sems, thru, after)[1]

    gw_ff2, gw_ff2_16 = _grad_w("grad_w_ff2", act, dff)
    sent_ff2 = send_grads("ff2", [gw_ff2_16], big_axes[5:6])
    gw_ff1, gw_ff1_16 = _grad_w("grad_w_ff1", a2, dhff, token=sent_ff2)
    sent_ff1 = send_grads("ff1", [gw_ff1_16], big_axes[4:5])

    def pre_ffn_bwd(acc, i, j, extra_refs, out_refs, rows):
        h1_r, dy_r, mo_r, sc2_r, g3_r, gt1_r, g2_r = extra_refs[:7]
        dh1_r, dmo_r, sums_r = out_refs
        h1v = h1_r[rows, :]
        r3 = _rms(h1v)
        h1n = h1v * r3
        dn3 = acc * (1.0 + sc2_r[...])
        dh1n = dn3 * g3_r[...]
        dh1 = dy_r[rows, :].astype(F32) + r3 * (dh1n - h1n * jnp.mean(dh1n * h1n, axis=-1, keepdims=True))
        dh1_r[rows, :] = dh1.astype(BF16)
        mov = mo_r[rows, :].astype(F32)
        r2 = _rms(mov)
        mon = mov * r2
        dn2 = dh1 * gt1_r[...]
        dmon = dn2 * g2_r[...]
        dmo_r[rows, :] = (r2 * (dmon - mon * jnp.mean(dmon * mon, axis=-1, keepdims=True))).astype(BF16)
        zero_first(sums_r, i, rows)

        sums_r[0:1, :] += _colsum(acc)
        sums_r[1:2, :] += _colsum(acc * (h1n * g3_r[...]))
        sums_r[2:3, :] += _colsum(dn3 * h1n)
        sums_r[3:4, :] += _colsum(dh1 * (mon * g2_r[...]))
        sums_r[4:5, :] += _colsum(dn2 * mon)

    dh1, dmo, sums_m = _mm("d_a2_pre_ffn", dhff, wf_1, _NT, t, d, d_ff, tr, d, tk_f,
                           [(h1,) + row_d, (dy,) + row_d, (mo,) + row_d, (sc2,) + vec_d, (g_pre_ffn,) + vec_d,
                            (gt1,) + vec_d, (g_post_mix,) + vec_d] + _after(sent_ff1),
                           [(jax.ShapeDtypeStruct((t, d), BF16),) + row_d, (jax.ShapeDtypeStruct((t, d), BF16),) + row_d,
                            (jax.ShapeDtypeStruct((8, d), F32),) + sums_d], pre_ffn_bwd, row_chunk=rc)
    gw_o, gw_o_16 = _grad_w("grad_w_o", merged, dmo)

    n_j = d // tn_d

    def merge_bwd_body(dmo_ref, wo_ref, ga_ref, gb_ref, ya_ref, yb_ref, dya_ref, dyb_ref, dproj_ref, acc_s):
        g = pl.program_id(2)

        @pl.when(g == 0)
        def _():
            dm = _dot(dmo_ref[...], wo_ref[...], _NT)
            acc_s[...] = dm
            sa = _sigmoid(ga_ref[...])
            dya_ref[...] = (dm * sa).astype(BF16)
            dproj_ref[...] = (dm * ya_ref[...].astype(F32) * sa * (1.0 - sa)).astype(BF16)

        @pl.when(g == 1)
        def _():
            dm = acc_s[...]
            sb = _sigmoid(gb_ref[...])
            dyb_ref[...] = (dm * sb).astype(BF16)
            dproj_ref[...] = (dm * yb_ref[...].astype(F32) * sb * (1.0 - sb)).astype(BF16)

    tile3 = pl.BlockSpec((tm, tn_d), lambda i, j, g: (i, j))
    dy_a, dy_b, dproj = pl.pallas_call(
        merge_bwd_body, name="d_merged", grid=(t // tm, n_j, 2),
        in_specs=[pl.BlockSpec((tm, d), lambda i, j, g: (i, 0)), pl.BlockSpec((tn_d, d), lambda i, j, g: (j, 0)),
                  pl.BlockSpec((tm, tn_d), lambda i, j, g: (i, ga_blk + j)),
                  pl.BlockSpec((tm, tn_d), lambda i, j, g: (i, gb_blk + j)), tile3, tile3],
        out_specs=[tile3, tile3, pl.BlockSpec((tm, tn_d), lambda i, j, g: (i, ga_blk + g * n_j + j))],
        out_shape=[jax.ShapeDtypeStruct((t, d), BF16), jax.ShapeDtypeStruct((t, d), BF16),
                   jax.ShapeDtypeStruct((t, n_in), BF16)],
        scratch_shapes=[pltpu.VMEM((tm, tn_d), F32)], compiler_params=_params(3),
    )(dmo, wf_o, proj, proj, y_a, y_b)

    tn_w = _tile(width, 512)
    blk_w = ((tm, tn_w), lambda i, j: (i, j))
    dout_a, = _mm("d_out_a", dy_a, wf_a, _NT, t, width, d, tm, tn_w, d, [],
                  [(jax.ShapeDtypeStruct((t, width), BF16),) + blk_w], store_bf16)
    dout_b, = _mm("d_out_b", dy_b, wf_b, _NT, t, width, d, tm, tn_w, d, [],
                  [(jax.ShapeDtypeStruct((t, width), BF16),) + blk_w], store_bf16)
    gw_a, gw_a_16 = _grad_w("grad_w_a_out", out_a, dy_a)
    gw_b, gw_b_16 = _grad_w("grad_w_b_out", out_b, dy_b)

    w_st = jnp.swapaxes(w_spatial[0], 1, 2)
    dproj, dg_sgu, dw_sp, dbst = _sgu_bwd(proj, dout_b, dproj, g_sgu_norm, w_spatial[0], w_st, bst, width, z_block)
    sent_mid = send_grads("mid", [gw_a_16, gw_b_16, gw_o_16], big_axes[1:4])
    dproj, dgh_heads, dlb = _hgrn_bwd(proj, osum, dout_a, dproj, lb, g_hgrn_norm, width, after=sent_mid)
    gw_in_top, gw_in_top16 = _grad_w("grad_w_in_top", a1, dproj, rows=(0, d // 2))
    sent_top = send_grads("in_top", [gw_in_top16], big_axes[:1])
    gw_in_bot, gw_in_bot16 = _grad_w("grad_w_in_bot", a1, dproj, token=sent_top, rows=(d // 2, d // 2))
    sent_in = send_grads("in_bot", [gw_in_bot16], big_axes[:1])

    def pre_mix_bwd(acc, i, j, extra_refs, out_refs, rows):
        x_r, dh1_r, sc1_r, g1_r = extra_refs[:4]
        dx_r, sums_r = out_refs
        xv = x_r[rows, :]
        r1 = _rms(xv)
        xn = xv * r1
        dn1 = acc * (1.0 + sc1_r[...])
        dxn = dn1 * g1_r[...]
        dx_r[rows, :] = dh1_r[rows, :].astype(F32) + r1 * (dxn - xn * jnp.mean(dxn * xn, axis=-1, keepdims=True))
        zero_first(sums_r, i, rows)

        sums_r[0:1, :] += _colsum(acc)
        sums_r[1:2, :] += _colsum(acc * (xn * g1_r[...]))
        sums_r[2:3, :] += _colsum(dn1 * xn)

    tk_in = _tile(n_in, 1024)
    grad_x, sums_x = _mm(
        "d_a1_pre_mix", dproj, wf_in, _NT, t, d, n_in, tr, d, tk_in,
        [(x2,) + row_d, (dh1,) + row_d, (sc1,) + vec_d, (g_pre_mix,) + vec_d] + _after(sent_in),
        [(jax.ShapeDtypeStruct((t, d), F32),) + row_d, (jax.ShapeDtypeStruct((8, d), F32),) + sums_d],
        pre_mix_bwd, row_chunk=rc)

    dmod = jnp.concatenate([sums_x[0:2], sums_m[3:4], sums_m[0:2], sums_f[1:2]], axis=0).reshape(N_DEV, n_ada // LANE, LANE)
    ada_rows = -(-(n_ada // LANE) // 8) * 8
    dmod = jnp.pad(dmod, ((0, 0), (0, ada_rows - n_ada // LANE), (0, 0))).reshape(N_DEV * ada_rows, LANE)
    parts = [dmod, _rows(sums_x[2:3]), _rows(sums_m[4:5]), _rows(sums_m[2:3]), _rows(sums_f[2:3]),
             _rows(jnp.sum(dgh_heads, axis=0)), _rows(dg_sgu), _rows(dw_sp), _rows(dbst.T)]
    n_params = sum(p.shape[0] for p in parts)
    parts.append(jnp.full((8, LANE), loss_mine, F32))
    n_common = n_params + 8
    payload = jnp.concatenate(parts + [_rows(dlb)], axis=0)

    moms = [m_w_in, m_w_a_out, m_w_b_out, m_w_o, m_w_ff1, m_w_ff2]
    vars_ = [v_w_in, v_w_a_out, v_w_b_out, v_w_o, v_w_ff1, v_w_ff2]
    big_out = {}

    def big_update(nm, parts):
        k = big_names.index(nm)
        outs = _adamw_big("adamw_" + nm, me_arr, big[k], moms[k][0], vars_[k][0], parts, big_axes[k])
        big_out[nm] = [o[None] for o in outs]
        return outs[0]

    land_ff2, = received_grads("ff2", grad_x)
    done = big_update("w_ff2", [(gw_ff2, land_ff2)])
    land_ff1, = received_grads("ff1", done)
    done = big_update("w_ff1", [(gw_ff1, land_ff1)])
    land_a, land_b, land_o = received_grads("mid", done)
    big_update("w_a_out", [(gw_a, land_a)])
    big_update("w_b_out", [(gw_b, land_b)])
    done = big_update("w_o", [(gw_o, land_o)])

    payload, _ = lax.optimization_barrier((payload, done))
    gathered = _all_gather_small("gather_small_grads", payload)

    dmod_mine = lax.dynamic_slice_in_dim(gathered[:, :N_DEV * ada_rows, :].reshape(N_DEV, N_DEV, ada_rows * LANE),
                                         me, 1, axis=1)[:, 0, :n_ada]
    ada_out = [o[None] for o in _adamw_ada(sc_all.T, dmod_mine, w_ada[0], m_w_ada[0], v_w_ada[0])]

    def pack(b_, g1_, g2_, g3_, g4_, gh_, gs_, ws_, bs_):
        b3 = b_.reshape(N_DEV, n_ada // LANE, LANE)
        b3 = jnp.pad(b3, ((0, 0), (0, ada_rows - n_ada // LANE), (0, 0))).reshape(N_DEV * ada_rows, LANE)
        return jnp.concatenate([b3, _rows(g1_), _rows(g2_), _rows(g3_), _rows(g4_), _rows(gh_), _rows(gs_),
                                _rows(ws_), _rows(bs_), jnp.zeros((8, LANE), F32)], axis=0)

    small_w = (b_ada, g_pre_mix, g_post_mix, g_pre_ffn, g_post_ffn, g_hgrn_norm, g_sgu_norm, w_spatial, b_spatial)
    small_m = (m_b_ada, m_g_pre_mix, m_g_post_mix, m_g_pre_ffn, m_g_post_ffn, m_g_hgrn_norm, m_g_sgu_norm, m_w_spatial, m_b_spatial)
    small_v = (v_b_ada, v_g_pre_mix, v_g_post_mix, v_g_pre_ffn, v_g_post_ffn, v_g_hgrn_norm, v_g_sgu_norm, v_w_spatial, v_b_spatial)
    packed = _adamw_small(gathered[:, :n_common, :], pack(*small_w), pack(*small_m), pack(*small_v))

    def unpack(slab):
        outs, at = [], 0
        b3 = slab[:N_DEV * ada_rows].reshape(N_DEV, ada_rows, LANE)[:, :n_ada // LANE, :]
        outs.append(b3.reshape(b_ada.shape))
        at = N_DEV * ada_rows
        for ref in small_w[1:]:
            n_el = ref.size
            n_r = -(-(n_el // LANE) // 8) * 8
            outs.append(slab[at:at + n_el // LANE].reshape(ref.shape))
            at += n_r
        return outs

    small_out = [unpack(s) for s in packed]
    loss = packed[0][n_params, 0]

    dlb_all = gathered[:, n_common:n_common + 2 * heads, :].reshape(N_DEV, 2, heads, LANE)
    dlb_mine = lax.dynamic_index_in_dim(dlb_all, me, axis=2, keepdims=False)
    lb_out = _adamw_lb(dlb_mine, lb_logits, m_lb_logits, v_lb_logits)

    land_top, = received_grads("in_top", ada_out[0])
    land_bot, = received_grads("in_bot", land_top)
    big_update("w_in", [(gw_in_top, land_top), (gw_in_bot, land_bot)])

    order = ["w_ada", "b_ada", "g_pre_mix", "g_post_mix", "g_pre_ffn", "g_post_ffn", "w_in", "lb_logits", "g_hgrn_norm",
             "w_a_out", "g_sgu_norm", "w_spatial", "b_spatial", "w_b_out", "w_o", "w_ff1", "w_ff2"]
    small_names = ["b_ada", "g_pre_mix", "g_post_mix", "g_pre_ffn", "g_post_ffn", "g_hgrn_norm", "g_sgu_norm", "w_spatial", "b_spatial"]

    def leaf(kind, nm):
        if nm == "w_ada":
            return ada_out[kind]
        if nm == "lb_logits":
            return lb_out[kind]
        if nm in big_out:
            return big_out[nm][kind]
        return small_out[kind][small_names.index(nm)]

    result = [loss, grad_x[None]]
    for kind in range(4):
        result += [leaf(kind, nm) for nm in order]
    return tuple(result)
```

```python
import math

import jax
import jax.numpy as jnp
from jax import lax
from jax.experimental import pallas as pl
from jax.experimental.pallas import tpu as pltpu

F32 = jnp.float32
BF16 = jnp.bfloat16
MESH = pl.DeviceIdType.MESH
HIGHEST = lax.Precision.HIGHEST

N_DEV = 8
HEAD = 128
A_CHUNK = 32
N_MOD = 6
EPS = 1e-6
LANE = 128
VMEM_LIMIT = 60 * 1024 * 1024

ADAM_LR = 0.001
ADAM_B1 = 0.9
ADAM_B2 = 0.999
ADAM_EPS = 1e-08
ADAM_WD = 0.01
ADAM_STEP = 10

_NN = (((1,), (0,)), ((), ()))
_NT = (((1,), (1,)), ((), ()))
_TN = (((0,), (0,)), ((), ()))


def _dot(a, b, dims=_NN, precision=None):
    return lax.dot_general(a, b, dims, preferred_element_type=F32, precision=precision)


def _bdot(a, b, dims=_NN):
    return _dot(a.astype(BF16), b.astype(BF16), dims)


def _params(n_grid):
    return pltpu.CompilerParams(dimension_semantics=("arbitrary",) * n_grid, vmem_limit_bytes=VMEM_LIMIT)


def _dev_index():
    return lax.axis_index("x") * 4 + lax.axis_index("y") * 2 + lax.axis_index("c")


def _dev_coords(i):
    return (i // 4, (i // 2) % 2, i % 2)


def _sigmoid(x):
    return 1.0 / (1.0 + jnp.exp(-x))


def _erf(x):
    ax = jnp.abs(x)
    t = 1.0 / (1.0 + 0.3275911 * ax)
    poly = ((((1.061405429 * t - 1.453152027) * t + 1.421413741) * t - 0.284496736) * t + 0.254829592) * t
    y = 1.0 - poly * jnp.exp(-ax * ax)
    return jnp.where(x < 0, -y, y)


def _gelu_and_grad(x):
    cdf = 0.5 * (1.0 + _erf(x * (2.0 ** -0.5)))
    pdf = jnp.exp(-0.5 * x * x) * (1.0 / math.sqrt(2.0 * math.pi))
    return x * cdf, cdf + x * pdf


def _rms(x):
    return lax.rsqrt(jnp.mean(x * x, axis=-1, keepdims=True) + EPS)


def _colsum(x):
    return jnp.sum(x, axis=0, keepdims=True)


def _tile(n, want):
    if n <= want:
        return n
    t = (want // LANE) * LANE
    while n % t:
        t -= LANE
    assert t > 0, (n, want)
    return t


def _all_gather_small(name, payload):
    rows = payload.shape[0]

    def body(p_ref, out_ref, send_sems, recv_sems, local_sem):
        me = _dev_index()
        mine = pltpu.make_async_copy(p_ref, out_ref.at[me], local_sem)
        mine.start()
        sends = []
        for r in range(1, N_DEV):
            peer = (me + r) % N_DEV
            cp = pltpu.make_async_remote_copy(
                src_ref=p_ref, dst_ref=out_ref.at[me], send_sem=send_sems.at[r - 1], recv_sem=recv_sems.at[r - 1],
                device_id=_dev_coords(peer), device_id_type=MESH)
            cp.start()
            sends.append(cp)
        for r in range(1, N_DEV):
            src = (me + N_DEV - r) % N_DEV
            pltpu.make_async_remote_copy(
                src_ref=p_ref, dst_ref=out_ref.at[src], send_sem=send_sems.at[r - 1], recv_sem=recv_sems.at[r - 1],
                device_id=_dev_coords(src), device_id_type=MESH).wait_recv()
        for cp in sends:
            cp.wait_send()
        mine.wait()

    return pl.pallas_call(
        body, name=name,
        out_shape=jax.ShapeDtypeStruct((N_DEV, rows, LANE), F32),
        in_specs=[pl.BlockSpec(memory_space=pltpu.VMEM)],
        out_specs=pl.BlockSpec(memory_space=pltpu.VMEM),
        scratch_shapes=[pltpu.SemaphoreType.DMA((N_DEV - 1,)), pltpu.SemaphoreType.DMA((N_DEV - 1,)),
                        pltpu.SemaphoreType.DMA],
        compiler_params=pltpu.CompilerParams(vmem_limit_bytes=VMEM_LIMIT),
    )(payload)


def _region(ref, dev, axis, n):
    start = pl.multiple_of(dev * n, LANE if axis == 1 else 16)
    return ref.at[:, pl.ds(start, n)] if axis == 1 else ref.at[pl.ds(start, n), :]


class _Exchange:
    def __init__(self, arrays, out_shapes, sems, start, finish):
        self.arrays, self.out_shapes, self.sems, self.start, self.finish = arrays, out_shapes, sems, start, finish


def _scatter_plan(grads, axes):
    n_w = len(grads)
    lands = []
    for g, ax in zip(grads, axes):
        shp = (g.shape[0], g.shape[1] // N_DEV) if ax == 1 else (g.shape[0] // N_DEV, g.shape[1])
        lands.append(jax.ShapeDtypeStruct((N_DEV - 1,) + shp, BF16))
    widths = [ld.shape[1 + ax] for ld, ax in zip(lands, axes)]

    def copy(w, r, g_refs, l_refs, sems, block, to):
        return pltpu.make_async_remote_copy(
            src_ref=_region(g_refs[w], block, axes[w], widths[w]), dst_ref=l_refs[w].at[r - 1],
            send_sem=sems[0].at[w * (N_DEV - 1) + r - 1], recv_sem=sems[1].at[w * (N_DEV - 1) + r - 1],
            device_id=_dev_coords(to), device_id_type=MESH)

    def start(g_refs, l_refs, sems):
        me = _dev_index()
        for w in range(n_w):
            for r in range(1, N_DEV):
                owner = (me + r) % N_DEV
                copy(w, r, g_refs, l_refs, sems, owner, owner).start()

    def finish(g_refs, l_refs, sems):
        me = _dev_index()
        for w in range(n_w):
            for r in range(1, N_DEV):
                copy(w, r, g_refs, l_refs, sems, me, (me + N_DEV - r) % N_DEV).wait_recv()
        for w in range(n_w):
            for r in range(1, N_DEV):
                copy(w, r, g_refs, l_refs, sems, me, (me + r) % N_DEV).wait_send()

    sems = [pltpu.SemaphoreType.DMA((n_w * (N_DEV - 1),)), pltpu.SemaphoreType.DMA((n_w * (N_DEV - 1),))]
    return _Exchange(list(grads), lands, sems, start, finish)


def _places():
    x, y, c = lax.axis_index("x"), lax.axis_index("y"), lax.axis_index("c")
    return (x, y, c), (x, y, 1 - c), [(1 - x, y), (x, 1 - y), (1 - x, 1 - y)]


def _place_index(p):
    return p[0] * 4 + p[1] * 2 + p[2]


def _gather_stage_plans(fulls, axes):
    n_w = len(fulls)
    widths = [f.shape[ax] // N_DEV for f, ax in zip(fulls, axes)]
    shapes = [jax.ShapeDtypeStruct(f.shape, f.dtype) for f in fulls]

    def copy(per, w, k, f_refs, sems, block, to):
        part = _region(f_refs[w], _place_index(block), axes[w], widths[w])
        return pltpu.make_async_remote_copy(
            src_ref=part, dst_ref=part, send_sem=sems[0].at[w * per + k], recv_sem=sems[1].at[w * per + k],
            device_id=to, device_id_type=MESH)

    def start1(_, f_refs, sems):
        me, sib, chips = _places()
        for w in range(n_w):
            copy(4, w, 0, f_refs, sems, me, sib).start()
            for j, chip in enumerate(chips):
                copy(4, w, 1 + j, f_refs, sems, me, (*chip, me[2])).start()

    def finish1(_, f_refs, sems):
        me, sib, chips = _places()
        for w in range(n_w):
            copy(4, w, 0, f_refs, sems, sib, me).wait_recv()
            for j, chip in enumerate(chips):
                copy(4, w, 1 + j, f_refs, sems, (*chip, me[2]), me).wait_recv()
        for w in range(n_w):
            for k in range(4):
                copy(4, w, k, f_refs, sems, me, sib).wait_send()

    def start2(_, f_refs, sems):
        me, sib, chips = _places()
        for w in range(n_w):
            for j, chip in enumerate(chips):
                copy(3, w, j, f_refs, sems, (*chip, me[2]), sib).start()

    def finish2(_, f_refs, sems):
        me, sib, chips = _places()
        for w in range(n_w):
            for j, chip in enumerate(chips):
                copy(3, w, j, f_refs, sems, (*chip, sib[2]), me).wait_recv()
        for w in range(n_w):
            for j, chip in enumerate(chips):
                copy(3, w, j, f_refs, sems, (*chip, me[2]), sib).wait_send()

    sems1 = [pltpu.SemaphoreType.DMA((n_w * 4,)), pltpu.SemaphoreType.DMA((n_w * 4,))]
    sems2 = [pltpu.SemaphoreType.DMA((n_w * 3,)), pltpu.SemaphoreType.DMA((n_w * 3,))]
    return _Exchange([], shapes, sems1, start1, finish1), _Exchange([], shapes, sems2, start2, finish2)


_HBM = pl.BlockSpec(memory_space=pltpu.HBM)
_SEM = pl.BlockSpec(memory_space=pltpu.SEMAPHORE)
_EFFECT = pltpu.SideEffectType.DATAFLOW_SIDE_EFFECTING


def _split_start(name, plan, landing=None):
    n_in, n_out, n_sem = len(plan.arrays), len(plan.out_shapes), len(plan.sems)

    def body(*refs):
        ins, lands = refs[:n_in], refs[n_in:n_in + n_out]
        sems = refs[n_in + n_out:n_in + n_out + n_sem]
        token = refs[-1]
        plan.start(ins, lands, sems)
        token[...] = jnp.zeros_like(token)

    hbm = lambda a: pltpu.HBM(a.shape, a.dtype)
    results = pl.pallas_call(
        body, name=name,
        out_shape=tuple(plan.sems) + tuple(hbm(a) for a in plan.arrays) + tuple(hbm(a) for a in plan.out_shapes)
        + (jax.ShapeDtypeStruct((8, LANE), F32),),
        in_specs=(_HBM,) * (n_in + n_out),
        out_specs=(_SEM,) * n_sem + (_HBM,) * (n_in + n_out) + (pl.BlockSpec(memory_space=pltpu.VMEM),),
        input_output_aliases={i: n_sem + i for i in range(n_in + n_out)},
        compiler_params=pltpu.CompilerParams(has_side_effects=_EFFECT),
    )(*[pltpu.with_memory_space_constraint(a, pltpu.HBM) for a in plan.arrays],
      *[pltpu.with_memory_space_constraint(a, pltpu.HBM)
        for a in (landing if landing is not None else [lax.empty(a.shape, a.dtype) for a in plan.out_shapes])])
    return results[:n_sem], results[n_sem:n_sem + n_in + n_out], results[-1]


def _split_wait(name, plan, sems, thru, after):
    n_in, n_out, n_sem = len(plan.arrays), len(plan.out_shapes), len(plan.sems)

    def body(*refs):
        ins, lands = refs[:n_in], refs[n_in:n_in + n_out]
        sem_refs = refs[n_in + n_out:n_in + n_out + n_sem]
        plan.finish(ins, lands, sem_refs)

    hbm = lambda a: pltpu.HBM(a.shape, a.dtype)
    results = pl.pallas_call(
        body, name=name,
        out_shape=tuple(hbm(a) for a in plan.arrays) + tuple(hbm(a) for a in plan.out_shapes),
        in_specs=(_HBM,) * (n_in + n_out) + (_SEM,) * n_sem + (pl.BlockSpec(memory_space=pl.ANY),),
        out_specs=(_HBM,) * (n_in + n_out),
        input_output_aliases={i: i for i in range(n_in + n_out)},
        compiler_params=pltpu.CompilerParams(has_side_effects=_EFFECT),
    )(*thru, *sems, after)
    return results[:n_in], results[n_in:]


def _cast_into_full(name, me, w, axis):
    r, c = w.shape
    tr = _tile(r, 256)
    if axis == 1:
        shape, place = (r, c * N_DEV), pl.BlockSpec((tr, c), lambda i, me_ref: (i, me_ref[0]))
    else:
        shape, place = (r * N_DEV, c), pl.BlockSpec((tr, c), lambda i, me_ref: (me_ref[0] * (r // tr) + i, 0))

    def body(me_ref, w_ref, o_ref):
        o_ref[...] = w_ref[...].astype(BF16)

    grid_spec = pltpu.PrefetchScalarGridSpec(
        num_scalar_prefetch=1, grid=(r // tr,),
        in_specs=[pl.BlockSpec((tr, c), lambda i, me_ref: (i, 0))], out_specs=place)
    return pl.pallas_call(body, name=name, grid_spec=grid_spec, out_shape=jax.ShapeDtypeStruct(shape, BF16),
                          compiler_params=_params(1))(me, w)


def _mm(name, a, b, dims, m, n, k, tm, tn, tk, extras, outs, epilogue, row_chunk=None, a_col_block=0):
    ni, nj, nk = m // tm, n // tn, k // tk
    ne, no = len(extras), len(outs)
    if dims == _TN:
        a_spec = pl.BlockSpec((tk, tm), lambda i, j, kk: (kk, i + a_col_block))
    else:
        a_spec = pl.BlockSpec((tm, tk), lambda i, j, kk: (i, kk))
    if dims == _NT:
        b_spec = pl.BlockSpec((tn, tk), lambda i, j, kk: (j, kk))
    else:
        b_spec = pl.BlockSpec((tk, tn), lambda i, j, kk: (kk, j))
    chunks = [slice(None)] if row_chunk is None else [slice(r, r + row_chunk) for r in range(0, tm, row_chunk)]

    def lift(index_map):
        return lambda i, j, kk: index_map(i, j)

    def body(a_ref, b_ref, *rest):
        extra_refs, out_refs, rest = rest[:ne], rest[ne:ne + no], rest[ne + no:]
        i, j, kk = pl.program_id(0), pl.program_id(1), pl.program_id(2)
        if nk == 1:
            part = _dot(a_ref[...], b_ref[...], dims)
            for rows in chunks:
                epilogue(part[rows], i, j, extra_refs, out_refs, rows)
        else:
            acc_ref = rest[0]

            @pl.when(kk == 0)
            def _():
                acc_ref[...] = _dot(a_ref[...], b_ref[...], dims)

            @pl.when(kk > 0)
            def _():
                acc_ref[...] += _dot(a_ref[...], b_ref[...], dims)

            @pl.when(kk == nk - 1)
            def _():
                for rows in chunks:
                    epilogue(acc_ref[rows, :], i, j, extra_refs, out_refs, rows)

    once = dict(pipeline_mode=pl.Buffered(1)) if (row_chunk is not None and nk > 1) else {}
    return pl.pallas_call(
        body, name=name,
        grid=(ni, nj, nk),
        in_specs=[a_spec, b_spec] + [pl.BlockSpec(bs, lift(im), **once) for _, bs, im in extras],
        out_specs=[pl.BlockSpec(bs, lift(im), **once) for _, bs, im in outs],
        out_shape=[sd for sd, _, _ in outs],
        scratch_shapes=[pltpu.VMEM((tm, tn), F32)] if nk > 1 else [],
        compiler_params=_params(3),
    )(a, b, *[arr for arr, _, _ in extras])


def _after(token):
    return [(token, (8, LANE), lambda i, j: (0, 0))]


def _grad_w(name, a, dc, token=None, tm=512, tn=1024, rows=None):
    t = a.shape[0]
    n = dc.shape[1]
    first, m = rows if rows is not None else (0, a.shape[1])
    tm, tn = _tile(m, tm), _tile(n, tn)
    assert first % tm == 0

    def epilogue(acc, i, j, extra_refs, out_refs, rows):
        out_refs[0][...] = acc
        out_refs[1][...] = acc.astype(BF16)

    blk = ((tm, tn), lambda i, j: (i, j))
    return _mm(name, a, dc, _TN, m, n, t, tm, tn, t, _after(token) if token is not None else [],
               [(jax.ShapeDtypeStruct((m, n), F32),) + blk, (jax.ShapeDtypeStruct((m, n), BF16),) + blk], epilogue,
               a_col_block=first // tm)


def _proj_gather(a1, w_shard, order):
    t, d = a1.shape
    nsh = w_shard.shape[1]
    tm = _tile(t, 512)
    n_i = t // tm

    def body(ord_ref, a_ref, wsh_ref, proj_ref, full_ref, bbuf, bsem, send_sems, recv_sems, own_sem):
        s, i = pl.program_id(0), pl.program_id(1)
        me, sib, chips = _places()
        near, far = chips[:2], chips[2]
        steps = ([(me, None, None), (sib, 0, None)]
                 + [((*ch, me[2]), 1 + j, 4 + j) for j, ch in enumerate(near)]
                 + [((*ch, sib[2]), 4 + j, None) for j, ch in enumerate(near)]
                 + [((*far, me[2]), 3, 6), ((*far, sib[2]), 6, None)])
        blocks = [st[0] for st in steps]

        def part(block):
            return _region(full_ref, _place_index(block), 1, nsh)

        def remote(k, block, to, from_shard=False):
            return pltpu.make_async_remote_copy(
                src_ref=wsh_ref if from_shard else part(block), dst_ref=part(block),
                send_sem=send_sems.at[k], recv_sem=recv_sems.at[k], device_id=to, device_id_type=MESH)

        def load(pos):
            src = wsh_ref if pos == 0 else part(blocks[pos])
            return pltpu.make_async_copy(src, bbuf.at[pos % 2], bsem.at[pos % 2])

        own = pltpu.make_async_copy(wsh_ref, part(me), own_sem)

        @pl.when((s == 0) & (i == 0))
        def _():
            own.start()
            remote(0, me, sib, True).start()
            for j, ch in enumerate(chips):
                remote(1 + j, me, (*ch, me[2]), True).start()
            load(0).start()
            load(0).wait()

        for pos in range(1, N_DEV):
            @pl.when((s == pos) & (i == 0))
            def _():
                load(pos).wait()

        for pos in range(N_DEV - 1):
            @pl.when((s == pos) & (i == n_i - 1))
            def _():
                nxt = pos + 1
                block, arrives_on, pass_on_with = steps[nxt]
                remote(arrives_on, block, me).wait_recv()
                if pass_on_with is not None:
                    remote(pass_on_with, block, sib).start()
                load(nxt).start()

        proj_ref[...] = _dot(a_ref[...], bbuf[s % 2])

        @pl.when((s == N_DEV - 1) & (i == n_i - 1))
        def _():
            for k in range(N_DEV - 1):
                remote(k, me, sib, True).wait_send()
            own.wait()

    grid_spec = pltpu.PrefetchScalarGridSpec(
        num_scalar_prefetch=1, grid=(N_DEV, n_i),
        in_specs=[pl.BlockSpec((tm, d), lambda s, i, ord_ref: (i, 0)), pl.BlockSpec(memory_space=pl.ANY)],
        out_specs=[pl.BlockSpec((tm, nsh), lambda s, i, ord_ref: (i, ord_ref[s])), pl.BlockSpec(memory_space=pl.ANY)],
        scratch_shapes=[pltpu.VMEM((2, d, nsh), BF16), pltpu.SemaphoreType.DMA((2,)),
                        pltpu.SemaphoreType.DMA((N_DEV - 1,)), pltpu.SemaphoreType.DMA((N_DEV - 1,)),
                        pltpu.SemaphoreType.DMA])
    return pl.pallas_call(
        body, name="proj_gather", grid_spec=grid_spec,
        out_shape=[jax.ShapeDtypeStruct((t, nsh * N_DEV), F32), jax.ShapeDtypeStruct((d, nsh * N_DEV), BF16)],
        compiler_params=_params(2),
    )(order, a1, w_shard)


def _cast_bf16(name, w):
    r, c = w.shape
    tr = _tile(r, 256)
    return pl.pallas_call(
        lambda w_ref, o_ref: o_ref.__setitem__(Ellipsis, w_ref[...].astype(BF16)), name=name,
        grid=(r // tr,), in_specs=[pl.BlockSpec((tr, c), lambda i: (i, 0))],
        out_specs=pl.BlockSpec((tr, c), lambda i: (i, 0)), out_shape=jax.ShapeDtypeStruct((r, c), BF16),
        compiler_params=_params(1),
    )(w)


def _prep_small(c_row, lb_logits):
    d = c_row.shape[1]
    rows = d // LANE

    def body(c_ref, l_ref, o_ref):
        cv = c_ref[...]
        o_ref[0:rows, :] = cv * _sigmoid(cv)
        lbs = [_sigmoid(l_ref[dr][0:1, :] - l_ref[dr][1:2, :]) for dr in range(2)]
        o_ref[rows:rows + 8, :] = jnp.concatenate(lbs + [jnp.zeros((6, LANE), F32)], axis=0)

    return pl.pallas_call(
        body, name="prep_small", out_shape=jax.ShapeDtypeStruct((rows + 8, LANE), F32),
    )(c_row.reshape(rows, LANE), lb_logits)


def _mod_shard(sc_all, w_ada_shard, b_shard):
    d, n = w_ada_shard.shape
    tn = _tile(n, 512)

    def body(s_ref, w_ref, b_ref, o_ref):
        o_ref[...] = _dot(s_ref[...], w_ref[...], precision=HIGHEST) + b_ref[...]

    return pl.pallas_call(
        body, name="mod_shard", grid=(n // tn,),
        in_specs=[pl.BlockSpec((N_DEV, d), lambda j: (0, 0)), pl.BlockSpec((d, tn), lambda j: (0, j)),
                  pl.BlockSpec((1, tn), lambda j: (0, j))],
        out_specs=pl.BlockSpec((N_DEV, tn), lambda j: (0, j)),
        out_shape=jax.ShapeDtypeStruct((N_DEV, n), F32), compiler_params=_params(1),
    )(sc_all, w_ada_shard, b_shard)


def _norm_mod(x, gain, shift, scale):
    t, d = x.shape
    tm = _tile(t, 512)

    def body(x_ref, g_ref, sh_ref, sc_ref, o_ref):
        xv = x_ref[...]
        o_ref[...] = ((xv * _rms(xv) * g_ref[...]) * (1.0 + sc_ref[...]) + sh_ref[...]).astype(BF16)

    vec = pl.BlockSpec((1, d), lambda i: (0, 0))
    return pl.pallas_call(
        body, name="norm_mod", grid=(t // tm,),
        in_specs=[pl.BlockSpec((tm, d), lambda i: (i, 0)), vec, vec, vec],
        out_specs=pl.BlockSpec((tm, d), lambda i: (i, 0)), out_shape=jax.ShapeDtypeStruct((t, d), BF16),
        compiler_params=_params(1),
    )(x, gain, shift, scale)


def _chunk_masks():
    row = lax.broadcasted_iota(jnp.int32, (HEAD, HEAD), 0)
    col = lax.broadcasted_iota(jnp.int32, (HEAD, HEAD), 1)
    same = (row // A_CHUNK) == (col // A_CHUNK)
    return same & (col <= row), same & (col >= row)


def _ones(mask):
    return jnp.where(mask, 1.0, 0.0).astype(BF16)


def _dot_split(ones_bf16, x):
    hi = x.astype(BF16)
    lo = (x - hi.astype(F32)).astype(BF16)
    return _dot(ones_bf16, hi) + _dot(ones_bf16, lo)


def _hgrn_block(direction, f, lb, cum2):
    sf = _sigmoid(f)
    big_f = lb + (1.0 - lb) * sf
    k = (1.0 - lb) * (1.0 - sf)
    lf = jnp.log(big_f)
    both = _dot_split(cum2, lf)
    cf, cr = both[:HEAD], both[HEAD:]
    b, rest = (cf, cr - lf) if direction == 0 else (cr, cf - lf)
    return k, sf, big_f, jnp.exp(b), jnp.exp(-b), jnp.exp(rest)


def _hgrn_fwd(proj, lb, g_norm, width, after):
    t = proj.shape[0]
    heads = width // HEAD
    nb, nc = t // HEAD, t // A_CHUNK
    ua = 4 if nb % 4 == 0 else (2 if nb % 2 == 0 else 1)
    ub = 16 if nc % 16 == 0 else (8 if nc % 8 == 0 else 4)
    q_scale = HEAD ** -0.5

    def body(q_ref, ffw_ref, fbw_ref, v_ref, og_ref, lb_ref, g_ref, *rest):
        outa_ref, osum_ref, qd_s, ke_s, dc_s, o_s = rest[len(after):]
        tril, triu = _chunk_masks()
        cum2 = jnp.concatenate([_ones(tril), _ones(triu)], axis=0)
        f_refs = (ffw_ref, fbw_ref)
        lbs = (lb_ref[0:1, :], lb_ref[1:2, :])

        def phase_a(it, carry):
            loaded = []
            for u in range(ua):
                rows = pl.ds(pl.multiple_of((it * ua + u) * HEAD, HEAD), HEAD)
                loaded.append((rows, q_ref[rows, :], v_ref[rows, :], ffw_ref[rows, :], fbw_ref[rows, :]))
            chains = [(d, rows, qv * q_scale, vv.astype(BF16), fv)
                      for rows, qv, vv, f0, f1 in loaded for d, fv in ((0, f0), (1, f1))]
            blocks = [_hgrn_block(d, fv, lbs[d], cum2) for d, _, _, _, fv in chains]
            scaled = [(qv * eb, k * enb, k * erest, eb * erest)
                      for (_, _, qv, _, _), (k, _, _, eb, enb, erest) in zip(chains, blocks)]
            atts = [jnp.where(tril if d == 0 else triu, _bdot(qd, kd, _NT), 0.0)
                    for (d, _, _, _, _), (qd, kd, _, _) in zip(chains, scaled)]
            intras = [_bdot(att, vv) for att, (_, _, _, vv, _) in zip(atts, chains)]
            results = [(d, rows, o_intra, qd.astype(BF16), ke.astype(BF16), decay)
                       for (d, rows, _, _, _), (qd, _, ke, decay), o_intra in zip(chains, scaled, intras)]
            for d, rows, o_intra, qd16, ke16, decay in results:
                o_s[d, rows, :] = o_intra
                qd_s[d, rows, :] = qd16
                ke_s[d, rows, :] = ke16
                dc_s[d, rows, :] = decay
            return carry

        lax.fori_loop(0, nb // ua, phase_a, 0)

        def phase_b(it, states):
            loaded = []
            for u in range(ub):
                n = it * ub + u
                for d in range(2):
                    c = n if d == 0 else nc - 1 - n
                    start = pl.multiple_of(c * A_CHUNK, A_CHUNK)
                    rows = pl.ds(start, A_CHUNK)
                    loaded.append((d, rows, qd_s[d, rows, :], ke_s[d, rows, :], v_ref[rows, :],
                                   dc_s[d, pl.ds(start, 1), :], o_s[d, rows, :]))
            increments = [_dot(vv.astype(BF16), ke16, _TN) for _, _, _, ke16, vv, _, _ in loaded]
            states = list(states)
            befores = []
            for (d, _, _, _, _, decay, _), inc in zip(loaded, increments):
                befores.append(states[d].astype(BF16))
                states[d] = states[d] * decay + inc
            inters = [_dot(qd16, before, _NT) for (_, _, qd16, _, _, _, _), before in zip(loaded, befores)]
            for (d, rows, _, _, _, _, o_intra), o_inter in zip(loaded, inters):
                o_s[d, rows, :] = o_intra + o_inter
            return tuple(states)

        zero_state = jnp.zeros((HEAD, HEAD), F32)
        lax.fori_loop(0, nc // ub, phase_b, (zero_state, zero_state))

        def phase_c(i, carry):
            rows = pl.ds(pl.multiple_of(i * HEAD, HEAD), HEAD)
            o = o_s[0, rows, :] + o_s[1, rows, :]
            osum_ref[rows, :] = o
            og = og_ref[rows, :]
            outa_ref[rows, :] = (o * _rms(o) * g_ref[...] * (og * _sigmoid(og))).astype(BF16)
            return carry

        lax.fori_loop(0, nb, phase_c, 0)

    def col(p):
        return pl.BlockSpec((t, HEAD), lambda h: (0, p * heads + h))

    return pl.pallas_call(
        body, name="hgrn_fwd", grid=(heads,),
        in_specs=[col(0), col(1), col(2), col(3), col(4),
                  pl.BlockSpec((2, HEAD), lambda h: (0, h)), pl.BlockSpec((1, HEAD), lambda h: (0, 0))]
        + [pl.BlockSpec(memory_space=pl.ANY)] * len(after),
        out_specs=[pl.BlockSpec((t, HEAD), lambda h: (0, h)), pl.BlockSpec((t, HEAD), lambda h: (0, h))],
        out_shape=[jax.ShapeDtypeStruct((t, width), BF16), jax.ShapeDtypeStruct((t, width), F32)],
        scratch_shapes=[pltpu.VMEM((2, t, HEAD), BF16), pltpu.VMEM((2, t, HEAD), BF16), pltpu.VMEM((2, t, HEAD), F32),
                        pltpu.VMEM((2, t, HEAD), F32)],
        compiler_params=_params(1),
    )(proj, proj, proj, proj, proj, lb, g_norm, *after)


def _sgu_core(u_pre, v_pre, g_v, ws_ref, bst):
    u, du = _gelu_and_grad(u_pre)
    v, dv = _gelu_and_grad(v_pre)
    mu = jnp.mean(v, axis=-1, keepdims=True)
    dlt = v - mu
    rstd = lax.rsqrt(jnp.mean(dlt * dlt, axis=-1, keepdims=True) + EPS)
    vhat = dlt * rstd
    vn = vhat * g_v
    groups = vn.shape[1] // HEAD
    cols = []
    for g in range(groups):
        vm_g = _bdot(ws_ref[g], vn[:, g * HEAD:(g + 1) * HEAD]) + bst[:, g:g + 1]
        cols.append(vm_g)
    return u, du, dv, vhat, rstd, vn, jnp.concatenate(cols, axis=1)


def _sgu_fwd(proj, g_v, w_s, bst, width, z_block):
    t = proj.shape[0]

    def body(u_ref, v_ref, g_ref, ws_ref, bst_ref, o_ref):
        u, _, _, _, _, _, vm = _sgu_core(u_ref[...], v_ref[...], g_ref[...], ws_ref, bst_ref[...])
        o_ref[...] = (u * vm).astype(BF16)

    groups = width // HEAD
    return pl.pallas_call(
        body, name="sgu_fwd", grid=(t // HEAD,),
        in_specs=[pl.BlockSpec((HEAD, width), lambda i: (i, z_block)), pl.BlockSpec((HEAD, width), lambda i: (i, z_block + 1)),
                  pl.BlockSpec((1, width), lambda i: (0, 0)), pl.BlockSpec((groups, HEAD, HEAD), lambda i: (0, 0, 0)),
                  pl.BlockSpec((HEAD, groups), lambda i: (0, 0))],
        out_specs=pl.BlockSpec((HEAD, width), lambda i: (i, 0)),
        out_shape=jax.ShapeDtypeStruct((t, width), BF16), compiler_params=_params(1),
    )(proj, proj, g_v, w_s, bst)


def _sgu_bwd(proj, dout_b, dproj, g_v, w_s, w_st, bst, width, z_block):
    t = proj.shape[0]
    groups = width // HEAD
    nblk = t // HEAD

    def body(u_ref, v_ref, do_ref, g_ref, ws_ref, wst_ref, bst_ref, dproj_hbm,
             dz_ref, dg_ref, dws_ref, dbst_ref, res_s):
        i, p = pl.program_id(0), pl.program_id(1)

        @pl.when((i == 0) & (p == 0))
        def _():
            dg_ref[...] = jnp.zeros_like(dg_ref)
            dws_ref[...] = jnp.zeros_like(dws_ref)
            dbst_ref[...] = jnp.zeros_like(dbst_ref)

        @pl.when(p == 0)
        def _():
            g_v = g_ref[...]
            u, du, dv, vhat, rstd, vn, vm = _sgu_core(u_ref[...], v_ref[...], g_v, ws_ref, bst_ref[...])
            dout = do_ref[...].astype(F32)
            res_s[0] = (dout * vm * du).astype(BF16)
            dvm = dout * u
            dvn_cols = []
            for g in range(groups):
                sl = slice(g * HEAD, (g + 1) * HEAD)
                dvm_g = dvm[:, sl]
                dbst_ref[:, g:g + 1] += jnp.sum(dvm_g, axis=1, keepdims=True)
                dws_ref[g] += _bdot(dvm_g, vn[:, sl], _NT)
                dvn_cols.append(_bdot(wst_ref[g], dvm_g))
            dvn = jnp.concatenate(dvn_cols, axis=1)
            dg_ref[...] += _colsum(dvn * vhat)
            dvh = dvn * g_v
            dvg = rstd * (dvh - jnp.mean(dvh, axis=-1, keepdims=True)
                          - vhat * jnp.mean(dvh * vhat, axis=-1, keepdims=True))
            res_s[1] = (dvg * dv).astype(BF16)

        dz_ref[...] = res_s[p]

    n_in = dproj.shape[1]
    return pl.pallas_call(
        body, name="sgu_bwd", grid=(nblk, 2),
        in_specs=[pl.BlockSpec((HEAD, width), lambda i, p: (i, z_block)),
                  pl.BlockSpec((HEAD, width), lambda i, p: (i, z_block + 1)),
                  pl.BlockSpec((HEAD, width), lambda i, p: (i, 0)),
                  pl.BlockSpec((1, width), lambda i, p: (0, 0)),
                  pl.BlockSpec((groups, HEAD, HEAD), lambda i, p: (0, 0, 0)),
                  pl.BlockSpec((groups, HEAD, HEAD), lambda i, p: (0, 0, 0)),
                  pl.BlockSpec((HEAD, groups), lambda i, p: (0, 0)),
                  pl.BlockSpec(memory_space=pl.ANY)],
        out_specs=[pl.BlockSpec((HEAD, width), lambda i, p: (i, z_block + p)),
                   pl.BlockSpec((1, width), lambda i, p: (0, 0)),
                   pl.BlockSpec((groups, HEAD, HEAD), lambda i, p: (0, 0, 0)),
                   pl.BlockSpec((HEAD, groups), lambda i, p: (0, 0))],
        out_shape=[jax.ShapeDtypeStruct((t, n_in), BF16), jax.ShapeDtypeStruct((1, width), F32),
                   jax.ShapeDtypeStruct((groups, HEAD, HEAD), F32), jax.ShapeDtypeStruct((HEAD, groups), F32)],
        scratch_shapes=[pltpu.VMEM((2, HEAD, width), BF16)],
        input_output_aliases={7: 0},
        compiler_params=_params(2),
    )(proj, proj, dout_b, g_v, w_s, w_st, bst, dproj)


def _hgrn_bwd(proj, osum, dout_a, dproj, lb, g_norm, width, after):
    t = proj.shape[0]
    heads = width // HEAD
    nb = t // HEAD
    cpb = HEAD // A_CHUNK
    ubk = 2 if nb % 2 == 0 else 1
    q_scale = HEAD ** -0.5

    def body(q_ref, ffw_ref, fbw_ref, v_ref, og_ref, osum_ref, douta_ref, lb_ref, g_ref, dproj_hbm, after_hbm,
             out_ref, dgh_ref, dlb_ref, do_s, dq_s, dv_s, res_s, ck_s):
        p = pl.program_id(1)
        f_refs = (ffw_ref, fbw_ref)

        @pl.when(p == 0)
        def _():
            tril, triu = _chunk_masks()
            cum2 = jnp.concatenate([_ones(tril), _ones(triu)], axis=0)
            g_row = g_ref[...]

            def pass_norm(i, dgh):
                rows = pl.ds(pl.multiple_of(i * HEAD, HEAD), HEAD)
                o = osum_ref[rows, :]
                r = _rms(o)
                oh = o * r
                og = og_ref[rows, :]
                sg = _sigmoid(og)
                dout = douta_ref[rows, :].astype(F32)
                don = dout * (og * sg)
                res_s[4, rows, :] = (dout * (oh * g_row) * (sg * (1.0 + og * (1.0 - sg)))).astype(BF16)
                doh = don * g_row
                do_s[rows, :] = r * (doh - oh * jnp.mean(doh * oh, axis=-1, keepdims=True))
                return dgh + _colsum(don * oh)

            dgh_ref[...] = lax.fori_loop(0, nb, pass_norm, jnp.zeros((1, HEAD), F32))

            lbs = (lb_ref[0:1, :], lb_ref[1:2, :])
            zero_state = jnp.zeros((HEAD, HEAD), F32)

            def chunk_order(d):
                return list(range(cpb)) if d == 0 else list(range(cpb - 1, -1, -1))

            def chunk(x, j):
                return x[j * A_CHUNK:(j + 1) * A_CHUNK, :]

            def decay_row(e_big, j):
                return e_big[j * A_CHUNK:j * A_CHUNK + 1, :]

            def cat(parts):
                return jnp.concatenate([parts[j] for j in range(cpb)], axis=0)

            def block_states(d, start, incs, e_big):
                befores, st = {}, start
                for j in chunk_order(d):
                    befores[j] = st
                    st = st * decay_row(e_big, j) + incs[j]
                return befores, st

            def pass_states(it, states):
                loaded = []
                for u in range(ubk):
                    for d in range(2):
                        blk = it * ubk + u if d == 0 else nb - 1 - (it * ubk + u)
                        rows = pl.ds(pl.multiple_of(blk * HEAD, HEAD), HEAD)
                        loaded.append((d, blk, f_refs[d][rows, :], v_ref[rows, :]))
                blocks = [_hgrn_block(d, fv, lbs[d], cum2) for d, _, fv, _ in loaded]
                incs = [{j: _bdot(chunk(vv, j), chunk(k * erest, j), _TN) for j in range(cpb)}
                        for (_, _, _, vv), (k, _, _, _, _, erest) in zip(loaded, blocks)]
                states, starts = list(states), []
                for (d, _, _, _), (_, _, _, eb, _, erest), inc in zip(loaded, blocks, incs):
                    starts.append(states[d])
                    states[d] = block_states(d, states[d], inc, eb * erest)[1]
                for (d, blk, _, _), start in zip(loaded, starts):
                    ck_s[d, blk] = start
                return tuple(states)

            lax.fori_loop(0, nb // ubk, pass_states, (zero_state, zero_state))

            def pass_back(it, carry):
                gts, dlb = [carry[0], carry[1]], carry[2]
                loaded = []
                for u, d in ((u, d) for u in range(ubk) for d in range(2)):
                    blk = nb - 1 - (it * ubk + u) if d == 0 else it * ubk + u
                    rows = pl.ds(pl.multiple_of(blk * HEAD, HEAD), HEAD)
                    loaded.append((d, rows, f_refs[d][rows, :], q_ref[rows, :], v_ref[rows, :], do_s[rows, :], ck_s[d, blk]))
                blocks = [_hgrn_block(d, fv, lbs[d], cum2) for d, _, fv, _, _, _, _ in loaded]
                scaled = []
                for (_, _, _, qv, _, _, _), (k, _, _, eb, enb, erest) in zip(loaded, blocks):
                    qh = qv * q_scale
                    scaled.append((qh, qh * eb, k * enb, k * erest, eb * erest))
                masks = [tril if d == 0 else triu for d, *_ in loaded]
                atts = [jnp.where(m, _bdot(qd, kd, _NT), 0.0) for m, (_, qd, kd, _, _) in zip(masks, scaled)]
                datts = [jnp.where(m, _bdot(do, vv, _NT), 0.0) for m, (_, _, _, _, vv, do, _) in zip(masks, loaded)]
                dvs = [_bdot(att, do, _TN) for att, (_, _, _, _, _, do, _) in zip(atts, loaded)]
                dqds = [_bdot(datt, kd) for datt, (_, _, kd, _, _) in zip(datts, scaled)]
                dkds = [_bdot(datt, qd, _TN) for datt, (_, qd, _, _, _) in zip(datts, scaled)]
                s_incs = [{j: _bdot(chunk(vv, j), chunk(ke, j), _TN) for j in range(cpb)}
                          for (_, _, _, _, vv, _, _), (_, _, _, ke, _) in zip(loaded, scaled)]
                g_incs = [{j: _bdot(chunk(do, j), chunk(qd, j), _TN) for j in range(cpb)}
                          for (_, _, _, _, _, do, _), (_, qd, _, _, _) in zip(loaded, scaled)]
                befores, afters, g_at = [], [], []
                for (d, _, _, _, _, _, ck), (_, _, _, _, e_big), s_inc, g_inc in zip(loaded, scaled, s_incs, g_incs):
                    order = chunk_order(d)
                    before, after = block_states(d, ck, s_inc, e_big)
                    befores.append(before)
                    afters.append({j: (before[order[n + 1]] if n + 1 < cpb else after) for n, j in enumerate(order)})
                    at, gt = {}, gts[d]
                    for j in reversed(order):
                        at[j] = gt
                        gt = gt * decay_row(e_big, j) + g_inc[j]
                    gts[d] = gt
                    g_at.append(at)
                dqd_i = [{j: _bdot(chunk(do, j), before[j]) for j in range(cpb)}
                         for (_, _, _, _, _, do, _), before in zip(loaded, befores)]
                dv_i = [{j: _bdot(chunk(ke, j), at[j], _NT) for j in range(cpb)}
                        for (_, _, _, ke, _), at in zip(scaled, g_at)]
                dke = [{j: _bdot(chunk(vv, j), at[j]) for j in range(cpb)}
                       for (_, _, _, _, vv, _, _), at in zip(loaded, g_at)]
                results, new = [], []
                for n, ((d, rows, _, _, _, _, _), (k, sf, big_f, eb, enb, erest), (qh, _, _, _, _)) in enumerate(
                        zip(loaded, blocks, scaled)):
                    dqh = (dqds[n] + cat(dqd_i[n])) * eb
                    dk = dkds[n] * enb + cat(dke[n]) * erest
                    carry_rows = {j: jnp.broadcast_to(_colsum(g_at[n][j] * afters[n][j]), (A_CHUNK, HEAD))
                                  for j in range(cpb)}
                    dlf = _dot_split(_ones(triu if d == 0 else tril), qh * dqh - k * dk) + cat(carry_rows)
                    common = dlf / big_f - dk
                    results.append((d, rows, (k * sf * common).astype(BF16), dqh.astype(BF16),
                                    (dvs[n] + cat(dv_i[n])).astype(BF16)))
                    new.append(_colsum((1.0 - sf) * common))
                for d, rows, df16, dq16, dv16 in results:
                    res_s[1 + d, rows, :] = df16
                    dq_s[d, rows, :] = dq16
                    dv_s[d, rows, :] = dv16
                per_dir = [sum(c for (d, *_), c in zip(loaded, new) if d == dd) for dd in range(2)]
                return gts[0], gts[1], dlb + jnp.concatenate(per_dir, axis=0)

            dlb_ref[...] = lax.fori_loop(0, nb // ubk, pass_back,
                                         (zero_state, zero_state, jnp.zeros((2, HEAD), F32)))[2]

            def pass_out(i, carry):
                rows = pl.ds(pl.multiple_of(i * HEAD, HEAD), HEAD)
                dq = dq_s[0, rows, :].astype(F32) + dq_s[1, rows, :].astype(F32)
                res_s[0, rows, :] = (dq * q_scale).astype(BF16)
                res_s[3, rows, :] = (dv_s[0, rows, :].astype(F32) + dv_s[1, rows, :].astype(F32)).astype(BF16)
                return carry

            lax.fori_loop(0, nb, pass_out, 0)

        out_ref[...] = res_s[p]

    def col(pp):
        return pl.BlockSpec((t, HEAD), lambda h, p: (0, pp * heads + h))

    n_in = dproj.shape[1]
    any_spec = pl.BlockSpec(memory_space=pl.ANY)
    return pl.pallas_call(
        body, name="hgrn_bwd", grid=(heads, 5),
        in_specs=[col(0), col(1), col(2), col(3), col(4),
                  pl.BlockSpec((t, HEAD), lambda h, p: (0, h)), pl.BlockSpec((t, HEAD), lambda h, p: (0, h)),
                  pl.BlockSpec((2, HEAD), lambda h, p: (0, h)), pl.BlockSpec((1, HEAD), lambda h, p: (0, 0)),
                  any_spec, any_spec],
        out_specs=[pl.BlockSpec((t, HEAD), lambda h, p: (0, p * heads + h)),
                   pl.BlockSpec((None, 1, HEAD), lambda h, p: (h, 0, 0)),
                   pl.BlockSpec((2, HEAD), lambda h, p: (0, h))],
        out_shape=[jax.ShapeDtypeStruct((t, n_in), BF16), jax.ShapeDtypeStruct((heads, 1, HEAD), F32),
                   jax.ShapeDtypeStruct((2, width), F32)],
        scratch_shapes=[pltpu.VMEM((t, HEAD), F32), pltpu.VMEM((2, t, HEAD), BF16), pltpu.VMEM((2, t, HEAD), BF16),
                        pltpu.VMEM((5, t, HEAD), BF16), pltpu.VMEM((2, nb, HEAD, HEAD), F32)],
        input_output_aliases={9: 0},
        compiler_params=_params(2),
    )(proj, proj, proj, proj, proj, osum, dout_a, lb, g_norm, dproj, after)


def _adamw(w, g, m, v):
    m = ADAM_B1 * m + (1.0 - ADAM_B1) * g
    v = ADAM_B2 * v + (1.0 - ADAM_B2) * (g * g)
    m_hat = m / (1.0 - ADAM_B1 ** ADAM_STEP)
    v_hat = v / (1.0 - ADAM_B2 ** ADAM_STEP)
    delta = -ADAM_LR * (m_hat / (jnp.sqrt(v_hat) + ADAM_EPS) + ADAM_WD * w)
    return delta, m, v


def _adamw_big(name, me, w, m, v, parts, axis):
    r, c = w.shape
    n_parts = len(parts)
    tr = _tile(r // n_parts, 128)
    per = r // n_parts // tr
    assert axis == 1 or n_parts == 1

    def body(me_ref, w_ref, m_ref, v_ref, *rest):
        g_refs, l_refs = rest[:n_parts], rest[n_parts:2 * n_parts]
        og_ref, od_ref, om_ref, ov_ref = rest[2 * n_parts:]
        g = None
        for p in range(n_parts):
            total = g_refs[p][...]
            for s in range(N_DEV - 1):
                total = total + l_refs[p][s].astype(F32)
            g = total if p == 0 else jnp.where(pl.program_id(0) // per == p, total, g)
        og_ref[...] = g
        od_ref[...], om_ref[...], ov_ref[...] = _adamw(w_ref[...], g, m_ref[...], v_ref[...])

    def within(p, i):
        return jnp.clip(i - p * per, 0, per - 1)

    shard = pl.BlockSpec((tr, c), lambda i, me_ref: (i, 0))
    if axis == 1:
        own = [pl.BlockSpec((tr, c), lambda i, me_ref, p=p: (within(p, i), me_ref[0])) for p in range(n_parts)]
    else:
        own = [pl.BlockSpec((tr, c), lambda i, me_ref: (me_ref[0] * (r // tr) + i, 0))]
    landed = [pl.BlockSpec((N_DEV - 1, tr, c), lambda i, me_ref, p=p: (0, within(p, i), 0)) for p in range(n_parts)]
    grid_spec = pltpu.PrefetchScalarGridSpec(
        num_scalar_prefetch=1, grid=(r // tr,),
        in_specs=[shard, shard, shard] + own + landed, out_specs=[shard] * 4)
    return pl.pallas_call(
        body, name=name, grid_spec=grid_spec, out_shape=[jax.ShapeDtypeStruct((r, c), F32)] * 4,
        compiler_params=_params(1),
    )(me, w, m, v, *[g for g, _ in parts], *[ld for _, ld in parts])


def _adamw_ada(sct, dmod_mine, w, m, v):
    d, n = w.shape
    tr = _tile(d, 256)

    def body(s_ref, dm_ref, w_ref, m_ref, v_ref, og_ref, od_ref, om_ref, ov_ref):
        g = _dot(s_ref[...], dm_ref[...], precision=HIGHEST)
        og_ref[...] = g
        od_ref[...], om_ref[...], ov_ref[...] = _adamw(w_ref[...], g, m_ref[...], v_ref[...])

    blk = pl.BlockSpec((tr, n), lambda i: (i, 0))
    return pl.pallas_call(
        body, name="adamw_ada", grid=(d // tr,),
        in_specs=[pl.BlockSpec((tr, N_DEV), lambda i: (i, 0)), pl.BlockSpec((N_DEV, n), lambda i: (0, 0)), blk, blk, blk],
        out_specs=[blk] * 4, out_shape=[jax.ShapeDtypeStruct((d, n), F32)] * 4, compiler_params=_params(1),
    )(sct, dmod_mine, w, m, v)


def _adamw_small(gathered, w, m, v):
    def body(g_ref, w_ref, m_ref, v_ref, og_ref, od_ref, om_ref, ov_ref):
        g = g_ref[0]
        for s in range(1, N_DEV):
            g = g + g_ref[s]
        og_ref[...] = g
        od_ref[...], om_ref[...], ov_ref[...] = _adamw(w_ref[...], g, m_ref[...], v_ref[...])

    return pl.pallas_call(
        body, name="adamw_small", out_shape=[jax.ShapeDtypeStruct(w.shape, F32)] * 4,
        compiler_params=pltpu.CompilerParams(vmem_limit_bytes=VMEM_LIMIT),
    )(gathered, w, m, v)


def _adamw_lb(dlb_mine, lb_logits, m, v):
    def body(d_ref, l_ref, m_ref, v_ref, og_ref, od_ref, om_ref, ov_ref):
        dlb = d_ref[0]
        for s in range(1, N_DEV):
            dlb = dlb + d_ref[s]
        for dr in range(2):
            lb = _sigmoid(l_ref[dr][0:1, :] - l_ref[dr][1:2, :])
            d0 = dlb[dr:dr + 1] * lb * (1.0 - lb)
            g = jnp.concatenate([d0, -d0], axis=0)
            og_ref[dr] = g
            od_ref[dr], om_ref[dr], ov_ref[dr] = _adamw(l_ref[dr], g, m_ref[dr], v_ref[dr])

    return pl.pallas_call(body, name="adamw_lb", out_shape=[jax.ShapeDtypeStruct(lb_logits.shape, F32)] * 4,
                          )(dlb_mine, lb_logits, m, v)


def _rows(a, pad_to=8):
    flat = a.reshape(-1, LANE)
    pad = (-flat.shape[0]) % pad_to
    return jnp.pad(flat, ((0, pad), (0, 0))) if pad else flat


def kernel(x, c, w_ada, b_ada, g_pre_mix, g_post_mix, g_pre_ffn, g_post_ffn, w_in, lb_logits, g_hgrn_norm, w_a_out, g_sgu_norm, w_spatial, b_spatial, w_b_out, w_o, w_ff1, w_ff2, loss_target, m_w_ada, m_b_ada, m_g_pre_mix, m_g_post_mix, m_g_pre_ffn, m_g_post_ffn, m_w_in, m_lb_logits, m_g_hgrn_norm, m_w_a_out, m_g_sgu_norm, m_w_spatial, m_b_spatial, m_w_b_out, m_w_o, m_w_ff1, m_w_ff2, v_w_ada, v_b_ada, v_g_pre_mix, v_g_post_mix, v_g_pre_ffn, v_g_post_ffn, v_w_in, v_lb_logits, v_g_hgrn_norm, v_w_a_out, v_g_sgu_norm, v_w_spatial, v_b_spatial, v_w_b_out, v_w_o, v_w_ff1, v_w_ff2):
    t, d = x.shape[1], x.shape[2]
    n_in = w_in.shape[2] * N_DEV
    width = (n_in - 2 * d) // 7
    heads = width // HEAD
    assert heads == N_DEV and width % LANE == 0
    d_ff = w_ff1.shape[2] * N_DEV
    n_ada = w_ada.shape[2]
    me = _dev_index()
    me_arr = me.reshape(1).astype(jnp.int32)
    x2, tgt = x[0], loss_target[0]

    big = [w_in[0], w_a_out[0], w_b_out[0], w_o[0], w_ff1[0], w_ff2[0]]
    big_axes = [1, 1, 1, 0, 1, 0]
    big_names = ["w_in", "w_a_out", "w_b_out", "w_o", "w_ff1", "w_ff2"]
    w_in16 = _cast_bf16("cast_w_in", big[0])
    own_parts = [_cast_into_full("cast_" + nm, me_arr, w, ax) for nm, w, ax in zip(big_names[1:], big[1:], big_axes[1:])]

    c_rows = d // LANE
    small = _all_gather_small("gather_c_lb", _prep_small(c[0:1], lb_logits))
    sc_all = small[:, :c_rows, :].reshape(N_DEV, d)
    lb = jnp.transpose(small[:, c_rows:c_rows + 2, :], (1, 0, 2)).reshape(2, width)
    b_shard = lax.dynamic_slice_in_dim(b_ada, me * n_ada, n_ada, axis=1)
    mod_sh = _mod_shard(sc_all, w_ada[0], b_shard)
    mod_all = _all_gather_small("gather_mod", _rows(mod_sh))
    mod_all = mod_all[:, :N_DEV * n_ada // LANE, :].reshape(N_DEV, N_DEV, n_ada)
    mod6 = lax.dynamic_index_in_dim(mod_all, me, axis=1, keepdims=False).reshape(N_MOD, d)
    sh1, sc1, gt1, sh2, sc2, gt2 = [mod6[i:i + 1] for i in range(N_MOD)]

    a1 = _norm_mod(x2, g_pre_mix, sh1, sc1)
    tm = _tile(t, 512)

    def store_bf16(acc, i, j, extra_refs, out_refs, rows):
        out_refs[0][...] = acc.astype(BF16)

    xq, yq, cq = lax.axis_index("x"), lax.axis_index("y"), lax.axis_index("c")
    chips = [(1 - xq, yq), (xq, 1 - yq), (1 - xq, 1 - yq)]
    order = jnp.stack([me, 4 * xq + 2 * yq + 1 - cq]
                      + [4 * a + 2 * b + cq for a, b in chips[:2]] + [4 * a + 2 * b + 1 - cq for a, b in chips[:2]]
                      + [4 * chips[2][0] + 2 * chips[2][1] + cq, 4 * chips[2][0] + 2 * chips[2][1] + 1 - cq]).astype(jnp.int32)
    proj, wf_in = _proj_gather(a1, w_in16, order)

    proj, own_parts = lax.optimization_barrier((proj, own_parts))
    gathers = {}
    for key, lo, hi in (("mid", 1, 4), ("ff1", 4, 5), ("ff2", 5, 6)):
        far, near = _gather_stage_plans(own_parts[lo - 1:hi - 1], big_axes[lo:hi])
        gathers[key] = [far, near, _split_start("gather_%s_start" % key, far, landing=own_parts[lo - 1:hi - 1])]

    def pass_on(key, after):
        far, near, (sems, thru, _) = gathers[key]
        parts = _split_wait("gather_%s_wait" % key, far, sems, thru, after)[1]
        gathers[key].append(_split_start("pass_%s_start" % key, near, landing=list(parts)))
        return gathers[key][3][2]

    def gathered_weights(key, after):
        near, (sems, thru, _) = gathers[key][1], gathers[key][3]
        return _split_wait("pass_%s_wait" % key, near, sems, thru, after)[1]

    out_a, osum = _hgrn_fwd(proj, lb, g_hgrn_norm, width,
                            after=[gathers[key][2][2] for key in ("mid", "ff1", "ff2")])
    passed_mid = pass_on("mid", out_a)
    z_block = 5
    bst = b_spatial[0].T
    out_b = _sgu_fwd(proj, g_sgu_norm, w_spatial[0], bst, width, z_block)
    wf_a, wf_b, wf_o = gathered_weights("mid", out_b)

    tn_d = _tile(d, 1024)
    blk_d = ((tm, tn_d), lambda i, j: (i, j))
    y_a, = _mm("y_a", out_a, wf_a, _NN, t, d, width, tm, tn_d, width, _after(passed_mid),
               [(jax.ShapeDtypeStruct((t, d), BF16),) + blk_d], store_bf16)
    ga_blk = (5 * width + 2 * width) // tn_d
    gb_blk = ga_blk + d // tn_d

    def merge(acc, i, j, extra_refs, out_refs, rows):
        ga, gb, ya = extra_refs
        out_refs[0][...] = acc.astype(BF16)
        out_refs[1][...] = (_sigmoid(ga[...]) * ya[...].astype(F32) + _sigmoid(gb[...]) * acc).astype(BF16)

    y_b, merged = _mm("y_b_merge", out_b, wf_b, _NN, t, d, width, tm, tn_d, width,
                      [(proj, (tm, tn_d), lambda i, j: (i, ga_blk + j)), (proj, (tm, tn_d), lambda i, j: (i, gb_blk + j)),
                       (y_a,) + blk_d],
                      [(jax.ShapeDtypeStruct((t, d), BF16),) + blk_d, (jax.ShapeDtypeStruct((t, d), BF16),) + blk_d], merge)

    tr = _tile(t, 512)
    rc = 32 if tr % 32 == 0 else None
    row_d = ((tr, d), lambda i, j: (i, 0))
    vec_d = ((1, d), lambda i, j: (0, 0))

    passed_ff1 = pass_on("ff1", merged)

    def post_mix(acc, i, j, extra_refs, out_refs, rows):
        x_r, gt1_r, g2_r, g3_r, sc2_r, sh2_r = extra_refs[:6]
        h1 = x_r[rows, :] + gt1_r[...] * (acc * _rms(acc) * g2_r[...])
        out_refs[0][rows, :] = acc.astype(BF16)
        out_refs[1][rows, :] = h1
        out_refs[2][rows, :] = ((h1 * _rms(h1) * g3_r[...]) * (1.0 + sc2_r[...]) + sh2_r[...]).astype(BF16)

    mo, h1, a2 = _mm("w_o_post_mix", merged, wf_o, _NN, t, d, d, tr, d, d,
                     [(x2,) + row_d, (gt1,) + vec_d, (g_post_mix,) + vec_d, (g_pre_ffn,) + vec_d, (sc2,) + vec_d, (sh2,) + vec_d]
                     + _after(passed_ff1),
                     [(jax.ShapeDtypeStruct((t, d), BF16),) + row_d, (jax.ShapeDtypeStruct((t, d), F32),) + row_d,
                      (jax.ShapeDtypeStruct((t, d), BF16),) + row_d], post_mix, row_chunk=rc)

    tn_f = _tile(d_ff, 2048)
    blk_f = ((tm, tn_f), lambda i, j: (i, j))

    def relu_sq(acc, i, j, extra_refs, out_refs, rows):
        r = jnp.maximum(acc, 0.0)
        out_refs[0][...] = acc.astype(BF16)
        out_refs[1][...] = (r * r).astype(BF16)

    wf_1, = gathered_weights("ff1", a2)
    hff, act = _mm(
        "ff1", a2, wf_1, _NN, t, d_ff, d, tm, tn_f, d, [],
        [(jax.ShapeDtypeStruct((t, d_ff), BF16),) + blk_f, (jax.ShapeDtypeStruct((t, d_ff), BF16),) + blk_f], relu_sq)
    pass_on("ff2", hff)
    wf_2, = gathered_weights("ff2", act)

    sums_d = ((8, d), lambda i, j: (0, 0))

    def zero_first(sums_r, i, rows):
        if rows.start in (None, 0):
            @pl.when(i == 0)
            def _():
                sums_r[...] = jnp.zeros_like(sums_r)

    def loss_head(acc, i, j, extra_refs, out_refs, rows):
        h1_r, tgt_r, gt2_r, g4_r = extra_refs
        dy_r, dff_r, sums_r = out_refs
        r4 = _rms(acc)
        ffn = acc * r4
        n4 = ffn * g4_r[...]
        err = h1_r[rows, :] + gt2_r[...] * n4 - tgt_r[rows, :]
        dy = err * (1.0 / d)
        dy_r[rows, :] = dy.astype(BF16)
        dn4 = dy * gt2_r[...]
        dffn = dn4 * g4_r[...]
        dff_r[rows, :] = (r4 * (dffn - ffn * jnp.mean(dffn * ffn, axis=-1, keepdims=True))).astype(BF16)
        zero_first(sums_r, i, rows)

        sums_r[0:1, :] += _colsum(err * err)
        sums_r[1:2, :] += _colsum(dy * n4)
        sums_r[2:3, :] += _colsum(dn4 * ffn)

    tk_f = _tile(d_ff, 1024)
    dy, dff, sums_f = _mm("ff2_loss", act, wf_2, _NN, t, d, d_ff, tr, d, tk_f,
                          [(h1,) + row_d, (tgt,) + row_d, (gt2,) + vec_d, (g_post_ffn,) + vec_d],
                          [(jax.ShapeDtypeStruct((t, d), BF16),) + row_d, (jax.ShapeDtypeStruct((t, d), BF16),) + row_d,
                           (jax.ShapeDtypeStruct((8, d), F32),) + sums_d], loss_head, row_chunk=rc)
    loss_mine = (0.5 / d) * jnp.sum(sums_f[0])

    def relu_sq_bwd(acc, i, j, extra_refs, out_refs, rows):
        out_refs[0][...] = (acc * (2.0 * jnp.maximum(extra_refs[0][...].astype(F32), 0.0))).astype(BF16)

    dhff, = _mm("d_hff", dff, wf_2, _NT, t, d_ff, d, tm, tn_f, d, [(hff,) + blk_f],
                [(jax.ShapeDtypeStruct((t, d_ff), BF16),) + blk_f], relu_sq_bwd)
    scatters = {}

    def send_grads(key, grads16, axes):
        plan = _scatter_plan(grads16, axes)
        scatters[key] = (plan,) + _split_start("scatter_%s_start" % key, plan)
        return scatters[key][3]

    def received_grads(key, after):
        plan, sems, thru, _ = scatters[key]
        return _split_wait("scatter_%s_wait" % key, plan, sems, thru, after)[1]

    gw_ff2, gw_ff2_16 = _grad_w("grad_w_ff2", act, dff)
    sent_ff2 = send_grads("ff2", [gw_ff2_16], big_axes[5:6])
    gw_ff1, gw_ff1_16 = _grad_w("grad_w_ff1", a2, dhff, token=sent_ff2)
    sent_ff1 = send_grads("ff1", [gw_ff1_16], big_axes[4:5])

    def pre_ffn_bwd(acc, i, j, extra_refs, out_refs, rows):
        h1_r, dy_r, mo_r, sc2_r, g3_r, gt1_r, g2_r = extra_refs[:7]
        dh1_r, dmo_r, sums_r = out_refs
        h1v = h1_r[rows, :]
        r3 = _rms(h1v)
        h1n = h1v * r3
        dn3 = acc * (1.0 + sc2_r[...])
        dh1n = dn3 * g3_r[...]
        dh1 = dy_r[rows, :].astype(F32) + r3 * (dh1n - h1n * jnp.mean(dh1n * h1n, axis=-1, keepdims=True))
        dh1_r[rows, :] = dh1.astype(BF16)
        mov = mo_r[rows, :].astype(F32)
        r2 = _rms(mov)
        mon = mov * r2
        dn2 = dh1 * gt1_r[...]
        dmon = dn2 * g2_r[...]
        dmo_r[rows, :] = (r2 * (dmon - mon * jnp.mean(dmon * mon, axis=-1, keepdims=True))).astype(BF16)
        zero_first(sums_r, i, rows)

        sums_r[0:1, :] += _colsum(acc)
        sums_r[1:2, :] += _colsum(acc * (h1n * g3_r[...]))
        sums_r[2:3, :] += _colsum(dn3 * h1n)
        sums_r[3:4, :] += _colsum(dh1 * (mon * g2_r[...]))
        sums_r[4:5, :] += _colsum(dn2 * mon)

    dh1, dmo, sums_m = _mm("d_a2_pre_ffn", dhff, wf_1, _NT, t, d, d_ff, tr, d, tk_f,
                           [(h1,) + row_d, (dy,) + row_d, (mo,) + row_d, (sc2,) + vec_d, (g_pre_ffn,) + vec_d,
                            (gt1,) + vec_d, (g_post_mix,) + vec_d] + _after(sent_ff1),
                           [(jax.ShapeDtypeStruct((t, d), BF16),) + row_d, (jax.ShapeDtypeStruct((t, d), BF16),) + row_d,
                            (jax.ShapeDtypeStruct((8, d), F32),) + sums_d], pre_ffn_bwd, row_chunk=rc)
    gw_o, gw_o_16 = _grad_w("grad_w_o", merged, dmo)

    n_j = d // tn_d

    def merge_bwd_body(dmo_ref, wo_ref, ga_ref, gb_ref, ya_ref, yb_ref, dya_ref, dyb_ref, dproj_ref, acc_s):
        g = pl.program_id(2)

        @pl.when(g == 0)
        def _():
            dm = _dot(dmo_ref[...], wo_ref[...], _NT)
            acc_s[...] = dm
            sa = _sigmoid(ga_ref[...])
            dya_ref[...] = (dm * sa).astype(BF16)
            dproj_ref[...] = (dm * ya_ref[...].astype(F32) * sa * (1.0 - sa)).astype(BF16)

        @pl.when(g == 1)
        def _():
            dm = acc_s[...]
            sb = _sigmoid(gb_ref[...])
            dyb_ref[...] = (dm * sb).astype(BF16)
            dproj_ref[...] = (dm * yb_ref[...].astype(F32) * sb * (1.0 - sb)).astype(BF16)

    tile3 = pl.BlockSpec((tm, tn_d), lambda i, j, g: (i, j))
    dy_a, dy_b, dproj = pl.pallas_call(
        merge_bwd_body, name="d_merged", grid=(t // tm, n_j, 2),
        in_specs=[pl.BlockSpec((tm, d), lambda i, j, g: (i, 0)), pl.BlockSpec((tn_d, d), lambda i, j, g: (j, 0)),
                  pl.BlockSpec((tm, tn_d), lambda i, j, g: (i, ga_blk + j)),
                  pl.BlockSpec((tm, tn_d), lambda i, j, g: (i, gb_blk + j)), tile3, tile3],
        out_specs=[tile3, tile3, pl.BlockSpec((tm, tn_d), lambda i, j, g: (i, ga_blk + g * n_j + j))],
        out_shape=[jax.ShapeDtypeStruct((t, d), BF16), jax.ShapeDtypeStruct((t, d), BF16),
                   jax.ShapeDtypeStruct((t, n_in), BF16)],
        scratch_shapes=[pltpu.VMEM((tm, tn_d), F32)], compiler_params=_params(3),
    )(dmo, wf_o, proj, proj, y_a, y_b)

    tn_w = _tile(width, 1024)
    blk_w = ((tm, tn_w), lambda i, j: (i, j))
    dout_a, = _mm("d_out_a", dy_a, wf_a, _NT, t, width, d, tm, tn_w, d, [],
                  [(jax.ShapeDtypeStruct((t, width), BF16),) + blk_w], store_bf16)
    dout_b, = _mm("d_out_b", dy_b, wf_b, _NT, t, width, d, tm, tn_w, d, [],
                  [(jax.ShapeDtypeStruct((t, width), BF16),) + blk_w], store_bf16)
    gw_a, gw_a_16 = _grad_w("grad_w_a_out", out_a, dy_a)
    gw_b, gw_b_16 = _grad_w("grad_w_b_out", out_b, dy_b)

    w_st = jnp.swapaxes(w_spatial[0], 1, 2)
    dproj, dg_sgu, dw_sp, dbst = _sgu_bwd(proj, dout_b, dproj, g_sgu_norm, w_spatial[0], w_st, bst, width, z_block)
    sent_mid = send_grads("mid", [gw_a_16, gw_b_16, gw_o_16], big_axes[1:4])
    dproj, dgh_heads, dlb = _hgrn_bwd(proj, osum, dout_a, dproj, lb, g_hgrn_norm, width, after=sent_mid)
    gw_in_top, gw_in_top16 = _grad_w("grad_w_in_top", a1, dproj, rows=(0, d // 2))
    sent_top = send_grads("in_top", [gw_in_top16], big_axes[:1])
    gw_in_bot, gw_in_bot16 = _grad_w("grad_w_in_bot", a1, dproj, token=sent_top, rows=(d // 2, d // 2))
    sent_in = send_grads("in_bot", [gw_in_bot16], big_axes[:1])

    def pre_mix_bwd(acc, i, j, extra_refs, out_refs, rows):
        x_r, dh1_r, sc1_r, g1_r = extra_refs[:4]
        dx_r, sums_r = out_refs
        xv = x_r[rows, :]
        r1 = _rms(xv)
        xn = xv * r1
        dn1 = acc * (1.0 + sc1_r[...])
        dxn = dn1 * g1_r[...]
        dx_r[rows, :] = dh1_r[rows, :].astype(F32) + r1 * (dxn - xn * jnp.mean(dxn * xn, axis=-1, keepdims=True))
        zero_first(sums_r, i, rows)

        sums_r[0:1, :] += _colsum(acc)
        sums_r[1:2, :] += _colsum(acc * (xn * g1_r[...]))
        sums_r[2:3, :] += _colsum(dn1 * xn)

    tk_in = _tile(n_in, 1024)
    grad_x, sums_x = _mm(
        "d_a1_pre_mix", dproj, wf_in, _NT, t, d, n_in, tr, d, tk_in,
        [(x2,) + row_d, (dh1,) + row_d, (sc1,) + vec_d, (g_pre_mix,) + vec_d] + _after(sent_in),
        [(jax.ShapeDtypeStruct((t, d), F32),) + row_d, (jax.ShapeDtypeStruct((8, d), F32),) + sums_d],
        pre_mix_bwd, row_chunk=rc)

    dmod = jnp.concatenate([sums_x[0:2], sums_m[3:4], sums_m[0:2], sums_f[1:2]], axis=0).reshape(N_DEV, n_ada // LANE, LANE)
    ada_rows = -(-(n_ada // LANE) // 8) * 8
    dmod = jnp.pad(dmod, ((0, 0), (0, ada_rows - n_ada // LANE), (0, 0))).reshape(N_DEV * ada_rows, LANE)
    parts = [dmod, _rows(sums_x[2:3]), _rows(sums_m[4:5]), _rows(sums_m[2:3]), _rows(sums_f[2:3]),
             _rows(jnp.sum(dgh_heads, axis=0)), _rows(dg_sgu), _rows(dw_sp), _rows(dbst.T)]
    n_params = sum(p.shape[0] for p in parts)
    parts.append(jnp.full((8, LANE), loss_mine, F32))
    n_common = n_params + 8
    payload = jnp.concatenate(parts + [_rows(dlb)], axis=0)

    moms = [m_w_in, m_w_a_out, m_w_b_out, m_w_o, m_w_ff1, m_w_ff2]
    vars_ = [v_w_in, v_w_a_out, v_w_b_out, v_w_o, v_w_ff1, v_w_ff2]
    big_out = {}

    def big_update(nm, parts):
        k = big_names.index(nm)
        outs = _adamw_big("adamw_" + nm, me_arr, big[k], moms[k][0], vars_[k][0], parts, big_axes[k])
        big_out[nm] = [o[None] for o in outs]
        return outs[0]

    land_ff2, = received_grads("ff2", grad_x)
    done = big_update("w_ff2", [(gw_ff2, land_ff2)])
    land_ff1, = received_grads("ff1", done)
    done = big_update("w_ff1", [(gw_ff1, land_ff1)])
    land_a, land_b, land_o = received_grads("mid", done)
    big_update("w_a_out", [(gw_a, land_a)])
    big_update("w_b_out", [(gw_b, land_b)])
    done = big_update("w_o", [(gw_o, land_o)])

    payload, _ = lax.optimization_barrier((payload, done))
    gathered = _all_gather_small("gather_small_grads", payload)

    dmod_mine = lax.dynamic_slice_in_dim(gathered[:, :N_DEV * ada_rows, :].reshape(N_DEV, N_DEV, ada_rows * LANE),
                                         me, 1, axis=1)[:, 0, :n_ada]
    ada_out = [o[None] for o in _adamw_ada(sc_all.T, dmod_mine, w_ada[0], m_w_ada[0], v_w_ada[0])]

    def pack(b_, g1_, g2_, g3_, g4_, gh_, gs_, ws_, bs_):
        b3 = b_.reshape(N_DEV, n_ada // LANE, LANE)
        b3 = jnp.pad(b3, ((0, 0), (0, ada_rows - n_ada // LANE), (0, 0))).reshape(N_DEV * ada_rows, LANE)
        return jnp.concatenate([b3, _rows(g1_), _rows(g2_), _rows(g3_), _rows(g4_), _rows(gh_), _rows(gs_),
                                _rows(ws_), _rows(bs_), jnp.zeros((8, LANE), F32)], axis=0)

    small_w = (b_ada, g_pre_mix, g_post_mix, g_pre_ffn, g_post_ffn, g_hgrn_norm, g_sgu_norm, w_spatial, b_spatial)
    small_m = (m_b_ada, m_g_pre_mix, m_g_post_mix, m_g_pre_ffn, m_g_post_ffn, m_g_hgrn_norm, m_g_sgu_norm, m_w_spatial, m_b_spatial)
    small_v = (v_b_ada, v_g_pre_mix, v_g_post_mix, v_g_pre_ffn, v_g_post_ffn, v_g_hgrn_norm, v_g_sgu_norm, v_w_spatial, v_b_spatial)
    packed = _adamw_small(gathered[:, :n_common, :], pack(*small_w), pack(*small_m), pack(*small_v))

    def unpack(slab):
        outs, at = [], 0
        b3 = slab[:N_DEV * ada_rows].reshape(N_DEV, ada_rows, LANE)[:, :n_ada // LANE, :]
        outs.append(b3.reshape(b_ada.shape))
        at = N_DEV * ada_rows
        for ref in small_w[1:]:
            n_el = ref.size
            n_r = -(-(n_el // LANE) // 8) * 8
            outs.append(slab[at:at + n_el // LANE].reshape(ref.shape))
            at += n_r
        return outs

    small_out = [unpack(s) for s in packed]
    loss = packed[0][n_params, 0]

    dlb_all = gathered[:, n_common:n_common + 2 * heads, :].reshape(N_DEV, 2, heads, LANE)
    dlb_mine = lax.dynamic_index_in_dim(dlb_all, me, axis=2, keepdims=False)
    lb_out = _adamw_lb(dlb_mine, lb_logits, m_lb_logits, v_lb_logits)

    land_top, = received_grads("in_top", ada_out[0])
    land_bot, = received_grads("in_bot", land_top)
    big_update("w_in", [(gw_in_top, land_top), (gw_in_bot, land_bot)])

    order = ["w_ada", "b_ada", "g_pre_mix", "g_post_mix", "g_pre_ffn", "g_post_ffn", "w_in", "lb_logits", "g_hgrn_norm",
             "w_a_out", "g_sgu_norm", "w_spatial", "b_spatial", "w_b_out", "w_o", "w_ff1", "w_ff2"]
    small_names = ["b_ada", "g_pre_mix", "g_post_mix", "g_pre_ffn", "g_post_ffn", "g_hgrn_norm", "g_sgu_norm", "w_spatial", "b_spatial"]

    def leaf(kind, nm):
        if nm == "w_ada":
            return ada_out[kind]
        if nm == "lb_logits":
            return lb_out[kind]
        if nm in big_out:
            return big_out[nm][kind]
        return small_out[kind][small_names.index(nm)]

    result = [loss, grad_x[None]]
    for kind in range(4):
        result += [leaf(kind, nm) for nm in order]
    return tuple(result)
```

```python
import math

import jax
import jax.numpy as jnp
from jax import lax
from jax.experimental import pallas as pl
from jax.experimental.pallas import tpu as pltpu

F32 = jnp.float32
BF16 = jnp.bfloat16
MESH = pl.DeviceIdType.MESH
HIGHEST = lax.Precision.HIGHEST

N_DEV = 8
HEAD = 128
A_CHUNK = 32
N_MOD = 6
EPS = 1e-6
LANE = 128
VMEM_LIMIT = 60 * 1024 * 1024

ADAM_LR = 0.001
ADAM_B1 = 0.9
ADAM_B2 = 0.999
ADAM_EPS = 1e-08
ADAM_WD = 0.01
ADAM_STEP = 10

_NN = (((1,), (0,)), ((), ()))
_NT = (((1,), (1,)), ((), ()))
_TN = (((0,), (0,)), ((), ()))


def _dot(a, b, dims=_NN, precision=None):
    return lax.dot_general(a, b, dims, preferred_element_type=F32, precision=precision)


def _bdot(a, b, dims=_NN):
    return _dot(a.astype(BF16), b.astype(BF16), dims)


def _params(n_grid):
    return pltpu.CompilerParams(dimension_semantics=("arbitrary",) * n_grid, vmem_limit_bytes=VMEM_LIMIT)


def _dev_index():
    return lax.axis_index("x") * 4 + lax.axis_index("y") * 2 + lax.axis_index("c")


def _dev_coords(i):
    return (i // 4, (i // 2) % 2, i % 2)


def _sigmoid(x):
    return 1.0 / (1.0 + jnp.exp(-x))


def _erf(x):
    ax = jnp.abs(x)
    t = 1.0 / (1.0 + 0.3275911 * ax)
    poly = ((((1.061405429 * t - 1.453152027) * t + 1.421413741) * t - 0.284496736) * t + 0.254829592) * t
    y = 1.0 - poly * jnp.exp(-ax * ax)
    return jnp.where(x < 0, -y, y)


def _gelu_and_grad(x):
    cdf = 0.5 * (1.0 + _erf(x * (2.0 ** -0.5)))
    pdf = jnp.exp(-0.5 * x * x) * (1.0 / math.sqrt(2.0 * math.pi))
    return x * cdf, cdf + x * pdf


def _rms(x):
    return lax.rsqrt(jnp.mean(x * x, axis=-1, keepdims=True) + EPS)


def _colsum(x):
    return jnp.sum(x, axis=0, keepdims=True)


def _tile(n, want):
    if n <= want:
        return n
    t = (want // LANE) * LANE
    while n % t:
        t -= LANE
    assert t > 0, (n, want)
    return t


def _all_gather_small(name, payload):
    rows = payload.shape[0]

    def body(p_ref, out_ref, send_sems, recv_sems, local_sem):
        me = _dev_index()
        mine = pltpu.make_async_copy(p_ref, out_ref.at[me], local_sem)
        mine.start()
        sends = []
        for r in range(1, N_DEV):
            peer = (me + r) % N_DEV
            cp = pltpu.make_async_remote_copy(
                src_ref=p_ref, dst_ref=out_ref.at[me], send_sem=send_sems.at[r - 1], recv_sem=recv_sems.at[r - 1],
                device_id=_dev_coords(peer), device_id_type=MESH)
            cp.start()
            sends.append(cp)
        for r in range(1, N_DEV):
            src = (me + N_DEV - r) % N_DEV
            pltpu.make_async_remote_copy(
                src_ref=p_ref, dst_ref=out_ref.at[src], send_sem=send_sems.at[r - 1], recv_sem=recv_sems.at[r - 1],
                device_id=_dev_coords(src), device_id_type=MESH).wait_recv()
        for cp in sends:
            cp.wait_send()
        mine.wait()

    return pl.pallas_call(
        body, name=name,
        out_shape=jax.ShapeDtypeStruct((N_DEV, rows, LANE), F32),
        in_specs=[pl.BlockSpec(memory_space=pltpu.VMEM)],
        out_specs=pl.BlockSpec(memory_space=pltpu.VMEM),
        scratch_shapes=[pltpu.SemaphoreType.DMA((N_DEV - 1,)), pltpu.SemaphoreType.DMA((N_DEV - 1,)),
                        pltpu.SemaphoreType.DMA],
        compiler_params=pltpu.CompilerParams(vmem_limit_bytes=VMEM_LIMIT),
    )(payload)


def _region(ref, dev, axis, n):
    start = pl.multiple_of(dev * n, LANE if axis == 1 else 16)
    return ref.at[:, pl.ds(start, n)] if axis == 1 else ref.at[pl.ds(start, n), :]


class _Exchange:
    def __init__(self, arrays, out_shapes, sems, start, finish):
        self.arrays, self.out_shapes, self.sems, self.start, self.finish = arrays, out_shapes, sems, start, finish


def _scatter_plan(grads, axes):
    n_w = len(grads)
    lands = []
    for g, ax in zip(grads, axes):
        shp = (g.shape[0], g.shape[1] // N_DEV) if ax == 1 else (g.shape[0] // N_DEV, g.shape[1])
        lands.append(jax.ShapeDtypeStruct((N_DEV - 1,) + shp, BF16))
    widths = [ld.shape[1 + ax] for ld, ax in zip(lands, axes)]

    def copy(w, r, g_refs, l_refs, sems, block, to):
        return pltpu.make_async_remote_copy(
            src_ref=_region(g_refs[w], block, axes[w], widths[w]), dst_ref=l_refs[w].at[r - 1],
            send_sem=sems[0].at[w * (N_DEV - 1) + r - 1], recv_sem=sems[1].at[w * (N_DEV - 1) + r - 1],
            device_id=_dev_coords(to), device_id_type=MESH)

    def start(g_refs, l_refs, sems):
        me = _dev_index()
        for w in range(n_w):
            for r in range(1, N_DEV):
                owner = (me + r) % N_DEV
                copy(w, r, g_refs, l_refs, sems, owner, owner).start()

    def finish(g_refs, l_refs, sems):
        me = _dev_index()
        for w in range(n_w):
            for r in range(1, N_DEV):
                copy(w, r, g_refs, l_refs, sems, me, (me + N_DEV - r) % N_DEV).wait_recv()
        for w in range(n_w):
            for r in range(1, N_DEV):
                copy(w, r, g_refs, l_refs, sems, me, (me + r) % N_DEV).wait_send()

    sems = [pltpu.SemaphoreType.DMA((n_w * (N_DEV - 1),)), pltpu.SemaphoreType.DMA((n_w * (N_DEV - 1),))]
    return _Exchange(list(grads), lands, sems, start, finish)


def _places():
    x, y, c = lax.axis_index("x"), lax.axis_index("y"), lax.axis_index("c")
    return (x, y, c), (x, y, 1 - c), [(1 - x, y), (x, 1 - y), (1 - x, 1 - y)]


def _place_index(p):
    return p[0] * 4 + p[1] * 2 + p[2]


def _gather_stage_plans(fulls, axes):
    n_w = len(fulls)
    widths = [f.shape[ax] // N_DEV for f, ax in zip(fulls, axes)]
    shapes = [jax.ShapeDtypeStruct(f.shape, f.dtype) for f in fulls]

    def copy(per, w, k, f_refs, sems, block, to):
        part = _region(f_refs[w], _place_index(block), axes[w], widths[w])
        return pltpu.make_async_remote_copy(
            src_ref=part, dst_ref=part, send_sem=sems[0].at[w * per + k], recv_sem=sems[1].at[w * per + k],
            device_id=to, device_id_type=MESH)

    def start1(_, f_refs, sems):
        me, sib, chips = _places()
        for w in range(n_w):
            copy(4, w, 0, f_refs, sems, me, sib).start()
            for j, chip in enumerate(chips):
                copy(4, w, 1 + j, f_refs, sems, me, (*chip, me[2])).start()

    def finish1(_, f_refs, sems):
        me, sib, chips = _places()
        for w in range(n_w):
            copy(4, w, 0, f_refs, sems, sib, me).wait_recv()
            for j, chip in enumerate(chips):
                copy(4, w, 1 + j, f_refs, sems, (*chip, me[2]), me).wait_recv()
        for w in range(n_w):
            for k in range(4):
                copy(4, w, k, f_refs, sems, me, sib).wait_send()

    def start2(_, f_refs, sems):
        me, sib, chips = _places()
        for w in range(n_w):
            for j, chip in enumerate(chips):
                copy(3, w, j, f_refs, sems, (*chip, me[2]), sib).start()

    def finish2(_, f_refs, sems):
        me, sib, chips = _places()
        for w in range(n_w):
            for j, chip in enumerate(chips):
                copy(3, w, j, f_refs, sems, (*chip, sib[2]), me).wait_recv()
        for w in range(n_w):
            for j, chip in enumerate(chips):
                copy(3, w, j, f_refs, sems, (*chip, me[2]), sib).wait_send()

    sems1 = [pltpu.SemaphoreType.DMA((n_w * 4,)), pltpu.SemaphoreType.DMA((n_w * 4,))]
    sems2 = [pltpu.SemaphoreType.DMA((n_w * 3,)), pltpu.SemaphoreType.DMA((n_w * 3,))]
    return _Exchange([], shapes, sems1, start1, finish1), _Exchange([], shapes, sems2, start2, finish2)


_HBM = pl.BlockSpec(memory_space=pltpu.HBM)
_SEM = pl.BlockSpec(memory_space=pltpu.SEMAPHORE)
_EFFECT = pltpu.SideEffectType.DATAFLOW_SIDE_EFFECTING


def _split_start(name, plan, landing=None):
    n_in, n_out, n_sem = len(plan.arrays), len(plan.out_shapes), len(plan.sems)

    def body(*refs):
        ins, lands = refs[:n_in], refs[n_in:n_in + n_out]
        sems = refs[n_in + n_out:n_in + n_out + n_sem]
        token = refs[-1]
        plan.start(ins, lands, sems)
        token[...] = jnp.zeros_like(token)

    hbm = lambda a: pltpu.HBM(a.shape, a.dtype)
    results = pl.pallas_call(
        body, name=name,
        out_shape=tuple(plan.sems) + tuple(hbm(a) for a in plan.arrays) + tuple(hbm(a) for a in plan.out_shapes)
        + (jax.ShapeDtypeStruct((8, LANE), F32),),
        in_specs=(_HBM,) * (n_in + n_out),
        out_specs=(_SEM,) * n_sem + (_HBM,) * (n_in + n_out) + (pl.BlockSpec(memory_space=pltpu.VMEM),),
        input_output_aliases={i: n_sem + i for i in range(n_in + n_out)},
        compiler_params=pltpu.CompilerParams(has_side_effects=_EFFECT),
    )(*[pltpu.with_memory_space_constraint(a, pltpu.HBM) for a in plan.arrays],
      *[pltpu.with_memory_space_constraint(a, pltpu.HBM)
        for a in (landing if landing is not None else [lax.empty(a.shape, a.dtype) for a in plan.out_shapes])])
    return results[:n_sem], results[n_sem:n_sem + n_in + n_out], results[-1]


def _split_wait(name, plan, sems, thru, after):
    n_in, n_out, n_sem = len(plan.arrays), len(plan.out_shapes), len(plan.sems)

    def body(*refs):
        ins, lands = refs[:n_in], refs[n_in:n_in + n_out]
        sem_refs = refs[n_in + n_out:n_in + n_out + n_sem]
        plan.finish(ins, lands, sem_refs)

    hbm = lambda a: pltpu.HBM(a.shape, a.dtype)
    results = pl.pallas_call(
        body, name=name,
        out_shape=tuple(hbm(a) for a in plan.arrays) + tuple(hbm(a) for a in plan.out_shapes),
        in_specs=(_HBM,) * (n_in + n_out) + (_SEM,) * n_sem + (pl.BlockSpec(memory_space=pl.ANY),),
        out_specs=(_HBM,) * (n_in + n_out),
        input_output_aliases={i: i for i in range(n_in + n_out)},
        compiler_params=pltpu.CompilerParams(has_side_effects=_EFFECT),
    )(*thru, *sems, after)
    return results[:n_in], results[n_in:]


def _cast_into_full(name, me, w, axis):
    r, c = w.shape
    tr = _tile(r, 256)
    if axis == 1:
        shape, place = (r, c * N_DEV), pl.BlockSpec((tr, c), lambda i, me_ref: (i, me_ref[0]))
    else:
        shape, place = (r * N_DEV, c), pl.BlockSpec((tr, c), lambda i, me_ref: (me_ref[0] * (r // tr) + i, 0))

    def body(me_ref, w_ref, o_ref):
        o_ref[...] = w_ref[...].astype(BF16)

    grid_spec = pltpu.PrefetchScalarGridSpec(
        num_scalar_prefetch=1, grid=(r // tr,),
        in_specs=[pl.BlockSpec((tr, c), lambda i, me_ref: (i, 0))], out_specs=place)
    return pl.pallas_call(body, name=name, grid_spec=grid_spec, out_shape=jax.ShapeDtypeStruct(shape, BF16),
                          compiler_params=_params(1))(me, w)


def _mm(name, a, b, dims, m, n, k, tm, tn, tk, extras, outs, epilogue, row_chunk=None, a_col_block=0):
    ni, nj, nk = m // tm, n // tn, k // tk
    ne, no = len(extras), len(outs)
    if dims == _TN:
        a_spec = pl.BlockSpec((tk, tm), lambda i, j, kk: (kk, i + a_col_block))
    else:
        a_spec = pl.BlockSpec((tm, tk), lambda i, j, kk: (i, kk))
    if dims == _NT:
        b_spec = pl.BlockSpec((tn, tk), lambda i, j, kk: (j, kk))
    else:
        b_spec = pl.BlockSpec((tk, tn), lambda i, j, kk: (kk, j))
    chunks = [slice(None)] if row_chunk is None else [slice(r, r + row_chunk) for r in range(0, tm, row_chunk)]

    def lift(index_map):
        return lambda i, j, kk: index_map(i, j)

    def body(a_ref, b_ref, *rest):
        extra_refs, out_refs, rest = rest[:ne], rest[ne:ne + no], rest[ne + no:]
        i, j, kk = pl.program_id(0), pl.program_id(1), pl.program_id(2)
        if nk == 1:
            part = _dot(a_ref[...], b_ref[...], dims)
            for rows in chunks:
                epilogue(part[rows], i, j, extra_refs, out_refs, rows)
        else:
            acc_ref = rest[0]

            @pl.when(kk == 0)
            def _():
                acc_ref[...] = _dot(a_ref[...], b_ref[...], dims)

            @pl.when(kk > 0)
            def _():
                acc_ref[...] += _dot(a_ref[...], b_ref[...], dims)

            @pl.when(kk == nk - 1)
            def _():
                for rows in chunks:
                    epilogue(acc_ref[rows, :], i, j, extra_refs, out_refs, rows)

    once = dict(pipeline_mode=pl.Buffered(1)) if (row_chunk is not None and nk > 1) else {}
    return pl.pallas_call(
        body, name=name,
        grid=(ni, nj, nk),
        in_specs=[a_spec, b_spec] + [pl.BlockSpec(bs, lift(im), **once) for _, bs, im in extras],
        out_specs=[pl.BlockSpec(bs, lift(im), **once) for _, bs, im in outs],
        out_shape=[sd for sd, _, _ in outs],
        scratch_shapes=[pltpu.VMEM((tm, tn), F32)] if nk > 1 else [],
        compiler_params=_params(3),
    )(a, b, *[arr for arr, _, _ in extras])


def _after(token):
    return [(token, (8, LANE), lambda i, j: (0, 0))]


def _grad_w(name, a, dc, token=None, tm=512, tn=1024, rows=None):
    t = a.shape[0]
    n = dc.shape[1]
    first, m = rows if rows is not None else (0, a.shape[1])
    tm, tn = _tile(m, tm), _tile(n, tn)
    assert first % tm == 0

    def epilogue(acc, i, j, extra_refs, out_refs, rows):
        out_refs[0][...] = acc
        out_refs[1][...] = acc.astype(BF16)

    blk = ((tm, tn), lambda i, j: (i, j))
    return _mm(name, a, dc, _TN, m, n, t, tm, tn, t, _after(token) if token is not None else [],
               [(jax.ShapeDtypeStruct((m, n), F32),) + blk, (jax.ShapeDtypeStruct((m, n), BF16),) + blk], epilogue,
               a_col_block=first // tm)


def _proj_gather(a1, w_shard, order):
    t, d = a1.shape
    nsh = w_shard.shape[1]
    tm = _tile(t, 512)
    n_i = t // tm

    def body(ord_ref, a_ref, wsh_ref, proj_ref, full_ref, bbuf, bsem, send_sems, recv_sems, own_sem):
        s, i = pl.program_id(0), pl.program_id(1)
        me, sib, chips = _places()
        near, far = chips[:2], chips[2]
        steps = ([(me, None, None), (sib, 0, None)]
                 + [((*ch, me[2]), 1 + j, 4 + j) for j, ch in enumerate(near)]
                 + [((*ch, sib[2]), 4 + j, None) for j, ch in enumerate(near)]
                 + [((*far, me[2]), 3, 6), ((*far, sib[2]), 6, None)])
        blocks = [st[0] for st in steps]

        def part(block):
            return _region(full_ref, _place_index(block), 1, nsh)

        def remote(k, block, to, from_shard=False):
            return pltpu.make_async_remote_copy(
                src_ref=wsh_ref if from_shard else part(block), dst_ref=part(block),
                send_sem=send_sems.at[k], recv_sem=recv_sems.at[k], device_id=to, device_id_type=MESH)

        def load(pos):
            src = wsh_ref if pos == 0 else part(blocks[pos])
            return pltpu.make_async_copy(src, bbuf.at[pos % 2], bsem.at[pos % 2])

        own = pltpu.make_async_copy(wsh_ref, part(me), own_sem)

        @pl.when((s == 0) & (i == 0))
        def _():
            own.start()
            remote(0, me, sib, True).start()
            for j, ch in enumerate(chips):
                remote(1 + j, me, (*ch, me[2]), True).start()
            load(0).start()
            load(0).wait()

        for pos in range(1, N_DEV):
            @pl.when((s == pos) & (i == 0))
            def _():
                load(pos).wait()

        for pos in range(N_DEV - 1):
            @pl.when((s == pos) & (i == n_i - 1))
            def _():
                nxt = pos + 1
                block, arrives_on, pass_on_with = steps[nxt]
                remote(arrives_on, block, me).wait_recv()
                if pass_on_with is not None:
                    remote(pass_on_with, block, sib).start()
                load(nxt).start()

        proj_ref[...] = _dot(a_ref[...], bbuf[s % 2])

        @pl.when((s == N_DEV - 1) & (i == n_i - 1))
        def _():
            for k in range(N_DEV - 1):
                remote(k, me, sib, True).wait_send()
            own.wait()

    grid_spec = pltpu.PrefetchScalarGridSpec(
        num_scalar_prefetch=1, grid=(N_DEV, n_i),
        in_specs=[pl.BlockSpec((tm, d), lambda s, i, ord_ref: (i, 0)), pl.BlockSpec(memory_space=pl.ANY)],
        out_specs=[pl.BlockSpec((tm, nsh), lambda s, i, ord_ref: (i, ord_ref[s])), pl.BlockSpec(memory_space=pl.ANY)],
        scratch_shapes=[pltpu.VMEM((2, d, nsh), BF16), pltpu.SemaphoreType.DMA((2,)),
                        pltpu.SemaphoreType.DMA((N_DEV - 1,)), pltpu.SemaphoreType.DMA((N_DEV - 1,)),
                        pltpu.SemaphoreType.DMA])
    return pl.pallas_call(
        body, name="proj_gather", grid_spec=grid_spec,
        out_shape=[jax.ShapeDtypeStruct((t, nsh * N_DEV), F32), jax.ShapeDtypeStruct((d, nsh * N_DEV), BF16)],
        compiler_params=_params(2),
    )(order, a1, w_shard)


def _cast_bf16(name, w):
    r, c = w.shape
    tr = _tile(r, 256)
    return pl.pallas_call(
        lambda w_ref, o_ref: o_ref.__setitem__(Ellipsis, w_ref[...].astype(BF16)), name=name,
        grid=(r // tr,), in_specs=[pl.BlockSpec((tr, c), lambda i: (i, 0))],
        out_specs=pl.BlockSpec((tr, c), lambda i: (i, 0)), out_shape=jax.ShapeDtypeStruct((r, c), BF16),
        compiler_params=_params(1),
    )(w)


def _prep_small(c_row, lb_logits):
    d = c_row.shape[1]
    rows = d // LANE

    def body(c_ref, l_ref, o_ref):
        cv = c_ref[...]
        o_ref[0:rows, :] = cv * _sigmoid(cv)
        lbs = [_sigmoid(l_ref[dr][0:1, :] - l_ref[dr][1:2, :]) for dr in range(2)]
        o_ref[rows:rows + 8, :] = jnp.concatenate(lbs + [jnp.zeros((6, LANE), F32)], axis=0)

    return pl.pallas_call(
        body, name="prep_small", out_shape=jax.ShapeDtypeStruct((rows + 8, LANE), F32),
    )(c_row.reshape(rows, LANE), lb_logits)


def _mod_shard(sc_all, w_ada_shard, b_shard):
    d, n = w_ada_shard.shape
    tn = _tile(n, 512)

    def body(s_ref, w_ref, b_ref, o_ref):
        o_ref[...] = _dot(s_ref[...], w_ref[...], precision=HIGHEST) + b_ref[...]

    return pl.pallas_call(
        body, name="mod_shard", grid=(n // tn,),
        in_specs=[pl.BlockSpec((N_DEV, d), lambda j: (0, 0)), pl.BlockSpec((d, tn), lambda j: (0, j)),
                  pl.BlockSpec((1, tn), lambda j: (0, j))],
        out_specs=pl.BlockSpec((N_DEV, tn), lambda j: (0, j)),
        out_shape=jax.ShapeDtypeStruct((N_DEV, n), F32), compiler_params=_params(1),
    )(sc_all, w_ada_shard, b_shard)


def _norm_mod(x, gain, shift, scale):
    t, d = x.shape
    tm = _tile(t, 512)

    def body(x_ref, g_ref, sh_ref, sc_ref, o_ref):
        xv = x_ref[...]
        o_ref[...] = ((xv * _rms(xv) * g_ref[...]) * (1.0 + sc_ref[...]) + sh_ref[...]).astype(BF16)

    vec = pl.BlockSpec((1, d), lambda i: (0, 0))
    return pl.pallas_call(
        body, name="norm_mod", grid=(t // tm,),
        in_specs=[pl.BlockSpec((tm, d), lambda i: (i, 0)), vec, vec, vec],
        out_specs=pl.BlockSpec((tm, d), lambda i: (i, 0)), out_shape=jax.ShapeDtypeStruct((t, d), BF16),
        compiler_params=_params(1),
    )(x, gain, shift, scale)


def _chunk_masks():
    row = lax.broadcasted_iota(jnp.int32, (HEAD, HEAD), 0)
    col = lax.broadcasted_iota(jnp.int32, (HEAD, HEAD), 1)
    same = (row // A_CHUNK) == (col // A_CHUNK)
    return same & (col <= row), same & (col >= row)


def _ones(mask):
    return jnp.where(mask, 1.0, 0.0).astype(BF16)


def _dot_split(ones_bf16, x):
    hi = x.astype(BF16)
    lo = (x - hi.astype(F32)).astype(BF16)
    return _dot(ones_bf16, hi) + _dot(ones_bf16, lo)


def _hgrn_block(direction, f, lb, cum2):
    sf = _sigmoid(f)
    big_f = lb + (1.0 - lb) * sf
    k = (1.0 - lb) * (1.0 - sf)
    lf = jnp.log(big_f)
    both = _dot_split(cum2, lf)
    cf, cr = both[:HEAD], both[HEAD:]
    b, rest = (cf, cr - lf) if direction == 0 else (cr, cf - lf)
    return k, sf, big_f, jnp.exp(b), jnp.exp(-b), jnp.exp(rest)


def _hgrn_fwd(proj, lb, g_norm, width, after):
    t = proj.shape[0]
    heads = width // HEAD
    nb, nc = t // HEAD, t // A_CHUNK
    ua = 4 if nb % 4 == 0 else (2 if nb % 2 == 0 else 1)
    ub = 16 if nc % 16 == 0 else (8 if nc % 8 == 0 else 4)
    q_scale = HEAD ** -0.5

    def body(q_ref, ffw_ref, fbw_ref, v_ref, og_ref, lb_ref, g_ref, *rest):
        outa_ref, osum_ref, qd_s, ke_s, dc_s, o_s = rest[len(after):]
        tril, triu = _chunk_masks()
        cum2 = jnp.concatenate([_ones(tril), _ones(triu)], axis=0)
        f_refs = (ffw_ref, fbw_ref)
        lbs = (lb_ref[0:1, :], lb_ref[1:2, :])

        def phase_a(it, carry):
            loaded = []
            for u in range(ua):
                rows = pl.ds(pl.multiple_of((it * ua + u) * HEAD, HEAD), HEAD)
                loaded.append((rows, q_ref[rows, :], v_ref[rows, :], ffw_ref[rows, :], fbw_ref[rows, :]))
            chains = [(d, rows, qv * q_scale, vv.astype(BF16), fv)
                      for rows, qv, vv, f0, f1 in loaded for d, fv in ((0, f0), (1, f1))]
            blocks = [_hgrn_block(d, fv, lbs[d], cum2) for d, _, _, _, fv in chains]
            scaled = [(qv * eb, k * enb, k * erest, eb * erest)
                      for (_, _, qv, _, _), (k, _, _, eb, enb, erest) in zip(chains, blocks)]
            atts = [jnp.where(tril if d == 0 else triu, _bdot(qd, kd, _NT), 0.0)
                    for (d, _, _, _, _), (qd, kd, _, _) in zip(chains, scaled)]
            intras = [_bdot(att, vv) for att, (_, _, _, vv, _) in zip(atts, chains)]
            results = [(d, rows, o_intra, qd.astype(BF16), ke.astype(BF16), decay)
                       for (d, rows, _, _, _), (qd, _, ke, decay), o_intra in zip(chains, scaled, intras)]
            for d, rows, o_intra, qd16, ke16, decay in results:
                o_s[d, rows, :] = o_intra
                qd_s[d, rows, :] = qd16
                ke_s[d, rows, :] = ke16
                dc_s[d, rows, :] = decay
            return carry

        lax.fori_loop(0, nb // ua, phase_a, 0)

        def phase_b(it, states):
            loaded = []
            for u in range(ub):
                n = it * ub + u
                for d in range(2):
                    c = n if d == 0 else nc - 1 - n
                    start = pl.multiple_of(c * A_CHUNK, A_CHUNK)
                    rows = pl.ds(start, A_CHUNK)
                    loaded.append((d, rows, qd_s[d, rows, :], ke_s[d, rows, :], v_ref[rows, :],
                                   dc_s[d, pl.ds(start, 1), :], o_s[d, rows, :]))
            increments = [_dot(vv.astype(BF16), ke16, _TN) for _, _, _, ke16, vv, _, _ in loaded]
            states = list(states)
            befores = []
            for (d, _, _, _, _, decay, _), inc in zip(loaded, increments):
                befores.append(states[d].astype(BF16))
                states[d] = states[d] * decay + inc
            inters = [_dot(qd16, before, _NT) for (_, _, qd16, _, _, _, _), before in zip(loaded, befores)]
            for (d, rows, _, _, _, _, o_intra), o_inter in zip(loaded, inters):
                o_s[d, rows, :] = o_intra + o_inter
            return tuple(states)

        zero_state = jnp.zeros((HEAD, HEAD), F32)
        lax.fori_loop(0, nc // ub, phase_b, (zero_state, zero_state))

        def phase_c(i, carry):
            rows = pl.ds(pl.multiple_of(i * HEAD, HEAD), HEAD)
            o = o_s[0, rows, :] + o_s[1, rows, :]
            osum_ref[rows, :] = o
            og = og_ref[rows, :]
            outa_ref[rows, :] = (o * _rms(o) * g_ref[...] * (og * _sigmoid(og))).astype(BF16)
            return carry

        lax.fori_loop(0, nb, phase_c, 0)

    def col(p):
        return pl.BlockSpec((t, HEAD), lambda h: (0, p * heads + h))

    return pl.pallas_call(
        body, name="hgrn_fwd", grid=(heads,),
        in_specs=[col(0), col(1), col(2), col(3), col(4),
                  pl.BlockSpec((2, HEAD), lambda h: (0, h)), pl.BlockSpec((1, HEAD), lambda h: (0, 0))]
        + [pl.BlockSpec(memory_space=pl.ANY)] * len(after),
        out_specs=[pl.BlockSpec((t, HEAD), lambda h: (0, h)), pl.BlockSpec((t, HEAD), lambda h: (0, h))],
        out_shape=[jax.ShapeDtypeStruct((t, width), BF16), jax.ShapeDtypeStruct((t, width), F32)],
        scratch_shapes=[pltpu.VMEM((2, t, HEAD), BF16), pltpu.VMEM((2, t, HEAD), BF16), pltpu.VMEM((2, t, HEAD), F32),
                        pltpu.VMEM((2, t, HEAD), F32)],
        compiler_params=_params(1),
    )(proj, proj, proj, proj, proj, lb, g_norm, *after)


def _sgu_core(u_pre, v_pre, g_v, ws_ref, bst):
    u, du = _gelu_and_grad(u_pre)
    v, dv = _gelu_and_grad(v_pre)
    mu = jnp.mean(v, axis=-1, keepdims=True)
    dlt = v - mu
    rstd = lax.rsqrt(jnp.mean(dlt * dlt, axis=-1, keepdims=True) + EPS)
    vhat = dlt * rstd
    vn = vhat * g_v
    groups = vn.shape[1] // HEAD
    cols = []
    for g in range(groups):
        vm_g = _bdot(ws_ref[g], vn[:, g * HEAD:(g + 1) * HEAD]) + bst[:, g:g + 1]
        cols.append(vm_g)
    return u, du, dv, vhat, rstd, vn, jnp.concatenate(cols, axis=1)


def _sgu_fwd(proj, g_v, w_s, bst, width, z_block):
    t = proj.shape[0]

    def body(u_ref, v_ref, g_ref, ws_ref, bst_ref, o_ref):
        u, _, _, _, _, _, vm = _sgu_core(u_ref[...], v_ref[...], g_ref[...], ws_ref, bst_ref[...])
        o_ref[...] = (u * vm).astype(BF16)

    groups = width // HEAD
    return pl.pallas_call(
        body, name="sgu_fwd", grid=(t // HEAD,),
        in_specs=[pl.BlockSpec((HEAD, width), lambda i: (i, z_block)), pl.BlockSpec((HEAD, width), lambda i: (i, z_block + 1)),
                  pl.BlockSpec((1, width), lambda i: (0, 0)), pl.BlockSpec((groups, HEAD, HEAD), lambda i: (0, 0, 0)),
                  pl.BlockSpec((HEAD, groups), lambda i: (0, 0))],
        out_specs=pl.BlockSpec((HEAD, width), lambda i: (i, 0)),
        out_shape=jax.ShapeDtypeStruct((t, width), BF16), compiler_params=_params(1),
    )(proj, proj, g_v, w_s, bst)


def _sgu_bwd(proj, dout_b, dproj, g_v, w_s, w_st, bst, width, z_block):
    t = proj.shape[0]
    groups = width // HEAD
    nblk = t // HEAD

    def body(u_ref, v_ref, do_ref, g_ref, ws_ref, wst_ref, bst_ref, dproj_hbm,
             dz_ref, dg_ref, dws_ref, dbst_ref, res_s):
        i, p = pl.program_id(0), pl.program_id(1)

        @pl.when((i == 0) & (p == 0))
        def _():
            dg_ref[...] = jnp.zeros_like(dg_ref)
            dws_ref[...] = jnp.zeros_like(dws_ref)
            dbst_ref[...] = jnp.zeros_like(dbst_ref)

        @pl.when(p == 0)
        def _():
            g_v = g_ref[...]
            u, du, dv, vhat, rstd, vn, vm = _sgu_core(u_ref[...], v_ref[...], g_v, ws_ref, bst_ref[...])
            dout = do_ref[...].astype(F32)
            res_s[0] = (dout * vm * du).astype(BF16)
            dvm = dout * u
            dvn_cols = []
            for g in range(groups):
                sl = slice(g * HEAD, (g + 1) * HEAD)
                dvm_g = dvm[:, sl]
                dbst_ref[:, g:g + 1] += jnp.sum(dvm_g, axis=1, keepdims=True)
                dws_ref[g] += _bdot(dvm_g, vn[:, sl], _NT)
                dvn_cols.append(_bdot(wst_ref[g], dvm_g))
            dvn = jnp.concatenate(dvn_cols, axis=1)
            dg_ref[...] += _colsum(dvn * vhat)
            dvh = dvn * g_v
            dvg = rstd * (dvh - jnp.mean(dvh, axis=-1, keepdims=True)
                          - vhat * jnp.mean(dvh * vhat, axis=-1, keepdims=True))
            res_s[1] = (dvg * dv).astype(BF16)

        dz_ref[...] = res_s[p]

    n_in = dproj.shape[1]
    return pl.pallas_call(
        body, name="sgu_bwd", grid=(nblk, 2),
        in_specs=[pl.BlockSpec((HEAD, width), lambda i, p: (i, z_block)),
                  pl.BlockSpec((HEAD, width), lambda i, p: (i, z_block + 1)),
                  pl.BlockSpec((HEAD, width), lambda i, p: (i, 0)),
                  pl.BlockSpec((1, width), lambda i, p: (0, 0)),
                  pl.BlockSpec((groups, HEAD, HEAD), lambda i, p: (0, 0, 0)),
                  pl.BlockSpec((groups, HEAD, HEAD), lambda i, p: (0, 0, 0)),
                  pl.BlockSpec((HEAD, groups), lambda i, p: (0, 0)),
                  pl.BlockSpec(memory_space=pl.ANY)],
        out_specs=[pl.BlockSpec((HEAD, width), lambda i, p: (i, z_block + p)),
                   pl.BlockSpec((1, width), lambda i, p: (0, 0)),
                   pl.BlockSpec((groups, HEAD, HEAD), lambda i, p: (0, 0, 0)),
                   pl.BlockSpec((HEAD, groups), lambda i, p: (0, 0))],
        out_shape=[jax.ShapeDtypeStruct((t, n_in), BF16), jax.ShapeDtypeStruct((1, width), F32),
                   jax.ShapeDtypeStruct((groups, HEAD, HEAD), F32), jax.ShapeDtypeStruct((HEAD, groups), F32)],
        scratch_shapes=[pltpu.VMEM((2, HEAD, width), BF16)],
        input_output_aliases={7: 0},
        compiler_params=_params(2),
    )(proj, proj, dout_b, g_v, w_s, w_st, bst, dproj)


def _hgrn_bwd(proj, osum, dout_a, dproj, lb, g_norm, width, after):
    t = proj.shape[0]
    heads = width // HEAD
    nb = t // HEAD
    cpb = HEAD // A_CHUNK
    ubk = 4 if nb % 4 == 0 else (2 if nb % 2 == 0 else 1)
    q_scale = HEAD ** -0.5

    def body(q_ref, ffw_ref, fbw_ref, v_ref, og_ref, osum_ref, douta_ref, lb_ref, g_ref, dproj_hbm, after_hbm,
             out_ref, dgh_ref, dlb_ref, do_s, dq_s, dv_s, res_s, ck_s):
        p = pl.program_id(1)
        f_refs = (ffw_ref, fbw_ref)

        @pl.when(p == 0)
        def _():
            tril, triu = _chunk_masks()
            cum2 = jnp.concatenate([_ones(tril), _ones(triu)], axis=0)
            g_row = g_ref[...]

            def pass_norm(i, dgh):
                rows = pl.ds(pl.multiple_of(i * HEAD, HEAD), HEAD)
                o = osum_ref[rows, :]
                r = _rms(o)
                oh = o * r
                og = og_ref[rows, :]
                sg = _sigmoid(og)
                dout = douta_ref[rows, :].astype(F32)
                don = dout * (og * sg)
                res_s[4, rows, :] = (dout * (oh * g_row) * (sg * (1.0 + og * (1.0 - sg)))).astype(BF16)
                doh = don * g_row
                do_s[rows, :] = r * (doh - oh * jnp.mean(doh * oh, axis=-1, keepdims=True))
                return dgh + _colsum(don * oh)

            dgh_ref[...] = lax.fori_loop(0, nb, pass_norm, jnp.zeros((1, HEAD), F32))

            lbs = (lb_ref[0:1, :], lb_ref[1:2, :])
            zero_state = jnp.zeros((HEAD, HEAD), F32)

            def chunk_order(d):
                return list(range(cpb)) if d == 0 else list(range(cpb - 1, -1, -1))

            def chunk(x, j):
                return x[j * A_CHUNK:(j + 1) * A_CHUNK, :]

            def decay_row(e_big, j):
                return e_big[j * A_CHUNK:j * A_CHUNK + 1, :]

            def cat(parts):
                return jnp.concatenate([parts[j] for j in range(cpb)], axis=0)

            def block_states(d, start, incs, e_big):
                befores, st = {}, start
                for j in chunk_order(d):
                    befores[j] = st
                    st = st * decay_row(e_big, j) + incs[j]
                return befores, st

            def pass_states(it, states):
                loaded = []
                for u in range(ubk):
                    for d in range(2):
                        blk = it * ubk + u if d == 0 else nb - 1 - (it * ubk + u)
                        rows = pl.ds(pl.multiple_of(blk * HEAD, HEAD), HEAD)
                        loaded.append((d, blk, f_refs[d][rows, :], v_ref[rows, :]))
                blocks = [_hgrn_block(d, fv, lbs[d], cum2) for d, _, fv, _ in loaded]
                incs = [{j: _bdot(chunk(vv, j), chunk(k * erest, j), _TN) for j in range(cpb)}
                        for (_, _, _, vv), (k, _, _, _, _, erest) in zip(loaded, blocks)]
                states, starts = list(states), []
                for (d, _, _, _), (_, _, _, eb, _, erest), inc in zip(loaded, blocks, incs):
                    starts.append(states[d])
                    states[d] = block_states(d, states[d], inc, eb * erest)[1]
                for (d, blk, _, _), start in zip(loaded, starts):
                    ck_s[d, blk] = start
                return tuple(states)

            lax.fori_loop(0, nb // ubk, pass_states, (zero_state, zero_state))

            def pass_back(it, carry):
                gts, dlb = [carry[0], carry[1]], carry[2]
                loaded = []
                for u, d in ((u, d) for u in range(ubk) for d in range(2)):
                    blk = nb - 1 - (it * ubk + u) if d == 0 else it * ubk + u
                    rows = pl.ds(pl.multiple_of(blk * HEAD, HEAD), HEAD)
                    loaded.append((d, rows, f_refs[d][rows, :], q_ref[rows, :], v_ref[rows, :], do_s[rows, :], ck_s[d, blk]))
                blocks = [_hgrn_block(d, fv, lbs[d], cum2) for d, _, fv, _, _, _, _ in loaded]
                scaled = []
                for (_, _, _, qv, _, _, _), (k, _, _, eb, enb, erest) in zip(loaded, blocks):
                    qh = qv * q_scale
                    scaled.append((qh, qh * eb, k * enb, k * erest, eb * erest))
                masks = [tril if d == 0 else triu for d, *_ in loaded]
                atts = [jnp.where(m, _bdot(qd, kd, _NT), 0.0) for m, (_, qd, kd, _, _) in zip(masks, scaled)]
                datts = [jnp.where(m, _bdot(do, vv, _NT), 0.0) for m, (_, _, _, _, vv, do, _) in zip(masks, loaded)]
                dvs = [_bdot(att, do, _TN) for att, (_, _, _, _, _, do, _) in zip(atts, loaded)]
                dqds = [_bdot(datt, kd) for datt, (_, _, kd, _, _) in zip(datts, scaled)]
                dkds = [_bdot(datt, qd, _TN) for datt, (_, qd, _, _, _) in zip(datts, scaled)]
                s_incs = [{j: _bdot(chunk(vv, j), chunk(ke, j), _TN) for j in range(cpb)}
                          for (_, _, _, _, vv, _, _), (_, _, _, ke, _) in zip(loaded, scaled)]
                g_incs = [{j: _bdot(chunk(do, j), chunk(qd, j), _TN) for j in range(cpb)}
                          for (_, _, _, _, _, do, _), (_, qd, _, _, _) in zip(loaded, scaled)]
                befores, afters, g_at = [], [], []
                for (d, _, _, _, _, _, ck), (_, _, _, _, e_big), s_inc, g_inc in zip(loaded, scaled, s_incs, g_incs):
                    order = chunk_order(d)
                    before, after = block_states(d, ck, s_inc, e_big)
                    befores.append(before)
                    afters.append({j: (before[order[n + 1]] if n + 1 < cpb else after) for n, j in enumerate(order)})
                    at, gt = {}, gts[d]
                    for j in reversed(order):
                        at[j] = gt
                        gt = gt * decay_row(e_big, j) + g_inc[j]
                    gts[d] = gt
                    g_at.append(at)
                dqd_i = [{j: _bdot(chunk(do, j), before[j]) for j in range(cpb)}
                         for (_, _, _, _, _, do, _), before in zip(loaded, befores)]
                dv_i = [{j: _bdot(chunk(ke, j), at[j], _NT) for j in range(cpb)}
                        for (_, _, _, ke, _), at in zip(scaled, g_at)]
                dke = [{j: _bdot(chunk(vv, j), at[j]) for j in range(cpb)}
                       for (_, _, _, _, vv, _, _), at in zip(loaded, g_at)]
                results, new = [], []
                for n, ((d, rows, _, _, _, _, _), (k, sf, big_f, eb, enb, erest), (qh, _, _, _, _)) in enumerate(
                        zip(loaded, blocks, scaled)):
                    dqh = (dqds[n] + cat(dqd_i[n])) * eb
                    dk = dkds[n] * enb + cat(dke[n]) * erest
                    carry_rows = {j: jnp.broadcast_to(_colsum(g_at[n][j] * afters[n][j]), (A_CHUNK, HEAD))
                                  for j in range(cpb)}
                    dlf = _dot_split(_ones(triu if d == 0 else tril), qh * dqh - k * dk) + cat(carry_rows)
                    common = dlf / big_f - dk
                    results.append((d, rows, (k * sf * common).astype(BF16), dqh.astype(BF16),
                                    (dvs[n] + cat(dv_i[n])).astype(BF16)))
                    new.append(_colsum((1.0 - sf) * common))
                for d, rows, df16, dq16, dv16 in results:
                    res_s[1 + d, rows, :] = df16
                    dq_s[d, rows, :] = dq16
                    dv_s[d, rows, :] = dv16
                per_dir = [sum(c for (d, *_), c in zip(loaded, new) if d == dd) for dd in range(2)]
                return gts[0], gts[1], dlb + jnp.concatenate(per_dir, axis=0)

            dlb_ref[...] = lax.fori_loop(0, nb // ubk, pass_back,
                                         (zero_state, zero_state, jnp.zeros((2, HEAD), F32)))[2]

            def pass_out(i, carry):
                rows = pl.ds(pl.multiple_of(i * HEAD, HEAD), HEAD)
                dq = dq_s[0, rows, :].astype(F32) + dq_s[1, rows, :].astype(F32)
                res_s[0, rows, :] = (dq * q_scale).astype(BF16)
                res_s[3, rows, :] = (dv_s[0, rows, :].astype(F32) + dv_s[1, rows, :].astype(F32)).astype(BF16)
                return carry

            lax.fori_loop(0, nb, pass_out, 0)

        out_ref[...] = res_s[p]

    def col(pp):
        return pl.BlockSpec((t, HEAD), lambda h, p: (0, pp * heads + h))

    n_in = dproj.shape[1]
    any_spec = pl.BlockSpec(memory_space=pl.ANY)
    return pl.pallas_call(
        body, name="hgrn_bwd", grid=(heads, 5),
        in_specs=[col(0), col(1), col(2), col(3), col(4),
                  pl.BlockSpec((t, HEAD), lambda h, p: (0, h)), pl.BlockSpec((t, HEAD), lambda h, p: (0, h)),
                  pl.BlockSpec((2, HEAD), lambda h, p: (0, h)), pl.BlockSpec((1, HEAD), lambda h, p: (0, 0)),
                  any_spec, any_spec],
        out_specs=[pl.BlockSpec((t, HEAD), lambda h, p: (0, p * heads + h)),
                   pl.BlockSpec((None, 1, HEAD), lambda h, p: (h, 0, 0)),
                   pl.BlockSpec((2, HEAD), lambda h, p: (0, h))],
        out_shape=[jax.ShapeDtypeStruct((t, n_in), BF16), jax.ShapeDtypeStruct((heads, 1, HEAD), F32),
                   jax.ShapeDtypeStruct((2, width), F32)],
        scratch_shapes=[pltpu.VMEM((t, HEAD), F32), pltpu.VMEM((2, t, HEAD), BF16), pltpu.VMEM((2, t, HEAD), BF16),
                        pltpu.VMEM((5, t, HEAD), BF16), pltpu.VMEM((2, nb, HEAD, HEAD), F32)],
        input_output_aliases={9: 0},
        compiler_params=_params(2),
    )(proj, proj, proj, proj, proj, osum, dout_a, lb, g_norm, dproj, after)


def _adamw(w, g, m, v):
    m = ADAM_B1 * m + (1.0 - ADAM_B1) * g
    v = ADAM_B2 * v + (1.0 - ADAM_B2) * (g * g)
    m_hat = m / (1.0 - ADAM_B1 ** ADAM_STEP)
    v_hat = v / (1.0 - ADAM_B2 ** ADAM_STEP)
    delta = -ADAM_LR * (m_hat / (jnp.sqrt(v_hat) + ADAM_EPS) + ADAM_WD * w)
    return delta, m, v


def _adamw_big(name, me, w, m, v, parts, axis):
    r, c = w.shape
    n_parts = len(parts)
    tr = _tile(r // n_parts, 128)
    per = r // n_parts // tr
    assert axis == 1 or n_parts == 1

    def body(me_ref, w_ref, m_ref, v_ref, *rest):
        g_refs, l_refs = rest[:n_parts], rest[n_parts:2 * n_parts]
        og_ref, od_ref, om_ref, ov_ref = rest[2 * n_parts:]
        g = None
        for p in range(n_parts):
            total = g_refs[p][...]
            for s in range(N_DEV - 1):
                total = total + l_refs[p][s].astype(F32)
            g = total if p == 0 else jnp.where(pl.program_id(0) // per == p, total, g)
        og_ref[...] = g
        od_ref[...], om_ref[...], ov_ref[...] = _adamw(w_ref[...], g, m_ref[...], v_ref[...])

    def within(p, i):
        return jnp.clip(i - p * per, 0, per - 1)

    shard = pl.BlockSpec((tr, c), lambda i, me_ref: (i, 0))
    if axis == 1:
        own = [pl.BlockSpec((tr, c), lambda i, me_ref, p=p: (within(p, i), me_ref[0])) for p in range(n_parts)]
    else:
        own = [pl.BlockSpec((tr, c), lambda i, me_ref: (me_ref[0] * (r // tr) + i, 0))]
    landed = [pl.BlockSpec((N_DEV - 1, tr, c), lambda i, me_ref, p=p: (0, within(p, i), 0)) for p in range(n_parts)]
    grid_spec = pltpu.PrefetchScalarGridSpec(
        num_scalar_prefetch=1, grid=(r // tr,),
        in_specs=[shard, shard, shard] + own + landed, out_specs=[shard] * 4)
    return pl.pallas_call(
        body, name=name, grid_spec=grid_spec, out_shape=[jax.ShapeDtypeStruct((r, c), F32)] * 4,
        compiler_params=_params(1),
    )(me, w, m, v, *[g for g, _ in parts], *[ld for _, ld in parts])


def _adamw_ada(sct, dmod_mine, w, m, v):
    d, n = w.shape
    tr = _tile(d, 256)

    def body(s_ref, dm_ref, w_ref, m_ref, v_ref, og_ref, od_ref, om_ref, ov_ref):
        g = _dot(s_ref[...], dm_ref[...], precision=HIGHEST)
        og_ref[...] = g
        od_ref[...], om_ref[...], ov_ref[...] = _adamw(w_ref[...], g, m_ref[...], v_ref[...])

    blk = pl.BlockSpec((tr, n), lambda i: (i, 0))
    return pl.pallas_call(
        body, name="adamw_ada", grid=(d // tr,),
        in_specs=[pl.BlockSpec((tr, N_DEV), lambda i: (i, 0)), pl.BlockSpec((N_DEV, n), lambda i: (0, 0)), blk, blk, blk],
        out_specs=[blk] * 4, out_shape=[jax.ShapeDtypeStruct((d, n), F32)] * 4, compiler_params=_params(1),
    )(sct, dmod_mine, w, m, v)


def _adamw_small(gathered, w, m, v):
    def body(g_ref, w_ref, m_ref, v_ref, og_ref, od_ref, om_ref, ov_ref):
        g = g_ref[0]
        for s in range(1, N_DEV):
            g = g + g_ref[s]
        og_ref[...] = g
        od_ref[...], om_ref[...], ov_ref[...] = _adamw(w_ref[...], g, m_ref[...], v_ref[...])

    return pl.pallas_call(
        body, name="adamw_small", out_shape=[jax.ShapeDtypeStruct(w.shape, F32)] * 4,
        compiler_params=pltpu.CompilerParams(vmem_limit_bytes=VMEM_LIMIT),
    )(gathered, w, m, v)


def _adamw_lb(dlb_mine, lb_logits, m, v):
    def body(d_ref, l_ref, m_ref, v_ref, og_ref, od_ref, om_ref, ov_ref):
        dlb = d_ref[0]
        for s in range(1, N_DEV):
            dlb = dlb + d_ref[s]
        for dr in range(2):
            lb = _sigmoid(l_ref[dr][0:1, :] - l_ref[dr][1:2, :])
            d0 = dlb[dr:dr + 1] * lb * (1.0 - lb)
            g = jnp.concatenate([d0, -d0], axis=0)
            og_ref[dr] = g
            od_ref[dr], om_ref[dr], ov_ref[dr] = _adamw(l_ref[dr], g, m_ref[dr], v_ref[dr])

    return pl.pallas_call(body, name="adamw_lb", out_shape=[jax.ShapeDtypeStruct(lb_logits.shape, F32)] * 4,
                          )(dlb_mine, lb_logits, m, v)


def _rows(a, pad_to=8):
    flat = a.reshape(-1, LANE)
    pad = (-flat.shape[0]) % pad_to
    return jnp.pad(flat, ((0, pad), (0, 0))) if pad else flat


def kernel(x, c, w_ada, b_ada, g_pre_mix, g_post_mix, g_pre_ffn, g_post_ffn, w_in, lb_logits, g_hgrn_norm, w_a_out, g_sgu_norm, w_spatial, b_spatial, w_b_out, w_o, w_ff1, w_ff2, loss_target, m_w_ada, m_b_ada, m_g_pre_mix, m_g_post_mix, m_g_pre_ffn, m_g_post_ffn, m_w_in, m_lb_logits, m_g_hgrn_norm, m_w_a_out, m_g_sgu_norm, m_w_spatial, m_b_spatial, m_w_b_out, m_w_o, m_w_ff1, m_w_ff2, v_w_ada, v_b_ada, v_g_pre_mix, v_g_post_mix, v_g_pre_ffn, v_g_post_ffn, v_w_in, v_lb_logits, v_g_hgrn_norm, v_w_a_out, v_g_sgu_norm, v_w_spatial, v_b_spatial, v_w_b_out, v_w_o, v_w_ff1, v_w_ff2):
    t, d = x.shape[1], x.shape[2]
    n_in = w_in.shape[2] * N_DEV
    width = (n_in - 2 * d) // 7
    heads = width // HEAD
    assert heads == N_DEV and width % LANE == 0
    d_ff = w_ff1.shape[2] * N_DEV
    n_ada = w_ada.shape[2]
    me = _dev_index()
    me_arr = me.reshape(1).astype(jnp.int32)
    x2, tgt = x[0], loss_target[0]

    big = [w_in[0], w_a_out[0], w_b_out[0], w_o[0], w_ff1[0], w_ff2[0]]
    big_axes = [1, 1, 1, 0, 1, 0]
    big_names = ["w_in", "w_a_out", "w_b_out", "w_o", "w_ff1", "w_ff2"]
    w_in16 = _cast_bf16("cast_w_in", big[0])
    own_parts = [_cast_into_full("cast_" + nm, me_arr, w, ax) for nm, w, ax in zip(big_names[1:], big[1:], big_axes[1:])]

    c_rows = d // LANE
    small = _all_gather_small("gather_c_lb", _prep_small(c[0:1], lb_logits))
    sc_all = small[:, :c_rows, :].reshape(N_DEV, d)
    lb = jnp.transpose(small[:, c_rows:c_rows + 2, :], (1, 0, 2)).reshape(2, width)
    b_shard = lax.dynamic_slice_in_dim(b_ada, me * n_ada, n_ada, axis=1)
    mod_sh = _mod_shard(sc_all, w_ada[0], b_shard)
    mod_all = _all_gather_small("gather_mod", _rows(mod_sh))
    mod_all = mod_all[:, :N_DEV * n_ada // LANE, :].reshape(N_DEV, N_DEV, n_ada)
    mod6 = lax.dynamic_index_in_dim(mod_all, me, axis=1, keepdims=False).reshape(N_MOD, d)
    sh1, sc1, gt1, sh2, sc2, gt2 = [mod6[i:i + 1] for i in range(N_MOD)]

    a1 = _norm_mod(x2, g_pre_mix, sh1, sc1)
    tm = _tile(t, 512)

    def store_bf16(acc, i, j, extra_refs, out_refs, rows):
        out_refs[0][...] = acc.astype(BF16)

    xq, yq, cq = lax.axis_index("x"), lax.axis_index("y"), lax.axis_index("c")
    chips = [(1 - xq, yq), (xq, 1 - yq), (1 - xq, 1 - yq)]
    order = jnp.stack([me, 4 * xq + 2 * yq + 1 - cq]
                      + [4 * a + 2 * b + cq for a, b in chips[:2]] + [4 * a + 2 * b + 1 - cq for a, b in chips[:2]]
                      + [4 * chips[2][0] + 2 * chips[2][1] + cq, 4 * chips[2][0] + 2 * chips[2][1] + 1 - cq]).astype(jnp.int32)
    proj, wf_in = _proj_gather(a1, w_in16, order)

    proj, own_parts = lax.optimization_barrier((proj, own_parts))
    gathers = {}
    for key, lo, hi in (("mid", 1, 4), ("ff1", 4, 5), ("ff2", 5, 6)):
        far, near = _gather_stage_plans(own_parts[lo - 1:hi - 1], big_axes[lo:hi])
        gathers[key] = [far, near, _split_start("gather_%s_start" % key, far, landing=own_parts[lo - 1:hi - 1])]

    def pass_on(key, after):
        far, near, (sems, thru, _) = gathers[key]
        parts = _split_wait("gather_%s_wait" % key, far, sems, thru, after)[1]
        gathers[key].append(_split_start("pass_%s_start" % key, near, landing=list(parts)))
        return gathers[key][3][2]

    def gathered_weights(key, after):
        near, (sems, thru, _) = gathers[key][1], gathers[key][3]
        return _split_wait("pass_%s_wait" % key, near, sems, thru, after)[1]

    out_a, osum = _hgrn_fwd(proj, lb, g_hgrn_norm, width,
                            after=[gathers[key][2][2] for key in ("mid", "ff1", "ff2")])
    passed_mid = pass_on("mid", out_a)
    z_block = 5
    bst = b_spatial[0].T
    out_b = _sgu_fwd(proj, g_sgu_norm, w_spatial[0], bst, width, z_block)
    wf_a, wf_b, wf_o = gathered_weights("mid", out_b)

    tn_d = _tile(d, 1024)
    blk_d = ((tm, tn_d), lambda i, j: (i, j))
    y_a, = _mm("y_a", out_a, wf_a, _NN, t, d, width, tm, tn_d, width, _after(passed_mid),
               [(jax.ShapeDtypeStruct((t, d), BF16),) + blk_d], store_bf16)
    ga_blk = (5 * width + 2 * width) // tn_d
    gb_blk = ga_blk + d // tn_d

    def merge(acc, i, j, extra_refs, out_refs, rows):
        ga, gb, ya = extra_refs
        out_refs[0][...] = acc.astype(BF16)
        out_refs[1][...] = (_sigmoid(ga[...]) * ya[...].astype(F32) + _sigmoid(gb[...]) * acc).astype(BF16)

    y_b, merged = _mm("y_b_merge", out_b, wf_b, _NN, t, d, width, tm, tn_d, width,
                      [(proj, (tm, tn_d), lambda i, j: (i, ga_blk + j)), (proj, (tm, tn_d), lambda i, j: (i, gb_blk + j)),
                       (y_a,) + blk_d],
                      [(jax.ShapeDtypeStruct((t, d), BF16),) + blk_d, (jax.ShapeDtypeStruct((t, d), BF16),) + blk_d], merge)

    tr = _tile(t, 512)
    rc = 32 if tr % 32 == 0 else None
    row_d = ((tr, d), lambda i, j: (i, 0))
    vec_d = ((1, d), lambda i, j: (0, 0))

    passed_ff1 = pass_on("ff1", merged)

    def post_mix(acc, i, j, extra_refs, out_refs, rows):
        x_r, gt1_r, g2_r, g3_r, sc2_r, sh2_r = extra_refs[:6]
        h1 = x_r[rows, :] + gt1_r[...] * (acc * _rms(acc) * g2_r[...])
        out_refs[0][rows, :] = acc.astype(BF16)
        out_refs[1][rows, :] = h1
        out_refs[2][rows, :] = ((h1 * _rms(h1) * g3_r[...]) * (1.0 + sc2_r[...]) + sh2_r[...]).astype(BF16)

    mo, h1, a2 = _mm("w_o_post_mix", merged, wf_o, _NN, t, d, d, tr, d, d,
                     [(x2,) + row_d, (gt1,) + vec_d, (g_post_mix,) + vec_d, (g_pre_ffn,) + vec_d, (sc2,) + vec_d, (sh2,) + vec_d]
                     + _after(passed_ff1),
                     [(jax.ShapeDtypeStruct((t, d), BF16),) + row_d, (jax.ShapeDtypeStruct((t, d), F32),) + row_d,
                      (jax.ShapeDtypeStruct((t, d), BF16),) + row_d], post_mix, row_chunk=rc)

    tn_f = _tile(d_ff, 2048)
    blk_f = ((tm, tn_f), lambda i, j: (i, j))

    def relu_sq(acc, i, j, extra_refs, out_refs, rows):
        r = jnp.maximum(acc, 0.0)
        out_refs[0][...] = acc.astype(BF16)
        out_refs[1][...] = (r * r).astype(BF16)

    wf_1, = gathered_weights("ff1", a2)
    hff, act = _mm(
        "ff1", a2, wf_1, _NN, t, d_ff, d, tm, tn_f, d, [],
        [(jax.ShapeDtypeStruct((t, d_ff), BF16),) + blk_f, (jax.ShapeDtypeStruct((t, d_ff), BF16),) + blk_f], relu_sq)
    pass_on("ff2", hff)
    wf_2, = gathered_weights("ff2", act)

    sums_d = ((8, d), lambda i, j: (0, 0))

    def zero_first(sums_r, i, rows):
        if rows.start in (None, 0):
            @pl.when(i == 0)
            def _():
                sums_r[...] = jnp.zeros_like(sums_r)

    def loss_head(acc, i, j, extra_refs, out_refs, rows):
        h1_r, tgt_r, gt2_r, g4_r = extra_refs
        dy_r, dff_r, sums_r = out_refs
        r4 = _rms(acc)
        ffn = acc * r4
        n4 = ffn * g4_r[...]
        err = h1_r[rows, :] + gt2_r[...] * n4 - tgt_r[rows, :]
        dy = err * (1.0 / d)
        dy_r[rows, :] = dy.astype(BF16)
        dn4 = dy * gt2_r[...]
        dffn = dn4 * g4_r[...]
        dff_r[rows, :] = (r4 * (dffn - ffn * jnp.mean(dffn * ffn, axis=-1, keepdims=True))).astype(BF16)
        zero_first(sums_r, i, rows)

        sums_r[0:1, :] += _colsum(err * err)
        sums_r[1:2, :] += _colsum(dy * n4)
        sums_r[2:3, :] += _colsum(dn4 * ffn)

    tk_f = _tile(d_ff, 1024)
    dy, dff, sums_f = _mm("ff2_loss", act, wf_2, _NN, t, d, d_ff, tr, d, tk_f,
                          [(h1,) + row_d, (tgt,) + row_d, (gt2,) + vec_d, (g_post_ffn,) + vec_d],
                          [(jax.ShapeDtypeStruct((t, d), BF16),) + row_d, (jax.ShapeDtypeStruct((t, d), BF16),) + row_d,
                           (jax.ShapeDtypeStruct((8, d), F32),) + sums_d], loss_head, row_chunk=rc)
    loss_mine = (0.5 / d) * jnp.sum(sums_f[0])

    def relu_sq_bwd(acc, i, j, extra_refs, out_refs, rows):
        out_refs[0][...] = (acc * (2.0 * jnp.maximum(extra_refs[0][...].astype(F32), 0.0))).astype(BF16)

    dhff, = _mm("d_hff", dff, wf_2, _NT, t, d_ff, d, tm, tn_f, d, [(hff,) + blk_f],
                [(jax.ShapeDtypeStruct((t, d_ff), BF16),) + blk_f], relu_sq_bwd)
    scatters = {}

    def send_grads(key, grads16, axes):
        plan = _scatter_plan(grads16, axes)
        scatters[key] = (plan,) + _split_start("scatter_%s_start" % key, plan)
        return scatters[key][3]

    def received_grads(key, after):
        plan, sems, thru, _ = scatters[key]
        return _split_wait("scatter_%s_wait" % key, plan, sems, thru, after)[1]

    gw_ff2, gw_ff2_16 = _grad_w("grad_w_ff2", act, dff)
    sent_ff2 = send_grads("ff2", [gw_ff2_16], big_axes[5:6])
    gw_ff1, gw_ff1_16 = _grad_w("grad_w_ff1", a2, dhff, token=sent_ff2)
    sent_ff1 = send_grads("ff1", [gw_ff1_16], big_axes[4:5])

    def pre_ffn_bwd(acc, i, j, extra_refs, out_refs, rows):
        h1_r, dy_r, mo_r, sc2_r, g3_r, gt1_r, g2_r = extra_refs[:7]
        dh1_r, dmo_r, sums_r = out_refs
        h1v = h1_r[rows, :]
        r3 = _rms(h1v)
        h1n = h1v * r3
        dn3 = acc * (1.0 + sc2_r[...])
        dh1n = dn3 * g3_r[...]
        dh1 = dy_r[rows, :].astype(F32) + r3 * (dh1n - h1n * jnp.mean(dh1n * h1n, axis=-1, keepdims=True))
        dh1_r[rows, :] = dh1.astype(BF16)
        mov = mo_r[rows, :].astype(F32)
        r2 = _rms(mov)
        mon = mov * r2
        dn2 = dh1 * gt1_r[...]
        dmon = dn2 * g2_r[...]
        dmo_r[rows, :] = (r2 * (dmon - mon * jnp.mean(dmon * mon, axis=-1, keepdims=True))).astype(BF16)
        zero_first(sums_r, i, rows)

        sums_r[0:1, :] += _colsum(acc)
        sums_r[1:2, :] += _colsum(acc * (h1n * g3_r[...]))
        sums_r[2:3, :] += _colsum(dn3 * h1n)
        sums_r[3:4, :] += _colsum(dh1 * (mon * g2_r[...]))
        sums_r[4:5, :] += _colsum(dn2 * mon)

    dh1, dmo, sums_m = _mm("d_a2_pre_ffn", dhff, wf_1, _NT, t, d, d_ff, tr, d, tk_f,
                           [(h1,) + row_d, (dy,) + row_d, (mo,) + row_d, (sc2,) + vec_d, (g_pre_ffn,) + vec_d,
                            (gt1,) + vec_d, (g_post_mix,) + vec_d] + _after(sent_ff1),
                           [(jax.ShapeDtypeStruct((t, d), BF16),) + row_d, (jax.ShapeDtypeStruct((t, d), BF16),) + row_d,
                            (jax.ShapeDtypeStruct((8, d), F32),) + sums_d], pre_ffn_bwd, row_chunk=rc)
    gw_o, gw_o_16 = _grad_w("grad_w_o", merged, dmo)

    n_j = d // tn_d

    def merge_bwd_body(dmo_ref, wo_ref, ga_ref, gb_ref, ya_ref, yb_ref, dya_ref, dyb_ref, dproj_ref, acc_s):
        g = pl.program_id(2)

        @pl.when(g == 0)
        def _():
            dm = _dot(dmo_ref[...], wo_ref[...], _NT)
            acc_s[...] = dm
            sa = _sigmoid(ga_ref[...])
            dya_ref[...] = (dm * sa).astype(BF16)
            dproj_ref[...] = (dm * ya_ref[...].astype(F32) * sa * (1.0 - sa)).astype(BF16)

        @pl.when(g == 1)
        def _():
            dm = acc_s[...]
            sb = _sigmoid(gb_ref[...])
            dyb_ref[...] = (dm * sb).astype(BF16)
            dproj_ref[...] = (dm * yb_ref[...].astype(F32) * sb * (1.0 - sb)).astype(BF16)

    tile3 = pl.BlockSpec((tm, tn_d), lambda i, j, g: (i, j))
    dy_a, dy_b, dproj = pl.pallas_call(
        merge_bwd_body, name="d_merged", grid=(t // tm, n_j, 2),
        in_specs=[pl.BlockSpec((tm, d), lambda i, j, g: (i, 0)), pl.BlockSpec((tn_d, d), lambda i, j, g: (j, 0)),
                  pl.BlockSpec((tm, tn_d), lambda i, j, g: (i, ga_blk + j)),
                  pl.BlockSpec((tm, tn_d), lambda i, j, g: (i, gb_blk + j)), tile3, tile3],
        out_specs=[tile3, tile3, pl.BlockSpec((tm, tn_d), lambda i, j, g: (i, ga_blk + g * n_j + j))],
        out_shape=[jax.ShapeDtypeStruct((t, d), BF16), jax.ShapeDtypeStruct((t, d), BF16),
                   jax.ShapeDtypeStruct((t, n_in), BF16)],
        scratch_shapes=[pltpu.VMEM((tm, tn_d), F32)], compiler_params=_params(3),
    )(dmo, wf_o, proj, proj, y_a, y_b)

    tn_w = _tile(width, 1024)
    blk_w = ((tm, tn_w), lambda i, j: (i, j))
    dout_a, = _mm("d_out_a", dy_a, wf_a, _NT, t, width, d, tm, tn_w, d, [],
                  [(jax.ShapeDtypeStruct((t, width), BF16),) + blk_w], store_bf16)
    dout_b, = _mm("d_out_b", dy_b, wf_b, _NT, t, width, d, tm, tn_w, d, [],
                  [(jax.ShapeDtypeStruct((t, width), BF16),) + blk_w], store_bf16)
    gw_a, gw_a_16 = _grad_w("grad_w_a_out", out_a, dy_a)
    gw_b, gw_b_16 = _grad_w("grad_w_b_out", out_b, dy_b)

    w_st = jnp.swapaxes(w_spatial[0], 1, 2)
    dproj, dg_sgu, dw_sp, dbst = _sgu_bwd(proj, dout_b, dproj, g_sgu_norm, w_spatial[0], w_st, bst, width, z_block)
    sent_mid = send_grads("mid", [gw_a_16, gw_b_16, gw_o_16], big_axes[1:4])
    dproj, dgh_heads, dlb = _hgrn_bwd(proj, osum, dout_a, dproj, lb, g_hgrn_norm, width, after=sent_mid)
    gw_in_top, gw_in_top16 = _grad_w("grad_w_in_top", a1, dproj, rows=(0, d // 2))
    sent_top = send_grads("in_top", [gw_in_top16], big_axes[:1])
    gw_in_bot, gw_in_bot16 = _grad_w("grad_w_in_bot", a1, dproj, token=sent_top, rows=(d // 2, d // 2))
    sent_in = send_grads("in_bot", [gw_in_bot16], big_axes[:1])

    def pre_mix_bwd(acc, i, j, extra_refs, out_refs, rows):
        x_r, dh1_r, sc1_r, g1_r = extra_refs[:4]
        dx_r, sums_r = out_refs
        xv = x_r[rows, :]
        r1 = _rms(xv)
        xn = xv * r1
        dn1 = acc * (1.0 + sc1_r[...])
        dxn = dn1 * g1_r[...]
        dx_r[rows, :] = dh1_r[rows, :].astype(F32) + r1 * (dxn - xn * jnp.mean(dxn * xn, axis=-1, keepdims=True))
        zero_first(sums_r, i, rows)

        sums_r[0:1, :] += _colsum(acc)
        sums_r[1:2, :] += _colsum(acc * (xn * g1_r[...]))
        sums_r[2:3, :] += _colsum(dn1 * xn)

    tk_in = _tile(n_in, 1024)
    grad_x, sums_x = _mm(
        "d_a1_pre_mix", dproj, wf_in, _NT, t, d, n_in, tr, d, tk_in,
        [(x2,) + row_d, (dh1,) + row_d, (sc1,) + vec_d, (g_pre_mix,) + vec_d] + _after(sent_in),
        [(jax.ShapeDtypeStruct((t, d), F32),) + row_d, (jax.ShapeDtypeStruct((8, d), F32),) + sums_d],
        pre_mix_bwd, row_chunk=rc)

    dmod = jnp.concatenate([sums_x[0:2], sums_m[3:4], sums_m[0:2], sums_f[1:2]], axis=0).reshape(N_DEV, n_ada // LANE, LANE)
    ada_rows = -(-(n_ada // LANE) // 8) * 8
    dmod = jnp.pad(dmod, ((0, 0), (0, ada_rows - n_ada // LANE), (0, 0))).reshape(N_DEV * ada_rows, LANE)
    parts = [dmod, _rows(sums_x[2:3]), _rows(sums_m[4:5]), _rows(sums_m[2:3]), _rows(sums_f[2:3]),
             _rows(jnp.sum(dgh_heads, axis=0)), _rows(dg_sgu), _rows(dw_sp), _rows(dbst.T)]
    n_params = sum(p.shape[0] for p in parts)
    parts.append(jnp.full((8, LANE), loss_mine, F32))
    n_common = n_params + 8
    payload = jnp.concatenate(parts + [_rows(dlb)], axis=0)

    moms = [m_w_in, m_w_a_out, m_w_b_out, m_w_o, m_w_ff1, m_w_ff2]
    vars_ = [v_w_in, v_w_a_out, v_w_b_out, v_w_o, v_w_ff1, v_w_ff2]
    big_out = {}

    def big_update(nm, parts):
        k = big_names.index(nm)
        outs = _adamw_big("adamw_" + nm, me_arr, big[k], moms[k][0], vars_[k][0], parts, big_axes[k])
        big_out[nm] = [o[None] for o in outs]
        return outs[0]

    land_ff2, = received_grads("ff2", grad_x)
    done = big_update("w_ff2", [(gw_ff2, land_ff2)])
    land_ff1, = received_grads("ff1", done)
    done = big_update("w_ff1", [(gw_ff1, land_ff1)])
    land_a, land_b, land_o = received_grads("mid", done)
    big_update("w_a_out", [(gw_a, land_a)])
    big_update("w_b_out", [(gw_b, land_b)])
    done = big_update("w_o", [(gw_o, land_o)])

    payload, _ = lax.optimization_barrier((payload, done))
    gathered = _all_gather_small("gather_small_grads", payload)

    dmod_mine = lax.dynamic_slice_in_dim(gathered[:, :N_DEV * ada_rows, :].reshape(N_DEV, N_DEV, ada_rows * LANE),
                                         me, 1, axis=1)[:, 0, :n_ada]
    ada_out = [o[None] for o in _adamw_ada(sc_all.T, dmod_mine, w_ada[0], m_w_ada[0], v_w_ada[0])]

    def pack(b_, g1_, g2_, g3_, g4_, gh_, gs_, ws_, bs_):
        b3 = b_.reshape(N_DEV, n_ada // LANE, LANE)
        b3 = jnp.pad(b3, ((0, 0), (0, ada_rows - n_ada // LANE), (0, 0))).reshape(N_DEV * ada_rows, LANE)
        return jnp.concatenate([b3, _rows(g1_), _rows(g2_), _rows(g3_), _rows(g4_), _rows(gh_), _rows(gs_),
                                _rows(ws_), _rows(bs_), jnp.zeros((8, LANE), F32)], axis=0)

    small_w = (b_ada, g_pre_mix, g_post_mix, g_pre_ffn, g_post_ffn, g_hgrn_norm, g_sgu_norm, w_spatial, b_spatial)
    small_m = (m_b_ada, m_g_pre_mix, m_g_post_mix, m_g_pre_ffn, m_g_post_ffn, m_g_hgrn_norm, m_g_sgu_norm, m_w_spatial, m_b_spatial)
    small_v = (v_b_ada, v_g_pre_mix, v_g_post_mix, v_g_pre_ffn, v_g_post_ffn, v_g_hgrn_norm, v_g_sgu_norm, v_w_spatial, v_b_spatial)
    packed = _adamw_small(gathered[:, :n_common, :], pack(*small_w), pack(*small_m), pack(*small_v))

    def unpack(slab):
        outs, at = [], 0
        b3 = slab[:N_DEV * ada_rows].reshape(N_DEV, ada_rows, LANE)[:, :n_ada // LANE, :]
        outs.append(b3.reshape(b_ada.shape))
        at = N_DEV * ada_rows
        for ref in small_w[1:]:
            n_el = ref.size
            n_r = -(-(n_el // LANE) // 8) * 8
            outs.append(slab[at:at + n_el // LANE].reshape(ref.shape))
            at += n_r
        return outs

    small_out = [unpack(s) for s in packed]
    loss = packed[0][n_params, 0]

    dlb_all = gathered[:, n_common:n_common + 2 * heads, :].reshape(N_DEV, 2, heads, LANE)
    dlb_mine = lax.dynamic_index_in_dim(dlb_all, me, axis=2, keepdims=False)
    lb_out = _adamw_lb(dlb_mine, lb_logits, m_lb_logits, v_lb_logits)

    land_top, = received_grads("in_top", ada_out[0])
    land_bot, = received_grads("in_bot", land_top)
    big_update("w_in", [(gw_in_top, land_top), (gw_in_bot, land_bot)])

    order = ["w_ada", "b_ada", "g_pre_mix", "g_post_mix", "g_pre_ffn", "g_post_ffn", "w_in", "lb_logits", "g_hgrn_norm",
             "w_a_out", "g_sgu_norm", "w_spatial", "b_spatial", "w_b_out", "w_o", "w_ff1", "w_ff2"]
    small_names = ["b_ada", "g_pre_mix", "g_post_mix", "g_pre_ffn", "g_post_ffn", "g_hgrn_norm", "g_sgu_norm", "w_spatial", "b_spatial"]

    def leaf(kind, nm):
        if nm == "w_ada":
            return ada_out[kind]
        if nm == "lb_logits":
            return lb_out[kind]
        if nm in big_out:
            return big_out[nm][kind]
        return small_out[kind][small_names.index(nm)]

    result = [loss, grad_x[None]]
    for kind in range(4):
        result += [leaf(kind, nm) for nm in order]
    return tuple(result)
```

```python
import math

import jax
import jax.numpy as jnp
from jax import lax
from jax.experimental import pallas as pl
from jax.experimental.pallas import tpu as pltpu

F32 = jnp.float32
BF16 = jnp.bfloat16
MESH = pl.DeviceIdType.MESH
HIGHEST = lax.Precision.HIGHEST

N_DEV = 8
HEAD = 128
A_CHUNK = 32
N_MOD = 6
EPS = 1e-6
LANE = 128
VMEM_LIMIT = 60 * 1024 * 1024

ADAM_LR = 0.001
ADAM_B1 = 0.9
ADAM_B2 = 0.999
ADAM_EPS = 1e-08
ADAM_WD = 0.01
ADAM_STEP = 10

_NN = (((1,), (0,)), ((), ()))
_NT = (((1,), (1,)), ((), ()))
_TN = (((0,), (0,)), ((), ()))


def _dot(a, b, dims=_NN, precision=None):
    return lax.dot_general(a, b, dims, preferred_element_type=F32, precision=precision)


def _bdot(a, b, dims=_NN):
    return _dot(a.astype(BF16), b.astype(BF16), dims)


def _params(n_grid):
    return pltpu.CompilerParams(dimension_semantics=("arbitrary",) * n_grid, vmem_limit_bytes=VMEM_LIMIT)


def _dev_index():
    return lax.axis_index("x") * 4 + lax.axis_index("y") * 2 + lax.axis_index("c")


def _dev_coords(i):
    return (i // 4, (i // 2) % 2, i % 2)


def _sigmoid(x):
    return 1.0 / (1.0 + jnp.exp(-x))


def _erf(x):
    ax = jnp.abs(x)
    t = 1.0 / (1.0 + 0.3275911 * ax)
    poly = ((((1.061405429 * t - 1.453152027) * t + 1.421413741) * t - 0.284496736) * t + 0.254829592) * t
    y = 1.0 - poly * jnp.exp(-ax * ax)
    return jnp.where(x < 0, -y, y)


def _gelu_and_grad(x):
    cdf = 0.5 * (1.0 + _erf(x * (2.0 ** -0.5)))
    pdf = jnp.exp(-0.5 * x * x) * (1.0 / math.sqrt(2.0 * math.pi))
    return x * cdf, cdf + x * pdf


def _rms(x):
    return lax.rsqrt(jnp.mean(x * x, axis=-1, keepdims=True) + EPS)


def _colsum(x):
    return jnp.sum(x, axis=0, keepdims=True)


def _tile(n, want):
    if n <= want:
        return n
    t = (want // LANE) * LANE
    while n % t:
        t -= LANE
    assert t > 0, (n, want)
    return t


def _all_gather_small(name, payload):
    rows = payload.shape[0]

    def body(p_ref, out_ref, send_sems, recv_sems, local_sem):
        me = _dev_index()
        mine = pltpu.make_async_copy(p_ref, out_ref.at[me], local_sem)
        mine.start()
        sends = []
        for r in range(1, N_DEV):
            peer = (me + r) % N_DEV
            cp = pltpu.make_async_remote_copy(
                src_ref=p_ref, dst_ref=out_ref.at[me], send_sem=send_sems.at[r - 1], recv_sem=recv_sems.at[r - 1],
                device_id=_dev_coords(peer), device_id_type=MESH)
            cp.start()
            sends.append(cp)
        for r in range(1, N_DEV):
            src = (me + N_DEV - r) % N_DEV
            pltpu.make_async_remote_copy(
                src_ref=p_ref, dst_ref=out_ref.at[src], send_sem=send_sems.at[r - 1], recv_sem=recv_sems.at[r - 1],
                device_id=_dev_coords(src), device_id_type=MESH).wait_recv()
        for cp in sends:
            cp.wait_send()
        mine.wait()

    return pl.pallas_call(
        body, name=name,
        out_shape=jax.ShapeDtypeStruct((N_DEV, rows, LANE), F32),
        in_specs=[pl.BlockSpec(memory_space=pltpu.VMEM)],
        out_specs=pl.BlockSpec(memory_space=pltpu.VMEM),
        scratch_shapes=[pltpu.SemaphoreType.DMA((N_DEV - 1,)), pltpu.SemaphoreType.DMA((N_DEV - 1,)),
                        pltpu.SemaphoreType.DMA],
        compiler_params=pltpu.CompilerParams(vmem_limit_bytes=VMEM_LIMIT),
    )(payload)


def _region(ref, dev, axis, n):
    start = pl.multiple_of(dev * n, LANE if axis == 1 else 16)
    return ref.at[:, pl.ds(start, n)] if axis == 1 else ref.at[pl.ds(start, n), :]


class _Exchange:
    def __init__(self, arrays, out_shapes, sems, start, finish):
        self.arrays, self.out_shapes, self.sems, self.start, self.finish = arrays, out_shapes, sems, start, finish


def _scatter_plan(grads, axes):
    n_w = len(grads)
    lands = []
    for g, ax in zip(grads, axes):
        shp = (g.shape[0], g.shape[1] // N_DEV) if ax == 1 else (g.shape[0] // N_DEV, g.shape[1])
        lands.append(jax.ShapeDtypeStruct((N_DEV - 1,) + shp, BF16))
    widths = [ld.shape[1 + ax] for ld, ax in zip(lands, axes)]

    def copy(w, r, g_refs, l_refs, sems, block, to):
        return pltpu.make_async_remote_copy(
            src_ref=_region(g_refs[w], block, axes[w], widths[w]), dst_ref=l_refs[w].at[r - 1],
            send_sem=sems[0].at[w * (N_DEV - 1) + r - 1], recv_sem=sems[1].at[w * (N_DEV - 1) + r - 1],
            device_id=_dev_coords(to), device_id_type=MESH)

    def start(g_refs, l_refs, sems):
        me = _dev_index()
        for w in range(n_w):
            for r in range(1, N_DEV):
                owner = (me + r) % N_DEV
                copy(w, r, g_refs, l_refs, sems, owner, owner).start()

    def finish(g_refs, l_refs, sems):
        me = _dev_index()
        for w in range(n_w):
            for r in range(1, N_DEV):
                copy(w, r, g_refs, l_refs, sems, me, (me + N_DEV - r) % N_DEV).wait_recv()
        for w in range(n_w):
            for r in range(1, N_DEV):
                copy(w, r, g_refs, l_refs, sems, me, (me + r) % N_DEV).wait_send()

    sems = [pltpu.SemaphoreType.DMA((n_w * (N_DEV - 1),)), pltpu.SemaphoreType.DMA((n_w * (N_DEV - 1),))]
    return _Exchange(list(grads), lands, sems, start, finish)


def _places():
    x, y, c = lax.axis_index("x"), lax.axis_index("y"), lax.axis_index("c")
    return (x, y, c), (x, y, 1 - c), [(1 - x, y), (x, 1 - y), (1 - x, 1 - y)]


def _place_index(p):
    return p[0] * 4 + p[1] * 2 + p[2]


def _gather_stage_plans(fulls, axes):
    n_w = len(fulls)
    widths = [f.shape[ax] // N_DEV for f, ax in zip(fulls, axes)]
    shapes = [jax.ShapeDtypeStruct(f.shape, f.dtype) for f in fulls]

    def copy(per, w, k, f_refs, sems, block, to):
        part = _region(f_refs[w], _place_index(block), axes[w], widths[w])
        return pltpu.make_async_remote_copy(
            src_ref=part, dst_ref=part, send_sem=sems[0].at[w * per + k], recv_sem=sems[1].at[w * per + k],
            device_id=to, device_id_type=MESH)

    def start1(_, f_refs, sems):
        me, sib, chips = _places()
        for w in range(n_w):
            copy(4, w, 0, f_refs, sems, me, sib).start()
            for j, chip in enumerate(chips):
                copy(4, w, 1 + j, f_refs, sems, me, (*chip, me[2])).start()

    def finish1(_, f_refs, sems):
        me, sib, chips = _places()
        for w in range(n_w):
            copy(4, w, 0, f_refs, sems, sib, me).wait_recv()
            for j, chip in enumerate(chips):
                copy(4, w, 1 + j, f_refs, sems, (*chip, me[2]), me).wait_recv()
        for w in range(n_w):
            for k in range(4):
                copy(4, w, k, f_refs, sems, me, sib).wait_send()

    def start2(_, f_refs, sems):
        me, sib, chips = _places()
        for w in range(n_w):
            for j, chip in enumerate(chips):
                copy(3, w, j, f_refs, sems, (*chip, me[2]), sib).start()

    def finish2(_, f_refs, sems):
        me, sib, chips = _places()
        for w in range(n_w):
            for j, chip in enumerate(chips):
                copy(3, w, j, f_refs, sems, (*chip, sib[2]), me).wait_recv()
        for w in range(n_w):
            for j, chip in enumerate(chips):
                copy(3, w, j, f_refs, sems, (*chip, me[2]), sib).wait_send()

    sems1 = [pltpu.SemaphoreType.DMA((n_w * 4,)), pltpu.SemaphoreType.DMA((n_w * 4,))]
    sems2 = [pltpu.SemaphoreType.DMA((n_w * 3,)), pltpu.SemaphoreType.DMA((n_w * 3,))]
    return _Exchange([], shapes, sems1, start1, finish1), _Exchange([], shapes, sems2, start2, finish2)


_HBM = pl.BlockSpec(memory_space=pltpu.HBM)
_SEM = pl.BlockSpec(memory_space=pltpu.SEMAPHORE)
_EFFECT = pltpu.SideEffectType.DATAFLOW_SIDE_EFFECTING


def _split_start(name, plan, landing=None):
    n_in, n_out, n_sem = len(plan.arrays), len(plan.out_shapes), len(plan.sems)

    def body(*refs):
        ins, lands = refs[:n_in], refs[n_in:n_in + n_out]
        sems = refs[n_in + n_out:n_in + n_out + n_sem]
        token = refs[-1]
        plan.start(ins, lands, sems)
        token[...] = jnp.zeros_like(token)

    hbm = lambda a: pltpu.HBM(a.shape, a.dtype)
    results = pl.pallas_call(
        body, name=name,
        out_shape=tuple(plan.sems) + tuple(hbm(a) for a in plan.arrays) + tuple(hbm(a) for a in plan.out_shapes)
        + (jax.ShapeDtypeStruct((8, LANE), F32),),
        in_specs=(_HBM,) * (n_in + n_out),
        out_specs=(_SEM,) * n_sem + (_HBM,) * (n_in + n_out) + (pl.BlockSpec(memory_space=pltpu.VMEM),),
        input_output_aliases={i: n_sem + i for i in range(n_in + n_out)},
        compiler_params=pltpu.CompilerParams(has_side_effects=_EFFECT),
    )(*[pltpu.with_memory_space_constraint(a, pltpu.HBM) for a in plan.arrays],
      *[pltpu.with_memory_space_constraint(a, pltpu.HBM)
        for a in (landing if landing is not None else [lax.empty(a.shape, a.dtype) for a in plan.out_shapes])])
    return results[:n_sem], results[n_sem:n_sem + n_in + n_out], results[-1]


def _split_wait(name, plan, sems, thru, after):
    n_in, n_out, n_sem = len(plan.arrays), len(plan.out_shapes), len(plan.sems)

    def body(*refs):
        ins, lands = refs[:n_in], refs[n_in:n_in + n_out]
        sem_refs = refs[n_in + n_out:n_in + n_out + n_sem]
        plan.finish(ins, lands, sem_refs)

    hbm = lambda a: pltpu.HBM(a.shape, a.dtype)
    results = pl.pallas_call(
        body, name=name,
        out_shape=tuple(hbm(a) for a in plan.arrays) + tuple(hbm(a) for a in plan.out_shapes),
        in_specs=(_HBM,) * (n_in + n_out) + (_SEM,) * n_sem + (pl.BlockSpec(memory_space=pl.ANY),),
        out_specs=(_HBM,) * (n_in + n_out),
        input_output_aliases={i: i for i in range(n_in + n_out)},
        compiler_params=pltpu.CompilerParams(has_side_effects=_EFFECT),
    )(*thru, *sems, after)
    return results[:n_in], results[n_in:]


def _cast_into_full(name, me, w, axis):
    r, c = w.shape
    tr = _tile(r, 256)
    if axis == 1:
        shape, place = (r, c * N_DEV), pl.BlockSpec((tr, c), lambda i, me_ref: (i, me_ref[0]))
    else:
        shape, place = (r * N_DEV, c), pl.BlockSpec((tr, c), lambda i, me_ref: (me_ref[0] * (r // tr) + i, 0))

    def body(me_ref, w_ref, o_ref):
        o_ref[...] = w_ref[...].astype(BF16)

    grid_spec = pltpu.PrefetchScalarGridSpec(
        num_scalar_prefetch=1, grid=(r // tr,),
        in_specs=[pl.BlockSpec((tr, c), lambda i, me_ref: (i, 0))], out_specs=place)
    return pl.pallas_call(body, name=name, grid_spec=grid_spec, out_shape=jax.ShapeDtypeStruct(shape, BF16),
                          compiler_params=_params(1))(me, w)


def _mm(name, a, b, dims, m, n, k, tm, tn, tk, extras, outs, epilogue, row_chunk=None, a_col_block=0):
    ni, nj, nk = m // tm, n // tn, k // tk
    ne, no = len(extras), len(outs)
    if dims == _TN:
        a_spec = pl.BlockSpec((tk, tm), lambda i, j, kk: (kk, i + a_col_block))
    else:
        a_spec = pl.BlockSpec((tm, tk), lambda i, j, kk: (i, kk))
    if dims == _NT:
        b_spec = pl.BlockSpec((tn, tk), lambda i, j, kk: (j, kk))
    else:
        b_spec = pl.BlockSpec((tk, tn), lambda i, j, kk: (kk, j))
    chunks = [slice(None)] if row_chunk is None else [slice(r, r + row_chunk) for r in range(0, tm, row_chunk)]

    def lift(index_map):
        return lambda i, j, kk: index_map(i, j)

    def body(a_ref, b_ref, *rest):
        extra_refs, out_refs, rest = rest[:ne], rest[ne:ne + no], rest[ne + no:]
        i, j, kk = pl.program_id(0), pl.program_id(1), pl.program_id(2)
        if nk == 1:
            part = _dot(a_ref[...], b_ref[...], dims)
            for rows in chunks:
                epilogue(part[rows], i, j, extra_refs, out_refs, rows)
        else:
            acc_ref = rest[0]

            @pl.when(kk == 0)
            def _():
                acc_ref[...] = _dot(a_ref[...], b_ref[...], dims)

            @pl.when(kk > 0)
            def _():
                acc_ref[...] += _dot(a_ref[...], b_ref[...], dims)

            @pl.when(kk == nk - 1)
            def _():
                for rows in chunks:
                    epilogue(acc_ref[rows, :], i, j, extra_refs, out_refs, rows)

    once = dict(pipeline_mode=pl.Buffered(1)) if (row_chunk is not None and nk > 1) else {}
    return pl.pallas_call(
        body, name=name,
        grid=(ni, nj, nk),
        in_specs=[a_spec, b_spec] + [pl.BlockSpec(bs, lift(im), **once) for _, bs, im in extras],
        out_specs=[pl.BlockSpec(bs, lift(im), **once) for _, bs, im in outs],
        out_shape=[sd for sd, _, _ in outs],
        scratch_shapes=[pltpu.VMEM((tm, tn), F32)] if nk > 1 else [],
        compiler_params=_params(3),
    )(a, b, *[arr for arr, _, _ in extras])


def _after(token):
    return [(token, (8, LANE), lambda i, j: (0, 0))]


def _grad_w(name, a, dc, token=None, tm=512, tn=1024, rows=None):
    t = a.shape[0]
    n = dc.shape[1]
    first, m = rows if rows is not None else (0, a.shape[1])
    tm, tn = _tile(m, tm), _tile(n, tn)
    assert first % tm == 0

    def epilogue(acc, i, j, extra_refs, out_refs, rows):
        out_refs[0][...] = acc
        out_refs[1][...] = acc.astype(BF16)

    blk = ((tm, tn), lambda i, j: (i, j))
    return _mm(name, a, dc, _TN, m, n, t, tm, tn, t, _after(token) if token is not None else [],
               [(jax.ShapeDtypeStruct((m, n), F32),) + blk, (jax.ShapeDtypeStruct((m, n), BF16),) + blk], epilogue,
               a_col_block=first // tm)


def _proj_gather(a1, w_shard, order):
    t, d = a1.shape
    nsh = w_shard.shape[1]
    tm = _tile(t, 512)
    n_i = t // tm

    def body(ord_ref, a_ref, wsh_ref, proj_ref, full_ref, bbuf, bsem, send_sems, recv_sems, own_sem):
        s, i = pl.program_id(0), pl.program_id(1)
        me, sib, chips = _places()
        near, far = chips[:2], chips[2]
        steps = ([(me, None, None), (sib, 0, None)]
                 + [((*ch, me[2]), 1 + j, 4 + j) for j, ch in enumerate(near)]
                 + [((*ch, sib[2]), 4 + j, None) for j, ch in enumerate(near)]
                 + [((*far, me[2]), 3, 6), ((*far, sib[2]), 6, None)])
        blocks = [st[0] for st in steps]

        def part(block):
            return _region(full_ref, _place_index(block), 1, nsh)

        def remote(k, block, to, from_shard=False):
            return pltpu.make_async_remote_copy(
                src_ref=wsh_ref if from_shard else part(block), dst_ref=part(block),
                send_sem=send_sems.at[k], recv_sem=recv_sems.at[k], device_id=to, device_id_type=MESH)

        def load(pos):
            src = wsh_ref if pos == 0 else part(blocks[pos])
            return pltpu.make_async_copy(src, bbuf.at[pos % 2], bsem.at[pos % 2])

        own = pltpu.make_async_copy(wsh_ref, part(me), own_sem)

        @pl.when((s == 0) & (i == 0))
        def _():
            own.start()
            remote(0, me, sib, True).start()
            for j, ch in enumerate(chips):
                remote(1 + j, me, (*ch, me[2]), True).start()
            load(0).start()
            load(0).wait()

        for pos in range(1, N_DEV):
            @pl.when((s == pos) & (i == 0))
            def _():
                load(pos).wait()

        for pos in range(N_DEV - 1):
            @pl.when((s == pos) & (i == n_i - 1))
            def _():
                nxt = pos + 1
                block, arrives_on, pass_on_with = steps[nxt]
                remote(arrives_on, block, me).wait_recv()
                if pass_on_with is not None:
                    remote(pass_on_with, block, sib).start()
                load(nxt).start()

        proj_ref[...] = _dot(a_ref[...], bbuf[s % 2])

        @pl.when((s == N_DEV - 1) & (i == n_i - 1))
        def _():
            for k in range(N_DEV - 1):
                remote(k, me, sib, True).wait_send()
            own.wait()

    grid_spec = pltpu.PrefetchScalarGridSpec(
        num_scalar_prefetch=1, grid=(N_DEV, n_i),
        in_specs=[pl.BlockSpec((tm, d), lambda s, i, ord_ref: (i, 0)), pl.BlockSpec(memory_space=pl.ANY)],
        out_specs=[pl.BlockSpec((tm, nsh), lambda s, i, ord_ref: (i, ord_ref[s])), pl.BlockSpec(memory_space=pl.ANY)],
        scratch_shapes=[pltpu.VMEM((2, d, nsh), BF16), pltpu.SemaphoreType.DMA((2,)),
                        pltpu.SemaphoreType.DMA((N_DEV - 1,)), pltpu.SemaphoreType.DMA((N_DEV - 1,)),
                        pltpu.SemaphoreType.DMA])
    return pl.pallas_call(
        body, name="proj_gather", grid_spec=grid_spec,
        out_shape=[jax.ShapeDtypeStruct((t, nsh * N_DEV), F32), jax.ShapeDtypeStruct((d, nsh * N_DEV), BF16)],
        compiler_params=_params(2),
    )(order, a1, w_shard)


def _cast_bf16(name, w):
    r, c = w.shape
    tr = _tile(r, 256)
    return pl.pallas_call(
        lambda w_ref, o_ref: o_ref.__setitem__(Ellipsis, w_ref[...].astype(BF16)), name=name,
        grid=(r // tr,), in_specs=[pl.BlockSpec((tr, c), lambda i: (i, 0))],
        out_specs=pl.BlockSpec((tr, c), lambda i: (i, 0)), out_shape=jax.ShapeDtypeStruct((r, c), BF16),
        compiler_params=_params(1),
    )(w)


def _prep_small(c_row, lb_logits):
    d = c_row.shape[1]
    rows = d // LANE

    def body(c_ref, l_ref, o_ref):
        cv = c_ref[...]
        o_ref[0:rows, :] = cv * _sigmoid(cv)
        lbs = [_sigmoid(l_ref[dr][0:1, :] - l_ref[dr][1:2, :]) for dr in range(2)]
        o_ref[rows:rows + 8, :] = jnp.concatenate(lbs + [jnp.zeros((6, LANE), F32)], axis=0)

    return pl.pallas_call(
        body, name="prep_small", out_shape=jax.ShapeDtypeStruct((rows + 8, LANE), F32),
    )(c_row.reshape(rows, LANE), lb_logits)


def _mod_shard(sc_all, w_ada_shard, b_shard):
    d, n = w_ada_shard.shape
    tn = _tile(n, 512)

    def body(s_ref, w_ref, b_ref, o_ref):
        o_ref[...] = _dot(s_ref[...], w_ref[...], precision=HIGHEST) + b_ref[...]

    return pl.pallas_call(
        body, name="mod_shard", grid=(n // tn,),
        in_specs=[pl.BlockSpec((N_DEV, d), lambda j: (0, 0)), pl.BlockSpec((d, tn), lambda j: (0, j)),
                  pl.BlockSpec((1, tn), lambda j: (0, j))],
        out_specs=pl.BlockSpec((N_DEV, tn), lambda j: (0, j)),
        out_shape=jax.ShapeDtypeStruct((N_DEV, n), F32), compiler_params=_params(1),
    )(sc_all, w_ada_shard, b_shard)


def _norm_mod(x, gain, shift, scale):
    t, d = x.shape
    tm = _tile(t, 512)

    def body(x_ref, g_ref, sh_ref, sc_ref, o_ref):
        xv = x_ref[...]
        o_ref[...] = ((xv * _rms(xv) * g_ref[...]) * (1.0 + sc_ref[...]) + sh_ref[...]).astype(BF16)

    vec = pl.BlockSpec((1, d), lambda i: (0, 0))
    return pl.pallas_call(
        body, name="norm_mod", grid=(t // tm,),
        in_specs=[pl.BlockSpec((tm, d), lambda i: (i, 0)), vec, vec, vec],
        out_specs=pl.BlockSpec((tm, d), lambda i: (i, 0)), out_shape=jax.ShapeDtypeStruct((t, d), BF16),
        compiler_params=_params(1),
    )(x, gain, shift, scale)


def _chunk_masks():
    row = lax.broadcasted_iota(jnp.int32, (HEAD, HEAD), 0)
    col = lax.broadcasted_iota(jnp.int32, (HEAD, HEAD), 1)
    same = (row // A_CHUNK) == (col // A_CHUNK)
    return same & (col <= row), same & (col >= row)


def _ones(mask):
    return jnp.where(mask, 1.0, 0.0).astype(BF16)


def _dot_split(ones_bf16, x):
    hi = x.astype(BF16)
    lo = (x - hi.astype(F32)).astype(BF16)
    return _dot(ones_bf16, hi) + _dot(ones_bf16, lo)


def _hgrn_block(direction, f, lb, cum2):
    sf = _sigmoid(f)
    big_f = lb + (1.0 - lb) * sf
    k = (1.0 - lb) * (1.0 - sf)
    lf = jnp.log(big_f)
    both = _dot_split(cum2, lf)
    cf, cr = both[:HEAD], both[HEAD:]
    b, rest = (cf, cr - lf) if direction == 0 else (cr, cf - lf)
    return k, sf, big_f, jnp.exp(b), jnp.exp(-b), jnp.exp(rest)


def _hgrn_fwd(proj, lb, g_norm, width, after):
    t = proj.shape[0]
    heads = width // HEAD
    nb, nc = t // HEAD, t // A_CHUNK
    ua = 4 if nb % 4 == 0 else (2 if nb % 2 == 0 else 1)
    ub = 16 if nc % 16 == 0 else (8 if nc % 8 == 0 else 4)
    q_scale = HEAD ** -0.5

    def body(q_ref, ffw_ref, fbw_ref, v_ref, og_ref, lb_ref, g_ref, *rest):
        outa_ref, osum_ref, qd_s, ke_s, dc_s, o_s = rest[len(after):]
        tril, triu = _chunk_masks()
        cum2 = jnp.concatenate([_ones(tril), _ones(triu)], axis=0)
        f_refs = (ffw_ref, fbw_ref)
        lbs = (lb_ref[0:1, :], lb_ref[1:2, :])

        def phase_a(it, carry):
            loaded = []
            for u in range(ua):
                rows = pl.ds(pl.multiple_of((it * ua + u) * HEAD, HEAD), HEAD)
                loaded.append((rows, q_ref[rows, :], v_ref[rows, :], ffw_ref[rows, :], fbw_ref[rows, :]))
            chains = [(d, rows, qv * q_scale, vv.astype(BF16), fv)
                      for rows, qv, vv, f0, f1 in loaded for d, fv in ((0, f0), (1, f1))]
            blocks = [_hgrn_block(d, fv, lbs[d], cum2) for d, _, _, _, fv in chains]
            scaled = [(qv * eb, k * enb, k * erest, eb * erest)
                      for (_, _, qv, _, _), (k, _, _, eb, enb, erest) in zip(chains, blocks)]
            atts = [jnp.where(tril if d == 0 else triu, _bdot(qd, kd, _NT), 0.0)
                    for (d, _, _, _, _), (qd, kd, _, _) in zip(chains, scaled)]
            intras = [_bdot(att, vv) for att, (_, _, _, vv, _) in zip(atts, chains)]
            results = [(d, rows, o_intra, qd.astype(BF16), ke.astype(BF16), decay)
                       for (d, rows, _, _, _), (qd, _, ke, decay), o_intra in zip(chains, scaled, intras)]
            for d, rows, o_intra, qd16, ke16, decay in results:
                o_s[d, rows, :] = o_intra
                qd_s[d, rows, :] = qd16
                ke_s[d, rows, :] = ke16
                dc_s[d, rows, :] = decay
            return carry

        lax.fori_loop(0, nb // ua, phase_a, 0)

        def phase_b(it, states):
            loaded = []
            for u in range(ub):
                n = it * ub + u
                for d in range(2):
                    c = n if d == 0 else nc - 1 - n
                    start = pl.multiple_of(c * A_CHUNK, A_CHUNK)
                    rows = pl.ds(start, A_CHUNK)
                    loaded.append((d, rows, qd_s[d, rows, :], ke_s[d, rows, :], v_ref[rows, :],
                                   dc_s[d, pl.ds(start, 1), :], o_s[d, rows, :]))
            increments = [_dot(vv.astype(BF16), ke16, _TN) for _, _, _, ke16, vv, _, _ in loaded]
            states = list(states)
            befores = []
            for (d, _, _, _, _, decay, _), inc in zip(loaded, increments):
                befores.append(states[d].astype(BF16))
                states[d] = states[d] * decay + inc
            inters = [_dot(qd16, before, _NT) for (_, _, qd16, _, _, _, _), before in zip(loaded, befores)]
            for (d, rows, _, _, _, _, o_intra), o_inter in zip(loaded, inters):
                o_s[d, rows, :] = o_intra + o_inter
            return tuple(states)

        zero_state = jnp.zeros((HEAD, HEAD), F32)
        lax.fori_loop(0, nc // ub, phase_b, (zero_state, zero_state))

        def phase_c(i, carry):
            rows = pl.ds(pl.multiple_of(i * HEAD, HEAD), HEAD)
            o = o_s[0, rows, :] + o_s[1, rows, :]
            osum_ref[rows, :] = o
            og = og_ref[rows, :]
            outa_ref[rows, :] = (o * _rms(o) * g_ref[...] * (og * _sigmoid(og))).astype(BF16)
            return carry

        lax.fori_loop(0, nb, phase_c, 0)

    def col(p):
        return pl.BlockSpec((t, HEAD), lambda h: (0, p * heads + h))

    return pl.pallas_call(
        body, name="hgrn_fwd", grid=(heads,),
        in_specs=[col(0), col(1), col(2), col(3), col(4),
                  pl.BlockSpec((2, HEAD), lambda h: (0, h)), pl.BlockSpec((1, HEAD), lambda h: (0, 0))]
        + [pl.BlockSpec(memory_space=pl.ANY)] * len(after),
        out_specs=[pl.BlockSpec((t, HEAD), lambda h: (0, h)), pl.BlockSpec((t, HEAD), lambda h: (0, h))],
        out_shape=[jax.ShapeDtypeStruct((t, width), BF16), jax.ShapeDtypeStruct((t, width), F32)],
        scratch_shapes=[pltpu.VMEM((2, t, HEAD), BF16), pltpu.VMEM((2, t, HEAD), BF16), pltpu.VMEM((2, t, HEAD), F32),
                        pltpu.VMEM((2, t, HEAD), F32)],
        compiler_params=_params(1),
    )(proj, proj, proj, proj, proj, lb, g_norm, *after)


def _sgu_core(u_pre, v_pre, g_v, ws_ref, bst):
    u, du = _gelu_and_grad(u_pre)
    v, dv = _gelu_and_grad(v_pre)
    mu = jnp.mean(v, axis=-1, keepdims=True)
    dlt = v - mu
    rstd = lax.rsqrt(jnp.mean(dlt * dlt, axis=-1, keepdims=True) + EPS)
    vhat = dlt * rstd
    vn = vhat * g_v
    groups = vn.shape[1] // HEAD
    cols = []
    for g in range(groups):
        vm_g = _bdot(ws_ref[g], vn[:, g * HEAD:(g + 1) * HEAD]) + bst[:, g:g + 1]
        cols.append(vm_g)
    return u, du, dv, vhat, rstd, vn, jnp.concatenate(cols, axis=1)


def _sgu_fwd(proj, g_v, w_s, bst, width, z_block):
    t = proj.shape[0]

    def body(u_ref, v_ref, g_ref, ws_ref, bst_ref, o_ref):
        u, _, _, _, _, _, vm = _sgu_core(u_ref[...], v_ref[...], g_ref[...], ws_ref, bst_ref[...])
        o_ref[...] = (u * vm).astype(BF16)

    groups = width // HEAD
    return pl.pallas_call(
        body, name="sgu_fwd", grid=(t // HEAD,),
        in_specs=[pl.BlockSpec((HEAD, width), lambda i: (i, z_block)), pl.BlockSpec((HEAD, width), lambda i: (i, z_block + 1)),
                  pl.BlockSpec((1, width), lambda i: (0, 0)), pl.BlockSpec((groups, HEAD, HEAD), lambda i: (0, 0, 0)),
                  pl.BlockSpec((HEAD, groups), lambda i: (0, 0))],
        out_specs=pl.BlockSpec((HEAD, width), lambda i: (i, 0)),
        out_shape=jax.ShapeDtypeStruct((t, width), BF16), compiler_params=_params(1),
    )(proj, proj, g_v, w_s, bst)


def _sgu_bwd(proj, dout_b, dproj, g_v, w_s, w_st, bst, width, z_block):
    t = proj.shape[0]
    groups = width // HEAD
    nblk = t // HEAD

    def body(u_ref, v_ref, do_ref, g_ref, ws_ref, wst_ref, bst_ref, dproj_hbm,
             dz_ref, dg_ref, dws_ref, dbst_ref, res_s):
        i, p = pl.program_id(0), pl.program_id(1)

        @pl.when((i == 0) & (p == 0))
        def _():
            dg_ref[...] = jnp.zeros_like(dg_ref)
            dws_ref[...] = jnp.zeros_like(dws_ref)
            dbst_ref[...] = jnp.zeros_like(dbst_ref)

        @pl.when(p == 0)
        def _():
            g_v = g_ref[...]
            u, du, dv, vhat, rstd, vn, vm = _sgu_core(u_ref[...], v_ref[...], g_v, ws_ref, bst_ref[...])
            dout = do_ref[...].astype(F32)
            res_s[0] = (dout * vm * du).astype(BF16)
            dvm = dout * u
            dvn_cols = []
            for g in range(groups):
                sl = slice(g * HEAD, (g + 1) * HEAD)
                dvm_g = dvm[:, sl]
                dbst_ref[:, g:g + 1] += jnp.sum(dvm_g, axis=1, keepdims=True)
                dws_ref[g] += _bdot(dvm_g, vn[:, sl], _NT)
                dvn_cols.append(_bdot(wst_ref[g], dvm_g))
            dvn = jnp.concatenate(dvn_cols, axis=1)
            dg_ref[...] += _colsum(dvn * vhat)
            dvh = dvn * g_v
            dvg = rstd * (dvh - jnp.mean(dvh, axis=-1, keepdims=True)
                          - vhat * jnp.mean(dvh * vhat, axis=-1, keepdims=True))
            res_s[1] = (dvg * dv).astype(BF16)

        dz_ref[...] = res_s[p]

    n_in = dproj.shape[1]
    return pl.pallas_call(
        body, name="sgu_bwd", grid=(nblk, 2),
        in_specs=[pl.BlockSpec((HEAD, width), lambda i, p: (i, z_block)),
                  pl.BlockSpec((HEAD, width), lambda i, p: (i, z_block + 1)),
                  pl.BlockSpec((HEAD, width), lambda i, p: (i, 0)),
                  pl.BlockSpec((1, width), lambda i, p: (0, 0)),
                  pl.BlockSpec((groups, HEAD, HEAD), lambda i, p: (0, 0, 0)),
                  pl.BlockSpec((groups, HEAD, HEAD), lambda i, p: (0, 0, 0)),
                  pl.BlockSpec((HEAD, groups), lambda i, p: (0, 0)),
                  pl.BlockSpec(memory_space=pl.ANY)],
        out_specs=[pl.BlockSpec((HEAD, width), lambda i, p: (i, z_block + p)),
                   pl.BlockSpec((1, width), lambda i, p: (0, 0)),
                   pl.BlockSpec((groups, HEAD, HEAD), lambda i, p: (0, 0, 0)),
                   pl.BlockSpec((HEAD, groups), lambda i, p: (0, 0))],
        out_shape=[jax.ShapeDtypeStruct((t, n_in), BF16), jax.ShapeDtypeStruct((1, width), F32),
                   jax.ShapeDtypeStruct((groups, HEAD, HEAD), F32), jax.ShapeDtypeStruct((HEAD, groups), F32)],
        scratch_shapes=[pltpu.VMEM((2, HEAD, width), BF16)],
        input_output_aliases={7: 0},
        compiler_params=_params(2),
    )(proj, proj, dout_b, g_v, w_s, w_st, bst, dproj)


def _hgrn_bwd(proj, osum, dout_a, dproj, lb, g_norm, width, after):
    t = proj.shape[0]
    heads = width // HEAD
    nb = t // HEAD
    cpb = HEAD // A_CHUNK
    ubk = 4 if nb % 4 == 0 else (2 if nb % 2 == 0 else 1)
    q_scale = HEAD ** -0.5

    def body(q_ref, ffw_ref, fbw_ref, v_ref, og_ref, osum_ref, douta_ref, lb_ref, g_ref, dproj_hbm, after_hbm,
             out_ref, dgh_ref, dlb_ref, do_s, dq_s, dv_s, res_s, ck_s):
        p = pl.program_id(1)
        f_refs = (ffw_ref, fbw_ref)

        @pl.when(p == 0)
        def _():
            tril, triu = _chunk_masks()
            cum2 = jnp.concatenate([_ones(tril), _ones(triu)], axis=0)
            g_row = g_ref[...]

            def pass_norm(i, dgh):
                rows = pl.ds(pl.multiple_of(i * HEAD, HEAD), HEAD)
                o = osum_ref[rows, :]
                r = _rms(o)
                oh = o * r
                og = og_ref[rows, :]
                sg = _sigmoid(og)
                dout = douta_ref[rows, :].astype(F32)
                don = dout * (og * sg)
                res_s[4, rows, :] = (dout * (oh * g_row) * (sg * (1.0 + og * (1.0 - sg)))).astype(BF16)
                doh = don * g_row
                do_s[rows, :] = r * (doh - oh * jnp.mean(doh * oh, axis=-1, keepdims=True))
                return dgh + _colsum(don * oh)

            dgh_ref[...] = lax.fori_loop(0, nb, pass_norm, jnp.zeros((1, HEAD), F32))

            lbs = (lb_ref[0:1, :], lb_ref[1:2, :])
            zero_state = jnp.zeros((HEAD, HEAD), F32)

            def chunk_order(d):
                return list(range(cpb)) if d == 0 else list(range(cpb - 1, -1, -1))

            def chunk(x, j):
                return x[j * A_CHUNK:(j + 1) * A_CHUNK, :]

            def decay_row(e_big, j):
                return e_big[j * A_CHUNK:j * A_CHUNK + 1, :]

            def cat(parts):
                return jnp.concatenate([parts[j] for j in range(cpb)], axis=0)

            def block_states(d, start, incs, e_big):
                befores, st = {}, start
                for j in chunk_order(d):
                    befores[j] = st
                    st = st * decay_row(e_big, j) + incs[j]
                return befores, st

            def pass_states(it, states):
                loaded = []
                for u in range(ubk):
                    for d in range(2):
                        blk = it * ubk + u if d == 0 else nb - 1 - (it * ubk + u)
                        rows = pl.ds(pl.multiple_of(blk * HEAD, HEAD), HEAD)
                        loaded.append((d, blk, f_refs[d][rows, :], v_ref[rows, :]))
                blocks = [_hgrn_block(d, fv, lbs[d], cum2) for d, _, fv, _ in loaded]
                incs = [{j: _bdot(chunk(vv, j), chunk(k * erest, j), _TN) for j in range(cpb)}
                        for (_, _, _, vv), (k, _, _, _, _, erest) in zip(loaded, blocks)]
                states, starts = list(states), []
                for (d, _, _, _), (_, _, _, eb, _, erest), inc in zip(loaded, blocks, incs):
                    starts.append(states[d])
                    states[d] = block_states(d, states[d], inc, eb * erest)[1]
                for (d, blk, _, _), start in zip(loaded, starts):
                    ck_s[d, blk] = start
                return tuple(states)

            lax.fori_loop(0, nb // ubk, pass_states, (zero_state, zero_state))

            def pass_back(it, carry):
                gts, dlb = [carry[0], carry[1]], carry[2]
                loaded = []
                for u, d in ((u, d) for u in range(ubk) for d in range(2)):
                    blk = nb - 1 - (it * ubk + u) if d == 0 else it * ubk + u
                    rows = pl.ds(pl.multiple_of(blk * HEAD, HEAD), HEAD)
                    loaded.append((d, rows, f_refs[d][rows, :], q_ref[rows, :], v_ref[rows, :], do_s[rows, :], ck_s[d, blk]))
                blocks = [_hgrn_block(d, fv, lbs[d], cum2) for d, _, fv, _, _, _, _ in loaded]
                scaled = []
                for (_, _, _, qv, _, _, _), (k, _, _, eb, enb, erest) in zip(loaded, blocks):
                    qh = qv * q_scale
                    scaled.append((qh, qh * eb, k * enb, k * erest, eb * erest))
                masks = [tril if d == 0 else triu for d, *_ in loaded]
                atts = [jnp.where(m, _bdot(qd, kd, _NT), 0.0) for m, (_, qd, kd, _, _) in zip(masks, scaled)]
                datts = [jnp.where(m, _bdot(do, vv, _NT), 0.0) for m, (_, _, _, _, vv, do, _) in zip(masks, loaded)]
                dvs = [_bdot(att, do, _TN) for att, (_, _, _, _, _, do, _) in zip(atts, loaded)]
                dqds = [_bdot(datt, kd) for datt, (_, _, kd, _, _) in zip(datts, scaled)]
                dkds = [_bdot(datt, qd, _TN) for datt, (_, qd, _, _, _) in zip(datts, scaled)]
                s_incs = [{j: _bdot(chunk(vv, j), chunk(ke, j), _TN) for j in range(cpb)}
                          for (_, _, _, _, vv, _, _), (_, _, _, ke, _) in zip(loaded, scaled)]
                g_incs = [{j: _bdot(chunk(do, j), chunk(qd, j), _TN) for j in range(cpb)}
                          for (_, _, _, _, _, do, _), (_, qd, _, _, _) in zip(loaded, scaled)]
                befores, afters, g_at = [], [], []
                for (d, _, _, _, _, _, ck), (_, _, _, _, e_big), s_inc, g_inc in zip(loaded, scaled, s_incs, g_incs):
                    order = chunk_order(d)
                    before, after = block_states(d, ck, s_inc, e_big)
                    befores.append(before)
                    afters.append({j: (before[order[n + 1]] if n + 1 < cpb else after) for n, j in enumerate(order)})
                    at, gt = {}, gts[d]
                    for j in reversed(order):
                        at[j] = gt
                        gt = gt * decay_row(e_big, j) + g_inc[j]
                    gts[d] = gt
                    g_at.append(at)
                dqd_i = [{j: _bdot(chunk(do, j), before[j]) for j in range(cpb)}
                         for (_, _, _, _, _, do, _), before in zip(loaded, befores)]
                dv_i = [{j: _bdot(chunk(ke, j), at[j], _NT) for j in range(cpb)}
                        for (_, _, _, ke, _), at in zip(scaled, g_at)]
                dke = [{j: _bdot(chunk(vv, j), at[j]) for j in range(cpb)}
                       for (_, _, _, _, vv, _, _), at in zip(loaded, g_at)]
                results, new = [], []
                for n, ((d, rows, _, _, _, _, _), (k, sf, big_f, eb, enb, erest), (qh, _, _, _, _)) in enumerate(
                        zip(loaded, blocks, scaled)):
                    dqh = (dqds[n] + cat(dqd_i[n])) * eb
                    dk = dkds[n] * enb + cat(dke[n]) * erest
                    carry_rows = {j: jnp.broadcast_to(_colsum(g_at[n][j] * afters[n][j]), (A_CHUNK, HEAD))
                                  for j in range(cpb)}
                    dlf = _dot_split(_ones(triu if d == 0 else tril), qh * dqh - k * dk) + cat(carry_rows)
                    common = dlf / big_f - dk
                    results.append((d, rows, (k * sf * common).astype(BF16), dqh.astype(BF16),
                                    (dvs[n] + cat(dv_i[n])).astype(BF16)))
                    new.append(_colsum((1.0 - sf) * common))
                for d, rows, df16, dq16, dv16 in results:
                    res_s[1 + d, rows, :] = df16
                    dq_s[d, rows, :] = dq16
                    dv_s[d, rows, :] = dv16
                per_dir = [sum(c for (d, *_), c in zip(loaded, new) if d == dd) for dd in range(2)]
                return gts[0], gts[1], dlb + jnp.concatenate(per_dir, axis=0)

            dlb_ref[...] = lax.fori_loop(0, nb // ubk, pass_back,
                                         (zero_state, zero_state, jnp.zeros((2, HEAD), F32)))[2]

            def pass_out(i, carry):
                rows = pl.ds(pl.multiple_of(i * HEAD, HEAD), HEAD)
                dq = dq_s[0, rows, :].astype(F32) + dq_s[1, rows, :].astype(F32)
                res_s[0, rows, :] = (dq * q_scale).astype(BF16)
                res_s[3, rows, :] = (dv_s[0, rows, :].astype(F32) + dv_s[1, rows, :].astype(F32)).astype(BF16)
                return carry

            lax.fori_loop(0, nb, pass_out, 0)

        out_ref[...] = res_s[p]

    def col(pp):
        return pl.BlockSpec((t, HEAD), lambda h, p: (0, pp * heads + h))

    n_in = dproj.shape[1]
    any_spec = pl.BlockSpec(memory_space=pl.ANY)
    return pl.pallas_call(
        body, name="hgrn_bwd", grid=(heads, 5),
        in_specs=[col(0), col(1), col(2), col(3), col(4),
                  pl.BlockSpec((t, HEAD), lambda h, p: (0, h)), pl.BlockSpec((t, HEAD), lambda h, p: (0, h)),
                  pl.BlockSpec((2, HEAD), lambda h, p: (0, h)), pl.BlockSpec((1, HEAD), lambda h, p: (0, 0)),
                  any_spec, any_spec],
        out_specs=[pl.BlockSpec((t, HEAD), lambda h, p: (0, p * heads + h)),
                   pl.BlockSpec((None, 1, HEAD), lambda h, p: (h, 0, 0)),
                   pl.BlockSpec((2, HEAD), lambda h, p: (0, h))],
        out_shape=[jax.ShapeDtypeStruct((t, n_in), BF16), jax.ShapeDtypeStruct((heads, 1, HEAD), F32),
                   jax.ShapeDtypeStruct((2, width), F32)],
        scratch_shapes=[pltpu.VMEM((t, HEAD), F32), pltpu.VMEM((2, t, HEAD), BF16), pltpu.VMEM((2, t, HEAD), BF16),
                        pltpu.VMEM((5, t, HEAD), BF16), pltpu.VMEM((2, nb, HEAD, HEAD), F32)],
        input_output_aliases={9: 0},
        compiler_params=_params(2),
    )(proj, proj, proj, proj, proj, osum, dout_a, lb, g_norm, dproj, after)


def _adamw(w, g, m, v):
    m = ADAM_B1 * m + (1.0 - ADAM_B1) * g
    v = ADAM_B2 * v + (1.0 - ADAM_B2) * (g * g)
    m_hat = m / (1.0 - ADAM_B1 ** ADAM_STEP)
    v_hat = v / (1.0 - ADAM_B2 ** ADAM_STEP)
    delta = -ADAM_LR * (m_hat / (jnp.sqrt(v_hat) + ADAM_EPS) + ADAM_WD * w)
    return delta, m, v


def _adamw_big(name, me, w, m, v, parts, axis):
    r, c = w.shape
    n_parts = len(parts)
    tr = _tile(r // n_parts, 128)
    per = r // n_parts // tr
    assert axis == 1 or n_parts == 1

    def body(me_ref, w_ref, m_ref, v_ref, *rest):
        g_refs, l_refs = rest[:n_parts], rest[n_parts:2 * n_parts]
        og_ref, od_ref, om_ref, ov_ref = rest[2 * n_parts:]
        g = None
        for p in range(n_parts):
            total = g_refs[p][...]
            for s in range(N_DEV - 1):
                total = total + l_refs[p][s].astype(F32)
            g = total if p == 0 else jnp.where(pl.program_id(0) // per == p, total, g)
        og_ref[...] = g
        od_ref[...], om_ref[...], ov_ref[...] = _adamw(w_ref[...], g, m_ref[...], v_ref[...])

    def within(p, i):
        return jnp.clip(i - p * per, 0, per - 1)

    shard = pl.BlockSpec((tr, c), lambda i, me_ref: (i, 0))
    if axis == 1:
        own = [pl.BlockSpec((tr, c), lambda i, me_ref, p=p: (within(p, i), me_ref[0])) for p in range(n_parts)]
    else:
        own = [pl.BlockSpec((tr, c), lambda i, me_ref: (me_ref[0] * (r // tr) + i, 0))]
    landed = [pl.BlockSpec((N_DEV - 1, tr, c), lambda i, me_ref, p=p: (0, within(p, i), 0)) for p in range(n_parts)]
    grid_spec = pltpu.PrefetchScalarGridSpec(
        num_scalar_prefetch=1, grid=(r // tr,),
        in_specs=[shard, shard, shard] + own + landed, out_specs=[shard] * 4)
    return pl.pallas_call(
        body, name=name, grid_spec=grid_spec, out_shape=[jax.ShapeDtypeStruct((r, c), F32)] * 4,
        compiler_params=_params(1),
    )(me, w, m, v, *[g for g, _ in parts], *[ld for _, ld in parts])


def _adamw_ada(sct, dmod_mine, w, m, v):
    d, n = w.shape
    tr = _tile(d, 256)

    def body(s_ref, dm_ref, w_ref, m_ref, v_ref, og_ref, od_ref, om_ref, ov_ref):
        g = _dot(s_ref[...], dm_ref[...], precision=HIGHEST)
        og_ref[...] = g
        od_ref[...], om_ref[...], ov_ref[...] = _adamw(w_ref[...], g, m_ref[...], v_ref[...])

    blk = pl.BlockSpec((tr, n), lambda i: (i, 0))
    return pl.pallas_call(
        body, name="adamw_ada", grid=(d // tr,),
        in_specs=[pl.BlockSpec((tr, N_DEV), lambda i: (i, 0)), pl.BlockSpec((N_DEV, n), lambda i: (0, 0)), blk, blk, blk],
        out_specs=[blk] * 4, out_shape=[jax.ShapeDtypeStruct((d, n), F32)] * 4, compiler_params=_params(1),
    )(sct, dmod_mine, w, m, v)


def _adamw_small(gathered, w, m, v):
    def body(g_ref, w_ref, m_ref, v_ref, og_ref, od_ref, om_ref, ov_ref):
        g = g_ref[0]
        for s in range(1, N_DEV):
            g = g + g_ref[s]
        og_ref[...] = g
        od_ref[...], om_ref[...], ov_ref[...] = _adamw(w_ref[...], g, m_ref[...], v_ref[...])

    return pl.pallas_call(
        body, name="adamw_small", out_shape=[jax.ShapeDtypeStruct(w.shape, F32)] * 4,
        compiler_params=pltpu.CompilerParams(vmem_limit_bytes=VMEM_LIMIT),
    )(gathered, w, m, v)


def _adamw_lb(dlb_mine, lb_logits, m, v):
    def body(d_ref, l_ref, m_ref, v_ref, og_ref, od_ref, om_ref, ov_ref):
        dlb = d_ref[0]
        for s in range(1, N_DEV):
            dlb = dlb + d_ref[s]
        for dr in range(2):
            lb = _sigmoid(l_ref[dr][0:1, :] - l_ref[dr][1:2, :])
            d0 = dlb[dr:dr + 1] * lb * (1.0 - lb)
            g = jnp.concatenate([d0, -d0], axis=0)
            og_ref[dr] = g
            od_ref[dr], om_ref[dr], ov_ref[dr] = _adamw(l_ref[dr], g, m_ref[dr], v_ref[dr])

    return pl.pallas_call(body, name="adamw_lb", out_shape=[jax.ShapeDtypeStruct(lb_logits.shape, F32)] * 4,
                          )(dlb_mine, lb_logits, m, v)


def _rows(a, pad_to=8):
    flat = a.reshape(-1, LANE)
    pad = (-flat.shape[0]) % pad_to
    return jnp.pad(flat, ((0, pad), (0, 0))) if pad else flat


def kernel(x, c, w_ada, b_ada, g_pre_mix, g_post_mix, g_pre_ffn, g_post_ffn, w_in, lb_logits, g_hgrn_norm, w_a_out, g_sgu_norm, w_spatial, b_spatial, w_b_out, w_o, w_ff1, w_ff2, loss_target, m_w_ada, m_b_ada, m_g_pre_mix, m_g_post_mix, m_g_pre_ffn, m_g_post_ffn, m_w_in, m_lb_logits, m_g_hgrn_norm, m_w_a_out, m_g_sgu_norm, m_w_spatial, m_b_spatial, m_w_b_out, m_w_o, m_w_ff1, m_w_ff2, v_w_ada, v_b_ada, v_g_pre_mix, v_g_post_mix, v_g_pre_ffn, v_g_post_ffn, v_w_in, v_lb_logits, v_g_hgrn_norm, v_w_a_out, v_g_sgu_norm, v_w_spatial, v_b_spatial, v_w_b_out, v_w_o, v_w_ff1, v_w_ff2):
    t, d = x.shape[1], x.shape[2]
    n_in = w_in.shape[2] * N_DEV
    width = (n_in - 2 * d) // 7
    heads = width // HEAD
    assert heads == N_DEV and width % LANE == 0
    d_ff = w_ff1.shape[2] * N_DEV
    n_ada = w_ada.shape[2]
    me = _dev_index()
    me_arr = me.reshape(1).astype(jnp.int32)
    x2, tgt = x[0], loss_target[0]

    big = [w_in[0], w_a_out[0], w_b_out[0], w_o[0], w_ff1[0], w_ff2[0]]
    big_axes = [1, 1, 1, 0, 1, 0]
    big_names = ["w_in", "w_a_out", "w_b_out", "w_o", "w_ff1", "w_ff2"]
    w_in16 = _cast_bf16("cast_w_in", big[0])
    own_parts = [_cast_into_full("cast_" + nm, me_arr, w, ax) for nm, w, ax in zip(big_names[1:], big[1:], big_axes[1:])]

    c_rows = d // LANE
    small = _all_gather_small("gather_c_lb", _prep_small(c[0:1], lb_logits))
    sc_all = small[:, :c_rows, :].reshape(N_DEV, d)
    lb = jnp.transpose(small[:, c_rows:c_rows + 2, :], (1, 0, 2)).reshape(2, width)
    b_shard = lax.dynamic_slice_in_dim(b_ada, me * n_ada, n_ada, axis=1)
    mod_sh = _mod_shard(sc_all, w_ada[0], b_shard)
    mod_all = _all_gather_small("gather_mod", _rows(mod_sh))
    mod_all = mod_all[:, :N_DEV * n_ada // LANE, :].reshape(N_DEV, N_DEV, n_ada)
    mod6 = lax.dynamic_index_in_dim(mod_all, me, axis=1, keepdims=False).reshape(N_MOD, d)
    sh1, sc1, gt1, sh2, sc2, gt2 = [mod6[i:i + 1] for i in range(N_MOD)]

    a1 = _norm_mod(x2, g_pre_mix, sh1, sc1)
    tm = _tile(t, 512)

    def store_bf16(acc, i, j, extra_refs, out_refs, rows):
        out_refs[0][...] = acc.astype(BF16)

    xq, yq, cq = lax.axis_index("x"), lax.axis_index("y"), lax.axis_index("c")
    chips = [(1 - xq, yq), (xq, 1 - yq), (1 - xq, 1 - yq)]
    order = jnp.stack([me, 4 * xq + 2 * yq + 1 - cq]
                      + [4 * a + 2 * b + cq for a, b in chips[:2]] + [4 * a + 2 * b + 1 - cq for a, b in chips[:2]]
                      + [4 * chips[2][0] + 2 * chips[2][1] + cq, 4 * chips[2][0] + 2 * chips[2][1] + 1 - cq]).astype(jnp.int32)
    proj, wf_in = _proj_gather(a1, w_in16, order)

    proj, own_parts = lax.optimization_barrier((proj, own_parts))
    gathers = {}
    for key, lo, hi in (("mid", 1, 4), ("ff1", 4, 5), ("ff2", 5, 6)):
        far, near = _gather_stage_plans(own_parts[lo - 1:hi - 1], big_axes[lo:hi])
        gathers[key] = [far, near, _split_start("gather_%s_start" % key, far, landing=own_parts[lo - 1:hi - 1])]

    def pass_on(key, after):
        far, near, (sems, thru, _) = gathers[key]
        parts = _split_wait("gather_%s_wait" % key, far, sems, thru, after)[1]
        gathers[key].append(_split_start("pass_%s_start" % key, near, landing=list(parts)))
        return gathers[key][3][2]

    def gathered_weights(key, after):
        near, (sems, thru, _) = gathers[key][1], gathers[key][3]
        return _split_wait("pass_%s_wait" % key, near, sems, thru, after)[1]

    out_a, osum = _hgrn_fwd(proj, lb, g_hgrn_norm, width,
                            after=[gathers[key][2][2] for key in ("mid", "ff1", "ff2")])
    passed_mid = pass_on("mid", out_a)
    z_block = 5
    bst = b_spatial[0].T
    out_b = _sgu_fwd(proj, g_sgu_norm, w_spatial[0], bst, width, z_block)
    wf_a, wf_b, wf_o = gathered_weights("mid", out_b)

    tn_d = _tile(d, 1024)
    blk_d = ((tm, tn_d), lambda i, j: (i, j))
    y_a, = _mm("y_a", out_a, wf_a, _NN, t, d, width, tm, tn_d, width, _after(passed_mid),
               [(jax.ShapeDtypeStruct((t, d), BF16),) + blk_d], store_bf16)
    ga_blk = (5 * width + 2 * width) // tn_d
    gb_blk = ga_blk + d // tn_d

    def merge(acc, i, j, extra_refs, out_refs, rows):
        ga, gb, ya = extra_refs
        out_refs[0][...] = acc.astype(BF16)
        out_refs[1][...] = (_sigmoid(ga[...]) * ya[...].astype(F32) + _sigmoid(gb[...]) * acc).astype(BF16)

    y_b, merged = _mm("y_b_merge", out_b, wf_b, _NN, t, d, width, tm, tn_d, width,
                      [(proj, (tm, tn_d), lambda i, j: (i, ga_blk + j)), (proj, (tm, tn_d), lambda i, j: (i, gb_blk + j)),
                       (y_a,) + blk_d],
                      [(jax.ShapeDtypeStruct((t, d), BF16),) + blk_d, (jax.ShapeDtypeStruct((t, d), BF16),) + blk_d], merge)

    tr = _tile(t, 512)
    rc = 32 if tr % 32 == 0 else None
    row_d = ((tr, d), lambda i, j: (i, 0))
    vec_d = ((1, d), lambda i, j: (0, 0))

    passed_ff1 = pass_on("ff1", merged)

    def post_mix(acc, i, j, extra_refs, out_refs, rows):
        x_r, gt1_r, g2_r, g3_r, sc2_r, sh2_r = extra_refs[:6]
        h1 = x_r[rows, :] + gt1_r[...] * (acc * _rms(acc) * g2_r[...])
        out_refs[0][rows, :] = acc.astype(BF16)
        out_refs[1][rows, :] = h1
        out_refs[2][rows, :] = ((h1 * _rms(h1) * g3_r[...]) * (1.0 + sc2_r[...]) + sh2_r[...]).astype(BF16)

    mo, h1, a2 = _mm("w_o_post_mix", merged, wf_o, _NN, t, d, d, tr, d, d,
                     [(x2,) + row_d, (gt1,) + vec_d, (g_post_mix,) + vec_d, (g_pre_ffn,) + vec_d, (sc2,) + vec_d, (sh2,) + vec_d]
                     + _after(passed_ff1),
                     [(jax.ShapeDtypeStruct((t, d), BF16),) + row_d, (jax.ShapeDtypeStruct((t, d), F32),) + row_d,
                      (jax.ShapeDtypeStruct((t, d), BF16),) + row_d], post_mix, row_chunk=rc)

    tn_f = _tile(d_ff, 2048)
    blk_f = ((tm, tn_f), lambda i, j: (i, j))

    def relu_sq(acc, i, j, extra_refs, out_refs, rows):
        r = jnp.maximum(acc, 0.0)
        out_refs[0][...] = acc.astype(BF16)
        out_refs[1][...] = (r * r).astype(BF16)

    wf_1, = gathered_weights("ff1", a2)
    hff, act = _mm(
        "ff1", a2, wf_1, _NN, t, d_ff, d, tm, tn_f, d, [],
        [(jax.ShapeDtypeStruct((t, d_ff), BF16),) + blk_f, (jax.ShapeDtypeStruct((t, d_ff), BF16),) + blk_f], relu_sq)
    pass_on("ff2", hff)
    wf_2, = gathered_weights("ff2", act)

    sums_d = ((8, d), lambda i, j: (0, 0))

    def zero_first(sums_r, i, rows):
        if rows.start in (None, 0):
            @pl.when(i == 0)
            def _():
                sums_r[...] = jnp.zeros_like(sums_r)

    def loss_head(acc, i, j, extra_refs, out_refs, rows):
        h1_r, tgt_r, gt2_r, g4_r = extra_refs
        dy_r, dff_r, sums_r = out_refs
        r4 = _rms(acc)
        ffn = acc * r4
        n4 = ffn * g4_r[...]
        err = h1_r[rows, :] + gt2_r[...] * n4 - tgt_r[rows, :]
        dy = err * (1.0 / d)
        dy_r[rows, :] = dy.astype(BF16)
        dn4 = dy * gt2_r[...]
        dffn = dn4 * g4_r[...]
        dff_r[rows, :] = (r4 * (dffn - ffn * jnp.mean(dffn * ffn, axis=-1, keepdims=True))).astype(BF16)
        zero_first(sums_r, i, rows)

        sums_r[0:1, :] += _colsum(err * err)
        sums_r[1:2, :] += _colsum(dy * n4)
        sums_r[2:3, :] += _colsum(dn4 * ffn)

    tk_f = _tile(d_ff, 1024)
    dy, dff, sums_f = _mm("ff2_loss", act, wf_2, _NN, t, d, d_ff, tr, d, _tile(d_ff, 2048),
                          [(h1,) + row_d, (tgt,) + row_d, (gt2,) + vec_d, (g_post_ffn,) + vec_d],
                          [(jax.ShapeDtypeStruct((t, d), BF16),) + row_d, (jax.ShapeDtypeStruct((t, d), BF16),) + row_d,
                           (jax.ShapeDtypeStruct((8, d), F32),) + sums_d], loss_head, row_chunk=rc)
    loss_mine = (0.5 / d) * jnp.sum(sums_f[0])

    def relu_sq_bwd(acc, i, j, extra_refs, out_refs, rows):
        out_refs[0][...] = (acc * (2.0 * jnp.maximum(extra_refs[0][...].astype(F32), 0.0))).astype(BF16)

    dhff, = _mm("d_hff", dff, wf_2, _NT, t, d_ff, d, tm, tn_f, d, [(hff,) + blk_f],
                [(jax.ShapeDtypeStruct((t, d_ff), BF16),) + blk_f], relu_sq_bwd)
    scatters = {}

    def send_grads(key, grads16, axes):
        plan = _scatter_plan(grads16, axes)
        scatters[key] = (plan,) + _split_start("scatter_%s_start" % key, plan)
        return scatters[key][3]

    def received_grads(key, after):
        plan, sems, thru, _ = scatters[key]
        return _split_wait("scatter_%s_wait" % key, plan, sems, thru, after)[1]

    gw_ff2, gw_ff2_16 = _grad_w("grad_w_ff2", act, dff)
    sent_ff2 = send_grads("ff2", [gw_ff2_16], big_axes[5:6])
    gw_ff1, gw_ff1_16 = _grad_w("grad_w_ff1", a2, dhff, token=sent_ff2)
    sent_ff1 = send_grads("ff1", [gw_ff1_16], big_axes[4:5])

    def pre_ffn_bwd(acc, i, j, extra_refs, out_refs, rows):
        h1_r, dy_r, mo_r, sc2_r, g3_r, gt1_r, g2_r = extra_refs[:7]
        dh1_r, dmo_r, sums_r = out_refs
        h1v = h1_r[rows, :]
        r3 = _rms(h1v)
        h1n = h1v * r3
        dn3 = acc * (1.0 + sc2_r[...])
        dh1n = dn3 * g3_r[...]
        dh1 = dy_r[rows, :].astype(F32) + r3 * (dh1n - h1n * jnp.mean(dh1n * h1n, axis=-1, keepdims=True))
        dh1_r[rows, :] = dh1.astype(BF16)
        mov = mo_r[rows, :].astype(F32)
        r2 = _rms(mov)
        mon = mov * r2
        dn2 = dh1 * gt1_r[...]
        dmon = dn2 * g2_r[...]
        dmo_r[rows, :] = (r2 * (dmon - mon * jnp.mean(dmon * mon, axis=-1, keepdims=True))).astype(BF16)
        zero_first(sums_r, i, rows)

        sums_r[0:1, :] += _colsum(acc)
        sums_r[1:2, :] += _colsum(acc * (h1n * g3_r[...]))
        sums_r[2:3, :] += _colsum(dn3 * h1n)
        sums_r[3:4, :] += _colsum(dh1 * (mon * g2_r[...]))
        sums_r[4:5, :] += _colsum(dn2 * mon)

    dh1, dmo, sums_m = _mm("d_a2_pre_ffn", dhff, wf_1, _NT, t, d, d_ff, tr, d, tk_f,
                           [(h1,) + row_d, (dy,) + row_d, (mo,) + row_d, (sc2,) + vec_d, (g_pre_ffn,) + vec_d,
                            (gt1,) + vec_d, (g_post_mix,) + vec_d] + _after(sent_ff1),
                           [(jax.ShapeDtypeStruct((t, d), BF16),) + row_d, (jax.ShapeDtypeStruct((t, d), BF16),) + row_d,
                            (jax.ShapeDtypeStruct((8, d), F32),) + sums_d], pre_ffn_bwd, row_chunk=rc)
    gw_o, gw_o_16 = _grad_w("grad_w_o", merged, dmo)

    n_j = d // tn_d

    def merge_bwd_body(dmo_ref, wo_ref, ga_ref, gb_ref, ya_ref, yb_ref, dya_ref, dyb_ref, dproj_ref, acc_s):
        g = pl.program_id(2)

        @pl.when(g == 0)
        def _():
            dm = _dot(dmo_ref[...], wo_ref[...], _NT)
            acc_s[...] = dm
            sa = _sigmoid(ga_ref[...])
            dya_ref[...] = (dm * sa).astype(BF16)
            dproj_ref[...] = (dm * ya_ref[...].astype(F32) * sa * (1.0 - sa)).astype(BF16)

        @pl.when(g == 1)
        def _():
            dm = acc_s[...]
            sb = _sigmoid(gb_ref[...])
            dyb_ref[...] = (dm * sb).astype(BF16)
            dproj_ref[...] = (dm * yb_ref[...].astype(F32) * sb * (1.0 - sb)).astype(BF16)

    tile3 = pl.BlockSpec((tm, tn_d), lambda i, j, g: (i, j))
    dy_a, dy_b, dproj = pl.pallas_call(
        merge_bwd_body, name="d_merged", grid=(t // tm, n_j, 2),
        in_specs=[pl.BlockSpec((tm, d), lambda i, j, g: (i, 0)), pl.BlockSpec((tn_d, d), lambda i, j, g: (j, 0)),
                  pl.BlockSpec((tm, tn_d), lambda i, j, g: (i, ga_blk + j)),
                  pl.BlockSpec((tm, tn_d), lambda i, j, g: (i, gb_blk + j)), tile3, tile3],
        out_specs=[tile3, tile3, pl.BlockSpec((tm, tn_d), lambda i, j, g: (i, ga_blk + g * n_j + j))],
        out_shape=[jax.ShapeDtypeStruct((t, d), BF16), jax.ShapeDtypeStruct((t, d), BF16),
                   jax.ShapeDtypeStruct((t, n_in), BF16)],
        scratch_shapes=[pltpu.VMEM((tm, tn_d), F32)], compiler_params=_params(3),
    )(dmo, wf_o, proj, proj, y_a, y_b)

    tn_w = _tile(width, 1024)
    blk_w = ((tm, tn_w), lambda i, j: (i, j))
    dout_a, = _mm("d_out_a", dy_a, wf_a, _NT, t, width, d, tm, tn_w, d, [],
                  [(jax.ShapeDtypeStruct((t, width), BF16),) + blk_w], store_bf16)
    dout_b, = _mm("d_out_b", dy_b, wf_b, _NT, t, width, d, tm, tn_w, d, [],
                  [(jax.ShapeDtypeStruct((t, width), BF16),) + blk_w], store_bf16)
    gw_a, gw_a_16 = _grad_w("grad_w_a_out", out_a, dy_a)
    gw_b, gw_b_16 = _grad_w("grad_w_b_out", out_b, dy_b)

    w_st = jnp.swapaxes(w_spatial[0], 1, 2)
    dproj, dg_sgu, dw_sp, dbst = _sgu_bwd(proj, dout_b, dproj, g_sgu_norm, w_spatial[0], w_st, bst, width, z_block)
    sent_mid = send_grads("mid", [gw_a_16, gw_b_16, gw_o_16], big_axes[1:4])
    dproj, dgh_heads, dlb = _hgrn_bwd(proj, osum, dout_a, dproj, lb, g_hgrn_norm, width, after=sent_mid)
    gw_in_top, gw_in_top16 = _grad_w("grad_w_in_top", a1, dproj, rows=(0, d // 2))
    sent_top = send_grads("in_top", [gw_in_top16], big_axes[:1])
    gw_in_bot, gw_in_bot16 = _grad_w("grad_w_in_bot", a1, dproj, token=sent_top, rows=(d // 2, d // 2))
    sent_in = send_grads("in_bot", [gw_in_bot16], big_axes[:1])

    def pre_mix_bwd(acc, i, j, extra_refs, out_refs, rows):
        x_r, dh1_r, sc1_r, g1_r = extra_refs[:4]
        dx_r, sums_r = out_refs
        xv = x_r[rows, :]
        r1 = _rms(xv)
        xn = xv * r1
        dn1 = acc * (1.0 + sc1_r[...])
        dxn = dn1 * g1_r[...]
        dx_r[rows, :] = dh1_r[rows, :].astype(F32) + r1 * (dxn - xn * jnp.mean(dxn * xn, axis=-1, keepdims=True))
        zero_first(sums_r, i, rows)

        sums_r[0:1, :] += _colsum(acc)
        sums_r[1:2, :] += _colsum(acc * (xn * g1_r[...]))
        sums_r[2:3, :] += _colsum(dn1 * xn)

    tk_in = _tile(n_in, 2816)
    grad_x, sums_x = _mm(
        "d_a1_pre_mix", dproj, wf_in, _NT, t, d, n_in, tr, d, tk_in,
        [(x2,) + row_d, (dh1,) + row_d, (sc1,) + vec_d, (g_pre_mix,) + vec_d] + _after(sent_in),
        [(jax.ShapeDtypeStruct((t, d), F32),) + row_d, (jax.ShapeDtypeStruct((8, d), F32),) + sums_d],
        pre_mix_bwd, row_chunk=rc)

    dmod = jnp.concatenate([sums_x[0:2], sums_m[3:4], sums_m[0:2], sums_f[1:2]], axis=0).reshape(N_DEV, n_ada // LANE, LANE)
    ada_rows = -(-(n_ada // LANE) // 8) * 8
    dmod = jnp.pad(dmod, ((0, 0), (0, ada_rows - n_ada // LANE), (0, 0))).reshape(N_DEV * ada_rows, LANE)
    parts = [dmod, _rows(sums_x[2:3]), _rows(sums_m[4:5]), _rows(sums_m[2:3]), _rows(sums_f[2:3]),
             _rows(jnp.sum(dgh_heads, axis=0)), _rows(dg_sgu), _rows(dw_sp), _rows(dbst.T)]
    n_params = sum(p.shape[0] for p in parts)
    parts.append(jnp.full((8, LANE), loss_mine, F32))
    n_common = n_params + 8
    payload = jnp.concatenate(parts + [_rows(dlb)], axis=0)

    moms = [m_w_in, m_w_a_out, m_w_b_out, m_w_o, m_w_ff1, m_w_ff2]
    vars_ = [v_w_in, v_w_a_out, v_w_b_out, v_w_o, v_w_ff1, v_w_ff2]
    big_out = {}

    def big_update(nm, parts):
        k = big_names.index(nm)
        outs = _adamw_big("adamw_" + nm, me_arr, big[k], moms[k][0], vars_[k][0], parts, big_axes[k])
        big_out[nm] = [o[None] for o in outs]
        return outs[0]

    land_ff2, = received_grads("ff2", grad_x)
    done = big_update("w_ff2", [(gw_ff2, land_ff2)])
    land_ff1, = received_grads("ff1", done)
    done = big_update("w_ff1", [(gw_ff1, land_ff1)])
    land_a, land_b, land_o = received_grads("mid", done)
    big_update("w_a_out", [(gw_a, land_a)])
    big_update("w_b_out", [(gw_b, land_b)])
    done = big_update("w_o", [(gw_o, land_o)])

    payload, _ = lax.optimization_barrier((payload, done))
    gathered = _all_gather_small("gather_small_grads", payload)

    dmod_mine = lax.dynamic_slice_in_dim(gathered[:, :N_DEV * ada_rows, :].reshape(N_DEV, N_DEV, ada_rows * LANE),
                                         me, 1, axis=1)[:, 0, :n_ada]
    ada_out = [o[None] for o in _adamw_ada(sc_all.T, dmod_mine, w_ada[0], m_w_ada[0], v_w_ada[0])]

    def pack(b_, g1_, g2_, g3_, g4_, gh_, gs_, ws_, bs_):
        b3 = b_.reshape(N_DEV, n_ada // LANE, LANE)
        b3 = jnp.pad(b3, ((0, 0), (0, ada_rows - n_ada // LANE), (0, 0))).reshape(N_DEV * ada_rows, LANE)
        return jnp.concatenate([b3, _rows(g1_), _rows(g2_), _rows(g3_), _rows(g4_), _rows(gh_), _rows(gs_),
                                _rows(ws_), _rows(bs_), jnp.zeros((8, LANE), F32)], axis=0)

    small_w = (b_ada, g_pre_mix, g_post_mix, g_pre_ffn, g_post_ffn, g_hgrn_norm, g_sgu_norm, w_spatial, b_spatial)
    small_m = (m_b_ada, m_g_pre_mix, m_g_post_mix, m_g_pre_ffn, m_g_post_ffn, m_g_hgrn_norm, m_g_sgu_norm, m_w_spatial, m_b_spatial)
    small_v = (v_b_ada, v_g_pre_mix, v_g_post_mix, v_g_pre_ffn, v_g_post_ffn, v_g_hgrn_norm, v_g_sgu_norm, v_w_spatial, v_b_spatial)
    packed = _adamw_small(gathered[:, :n_common, :], pack(*small_w), pack(*small_m), pack(*small_v))

    def unpack(slab):
        outs, at = [], 0
        b3 = slab[:N_DEV * ada_rows].reshape(N_DEV, ada_rows, LANE)[:, :n_ada // LANE, :]
        outs.append(b3.reshape(b_ada.shape))
        at = N_DEV * ada_rows
        for ref in small_w[1:]:
            n_el = ref.size
            n_r = -(-(n_el // LANE) // 8) * 8
            outs.append(slab[at:at + n_el // LANE].reshape(ref.shape))
            at += n_r
        return outs

    small_out = [unpack(s) for s in packed]
    loss = packed[0][n_params, 0]

    dlb_all = gathered[:, n_common:n_common + 2 * heads, :].reshape(N_DEV, 2, heads, LANE)
    dlb_mine = lax.dynamic_index_in_dim(dlb_all, me, axis=2, keepdims=False)
    lb_out = _adamw_lb(dlb_mine, lb_logits, m_lb_logits, v_lb_logits)

    land_top, = received_grads("in_top", ada_out[0])
    land_bot, = received_grads("in_bot", land_top)
    big_update("w_in", [(gw_in_top, land_top), (gw_in_bot, land_bot)])

    order = ["w_ada", "b_ada", "g_pre_mix", "g_post_mix", "g_pre_ffn", "g_post_ffn", "w_in", "lb_logits", "g_hgrn_norm",
             "w_a_out", "g_sgu_norm", "w_spatial", "b_spatial", "w_b_out", "w_o", "w_ff1", "w_ff2"]
    small_names = ["b_ada", "g_pre_mix", "g_post_mix", "g_pre_ffn", "g_post_ffn", "g_hgrn_norm", "g_sgu_norm", "w_spatial", "b_spatial"]

    def leaf(kind, nm):
        if nm == "w_ada":
            return ada_out[kind]
        if nm == "lb_logits":
            return lb_out[kind]
        if nm in big_out:
            return big_out[nm][kind]
        return small_out[kind][small_names.index(nm)]

    result = [loss, grad_x[None]]
    for kind in range(4):
        result += [leaf(kind, nm) for nm in order]
    return tuple(result)
```

```python
import math

import jax
import jax.numpy as jnp
from jax import lax
from jax.experimental import pallas as pl
from jax.experimental.pallas import tpu as pltpu

F32 = jnp.float32
BF16 = jnp.bfloat16
MESH = pl.DeviceIdType.MESH
HIGHEST = lax.Precision.HIGHEST

N_DEV = 8
HEAD = 128
A_CHUNK = 32
N_MOD = 6
EPS = 1e-6
LANE = 128
VMEM_LIMIT = 60 * 1024 * 1024

ADAM_LR = 0.001
ADAM_B1 = 0.9
ADAM_B2 = 0.999
ADAM_EPS = 1e-08
ADAM_WD = 0.01
ADAM_STEP = 10

_NN = (((1,), (0,)), ((), ()))
_NT = (((1,), (1,)), ((), ()))
_TN = (((0,), (0,)), ((), ()))


def _dot(a, b, dims=_NN, precision=None):
    return lax.dot_general(a, b, dims, preferred_element_type=F32, precision=precision)


def _bdot(a, b, dims=_NN):
    return _dot(a.astype(BF16), b.astype(BF16), dims)


def _params(n_grid):
    return pltpu.CompilerParams(dimension_semantics=("arbitrary",) * n_grid, vmem_limit_bytes=VMEM_LIMIT)


def _dev_index():
    return lax.axis_index("x") * 4 + lax.axis_index("y") * 2 + lax.axis_index("c")


def _dev_coords(i):
    return (i // 4, (i // 2) % 2, i % 2)


def _sigmoid(x):
    return 1.0 / (1.0 + jnp.exp(-x))


def _erf(x):
    ax = jnp.abs(x)
    t = 1.0 / (1.0 + 0.3275911 * ax)
    poly = ((((1.061405429 * t - 1.453152027) * t + 1.421413741) * t - 0.284496736) * t + 0.254829592) * t
    y = 1.0 - poly * jnp.exp(-ax * ax)
    return jnp.where(x < 0, -y, y)


def _gelu_and_grad(x):
    cdf = 0.5 * (1.0 + _erf(x * (2.0 ** -0.5)))
    pdf = jnp.exp(-0.5 * x * x) * (1.0 / math.sqrt(2.0 * math.pi))
    return x * cdf, cdf + x * pdf


def _rms(x):
    return lax.rsqrt(jnp.mean(x * x, axis=-1, keepdims=True) + EPS)


def _colsum(x):
    return jnp.sum(x, axis=0, keepdims=True)


def _tile(n, want):
    if n <= want:
        return n
    t = (want // LANE) * LANE
    while n % t:
        t -= LANE
    assert t > 0, (n, want)
    return t


def _all_gather_small(name, payload):
    rows = payload.shape[0]

    def body(p_ref, out_ref, send_sems, recv_sems, local_sem):
        me = _dev_index()
        mine = pltpu.make_async_copy(p_ref, out_ref.at[me], local_sem)
        mine.start()
        sends = []
        for r in range(1, N_DEV):
            peer = (me + r) % N_DEV
            cp = pltpu.make_async_remote_copy(
                src_ref=p_ref, dst_ref=out_ref.at[me], send_sem=send_sems.at[r - 1], recv_sem=recv_sems.at[r - 1],
                device_id=_dev_coords(peer), device_id_type=MESH)
            cp.start()
            sends.append(cp)
        for r in range(1, N_DEV):
            src = (me + N_DEV - r) % N_DEV
            pltpu.make_async_remote_copy(
                src_ref=p_ref, dst_ref=out_ref.at[src], send_sem=send_sems.at[r - 1], recv_sem=recv_sems.at[r - 1],
                device_id=_dev_coords(src), device_id_type=MESH).wait_recv()
        for cp in sends:
            cp.wait_send()
        mine.wait()

    return pl.pallas_call(
        body, name=name,
        out_shape=jax.ShapeDtypeStruct((N_DEV, rows, LANE), F32),
        in_specs=[pl.BlockSpec(memory_space=pltpu.VMEM)],
        out_specs=pl.BlockSpec(memory_space=pltpu.VMEM),
        scratch_shapes=[pltpu.SemaphoreType.DMA((N_DEV - 1,)), pltpu.SemaphoreType.DMA((N_DEV - 1,)),
                        pltpu.SemaphoreType.DMA],
        compiler_params=pltpu.CompilerParams(vmem_limit_bytes=VMEM_LIMIT),
    )(payload)


def _region(ref, dev, axis, n):
    start = pl.multiple_of(dev * n, LANE if axis == 1 else 16)
    return ref.at[:, pl.ds(start, n)] if axis == 1 else ref.at[pl.ds(start, n), :]


class _Exchange:
    def __init__(self, arrays, out_shapes, sems, start, finish):
        self.arrays, self.out_shapes, self.sems, self.start, self.finish = arrays, out_shapes, sems, start, finish


def _scatter_plan(grads, axes):
    n_w = len(grads)
    lands = []
    for g, ax in zip(grads, axes):
        shp = (g.shape[0], g.shape[1] // N_DEV) if ax == 1 else (g.shape[0] // N_DEV, g.shape[1])
        lands.append(jax.ShapeDtypeStruct((N_DEV - 1,) + shp, BF16))
    widths = [ld.shape[1 + ax] for ld, ax in zip(lands, axes)]

    def copy(w, r, g_refs, l_refs, sems, block, to):
        return pltpu.make_async_remote_copy(
            src_ref=_region(g_refs[w], block, axes[w], widths[w]), dst_ref=l_refs[w].at[r - 1],
            send_sem=sems[0].at[w * (N_DEV - 1) + r - 1], recv_sem=sems[1].at[w * (N_DEV - 1) + r - 1],
            device_id=_dev_coords(to), device_id_type=MESH)

    def start(g_refs, l_refs, sems):
        me = _dev_index()
        for w in range(n_w):
            for r in range(1, N_DEV):
                owner = (me + r) % N_DEV
                copy(w, r, g_refs, l_refs, sems, owner, owner).start()

    def finish(g_refs, l_refs, sems):
        me = _dev_index()
        for w in range(n_w):
            for r in range(1, N_DEV):
                copy(w, r, g_refs, l_refs, sems, me, (me + N_DEV - r) % N_DEV).wait_recv()
        for w in range(n_w):
            for r in range(1, N_DEV):
                copy(w, r, g_refs, l_refs, sems, me, (me + r) % N_DEV).wait_send()

    sems = [pltpu.SemaphoreType.DMA((n_w * (N_DEV - 1),)), pltpu.SemaphoreType.DMA((n_w * (N_DEV - 1),))]
    return _Exchange(list(grads), lands, sems, start, finish)


def _places():
    x, y, c = lax.axis_index("x"), lax.axis_index("y"), lax.axis_index("c")
    return (x, y, c), (x, y, 1 - c), [(1 - x, y), (x, 1 - y), (1 - x, 1 - y)]


def _place_index(p):
    return p[0] * 4 + p[1] * 2 + p[2]


def _gather_stage_plans(fulls, axes):
    n_w = len(fulls)
    widths = [f.shape[ax] // N_DEV for f, ax in zip(fulls, axes)]
    shapes = [jax.ShapeDtypeStruct(f.shape, f.dtype) for f in fulls]

    def copy(per, w, k, f_refs, sems, block, to):
        part = _region(f_refs[w], _place_index(block), axes[w], widths[w])
        return pltpu.make_async_remote_copy(
            src_ref=part, dst_ref=part, send_sem=sems[0].at[w * per + k], recv_sem=sems[1].at[w * per + k],
            device_id=to, device_id_type=MESH)

    def start1(_, f_refs, sems):
        me, sib, chips = _places()
        for w in range(n_w):
            copy(4, w, 0, f_refs, sems, me, sib).start()
            for j, chip in enumerate(chips):
                copy(4, w, 1 + j, f_refs, sems, me, (*chip, me[2])).start()

    def finish1(_, f_refs, sems):
        me, sib, chips = _places()
        for w in range(n_w):
            copy(4, w, 0, f_refs, sems, sib, me).wait_recv()
            for j, chip in enumerate(chips):
                copy(4, w, 1 + j, f_refs, sems, (*chip, me[2]), me).wait_recv()
        for w in range(n_w):
            for k in range(4):
                copy(4, w, k, f_refs, sems, me, sib).wait_send()

    def start2(_, f_refs, sems):
        me, sib, chips = _places()
        for w in range(n_w):
            for j, chip in enumerate(chips):
                copy(3, w, j, f_refs, sems, (*chip, me[2]), sib).start()

    def finish2(_, f_refs, sems):
        me, sib, chips = _places()
        for w in range(n_w):
            for j, chip in enumerate(chips):
                copy(3, w, j, f_refs, sems, (*chip, sib[2]), me).wait_recv()
        for w in range(n_w):
            for j, chip in enumerate(chips):
                copy(3, w, j, f_refs, sems, (*chip, me[2]), sib).wait_send()

    sems1 = [pltpu.SemaphoreType.DMA((n_w * 4,)), pltpu.SemaphoreType.DMA((n_w * 4,))]
    sems2 = [pltpu.SemaphoreType.DMA((n_w * 3,)), pltpu.SemaphoreType.DMA((n_w * 3,))]
    return _Exchange([], shapes, sems1, start1, finish1), _Exchange([], shapes, sems2, start2, finish2)


_HBM = pl.BlockSpec(memory_space=pltpu.HBM)
_SEM = pl.BlockSpec(memory_space=pltpu.SEMAPHORE)
_EFFECT = pltpu.SideEffectType.DATAFLOW_SIDE_EFFECTING


def _split_start(name, plan, landing=None):
    n_in, n_out, n_sem = len(plan.arrays), len(plan.out_shapes), len(plan.sems)

    def body(*refs):
        ins, lands = refs[:n_in], refs[n_in:n_in + n_out]
        sems = refs[n_in + n_out:n_in + n_out + n_sem]
        token = refs[-1]
        plan.start(ins, lands, sems)
        token[...] = jnp.zeros_like(token)

    hbm = lambda a: pltpu.HBM(a.shape, a.dtype)
    results = pl.pallas_call(
        body, name=name,
        out_shape=tuple(plan.sems) + tuple(hbm(a) for a in plan.arrays) + tuple(hbm(a) for a in plan.out_shapes)
        + (jax.ShapeDtypeStruct((8, LANE), F32),),
        in_specs=(_HBM,) * (n_in + n_out),
        out_specs=(_SEM,) * n_sem + (_HBM,) * (n_in + n_out) + (pl.BlockSpec(memory_space=pltpu.VMEM),),
        input_output_aliases={i: n_sem + i for i in range(n_in + n_out)},
        compiler_params=pltpu.CompilerParams(has_side_effects=_EFFECT),
    )(*[pltpu.with_memory_space_constraint(a, pltpu.HBM) for a in plan.arrays],
      *[pltpu.with_memory_space_constraint(a, pltpu.HBM)
        for a in (landing if landing is not None else [lax.empty(a.shape, a.dtype) for a in plan.out_shapes])])
    return results[:n_sem], results[n_sem:n_sem + n_in + n_out], results[-1]


def _split_wait(name, plan, sems, thru, after):
    n_in, n_out, n_sem = len(plan.arrays), len(plan.out_shapes), len(plan.sems)

    def body(*refs):
        ins, lands = refs[:n_in], refs[n_in:n_in + n_out]
        sem_refs = refs[n_in + n_out:n_in + n_out + n_sem]
        plan.finish(ins, lands, sem_refs)

    hbm = lambda a: pltpu.HBM(a.shape, a.dtype)
    results = pl.pallas_call(
        body, name=name,
        out_shape=tuple(hbm(a) for a in plan.arrays) + tuple(hbm(a) for a in plan.out_shapes),
        in_specs=(_HBM,) * (n_in + n_out) + (_SEM,) * n_sem + (pl.BlockSpec(memory_space=pl.ANY),),
        out_specs=(_HBM,) * (n_in + n_out),
        input_output_aliases={i: i for i in range(n_in + n_out)},
        compiler_params=pltpu.CompilerParams(has_side_effects=_EFFECT),
    )(*thru, *sems, after)
    return results[:n_in], results[n_in:]


def _cast_into_full(name, me, w, axis):
    r, c = w.shape
    tr = _tile(r, 256)
    if axis == 1:
        shape, place = (r, c * N_DEV), pl.BlockSpec((tr, c), lambda i, me_ref: (i, me_ref[0]))
    else:
        shape, place = (r * N_DEV, c), pl.BlockSpec((tr, c), lambda i, me_ref: (me_ref[0] * (r // tr) + i, 0))

    def body(me_ref, w_ref, o_ref):
        o_ref[...] = w_ref[...].astype(BF16)

    grid_spec = pltpu.PrefetchScalarGridSpec(
        num_scalar_prefetch=1, grid=(r // tr,),
        in_specs=[pl.BlockSpec((tr, c), lambda i, me_ref: (i, 0))], out_specs=place)
    return pl.pallas_call(body, name=name, grid_spec=grid_spec, out_shape=jax.ShapeDtypeStruct(shape, BF16),
                          compiler_params=_params(1))(me, w)


def _mm(name, a, b, dims, m, n, k, tm, tn, tk, extras, outs, epilogue, row_chunk=None, a_col_block=0):
    ni, nj, nk = m // tm, n // tn, k // tk
    ne, no = len(extras), len(outs)
    if dims == _TN:
        a_spec = pl.BlockSpec((tk, tm), lambda i, j, kk: (kk, i + a_col_block))
    else:
        a_spec = pl.BlockSpec((tm, tk), lambda i, j, kk: (i, kk))
    if dims == _NT:
        b_spec = pl.BlockSpec((tn, tk), lambda i, j, kk: (j, kk))
    else:
        b_spec = pl.BlockSpec((tk, tn), lambda i, j, kk: (kk, j))
    chunks = [slice(None)] if row_chunk is None else [slice(r, r + row_chunk) for r in range(0, tm, row_chunk)]

    def lift(index_map):
        return lambda i, j, kk: index_map(i, j)

    def body(a_ref, b_ref, *rest):
        extra_refs, out_refs, rest = rest[:ne], rest[ne:ne + no], rest[ne + no:]
        i, j, kk = pl.program_id(0), pl.program_id(1), pl.program_id(2)
        if nk == 1:
            part = _dot(a_ref[...], b_ref[...], dims)
            for rows in chunks:
                epilogue(part[rows], i, j, extra_refs, out_refs, rows)
        else:
            acc_ref = rest[0]

            @pl.when(kk == 0)
            def _():
                acc_ref[...] = _dot(a_ref[...], b_ref[...], dims)

            @pl.when(kk > 0)
            def _():
                acc_ref[...] += _dot(a_ref[...], b_ref[...], dims)

            @pl.when(kk == nk - 1)
            def _():
                for rows in chunks:
                    epilogue(acc_ref[rows, :], i, j, extra_refs, out_refs, rows)

    once = dict(pipeline_mode=pl.Buffered(1)) if (row_chunk is not None and nk > 1) else {}
    return pl.pallas_call(
        body, name=name,
        grid=(ni, nj, nk),
        in_specs=[a_spec, b_spec] + [pl.BlockSpec(bs, lift(im), **once) for _, bs, im in extras],
        out_specs=[pl.BlockSpec(bs, lift(im), **once) for _, bs, im in outs],
        out_shape=[sd for sd, _, _ in outs],
        scratch_shapes=[pltpu.VMEM((tm, tn), F32)] if nk > 1 else [],
        compiler_params=_params(3),
    )(a, b, *[arr for arr, _, _ in extras])


def _after(token):
    return [(token, (8, LANE), lambda i, j: (0, 0))]


def _grad_w(name, a, dc, token=None, tm=512, tn=1024, rows=None):
    t = a.shape[0]
    n = dc.shape[1]
    first, m = rows if rows is not None else (0, a.shape[1])
    tm, tn = _tile(m, tm), _tile(n, tn)
    assert first % tm == 0

    def epilogue(acc, i, j, extra_refs, out_refs, rows):
        out_refs[0][...] = acc
        out_refs[1][...] = acc.astype(BF16)

    blk = ((tm, tn), lambda i, j: (i, j))
    return _mm(name, a, dc, _TN, m, n, t, tm, tn, t, _after(token) if token is not None else [],
               [(jax.ShapeDtypeStruct((m, n), F32),) + blk, (jax.ShapeDtypeStruct((m, n), BF16),) + blk], epilogue,
               a_col_block=first // tm)


def _proj_gather(a1, w_shard, order):
    t, d = a1.shape
    nsh = w_shard.shape[1]
    tm = _tile(t, 512)
    n_i = t // tm

    def body(ord_ref, a_ref, wsh_ref, proj_ref, full_ref, bbuf, bsem, send_sems, recv_sems, own_sem):
        s, i = pl.program_id(0), pl.program_id(1)
        me, sib, chips = _places()
        near, far = chips[:2], chips[2]
        steps = ([(me, None, None), (sib, 0, None)]
                 + [((*ch, me[2]), 1 + j, 4 + j) for j, ch in enumerate(near)]
                 + [((*ch, sib[2]), 4 + j, None) for j, ch in enumerate(near)]
                 + [((*far, me[2]), 3, 6), ((*far, sib[2]), 6, None)])
        blocks = [st[0] for st in steps]

        def part(block):
            return _region(full_ref, _place_index(block), 1, nsh)

        def remote(k, block, to, from_shard=False):
            return pltpu.make_async_remote_copy(
                src_ref=wsh_ref if from_shard else part(block), dst_ref=part(block),
                send_sem=send_sems.at[k], recv_sem=recv_sems.at[k], device_id=to, device_id_type=MESH)

        def load(pos):
            src = wsh_ref if pos == 0 else part(blocks[pos])
            return pltpu.make_async_copy(src, bbuf.at[pos % 2], bsem.at[pos % 2])

        own = pltpu.make_async_copy(wsh_ref, part(me), own_sem)

        @pl.when((s == 0) & (i == 0))
        def _():
            own.start()
            remote(0, me, sib, True).start()
            for j, ch in enumerate(chips):
                remote(1 + j, me, (*ch, me[2]), True).start()
            load(0).start()
            load(0).wait()

        for pos in range(1, N_DEV):
            @pl.when((s == pos) & (i == 0))
            def _():
                load(pos).wait()

        for pos in range(N_DEV - 1):
            @pl.when((s == pos) & (i == n_i - 1))
            def _():
                nxt = pos + 1
                block, arrives_on, pass_on_with = steps[nxt]
                remote(arrives_on, block, me).wait_recv()
                if pass_on_with is not None:
                    remote(pass_on_with, block, sib).start()
                load(nxt).start()

        proj_ref[...] = _dot(a_ref[...], bbuf[s % 2])

        @pl.when((s == N_DEV - 1) & (i == n_i - 1))
        def _():
            for k in range(N_DEV - 1):
                remote(k, me, sib, True).wait_send()
            own.wait()

    grid_spec = pltpu.PrefetchScalarGridSpec(
        num_scalar_prefetch=1, grid=(N_DEV, n_i),
        in_specs=[pl.BlockSpec((tm, d), lambda s, i, ord_ref: (i, 0)), pl.BlockSpec(memory_space=pl.ANY)],
        out_specs=[pl.BlockSpec((tm, nsh), lambda s, i, ord_ref: (i, ord_ref[s])), pl.BlockSpec(memory_space=pl.ANY)],
        scratch_shapes=[pltpu.VMEM((2, d, nsh), BF16), pltpu.SemaphoreType.DMA((2,)),
                        pltpu.SemaphoreType.DMA((N_DEV - 1,)), pltpu.SemaphoreType.DMA((N_DEV - 1,)),
                        pltpu.SemaphoreType.DMA])
    return pl.pallas_call(
        body, name="proj_gather", grid_spec=grid_spec,
        out_shape=[jax.ShapeDtypeStruct((t, nsh * N_DEV), F32), jax.ShapeDtypeStruct((d, nsh * N_DEV), BF16)],
        compiler_params=_params(2),
    )(order, a1, w_shard)


def _cast_bf16(name, w):
    r, c = w.shape
    tr = _tile(r, 256)
    return pl.pallas_call(
        lambda w_ref, o_ref: o_ref.__setitem__(Ellipsis, w_ref[...].astype(BF16)), name=name,
        grid=(r // tr,), in_specs=[pl.BlockSpec((tr, c), lambda i: (i, 0))],
        out_specs=pl.BlockSpec((tr, c), lambda i: (i, 0)), out_shape=jax.ShapeDtypeStruct((r, c), BF16),
        compiler_params=_params(1),
    )(w)


def _prep_small(c_row, lb_logits):
    d = c_row.shape[1]
    rows = d // LANE

    def body(c_ref, l_ref, o_ref):
        cv = c_ref[...]
        o_ref[0:rows, :] = cv * _sigmoid(cv)
        lbs = [_sigmoid(l_ref[dr][0:1, :] - l_ref[dr][1:2, :]) for dr in range(2)]
        o_ref[rows:rows + 8, :] = jnp.concatenate(lbs + [jnp.zeros((6, LANE), F32)], axis=0)

    return pl.pallas_call(
        body, name="prep_small", out_shape=jax.ShapeDtypeStruct((rows + 8, LANE), F32),
    )(c_row.reshape(rows, LANE), lb_logits)


def _mod_shard(sc_all, w_ada_shard, b_shard):
    d, n = w_ada_shard.shape
    tn = _tile(n, 512)

    def body(s_ref, w_ref, b_ref, o_ref):
        o_ref[...] = _dot(s_ref[...], w_ref[...], precision=HIGHEST) + b_ref[...]

    return pl.pallas_call(
        body, name="mod_shard", grid=(n // tn,),
        in_specs=[pl.BlockSpec((N_DEV, d), lambda j: (0, 0)), pl.BlockSpec((d, tn), lambda j: (0, j)),
                  pl.BlockSpec((1, tn), lambda j: (0, j))],
        out_specs=pl.BlockSpec((N_DEV, tn), lambda j: (0, j)),
        out_shape=jax.ShapeDtypeStruct((N_DEV, n), F32), compiler_params=_params(1),
    )(sc_all, w_ada_shard, b_shard)


def _norm_mod(x, gain, shift, scale):
    t, d = x.shape
    tm = _tile(t, 512)

    def body(x_ref, g_ref, sh_ref, sc_ref, o_ref):
        xv = x_ref[...]
        o_ref[...] = ((xv * _rms(xv) * g_ref[...]) * (1.0 + sc_ref[...]) + sh_ref[...]).astype(BF16)

    vec = pl.BlockSpec((1, d), lambda i: (0, 0))
    return pl.pallas_call(
        body, name="norm_mod", grid=(t // tm,),
        in_specs=[pl.BlockSpec((tm, d), lambda i: (i, 0)), vec, vec, vec],
        out_specs=pl.BlockSpec((tm, d), lambda i: (i, 0)), out_shape=jax.ShapeDtypeStruct((t, d), BF16),
        compiler_params=_params(1),
    )(x, gain, shift, scale)


def _chunk_masks():
    row = lax.broadcasted_iota(jnp.int32, (HEAD, HEAD), 0)
    col = lax.broadcasted_iota(jnp.int32, (HEAD, HEAD), 1)
    same = (row // A_CHUNK) == (col // A_CHUNK)
    return same & (col <= row), same & (col >= row)


def _ones(mask):
    return jnp.where(mask, 1.0, 0.0).astype(BF16)


def _dot_split(ones_bf16, x):
    hi = x.astype(BF16)
    lo = (x - hi.astype(F32)).astype(BF16)
    return _dot(ones_bf16, hi) + _dot(ones_bf16, lo)


def _hgrn_block(direction, f, lb, cum2):
    sf = _sigmoid(f)
    big_f = lb + (1.0 - lb) * sf
    k = (1.0 - lb) * (1.0 - sf)
    lf = jnp.log(big_f)
    both = _dot_split(cum2, lf)
    cf, cr = both[:HEAD], both[HEAD:]
    b, rest = (cf, cr - lf) if direction == 0 else (cr, cf - lf)
    return k, sf, big_f, jnp.exp(b), jnp.exp(-b), jnp.exp(rest)


def _hgrn_fwd(proj, lb, g_norm, width, after):
    t = proj.shape[0]
    heads = width // HEAD
    nb, nc = t // HEAD, t // A_CHUNK
    ua = 4 if nb % 4 == 0 else (2 if nb % 2 == 0 else 1)
    ub = 16 if nc % 16 == 0 else (8 if nc % 8 == 0 else 4)
    q_scale = HEAD ** -0.5

    def body(q_ref, ffw_ref, fbw_ref, v_ref, og_ref, lb_ref, g_ref, *rest):
        outa_ref, osum_ref, qd_s, ke_s, dc_s, o_s = rest[len(after):]
        tril, triu = _chunk_masks()
        cum2 = jnp.concatenate([_ones(tril), _ones(triu)], axis=0)
        f_refs = (ffw_ref, fbw_ref)
        lbs = (lb_ref[0:1, :], lb_ref[1:2, :])

        def phase_a(it, carry):
            loaded = []
            for u in range(ua):
                rows = pl.ds(pl.multiple_of((it * ua + u) * HEAD, HEAD), HEAD)
                loaded.append((rows, q_ref[rows, :], v_ref[rows, :], ffw_ref[rows, :], fbw_ref[rows, :]))
            chains = [(d, rows, qv * q_scale, vv.astype(BF16), fv)
                      for rows, qv, vv, f0, f1 in loaded for d, fv in ((0, f0), (1, f1))]
            blocks = [_hgrn_block(d, fv, lbs[d], cum2) for d, _, _, _, fv in chains]
            scaled = [(qv * eb, k * enb, k * erest, eb * erest)
                      for (_, _, qv, _, _), (k, _, _, eb, enb, erest) in zip(chains, blocks)]
            atts = [jnp.where(tril if d == 0 else triu, _bdot(qd, kd, _NT), 0.0)
                    for (d, _, _, _, _), (qd, kd, _, _) in zip(chains, scaled)]
            intras = [_bdot(att, vv) for att, (_, _, _, vv, _) in zip(atts, chains)]
            results = [(d, rows, o_intra, qd.astype(BF16), ke.astype(BF16), decay)
                       for (d, rows, _, _, _), (qd, _, ke, decay), o_intra in zip(chains, scaled, intras)]
            for d, rows, o_intra, qd16, ke16, decay in results:
                o_s[d, rows, :] = o_intra
                qd_s[d, rows, :] = qd16
                ke_s[d, rows, :] = ke16
                dc_s[d, rows, :] = decay
            return carry

        lax.fori_loop(0, nb // ua, phase_a, 0)

        def phase_b(it, states):
            loaded = []
            for u in range(ub):
                n = it * ub + u
                for d in range(2):
                    c = n if d == 0 else nc - 1 - n
                    start = pl.multiple_of(c * A_CHUNK, A_CHUNK)
                    rows = pl.ds(start, A_CHUNK)
                    loaded.append((d, rows, qd_s[d, rows, :], ke_s[d, rows, :], v_ref[rows, :],
                                   dc_s[d, pl.ds(start, 1), :], o_s[d, rows, :]))
            increments = [_dot(vv.astype(BF16), ke16, _TN) for _, _, _, ke16, vv, _, _ in loaded]
            states = list(states)
            befores = []
            for (d, _, _, _, _, decay, _), inc in zip(loaded, increments):
                befores.append(states[d].astype(BF16))
                states[d] = states[d] * decay + inc
            inters = [_dot(qd16, before, _NT) for (_, _, qd16, _, _, _, _), before in zip(loaded, befores)]
            for (d, rows, _, _, _, _, o_intra), o_inter in zip(loaded, inters):
                o_s[d, rows, :] = o_intra + o_inter
            return tuple(states)

        zero_state = jnp.zeros((HEAD, HEAD), F32)
        lax.fori_loop(0, nc // ub, phase_b, (zero_state, zero_state))

        def phase_c(i, carry):
            rows = pl.ds(pl.multiple_of(i * HEAD, HEAD), HEAD)
            o = o_s[0, rows, :] + o_s[1, rows, :]
            osum_ref[rows, :] = o
            og = og_ref[rows, :]
            outa_ref[rows, :] = (o * _rms(o) * g_ref[...] * (og * _sigmoid(og))).astype(BF16)
            return carry

        lax.fori_loop(0, nb, phase_c, 0)

    def col(p):
        return pl.BlockSpec((t, HEAD), lambda h: (0, p * heads + h))

    return pl.pallas_call(
        body, name="hgrn_fwd", grid=(heads,),
        in_specs=[col(0), col(1), col(2), col(3), col(4),
                  pl.BlockSpec((2, HEAD), lambda h: (0, h)), pl.BlockSpec((1, HEAD), lambda h: (0, 0))]
        + [pl.BlockSpec(memory_space=pl.ANY)] * len(after),
        out_specs=[pl.BlockSpec((t, HEAD), lambda h: (0, h)), pl.BlockSpec((t, HEAD), lambda h: (0, h))],
        out_shape=[jax.ShapeDtypeStruct((t, width), BF16), jax.ShapeDtypeStruct((t, width), F32)],
        scratch_shapes=[pltpu.VMEM((2, t, HEAD), BF16), pltpu.VMEM((2, t, HEAD), BF16), pltpu.VMEM((2, t, HEAD), F32),
                        pltpu.VMEM((2, t, HEAD), F32)],
        compiler_params=_params(1),
    )(proj, proj, proj, proj, proj, lb, g_norm, *after)


def _sgu_core(u_pre, v_pre, g_v, ws_ref, bst):
    u, du = _gelu_and_grad(u_pre)
    v, dv = _gelu_and_grad(v_pre)
    mu = jnp.mean(v, axis=-1, keepdims=True)
    dlt = v - mu
    rstd = lax.rsqrt(jnp.mean(dlt * dlt, axis=-1, keepdims=True) + EPS)
    vhat = dlt * rstd
    vn = vhat * g_v
    groups = vn.shape[1] // HEAD
    cols = []
    for g in range(groups):
        vm_g = _bdot(ws_ref[g], vn[:, g * HEAD:(g + 1) * HEAD]) + bst[:, g:g + 1]
        cols.append(vm_g)
    return u, du, dv, vhat, rstd, vn, jnp.concatenate(cols, axis=1)


def _sgu_fwd(proj, g_v, w_s, bst, width, z_block):
    t = proj.shape[0]

    def body(u_ref, v_ref, g_ref, ws_ref, bst_ref, o_ref):
        u, _, _, _, _, _, vm = _sgu_core(u_ref[...], v_ref[...], g_ref[...], ws_ref, bst_ref[...])
        o_ref[...] = (u * vm).astype(BF16)

    groups = width // HEAD
    return pl.pallas_call(
        body, name="sgu_fwd", grid=(t // HEAD,),
        in_specs=[pl.BlockSpec((HEAD, width), lambda i: (i, z_block)), pl.BlockSpec((HEAD, width), lambda i: (i, z_block + 1)),
                  pl.BlockSpec((1, width), lambda i: (0, 0)), pl.BlockSpec((groups, HEAD, HEAD), lambda i: (0, 0, 0)),
                  pl.BlockSpec((HEAD, groups), lambda i: (0, 0))],
        out_specs=pl.BlockSpec((HEAD, width), lambda i: (i, 0)),
        out_shape=jax.ShapeDtypeStruct((t, width), BF16), compiler_params=_params(1),
    )(proj, proj, g_v, w_s, bst)


def _sgu_bwd(proj, dout_b, dproj, g_v, w_s, w_st, bst, width, z_block):
    t = proj.shape[0]
    groups = width // HEAD
    nblk = t // HEAD

    def body(u_ref, v_ref, do_ref, g_ref, ws_ref, wst_ref, bst_ref, dproj_hbm,
             dz_ref, dg_ref, dws_ref, dbst_ref, res_s):
        i, p = pl.program_id(0), pl.program_id(1)

        @pl.when((i == 0) & (p == 0))
        def _():
            dg_ref[...] = jnp.zeros_like(dg_ref)
            dws_ref[...] = jnp.zeros_like(dws_ref)
            dbst_ref[...] = jnp.zeros_like(dbst_ref)

        @pl.when(p == 0)
        def _():
            g_v = g_ref[...]
            u, du, dv, vhat, rstd, vn, vm = _sgu_core(u_ref[...], v_ref[...], g_v, ws_ref, bst_ref[...])
            dout = do_ref[...].astype(F32)
            res_s[0] = (dout * vm * du).astype(BF16)
            dvm = dout * u
            dvn_cols = []
            for g in range(groups):
                sl = slice(g * HEAD, (g + 1) * HEAD)
                dvm_g = dvm[:, sl]
                dbst_ref[:, g:g + 1] += jnp.sum(dvm_g, axis=1, keepdims=True)
                dws_ref[g] += _bdot(dvm_g, vn[:, sl], _NT)
                dvn_cols.append(_bdot(wst_ref[g], dvm_g))
            dvn = jnp.concatenate(dvn_cols, axis=1)
            dg_ref[...] += _colsum(dvn * vhat)
            dvh = dvn * g_v
            dvg = rstd * (dvh - jnp.mean(dvh, axis=-1, keepdims=True)
                          - vhat * jnp.mean(dvh * vhat, axis=-1, keepdims=True))
            res_s[1] = (dvg * dv).astype(BF16)

        dz_ref[...] = res_s[p]

    n_in = dproj.shape[1]
    return pl.pallas_call(
        body, name="sgu_bwd", grid=(nblk, 2),
        in_specs=[pl.BlockSpec((HEAD, width), lambda i, p: (i, z_block)),
                  pl.BlockSpec((HEAD, width), lambda i, p: (i, z_block + 1)),
                  pl.BlockSpec((HEAD, width), lambda i, p: (i, 0)),
                  pl.BlockSpec((1, width), lambda i, p: (0, 0)),
                  pl.BlockSpec((groups, HEAD, HEAD), lambda i, p: (0, 0, 0)),
                  pl.BlockSpec((groups, HEAD, HEAD), lambda i, p: (0, 0, 0)),
                  pl.BlockSpec((HEAD, groups), lambda i, p: (0, 0)),
                  pl.BlockSpec(memory_space=pl.ANY)],
        out_specs=[pl.BlockSpec((HEAD, width), lambda i, p: (i, z_block + p)),
                   pl.BlockSpec((1, width), lambda i, p: (0, 0)),
                   pl.BlockSpec((groups, HEAD, HEAD), lambda i, p: (0, 0, 0)),
                   pl.BlockSpec((HEAD, groups), lambda i, p: (0, 0))],
        out_shape=[jax.ShapeDtypeStruct((t, n_in), BF16), jax.ShapeDtypeStruct((1, width), F32),
                   jax.ShapeDtypeStruct((groups, HEAD, HEAD), F32), jax.ShapeDtypeStruct((HEAD, groups), F32)],
        scratch_shapes=[pltpu.VMEM((2, HEAD, width), BF16)],
        input_output_aliases={7: 0},
        compiler_params=_params(2),
    )(proj, proj, dout_b, g_v, w_s, w_st, bst, dproj)


def _hgrn_bwd(proj, osum, dout_a, dproj, lb, g_norm, width, after):
    t = proj.shape[0]
    heads = width // HEAD
    nb = t // HEAD
    cpb = HEAD // A_CHUNK
    ubk = 4 if nb % 4 == 0 else (2 if nb % 2 == 0 else 1)
    q_scale = HEAD ** -0.5

    def body(q_ref, ffw_ref, fbw_ref, v_ref, og_ref, osum_ref, douta_ref, lb_ref, g_ref, dproj_hbm, after_hbm,
             out_ref, dgh_ref, dlb_ref, do_s, dq_s, dv_s, res_s, ck_s):
        p = pl.program_id(1)
        f_refs = (ffw_ref, fbw_ref)

        @pl.when(p == 0)
        def _():
            tril, triu = _chunk_masks()
            cum2 = jnp.concatenate([_ones(tril), _ones(triu)], axis=0)
            g_row = g_ref[...]

            def pass_norm(i, dgh):
                rows = pl.ds(pl.multiple_of(i * HEAD, HEAD), HEAD)
                o = osum_ref[rows, :]
                r = _rms(o)
                oh = o * r
                og = og_ref[rows, :]
                sg = _sigmoid(og)
                dout = douta_ref[rows, :].astype(F32)
                don = dout * (og * sg)
                res_s[4, rows, :] = (dout * (oh * g_row) * (sg * (1.0 + og * (1.0 - sg)))).astype(BF16)
                doh = don * g_row
                do_s[rows, :] = r * (doh - oh * jnp.mean(doh * oh, axis=-1, keepdims=True))
                return dgh + _colsum(don * oh)

            dgh_ref[...] = lax.fori_loop(0, nb, pass_norm, jnp.zeros((1, HEAD), F32))

            lbs = (lb_ref[0:1, :], lb_ref[1:2, :])
            zero_state = jnp.zeros((HEAD, HEAD), F32)

            def chunk_order(d):
                return list(range(cpb)) if d == 0 else list(range(cpb - 1, -1, -1))

            def chunk(x, j):
                return x[j * A_CHUNK:(j + 1) * A_CHUNK, :]

            def decay_row(e_big, j):
                return e_big[j * A_CHUNK:j * A_CHUNK + 1, :]

            def cat(parts):
                return jnp.concatenate([parts[j] for j in range(cpb)], axis=0)

            def block_states(d, start, incs, e_big):
                befores, st = {}, start
                for j in chunk_order(d):
                    befores[j] = st
                    st = st * decay_row(e_big, j) + incs[j]
                return befores, st

            def pass_states(it, states):
                loaded = []
                for u in range(ubk):
                    for d in range(2):
                        blk = it * ubk + u if d == 0 else nb - 1 - (it * ubk + u)
                        rows = pl.ds(pl.multiple_of(blk * HEAD, HEAD), HEAD)
                        loaded.append((d, blk, f_refs[d][rows, :], v_ref[rows, :]))
                blocks = [_hgrn_block(d, fv, lbs[d], cum2) for d, _, fv, _ in loaded]
                incs = [{j: _bdot(chunk(vv, j), chunk(k * erest, j), _TN) for j in range(cpb)}
                        for (_, _, _, vv), (k, _, _, _, _, erest) in zip(loaded, blocks)]
                states, starts = list(states), []
                for (d, _, _, _), (_, _, _, eb, _, erest), inc in zip(loaded, blocks, incs):
                    starts.append(states[d])
                    states[d] = block_states(d, states[d], inc, eb * erest)[1]
                for (d, blk, _, _), start in zip(loaded, starts):
                    ck_s[d, blk] = start
                return tuple(states)

            lax.fori_loop(0, nb // ubk, pass_states, (zero_state, zero_state))

            def pass_back(it, carry):
                gts, dlb = [carry[0], carry[1]], carry[2]
                loaded = []
                for u, d in ((u, d) for u in range(ubk) for d in range(2)):
                    blk = nb - 1 - (it * ubk + u) if d == 0 else it * ubk + u
                    rows = pl.ds(pl.multiple_of(blk * HEAD, HEAD), HEAD)
                    loaded.append((d, rows, f_refs[d][rows, :], q_ref[rows, :], v_ref[rows, :], do_s[rows, :], ck_s[d, blk]))
                blocks = [_hgrn_block(d, fv, lbs[d], cum2) for d, _, fv, _, _, _, _ in loaded]
                scaled = []
                for (_, _, _, qv, _, _, _), (k, _, _, eb, enb, erest) in zip(loaded, blocks):
                    qh = qv * q_scale
                    scaled.append((qh, qh * eb, k * enb, k * erest, eb * erest))
                masks = [tril if d == 0 else triu for d, *_ in loaded]
                atts = [jnp.where(m, _bdot(qd, kd, _NT), 0.0) for m, (_, qd, kd, _, _) in zip(masks, scaled)]
                datts = [jnp.where(m, _bdot(do, vv, _NT), 0.0) for m, (_, _, _, _, vv, do, _) in zip(masks, loaded)]
                dvs = [_bdot(att, do, _TN) for att, (_, _, _, _, _, do, _) in zip(atts, loaded)]
                dqds = [_bdot(datt, kd) for datt, (_, _, kd, _, _) in zip(datts, scaled)]
                dkds = [_bdot(datt, qd, _TN) for datt, (_, qd, _, _, _) in zip(datts, scaled)]
                s_incs = [{j: _bdot(chunk(vv, j), chunk(ke, j), _TN) for j in range(cpb)}
                          for (_, _, _, _, vv, _, _), (_, _, _, ke, _) in zip(loaded, scaled)]
                g_incs = [{j: _bdot(chunk(do, j), chunk(qd, j), _TN) for j in range(cpb)}
                          for (_, _, _, _, _, do, _), (_, qd, _, _, _) in zip(loaded, scaled)]
                befores, afters, g_at = [], [], []
                for (d, _, _, _, _, _, ck), (_, _, _, _, e_big), s_inc, g_inc in zip(loaded, scaled, s_incs, g_incs):
                    order = chunk_order(d)
                    before, after = block_states(d, ck, s_inc, e_big)
                    befores.append(before)
                    afters.append({j: (before[order[n + 1]] if n + 1 < cpb else after) for n, j in enumerate(order)})
                    at, gt = {}, gts[d]
                    for j in reversed(order):
                        at[j] = gt
                        gt = gt * decay_row(e_big, j) + g_inc[j]
                    gts[d] = gt
                    g_at.append(at)
                dqd_i = [{j: _bdot(chunk(do, j), before[j]) for j in range(cpb)}
                         for (_, _, _, _, _, do, _), before in zip(loaded, befores)]
                dv_i = [{j: _bdot(chunk(ke, j), at[j], _NT) for j in range(cpb)}
                        for (_, _, _, ke, _), at in zip(scaled, g_at)]
                dke = [{j: _bdot(chunk(vv, j), at[j]) for j in range(cpb)}
                       for (_, _, _, _, vv, _, _), at in zip(loaded, g_at)]
                results, new = [], []
                for n, ((d, rows, _, _, _, _, _), (k, sf, big_f, eb, enb, erest), (qh, _, _, _, _)) in enumerate(
                        zip(loaded, blocks, scaled)):
                    dqh = (dqds[n] + cat(dqd_i[n])) * eb
                    dk = dkds[n] * enb + cat(dke[n]) * erest
                    carry_rows = {j: jnp.broadcast_to(_colsum(g_at[n][j] * afters[n][j]), (A_CHUNK, HEAD))
                                  for j in range(cpb)}
                    dlf = _dot_split(_ones(triu if d == 0 else tril), qh * dqh - k * dk) + cat(carry_rows)
                    common = dlf / big_f - dk
                    results.append((d, rows, (k * sf * common).astype(BF16), dqh.astype(BF16),
                                    (dvs[n] + cat(dv_i[n])).astype(BF16)))
                    new.append(_colsum((1.0 - sf) * common))
                for d, rows, df16, dq16, dv16 in results:
                    res_s[1 + d, rows, :] = df16
                    dq_s[d, rows, :] = dq16
                    dv_s[d, rows, :] = dv16
                per_dir = [sum(c for (d, *_), c in zip(loaded, new) if d == dd) for dd in range(2)]
                return gts[0], gts[1], dlb + jnp.concatenate(per_dir, axis=0)

            dlb_ref[...] = lax.fori_loop(0, nb // ubk, pass_back,
                                         (zero_state, zero_state, jnp.zeros((2, HEAD), F32)))[2]

            def pass_out(i, carry):
                rows = pl.ds(pl.multiple_of(i * HEAD, HEAD), HEAD)
                dq = dq_s[0, rows, :].astype(F32) + dq_s[1, rows, :].astype(F32)
                res_s[0, rows, :] = (dq * q_scale).astype(BF16)
                res_s[3, rows, :] = (dv_s[0, rows, :].astype(F32) + dv_s[1, rows, :].astype(F32)).astype(BF16)
                return carry

            lax.fori_loop(0, nb, pass_out, 0)

        out_ref[...] = res_s[p]

    def col(pp):
        return pl.BlockSpec((t, HEAD), lambda h, p: (0, pp * heads + h))

    n_in = dproj.shape[1]
    any_spec = pl.BlockSpec(memory_space=pl.ANY)
    return pl.pallas_call(
        body, name="hgrn_bwd", grid=(heads, 5),
        in_specs=[col(0), col(1), col(2), col(3), col(4),
                  pl.BlockSpec((t, HEAD), lambda h, p: (0, h)), pl.BlockSpec((t, HEAD), lambda h, p: (0, h)),
                  pl.BlockSpec((2, HEAD), lambda h, p: (0, h)), pl.BlockSpec((1, HEAD), lambda h, p: (0, 0)),
                  any_spec, any_spec],
        out_specs=[pl.BlockSpec((t, HEAD), lambda h, p: (0, p * heads + h)),
                   pl.BlockSpec((None, 1, HEAD), lambda h, p: (h, 0, 0)),
                   pl.BlockSpec((2, HEAD), lambda h, p: (0, h))],
        out_shape=[jax.ShapeDtypeStruct((t, n_in), BF16), jax.ShapeDtypeStruct((heads, 1, HEAD), F32),
                   jax.ShapeDtypeStruct((2, width), F32)],
        scratch_shapes=[pltpu.VMEM((t, HEAD), F32), pltpu.VMEM((2, t, HEAD), BF16), pltpu.VMEM((2, t, HEAD), BF16),
                        pltpu.VMEM((5, t, HEAD), BF16), pltpu.VMEM((2, nb, HEAD, HEAD), F32)],
        input_output_aliases={9: 0},
        compiler_params=_params(2),
    )(proj, proj, proj, proj, proj, osum, dout_a, lb, g_norm, dproj, after)


def _adamw(w, g, m, v):
    m = ADAM_B1 * m + (1.0 - ADAM_B1) * g
    v = ADAM_B2 * v + (1.0 - ADAM_B2) * (g * g)
    m_hat = m / (1.0 - ADAM_B1 ** ADAM_STEP)
    v_hat = v / (1.0 - ADAM_B2 ** ADAM_STEP)
    delta = -ADAM_LR * (m_hat / (jnp.sqrt(v_hat) + ADAM_EPS) + ADAM_WD * w)
    return delta, m, v


def _adamw_big(name, me, w, m, v, parts, axis):
    r, c = w.shape
    n_parts = len(parts)
    tr = _tile(r // n_parts, 128)
    per = r // n_parts // tr
    assert axis == 1 or n_parts == 1

    def body(me_ref, w_ref, m_ref, v_ref, *rest):
        g_refs, l_refs = rest[:n_parts], rest[n_parts:2 * n_parts]
        og_ref, od_ref, om_ref, ov_ref = rest[2 * n_parts:]
        g = None
        for p in range(n_parts):
            total = g_refs[p][...]
            for s in range(N_DEV - 1):
                total = total + l_refs[p][s].astype(F32)
            g = total if p == 0 else jnp.where(pl.program_id(0) // per == p, total, g)
        og_ref[...] = g
        od_ref[...], om_ref[...], ov_ref[...] = _adamw(w_ref[...], g, m_ref[...], v_ref[...])

    def within(p, i):
        return jnp.clip(i - p * per, 0, per - 1)

    shard = pl.BlockSpec((tr, c), lambda i, me_ref: (i, 0))
    if axis == 1:
        own = [pl.BlockSpec((tr, c), lambda i, me_ref, p=p: (within(p, i), me_ref[0])) for p in range(n_parts)]
    else:
        own = [pl.BlockSpec((tr, c), lambda i, me_ref: (me_ref[0] * (r // tr) + i, 0))]
    landed = [pl.BlockSpec((N_DEV - 1, tr, c), lambda i, me_ref, p=p: (0, within(p, i), 0)) for p in range(n_parts)]
    grid_spec = pltpu.PrefetchScalarGridSpec(
        num_scalar_prefetch=1, grid=(r // tr,),
        in_specs=[shard, shard, shard] + own + landed, out_specs=[shard] * 4)
    return pl.pallas_call(
        body, name=name, grid_spec=grid_spec, out_shape=[jax.ShapeDtypeStruct((r, c), F32)] * 4,
        compiler_params=_params(1),
    )(me, w, m, v, *[g for g, _ in parts], *[ld for _, ld in parts])


def _adamw_ada(sct, dmod_mine, w, m, v):
    d, n = w.shape
    tr = _tile(d, 256)

    def body(s_ref, dm_ref, w_ref, m_ref, v_ref, og_ref, od_ref, om_ref, ov_ref):
        g = _dot(s_ref[...], dm_ref[...], precision=HIGHEST)
        og_ref[...] = g
        od_ref[...], om_ref[...], ov_ref[...] = _adamw(w_ref[...], g, m_ref[...], v_ref[...])

    blk = pl.BlockSpec((tr, n), lambda i: (i, 0))
    return pl.pallas_call(
        body, name="adamw_ada", grid=(d // tr,),
        in_specs=[pl.BlockSpec((tr, N_DEV), lambda i: (i, 0)), pl.BlockSpec((N_DEV, n), lambda i: (0, 0)), blk, blk, blk],
        out_specs=[blk] * 4, out_shape=[jax.ShapeDtypeStruct((d, n), F32)] * 4, compiler_params=_params(1),
    )(sct, dmod_mine, w, m, v)


def _adamw_small(gathered, w, m, v):
    def body(g_ref, w_ref, m_ref, v_ref, og_ref, od_ref, om_ref, ov_ref):
        g = g_ref[0]
        for s in range(1, N_DEV):
            g = g + g_ref[s]
        og_ref[...] = g
        od_ref[...], om_ref[...], ov_ref[...] = _adamw(w_ref[...], g, m_ref[...], v_ref[...])

    return pl.pallas_call(
        body, name="adamw_small", out_shape=[jax.ShapeDtypeStruct(w.shape, F32)] * 4,
        compiler_params=pltpu.CompilerParams(vmem_limit_bytes=VMEM_LIMIT),
    )(gathered, w, m, v)


def _adamw_lb(dlb_mine, lb_logits, m, v):
    def body(d_ref, l_ref, m_ref, v_ref, og_ref, od_ref, om_ref, ov_ref):
        dlb = d_ref[0]
        for s in range(1, N_DEV):
            dlb = dlb + d_ref[s]
        for dr in range(2):
            lb = _sigmoid(l_ref[dr][0:1, :] - l_ref[dr][1:2, :])
            d0 = dlb[dr:dr + 1] * lb * (1.0 - lb)
            g = jnp.concatenate([d0, -d0], axis=0)
            og_ref[dr] = g
            od_ref[dr], om_ref[dr], ov_ref[dr] = _adamw(l_ref[dr], g, m_ref[dr], v_ref[dr])

    return pl.pallas_call(body, name="adamw_lb", out_shape=[jax.ShapeDtypeStruct(lb_logits.shape, F32)] * 4,
                          )(dlb_mine, lb_logits, m, v)


def _rows(a, pad_to=8):
    flat = a.reshape(-1, LANE)
    pad = (-flat.shape[0]) % pad_to
    return jnp.pad(flat, ((0, pad), (0, 0))) if pad else flat


def kernel(x, c, w_ada, b_ada, g_pre_mix, g_post_mix, g_pre_ffn, g_post_ffn, w_in, lb_logits, g_hgrn_norm, w_a_out, g_sgu_norm, w_spatial, b_spatial, w_b_out, w_o, w_ff1, w_ff2, loss_target, m_w_ada, m_b_ada, m_g_pre_mix, m_g_post_mix, m_g_pre_ffn, m_g_post_ffn, m_w_in, m_lb_logits, m_g_hgrn_norm, m_w_a_out, m_g_sgu_norm, m_w_spatial, m_b_spatial, m_w_b_out, m_w_o, m_w_ff1, m_w_ff2, v_w_ada, v_b_ada, v_g_pre_mix, v_g_post_mix, v_g_pre_ffn, v_g_post_ffn, v_w_in, v_lb_logits, v_g_hgrn_norm, v_w_a_out, v_g_sgu_norm, v_w_spatial, v_b_spatial, v_w_b_out, v_w_o, v_w_ff1, v_w_ff2):
    t, d = x.shape[1], x.shape[2]
    n_in = w_in.shape[2] * N_DEV
    width = (n_in - 2 * d) // 7
    heads = width // HEAD
    assert heads == N_DEV and width % LANE == 0
    d_ff = w_ff1.shape[2] * N_DEV
    n_ada = w_ada.shape[2]
    me = _dev_index()
    me_arr = me.reshape(1).astype(jnp.int32)
    x2, tgt = x[0], loss_target[0]

    big = [w_in[0], w_a_out[0], w_b_out[0], w_o[0], w_ff1[0], w_ff2[0]]
    big_axes = [1, 1, 1, 0, 1, 0]
    big_names = ["w_in", "w_a_out", "w_b_out", "w_o", "w_ff1", "w_ff2"]
    w_in16 = _cast_bf16("cast_w_in", big[0])
    own_parts = [_cast_into_full("cast_" + nm, me_arr, w, ax) for nm, w, ax in zip(big_names[1:], big[1:], big_axes[1:])]

    c_rows = d // LANE
    small = _all_gather_small("gather_c_lb", _prep_small(c[0:1], lb_logits))
    sc_all = small[:, :c_rows, :].reshape(N_DEV, d)
    lb = jnp.transpose(small[:, c_rows:c_rows + 2, :], (1, 0, 2)).reshape(2, width)
    b_shard = lax.dynamic_slice_in_dim(b_ada, me * n_ada, n_ada, axis=1)
    mod_sh = _mod_shard(sc_all, w_ada[0], b_shard)
    mod_all = _all_gather_small("gather_mod", _rows(mod_sh))
    mod_all = mod_all[:, :N_DEV * n_ada // LANE, :].reshape(N_DEV, N_DEV, n_ada)
    mod6 = lax.dynamic_index_in_dim(mod_all, me, axis=1, keepdims=False).reshape(N_MOD, d)
    sh1, sc1, gt1, sh2, sc2, gt2 = [mod6[i:i + 1] for i in range(N_MOD)]

    a1 = _norm_mod(x2, g_pre_mix, sh1, sc1)
    tm = _tile(t, 512)

    def store_bf16(acc, i, j, extra_refs, out_refs, rows):
        out_refs[0][...] = acc.astype(BF16)

    xq, yq, cq = lax.axis_index("x"), lax.axis_index("y"), lax.axis_index("c")
    chips = [(1 - xq, yq), (xq, 1 - yq), (1 - xq, 1 - yq)]
    order = jnp.stack([me, 4 * xq + 2 * yq + 1 - cq]
                      + [4 * a + 2 * b + cq for a, b in chips[:2]] + [4 * a + 2 * b + 1 - cq for a, b in chips[:2]]
                      + [4 * chips[2][0] + 2 * chips[2][1] + cq, 4 * chips[2][0] + 2 * chips[2][1] + 1 - cq]).astype(jnp.int32)
    proj, wf_in = _proj_gather(a1, w_in16, order)

    proj, own_parts = lax.optimization_barrier((proj, own_parts))
    gathers = {}
    for key, lo, hi in (("mid", 1, 4), ("ff1", 4, 5), ("ff2", 5, 6)):
        far, near = _gather_stage_plans(own_parts[lo - 1:hi - 1], big_axes[lo:hi])
        gathers[key] = [far, near, _split_start("gather_%s_start" % key, far, landing=own_parts[lo - 1:hi - 1])]

    def pass_on(key, after):
        far, near, (sems, thru, _) = gathers[key]
        parts = _split_wait("gather_%s_wait" % key, far, sems, thru, after)[1]
        gathers[key].append(_split_start("pass_%s_start" % key, near, landing=list(parts)))
        return gathers[key][3][2]

    def gathered_weights(key, after):
        near, (sems, thru, _) = gathers[key][1], gathers[key][3]
        return _split_wait("pass_%s_wait" % key, near, sems, thru, after)[1]

    out_a, osum = _hgrn_fwd(proj, lb, g_hgrn_norm, width,
                            after=[gathers[key][2][2] for key in ("mid", "ff1", "ff2")])
    passed_mid = pass_on("mid", out_a)
    z_block = 5
    bst = b_spatial[0].T
    out_b = _sgu_fwd(proj, g_sgu_norm, w_spatial[0], bst, width, z_block)
    wf_a, wf_b, wf_o = gathered_weights("mid", out_b)

    tn_d = _tile(d, 1024)
    blk_d = ((tm, tn_d), lambda i, j: (i, j))
    ga_blk = (5 * width + 2 * width) // tn_d
    gb_blk = ga_blk + d // tn_d

    def merge(acc, i, j, extra_refs, out_refs, rows):
        ga, gb, oa, wa = extra_refs[:4]
        ya = _dot(oa[...], wa[...])
        out_refs[0][...] = ya.astype(BF16)
        out_refs[1][...] = acc.astype(BF16)
        out_refs[2][...] = (_sigmoid(ga[...]) * ya + _sigmoid(gb[...]) * acc).astype(BF16)

    y_a, y_b, merged = _mm(
        "y_ab_merge", out_b, wf_b, _NN, t, d, width, tm, tn_d, width,
        [(proj, (tm, tn_d), lambda i, j: (i, ga_blk + j)), (proj, (tm, tn_d), lambda i, j: (i, gb_blk + j)),
         (out_a, (tm, width), lambda i, j: (i, 0)), (wf_a, (width, tn_d), lambda i, j: (0, j))] + _after(passed_mid),
        [(jax.ShapeDtypeStruct((t, d), BF16),) + blk_d] * 3, merge)

    tr = _tile(t, 512)
    rc = 32 if tr % 32 == 0 else None
    row_d = ((tr, d), lambda i, j: (i, 0))
    vec_d = ((1, d), lambda i, j: (0, 0))

    passed_ff1 = pass_on("ff1", merged)

    def post_mix(acc, i, j, extra_refs, out_refs, rows):
        x_r, gt1_r, g2_r, g3_r, sc2_r, sh2_r = extra_refs[:6]
        h1 = x_r[rows, :] + gt1_r[...] * (acc * _rms(acc) * g2_r[...])
        out_refs[0][rows, :] = acc.astype(BF16)
        out_refs[1][rows, :] = h1
        out_refs[2][rows, :] = ((h1 * _rms(h1) * g3_r[...]) * (1.0 + sc2_r[...]) + sh2_r[...]).astype(BF16)

    mo, h1, a2 = _mm("w_o_post_mix", merged, wf_o, _NN, t, d, d, tr, d, d,
                     [(x2,) + row_d, (gt1,) + vec_d, (g_post_mix,) + vec_d, (g_pre_ffn,) + vec_d, (sc2,) + vec_d, (sh2,) + vec_d]
                     + _after(passed_ff1),
                     [(jax.ShapeDtypeStruct((t, d), BF16),) + row_d, (jax.ShapeDtypeStruct((t, d), F32),) + row_d,
                      (jax.ShapeDtypeStruct((t, d), BF16),) + row_d], post_mix, row_chunk=rc)

    tn_f = _tile(d_ff, 2048)
    blk_f = ((tm, tn_f), lambda i, j: (i, j))

    def relu_sq(acc, i, j, extra_refs, out_refs, rows):
        r = jnp.maximum(acc, 0.0)
        out_refs[0][...] = acc.astype(BF16)
        out_refs[1][...] = (r * r).astype(BF16)

    wf_1, = gathered_weights("ff1", a2)
    hff, act = _mm(
        "ff1", a2, wf_1, _NN, t, d_ff, d, tm, tn_f, d, [],
        [(jax.ShapeDtypeStruct((t, d_ff), BF16),) + blk_f, (jax.ShapeDtypeStruct((t, d_ff), BF16),) + blk_f], relu_sq)
    pass_on("ff2", hff)
    wf_2, = gathered_weights("ff2", act)

    sums_d = ((8, d), lambda i, j: (0, 0))

    def zero_first(sums_r, i, rows):
        if rows.start in (None, 0):
            @pl.when(i == 0)
            def _():
                sums_r[...] = jnp.zeros_like(sums_r)

    def loss_head(acc, i, j, extra_refs, out_refs, rows):
        h1_r, tgt_r, gt2_r, g4_r = extra_refs
        dy_r, dff_r, sums_r = out_refs
        r4 = _rms(acc)
        ffn = acc * r4
        n4 = ffn * g4_r[...]
        err = h1_r[rows, :] + gt2_r[...] * n4 - tgt_r[rows, :]
        dy = err * (1.0 / d)
        dy_r[rows, :] = dy.astype(BF16)
        dn4 = dy * gt2_r[...]
        dffn = dn4 * g4_r[...]
        dff_r[rows, :] = (r4 * (dffn - ffn * jnp.mean(dffn * ffn, axis=-1, keepdims=True))).astype(BF16)
        zero_first(sums_r, i, rows)

        sums_r[0:1, :] += _colsum(err * err)
        sums_r[1:2, :] += _colsum(dy * n4)
        sums_r[2:3, :] += _colsum(dn4 * ffn)

    tk_f = _tile(d_ff, 1024)
    dy, dff, sums_f = _mm("ff2_loss", act, wf_2, _NN, t, d, d_ff, tr, d, _tile(d_ff, 2048),
                          [(h1,) + row_d, (tgt,) + row_d, (gt2,) + vec_d, (g_post_ffn,) + vec_d],
                          [(jax.ShapeDtypeStruct((t, d), BF16),) + row_d, (jax.ShapeDtypeStruct((t, d), BF16),) + row_d,
                           (jax.ShapeDtypeStruct((8, d), F32),) + sums_d], loss_head, row_chunk=rc)
    loss_mine = (0.5 / d) * jnp.sum(sums_f[0])

    def relu_sq_bwd(acc, i, j, extra_refs, out_refs, rows):
        out_refs[0][...] = (acc * (2.0 * jnp.maximum(extra_refs[0][...].astype(F32), 0.0))).astype(BF16)

    dhff, = _mm("d_hff", dff, wf_2, _NT, t, d_ff, d, tm, tn_f, d, [(hff,) + blk_f],
                [(jax.ShapeDtypeStruct((t, d_ff), BF16),) + blk_f], relu_sq_bwd)
    scatters = {}

    def send_grads(key, grads16, axes):
        plan = _scatter_plan(grads16, axes)
        scatters[key] = (plan,) + _split_start("scatter_%s_start" % key, plan)
        return scatters[key][3]

    def received_grads(key, after):
        plan, sems, thru, _ = scatters[key]
        return _split_wait("scatter_%s_wait" % key, plan, sems, thru, after)[1]

    gw_ff2, gw_ff2_16 = _grad_w("grad_w_ff2", act, dff)
    sent_ff2 = send_grads("ff2", [gw_ff2_16], big_axes[5:6])
    gw_ff1, gw_ff1_16 = _grad_w("grad_w_ff1", a2, dhff, token=sent_ff2)
    sent_ff1 = send_grads("ff1", [gw_ff1_16], big_axes[4:5])

    def pre_ffn_bwd(acc, i, j, extra_refs, out_refs, rows):
        h1_r, dy_r, mo_r, sc2_r, g3_r, gt1_r, g2_r = extra_refs[:7]
        dh1_r, dmo_r, sums_r = out_refs
        h1v = h1_r[rows, :]
        r3 = _rms(h1v)
        h1n = h1v * r3
        dn3 = acc * (1.0 + sc2_r[...])
        dh1n = dn3 * g3_r[...]
        dh1 = dy_r[rows, :].astype(F32) + r3 * (dh1n - h1n * jnp.mean(dh1n * h1n, axis=-1, keepdims=True))
        dh1_r[rows, :] = dh1.astype(BF16)
        mov = mo_r[rows, :].astype(F32)
        r2 = _rms(mov)
        mon = mov * r2
        dn2 = dh1 * gt1_r[...]
        dmon = dn2 * g2_r[...]
        dmo_r[rows, :] = (r2 * (dmon - mon * jnp.mean(dmon * mon, axis=-1, keepdims=True))).astype(BF16)
        zero_first(sums_r, i, rows)

        sums_r[0:1, :] += _colsum(acc)
        sums_r[1:2, :] += _colsum(acc * (h1n * g3_r[...]))
        sums_r[2:3, :] += _colsum(dn3 * h1n)
        sums_r[3:4, :] += _colsum(dh1 * (mon * g2_r[...]))
        sums_r[4:5, :] += _colsum(dn2 * mon)

    dh1, dmo, sums_m = _mm("d_a2_pre_ffn", dhff, wf_1, _NT, t, d, d_ff, tr, d, tk_f,
                           [(h1,) + row_d, (dy,) + row_d, (mo,) + row_d, (sc2,) + vec_d, (g_pre_ffn,) + vec_d,
                            (gt1,) + vec_d, (g_post_mix,) + vec_d] + _after(sent_ff1),
                           [(jax.ShapeDtypeStruct((t, d), BF16),) + row_d, (jax.ShapeDtypeStruct((t, d), BF16),) + row_d,
                            (jax.ShapeDtypeStruct((8, d), F32),) + sums_d], pre_ffn_bwd, row_chunk=rc)
    gw_o, gw_o_16 = _grad_w("grad_w_o", merged, dmo)

    n_j = d // tn_d

    def merge_bwd_body(dmo_ref, wo_ref, ga_ref, gb_ref, ya_ref, yb_ref, dya_ref, dyb_ref, dproj_ref, acc_s):
        g = pl.program_id(2)

        @pl.when(g == 0)
        def _():
            dm = _dot(dmo_ref[...], wo_ref[...], _NT)
            acc_s[...] = dm
            sa = _sigmoid(ga_ref[...])
            dya_ref[...] = (dm * sa).astype(BF16)
            dproj_ref[...] = (dm * ya_ref[...].astype(F32) * sa * (1.0 - sa)).astype(BF16)

        @pl.when(g == 1)
        def _():
            dm = acc_s[...]
            sb = _sigmoid(gb_ref[...])
            dyb_ref[...] = (dm * sb).astype(BF16)
            dproj_ref[...] = (dm * yb_ref[...].astype(F32) * sb * (1.0 - sb)).astype(BF16)

    tile3 = pl.BlockSpec((tm, tn_d), lambda i, j, g: (i, j))
    dy_a, dy_b, dproj = pl.pallas_call(
        merge_bwd_body, name="d_merged", grid=(t // tm, n_j, 2),
        in_specs=[pl.BlockSpec((tm, d), lambda i, j, g: (i, 0)), pl.BlockSpec((tn_d, d), lambda i, j, g: (j, 0)),
                  pl.BlockSpec((tm, tn_d), lambda i, j, g: (i, ga_blk + j)),
                  pl.BlockSpec((tm, tn_d), lambda i, j, g: (i, gb_blk + j)), tile3, tile3],
        out_specs=[tile3, tile3, pl.BlockSpec((tm, tn_d), lambda i, j, g: (i, ga_blk + g * n_j + j))],
        out_shape=[jax.ShapeDtypeStruct((t, d), BF16), jax.ShapeDtypeStruct((t, d), BF16),
                   jax.ShapeDtypeStruct((t, n_in), BF16)],
        scratch_shapes=[pltpu.VMEM((tm, tn_d), F32)], compiler_params=_params(3),
    )(dmo, wf_o, proj, proj, y_a, y_b)

    tn_w = _tile(width, 1024)
    blk_w = ((tm, tn_w), lambda i, j: (i, j))
    dout_a, = _mm("d_out_a", dy_a, wf_a, _NT, t, width, d, tm, tn_w, d, [],
                  [(jax.ShapeDtypeStruct((t, width), BF16),) + blk_w], store_bf16)
    dout_b, = _mm("d_out_b", dy_b, wf_b, _NT, t, width, d, tm, tn_w, d, [],
                  [(jax.ShapeDtypeStruct((t, width), BF16),) + blk_w], store_bf16)
    gw_a, gw_a_16 = _grad_w("grad_w_a_out", out_a, dy_a)
    gw_b, gw_b_16 = _grad_w("grad_w_b_out", out_b, dy_b)

    w_st = jnp.swapaxes(w_spatial[0], 1, 2)
    dproj, dg_sgu, dw_sp, dbst = _sgu_bwd(proj, dout_b, dproj, g_sgu_norm, w_spatial[0], w_st, bst, width, z_block)
    sent_mid = send_grads("mid", [gw_a_16, gw_b_16, gw_o_16], big_axes[1:4])
    dproj, dgh_heads, dlb = _hgrn_bwd(proj, osum, dout_a, dproj, lb, g_hgrn_norm, width, after=sent_mid)
    gw_in_top, gw_in_top16 = _grad_w("grad_w_in_top", a1, dproj, rows=(0, d // 2))
    sent_top = send_grads("in_top", [gw_in_top16], big_axes[:1])
    gw_in_bot, gw_in_bot16 = _grad_w("grad_w_in_bot", a1, dproj, token=sent_top, rows=(d // 2, d // 2))
    sent_in = send_grads("in_bot", [gw_in_bot16], big_axes[:1])

    def pre_mix_bwd(acc, i, j, extra_refs, out_refs, rows):
        x_r, dh1_r, sc1_r, g1_r = extra_refs[:4]
        dx_r, sums_r = out_refs
        xv = x_r[rows, :]
        r1 = _rms(xv)
        xn = xv * r1
        dn1 = acc * (1.0 + sc1_r[...])
        dxn = dn1 * g1_r[...]
        dx_r[rows, :] = dh1_r[rows, :].astype(F32) + r1 * (dxn - xn * jnp.mean(dxn * xn, axis=-1, keepdims=True))
        zero_first(sums_r, i, rows)

        sums_r[0:1, :] += _colsum(acc)
        sums_r[1:2, :] += _colsum(acc * (xn * g1_r[...]))
        sums_r[2:3, :] += _colsum(dn1 * xn)

    tk_in = _tile(n_in, 2816)
    grad_x, sums_x = _mm(
        "d_a1_pre_mix", dproj, wf_in, _NT, t, d, n_in, tr, d, tk_in,
        [(x2,) + row_d, (dh1,) + row_d, (sc1,) + vec_d, (g_pre_mix,) + vec_d] + _after(sent_in),
        [(jax.ShapeDtypeStruct((t, d), F32),) + row_d, (jax.ShapeDtypeStruct((8, d), F32),) + sums_d],
        pre_mix_bwd, row_chunk=rc)

    dmod = jnp.concatenate([sums_x[0:2], sums_m[3:4], sums_m[0:2], sums_f[1:2]], axis=0).reshape(N_DEV, n_ada // LANE, LANE)
    ada_rows = -(-(n_ada // LANE) // 8) * 8
    dmod = jnp.pad(dmod, ((0, 0), (0, ada_rows - n_ada // LANE), (0, 0))).reshape(N_DEV * ada_rows, LANE)
    parts = [dmod, _rows(sums_x[2:3]), _rows(sums_m[4:5]), _rows(sums_m[2:3]), _rows(sums_f[2:3]),
             _rows(jnp.sum(dgh_heads, axis=0)), _rows(dg_sgu), _rows(dw_sp), _rows(dbst.T)]
    n_params = sum(p.shape[0] for p in parts)
    parts.append(jnp.full((8, LANE), loss_mine, F32))
    n_common = n_params + 8
    payload = jnp.concatenate(parts + [_rows(dlb)], axis=0)

    moms = [m_w_in, m_w_a_out, m_w_b_out, m_w_o, m_w_ff1, m_w_ff2]
    vars_ = [v_w_in, v_w_a_out, v_w_b_out, v_w_o, v_w_ff1, v_w_ff2]
    big_out = {}

    def big_update(nm, parts):
        k = big_names.index(nm)
        outs = _adamw_big("adamw_" + nm, me_arr, big[k], moms[k][0], vars_[k][0], parts, big_axes[k])
        big_out[nm] = [o[None] for o in outs]
        return outs[0]

    land_ff2, = received_grads("ff2", grad_x)
    done = big_update("w_ff2", [(gw_ff2, land_ff2)])
    land_ff1, = received_grads("ff1", done)
    done = big_update("w_ff1", [(gw_ff1, land_ff1)])
    land_a, land_b, land_o = received_grads("mid", done)
    big_update("w_a_out", [(gw_a, land_a)])
    big_update("w_b_out", [(gw_b, land_b)])
    done = big_update("w_o", [(gw_o, land_o)])

    payload, _ = lax.optimization_barrier((payload, done))
    gathered = _all_gather_small("gather_small_grads", payload)

    dmod_mine = lax.dynamic_slice_in_dim(gathered[:, :N_DEV * ada_rows, :].reshape(N_DEV, N_DEV, ada_rows * LANE),
                                         me, 1, axis=1)[:, 0, :n_ada]
    ada_out = [o[None] for o in _adamw_ada(sc_all.T, dmod_mine, w_ada[0], m_w_ada[0], v_w_ada[0])]

    def pack(b_, g1_, g2_, g3_, g4_, gh_, gs_, ws_, bs_):
        b3 = b_.reshape(N_DEV, n_ada // LANE, LANE)
        b3 = jnp.pad(b3, ((0, 0), (0, ada_rows - n_ada // LANE), (0, 0))).reshape(N_DEV * ada_rows, LANE)
        return jnp.concatenate([b3, _rows(g1_), _rows(g2_), _rows(g3_), _rows(g4_), _rows(gh_), _rows(gs_),
                                _rows(ws_), _rows(bs_), jnp.zeros((8, LANE), F32)], axis=0)

    small_w = (b_ada, g_pre_mix, g_post_mix, g_pre_ffn, g_post_ffn, g_hgrn_norm, g_sgu_norm, w_spatial, b_spatial)
    small_m = (m_b_ada, m_g_pre_mix, m_g_post_mix, m_g_pre_ffn, m_g_post_ffn, m_g_hgrn_norm, m_g_sgu_norm, m_w_spatial, m_b_spatial)
    small_v = (v_b_ada, v_g_pre_mix, v_g_post_mix, v_g_pre_ffn, v_g_post_ffn, v_g_hgrn_norm, v_g_sgu_norm, v_w_spatial, v_b_spatial)
    packed = _adamw_small(gathered[:, :n_common, :], pack(*small_w), pack(*small_m), pack(*small_v))

    def unpack(slab):
        outs, at = [], 0
        b3 = slab[:N_DEV * ada_rows].reshape(N_DEV, ada_rows, LANE)[:, :n_ada // LANE, :]
        outs.append(b3.reshape(b_ada.shape))
        at = N_DEV * ada_rows
        for ref in small_w[1:]:
            n_el = ref.size
            n_r = -(-(n_el // LANE) // 8) * 8
            outs.append(slab[at:at + n_el // LANE].reshape(ref.shape))
            at += n_r
        return outs

    small_out = [unpack(s) for s in packed]
    loss = packed[0][n_params, 0]

    dlb_all = gathered[:, n_common:n_common + 2 * heads, :].reshape(N_DEV, 2, heads, LANE)
    dlb_mine = lax.dynamic_index_in_dim(dlb_all, me, axis=2, keepdims=False)
    lb_out = _adamw_lb(dlb_mine, lb_logits, m_lb_logits, v_lb_logits)

    land_top, = received_grads("in_top", ada_out[0])
    land_bot, = received_grads("in_bot", land_top)
    big_update("w_in", [(gw_in_top, land_top), (gw_in_bot, land_bot)])

    order = ["w_ada", "b_ada", "g_pre_mix", "g_post_mix", "g_pre_ffn", "g_post_ffn", "w_in", "lb_logits", "g_hgrn_norm",
             "w_a_out", "g_sgu_norm", "w_spatial", "b_spatial", "w_b_out", "w_o", "w_ff1", "w_ff2"]
    small_names = ["b_ada", "g_pre_mix", "g_post_mix", "g_pre_ffn", "g_post_ffn", "g_hgrn_norm", "g_sgu_norm", "w_spatial", "b_spatial"]

    def leaf(kind, nm):
        if nm == "w_ada":
            return ada_out[kind]
        if nm == "lb_logits":
            return lb_out[kind]
        if nm in big_out:
            return big_out[nm][kind]
        return small_out[kind][small_names.index(nm)]

    result = [loss, grad_x[None]]
    for kind in range(4):
        result += [leaf(kind, nm) for nm in order]
    return tuple(result)
```

```python
import math

import jax
import jax.numpy as jnp
from jax import lax
from jax.experimental import pallas as pl
from jax.experimental.pallas import tpu as pltpu

F32 = jnp.float32
BF16 = jnp.bfloat16
MESH = pl.DeviceIdType.MESH
HIGHEST = lax.Precision.HIGHEST

N_DEV = 8
HEAD = 128
A_CHUNK = 32
N_MOD = 6
EPS = 1e-6
LANE = 128
VMEM_LIMIT = 60 * 1024 * 1024

ADAM_LR = 0.001
ADAM_B1 = 0.9
ADAM_B2 = 0.999
ADAM_EPS = 1e-08
ADAM_WD = 0.01
ADAM_STEP = 10

_NN = (((1,), (0,)), ((), ()))
_NT = (((1,), (1,)), ((), ()))
_TN = (((0,), (0,)), ((), ()))


def _dot(a, b, dims=_NN, precision=None):
    return lax.dot_general(a, b, dims, preferred_element_type=F32, precision=precision)


def _bdot(a, b, dims=_NN):
    return _dot(a.astype(BF16), b.astype(BF16), dims)


def _params(n_grid):
    return pltpu.CompilerParams(dimension_semantics=("arbitrary",) * n_grid, vmem_limit_bytes=VMEM_LIMIT)


def _dev_index():
    return lax.axis_index("x") * 4 + lax.axis_index("y") * 2 + lax.axis_index("c")


def _dev_coords(i):
    return (i // 4, (i // 2) % 2, i % 2)


def _sigmoid(x):
    return 1.0 / (1.0 + jnp.exp(-x))


def _erf(x):
    ax = jnp.abs(x)
    t = 1.0 / (1.0 + 0.3275911 * ax)
    poly = ((((1.061405429 * t - 1.453152027) * t + 1.421413741) * t - 0.284496736) * t + 0.254829592) * t
    y = 1.0 - poly * jnp.exp(-ax * ax)
    return jnp.where(x < 0, -y, y)


def _gelu_and_grad(x):
    cdf = 0.5 * (1.0 + _erf(x * (2.0 ** -0.5)))
    pdf = jnp.exp(-0.5 * x * x) * (1.0 / math.sqrt(2.0 * math.pi))
    return x * cdf, cdf + x * pdf


def _rms(x):
    return lax.rsqrt(jnp.mean(x * x, axis=-1, keepdims=True) + EPS)


def _colsum(x):
    return jnp.sum(x, axis=0, keepdims=True)


def _tile(n, want):
    if n <= want:
        return n
    t = (want // LANE) * LANE
    while n % t:
        t -= LANE
    assert t > 0, (n, want)
    return t


def _all_gather_small(name, payload):
    rows = payload.shape[0]

    def body(p_ref, out_ref, send_sems, recv_sems, local_sem):
        me = _dev_index()
        mine = pltpu.make_async_copy(p_ref, out_ref.at[me], local_sem)
        mine.start()
        sends = []
        for r in range(1, N_DEV):
            peer = (me + r) % N_DEV
            cp = pltpu.make_async_remote_copy(
                src_ref=p_ref, dst_ref=out_ref.at[me], send_sem=send_sems.at[r - 1], recv_sem=recv_sems.at[r - 1],
                device_id=_dev_coords(peer), device_id_type=MESH)
            cp.start()
            sends.append(cp)
        for r in range(1, N_DEV):
            src = (me + N_DEV - r) % N_DEV
            pltpu.make_async_remote_copy(
                src_ref=p_ref, dst_ref=out_ref.at[src], send_sem=send_sems.at[r - 1], recv_sem=recv_sems.at[r - 1],
                device_id=_dev_coords(src), device_id_type=MESH).wait_recv()
        for cp in sends:
            cp.wait_send()
        mine.wait()

    return pl.pallas_call(
        body, name=name,
        out_shape=jax.ShapeDtypeStruct((N_DEV, rows, LANE), F32),
        in_specs=[pl.BlockSpec(memory_space=pltpu.VMEM)],
        out_specs=pl.BlockSpec(memory_space=pltpu.VMEM),
        scratch_shapes=[pltpu.SemaphoreType.DMA((N_DEV - 1,)), pltpu.SemaphoreType.DMA((N_DEV - 1,)),
                        pltpu.SemaphoreType.DMA],
        compiler_params=pltpu.CompilerParams(vmem_limit_bytes=VMEM_LIMIT),
    )(payload)


def _region(ref, dev, axis, n):
    start = pl.multiple_of(dev * n, LANE if axis == 1 else 16)
    return ref.at[:, pl.ds(start, n)] if axis == 1 else ref.at[pl.ds(start, n), :]


class _Exchange:
    def __init__(self, arrays, out_shapes, sems, start, finish):
        self.arrays, self.out_shapes, self.sems, self.start, self.finish = arrays, out_shapes, sems, start, finish


def _scatter_plan(grads, axes):
    n_w = len(grads)
    lands = []
    for g, ax in zip(grads, axes):
        shp = (g.shape[0], g.shape[1] // N_DEV) if ax == 1 else (g.shape[0] // N_DEV, g.shape[1])
        lands.append(jax.ShapeDtypeStruct((N_DEV - 1,) + shp, BF16))
    widths = [ld.shape[1 + ax] for ld, ax in zip(lands, axes)]

    def copy(w, r, g_refs, l_refs, sems, block, to):
        return pltpu.make_async_remote_copy(
            src_ref=_region(g_refs[w], block, axes[w], widths[w]), dst_ref=l_refs[w].at[r - 1],
            send_sem=sems[0].at[w * (N_DEV - 1) + r - 1], recv_sem=sems[1].at[w * (N_DEV - 1) + r - 1],
            device_id=_dev_coords(to), device_id_type=MESH)

    def start(g_refs, l_refs, sems):
        me = _dev_index()
        for w in range(n_w):
            for r in range(1, N_DEV):
                owner = (me + r) % N_DEV
                copy(w, r, g_refs, l_refs, sems, owner, owner).start()

    def finish(g_refs, l_refs, sems):
        me = _dev_index()
        for w in range(n_w):
            for r in range(1, N_DEV):
                copy(w, r, g_refs, l_refs, sems, me, (me + N_DEV - r) % N_DEV).wait_recv()
        for w in range(n_w):
            for r in range(1, N_DEV):
                copy(w, r, g_refs, l_refs, sems, me, (me + r) % N_DEV).wait_send()

    sems = [pltpu.SemaphoreType.DMA((n_w * (N_DEV - 1),)), pltpu.SemaphoreType.DMA((n_w * (N_DEV - 1),))]
    return _Exchange(list(grads), lands, sems, start, finish)


def _places():
    x, y, c = lax.axis_index("x"), lax.axis_index("y"), lax.axis_index("c")
    return (x, y, c), (x, y, 1 - c), [(1 - x, y), (x, 1 - y), (1 - x, 1 - y)]


def _place_index(p):
    return p[0] * 4 + p[1] * 2 + p[2]


def _gather_stage_plans(fulls, axes):
    n_w = len(fulls)
    widths = [f.shape[ax] // N_DEV for f, ax in zip(fulls, axes)]
    shapes = [jax.ShapeDtypeStruct(f.shape, f.dtype) for f in fulls]

    def copy(per, w, k, f_refs, sems, block, to):
        part = _region(f_refs[w], _place_index(block), axes[w], widths[w])
        return pltpu.make_async_remote_copy(
            src_ref=part, dst_ref=part, send_sem=sems[0].at[w * per + k], recv_sem=sems[1].at[w * per + k],
            device_id=to, device_id_type=MESH)

    def start1(_, f_refs, sems):
        me, sib, chips = _places()
        for w in range(n_w):
            copy(4, w, 0, f_refs, sems, me, sib).start()
            for j, chip in enumerate(chips):
                copy(4, w, 1 + j, f_refs, sems, me, (*chip, me[2])).start()

    def finish1(_, f_refs, sems):
        me, sib, chips = _places()
        for w in range(n_w):
            copy(4, w, 0, f_refs, sems, sib, me).wait_recv()
            for j, chip in enumerate(chips):
                copy(4, w, 1 + j, f_refs, sems, (*chip, me[2]), me).wait_recv()
        for w in range(n_w):
            for k in range(4):
                copy(4, w, k, f_refs, sems, me, sib).wait_send()

    def start2(_, f_refs, sems):
        me, sib, chips = _places()
        for w in range(n_w):
            for j, chip in enumerate(chips):
                copy(3, w, j, f_refs, sems, (*chip, me[2]), sib).start()

    def finish2(_, f_refs, sems):
        me, sib, chips = _places()
        for w in range(n_w):
            for j, chip in enumerate(chips):
                copy(3, w, j, f_refs, sems, (*chip, sib[2]), me).wait_recv()
        for w in range(n_w):
            for j, chip in enumerate(chips):
                copy(3, w, j, f_refs, sems, (*chip, me[2]), sib).wait_send()

    sems1 = [pltpu.SemaphoreType.DMA((n_w * 4,)), pltpu.SemaphoreType.DMA((n_w * 4,))]
    sems2 = [pltpu.SemaphoreType.DMA((n_w * 3,)), pltpu.SemaphoreType.DMA((n_w * 3,))]
    return _Exchange([], shapes, sems1, start1, finish1), _Exchange([], shapes, sems2, start2, finish2)


_HBM = pl.BlockSpec(memory_space=pltpu.HBM)
_SEM = pl.BlockSpec(memory_space=pltpu.SEMAPHORE)
_EFFECT = pltpu.SideEffectType.DATAFLOW_SIDE_EFFECTING


def _split_start(name, plan, landing=None):
    n_in, n_out, n_sem = len(plan.arrays), len(plan.out_shapes), len(plan.sems)

    def body(*refs):
        ins, lands = refs[:n_in], refs[n_in:n_in + n_out]
        sems = refs[n_in + n_out:n_in + n_out + n_sem]
        token = refs[-1]
        plan.start(ins, lands, sems)
        token[...] = jnp.zeros_like(token)

    hbm = lambda a: pltpu.HBM(a.shape, a.dtype)
    results = pl.pallas_call(
        body, name=name,
        out_shape=tuple(plan.sems) + tuple(hbm(a) for a in plan.arrays) + tuple(hbm(a) for a in plan.out_shapes)
        + (jax.ShapeDtypeStruct((8, LANE), F32),),
        in_specs=(_HBM,) * (n_in + n_out),
        out_specs=(_SEM,) * n_sem + (_HBM,) * (n_in + n_out) + (pl.BlockSpec(memory_space=pltpu.VMEM),),
        input_output_aliases={i: n_sem + i for i in range(n_in + n_out)},
        compiler_params=pltpu.CompilerParams(has_side_effects=_EFFECT),
    )(*[pltpu.with_memory_space_constraint(a, pltpu.HBM) for a in plan.arrays],
      *[pltpu.with_memory_space_constraint(a, pltpu.HBM)
        for a in (landing if landing is not None else [lax.empty(a.shape, a.dtype) for a in plan.out_shapes])])
    return results[:n_sem], results[n_sem:n_sem + n_in + n_out], results[-1]


def _split_wait(name, plan, sems, thru, after):
    n_in, n_out, n_sem = len(plan.arrays), len(plan.out_shapes), len(plan.sems)

    def body(*refs):
        ins, lands = refs[:n_in], refs[n_in:n_in + n_out]
        sem_refs = refs[n_in + n_out:n_in + n_out + n_sem]
        plan.finish(ins, lands, sem_refs)

    hbm = lambda a: pltpu.HBM(a.shape, a.dtype)
    results = pl.pallas_call(
        body, name=name,
        out_shape=tuple(hbm(a) for a in plan.arrays) + tuple(hbm(a) for a in plan.out_shapes),
        in_specs=(_HBM,) * (n_in + n_out) + (_SEM,) * n_sem + (pl.BlockSpec(memory_space=pl.ANY),),
        out_specs=(_HBM,) * (n_in + n_out),
        input_output_aliases={i: i for i in range(n_in + n_out)},
        compiler_params=pltpu.CompilerParams(has_side_effects=_EFFECT),
    )(*thru, *sems, after)
    return results[:n_in], results[n_in:]


def _cast_into_full(name, me, w, axis):
    r, c = w.shape
    tr = _tile(r, 256)
    if axis == 1:
        shape, place = (r, c * N_DEV), pl.BlockSpec((tr, c), lambda i, me_ref: (i, me_ref[0]))
    else:
        shape, place = (r * N_DEV, c), pl.BlockSpec((tr, c), lambda i, me_ref: (me_ref[0] * (r // tr) + i, 0))

    def body(me_ref, w_ref, o_ref):
        o_ref[...] = w_ref[...].astype(BF16)

    grid_spec = pltpu.PrefetchScalarGridSpec(
        num_scalar_prefetch=1, grid=(r // tr,),
        in_specs=[pl.BlockSpec((tr, c), lambda i, me_ref: (i, 0))], out_specs=place)
    return pl.pallas_call(body, name=name, grid_spec=grid_spec, out_shape=jax.ShapeDtypeStruct(shape, BF16),
                          compiler_params=_params(1))(me, w)


def _mm(name, a, b, dims, m, n, k, tm, tn, tk, extras, outs, epilogue, row_chunk=None, a_col_block=0):
    ni, nj, nk = m // tm, n // tn, k // tk
    ne, no = len(extras), len(outs)
    if dims == _TN:
        a_spec = pl.BlockSpec((tk, tm), lambda i, j, kk: (kk, i + a_col_block))
    else:
        a_spec = pl.BlockSpec((tm, tk), lambda i, j, kk: (i, kk))
    if dims == _NT:
        b_spec = pl.BlockSpec((tn, tk), lambda i, j, kk: (j, kk))
    else:
        b_spec = pl.BlockSpec((tk, tn), lambda i, j, kk: (kk, j))
    chunks = [slice(None)] if row_chunk is None else [slice(r, r + row_chunk) for r in range(0, tm, row_chunk)]

    def lift(index_map):
        return lambda i, j, kk: index_map(i, j)

    def body(a_ref, b_ref, *rest):
        extra_refs, out_refs, rest = rest[:ne], rest[ne:ne + no], rest[ne + no:]
        i, j, kk = pl.program_id(0), pl.program_id(1), pl.program_id(2)
        if nk == 1:
            part = _dot(a_ref[...], b_ref[...], dims)
            for rows in chunks:
                epilogue(part[rows], i, j, extra_refs, out_refs, rows)
        else:
            acc_ref = rest[0]

            @pl.when(kk == 0)
            def _():
                acc_ref[...] = _dot(a_ref[...], b_ref[...], dims)

            @pl.when(kk > 0)
            def _():
                acc_ref[...] += _dot(a_ref[...], b_ref[...], dims)

            @pl.when(kk == nk - 1)
            def _():
                for rows in chunks:
                    epilogue(acc_ref[rows, :], i, j, extra_refs, out_refs, rows)

    once = dict(pipeline_mode=pl.Buffered(1)) if (row_chunk is not None and nk > 1) else {}
    return pl.pallas_call(
        body, name=name,
        grid=(ni, nj, nk),
        in_specs=[a_spec, b_spec] + [pl.BlockSpec(bs, lift(im), **once) for _, bs, im in extras],
        out_specs=[pl.BlockSpec(bs, lift(im), **once) for _, bs, im in outs],
        out_shape=[sd for sd, _, _ in outs],
        scratch_shapes=[pltpu.VMEM((tm, tn), F32)] if nk > 1 else [],
        compiler_params=_params(3),
    )(a, b, *[arr for arr, _, _ in extras])


def _after(token):
    return [(token, (8, LANE), lambda i, j: (0, 0))]


def _grad_w(name, a, dc, token=None, tm=512, tn=1024, rows=None):
    t = a.shape[0]
    n = dc.shape[1]
    first, m = rows if rows is not None else (0, a.shape[1])
    tm, tn = _tile(m, tm), _tile(n, tn)
    assert first % tm == 0

    def epilogue(acc, i, j, extra_refs, out_refs, rows):
        out_refs[0][...] = acc
        out_refs[1][...] = acc.astype(BF16)

    blk = ((tm, tn), lambda i, j: (i, j))
    return _mm(name, a, dc, _TN, m, n, t, tm, tn, t, _after(token) if token is not None else [],
               [(jax.ShapeDtypeStruct((m, n), F32),) + blk, (jax.ShapeDtypeStruct((m, n), BF16),) + blk], epilogue,
               a_col_block=first // tm)


def _proj_gather(a1, w_shard, order):
    t, d = a1.shape
    nsh = w_shard.shape[1]
    tm = _tile(t, 512)
    n_i = t // tm

    def body(ord_ref, a_ref, wsh_ref, proj_ref, full_ref, bbuf, bsem, send_sems, recv_sems, own_sem):
        s, i = pl.program_id(0), pl.program_id(1)
        me, sib, chips = _places()
        near, far = chips[:2], chips[2]
        steps = ([(me, None, None), (sib, 0, None)]
                 + [((*ch, me[2]), 1 + j, 4 + j) for j, ch in enumerate(near)]
                 + [((*ch, sib[2]), 4 + j, None) for j, ch in enumerate(near)]
                 + [((*far, me[2]), 3, 6), ((*far, sib[2]), 6, None)])
        blocks = [st[0] for st in steps]

        def part(block):
            return _region(full_ref, _place_index(block), 1, nsh)

        def remote(k, block, to, from_shard=False):
            return pltpu.make_async_remote_copy(
                src_ref=wsh_ref if from_shard else part(block), dst_ref=part(block),
                send_sem=send_sems.at[k], recv_sem=recv_sems.at[k], device_id=to, device_id_type=MESH)

        def load(pos):
            src = wsh_ref if pos == 0 else part(blocks[pos])
            return pltpu.make_async_copy(src, bbuf.at[pos % 2], bsem.at[pos % 2])

        own = pltpu.make_async_copy(wsh_ref, part(me), own_sem)

        @pl.when((s == 0) & (i == 0))
        def _():
            own.start()
            remote(0, me, sib, True).start()
            for j, ch in enumerate(chips):
                remote(1 + j, me, (*ch, me[2]), True).start()
            load(0).start()
            load(0).wait()

        for pos in range(1, N_DEV):
            @pl.when((s == pos) & (i == 0))
            def _():
                load(pos).wait()

        for pos in range(N_DEV - 1):
            @pl.when((s == pos) & (i == n_i - 1))
            def _():
                nxt = pos + 1
                block, arrives_on, pass_on_with = steps[nxt]
                remote(arrives_on, block, me).wait_recv()
                if pass_on_with is not None:
                    remote(pass_on_with, block, sib).start()
                load(nxt).start()

        proj_ref[...] = _dot(a_ref[...], bbuf[s % 2])

        @pl.when((s == N_DEV - 1) & (i == n_i - 1))
        def _():
            for k in range(N_DEV - 1):
                remote(k, me, sib, True).wait_send()
            own.wait()

    grid_spec = pltpu.PrefetchScalarGridSpec(
        num_scalar_prefetch=1, grid=(N_DEV, n_i),
        in_specs=[pl.BlockSpec((tm, d), lambda s, i, ord_ref: (i, 0)), pl.BlockSpec(memory_space=pl.ANY)],
        out_specs=[pl.BlockSpec((tm, nsh), lambda s, i, ord_ref: (i, ord_ref[s])), pl.BlockSpec(memory_space=pl.ANY)],
        scratch_shapes=[pltpu.VMEM((2, d, nsh), BF16), pltpu.SemaphoreType.DMA((2,)),
                        pltpu.SemaphoreType.DMA((N_DEV - 1,)), pltpu.SemaphoreType.DMA((N_DEV - 1,)),
                        pltpu.SemaphoreType.DMA])
    return pl.pallas_call(
        body, name="proj_gather", grid_spec=grid_spec,
        out_shape=[jax.ShapeDtypeStruct((t, nsh * N_DEV), F32), jax.ShapeDtypeStruct((d, nsh * N_DEV), BF16)],
        compiler_params=_params(2),
    )(order, a1, w_shard)


def _cast_bf16(name, w):
    r, c = w.shape
    tr = _tile(r, 256)
    return pl.pallas_call(
        lambda w_ref, o_ref: o_ref.__setitem__(Ellipsis, w_ref[...].astype(BF16)), name=name,
        grid=(r // tr,), in_specs=[pl.BlockSpec((tr, c), lambda i: (i, 0))],
        out_specs=pl.BlockSpec((tr, c), lambda i: (i, 0)), out_shape=jax.ShapeDtypeStruct((r, c), BF16),
        compiler_params=_params(1),
    )(w)


def _prep_small(c_row, lb_logits):
    d = c_row.shape[1]
    rows = d // LANE

    def body(c_ref, l_ref, o_ref):
        cv = c_ref[...]
        o_ref[0:rows, :] = cv * _sigmoid(cv)
        lbs = [_sigmoid(l_ref[dr][0:1, :] - l_ref[dr][1:2, :]) for dr in range(2)]
        o_ref[rows:rows + 8, :] = jnp.concatenate(lbs + [jnp.zeros((6, LANE), F32)], axis=0)

    return pl.pallas_call(
        body, name="prep_small", out_shape=jax.ShapeDtypeStruct((rows + 8, LANE), F32),
    )(c_row.reshape(rows, LANE), lb_logits)


def _mod_shard(sc_all, w_ada_shard, b_shard):
    d, n = w_ada_shard.shape
    tn = _tile(n, 512)

    def body(s_ref, w_ref, b_ref, o_ref):
        o_ref[...] = _dot(s_ref[...], w_ref[...], precision=HIGHEST) + b_ref[...]

    return pl.pallas_call(
        body, name="mod_shard", grid=(n // tn,),
        in_specs=[pl.BlockSpec((N_DEV, d), lambda j: (0, 0)), pl.BlockSpec((d, tn), lambda j: (0, j)),
                  pl.BlockSpec((1, tn), lambda j: (0, j))],
        out_specs=pl.BlockSpec((N_DEV, tn), lambda j: (0, j)),
        out_shape=jax.ShapeDtypeStruct((N_DEV, n), F32), compiler_params=_params(1),
    )(sc_all, w_ada_shard, b_shard)


def _norm_mod(x, gain, shift, scale):
    t, d = x.shape
    tm = _tile(t, 512)

    def body(x_ref, g_ref, sh_ref, sc_ref, o_ref):
        xv = x_ref[...]
        o_ref[...] = ((xv * _rms(xv) * g_ref[...]) * (1.0 + sc_ref[...]) + sh_ref[...]).astype(BF16)

    vec = pl.BlockSpec((1, d), lambda i: (0, 0))
    return pl.pallas_call(
        body, name="norm_mod", grid=(t // tm,),
        in_specs=[pl.BlockSpec((tm, d), lambda i: (i, 0)), vec, vec, vec],
        out_specs=pl.BlockSpec((tm, d), lambda i: (i, 0)), out_shape=jax.ShapeDtypeStruct((t, d), BF16),
        compiler_params=_params(1),
    )(x, gain, shift, scale)


def _chunk_masks():
    row = lax.broadcasted_iota(jnp.int32, (HEAD, HEAD), 0)
    col = lax.broadcasted_iota(jnp.int32, (HEAD, HEAD), 1)
    same = (row // A_CHUNK) == (col // A_CHUNK)
    return same & (col <= row), same & (col >= row)


def _ones(mask):
    return jnp.where(mask, 1.0, 0.0).astype(BF16)


def _dot_split(ones_bf16, x):
    hi = x.astype(BF16)
    lo = (x - hi.astype(F32)).astype(BF16)
    return _dot(ones_bf16, hi) + _dot(ones_bf16, lo)


def _hgrn_block(direction, f, lb, cum2):
    sf = _sigmoid(f)
    big_f = lb + (1.0 - lb) * sf
    k = (1.0 - lb) * (1.0 - sf)
    lf = jnp.log(big_f)
    both = _dot_split(cum2, lf)
    cf, cr = both[:HEAD], both[HEAD:]
    b, rest = (cf, cr - lf) if direction == 0 else (cr, cf - lf)
    return k, sf, big_f, jnp.exp(b), jnp.exp(-b), jnp.exp(rest)


def _hgrn_fwd(proj, lb, g_norm, width, after):
    t = proj.shape[0]
    heads = width // HEAD
    nb, nc = t // HEAD, t // A_CHUNK
    ua = 8 if nb % 8 == 0 else (2 if nb % 2 == 0 else 1)
    ub = 32 if nc % 32 == 0 else (8 if nc % 8 == 0 else 4)
    q_scale = HEAD ** -0.5

    def body(q_ref, ffw_ref, fbw_ref, v_ref, og_ref, lb_ref, g_ref, *rest):
        outa_ref, osum_ref, qd_s, ke_s, dc_s, o_s = rest[len(after):]
        tril, triu = _chunk_masks()
        cum2 = jnp.concatenate([_ones(tril), _ones(triu)], axis=0)
        f_refs = (ffw_ref, fbw_ref)
        lbs = (lb_ref[0:1, :], lb_ref[1:2, :])

        def phase_a(it, carry):
            loaded = []
            for u in range(ua):
                rows = pl.ds(pl.multiple_of((it * ua + u) * HEAD, HEAD), HEAD)
                loaded.append((rows, q_ref[rows, :], v_ref[rows, :], ffw_ref[rows, :], fbw_ref[rows, :]))
            chains = [(d, rows, qv * q_scale, vv.astype(BF16), fv)
                      for rows, qv, vv, f0, f1 in loaded for d, fv in ((0, f0), (1, f1))]
            blocks = [_hgrn_block(d, fv, lbs[d], cum2) for d, _, _, _, fv in chains]
            scaled = [(qv * eb, k * enb, k * erest, eb * erest)
                      for (_, _, qv, _, _), (k, _, _, eb, enb, erest) in zip(chains, blocks)]
            atts = [jnp.where(tril if d == 0 else triu, _bdot(qd, kd, _NT), 0.0)
                    for (d, _, _, _, _), (qd, kd, _, _) in zip(chains, scaled)]
            intras = [_bdot(att, vv) for att, (_, _, _, vv, _) in zip(atts, chains)]
            results = [(d, rows, o_intra, qd.astype(BF16), ke.astype(BF16), decay)
                       for (d, rows, _, _, _), (qd, _, ke, decay), o_intra in zip(chains, scaled, intras)]
            for d, rows, o_intra, qd16, ke16, decay in results:
                o_s[d, rows, :] = o_intra
                qd_s[d, rows, :] = qd16
                ke_s[d, rows, :] = ke16
                dc_s[d, rows, :] = decay
            return carry

        lax.fori_loop(0, nb // ua, phase_a, 0)

        def phase_b(it, states):
            loaded = []
            for u in range(ub):
                n = it * ub + u
                for d in range(2):
                    c = n if d == 0 else nc - 1 - n
                    start = pl.multiple_of(c * A_CHUNK, A_CHUNK)
                    rows = pl.ds(start, A_CHUNK)
                    loaded.append((d, rows, qd_s[d, rows, :], ke_s[d, rows, :], v_ref[rows, :],
                                   dc_s[d, pl.ds(start, 1), :], o_s[d, rows, :]))
            increments = [_dot(vv.astype(BF16), ke16, _TN) for _, _, _, ke16, vv, _, _ in loaded]
            states = list(states)
            befores = []
            for (d, _, _, _, _, decay, _), inc in zip(loaded, increments):
                befores.append(states[d].astype(BF16))
                states[d] = states[d] * decay + inc
            inters = [_dot(qd16, before, _NT) for (_, _, qd16, _, _, _, _), before in zip(loaded, befores)]
            for (d, rows, _, _, _, _, o_intra), o_inter in zip(loaded, inters):
                o_s[d, rows, :] = o_intra + o_inter
            return tuple(states)

        zero_state = jnp.zeros((HEAD, HEAD), F32)
        lax.fori_loop(0, nc // ub, phase_b, (zero_state, zero_state))

        def phase_c(i, carry):
            rows = pl.ds(pl.multiple_of(i * HEAD, HEAD), HEAD)
            o = o_s[0, rows, :] + o_s[1, rows, :]
            osum_ref[rows, :] = o
            og = og_ref[rows, :]
            outa_ref[rows, :] = (o * _rms(o) * g_ref[...] * (og * _sigmoid(og))).astype(BF16)
            return carry

        lax.fori_loop(0, nb, phase_c, 0)

    def col(p):
        return pl.BlockSpec((t, HEAD), lambda h: (0, p * heads + h))

    return pl.pallas_call(
        body, name="hgrn_fwd", grid=(heads,),
        in_specs=[col(0), col(1), col(2), col(3), col(4),
                  pl.BlockSpec((2, HEAD), lambda h: (0, h)), pl.BlockSpec((1, HEAD), lambda h: (0, 0))]
        + [pl.BlockSpec(memory_space=pl.ANY)] * len(after),
        out_specs=[pl.BlockSpec((t, HEAD), lambda h: (0, h)), pl.BlockSpec((t, HEAD), lambda h: (0, h))],
        out_shape=[jax.ShapeDtypeStruct((t, width), BF16), jax.ShapeDtypeStruct((t, width), F32)],
        scratch_shapes=[pltpu.VMEM((2, t, HEAD), BF16), pltpu.VMEM((2, t, HEAD), BF16), pltpu.VMEM((2, t, HEAD), F32),
                        pltpu.VMEM((2, t, HEAD), F32)],
        compiler_params=_params(1),
    )(proj, proj, proj, proj, proj, lb, g_norm, *after)


def _sgu_core(u_pre, v_pre, g_v, ws_ref, bst):
    u, du = _gelu_and_grad(u_pre)
    v, dv = _gelu_and_grad(v_pre)
    mu = jnp.mean(v, axis=-1, keepdims=True)
    dlt = v - mu
    rstd = lax.rsqrt(jnp.mean(dlt * dlt, axis=-1, keepdims=True) + EPS)
    vhat = dlt * rstd
    vn = vhat * g_v
    groups = vn.shape[1] // HEAD
    cols = []
    for g in range(groups):
        vm_g = _bdot(ws_ref[g], vn[:, g * HEAD:(g + 1) * HEAD]) + bst[:, g:g + 1]
        cols.append(vm_g)
    return u, du, dv, vhat, rstd, vn, jnp.concatenate(cols, axis=1)


def _sgu_fwd(proj, g_v, w_s, bst, width, z_block):
    t = proj.shape[0]

    def body(u_ref, v_ref, g_ref, ws_ref, bst_ref, o_ref):
        u, _, _, _, _, _, vm = _sgu_core(u_ref[...], v_ref[...], g_ref[...], ws_ref, bst_ref[...])
        o_ref[...] = (u * vm).astype(BF16)

    groups = width // HEAD
    return pl.pallas_call(
        body, name="sgu_fwd", grid=(t // HEAD,),
        in_specs=[pl.BlockSpec((HEAD, width), lambda i: (i, z_block)), pl.BlockSpec((HEAD, width), lambda i: (i, z_block + 1)),
                  pl.BlockSpec((1, width), lambda i: (0, 0)), pl.BlockSpec((groups, HEAD, HEAD), lambda i: (0, 0, 0)),
                  pl.BlockSpec((HEAD, groups), lambda i: (0, 0))],
        out_specs=pl.BlockSpec((HEAD, width), lambda i: (i, 0)),
        out_shape=jax.ShapeDtypeStruct((t, width), BF16), compiler_params=_params(1),
    )(proj, proj, g_v, w_s, bst)


def _sgu_bwd(proj, dout_b, dproj, g_v, w_s, w_st, bst, width, z_block):
    t = proj.shape[0]
    groups = width // HEAD
    nblk = t // HEAD

    def body(u_ref, v_ref, do_ref, g_ref, ws_ref, wst_ref, bst_ref, dproj_hbm,
             dz_ref, dg_ref, dws_ref, dbst_ref, res_s):
        i, p = pl.program_id(0), pl.program_id(1)

        @pl.when((i == 0) & (p == 0))
        def _():
            dg_ref[...] = jnp.zeros_like(dg_ref)
            dws_ref[...] = jnp.zeros_like(dws_ref)
            dbst_ref[...] = jnp.zeros_like(dbst_ref)

        @pl.when(p == 0)
        def _():
            g_v = g_ref[...]
            u, du, dv, vhat, rstd, vn, vm = _sgu_core(u_ref[...], v_ref[...], g_v, ws_ref, bst_ref[...])
            dout = do_ref[...].astype(F32)
            res_s[0] = (dout * vm * du).astype(BF16)
            dvm = dout * u
            dvn_cols = []
            for g in range(groups):
                sl = slice(g * HEAD, (g + 1) * HEAD)
                dvm_g = dvm[:, sl]
                dbst_ref[:, g:g + 1] += jnp.sum(dvm_g, axis=1, keepdims=True)
                dws_ref[g] += _bdot(dvm_g, vn[:, sl], _NT)
                dvn_cols.append(_bdot(wst_ref[g], dvm_g))
            dvn = jnp.concatenate(dvn_cols, axis=1)
            dg_ref[...] += _colsum(dvn * vhat)
            dvh = dvn * g_v
            dvg = rstd * (dvh - jnp.mean(dvh, axis=-1, keepdims=True)
                          - vhat * jnp.mean(dvh * vhat, axis=-1, keepdims=True))
            res_s[1] = (dvg * dv).astype(BF16)

        dz_ref[...] = res_s[p]

    n_in = dproj.shape[1]
    return pl.pallas_call(
        body, name="sgu_bwd", grid=(nblk, 2),
        in_specs=[pl.BlockSpec((HEAD, width), lambda i, p: (i, z_block)),
                  pl.BlockSpec((HEAD, width), lambda i, p: (i, z_block + 1)),
                  pl.BlockSpec((HEAD, width), lambda i, p: (i, 0)),
                  pl.BlockSpec((1, width), lambda i, p: (0, 0)),
                  pl.BlockSpec((groups, HEAD, HEAD), lambda i, p: (0, 0, 0)),
                  pl.BlockSpec((groups, HEAD, HEAD), lambda i, p: (0, 0, 0)),
                  pl.BlockSpec((HEAD, groups), lambda i, p: (0, 0)),
                  pl.BlockSpec(memory_space=pl.ANY)],
        out_specs=[pl.BlockSpec((HEAD, width), lambda i, p: (i, z_block + p)),
                   pl.BlockSpec((1, width), lambda i, p: (0, 0)),
                   pl.BlockSpec((groups, HEAD, HEAD), lambda i, p: (0, 0, 0)),
                   pl.BlockSpec((HEAD, groups), lambda i, p: (0, 0))],
        out_shape=[jax.ShapeDtypeStruct((t, n_in), BF16), jax.ShapeDtypeStruct((1, width), F32),
                   jax.ShapeDtypeStruct((groups, HEAD, HEAD), F32), jax.ShapeDtypeStruct((HEAD, groups), F32)],
        scratch_shapes=[pltpu.VMEM((2, HEAD, width), BF16)],
        input_output_aliases={7: 0},
        compiler_params=_params(2),
    )(proj, proj, dout_b, g_v, w_s, w_st, bst, dproj)


def _hgrn_bwd(proj, osum, dout_a, dproj, lb, g_norm, width, after):
    t = proj.shape[0]
    heads = width // HEAD
    nb = t // HEAD
    cpb = HEAD // A_CHUNK
    ubk = 4 if nb % 4 == 0 else (2 if nb % 2 == 0 else 1)
    q_scale = HEAD ** -0.5

    def body(q_ref, ffw_ref, fbw_ref, v_ref, og_ref, osum_ref, douta_ref, lb_ref, g_ref, dproj_hbm, after_hbm,
             out_ref, dgh_ref, dlb_ref, do_s, dq_s, dv_s, res_s, ck_s):
        p = pl.program_id(1)
        f_refs = (ffw_ref, fbw_ref)

        @pl.when(p == 0)
        def _():
            tril, triu = _chunk_masks()
            cum2 = jnp.concatenate([_ones(tril), _ones(triu)], axis=0)
            g_row = g_ref[...]

            def pass_norm(i, dgh):
                rows = pl.ds(pl.multiple_of(i * HEAD, HEAD), HEAD)
                o = osum_ref[rows, :]
                r = _rms(o)
                oh = o * r
                og = og_ref[rows, :]
                sg = _sigmoid(og)
                dout = douta_ref[rows, :].astype(F32)
                don = dout * (og * sg)
                res_s[4, rows, :] = (dout * (oh * g_row) * (sg * (1.0 + og * (1.0 - sg)))).astype(BF16)
                doh = don * g_row
                do_s[rows, :] = r * (doh - oh * jnp.mean(doh * oh, axis=-1, keepdims=True))
                return dgh + _colsum(don * oh)

            dgh_ref[...] = lax.fori_loop(0, nb, pass_norm, jnp.zeros((1, HEAD), F32))

            lbs = (lb_ref[0:1, :], lb_ref[1:2, :])
            zero_state = jnp.zeros((HEAD, HEAD), F32)

            def chunk_order(d):
                return list(range(cpb)) if d == 0 else list(range(cpb - 1, -1, -1))

            def chunk(x, j):
                return x[j * A_CHUNK:(j + 1) * A_CHUNK, :]

            def decay_row(e_big, j):
                return e_big[j * A_CHUNK:j * A_CHUNK + 1, :]

            def cat(parts):
                return jnp.concatenate([parts[j] for j in range(cpb)], axis=0)

            def block_states(d, start, incs, e_big):
                befores, st = {}, start
                for j in chunk_order(d):
                    befores[j] = st
                    st = st * decay_row(e_big, j) + incs[j]
                return befores, st

            def pass_states(it, states):
                loaded = []
                for u in range(ubk):
                    for d in range(2):
                        blk = it * ubk + u if d == 0 else nb - 1 - (it * ubk + u)
                        rows = pl.ds(pl.multiple_of(blk * HEAD, HEAD), HEAD)
                        loaded.append((d, blk, f_refs[d][rows, :], v_ref[rows, :]))
                blocks = [_hgrn_block(d, fv, lbs[d], cum2) for d, _, fv, _ in loaded]
                incs = [{j: _bdot(chunk(vv, j), chunk(k * erest, j), _TN) for j in range(cpb)}
                        for (_, _, _, vv), (k, _, _, _, _, erest) in zip(loaded, blocks)]
                states, starts = list(states), []
                for (d, _, _, _), (_, _, _, eb, _, erest), inc in zip(loaded, blocks, incs):
                    starts.append(states[d])
                    states[d] = block_states(d, states[d], inc, eb * erest)[1]
                for (d, blk, _, _), start in zip(loaded, starts):
                    ck_s[d, blk] = start
                return tuple(states)

            lax.fori_loop(0, nb // ubk, pass_states, (zero_state, zero_state))

            def pass_back(it, carry):
                gts, dlb = [carry[0], carry[1]], carry[2]
                loaded = []
                for u, d in ((u, d) for u in range(ubk) for d in range(2)):
                    blk = nb - 1 - (it * ubk + u) if d == 0 else it * ubk + u
                    rows = pl.ds(pl.multiple_of(blk * HEAD, HEAD), HEAD)
                    loaded.append((d, rows, f_refs[d][rows, :], q_ref[rows, :], v_ref[rows, :], do_s[rows, :], ck_s[d, blk]))
                blocks = [_hgrn_block(d, fv, lbs[d], cum2) for d, _, fv, _, _, _, _ in loaded]
                scaled = []
                for (_, _, _, qv, _, _, _), (k, _, _, eb, enb, erest) in zip(loaded, blocks):
                    qh = qv * q_scale
                    scaled.append((qh, qh * eb, k * enb, k * erest, eb * erest))
                masks = [tril if d == 0 else triu for d, *_ in loaded]
                atts = [jnp.where(m, _bdot(qd, kd, _NT), 0.0) for m, (_, qd, kd, _, _) in zip(masks, scaled)]
                datts = [jnp.where(m, _bdot(do, vv, _NT), 0.0) for m, (_, _, _, _, vv, do, _) in zip(masks, loaded)]
                dvs = [_bdot(att, do, _TN) for att, (_, _, _, _, _, do, _) in zip(atts, loaded)]
                dqds = [_bdot(datt, kd) for datt, (_, _, kd, _, _) in zip(datts, scaled)]
                dkds = [_bdot(datt, qd, _TN) for datt, (_, qd, _, _, _) in zip(datts, scaled)]
                s_incs = [{j: _bdot(chunk(vv, j), chunk(ke, j), _TN) for j in range(cpb)}
                          for (_, _, _, _, vv, _, _), (_, _, _, ke, _) in zip(loaded, scaled)]
                g_incs = [{j: _bdot(chunk(do, j), chunk(qd, j), _TN) for j in range(cpb)}
                          for (_, _, _, _, _, do, _), (_, qd, _, _, _) in zip(loaded, scaled)]
                befores, afters, g_at = [], [], []
                for (d, _, _, _, _, _, ck), (_, _, _, _, e_big), s_inc, g_inc in zip(loaded, scaled, s_incs, g_incs):
                    order = chunk_order(d)
                    before, after = block_states(d, ck, s_inc, e_big)
                    befores.append(before)
                    afters.append({j: (before[order[n + 1]] if n + 1 < cpb else after) for n, j in enumerate(order)})
                    at, gt = {}, gts[d]
                    for j in reversed(order):
                        at[j] = gt
                        gt = gt * decay_row(e_big, j) + g_inc[j]
                    gts[d] = gt
                    g_at.append(at)
                dqd_i = [{j: _bdot(chunk(do, j), before[j]) for j in range(cpb)}
                         for (_, _, _, _, _, do, _), before in zip(loaded, befores)]
                dv_i = [{j: _bdot(chunk(ke, j), at[j], _NT) for j in range(cpb)}
                        for (_, _, _, ke, _), at in zip(scaled, g_at)]
                dke = [{j: _bdot(chunk(vv, j), at[j]) for j in range(cpb)}
                       for (_, _, _, _, vv, _, _), at in zip(loaded, g_at)]
                results, new = [], []
                for n, ((d, rows, _, _, _, _, _), (k, sf, big_f, eb, enb, erest), (qh, _, _, _, _)) in enumerate(
                        zip(loaded, blocks, scaled)):
                    dqh = (dqds[n] + cat(dqd_i[n])) * eb
                    dk = dkds[n] * enb + cat(dke[n]) * erest
                    carry_rows = {j: jnp.broadcast_to(_colsum(g_at[n][j] * afters[n][j]), (A_CHUNK, HEAD))
                                  for j in range(cpb)}
                    dlf = _dot_split(_ones(triu if d == 0 else tril), qh * dqh - k * dk) + cat(carry_rows)
                    common = dlf / big_f - dk
                    results.append((d, rows, (k * sf * common).astype(BF16), dqh.astype(BF16),
                                    (dvs[n] + cat(dv_i[n])).astype(BF16)))
                    new.append(_colsum((1.0 - sf) * common))
                for d, rows, df16, dq16, dv16 in results:
                    res_s[1 + d, rows, :] = df16
                    dq_s[d, rows, :] = dq16
                    dv_s[d, rows, :] = dv16
                per_dir = [sum(c for (d, *_), c in zip(loaded, new) if d == dd) for dd in range(2)]
                return gts[0], gts[1], dlb + jnp.concatenate(per_dir, axis=0)

            dlb_ref[...] = lax.fori_loop(0, nb // ubk, pass_back,
                                         (zero_state, zero_state, jnp.zeros((2, HEAD), F32)))[2]

            def pass_out(i, carry):
                rows = pl.ds(pl.multiple_of(i * HEAD, HEAD), HEAD)
                dq = dq_s[0, rows, :].astype(F32) + dq_s[1, rows, :].astype(F32)
                res_s[0, rows, :] = (dq * q_scale).astype(BF16)
                res_s[3, rows, :] = (dv_s[0, rows, :].astype(F32) + dv_s[1, rows, :].astype(F32)).astype(BF16)
                return carry

            lax.fori_loop(0, nb, pass_out, 0)

        out_ref[...] = res_s[p]

    def col(pp):
        return pl.BlockSpec((t, HEAD), lambda h, p: (0, pp * heads + h))

    n_in = dproj.shape[1]
    any_spec = pl.BlockSpec(memory_space=pl.ANY)
    return pl.pallas_call(
        body, name="hgrn_bwd", grid=(heads, 5),
        in_specs=[col(0), col(1), col(2), col(3), col(4),
                  pl.BlockSpec((t, HEAD), lambda h, p: (0, h)), pl.BlockSpec((t, HEAD), lambda h, p: (0, h)),
                  pl.BlockSpec((2, HEAD), lambda h, p: (0, h)), pl.BlockSpec((1, HEAD), lambda h, p: (0, 0)),
                  any_spec, any_spec],
        out_specs=[pl.BlockSpec((t, HEAD), lambda h, p: (0, p * heads + h)),
                   pl.BlockSpec((None, 1, HEAD), lambda h, p: (h, 0, 0)),
                   pl.BlockSpec((2, HEAD), lambda h, p: (0, h))],
        out_shape=[jax.ShapeDtypeStruct((t, n_in), BF16), jax.ShapeDtypeStruct((heads, 1, HEAD), F32),
                   jax.ShapeDtypeStruct((2, width), F32)],
        scratch_shapes=[pltpu.VMEM((t, HEAD), F32), pltpu.VMEM((2, t, HEAD), BF16), pltpu.VMEM((2, t, HEAD), BF16),
                        pltpu.VMEM((5, t, HEAD), BF16), pltpu.VMEM((2, nb, HEAD, HEAD), F32)],
        input_output_aliases={9: 0},
        compiler_params=_params(2),
    )(proj, proj, proj, proj, proj, osum, dout_a, lb, g_norm, dproj, after)


def _adamw(w, g, m, v):
    m = ADAM_B1 * m + (1.0 - ADAM_B1) * g
    v = ADAM_B2 * v + (1.0 - ADAM_B2) * (g * g)
    m_hat = m / (1.0 - ADAM_B1 ** ADAM_STEP)
    v_hat = v / (1.0 - ADAM_B2 ** ADAM_STEP)
    delta = -ADAM_LR * (m_hat / (jnp.sqrt(v_hat) + ADAM_EPS) + ADAM_WD * w)
    return delta, m, v


def _adamw_big(name, me, w, m, v, parts, axis):
    r, c = w.shape
    n_parts = len(parts)
    tr = _tile(r // n_parts, 128)
    per = r // n_parts // tr
    assert axis == 1 or n_parts == 1

    def body(me_ref, w_ref, m_ref, v_ref, *rest):
        g_refs, l_refs = rest[:n_parts], rest[n_parts:2 * n_parts]
        og_ref, od_ref, om_ref, ov_ref = rest[2 * n_parts:]
        g = None
        for p in range(n_parts):
            total = g_refs[p][...]
            for s in range(N_DEV - 1):
                total = total + l_refs[p][s].astype(F32)
            g = total if p == 0 else jnp.where(pl.program_id(0) // per == p, total, g)
        og_ref[...] = g
        od_ref[...], om_ref[...], ov_ref[...] = _adamw(w_ref[...], g, m_ref[...], v_ref[...])

    def within(p, i):
        return jnp.clip(i - p * per, 0, per - 1)

    shard = pl.BlockSpec((tr, c), lambda i, me_ref: (i, 0))
    if axis == 1:
        own = [pl.BlockSpec((tr, c), lambda i, me_ref, p=p: (within(p, i), me_ref[0])) for p in range(n_parts)]
    else:
        own = [pl.BlockSpec((tr, c), lambda i, me_ref: (me_ref[0] * (r // tr) + i, 0))]
    landed = [pl.BlockSpec((N_DEV - 1, tr, c), lambda i, me_ref, p=p: (0, within(p, i), 0)) for p in range(n_parts)]
    grid_spec = pltpu.PrefetchScalarGridSpec(
        num_scalar_prefetch=1, grid=(r // tr,),
        in_specs=[shard, shard, shard] + own + landed, out_specs=[shard] * 4)
    return pl.pallas_call(
        body, name=name, grid_spec=grid_spec, out_shape=[jax.ShapeDtypeStruct((r, c), F32)] * 4,
        compiler_params=_params(1),
    )(me, w, m, v, *[g for g, _ in parts], *[ld for _, ld in parts])


def _adamw_ada(sct, dmod_mine, w, m, v):
    d, n = w.shape
    tr = _tile(d, 256)

    def body(s_ref, dm_ref, w_ref, m_ref, v_ref, og_ref, od_ref, om_ref, ov_ref):
        g = _dot(s_ref[...], dm_ref[...], precision=HIGHEST)
        og_ref[...] = g
        od_ref[...], om_ref[...], ov_ref[...] = _adamw(w_ref[...], g, m_ref[...], v_ref[...])

    blk = pl.BlockSpec((tr, n), lambda i: (i, 0))
    return pl.pallas_call(
        body, name="adamw_ada", grid=(d // tr,),
        in_specs=[pl.BlockSpec((tr, N_DEV), lambda i: (i, 0)), pl.BlockSpec((N_DEV, n), lambda i: (0, 0)), blk, blk, blk],
        out_specs=[blk] * 4, out_shape=[jax.ShapeDtypeStruct((d, n), F32)] * 4, compiler_params=_params(1),
    )(sct, dmod_mine, w, m, v)


def _adamw_small(gathered, w, m, v):
    def body(g_ref, w_ref, m_ref, v_ref, og_ref, od_ref, om_ref, ov_ref):
        g = g_ref[0]
        for s in range(1, N_DEV):
            g = g + g_ref[s]
        og_ref[...] = g
        od_ref[...], om_ref[...], ov_ref[...] = _adamw(w_ref[...], g, m_ref[...], v_ref[...])

    return pl.pallas_call(
        body, name="adamw_small", out_shape=[jax.ShapeDtypeStruct(w.shape, F32)] * 4,
        compiler_params=pltpu.CompilerParams(vmem_limit_bytes=VMEM_LIMIT),
    )(gathered, w, m, v)


def _adamw_lb(dlb_mine, lb_logits, m, v):
    def body(d_ref, l_ref, m_ref, v_ref, og_ref, od_ref, om_ref, ov_ref):
        dlb = d_ref[0]
        for s in range(1, N_DEV):
            dlb = dlb + d_ref[s]
        for dr in range(2):
            lb = _sigmoid(l_ref[dr][0:1, :] - l_ref[dr][1:2, :])
            d0 = dlb[dr:dr + 1] * lb * (1.0 - lb)
            g = jnp.concatenate([d0, -d0], axis=0)
            og_ref[dr] = g
            od_ref[dr], om_ref[dr], ov_ref[dr] = _adamw(l_ref[dr], g, m_ref[dr], v_ref[dr])

    return pl.pallas_call(body, name="adamw_lb", out_shape=[jax.ShapeDtypeStruct(lb_logits.shape, F32)] * 4,
                          )(dlb_mine, lb_logits, m, v)


def _rows(a, pad_to=8):
    flat = a.reshape(-1, LANE)
    pad = (-flat.shape[0]) % pad_to
    return jnp.pad(flat, ((0, pad), (0, 0))) if pad else flat


def kernel(x, c, w_ada, b_ada, g_pre_mix, g_post_mix, g_pre_ffn, g_post_ffn, w_in, lb_logits, g_hgrn_norm, w_a_out, g_sgu_norm, w_spatial, b_spatial, w_b_out, w_o, w_ff1, w_ff2, loss_target, m_w_ada, m_b_ada, m_g_pre_mix, m_g_post_mix, m_g_pre_ffn, m_g_post_ffn, m_w_in, m_lb_logits, m_g_hgrn_norm, m_w_a_out, m_g_sgu_norm, m_w_spatial, m_b_spatial, m_w_b_out, m_w_o, m_w_ff1, m_w_ff2, v_w_ada, v_b_ada, v_g_pre_mix, v_g_post_mix, v_g_pre_ffn, v_g_post_ffn, v_w_in, v_lb_logits, v_g_hgrn_norm, v_w_a_out, v_g_sgu_norm, v_w_spatial, v_b_spatial, v_w_b_out, v_w_o, v_w_ff1, v_w_ff2):
    t, d = x.shape[1], x.shape[2]
    n_in = w_in.shape[2] * N_DEV
    width = (n_in - 2 * d) // 7
    heads = width // HEAD
    assert heads == N_DEV and width % LANE == 0
    d_ff = w_ff1.shape[2] * N_DEV
    n_ada = w_ada.shape[2]
    me = _dev_index()
    me_arr = me.reshape(1).astype(jnp.int32)
    x2, tgt = x[0], loss_target[0]

    big = [w_in[0], w_a_out[0], w_b_out[0], w_o[0], w_ff1[0], w_ff2[0]]
    big_axes = [1, 1, 1, 0, 1, 0]
    big_names = ["w_in", "w_a_out", "w_b_out", "w_o", "w_ff1", "w_ff2"]
    w_in16 = _cast_bf16("cast_w_in", big[0])
    own_parts = [_cast_into_full("cast_" + nm, me_arr, w, ax) for nm, w, ax in zip(big_names[1:], big[1:], big_axes[1:])]

    c_rows = d // LANE
    small = _all_gather_small("gather_c_lb", _prep_small(c[0:1], lb_logits))
    sc_all = small[:, :c_rows, :].reshape(N_DEV, d)
    lb = jnp.transpose(small[:, c_rows:c_rows + 2, :], (1, 0, 2)).reshape(2, width)
    b_shard = lax.dynamic_slice_in_dim(b_ada, me * n_ada, n_ada, axis=1)
    mod_sh = _mod_shard(sc_all, w_ada[0], b_shard)
    mod_all = _all_gather_small("gather_mod", _rows(mod_sh))
    mod_all = mod_all[:, :N_DEV * n_ada // LANE, :].reshape(N_DEV, N_DEV, n_ada)
    mod6 = lax.dynamic_index_in_dim(mod_all, me, axis=1, keepdims=False).reshape(N_MOD, d)
    sh1, sc1, gt1, sh2, sc2, gt2 = [mod6[i:i + 1] for i in range(N_MOD)]

    a1 = _norm_mod(x2, g_pre_mix, sh1, sc1)
    tm = _tile(t, 512)

    def store_bf16(acc, i, j, extra_refs, out_refs, rows):
        out_refs[0][...] = acc.astype(BF16)

    xq, yq, cq = lax.axis_index("x"), lax.axis_index("y"), lax.axis_index("c")
    chips = [(1 - xq, yq), (xq, 1 - yq), (1 - xq, 1 - yq)]
    order = jnp.stack([me, 4 * xq + 2 * yq + 1 - cq]
                      + [4 * a + 2 * b + cq for a, b in chips[:2]] + [4 * a + 2 * b + 1 - cq for a, b in chips[:2]]
                      + [4 * chips[2][0] + 2 * chips[2][1] + cq, 4 * chips[2][0] + 2 * chips[2][1] + 1 - cq]).astype(jnp.int32)
    proj, wf_in = _proj_gather(a1, w_in16, order)

    proj, own_parts = lax.optimization_barrier((proj, own_parts))
    gathers = {}
    for key, lo, hi in (("mid", 1, 4), ("ff1", 4, 5), ("ff2", 5, 6)):
        far, near = _gather_stage_plans(own_parts[lo - 1:hi - 1], big_axes[lo:hi])
        gathers[key] = [far, near, _split_start("gather_%s_start" % key, far, landing=own_parts[lo - 1:hi - 1])]

    def pass_on(key, after):
        far, near, (sems, thru, _) = gathers[key]
        parts = _split_wait("gather_%s_wait" % key, far, sems, thru, after)[1]
        gathers[key].append(_split_start("pass_%s_start" % key, near, landing=list(parts)))
        return gathers[key][3][2]

    def gathered_weights(key, after):
        near, (sems, thru, _) = gathers[key][1], gathers[key][3]
        return _split_wait("pass_%s_wait" % key, near, sems, thru, after)[1]

    out_a, osum = _hgrn_fwd(proj, lb, g_hgrn_norm, width,
                            after=[gathers[key][2][2] for key in ("mid", "ff1", "ff2")])
    passed_mid = pass_on("mid", out_a)
    z_block = 5
    bst = b_spatial[0].T
    out_b = _sgu_fwd(proj, g_sgu_norm, w_spatial[0], bst, width, z_block)
    wf_a, wf_b, wf_o = gathered_weights("mid", out_b)

    tn_d = _tile(d, 1024)
    blk_d = ((tm, tn_d), lambda i, j: (i, j))
    ga_blk = (5 * width + 2 * width) // tn_d
    gb_blk = ga_blk + d // tn_d

    def merge(acc, i, j, extra_refs, out_refs, rows):
        ga, gb, oa, wa = extra_refs[:4]
        ya = _dot(oa[...], wa[...])
        out_refs[0][...] = ya.astype(BF16)
        out_refs[1][...] = acc.astype(BF16)
        out_refs[2][...] = (_sigmoid(ga[...]) * ya + _sigmoid(gb[...]) * acc).astype(BF16)

    y_a, y_b, merged = _mm(
        "y_ab_merge", out_b, wf_b, _NN, t, d, width, tm, tn_d, width,
        [(proj, (tm, tn_d), lambda i, j: (i, ga_blk + j)), (proj, (tm, tn_d), lambda i, j: (i, gb_blk + j)),
         (out_a, (tm, width), lambda i, j: (i, 0)), (wf_a, (width, tn_d), lambda i, j: (0, j))] + _after(passed_mid),
        [(jax.ShapeDtypeStruct((t, d), BF16),) + blk_d] * 3, merge)

    tr = _tile(t, 512)
    rc = 32 if tr % 32 == 0 else None
    row_d = ((tr, d), lambda i, j: (i, 0))
    vec_d = ((1, d), lambda i, j: (0, 0))

    passed_ff1 = pass_on("ff1", merged)

    def post_mix(acc, i, j, extra_refs, out_refs, rows):
        x_r, gt1_r, g2_r, g3_r, sc2_r, sh2_r = extra_refs[:6]
        h1 = x_r[rows, :] + gt1_r[...] * (acc * _rms(acc) * g2_r[...])
        out_refs[0][rows, :] = acc.astype(BF16)
        out_refs[1][rows, :] = h1
        out_refs[2][rows, :] = ((h1 * _rms(h1) * g3_r[...]) * (1.0 + sc2_r[...]) + sh2_r[...]).astype(BF16)

    mo, h1, a2 = _mm("w_o_post_mix", merged, wf_o, _NN, t, d, d, tr, d, d,
                     [(x2,) + row_d, (gt1,) + vec_d, (g_post_mix,) + vec_d, (g_pre_ffn,) + vec_d, (sc2,) + vec_d, (sh2,) + vec_d]
                     + _after(passed_ff1),
                     [(jax.ShapeDtypeStruct((t, d), BF16),) + row_d, (jax.ShapeDtypeStruct((t, d), F32),) + row_d,
                      (jax.ShapeDtypeStruct((t, d), BF16),) + row_d], post_mix, row_chunk=rc)

    tn_f = _tile(d_ff, 2048)
    blk_f = ((tm, tn_f), lambda i, j: (i, j))

    def relu_sq(acc, i, j, extra_refs, out_refs, rows):
        r = jnp.maximum(acc, 0.0)
        out_refs[0][...] = acc.astype(BF16)
        out_refs[1][...] = (r * r).astype(BF16)

    wf_1, = gathered_weights("ff1", a2)
    hff, act = _mm(
        "ff1", a2, wf_1, _NN, t, d_ff, d, tm, tn_f, d, [],
        [(jax.ShapeDtypeStruct((t, d_ff), BF16),) + blk_f, (jax.ShapeDtypeStruct((t, d_ff), BF16),) + blk_f], relu_sq)
    pass_on("ff2", hff)
    wf_2, = gathered_weights("ff2", act)

    sums_d = ((8, d), lambda i, j: (0, 0))

    def zero_first(sums_r, i, rows):
        if rows.start in (None, 0):
            @pl.when(i == 0)
            def _():
                sums_r[...] = jnp.zeros_like(sums_r)

    def loss_head(acc, i, j, extra_refs, out_refs, rows):
        h1_r, tgt_r, gt2_r, g4_r = extra_refs
        dy_r, dff_r, sums_r = out_refs
        r4 = _rms(acc)
        ffn = acc * r4
        n4 = ffn * g4_r[...]
        err = h1_r[rows, :] + gt2_r[...] * n4 - tgt_r[rows, :]
        dy = err * (1.0 / d)
        dy_r[rows, :] = dy.astype(BF16)
        dn4 = dy * gt2_r[...]
        dffn = dn4 * g4_r[...]
        dff_r[rows, :] = (r4 * (dffn - ffn * jnp.mean(dffn * ffn, axis=-1, keepdims=True))).astype(BF16)
        zero_first(sums_r, i, rows)

        sums_r[0:1, :] += _colsum(err * err)
        sums_r[1:2, :] += _colsum(dy * n4)
        sums_r[2:3, :] += _colsum(dn4 * ffn)

    tk_f = _tile(d_ff, 1024)
    dy, dff, sums_f = _mm("ff2_loss", act, wf_2, _NN, t, d, d_ff, tr, d, _tile(d_ff, 2048),
                          [(h1,) + row_d, (tgt,) + row_d, (gt2,) + vec_d, (g_post_ffn,) + vec_d],
                          [(jax.ShapeDtypeStruct((t, d), BF16),) + row_d, (jax.ShapeDtypeStruct((t, d), BF16),) + row_d,
                           (jax.ShapeDtypeStruct((8, d), F32),) + sums_d], loss_head, row_chunk=rc)
    loss_mine = (0.5 / d) * jnp.sum(sums_f[0])

    def relu_sq_bwd(acc, i, j, extra_refs, out_refs, rows):
        out_refs[0][...] = (acc * (2.0 * jnp.maximum(extra_refs[0][...].astype(F32), 0.0))).astype(BF16)

    dhff, = _mm("d_hff", dff, wf_2, _NT, t, d_ff, d, tm, tn_f, d, [(hff,) + blk_f],
                [(jax.ShapeDtypeStruct((t, d_ff), BF16),) + blk_f], relu_sq_bwd)
    scatters = {}

    def send_grads(key, grads16, axes):
        plan = _scatter_plan(grads16, axes)
        scatters[key] = (plan,) + _split_start("scatter_%s_start" % key, plan)
        return scatters[key][3]

    def received_grads(key, after):
        plan, sems, thru, _ = scatters[key]
        return _split_wait("scatter_%s_wait" % key, plan, sems, thru, after)[1]

    gw_ff2, gw_ff2_16 = _grad_w("grad_w_ff2", act, dff)
    sent_ff2 = send_grads("ff2", [gw_ff2_16], big_axes[5:6])
    gw_ff1, gw_ff1_16 = _grad_w("grad_w_ff1", a2, dhff, token=sent_ff2)
    sent_ff1 = send_grads("ff1", [gw_ff1_16], big_axes[4:5])

    def pre_ffn_bwd(acc, i, j, extra_refs, out_refs, rows):
        h1_r, dy_r, mo_r, sc2_r, g3_r, gt1_r, g2_r = extra_refs[:7]
        dh1_r, dmo_r, sums_r = out_refs
        h1v = h1_r[rows, :]
        r3 = _rms(h1v)
        h1n = h1v * r3
        dn3 = acc * (1.0 + sc2_r[...])
        dh1n = dn3 * g3_r[...]
        dh1 = dy_r[rows, :].astype(F32) + r3 * (dh1n - h1n * jnp.mean(dh1n * h1n, axis=-1, keepdims=True))
        dh1_r[rows, :] = dh1.astype(BF16)
        mov = mo_r[rows, :].astype(F32)
        r2 = _rms(mov)
        mon = mov * r2
        dn2 = dh1 * gt1_r[...]
        dmon = dn2 * g2_r[...]
        dmo_r[rows, :] = (r2 * (dmon - mon * jnp.mean(dmon * mon, axis=-1, keepdims=True))).astype(BF16)
        zero_first(sums_r, i, rows)

        sums_r[0:1, :] += _colsum(acc)
        sums_r[1:2, :] += _colsum(acc * (h1n * g3_r[...]))
        sums_r[2:3, :] += _colsum(dn3 * h1n)
        sums_r[3:4, :] += _colsum(dh1 * (mon * g2_r[...]))
        sums_r[4:5, :] += _colsum(dn2 * mon)

    dh1, dmo, sums_m = _mm("d_a2_pre_ffn", dhff, wf_1, _NT, t, d, d_ff, tr, d, tk_f,
                           [(h1,) + row_d, (dy,) + row_d, (mo,) + row_d, (sc2,) + vec_d, (g_pre_ffn,) + vec_d,
                            (gt1,) + vec_d, (g_post_mix,) + vec_d] + _after(sent_ff1),
                           [(jax.ShapeDtypeStruct((t, d), BF16),) + row_d, (jax.ShapeDtypeStruct((t, d), BF16),) + row_d,
                            (jax.ShapeDtypeStruct((8, d), F32),) + sums_d], pre_ffn_bwd, row_chunk=rc)
    gw_o, gw_o_16 = _grad_w("grad_w_o", merged, dmo)

    n_j = d // tn_d

    def merge_bwd_body(dmo_ref, wo_ref, ga_ref, gb_ref, ya_ref, yb_ref, dya_ref, dyb_ref, dproj_ref, acc_s):
        g = pl.program_id(2)

        @pl.when(g == 0)
        def _():
            dm = _dot(dmo_ref[...], wo_ref[...], _NT)
            acc_s[...] = dm
            sa = _sigmoid(ga_ref[...])
            dya_ref[...] = (dm * sa).astype(BF16)
            dproj_ref[...] = (dm * ya_ref[...].astype(F32) * sa * (1.0 - sa)).astype(BF16)

        @pl.when(g == 1)
        def _():
            dm = acc_s[...]
            sb = _sigmoid(gb_ref[...])
            dyb_ref[...] = (dm * sb).astype(BF16)
            dproj_ref[...] = (dm * yb_ref[...].astype(F32) * sb * (1.0 - sb)).astype(BF16)

    tile3 = pl.BlockSpec((tm, tn_d), lambda i, j, g: (i, j))
    dy_a, dy_b, dproj = pl.pallas_call(
        merge_bwd_body, name="d_merged", grid=(t // tm, n_j, 2),
        in_specs=[pl.BlockSpec((tm, d), lambda i, j, g: (i, 0)), pl.BlockSpec((tn_d, d), lambda i, j, g: (j, 0)),
                  pl.BlockSpec((tm, tn_d), lambda i, j, g: (i, ga_blk + j)),
                  pl.BlockSpec((tm, tn_d), lambda i, j, g: (i, gb_blk + j)), tile3, tile3],
        out_specs=[tile3, tile3, pl.BlockSpec((tm, tn_d), lambda i, j, g: (i, ga_blk + g * n_j + j))],
        out_shape=[jax.ShapeDtypeStruct((t, d), BF16), jax.ShapeDtypeStruct((t, d), BF16),
                   jax.ShapeDtypeStruct((t, n_in), BF16)],
        scratch_shapes=[pltpu.VMEM((tm, tn_d), F32)], compiler_params=_params(3),
    )(dmo, wf_o, proj, proj, y_a, y_b)

    tn_w = _tile(width, 1024)
    blk_w = ((tm, tn_w), lambda i, j: (i, j))
    dout_a, = _mm("d_out_a", dy_a, wf_a, _NT, t, width, d, tm, tn_w, d, [],
                  [(jax.ShapeDtypeStruct((t, width), BF16),) + blk_w], store_bf16)
    dout_b, = _mm("d_out_b", dy_b, wf_b, _NT, t, width, d, tm, tn_w, d, [],
                  [(jax.ShapeDtypeStruct((t, width), BF16),) + blk_w], store_bf16)
    gw_a, gw_a_16 = _grad_w("grad_w_a_out", out_a, dy_a)
    gw_b, gw_b_16 = _grad_w("grad_w_b_out", out_b, dy_b)

    w_st = jnp.swapaxes(w_spatial[0], 1, 2)
    dproj, dg_sgu, dw_sp, dbst = _sgu_bwd(proj, dout_b, dproj, g_sgu_norm, w_spatial[0], w_st, bst, width, z_block)
    sent_mid = send_grads("mid", [gw_a_16, gw_b_16, gw_o_16], big_axes[1:4])
    dproj, dgh_heads, dlb = _hgrn_bwd(proj, osum, dout_a, dproj, lb, g_hgrn_norm, width, after=sent_mid)
    gw_in_top, gw_in_top16 = _grad_w("grad_w_in_top", a1, dproj, rows=(0, d // 2))
    sent_top = send_grads("in_top", [gw_in_top16], big_axes[:1])
    gw_in_bot, gw_in_bot16 = _grad_w("grad_w_in_bot", a1, dproj, token=sent_top, rows=(d // 2, d // 2))
    sent_in = send_grads("in_bot", [gw_in_bot16], big_axes[:1])

    def pre_mix_bwd(acc, i, j, extra_refs, out_refs, rows):
        x_r, dh1_r, sc1_r, g1_r = extra_refs[:4]
        dx_r, sums_r = out_refs
        xv = x_r[rows, :]
        r1 = _rms(xv)
        xn = xv * r1
        dn1 = acc * (1.0 + sc1_r[...])
        dxn = dn1 * g1_r[...]
        dx_r[rows, :] = dh1_r[rows, :].astype(F32) + r1 * (dxn - xn * jnp.mean(dxn * xn, axis=-1, keepdims=True))
        zero_first(sums_r, i, rows)

        sums_r[0:1, :] += _colsum(acc)
        sums_r[1:2, :] += _colsum(acc * (xn * g1_r[...]))
        sums_r[2:3, :] += _colsum(dn1 * xn)

    tk_in = _tile(n_in, 2816)
    grad_x, sums_x = _mm(
        "d_a1_pre_mix", dproj, wf_in, _NT, t, d, n_in, tr, d, tk_in,
        [(x2,) + row_d, (dh1,) + row_d, (sc1,) + vec_d, (g_pre_mix,) + vec_d] + _after(sent_in),
        [(jax.ShapeDtypeStruct((t, d), F32),) + row_d, (jax.ShapeDtypeStruct((8, d), F32),) + sums_d],
        pre_mix_bwd, row_chunk=rc)

    dmod = jnp.concatenate([sums_x[0:2], sums_m[3:4], sums_m[0:2], sums_f[1:2]], axis=0).reshape(N_DEV, n_ada // LANE, LANE)
    ada_rows = -(-(n_ada // LANE) // 8) * 8
    dmod = jnp.pad(dmod, ((0, 0), (0, ada_rows - n_ada // LANE), (0, 0))).reshape(N_DEV * ada_rows, LANE)
    parts = [dmod, _rows(sums_x[2:3]), _rows(sums_m[4:5]), _rows(sums_m[2:3]), _rows(sums_f[2:3]),
             _rows(jnp.sum(dgh_heads, axis=0)), _rows(dg_sgu), _rows(dw_sp), _rows(dbst.T)]
    n_params = sum(p.shape[0] for p in parts)
    parts.append(jnp.full((8, LANE), loss_mine, F32))
    n_common = n_params + 8
    payload = jnp.concatenate(parts + [_rows(dlb)], axis=0)

    moms = [m_w_in, m_w_a_out, m_w_b_out, m_w_o, m_w_ff1, m_w_ff2]
    vars_ = [v_w_in, v_w_a_out, v_w_b_out, v_w_o, v_w_ff1, v_w_ff2]
    big_out = {}

    def big_update(nm, parts):
        k = big_names.index(nm)
        outs = _adamw_big("adamw_" + nm, me_arr, big[k], moms[k][0], vars_[k][0], parts, big_axes[k])
        big_out[nm] = [o[None] for o in outs]
        return outs[0]

    land_ff2, = received_grads("ff2", grad_x)
    done = big_update("w_ff2", [(gw_ff2, land_ff2)])
    land_ff1, = received_grads("ff1", done)
    done = big_update("w_ff1", [(gw_ff1, land_ff1)])
    land_a, land_b, land_o = received_grads("mid", done)
    big_update("w_a_out", [(gw_a, land_a)])
    big_update("w_b_out", [(gw_b, land_b)])
    done = big_update("w_o", [(gw_o, land_o)])

    payload, _ = lax.optimization_barrier((payload, done))
    gathered = _all_gather_small("gather_small_grads", payload)

    dmod_mine = lax.dynamic_slice_in_dim(gathered[:, :N_DEV * ada_rows, :].reshape(N_DEV, N_DEV, ada_rows * LANE),
                                         me, 1, axis=1)[:, 0, :n_ada]
    ada_out = [o[None] for o in _adamw_ada(sc_all.T, dmod_mine, w_ada[0], m_w_ada[0], v_w_ada[0])]

    def pack(b_, g1_, g2_, g3_, g4_, gh_, gs_, ws_, bs_):
        b3 = b_.reshape(N_DEV, n_ada // LANE, LANE)
        b3 = jnp.pad(b3, ((0, 0), (0, ada_rows - n_ada // LANE), (0, 0))).reshape(N_DEV * ada_rows, LANE)
        return jnp.concatenate([b3, _rows(g1_), _rows(g2_), _rows(g3_), _rows(g4_), _rows(gh_), _rows(gs_),
                                _rows(ws_), _rows(bs_), jnp.zeros((8, LANE), F32)], axis=0)

    small_w = (b_ada, g_pre_mix, g_post_mix, g_pre_ffn, g_post_ffn, g_hgrn_norm, g_sgu_norm, w_spatial, b_spatial)
    small_m = (m_b_ada, m_g_pre_mix, m_g_post_mix, m_g_pre_ffn, m_g_post_ffn, m_g_hgrn_norm, m_g_sgu_norm, m_w_spatial, m_b_spatial)
    small_v = (v_b_ada, v_g_pre_mix, v_g_post_mix, v_g_pre_ffn, v_g_post_ffn, v_g_hgrn_norm, v_g_sgu_norm, v_w_spatial, v_b_spatial)
    packed = _adamw_small(gathered[:, :n_common, :], pack(*small_w), pack(*small_m), pack(*small_v))

    def unpack(slab):
        outs, at = [], 0
        b3 = slab[:N_DEV * ada_rows].reshape(N_DEV, ada_rows, LANE)[:, :n_ada // LANE, :]
        outs.append(b3.reshape(b_ada.shape))
        at = N_DEV * ada_rows
        for ref in small_w[1:]:
            n_el = ref.size
            n_r = -(-(n_el // LANE) // 8) * 8
            outs.append(slab[at:at + n_el // LANE].reshape(ref.shape))
            at += n_r
        return outs

    small_out = [unpack(s) for s in packed]
    loss = packed[0][n_params, 0]

    dlb_all = gathered[:, n_common:n_common + 2 * heads, :].reshape(N_DEV, 2, heads, LANE)
    dlb_mine = lax.dynamic_index_in_dim(dlb_all, me, axis=2, keepdims=False)
    lb_out = _adamw_lb(dlb_mine, lb_logits, m_lb_logits, v_lb_logits)

    land_top, = received_grads("in_top", ada_out[0])
    land_bot, = received_grads("in_bot", land_top)
    big_update("w_in", [(gw_in_top, land_top), (gw_in_bot, land_bot)])

    order = ["w_ada", "b_ada", "g_pre_mix", "g_post_mix", "g_pre_ffn", "g_post_ffn", "w_in", "lb_logits", "g_hgrn_norm",
             "w_a_out", "g_sgu_norm", "w_spatial", "b_spatial", "w_b_out", "w_o", "w_ff1", "w_ff2"]
    small_names = ["b_ada", "g_pre_mix", "g_post_mix", "g_pre_ffn", "g_post_ffn", "g_hgrn_norm", "g_sgu_norm", "w_spatial", "b_spatial"]

    def leaf(kind, nm):
        if nm == "w_ada":
            return ada_out[kind]
        if nm == "lb_logits":
            return lb_out[kind]
        if nm in big_out:
            return big_out[nm][kind]
        return small_out[kind][small_names.index(nm)]

    result = [loss, grad_x[None]]
    for kind in range(4):
        result += [leaf(kind, nm) for nm in order]
    return tuple(result)
```
